```python
import jax, jax.numpy as jnp
from jax import lax
import numpy as np

D_MODEL = 1024
BATCH = 8
SEQ = 8192
DEPTH = 4

POOL_WINDOWS = (2, 4, 8, 16)
POOL_GROUPS = len(POOL_WINDOWS)
POOL_GROUP_DIM = D_MODEL // 8
POOL_DIM = POOL_GROUPS * POOL_GROUP_DIM
POOL_WMAX = max(POOL_WINDOWS)
HEAD_DIM = 64
N_Q_HEADS = D_MODEL // 128
N_KV_HEADS = 2
GQA_GROUP = N_Q_HEADS // N_KV_HEADS
ATTN_DIM = N_Q_HEADS * HEAD_DIM
KV_DIM = N_KV_HEADS * HEAD_DIM
WINDOW = 128
BLOCK = 128
ROPE_THETA = 500000.0
ROT_DIM = HEAD_DIM // 4
N_BRANCHES = 2
IN_DIM = POOL_DIM + ATTN_DIM + 2 * KV_DIM + N_BRANCHES * D_MODEL
D_FF = 2816
EPS = 1e-6

kernel_name = "hybrid_pool_swa_macaron"


def rmsnorm(x, g):
    xf = x.astype(jnp.float32)
    y = xf * lax.rsqrt(jnp.mean(xf * xf, axis=-1, keepdims=True) + EPS)
    return (y * g.astype(jnp.float32)).astype(x.dtype)


def swiglu(h, w_gu, w_down):
    g, u = jnp.split(h @ w_gu, 2, axis=-1)
    return (jax.nn.silu(g) * u) @ w_down


def pool_mixer(u, w_grp, scale):
    B, S, _ = u.shape
    uf = u.astype(jnp.float32)
    c = jnp.cumsum(uf, axis=1)
    c_pad = jnp.pad(c, ((0, 0), (POOL_WMAX, 0), (0, 0)))
    t = jnp.arange(S)
    outs = []
    for g, w in enumerate(POOL_WINDOWS):
        lo, hi = g * POOL_GROUP_DIM, (g + 1) * POOL_GROUP_DIM
        win_sum = c[:, :, lo:hi] - c_pad[:, POOL_WMAX - w:POOL_WMAX - w + S, lo:hi]
        count = jnp.minimum(t + 1, w).astype(jnp.float32)[None, :, None]
        outs.append(win_sum / count - uf[:, :, lo:hi])
    d = jnp.stack(outs, axis=2).astype(u.dtype)
    y = jnp.einsum('bsgc,gcd->bsgd', d, w_grp).reshape(B, S, POOL_DIM)
    return y * scale


def partial_rope(x, cos, sin):
    half = ROT_DIM // 2
    x1 = x[..., :half].astype(jnp.float32)
    x2 = x[..., half:ROT_DIM].astype(jnp.float32)
    c = cos[None, :, None, :]
    s = sin[None, :, None, :]
    rot = jnp.concatenate([x1 * c - x2 * s, x2 * c + x1 * s], axis=-1).astype(x.dtype)
    return jnp.concatenate([rot, x[..., ROT_DIM:]], axis=-1)


def swa_sink_attention(q, k, v, sinks):
    B, S = q.shape[0], q.shape[1]
    nb = S // BLOCK
    qb = q.reshape(B, nb, BLOCK, N_KV_HEADS, GQA_GROUP, HEAD_DIM)

    def with_prev(t):
        tb = t.reshape(B, nb, BLOCK, N_KV_HEADS, HEAD_DIM)
        prev = jnp.pad(tb[:, :-1], ((0, 0), (1, 0), (0, 0), (0, 0), (0, 0)))
        return jnp.concatenate([prev, tb], axis=2)

    kb, vb = with_prev(k), with_prev(v)
    s = jnp.einsum('bnqhgd,bnkhd->bnhgqk', qb, kb).astype(jnp.float32) * (HEAD_DIM ** -0.5)
    qi = jnp.arange(BLOCK)[:, None]
    ki = jnp.arange(2 * BLOCK)[None, :]
    diff = qi + BLOCK - ki
    band = (diff >= 0) & (diff < WINDOW)
    valid = (jnp.arange(nb)[:, None, None] > 0) | (ki[None] >= BLOCK)
    mask = band[None] & valid
    s = jnp.where(mask[None, :, None, None], s, -jnp.inf)
    sink = sinks.astype(jnp.float32).reshape(1, 1, N_KV_HEADS, GQA_GROUP, 1, 1)
    m = jnp.maximum(jnp.max(s, axis=-1, keepdims=True), sink)
    p = jnp.exp(s - m)
    denom = jnp.sum(p, axis=-1, keepdims=True) + jnp.exp(sink - m)
    p = (p / denom).astype(v.dtype)
    o = jnp.einsum('bnhgqk,bnkhd->bnqhgd', p, vb)
    return o.reshape(B, S, ATTN_DIM)


def _fwd_setup_inputs(seed: int = 0) -> dict:
    key = jax.random.key(seed)
    ks = jax.random.split(key, 17)
    f32 = jnp.float32

    def w(k, shape, fan_in):
        return jax.random.normal(k, shape, f32) * (fan_in ** -0.5)

    def gain(k, shape):
        return 1.0 + 0.05 * jax.random.normal(k, shape, f32)

    L = DEPTH
    return {
        "x": jax.random.normal(ks[0], (BATCH, SEQ, D_MODEL), f32),
        "ln_ffn1": gain(ks[1], (L, D_MODEL)),
        "w_ffn1_gu": w(ks[2], (L, D_MODEL, 2 * D_FF), D_MODEL),
        "w_ffn1_down": w(ks[3], (L, D_FF, D_MODEL), D_FF),
        "ln_mix": gain(ks[4], (L, D_MODEL)),
        "w_in": w(ks[5], (L, D_MODEL, IN_DIM), D_MODEL),
        "pool_w": w(ks[6], (L, POOL_GROUPS, POOL_GROUP_DIM, POOL_GROUP_DIM), POOL_GROUP_DIM),
        "pool_scale": 1.0 + 0.1 * jax.random.normal(ks[7], (L, POOL_DIM), f32),
        "w_pool_branch": w(ks[8], (L, POOL_DIM, D_MODEL), POOL_DIM),
        "q_norm": gain(ks[9], (L, HEAD_DIM)),
        "k_norm": gain(ks[10], (L, HEAD_DIM)),
        "sinks": 0.5 * jax.random.normal(ks[11], (L, N_Q_HEADS), f32),
        "w_attn_branch": w(ks[12], (L, ATTN_DIM, D_MODEL), ATTN_DIM),
        "w_out": w(ks[13], (L, D_MODEL, D_MODEL), D_MODEL),
        "ln_ffn2": gain(ks[14], (L, D_MODEL)),
        "w_ffn2_gu": w(ks[15], (L, D_MODEL, 2 * D_FF), D_MODEL),
        "w_ffn2_down": w(ks[16], (L, D_FF, D_MODEL), D_FF),
    }


def _fwd_reference(x, ln_ffn1, w_ffn1_gu, w_ffn1_down, ln_mix, w_in, pool_w, pool_scale,
              w_pool_branch, q_norm, k_norm, sinks, w_attn_branch, w_out,
              ln_ffn2, w_ffn2_gu, w_ffn2_down):
    B, S, _ = x.shape
    pos = jnp.arange(S, dtype=jnp.float32)
    inv_freq = ROPE_THETA ** (-jnp.arange(0, ROT_DIM, 2, dtype=jnp.float32) / ROT_DIM)
    ang = pos[:, None] * inv_freq[None, :]
    cos, sin = jnp.cos(ang), jnp.sin(ang)
    splits = [POOL_DIM, POOL_DIM + ATTN_DIM, POOL_DIM + ATTN_DIM + KV_DIM,
              POOL_DIM + ATTN_DIM + 2 * KV_DIM]

    for l in range(DEPTH):
        x = x + 0.5 * swiglu(rmsnorm(x, ln_ffn1[l]), w_ffn1_gu[l], w_ffn1_down[l])

        h = rmsnorm(x, ln_mix[l])
        z = h @ w_in[l]
        u_pool, q, k, v, gate_logits = jnp.split(z, splits, axis=-1)

        a = pool_mixer(u_pool, pool_w[l], pool_scale[l]) @ w_pool_branch[l]

        q = rmsnorm(q.reshape(B, S, N_Q_HEADS, HEAD_DIM), q_norm[l])
        k = rmsnorm(k.reshape(B, S, N_KV_HEADS, HEAD_DIM), k_norm[l])
        v = v.reshape(B, S, N_KV_HEADS, HEAD_DIM)
        q = partial_rope(q, cos, sin)
        k = partial_rope(k, cos, sin)
        b = swa_sink_attention(q, k, v, sinks[l]) @ w_attn_branch[l]

        g_pool, g_attn = jnp.split(jax.nn.sigmoid(gate_logits), N_BRANCHES, axis=-1)
        x = x + (g_pool * a + g_attn * b) @ w_out[l]

        x = x + 0.5 * swiglu(rmsnorm(x, ln_ffn2[l]), w_ffn2_gu[l], w_ffn2_down[l])
    return x


import jax as _jax
import jax.numpy as _jnp

TWIN_FORMAT = 'train_step'
FWD_PARAMS = ['x', 'ln_ffn1', 'w_ffn1_gu', 'w_ffn1_down', 'ln_mix', 'w_in', 'pool_w', 'pool_scale', 'w_pool_branch', 'q_norm', 'k_norm', 'sinks', 'w_attn_branch', 'w_out', 'ln_ffn2', 'w_ffn2_gu', 'w_ffn2_down']
TWIN_WEIGHTS = ['ln_ffn1', 'w_ffn1_gu', 'w_ffn1_down', 'ln_mix', 'w_in', 'pool_w', 'pool_scale', 'w_pool_branch', 'q_norm', 'k_norm', 'sinks', 'w_attn_branch', 'w_out', 'ln_ffn2', 'w_ffn2_gu', 'w_ffn2_down']
TWIN_DIFF_INPUT = 'x'
TWIN_INPUTS = ['x', 'ln_ffn1', 'w_ffn1_gu', 'w_ffn1_down', 'ln_mix', 'w_in', 'pool_w', 'pool_scale', 'w_pool_branch', 'q_norm', 'k_norm', 'sinks', 'w_attn_branch', 'w_out', 'ln_ffn2', 'w_ffn2_gu', 'w_ffn2_down', 'loss_target', 'm_ln_ffn1', 'm_w_ffn1_gu', 'm_w_ffn1_down', 'm_ln_mix', 'm_w_in', 'm_pool_w', 'm_pool_scale', 'm_w_pool_branch', 'm_q_norm', 'm_k_norm', 'm_sinks', 'm_w_attn_branch', 'm_w_out', 'm_ln_ffn2', 'm_w_ffn2_gu', 'm_w_ffn2_down', 'v_ln_ffn1', 'v_w_ffn1_gu', 'v_w_ffn1_down', 'v_ln_mix', 'v_w_in', 'v_pool_w', 'v_pool_scale', 'v_w_pool_branch', 'v_q_norm', 'v_k_norm', 'v_sinks', 'v_w_attn_branch', 'v_w_out', 'v_ln_ffn2', 'v_w_ffn2_gu', 'v_w_ffn2_down']
TWIN_OUTPUTS = ['loss', 'grad_x', 'grad_ln_ffn1', 'grad_w_ffn1_gu', 'grad_w_ffn1_down', 'grad_ln_mix', 'grad_w_in', 'grad_pool_w', 'grad_pool_scale', 'grad_w_pool_branch', 'grad_q_norm', 'grad_k_norm', 'grad_sinks', 'grad_w_attn_branch', 'grad_w_out', 'grad_ln_ffn2', 'grad_w_ffn2_gu', 'grad_w_ffn2_down', 'delta_ln_ffn1', 'delta_w_ffn1_gu', 'delta_w_ffn1_down', 'delta_ln_mix', 'delta_w_in', 'delta_pool_w', 'delta_pool_scale', 'delta_w_pool_branch', 'delta_q_norm', 'delta_k_norm', 'delta_sinks', 'delta_w_attn_branch', 'delta_w_out', 'delta_ln_ffn2', 'delta_w_ffn2_gu', 'delta_w_ffn2_down', 'new_m_ln_ffn1', 'new_m_w_ffn1_gu', 'new_m_w_ffn1_down', 'new_m_ln_mix', 'new_m_w_in', 'new_m_pool_w', 'new_m_pool_scale', 'new_m_w_pool_branch', 'new_m_q_norm', 'new_m_k_norm', 'new_m_sinks', 'new_m_w_attn_branch', 'new_m_w_out', 'new_m_ln_ffn2', 'new_m_w_ffn2_gu', 'new_m_w_ffn2_down', 'new_v_ln_ffn1', 'new_v_w_ffn1_gu', 'new_v_w_ffn1_down', 'new_v_ln_mix', 'new_v_w_in', 'new_v_pool_w', 'new_v_pool_scale', 'new_v_w_pool_branch', 'new_v_q_norm', 'new_v_k_norm', 'new_v_sinks', 'new_v_w_attn_branch', 'new_v_w_out', 'new_v_ln_ffn2', 'new_v_w_ffn2_gu', 'new_v_w_ffn2_down']
TWIN_LEAF_KINDS = {'loss': 'loss', 'grad_x': 'grad_x', 'grad_ln_ffn1': 'grad_w', 'grad_w_ffn1_gu': 'grad_w', 'grad_w_ffn1_down': 'grad_w', 'grad_ln_mix': 'grad_w', 'grad_w_in': 'grad_w', 'grad_pool_w': 'grad_w', 'grad_pool_scale': 'grad_w', 'grad_w_pool_branch': 'grad_w', 'grad_q_norm': 'grad_w', 'grad_k_norm': 'grad_w', 'grad_sinks': 'grad_w', 'grad_w_attn_branch': 'grad_w', 'grad_w_out': 'grad_w', 'grad_ln_ffn2': 'grad_w', 'grad_w_ffn2_gu': 'grad_w', 'grad_w_ffn2_down': 'grad_w', 'delta_ln_ffn1': 'delta_w', 'delta_w_ffn1_gu': 'delta_w', 'delta_w_ffn1_down': 'delta_w', 'delta_ln_mix': 'delta_w', 'delta_w_in': 'delta_w', 'delta_pool_w': 'delta_w', 'delta_pool_scale': 'delta_w', 'delta_w_pool_branch': 'delta_w', 'delta_q_norm': 'delta_w', 'delta_k_norm': 'delta_w', 'delta_sinks': 'delta_w', 'delta_w_attn_branch': 'delta_w', 'delta_w_out': 'delta_w', 'delta_ln_ffn2': 'delta_w', 'delta_w_ffn2_gu': 'delta_w', 'delta_w_ffn2_down': 'delta_w', 'new_m_ln_ffn1': 'new_m', 'new_m_w_ffn1_gu': 'new_m', 'new_m_w_ffn1_down': 'new_m', 'new_m_ln_mix': 'new_m', 'new_m_w_in': 'new_m', 'new_m_pool_w': 'new_m', 'new_m_pool_scale': 'new_m', 'new_m_w_pool_branch': 'new_m', 'new_m_q_norm': 'new_m', 'new_m_k_norm': 'new_m', 'new_m_sinks': 'new_m', 'new_m_w_attn_branch': 'new_m', 'new_m_w_out': 'new_m', 'new_m_ln_ffn2': 'new_m', 'new_m_w_ffn2_gu': 'new_m', 'new_m_w_ffn2_down': 'new_m', 'new_v_ln_ffn1': 'new_v', 'new_v_w_ffn1_gu': 'new_v', 'new_v_w_ffn1_down': 'new_v', 'new_v_ln_mix': 'new_v', 'new_v_w_in': 'new_v', 'new_v_pool_w': 'new_v', 'new_v_pool_scale': 'new_v', 'new_v_w_pool_branch': 'new_v', 'new_v_q_norm': 'new_v', 'new_v_k_norm': 'new_v', 'new_v_sinks': 'new_v', 'new_v_w_attn_branch': 'new_v', 'new_v_w_out': 'new_v', 'new_v_ln_ffn2': 'new_v', 'new_v_w_ffn2_gu': 'new_v', 'new_v_w_ffn2_down': 'new_v'}


def _forward(args):
    return _fwd_reference(*[args[k] for k in FWD_PARAMS])


def _output_shape():
    out = _jax.eval_shape(lambda: _forward(_fwd_setup_inputs(0)))
    return out.shape, out.dtype

N_MICROBATCH = 1
ADAM_LR = 0.001
ADAM_B1 = 0.9
ADAM_B2 = 0.999
ADAM_EPS = 1e-08
ADAM_WD = 0.01
ADAM_STEP = 10
PER_EXAMPLE_BATCH_AXIS = {'x': 0, 'loss_target': 0}
SHARED_INPUTS = []
_WEIGHT_DTYPES = {'ln_ffn1': _jnp.float32, 'w_ffn1_gu': _jnp.float32, 'w_ffn1_down': _jnp.float32, 'ln_mix': _jnp.float32, 'w_in': _jnp.float32, 'pool_w': _jnp.float32, 'pool_scale': _jnp.float32, 'w_pool_branch': _jnp.float32, 'q_norm': _jnp.float32, 'k_norm': _jnp.float32, 'sinks': _jnp.float32, 'w_attn_branch': _jnp.float32, 'w_out': _jnp.float32, 'ln_ffn2': _jnp.float32, 'w_ffn2_gu': _jnp.float32, 'w_ffn2_down': _jnp.float32}
MOMENT_SCALE = {'ln_ffn1': 1.222439e+01, 'w_ffn1_gu': 1.616763e-01, 'w_ffn1_down': 2.914100e-01, 'ln_mix': 1.598583e+01, 'w_in': 6.160212e-01, 'pool_w': 2.795011e+00, 'pool_scale': 3.037046e+01, 'w_pool_branch': 1.351569e+00, 'q_norm': 2.964372e+00, 'k_norm': 2.959648e+00, 'sinks': 7.132657e-01, 'w_attn_branch': 9.549475e-02, 'w_out': 1.156785e+00, 'ln_ffn2': 1.210724e+01, 'w_ffn2_gu': 1.481289e-01, 'w_ffn2_down': 2.739295e-01}


def _to_microbatches(a, axis):
    t = _jnp.moveaxis(a, axis, 0)
    t = t.reshape((N_MICROBATCH, t.shape[0] // N_MICROBATCH) + t.shape[1:])
    return _jnp.moveaxis(t, 1, axis + 1)


def setup_inputs(seed: int = 0) -> dict:
    inp = _fwd_setup_inputs(seed)
    key = _jax.random.fold_in(_jax.random.key(seed), 7919)
    shape, _ = _output_shape()
    out = dict(inp)
    out["loss_target"] = _jax.random.normal(_jax.random.fold_in(key, 0), shape, _jnp.float32)
    for i, name in enumerate(TWIN_WEIGHTS):
        w = inp[name].astype(_jnp.float32)
        if MOMENT_SCALE is None:
            s = _jnp.sqrt(_jnp.mean(_jnp.square(w)) + 1e-30)
        else:
            s = MOMENT_SCALE[name]
        km, kv = _jax.random.split(_jax.random.fold_in(key, i + 1))
        out[name] = w
        out["m_" + name] = s * _jax.random.normal(km, w.shape, _jnp.float32)
        out["v_" + name] = (s * s) * _jax.random.uniform(kv, w.shape, _jnp.float32, 0.5, 1.5)
    if N_MICROBATCH > 1:
        for name, axis in PER_EXAMPLE_BATCH_AXIS.items():
            out[name] = _to_microbatches(out[name], axis)
    return {'x': out['x'], 'ln_ffn1': out['ln_ffn1'], 'w_ffn1_gu': out['w_ffn1_gu'], 'w_ffn1_down': out['w_ffn1_down'], 'ln_mix': out['ln_mix'], 'w_in': out['w_in'], 'pool_w': out['pool_w'], 'pool_scale': out['pool_scale'], 'w_pool_branch': out['w_pool_branch'], 'q_norm': out['q_norm'], 'k_norm': out['k_norm'], 'sinks': out['sinks'], 'w_attn_branch': out['w_attn_branch'], 'w_out': out['w_out'], 'ln_ffn2': out['ln_ffn2'], 'w_ffn2_gu': out['w_ffn2_gu'], 'w_ffn2_down': out['w_ffn2_down'], 'loss_target': out['loss_target'], 'm_ln_ffn1': out['m_ln_ffn1'], 'm_w_ffn1_gu': out['m_w_ffn1_gu'], 'm_w_ffn1_down': out['m_w_ffn1_down'], 'm_ln_mix': out['m_ln_mix'], 'm_w_in': out['m_w_in'], 'm_pool_w': out['m_pool_w'], 'm_pool_scale': out['m_pool_scale'], 'm_w_pool_branch': out['m_w_pool_branch'], 'm_q_norm': out['m_q_norm'], 'm_k_norm': out['m_k_norm'], 'm_sinks': out['m_sinks'], 'm_w_attn_branch': out['m_w_attn_branch'], 'm_w_out': out['m_w_out'], 'm_ln_ffn2': out['m_ln_ffn2'], 'm_w_ffn2_gu': out['m_w_ffn2_gu'], 'm_w_ffn2_down': out['m_w_ffn2_down'], 'v_ln_ffn1': out['v_ln_ffn1'], 'v_w_ffn1_gu': out['v_w_ffn1_gu'], 'v_w_ffn1_down': out['v_w_ffn1_down'], 'v_ln_mix': out['v_ln_mix'], 'v_w_in': out['v_w_in'], 'v_pool_w': out['v_pool_w'], 'v_pool_scale': out['v_pool_scale'], 'v_w_pool_branch': out['v_w_pool_branch'], 'v_q_norm': out['v_q_norm'], 'v_k_norm': out['v_k_norm'], 'v_sinks': out['v_sinks'], 'v_w_attn_branch': out['v_w_attn_branch'], 'v_w_out': out['v_w_out'], 'v_ln_ffn2': out['v_ln_ffn2'], 'v_w_ffn2_gu': out['v_w_ffn2_gu'], 'v_w_ffn2_down': out['v_w_ffn2_down']}


def _loss(weights, diff, rest, loss_target):
    with _jax.named_scope("forward"):
        args = {**rest, TWIN_DIFF_INPUT: diff, **{k: w.astype(_WEIGHT_DTYPES[k]) for k, w in weights.items()}}
        y = _forward(args)
    with _jax.named_scope("loss_head"):
        err = _jnp.square(y.astype(_jnp.float32) - loss_target)
        return 0.5 * _jnp.sum(_jnp.mean(err, axis=-1)) if err.ndim else 0.5 * err


def _adamw(w, g, m, v):
    m = ADAM_B1 * m + (1.0 - ADAM_B1) * g
    v = ADAM_B2 * v + (1.0 - ADAM_B2) * _jnp.square(g)
    m_hat = m / (1.0 - ADAM_B1 ** ADAM_STEP)
    v_hat = v / (1.0 - ADAM_B2 ** ADAM_STEP)
    delta = -ADAM_LR * (m_hat / (_jnp.sqrt(v_hat) + ADAM_EPS) + ADAM_WD * w)
    return delta, m, v


def reference(x, ln_ffn1, w_ffn1_gu, w_ffn1_down, ln_mix, w_in, pool_w, pool_scale, w_pool_branch, q_norm, k_norm, sinks, w_attn_branch, w_out, ln_ffn2, w_ffn2_gu, w_ffn2_down, loss_target, m_ln_ffn1, m_w_ffn1_gu, m_w_ffn1_down, m_ln_mix, m_w_in, m_pool_w, m_pool_scale, m_w_pool_branch, m_q_norm, m_k_norm, m_sinks, m_w_attn_branch, m_w_out, m_ln_ffn2, m_w_ffn2_gu, m_w_ffn2_down, v_ln_ffn1, v_w_ffn1_gu, v_w_ffn1_down, v_ln_mix, v_w_in, v_pool_w, v_pool_scale, v_w_pool_branch, v_q_norm, v_k_norm, v_sinks, v_w_attn_branch, v_w_out, v_ln_ffn2, v_w_ffn2_gu, v_w_ffn2_down):
    given = dict(x=x, ln_ffn1=ln_ffn1, w_ffn1_gu=w_ffn1_gu, w_ffn1_down=w_ffn1_down, ln_mix=ln_mix, w_in=w_in, pool_w=pool_w, pool_scale=pool_scale, w_pool_branch=w_pool_branch, q_norm=q_norm, k_norm=k_norm, sinks=sinks, w_attn_branch=w_attn_branch, w_out=w_out, ln_ffn2=ln_ffn2, w_ffn2_gu=w_ffn2_gu, w_ffn2_down=w_ffn2_down, loss_target=loss_target, m_ln_ffn1=m_ln_ffn1, m_w_ffn1_gu=m_w_ffn1_gu, m_w_ffn1_down=m_w_ffn1_down, m_ln_mix=m_ln_mix, m_w_in=m_w_in, m_pool_w=m_pool_w, m_pool_scale=m_pool_scale, m_w_pool_branch=m_w_pool_branch, m_q_norm=m_q_norm, m_k_norm=m_k_norm, m_sinks=m_sinks, m_w_attn_branch=m_w_attn_branch, m_w_out=m_w_out, m_ln_ffn2=m_ln_ffn2, m_w_ffn2_gu=m_w_ffn2_gu, m_w_ffn2_down=m_w_ffn2_down, v_ln_ffn1=v_ln_ffn1, v_w_ffn1_gu=v_w_ffn1_gu, v_w_ffn1_down=v_w_ffn1_down, v_ln_mix=v_ln_mix, v_w_in=v_w_in, v_pool_w=v_pool_w, v_pool_scale=v_pool_scale, v_w_pool_branch=v_w_pool_branch, v_q_norm=v_q_norm, v_k_norm=v_k_norm, v_sinks=v_sinks, v_w_attn_branch=v_w_attn_branch, v_w_out=v_w_out, v_ln_ffn2=v_ln_ffn2, v_w_ffn2_gu=v_w_ffn2_gu, v_w_ffn2_down=v_w_ffn2_down)
    weights = {n: given[n] for n in TWIN_WEIGHTS}
    shared = {n: given[n] for n in SHARED_INPUTS}
    per_example = {n: given[n] for n in ['x']}
    grad_fn = _jax.value_and_grad(_loss, argnums=(0, 1))

    def one_microbatch(ex, loss_target):
        ex = dict(ex)
        diff = ex.pop(TWIN_DIFF_INPUT)
        return grad_fn(weights, diff, {**shared, **ex}, loss_target)

    if N_MICROBATCH == 1:
        loss, (grad_w, grad_x) = one_microbatch(per_example, given["loss_target"])
    else:
        def body(carry, xs):
            loss_sum, grad_sum = carry
            l_k, (gw_k, gx_k) = one_microbatch(xs[0], xs[1])
            with _jax.named_scope("update"):
                return (loss_sum + l_k, _jax.tree.map(_jnp.add, grad_sum, gw_k)), gx_k

        init = (_jnp.zeros((), _jnp.float32), _jax.tree.map(_jnp.zeros_like, weights))
        (loss, grad_w), grad_x = _jax.lax.scan(body, init, (per_example, given["loss_target"]))
    with _jax.named_scope("update"):
        delta_w, new_m, new_v = {}, {}, {}
        for n in TWIN_WEIGHTS:
            delta_w[n], new_m[n], new_v[n] = _adamw(weights[n], grad_w[n], given["m_" + n], given["v_" + n])
    return (loss, grad_x, *[grad_w[n] for n in TWIN_WEIGHTS], *[delta_w[n] for n in TWIN_WEIGHTS],
            *[new_m[n] for n in TWIN_WEIGHTS], *[new_v[n] for n in TWIN_WEIGHTS])
```

```python
import functools

import jax
import jax.numpy as jnp
from jax import lax
from jax.experimental import pallas as pl
from jax.experimental.pallas import tpu as pltpu

F32 = jnp.float32
CDT = jnp.bfloat16

D_MODEL = 1024
POOL_WINDOWS = (2, 4, 8, 16)
POOL_WMAX = 16
GROUP = 128
POOL_DIM = 512
HEAD_DIM = 64
N_Q_HEADS = 8
ATTN_DIM = 512
KV_DIM = 128
QK_DIM = ATTN_DIM + KV_DIM
GATE_DIM = 2 * D_MODEL
BLOCK = 128
ROPE_THETA = 500000.0
ROT_DIM = 16
EPS = 1e-6
ATTN_SCALE = HEAD_DIM ** -0.5

ADAM_LR = 0.001
ADAM_B1 = 0.9
ADAM_B2 = 0.999
ADAM_EPS = 1e-08
ADAM_WD = 0.01
ADAM_STEP = 10

N_CHIPS = 4
N_DEV = 8
LANES = 128
VMEM_LIMIT_BYTES = 48 * 1024 * 1024

MESH = pl.DeviceIdType.MESH
ANY = pl.BlockSpec(memory_space=pl.ANY)

BIG = ("w_ffn1_gu", "w_ffn1_down", "w_in", "w_pool_branch", "w_attn_branch", "w_out", "w_ffn2_gu", "w_ffn2_down")
COL_SHARDED = ("w_ffn1_gu", "w_in", "w_pool_branch", "w_attn_branch", "w_ffn2_gu")
SMALL = ("ln_ffn1", "ln_mix", "pool_w", "pool_scale", "q_norm", "k_norm", "sinks", "ln_ffn2")
WEIGHTS = ("ln_ffn1", "w_ffn1_gu", "w_ffn1_down", "ln_mix", "w_in", "pool_w", "pool_scale", "w_pool_branch",
           "q_norm", "k_norm", "sinks", "w_attn_branch", "w_out", "ln_ffn2", "w_ffn2_gu", "w_ffn2_down")


def _tile(n, target, mult=8):
    if n <= target:
        return n
    for t in range(target - target % mult, 0, -mult):
        if n % t == 0:
            return t
    raise ValueError((n, target, mult))


def _params(*sem):
    return pltpu.CompilerParams(dimension_semantics=sem, vmem_limit_bytes=VMEM_LIMIT_BYTES)


def _sigmoid(v):
    return 1.0 / (1.0 + jnp.exp(-v))


def _dot(a, b):
    return jnp.dot(a, b, preferred_element_type=F32)


def _dot_nt(a, b):
    return lax.dot_general(a, b, (((1,), (1,)), ((), ())), preferred_element_type=F32)


def _dot_tn(a, b):
    return lax.dot_general(a, b, (((0,), (0,)), ((), ())), preferred_element_type=F32)


def _norm_fwd(x, g, name):
    T, Dm = x.shape
    tm = _tile(T, 512)

    def body(x_ref, g_ref, h_ref):
        xv = x_ref[...]
        r = lax.rsqrt(jnp.mean(xv * xv, axis=-1, keepdims=True) + EPS)
        h_ref[...] = (xv * r * g_ref[...]).astype(h_ref.dtype)

    row = pl.BlockSpec((tm, Dm), lambda i: (i, 0))
    return pl.pallas_call(
        body, name=name, grid=(T // tm,),
        in_specs=[row, pl.BlockSpec((1, Dm), lambda i: (0, 0))], out_specs=row,
        out_shape=jax.ShapeDtypeStruct((T, Dm), CDT), compiler_params=_params("parallel"),
    )(x, g.reshape(1, Dm))


def _norm_bwd(dh, x, g, dres, name):
    T, Dm = x.shape
    tm = _tile(T, 512)

    def body(dh_ref, x_ref, g_ref, dres_ref, dx_ref, dg_ref):
        @pl.when(pl.program_id(0) == 0)
        def _():
            dg_ref[...] = jnp.zeros_like(dg_ref)

        xv = x_ref[...]
        dhv = dh_ref[...]
        r = lax.rsqrt(jnp.mean(xv * xv, axis=-1, keepdims=True) + EPS)
        xh = xv * r
        dg_ref[...] += jnp.sum(dhv * xh, axis=0, keepdims=True)
        dxh = dhv * g_ref[...]
        dx_ref[...] = dres_ref[...] + r * (dxh - xh * jnp.mean(dxh * xh, axis=-1, keepdims=True))

    row = pl.BlockSpec((tm, Dm), lambda i: (i, 0))
    vec = pl.BlockSpec((1, Dm), lambda i: (0, 0))
    return pl.pallas_call(
        body, name=name, grid=(T // tm,),
        in_specs=[row, row, vec, row], out_specs=[row, vec],
        out_shape=[jax.ShapeDtypeStruct((T, Dm), F32), jax.ShapeDtypeStruct((1, Dm), F32)],
        compiler_params=_params("arbitrary"),
    )(dh, x, g.reshape(1, Dm), dres)


def _loss_head(y, tgt, name):
    T, Dm = y.shape
    tm = _tile(T, 512)

    def body(y_ref, t_ref, dy_ref, loss_ref):
        @pl.when(pl.program_id(0) == 0)
        def _():
            loss_ref[...] = jnp.zeros_like(loss_ref)

        diff = y_ref[...] - t_ref[...]
        dy_ref[...] = diff * (1.0 / Dm)
        part = jnp.sum(jnp.mean(diff * diff, axis=-1, keepdims=True), axis=0, keepdims=True)
        loss_ref[...] += 0.5 * part

    row = pl.BlockSpec((tm, Dm), lambda i: (i, 0))
    one = pl.BlockSpec((1, 1), lambda i: (0, 0))
    return pl.pallas_call(
        body, name=name, grid=(T // tm,),
        in_specs=[row, row], out_specs=[row, one],
        out_shape=[jax.ShapeDtypeStruct((T, Dm), F32), jax.ShapeDtypeStruct((1, 1), F32)],
        compiler_params=_params("arbitrary"),
    )(y, tgt)


def _mm_nn(a, b, name, out_dtype, res=None, scale=1.0, tm_target=512):
    M, K = a.shape
    N = b.shape[1]
    tm = _tile(M, tm_target)

    def body(a_ref, b_ref, *rest):
        acc = _dot(a_ref[...].astype(CDT), b_ref[...])
        if res is None:
            (o_ref,) = rest
        else:
            r_ref, o_ref = rest
            acc = r_ref[...] + scale * acc
        o_ref[...] = acc.astype(o_ref.dtype)

    in_specs = [pl.BlockSpec((tm, K), lambda i: (i, 0)), pl.BlockSpec((K, N), lambda i: (0, 0))]
    args = [a, b]
    if res is not None:
        in_specs.append(pl.BlockSpec((tm, N), lambda i: (i, 0)))
        args.append(res)
    return pl.pallas_call(
        body, name=name, grid=(M // tm,), in_specs=in_specs,
        out_specs=pl.BlockSpec((tm, N), lambda i: (i, 0)),
        out_shape=jax.ShapeDtypeStruct((M, N), out_dtype), compiler_params=_params("parallel"),
    )(*args)


def _mm_nt(a, b, name, out_dtype, tm_target=512):
    M, K = a.shape
    N = b.shape[0]
    tm = _tile(M, tm_target)

    def body(a_ref, b_ref, o_ref):
        o_ref[...] = _dot_nt(a_ref[...].astype(CDT), b_ref[...]).astype(o_ref.dtype)

    return pl.pallas_call(
        body, name=name, grid=(M // tm,),
        in_specs=[pl.BlockSpec((tm, K), lambda i: (i, 0)), pl.BlockSpec((N, K), lambda i: (0, 0))],
        out_specs=pl.BlockSpec((tm, N), lambda i: (i, 0)),
        out_shape=jax.ShapeDtypeStruct((M, N), out_dtype), compiler_params=_params("parallel"),
    )(a, b)


def _mm_tn(x, dy, name, scale=1.0, tn_target=1664, tm_target=1408, tk_target=1024):
    T, M = x.shape
    split = dy.ndim == 3
    Nh = dy.shape[-1]
    N = 2 * Nh if split else Nh
    tm = _tile(M, tm_target, LANES)
    tn = _tile(Nh, tn_target, LANES)
    tk = _tile(T, tk_target)
    nk = T // tk
    njh = Nh // tn

    def body(x_ref, dy_ref, o_ref):
        @pl.when(pl.program_id(2) == 0)
        def _():
            o_ref[...] = jnp.zeros_like(o_ref)

        part = _dot_tn(x_ref[...].astype(CDT), dy_ref[...].astype(CDT))
        o_ref[...] += part if scale == 1.0 else scale * part

    if split:
        dy_spec = pl.BlockSpec((None, tk, tn), lambda i, j, k: (j // njh, k, j % njh))
    else:
        dy_spec = pl.BlockSpec((tk, tn), lambda i, j, k: (k, j))
    return pl.pallas_call(
        body, name=name, grid=(M // tm, N // tn, nk),
        in_specs=[pl.BlockSpec((tk, tm), lambda i, j, k: (k, i)), dy_spec],
        out_specs=pl.BlockSpec((tm, tn), lambda i, j, k: (i, j)),
        out_shape=jax.ShapeDtypeStruct((M, N), F32),
        compiler_params=_params("parallel", "parallel", "arbitrary"),
    )(x, dy)


def _ffn_up(h, wgu, name):
    T, Dm = h.shape
    Fd = wgu.shape[1] // 2
    tm = _tile(T, 512)
    tn = _tile(Fd, 1408, LANES)
    nj = Fd // tn

    def body(h_ref, wg_ref, wu_ref, gu_ref, a_ref):
        hv = h_ref[...]
        g = _dot(hv, wg_ref[...])
        u = _dot(hv, wu_ref[...])
        gu_ref[0] = g.astype(gu_ref.dtype)
        gu_ref[1] = u.astype(gu_ref.dtype)
        a_ref[...] = (g * _sigmoid(g) * u).astype(a_ref.dtype)

    return pl.pallas_call(
        body, name=name, grid=(T // tm, nj),
        in_specs=[pl.BlockSpec((tm, Dm), lambda i, j: (i, 0)),
                  pl.BlockSpec((Dm, tn), lambda i, j: (0, j)),
                  pl.BlockSpec((Dm, tn), lambda i, j: (0, j + nj))],
        out_specs=[pl.BlockSpec((2, tm, tn), lambda i, j: (0, i, j)), pl.BlockSpec((tm, tn), lambda i, j: (i, j))],
        out_shape=[jax.ShapeDtypeStruct((2, T, Fd), CDT), jax.ShapeDtypeStruct((T, Fd), CDT)],
        compiler_params=_params("parallel", "parallel"),
    )(h, wgu, wgu)


def _ffn_down_bwd(dxo, wd, gu, name):
    T, Dm = dxo.shape
    Fd = wd.shape[0]
    tm = _tile(T, 512)
    tn = _tile(Fd, 1408, LANES)

    def body(dx_ref, wd_ref, gu_ref, dgu_ref, a_ref):
        da = 0.5 * _dot_nt(dx_ref[...].astype(CDT), wd_ref[...])
        g = gu_ref[0].astype(F32)
        u = gu_ref[1].astype(F32)
        sg = _sigmoid(g)
        silu = g * sg
        a_ref[...] = (silu * u).astype(a_ref.dtype)
        dgu_ref[0] = (da * u * (sg * (1.0 + g * (1.0 - sg)))).astype(dgu_ref.dtype)
        dgu_ref[1] = (da * silu).astype(dgu_ref.dtype)

    gu_spec = pl.BlockSpec((2, tm, tn), lambda i, j: (0, i, j))
    return pl.pallas_call(
        body, name=name, grid=(T // tm, Fd // tn),
        in_specs=[pl.BlockSpec((tm, Dm), lambda i, j: (i, 0)), pl.BlockSpec((tn, Dm), lambda i, j: (j, 0)), gu_spec],
        out_specs=[gu_spec, pl.BlockSpec((tm, tn), lambda i, j: (i, j))],
        out_shape=[jax.ShapeDtypeStruct((2, T, Fd), CDT), jax.ShapeDtypeStruct((T, Fd), CDT)],
        compiler_params=_params("parallel", "parallel"),
    )(dxo, wd, gu)


def _mm_nt_gu(dgu, wgu, name):
    _, T, Fd = dgu.shape
    Dm = wgu.shape[0]
    tm = _tile(T, 512)
    tk = _tile(Fd, 1408, LANES)
    nkh = Fd // tk
    nk = 2 * nkh

    def body(a_ref, b_ref, o_ref, acc_ref):
        k = pl.program_id(1)

        @pl.when(k == 0)
        def _():
            acc_ref[...] = jnp.zeros_like(acc_ref)

        acc_ref[...] += _dot_nt(a_ref[...], b_ref[...])

        @pl.when(k == nk - 1)
        def _():
            o_ref[...] = acc_ref[...]

    return pl.pallas_call(
        body, name=name, grid=(T // tm, nk),
        in_specs=[pl.BlockSpec((None, tm, tk), lambda i, k: (k // nkh, i, k % nkh)),
                  pl.BlockSpec((Dm, tk), lambda i, k: (0, k))],
        out_specs=pl.BlockSpec((tm, Dm), lambda i, k: (i, 0)),
        out_shape=jax.ShapeDtypeStruct((T, Dm), F32),
        scratch_shapes=[pltpu.VMEM((tm, Dm), F32)],
        compiler_params=_params("parallel", "arbitrary"),
    )(dgu, wgu)


def _mm_in(h, w_in, name):
    T, Dm = h.shape
    tm = _tile(T, 256)
    widths = (POOL_DIM, QK_DIM, KV_DIM, GATE_DIM)

    def body(h_ref, w_ref, *outs):
        z = _dot(h_ref[...], w_ref[...])
        lo = 0
        for o_ref, wd in zip(outs, widths):
            o_ref[...] = z[:, lo:lo + wd]
            lo += wd

    return pl.pallas_call(
        body, name=name, grid=(T // tm,),
        in_specs=[pl.BlockSpec((tm, Dm), lambda i: (i, 0)), pl.BlockSpec(w_in.shape, lambda i: (0, 0))],
        out_specs=[pl.BlockSpec((tm, wd), lambda i: (i, 0)) for wd in widths],
        out_shape=[jax.ShapeDtypeStruct((T, wd), F32) for wd in widths],
        compiler_params=_params("parallel"),
    )(h, w_in)


def _window_mean_minus_token(ext, u, g, w, pos):
    sl = slice(g * GROUP, (g + 1) * GROUP)
    s = ext[:, sl]
    span = 1
    while span < w:
        s = s + pltpu.roll(s, span, axis=0)
        span *= 2
    cnt = jnp.minimum(pos + 1, w).astype(F32)
    return s[POOL_WMAX:, :] / cnt - u[:, sl]


def _pool_fwd(zu, pool_w, scale, name):
    T = zu.shape[0]
    tm = _tile(T, 512, POOL_WMAX)
    hb = tm // POOL_WMAX

    def body(u_ref, halo_ref, pw_ref, sc_ref, pm_ref):
        i = pl.program_id(0)
        u = u_ref[...]
        halo = jnp.where(i > 0, halo_ref[...], 0.0)
        ext = jnp.concatenate([halo, u], axis=0)
        pos = i * tm + lax.broadcasted_iota(jnp.int32, (tm, 1), 0)
        ys = []
        for g, w in enumerate(POOL_WINDOWS):
            d = _window_mean_minus_token(ext, u, g, w, pos)
            ys.append(_dot(d.astype(CDT), pw_ref[g]))
        pm_ref[...] = (jnp.concatenate(ys, axis=1) * sc_ref[...]).astype(pm_ref.dtype)

    row = pl.BlockSpec((tm, POOL_DIM), lambda i: (i, 0))
    return pl.pallas_call(
        body, name=name, grid=(T // tm,),
        in_specs=[row, pl.BlockSpec((POOL_WMAX, POOL_DIM), lambda i: (jnp.maximum(i * hb - 1, 0), 0)),
                  pl.BlockSpec(pool_w.shape, lambda i: (0, 0, 0)), pl.BlockSpec((1, POOL_DIM), lambda i: (0, 0))],
        out_specs=row, out_shape=jax.ShapeDtypeStruct((T, POOL_DIM), CDT),
        compiler_params=_params("parallel"),
    )(zu, zu, pool_w, scale.reshape(1, POOL_DIM))


def _pool_bwd(zu, dpm, pool_w, scale, name):
    T = zu.shape[0]
    tm = _tile(T, 512, POOL_WMAX)
    hb = tm // POOL_WMAX
    nsteps = T // tm
    ext_rows = tm + POOL_WMAX

    def body(u_ref, halo_ref, dpm_ref, dnext_ref, pw_ref, sc_ref, du_ref, dpw_ref, dsc_ref):
        i = pl.program_id(0)

        @pl.when(i == 0)
        def _():
            dpw_ref[...] = jnp.zeros_like(dpw_ref)
            dsc_ref[...] = jnp.zeros_like(dsc_ref)

        u = u_ref[...]
        halo = jnp.where(i > 0, halo_ref[...], 0.0)
        ext = jnp.concatenate([halo, u], axis=0)
        dpm_t = dpm_ref[...].astype(F32)
        dnext = jnp.where(i < nsteps - 1, dnext_ref[...].astype(F32), 0.0)
        dext = jnp.concatenate([dpm_t, dnext], axis=0)
        sc = sc_ref[...]
        pos = i * tm + lax.broadcasted_iota(jnp.int32, (tm, 1), 0)
        pos_ext = i * tm + lax.broadcasted_iota(jnp.int32, (ext_rows, 1), 0)
        dus, dscs = [], []
        for g, w in enumerate(POOL_WINDOWS):
            sl = slice(g * GROUP, (g + 1) * GROUP)
            dc = _window_mean_minus_token(ext, u, g, w, pos).astype(CDT)
            y = _dot(dc, pw_ref[g])
            dscs.append(jnp.sum(dpm_t[:, sl] * y, axis=0, keepdims=True))
            dy_ext = (dext[:, sl] * sc[:, sl]).astype(CDT)
            dpw_ref[g] += _dot_tn(dc, dy_ext[:tm])
            dd = _dot_nt(dy_ext, pw_ref[g])
            r = dd / jnp.minimum(pos_ext + 1, w).astype(F32)
            span = 1
            while span < w:
                r = r + pltpu.roll(r, ext_rows - span, axis=0)
                span *= 2
            dus.append(r[:tm] - dd[:tm])
        du_ref[...] = jnp.concatenate(dus, axis=1).astype(du_ref.dtype)
        dsc_ref[...] += jnp.concatenate(dscs, axis=1)

    row = pl.BlockSpec((tm, POOL_DIM), lambda i: (i, 0))
    prev = pl.BlockSpec((POOL_WMAX, POOL_DIM), lambda i: (jnp.maximum(i * hb - 1, 0), 0))
    nxt = pl.BlockSpec((POOL_WMAX, POOL_DIM), lambda i: (jnp.minimum((i + 1) * hb, nsteps * hb - 1), 0))
    return pl.pallas_call(
        body, name=name, grid=(nsteps,),
        in_specs=[row, prev, row, nxt, pl.BlockSpec(pool_w.shape, lambda i: (0, 0, 0)),
                  pl.BlockSpec((1, POOL_DIM), lambda i: (0, 0))],
        out_specs=[row, pl.BlockSpec(pool_w.shape, lambda i: (0, 0, 0)), pl.BlockSpec((1, POOL_DIM), lambda i: (0, 0))],
        out_shape=[jax.ShapeDtypeStruct((T, POOL_DIM), CDT), jax.ShapeDtypeStruct(pool_w.shape, F32),
                   jax.ShapeDtypeStruct((1, POOL_DIM), F32)],
        compiler_params=_params("arbitrary"),
    )(zu, zu, dpm, dpm, pool_w, scale.reshape(1, POOL_DIM))


def _rope_tables(T):
    pos = jnp.arange(T, dtype=F32)
    inv_freq = ROPE_THETA ** (-jnp.arange(0, ROT_DIM, 2, dtype=F32) / ROT_DIM)
    ang = pos[:, None] * inv_freq[None, :]
    cos, sin = jnp.cos(ang), jnp.sin(ang)
    rest = HEAD_DIM - ROT_DIM
    cos_h = jnp.concatenate([cos, cos, jnp.ones((T, rest), F32)], axis=1)
    sin_h = jnp.concatenate([-sin, sin, jnp.zeros((T, rest), F32)], axis=1)
    return jnp.tile(cos_h, (1, 2)), jnp.tile(sin_h, (1, 2))


def _lane_masks():
    lane = lax.broadcasted_iota(jnp.int32, (1, LANES), 1)
    in_head = lane % HEAD_DIM
    return lane < HEAD_DIM, in_head < ROT_DIM // 2


def _rope_partner(v, low):
    lane = lax.broadcasted_iota(jnp.int32, (1, LANES), 1)
    swapped = jnp.where(low, pltpu.roll(v, LANES - ROT_DIM // 2, axis=1), pltpu.roll(v, ROT_DIM // 2, axis=1))
    return jnp.where(lane % HEAD_DIM < ROT_DIM, swapped, 0.0)


def _head_mean(v, first):
    lo = jnp.sum(jnp.where(first, v, 0.0), axis=-1, keepdims=True)
    hi = jnp.sum(jnp.where(first, 0.0, v), axis=-1, keepdims=True)
    return jnp.where(first, lo, hi) * (1.0 / HEAD_DIM)


def _qk_fwd(zqk, gqk, cos_t, sin_t, name):
    T = zqk.shape[0]
    tm = _tile(T, 512)

    def body(z_ref, g_ref, c_ref, s_ref, o_ref):
        first, low = _lane_masks()
        cosv, sinv = c_ref[...], s_ref[...]
        for c in range(QK_DIM // LANES):
            sl = slice(c * LANES, (c + 1) * LANES)
            xv = z_ref[:, sl]
            r = lax.rsqrt(_head_mean(xv * xv, first) + EPS)
            xn = xv * r * g_ref[:, sl]
            o_ref[:, sl] = (xn * cosv + _rope_partner(xn, low) * sinv).astype(o_ref.dtype)

    row = pl.BlockSpec((tm, QK_DIM), lambda i: (i, 0))
    tab = pl.BlockSpec((tm, LANES), lambda i: (i, 0))
    return pl.pallas_call(
        body, name=name, grid=(T // tm,),
        in_specs=[row, pl.BlockSpec((1, QK_DIM), lambda i: (0, 0)), tab, tab], out_specs=row,
        out_shape=jax.ShapeDtypeStruct((T, QK_DIM), CDT), compiler_params=_params("parallel"),
    )(zqk, gqk, cos_t, sin_t)


def _qk_bwd(dqk, zqk, gqk, cos_t, sin_t, name):
    T = zqk.shape[0]
    tm = _tile(T, 512)

    def body(d_ref, z_ref, g_ref, c_ref, s_ref, dz_ref, dg_ref):
        @pl.when(pl.program_id(0) == 0)
        def _():
            dg_ref[...] = jnp.zeros_like(dg_ref)

        first, low = _lane_masks()
        cosv, sinv = c_ref[...], s_ref[...]
        dgs = []
        for c in range(QK_DIM // LANES):
            sl = slice(c * LANES, (c + 1) * LANES)
            dout = d_ref[:, sl]
            dxn = dout * cosv + _rope_partner(dout * sinv, low)
            xv = z_ref[:, sl]
            r = lax.rsqrt(_head_mean(xv * xv, first) + EPS)
            xh = xv * r
            dgs.append(jnp.sum(dxn * xh, axis=0, keepdims=True))
            dxh = dxn * g_ref[:, sl]
            dz_ref[:, sl] = (r * (dxh - xh * _head_mean(dxh * xh, first))).astype(dz_ref.dtype)
        dg_ref[...] += jnp.concatenate(dgs, axis=1)

    row = pl.BlockSpec((tm, QK_DIM), lambda i: (i, 0))
    tab = pl.BlockSpec((tm, LANES), lambda i: (i, 0))
    vec = pl.BlockSpec((1, QK_DIM), lambda i: (0, 0))
    return pl.pallas_call(
        body, name=name, grid=(T // tm,),
        in_specs=[row, row, vec, tab, tab], out_specs=[row, vec],
        out_shape=[jax.ShapeDtypeStruct((T, QK_DIM), CDT), jax.ShapeDtypeStruct((1, QK_DIM), F32)],
        compiler_params=_params("arbitrary"),
    )(dqk, zqk, gqk, cos_t, sin_t)


def _dup_half(v, first, kv):
    swapped = pltpu.roll(v, HEAD_DIM, axis=1)
    return jnp.where(first, v, swapped) if kv == 0 else jnp.where(first, swapped, v)


def _band_mask(first_key):
    qi = lax.broadcasted_iota(jnp.int32, (BLOCK, 2 * BLOCK), 0)
    ki = lax.broadcasted_iota(jnp.int32, (BLOCK, 2 * BLOCK), 1)
    diff = qi + BLOCK - ki
    return (diff >= 0) & (diff < BLOCK) & (ki >= first_key)


def _attn_blocks(T):
    return _tile(T // BLOCK, 4, 1)


def _attn_fwd(qkn, zv, sinks, name):
    T = qkn.shape[0]
    R = _attn_blocks(T)
    tq = R * BLOCK

    def body(sink_ref, qk_ref, qkp_ref, v_ref, vp_ref, o_ref):
        i = pl.program_id(0)
        first, _ = _lane_masks()
        kall = jnp.concatenate([qkp_ref[:, ATTN_DIM:], qk_ref[:, ATTN_DIM:]], axis=0)
        vall = jnp.concatenate([vp_ref[...], v_ref[...]], axis=0).astype(CDT)
        for r in range(R):
            mask = _band_mask(jnp.where(i == 0, BLOCK, 0)) if r == 0 else _band_mask(0)
            rows = slice(r * BLOCK, (r + 2) * BLOCK)
            kdup = [_dup_half(kall[rows], first, kv) for kv in range(2)]
            vdup = [_dup_half(vall[rows], first, kv) for kv in range(2)]
            for c in range(ATTN_DIM // LANES):
                kv = c // 2
                qc = qk_ref[r * BLOCK:(r + 1) * BLOCK, c * LANES:(c + 1) * LANES]
                halves = []
                for e in range(2):
                    qm = jnp.where(first, qc, jnp.zeros_like(qc)) if e == 0 else jnp.where(first, jnp.zeros_like(qc), qc)
                    s = _dot_nt(qm, kdup[kv]) * ATTN_SCALE
                    s = jnp.where(mask, s, -jnp.inf)
                    sink = sink_ref[2 * c + e]
                    m = jnp.maximum(jnp.max(s, axis=-1, keepdims=True), sink)
                    p = jnp.exp(s - m)
                    denom = jnp.sum(p, axis=-1, keepdims=True) + jnp.exp(sink - m)
                    halves.append(_dot((p / denom).astype(CDT), vdup[kv]))
                o_ref[r * BLOCK:(r + 1) * BLOCK, c * LANES:(c + 1) * LANES] = jnp.where(first, halves[0], halves[1]).astype(o_ref.dtype)

    prev = lambda i: (jnp.maximum(i * R - 1, 0), 0)
    return pl.pallas_call(
        body, name=name, grid=(T // tq,),
        in_specs=[pl.BlockSpec(memory_space=pltpu.SMEM),
                  pl.BlockSpec((tq, QK_DIM), lambda i: (i, 0)), pl.BlockSpec((BLOCK, QK_DIM), prev),
                  pl.BlockSpec((tq, KV_DIM), lambda i: (i, 0)), pl.BlockSpec((BLOCK, KV_DIM), prev)],
        out_specs=pl.BlockSpec((tq, ATTN_DIM), lambda i: (i, 0)),
        out_shape=jax.ShapeDtypeStruct((T, ATTN_DIM), CDT), compiler_params=_params("parallel"),
    )(sinks, qkn, qkn, zv, zv)


def _attn_bwd(qkn, zv, sinks, do, name):
    T = qkn.shape[0]
    R = _attn_blocks(T)
    tq = R * BLOCK

    def body(sink_ref, qk_ref, qkp_ref, v_ref, vp_ref, do_ref, dq_ref, dkc_ref, dkp_ref, dvc_ref, dvp_ref, ds_ref):
        i = pl.program_id(0)

        @pl.when(i == 0)
        def _():
            ds_ref[...] = jnp.zeros_like(ds_ref)

        first, _ = _lane_masks()
        kall = jnp.concatenate([qkp_ref[:, ATTN_DIM:], qk_ref[:, ATTN_DIM:]], axis=0)
        vall = jnp.concatenate([vp_ref[...], v_ref[...]], axis=0).astype(CDT)
        for r in range(R):
            mask = _band_mask(jnp.where(i == 0, BLOCK, 0)) if r == 0 else _band_mask(0)
            rows = slice(r * BLOCK, (r + 2) * BLOCK)
            qrows = slice(r * BLOCK, (r + 1) * BLOCK)
            dk_out, dv_out = [], []
            for kv in range(2):
                kdup = _dup_half(kall[rows], first, kv)
                vdup = _dup_half(vall[rows], first, kv)
                dk_acc = jnp.zeros((2 * BLOCK, LANES), F32)
                dv_acc = jnp.zeros((2 * BLOCK, LANES), F32)
                for c in (2 * kv, 2 * kv + 1):
                    cols = slice(c * LANES, (c + 1) * LANES)
                    qc = qk_ref[qrows, cols]
                    doc = do_ref[qrows, cols].astype(CDT)
                    dq_halves = []
                    for e in range(2):
                        zq, zd = jnp.zeros_like(qc), jnp.zeros_like(doc)
                        qm = jnp.where(first, qc, zq) if e == 0 else jnp.where(first, zq, qc)
                        dom = jnp.where(first, doc, zd) if e == 0 else jnp.where(first, zd, doc)
                        s = _dot_nt(qm, kdup) * ATTN_SCALE
                        s = jnp.where(mask, s, -jnp.inf)
                        sink = sink_ref[2 * c + e]
                        m = jnp.maximum(jnp.max(s, axis=-1, keepdims=True), sink)
                        pu = jnp.exp(s - m)
                        es = jnp.exp(sink - m)
                        inv = 1.0 / (jnp.sum(pu, axis=-1, keepdims=True) + es)
                        p = pu * inv
                        dp = _dot_nt(dom, vdup)
                        delta = jnp.sum(p * dp, axis=-1, keepdims=True)
                        dsc = (p * (dp - delta)).astype(CDT)
                        h = 2 * c + e
                        ds_ref[h:h + 1, :] += jnp.sum(es * inv * delta, axis=0, keepdims=True)
                        dq_halves.append(_dot(dsc, kdup) * ATTN_SCALE)
                        dk_acc = dk_acc + _dot_tn(dsc, qm) * ATTN_SCALE
                        dv_acc = dv_acc + _dot_tn(p.astype(CDT), dom)
                    dq_ref[qrows, cols] = jnp.where(first, dq_halves[0], dq_halves[1])
                dk_out.append(dk_acc + pltpu.roll(dk_acc, HEAD_DIM, axis=1))
                dv_out.append(dv_acc + pltpu.roll(dv_acc, HEAD_DIM, axis=1))
            dk = jnp.where(first, dk_out[0], dk_out[1])
            dv = jnp.where(first, dv_out[0], dv_out[1])
            dkp_ref[qrows, :] = dk[:BLOCK]
            dkc_ref[qrows, :] = dk[BLOCK:]
            dvp_ref[qrows, :] = dv[:BLOCK]
            dvc_ref[qrows, :] = dv[BLOCK:]

    prev = lambda i: (jnp.maximum(i * R - 1, 0), 0)
    kvrow = pl.BlockSpec((tq, KV_DIM), lambda i: (i, 0))
    qrow = pl.BlockSpec((tq, ATTN_DIM), lambda i: (i, 0))
    kv_shape = jax.ShapeDtypeStruct((T, KV_DIM), F32)
    return pl.pallas_call(
        body, name=name, grid=(T // tq,),
        in_specs=[pl.BlockSpec(memory_space=pltpu.SMEM),
                  pl.BlockSpec((tq, QK_DIM), lambda i: (i, 0)), pl.BlockSpec((BLOCK, QK_DIM), prev),
                  kvrow, pl.BlockSpec((BLOCK, KV_DIM), prev), qrow],
        out_specs=[qrow, kvrow, kvrow, kvrow, kvrow, pl.BlockSpec((N_Q_HEADS, LANES), lambda i: (0, 0))],
        out_shape=[jax.ShapeDtypeStruct((T, ATTN_DIM), F32), kv_shape, kv_shape, kv_shape, kv_shape,
                   jax.ShapeDtypeStruct((N_Q_HEADS, LANES), F32)],
        compiler_params=_params("arbitrary"),
    )(sinks, qkn, qkn, zv, zv, do)


def _merge_fwd(pm, o, w_pb, w_ab, zg, name):
    T = pm.shape[0]
    tm = _tile(T, 512)

    def body(pm_ref, o_ref, wp_ref, wa_ref, zg_ref, a_ref, b_ref, m_ref):
        a = _dot(pm_ref[...], wp_ref[...])
        b = _dot(o_ref[...], wa_ref[...])
        gp = _sigmoid(zg_ref[:, :D_MODEL])
        ga = _sigmoid(zg_ref[:, D_MODEL:])
        a_ref[...] = a.astype(a_ref.dtype)
        b_ref[...] = b.astype(b_ref.dtype)
        m_ref[...] = (gp * a + ga * b).astype(m_ref.dtype)

    half = pl.BlockSpec((tm, POOL_DIM), lambda i: (i, 0))
    full = pl.BlockSpec((tm, D_MODEL), lambda i: (i, 0))
    wspec = pl.BlockSpec((POOL_DIM, D_MODEL), lambda i: (0, 0))
    out = jax.ShapeDtypeStruct((T, D_MODEL), CDT)
    return pl.pallas_call(
        body, name=name, grid=(T // tm,),
        in_specs=[half, half, wspec, wspec, pl.BlockSpec((tm, GATE_DIM), lambda i: (i, 0))],
        out_specs=[full, full, full], out_shape=[out, out, out], compiler_params=_params("parallel"),
    )(pm, o, w_pb, w_ab, zg)


def _merge_bwd(dxo, w_out, a, b, zg, name):
    T = dxo.shape[0]
    tm = _tile(T, 512)

    def body(dx_ref, w_ref, a_ref, b_ref, zg_ref, da_ref, db_ref, dg_ref):
        dm = _dot_nt(dx_ref[...].astype(CDT), w_ref[...])
        gp = _sigmoid(zg_ref[:, :D_MODEL])
        ga = _sigmoid(zg_ref[:, D_MODEL:])
        da_ref[...] = (dm * gp).astype(da_ref.dtype)
        db_ref[...] = (dm * ga).astype(db_ref.dtype)
        dg_ref[:, :D_MODEL] = (dm * a_ref[...].astype(F32) * (gp * (1.0 - gp))).astype(dg_ref.dtype)
        dg_ref[:, D_MODEL:] = (dm * b_ref[...].astype(F32) * (ga * (1.0 - ga))).astype(dg_ref.dtype)

    full = pl.BlockSpec((tm, D_MODEL), lambda i: (i, 0))
    gate = pl.BlockSpec((tm, GATE_DIM), lambda i: (i, 0))
    out = jax.ShapeDtypeStruct((T, D_MODEL), CDT)
    return pl.pallas_call(
        body, name=name, grid=(T // tm,),
        in_specs=[full, pl.BlockSpec((D_MODEL, D_MODEL), lambda i: (0, 0)), full, full, gate],
        out_specs=[full, full, gate], out_shape=[out, out, jax.ShapeDtypeStruct((T, GATE_DIM), CDT)],
        compiler_params=_params("parallel"),
    )(dxo, w_out, a, b, zg)


def _adamw(w, g, m, v, name):
    Rr, C = w.shape
    tr = _tile(Rr, max(8, (1 << 19) // C // 8 * 8))

    def body(w_ref, g_ref, m_ref, v_ref, d_ref, nm_ref, nv_ref):
        gv = g_ref[...]
        nm = ADAM_B1 * m_ref[...] + (1.0 - ADAM_B1) * gv
        nv = ADAM_B2 * v_ref[...] + (1.0 - ADAM_B2) * (gv * gv)
        m_hat = nm / (1.0 - ADAM_B1 ** ADAM_STEP)
        v_hat = nv / (1.0 - ADAM_B2 ** ADAM_STEP)
        d_ref[...] = -ADAM_LR * (m_hat / (jnp.sqrt(v_hat) + ADAM_EPS) + ADAM_WD * w_ref[...])
        nm_ref[...] = nm
        nv_ref[...] = nv

    blk = pl.BlockSpec((tr, C), lambda i: (i, 0))
    out = jax.ShapeDtypeStruct((Rr, C), F32)
    return pl.pallas_call(
        body, name=name, grid=(Rr // tr,), in_specs=[blk] * 4, out_specs=[blk] * 3, out_shape=[out] * 3,
        compiler_params=_params("parallel"),
    )(w, g, m, v)


def _place():
    return lax.axis_index("x"), lax.axis_index("y"), lax.axis_index("c")


def _other_chip(x, y, d):
    return (1 - x if d & 2 else x), (1 - y if d & 1 else y)


def _rcopy(src, dst, ssem, rsem, dev):
    return pltpu.make_async_remote_copy(src_ref=src, dst_ref=dst, send_sem=ssem, recv_sem=rsem, device_id=dev,
                                        device_id_type=MESH)


def _all_gather_weights(shards):
    n = len(shards)
    Lh = shards[0].shape[0] // 2

    def body(*refs):
        ins, outs = refs[:n], refs[n:2 * n]
        ssem, rsem, fssem, frsem, lsem = refs[2 * n:]
        x, y, c = _place()
        j = 2 * x + y
        mine, other = pl.ds(c * Lh, Lh), pl.ds((1 - c) * Lh, Lh)
        sibling = (x, y, 1 - c)
        local = [pltpu.make_async_copy(ins[w], outs[w].at[j], lsem.at[w]) for w in range(n)]
        for cp in local:
            cp.start()
        sends = []
        for w in range(n):
            for d in (1, 2, 3):
                px, py = _other_chip(x, y, d)
                k = 3 * w + d - 1
                sends.append(_rcopy(ins[w].at[mine], outs[w].at[j, mine], ssem.at[k], rsem.at[k], (px, py, c)))
                sends[-1].start()
        fwds = []
        for w in range(n):
            for d in (1, 2, 3):
                px, py = _other_chip(x, y, d)
                k = 3 * w + d - 1
                got = outs[w].at[2 * px + py, mine]
                _rcopy(got, got, ssem.at[k], rsem.at[k], (px, py, c)).wait_recv()
                fwds.append(_rcopy(got, got, fssem.at[k], frsem.at[k], sibling))
                fwds[-1].start()
        for w in range(n):
            for d in (1, 2, 3):
                px, py = _other_chip(x, y, d)
                k = 3 * w + d - 1
                got = outs[w].at[2 * px + py, other]
                _rcopy(got, got, fssem.at[k], frsem.at[k], sibling).wait_recv()
        for cp in sends + fwds:
            cp.wait_send()
        for cp in local:
            cp.wait()

    sems = [pltpu.SemaphoreType.DMA((3 * n,))] * 4 + [pltpu.SemaphoreType.DMA((n,))]
    return pl.pallas_call(
        body, name="all_gather_weights", in_specs=[ANY] * n, out_specs=[ANY] * n,
        out_shape=[jax.ShapeDtypeStruct((N_CHIPS,) + s.shape, s.dtype) for s in shards], scratch_shapes=sems,
    )(*shards)


def _sibling_exchange(gs):
    n = len(gs)
    Lh = gs[0].shape[1] // 2

    def body(*refs):
        ins, outs = refs[:n], refs[n:2 * n]
        ssem, rsem = refs[2 * n:]
        x, y, c = _place()
        other = pl.ds((1 - c) * Lh, Lh)
        cps = [_rcopy(ins[w].at[:, other], outs[w], ssem.at[w], rsem.at[w], (x, y, 1 - c)) for w in range(n)]
        for cp in cps:
            cp.start()
        for cp in cps:
            cp.wait_recv()
        for cp in cps:
            cp.wait_send()

    return pl.pallas_call(
        body, name="grad_sibling_exchange", in_specs=[ANY] * n, out_specs=[ANY] * n,
        out_shape=[jax.ShapeDtypeStruct((N_CHIPS, Lh) + g.shape[2:], g.dtype) for g in gs],
        scratch_shapes=[pltpu.SemaphoreType.DMA((n,))] * 2,
    )(*gs)


def _chip_exchange(ps):
    n = len(ps)

    def body(*refs):
        ins, outs = refs[:n], refs[n:2 * n]
        ssem, rsem, lsem = refs[2 * n:]
        x, y, c = _place()
        j = 2 * x + y
        local = [pltpu.make_async_copy(ins[w].at[j], outs[w].at[j], lsem.at[w]) for w in range(n)]
        for cp in local:
            cp.start()
        cps = []
        for w in range(n):
            for d in (1, 2, 3):
                px, py = _other_chip(x, y, d)
                k = 3 * w + d - 1
                cps.append(_rcopy(ins[w].at[2 * px + py], outs[w].at[j], ssem.at[k], rsem.at[k], (px, py, c)))
                cps[-1].start()
        for cp in cps:
            cp.wait_recv()
        for cp in cps:
            cp.wait_send()
        for cp in local:
            cp.wait()

    return pl.pallas_call(
        body, name="grad_chip_exchange", in_specs=[ANY] * n, out_specs=[ANY] * n,
        out_shape=[jax.ShapeDtypeStruct(p.shape, p.dtype) for p in ps],
        scratch_shapes=[pltpu.SemaphoreType.DMA((3 * n,))] * 2 + [pltpu.SemaphoreType.DMA((n,))],
    )(*ps)


def _sibling_share(fs):
    n = len(fs)
    Lh = fs[0].shape[0]

    def body(*refs):
        ins, outs = refs[:n], refs[n:2 * n]
        ssem, rsem, lsem = refs[2 * n:]
        x, y, c = _place()
        mine = pl.ds(c * Lh, Lh)
        local = [pltpu.make_async_copy(ins[w], outs[w].at[mine], lsem.at[w]) for w in range(n)]
        for cp in local:
            cp.start()
        cps = [_rcopy(ins[w], outs[w].at[mine], ssem.at[w], rsem.at[w], (x, y, 1 - c)) for w in range(n)]
        for cp in cps:
            cp.start()
        for cp in cps:
            cp.wait_recv()
        for cp in cps:
            cp.wait_send()
        for cp in local:
            cp.wait()

    return pl.pallas_call(
        body, name="grad_sibling_share", in_specs=[ANY] * n, out_specs=[ANY] * n,
        out_shape=[jax.ShapeDtypeStruct((2 * Lh,) + f.shape[1:], f.dtype) for f in fs],
        scratch_shapes=[pltpu.SemaphoreType.DMA((n,))] * 3,
    )(*fs)


def _pair_sum(g, recv, c_idx, name):
    _, _, a, b = g.shape
    Lh = recv.shape[1]
    ta = _tile(a, max(8, (1 << 19) // b // 8 * 8))

    def body(c_ref, g_ref, r_ref, o_ref):
        o_ref[...] = g_ref[...] + r_ref[...]

    blk = (None, None, ta, b)
    return pl.pallas_call(
        body, name=name,
        grid_spec=pltpu.PrefetchScalarGridSpec(
            num_scalar_prefetch=1, grid=(N_CHIPS, Lh, a // ta),
            in_specs=[pl.BlockSpec(blk, lambda j, l, r, c: (j, c[0] * Lh + l, r, 0)),
                      pl.BlockSpec(blk, lambda j, l, r, c: (j, l, r, 0))],
            out_specs=pl.BlockSpec(blk, lambda j, l, r, c: (j, l, r, 0))),
        out_shape=jax.ShapeDtypeStruct(recv.shape, F32),
        compiler_params=_params("parallel", "parallel", "parallel"),
    )(c_idx, g, recv)


def _chip_sum(slots, name):
    _, Lh, a, b = slots.shape
    ta = _tile(a, max(8, (1 << 19) // b // 8 * 8))

    def body(s_ref, o_ref):
        o_ref[...] = ((s_ref[0] + s_ref[1]) + s_ref[2]) + s_ref[3]

    return pl.pallas_call(
        body, name=name, grid=(Lh, a // ta),
        in_specs=[pl.BlockSpec((N_CHIPS, None, ta, b), lambda l, r: (0, l, r, 0))],
        out_specs=pl.BlockSpec((None, ta, b), lambda l, r: (l, r, 0)),
        out_shape=jax.ShapeDtypeStruct((Lh, a, b), F32), compiler_params=_params("parallel", "parallel"),
    )(slots)


def _all_reduce_small(v):
    Rr = v.shape[0]

    def body(v_ref, slots_ref, out_ref, ssem, rsem, lsem):
        x, y, c = _place()
        me = 4 * x + 2 * y + c
        local = pltpu.make_async_copy(v_ref, slots_ref.at[me], lsem)
        local.start()
        cps = []
        for d in range(1, N_DEV):
            px, py = _other_chip(x, y, d >> 1)
            pc = 1 - c if d & 1 else c
            cps.append(_rcopy(v_ref, slots_ref.at[me], ssem.at[d - 1], rsem.at[d - 1], (px, py, pc)))
            cps[-1].start()
        for cp in cps:
            cp.wait_recv()
        for cp in cps:
            cp.wait_send()
        local.wait()
        acc = slots_ref[0]
        for s in range(1, N_DEV):
            acc = acc + slots_ref[s]
        out_ref[...] = acc

    vm = pl.BlockSpec(memory_space=pltpu.VMEM)
    return pl.pallas_call(
        body, name="all_reduce_small", in_specs=[vm], out_specs=[vm, vm],
        out_shape=[jax.ShapeDtypeStruct((N_DEV, Rr, LANES), F32), jax.ShapeDtypeStruct((Rr, LANES), F32)],
        scratch_shapes=[pltpu.SemaphoreType.DMA((N_DEV - 1,)), pltpu.SemaphoreType.DMA((N_DEV - 1,)),
                        pltpu.SemaphoreType.DMA],
        compiler_params=pltpu.CompilerParams(vmem_limit_bytes=VMEM_LIMIT_BYTES),
    )(v)[1]


def _ffn_forward(x, ln, wgu, wd, tag):
    h = _norm_fwd(x, ln, f"{tag}_norm")
    gu, act = _ffn_up(h, wgu, f"{tag}_up")
    x_out = _mm_nn(act, wd, f"{tag}_down", F32, res=x, scale=0.5)
    return x_out, (x, h, gu)


def _ffn_backward(dxo, saved, ln, wgu, wd, tag):
    x, h, gu = saved
    dgu, act = _ffn_down_bwd(dxo, wd, gu, f"{tag}_down_bwd")
    d_wd = _mm_tn(act, dxo, f"{tag}_dwd", scale=0.5)
    d_wgu = _mm_tn(h, dgu, f"{tag}_dwgu", tn_target=1408, tm_target=1024)
    dh = _mm_nt_gu(dgu, wgu, f"{tag}_dh")
    dx, d_ln = _norm_bwd(dh, x, ln, dxo, f"{tag}_norm_bwd")
    return dx, d_ln, d_wgu, d_wd


def _mixer_forward(x, p, tabs):
    h = _norm_fwd(x, p["ln_mix"], "mix_norm")
    zu, zqk, zv, zg = _mm_in(h, p["w_in"], "mix_in")
    pm = _pool_fwd(zu, p["pool_w"], p["pool_scale"], "pool_fwd")
    qkn = _qk_fwd(zqk, p["gqk"], *tabs, "qk_fwd")
    o = _attn_fwd(qkn, zv, p["sinks"], "attn_fwd")
    a, b, m = _merge_fwd(pm, o, p["w_pool_branch"], p["w_attn_branch"], zg, "merge_fwd")
    x_out = _mm_nn(m, p["w_out"], "mix_out", F32, res=x, scale=1.0)
    return x_out, (x, h, zu, zqk, zv, zg, pm, qkn, o, a, b, m)


def _shift_up(v):
    return jnp.concatenate([v[BLOCK:], jnp.zeros((BLOCK, v.shape[1]), v.dtype)], axis=0)


def _mixer_backward(dxo, saved, p, tabs):
    x, h, zu, zqk, zv, zg, pm, qkn, o, a, b, m = saved
    g = {}
    d_a, d_b, dgl = _merge_bwd(dxo, p["w_out"], a, b, zg, "merge_bwd")
    g["w_out"] = _mm_tn(m, dxo, "mix_dwout")
    dpm = _mm_nt(d_a, p["w_pool_branch"], "pool_branch_dx", CDT)
    g["w_pool_branch"] = _mm_tn(pm, d_a, "pool_branch_dw")
    do = _mm_nt(d_b, p["w_attn_branch"], "attn_branch_dx", CDT)
    g["w_attn_branch"] = _mm_tn(o, d_b, "attn_branch_dw")
    du, g["pool_w"], g["pool_scale"] = _pool_bwd(zu, dpm, p["pool_w"], p["pool_scale"], "pool_bwd")
    dq, dkc, dkp, dvc, dvp, dsink = _attn_bwd(qkn, zv, p["sinks"], do, "attn_bwd")
    dqk = jnp.concatenate([dq, dkc + _shift_up(dkp)], axis=1)
    dv = dvc + _shift_up(dvp)
    dzqk, dgqk = _qk_bwd(dqk, zqk, p["gqk"], *tabs, "qk_bwd")
    g["q_norm"] = dgqk[0, :ATTN_DIM].reshape(N_Q_HEADS, HEAD_DIM).sum(axis=0)
    g["k_norm"] = dgqk[0, ATTN_DIM:].reshape(KV_DIM // HEAD_DIM, HEAD_DIM).sum(axis=0)
    g["sinks"] = -dsink[:, 0]
    dz = jnp.concatenate([du, dzqk, dv.astype(CDT), dgl], axis=1)
    g["w_in"] = _mm_tn(h, dz, "mix_dwin")
    dh = _mm_nt(dz, p["w_in"], "mix_in_dx", F32)
    dx, d_ln = _norm_bwd(dh, x, p["ln_mix"], dxo, "mix_norm_bwd")
    g["ln_mix"] = d_ln[0]
    return dx, g


def _local_step(x, tgt, layers):
    T = x.shape[0]
    tabs = _rope_tables(T)
    saved = []
    for p in layers:
        x, s1 = _ffn_forward(x, p["ln_ffn1"], p["w_ffn1_gu"], p["w_ffn1_down"], "ffn1")
        x, s2 = _mixer_forward(x, p, tabs)
        x, s3 = _ffn_forward(x, p["ln_ffn2"], p["w_ffn2_gu"], p["w_ffn2_down"], "ffn2")
        saved.append((s1, s2, s3))
    dx, loss = _loss_head(x, tgt, "loss_head")
    grads = [None] * len(layers)
    for l in reversed(range(len(layers))):
        p = layers[l]
        s1, s2, s3 = saved[l]
        dx, d_ln2, d_gu2, d_dn2 = _ffn_backward(dx, s3, p["ln_ffn2"], p["w_ffn2_gu"], p["w_ffn2_down"], "ffn2")
        dx, g = _mixer_backward(dx, s2, p, tabs)
        dx, d_ln1, d_gu1, d_dn1 = _ffn_backward(dx, s1, p["ln_ffn1"], p["w_ffn1_gu"], p["w_ffn1_down"], "ffn1")
        g.update(ln_ffn1=d_ln1[0], w_ffn1_gu=d_gu1, w_ffn1_down=d_dn1, ln_ffn2=d_ln2[0], w_ffn2_gu=d_gu2, w_ffn2_down=d_dn2)
        grads[l] = g
    return loss, dx, grads


def _full_from_gathered(name, gathered, l):
    blk = gathered[:, l]
    if name in COL_SHARDED:
        return jnp.transpose(blk, (1, 0, 2)).reshape(blk.shape[1], N_CHIPS * blk.shape[2])
    return blk.reshape(N_CHIPS * blk.shape[1], blk.shape[2])


def _shard_major(name, per_layer):
    full = jnp.stack(per_layer)
    L, K, N = full.shape
    if name in COL_SHARDED:
        return jnp.transpose(full.reshape(L, K, N_CHIPS, N // N_CHIPS), (2, 0, 1, 3))
    return jnp.transpose(full.reshape(L, N_CHIPS, K // N_CHIPS, N), (1, 0, 2, 3))


def _pack_small(parts):
    rows, spans, lo = [], [], 0
    for v in parts:
        flat = v.reshape(-1)
        nrow = -(-flat.shape[0] // LANES)
        flat = jnp.pad(flat, (0, nrow * LANES - flat.shape[0]))
        rows.append(flat.reshape(nrow, LANES))
        spans.append((lo, nrow))
        lo += nrow
    pad = -lo % 8
    if pad:
        rows.append(jnp.zeros((pad, LANES), F32))
    return jnp.concatenate(rows, axis=0), spans


def _unpack_small(packed, spans, shapes):
    out = []
    for (lo, nrow), shape in zip(spans, shapes):
        size = 1
        for s in shape:
            size *= s
        out.append(packed[lo:lo + nrow].reshape(-1)[:size].reshape(shape))
    return out


def kernel(x, ln_ffn1, w_ffn1_gu, w_ffn1_down, ln_mix, w_in, pool_w, pool_scale, w_pool_branch, q_norm, k_norm, sinks, w_attn_branch, w_out, ln_ffn2, w_ffn2_gu, w_ffn2_down, loss_target, m_ln_ffn1, m_w_ffn1_gu, m_w_ffn1_down, m_ln_mix, m_w_in, m_pool_w, m_pool_scale, m_w_pool_branch, m_q_norm, m_k_norm, m_sinks, m_w_attn_branch, m_w_out, m_ln_ffn2, m_w_ffn2_gu, m_w_ffn2_down, v_ln_ffn1, v_w_ffn1_gu, v_w_ffn1_down, v_ln_mix, v_w_in, v_pool_w, v_pool_scale, v_w_pool_branch, v_q_norm, v_k_norm, v_sinks, v_w_attn_branch, v_w_out, v_ln_ffn2, v_w_ffn2_gu, v_w_ffn2_down):
    w = dict(ln_ffn1=ln_ffn1, w_ffn1_gu=w_ffn1_gu, w_ffn1_down=w_ffn1_down, ln_mix=ln_mix, w_in=w_in, pool_w=pool_w,
             pool_scale=pool_scale, w_pool_branch=w_pool_branch, q_norm=q_norm, k_norm=k_norm, sinks=sinks,
             w_attn_branch=w_attn_branch, w_out=w_out, ln_ffn2=ln_ffn2, w_ffn2_gu=w_ffn2_gu, w_ffn2_down=w_ffn2_down)
    mom = dict(ln_ffn1=m_ln_ffn1, w_ffn1_gu=m_w_ffn1_gu, w_ffn1_down=m_w_ffn1_down, ln_mix=m_ln_mix, w_in=m_w_in,
               pool_w=m_pool_w, pool_scale=m_pool_scale, w_pool_branch=m_w_pool_branch, q_norm=m_q_norm, k_norm=m_k_norm,
               sinks=m_sinks, w_attn_branch=m_w_attn_branch, w_out=m_w_out, ln_ffn2=m_ln_ffn2, w_ffn2_gu=m_w_ffn2_gu,
               w_ffn2_down=m_w_ffn2_down)
    var = dict(ln_ffn1=v_ln_ffn1, w_ffn1_gu=v_w_ffn1_gu, w_ffn1_down=v_w_ffn1_down, ln_mix=v_ln_mix, w_in=v_w_in,
               pool_w=v_pool_w, pool_scale=v_pool_scale, w_pool_branch=v_w_pool_branch, q_norm=v_q_norm, k_norm=v_k_norm,
               sinks=v_sinks, w_attn_branch=v_w_attn_branch, w_out=v_w_out, ln_ffn2=v_ln_ffn2, w_ffn2_gu=v_w_ffn2_gu,
               w_ffn2_down=v_w_ffn2_down)
    L = ln_ffn1.shape[0]

    gathered = dict(zip(BIG, _all_gather_weights([w[n].astype(CDT) for n in BIG])))
    layers = []
    for l in range(L):
        p = {n: _full_from_gathered(n, gathered[n], l) for n in BIG}
        p.update(ln_ffn1=ln_ffn1[l], ln_mix=ln_mix[l], ln_ffn2=ln_ffn2[l], pool_w=pool_w[l].astype(CDT),
                 pool_scale=pool_scale[l], sinks=sinks[l],
                 gqk=jnp.concatenate([jnp.tile(q_norm[l], N_Q_HEADS), jnp.tile(k_norm[l], KV_DIM // HEAD_DIM)]).reshape(1, QK_DIM))
        layers.append(p)

    loss_part, grad_x, grads = _local_step(x[0], loss_target[0], layers)

    c_idx = lax.axis_index("c").astype(jnp.int32).reshape(1)
    g_major = [_shard_major(n, [g[n] for g in grads]) for n in BIG]
    from_sibling = _sibling_exchange(g_major)
    chip_part = [_pair_sum(g, r, c_idx, "grad_pair_sum") for g, r in zip(g_major, from_sibling)]
    slots = _chip_exchange(chip_part)
    reduced_half = [_chip_sum(s, "grad_chip_sum") for s in slots]
    g_big = dict(zip(BIG, _sibling_share(reduced_half)))

    small_parts = [jnp.stack([g[n] for g in grads]) for n in SMALL] + [loss_part]
    packed, spans = _pack_small(small_parts)
    summed = _all_reduce_small(packed)
    *g_small_list, loss_sum = _unpack_small(summed, spans, [w[n].shape for n in SMALL] + [(1, 1)])
    g_small = dict(zip(SMALL, g_small_list))
    loss = loss_sum[0, 0]

    grad_out, delta, new_m, new_v = {}, {}, {}, {}
    for n in BIG:
        shape = w[n].shape
        flat = (shape[0] * shape[1], shape[2])
        grad_out[n] = g_big[n]
        d, nm, nv = _adamw(w[n].reshape(flat), g_big[n].reshape(flat), mom[n].reshape(flat), var[n].reshape(flat), "adamw")
        delta[n], new_m[n], new_v[n] = d.reshape(shape), nm.reshape(shape), nv.reshape(shape)
    pw, _ = _pack_small([w[n] for n in SMALL])
    pg, sp = _pack_small([g_small[n] for n in SMALL])
    pm_, _ = _pack_small([mom[n] for n in SMALL])
    pv, _ = _pack_small([var[n] for n in SMALL])
    d, nm, nv = _adamw(pw, pg, pm_, pv, "adamw_small")
    shapes = [w[n].shape for n in SMALL]
    for n, dv, mv, vv in zip(SMALL, _unpack_small(d, sp, shapes), _unpack_small(nm, sp, shapes), _unpack_small(nv, sp, shapes)):
        grad_out[n], delta[n], new_m[n], new_v[n] = g_small[n], dv, mv, vv

    return (loss, grad_x[None], *[grad_out[n] for n in WEIGHTS], *[delta[n] for n in WEIGHTS],
            *[new_m[n] for n in WEIGHTS], *[new_v[n] for n in WEIGHTS])
```

```python
import functools

import jax
import jax.numpy as jnp
from jax import lax
from jax.experimental import pallas as pl
from jax.experimental.pallas import tpu as pltpu

F32 = jnp.float32
CDT = jnp.bfloat16
WIRE_DT = jnp.bfloat16

D_MODEL = 1024
POOL_WINDOWS = (2, 4, 8, 16)
POOL_WMAX = 16
GROUP = 128
POOL_DIM = 512
HEAD_DIM = 64
N_Q_HEADS = 8
ATTN_DIM = 512
KV_DIM = 128
QK_DIM = ATTN_DIM + KV_DIM
GATE_DIM = 2 * D_MODEL
BLOCK = 128
ROPE_THETA = 500000.0
ROT_DIM = 16
EPS = 1e-6
ATTN_SCALE = HEAD_DIM ** -0.5

ADAM_LR = 0.001
ADAM_B1 = 0.9
ADAM_B2 = 0.999
ADAM_EPS = 1e-08
ADAM_WD = 0.01
ADAM_STEP = 10

N_CHIPS = 4
N_DEV = 8
LANES = 128
VMEM_LIMIT_BYTES = 48 * 1024 * 1024

MESH = pl.DeviceIdType.MESH
ANY = pl.BlockSpec(memory_space=pl.ANY)

BIG = ("w_ffn1_gu", "w_ffn1_down", "w_in", "w_pool_branch", "w_attn_branch", "w_out", "w_ffn2_gu", "w_ffn2_down")
COL_SHARDED = ("w_ffn1_gu", "w_in", "w_pool_branch", "w_attn_branch", "w_ffn2_gu")
SMALL = ("ln_ffn1", "ln_mix", "pool_w", "pool_scale", "q_norm", "k_norm", "sinks", "ln_ffn2")
WEIGHTS = ("ln_ffn1", "w_ffn1_gu", "w_ffn1_down", "ln_mix", "w_in", "pool_w", "pool_scale", "w_pool_branch",
           "q_norm", "k_norm", "sinks", "w_attn_branch", "w_out", "ln_ffn2", "w_ffn2_gu", "w_ffn2_down")


def _tile(n, target, mult=8):
    if n <= target:
        return n
    for t in range(target - target % mult, 0, -mult):
        if n % t == 0:
            return t
    raise ValueError((n, target, mult))


def _params(*sem):
    return pltpu.CompilerParams(dimension_semantics=sem, vmem_limit_bytes=VMEM_LIMIT_BYTES)


def _sigmoid(v):
    return 1.0 / (1.0 + jnp.exp(-v))


def _dot(a, b):
    return jnp.dot(a, b, preferred_element_type=F32)


def _dot_nt(a, b):
    return lax.dot_general(a, b, (((1,), (1,)), ((), ())), preferred_element_type=F32)


def _dot_tn(a, b):
    return lax.dot_general(a, b, (((0,), (0,)), ((), ())), preferred_element_type=F32)


def _norm_fwd(x, g, name):
    T, Dm = x.shape
    tm = _tile(T, 512)

    def body(x_ref, g_ref, h_ref):
        xv = x_ref[...]
        r = lax.rsqrt(jnp.mean(xv * xv, axis=-1, keepdims=True) + EPS)
        h_ref[...] = (xv * r * g_ref[...]).astype(h_ref.dtype)

    row = pl.BlockSpec((tm, Dm), lambda i: (i, 0))
    return pl.pallas_call(
        body, name=name, grid=(T // tm,),
        in_specs=[row, pl.BlockSpec((1, Dm), lambda i: (0, 0))], out_specs=row,
        out_shape=jax.ShapeDtypeStruct((T, Dm), CDT), compiler_params=_params("parallel"),
    )(x, g.reshape(1, Dm))


def _norm_bwd(dh, x, g, dres, name):
    T, Dm = x.shape
    tm = _tile(T, 512)

    def body(dh_ref, x_ref, g_ref, dres_ref, dx_ref, dg_ref):
        @pl.when(pl.program_id(0) == 0)
        def _():
            dg_ref[...] = jnp.zeros_like(dg_ref)

        xv = x_ref[...]
        dhv = dh_ref[...]
        r = lax.rsqrt(jnp.mean(xv * xv, axis=-1, keepdims=True) + EPS)
        xh = xv * r
        dg_ref[...] += jnp.sum(dhv * xh, axis=0, keepdims=True)
        dxh = dhv * g_ref[...]
        dx_ref[...] = dres_ref[...] + r * (dxh - xh * jnp.mean(dxh * xh, axis=-1, keepdims=True))

    row = pl.BlockSpec((tm, Dm), lambda i: (i, 0))
    vec = pl.BlockSpec((1, Dm), lambda i: (0, 0))
    return pl.pallas_call(
        body, name=name, grid=(T // tm,),
        in_specs=[row, row, vec, row], out_specs=[row, vec],
        out_shape=[jax.ShapeDtypeStruct((T, Dm), F32), jax.ShapeDtypeStruct((1, Dm), F32)],
        compiler_params=_params("arbitrary"),
    )(dh, x, g.reshape(1, Dm), dres)


def _loss_head(y, tgt, name):
    T, Dm = y.shape
    tm = _tile(T, 512)

    def body(y_ref, t_ref, dy_ref, loss_ref):
        @pl.when(pl.program_id(0) == 0)
        def _():
            loss_ref[...] = jnp.zeros_like(loss_ref)

        diff = y_ref[...] - t_ref[...]
        dy_ref[...] = diff * (1.0 / Dm)
        part = jnp.sum(jnp.mean(diff * diff, axis=-1, keepdims=True), axis=0, keepdims=True)
        loss_ref[...] += 0.5 * part

    row = pl.BlockSpec((tm, Dm), lambda i: (i, 0))
    one = pl.BlockSpec((1, 1), lambda i: (0, 0))
    return pl.pallas_call(
        body, name=name, grid=(T // tm,),
        in_specs=[row, row], out_specs=[row, one],
        out_shape=[jax.ShapeDtypeStruct((T, Dm), F32), jax.ShapeDtypeStruct((1, 1), F32)],
        compiler_params=_params("arbitrary"),
    )(y, tgt)


def _mm_nn(a, b, name, out_dtype, res=None, scale=1.0, tm_target=512):
    M, K = a.shape
    N = b.shape[1]
    tm = _tile(M, tm_target)

    def body(a_ref, b_ref, *rest):
        acc = _dot(a_ref[...].astype(CDT), b_ref[...])
        if res is None:
            (o_ref,) = rest
        else:
            r_ref, o_ref = rest
            acc = r_ref[...] + scale * acc
        o_ref[...] = acc.astype(o_ref.dtype)

    in_specs = [pl.BlockSpec((tm, K), lambda i: (i, 0)), pl.BlockSpec((K, N), lambda i: (0, 0))]
    args = [a, b]
    if res is not None:
        in_specs.append(pl.BlockSpec((tm, N), lambda i: (i, 0)))
        args.append(res)
    return pl.pallas_call(
        body, name=name, grid=(M // tm,), in_specs=in_specs,
        out_specs=pl.BlockSpec((tm, N), lambda i: (i, 0)),
        out_shape=jax.ShapeDtypeStruct((M, N), out_dtype), compiler_params=_params("parallel"),
    )(*args)


def _mm_nt(a, b, name, out_dtype, tm_target=512):
    M, K = a.shape
    N = b.shape[0]
    tm = _tile(M, tm_target)

    def body(a_ref, b_ref, o_ref):
        o_ref[...] = _dot_nt(a_ref[...].astype(CDT), b_ref[...]).astype(o_ref.dtype)

    return pl.pallas_call(
        body, name=name, grid=(M // tm,),
        in_specs=[pl.BlockSpec((tm, K), lambda i: (i, 0)), pl.BlockSpec((N, K), lambda i: (0, 0))],
        out_specs=pl.BlockSpec((tm, N), lambda i: (i, 0)),
        out_shape=jax.ShapeDtypeStruct((M, N), out_dtype), compiler_params=_params("parallel"),
    )(a, b)


def _mm_tn(x, dy, name, scale=1.0, tn_target=1664, tm_target=1408, tk_target=1024):
    T, M = x.shape
    split = dy.ndim == 3
    Nh = dy.shape[-1]
    N = 2 * Nh if split else Nh
    tm = _tile(M, tm_target, LANES)
    tn = _tile(Nh, tn_target, LANES)
    tk = _tile(T, tk_target)
    nk = T // tk
    njh = Nh // tn

    def body(x_ref, dy_ref, o_ref):
        @pl.when(pl.program_id(2) == 0)
        def _():
            o_ref[...] = jnp.zeros_like(o_ref)

        part = _dot_tn(x_ref[...].astype(CDT), dy_ref[...].astype(CDT))
        o_ref[...] += part if scale == 1.0 else scale * part

    if split:
        dy_spec = pl.BlockSpec((None, tk, tn), lambda i, j, k: (j // njh, k, j % njh))
    else:
        dy_spec = pl.BlockSpec((tk, tn), lambda i, j, k: (k, j))
    return pl.pallas_call(
        body, name=name, grid=(M // tm, N // tn, nk),
        in_specs=[pl.BlockSpec((tk, tm), lambda i, j, k: (k, i)), dy_spec],
        out_specs=pl.BlockSpec((tm, tn), lambda i, j, k: (i, j)),
        out_shape=jax.ShapeDtypeStruct((M, N), F32),
        compiler_params=_params("parallel", "parallel", "arbitrary"),
    )(x, dy)


def _ffn_up(h, wgu, name):
    T, Dm = h.shape
    Fd = wgu.shape[1] // 2
    tm = _tile(T, 512)
    tn = _tile(Fd, 1408, LANES)
    nj = Fd // tn

    def body(h_ref, wg_ref, wu_ref, gu_ref, a_ref):
        hv = h_ref[...]
        g = _dot(hv, wg_ref[...])
        u = _dot(hv, wu_ref[...])
        gu_ref[0] = g.astype(gu_ref.dtype)
        gu_ref[1] = u.astype(gu_ref.dtype)
        a_ref[...] = (g * _sigmoid(g) * u).astype(a_ref.dtype)

    return pl.pallas_call(
        body, name=name, grid=(T // tm, nj),
        in_specs=[pl.BlockSpec((tm, Dm), lambda i, j: (i, 0)),
                  pl.BlockSpec((Dm, tn), lambda i, j: (0, j)),
                  pl.BlockSpec((Dm, tn), lambda i, j: (0, j + nj))],
        out_specs=[pl.BlockSpec((2, tm, tn), lambda i, j: (0, i, j)), pl.BlockSpec((tm, tn), lambda i, j: (i, j))],
        out_shape=[jax.ShapeDtypeStruct((2, T, Fd), CDT), jax.ShapeDtypeStruct((T, Fd), CDT)],
        compiler_params=_params("parallel", "parallel"),
    )(h, wgu, wgu)


def _ffn_down_bwd(dxo, wd, gu, name):
    T, Dm = dxo.shape
    Fd = wd.shape[0]
    tm = _tile(T, 512)
    tn = _tile(Fd, 1408, LANES)

    def body(dx_ref, wd_ref, gu_ref, dgu_ref, a_ref):
        da = 0.5 * _dot_nt(dx_ref[...].astype(CDT), wd_ref[...])
        g = gu_ref[0].astype(F32)
        u = gu_ref[1].astype(F32)
        sg = _sigmoid(g)
        silu = g * sg
        a_ref[...] = (silu * u).astype(a_ref.dtype)
        dgu_ref[0] = (da * u * (sg * (1.0 + g * (1.0 - sg)))).astype(dgu_ref.dtype)
        dgu_ref[1] = (da * silu).astype(dgu_ref.dtype)

    gu_spec = pl.BlockSpec((2, tm, tn), lambda i, j: (0, i, j))
    return pl.pallas_call(
        body, name=name, grid=(T // tm, Fd // tn),
        in_specs=[pl.BlockSpec((tm, Dm), lambda i, j: (i, 0)), pl.BlockSpec((tn, Dm), lambda i, j: (j, 0)), gu_spec],
        out_specs=[gu_spec, pl.BlockSpec((tm, tn), lambda i, j: (i, j))],
        out_shape=[jax.ShapeDtypeStruct((2, T, Fd), CDT), jax.ShapeDtypeStruct((T, Fd), CDT)],
        compiler_params=_params("parallel", "parallel"),
    )(dxo, wd, gu)


def _mm_nt_gu(dgu, wgu, name):
    _, T, Fd = dgu.shape
    Dm = wgu.shape[0]
    tm = _tile(T, 512)
    tk = _tile(Fd, 1408, LANES)
    nkh = Fd // tk
    nk = 2 * nkh

    def body(a_ref, b_ref, o_ref, acc_ref):
        k = pl.program_id(1)

        @pl.when(k == 0)
        def _():
            acc_ref[...] = jnp.zeros_like(acc_ref)

        acc_ref[...] += _dot_nt(a_ref[...], b_ref[...])

        @pl.when(k == nk - 1)
        def _():
            o_ref[...] = acc_ref[...]

    return pl.pallas_call(
        body, name=name, grid=(T // tm, nk),
        in_specs=[pl.BlockSpec((None, tm, tk), lambda i, k: (k // nkh, i, k % nkh)),
                  pl.BlockSpec((Dm, tk), lambda i, k: (0, k))],
        out_specs=pl.BlockSpec((tm, Dm), lambda i, k: (i, 0)),
        out_shape=jax.ShapeDtypeStruct((T, Dm), F32),
        scratch_shapes=[pltpu.VMEM((tm, Dm), F32)],
        compiler_params=_params("parallel", "arbitrary"),
    )(dgu, wgu)


def _mm_in(h, w_in, name):
    T, Dm = h.shape
    tm = _tile(T, 256)
    widths = (POOL_DIM, QK_DIM, KV_DIM, GATE_DIM)

    def body(h_ref, w_ref, *outs):
        z = _dot(h_ref[...], w_ref[...])
        lo = 0
        for o_ref, wd in zip(outs, widths):
            o_ref[...] = z[:, lo:lo + wd]
            lo += wd

    return pl.pallas_call(
        body, name=name, grid=(T // tm,),
        in_specs=[pl.BlockSpec((tm, Dm), lambda i: (i, 0)), pl.BlockSpec(w_in.shape, lambda i: (0, 0))],
        out_specs=[pl.BlockSpec((tm, wd), lambda i: (i, 0)) for wd in widths],
        out_shape=[jax.ShapeDtypeStruct((T, wd), F32) for wd in widths],
        compiler_params=_params("parallel"),
    )(h, w_in)


def _window_mean_minus_token(ext, u, g, w, pos):
    sl = slice(g * GROUP, (g + 1) * GROUP)
    s = ext[:, sl]
    span = 1
    while span < w:
        s = s + pltpu.roll(s, span, axis=0)
        span *= 2
    cnt = jnp.minimum(pos + 1, w).astype(F32)
    return s[POOL_WMAX:, :] / cnt - u[:, sl]


def _pool_fwd(zu, pool_w, scale, name):
    T = zu.shape[0]
    tm = _tile(T, 512, POOL_WMAX)
    hb = tm // POOL_WMAX

    def body(u_ref, halo_ref, pw_ref, sc_ref, pm_ref):
        i = pl.program_id(0)
        u = u_ref[...]
        halo = jnp.where(i > 0, halo_ref[...], 0.0)
        ext = jnp.concatenate([halo, u], axis=0)
        pos = i * tm + lax.broadcasted_iota(jnp.int32, (tm, 1), 0)
        ys = []
        for g, w in enumerate(POOL_WINDOWS):
            d = _window_mean_minus_token(ext, u, g, w, pos)
            ys.append(_dot(d.astype(CDT), pw_ref[g]))
        pm_ref[...] = (jnp.concatenate(ys, axis=1) * sc_ref[...]).astype(pm_ref.dtype)

    row = pl.BlockSpec((tm, POOL_DIM), lambda i: (i, 0))
    return pl.pallas_call(
        body, name=name, grid=(T // tm,),
        in_specs=[row, pl.BlockSpec((POOL_WMAX, POOL_DIM), lambda i: (jnp.maximum(i * hb - 1, 0), 0)),
                  pl.BlockSpec(pool_w.shape, lambda i: (0, 0, 0)), pl.BlockSpec((1, POOL_DIM), lambda i: (0, 0))],
        out_specs=row, out_shape=jax.ShapeDtypeStruct((T, POOL_DIM), CDT),
        compiler_params=_params("parallel"),
    )(zu, zu, pool_w, scale.reshape(1, POOL_DIM))


def _pool_bwd(zu, dpm, pool_w, scale, name):
    T = zu.shape[0]
    tm = _tile(T, 512, POOL_WMAX)
    hb = tm // POOL_WMAX
    nsteps = T // tm
    ext_rows = tm + POOL_WMAX

    def body(u_ref, halo_ref, dpm_ref, dnext_ref, pw_ref, sc_ref, du_ref, dpw_ref, dsc_ref):
        i = pl.program_id(0)

        @pl.when(i == 0)
        def _():
            dpw_ref[...] = jnp.zeros_like(dpw_ref)
            dsc_ref[...] = jnp.zeros_like(dsc_ref)

        u = u_ref[...]
        halo = jnp.where(i > 0, halo_ref[...], 0.0)
        ext = jnp.concatenate([halo, u], axis=0)
        dpm_t = dpm_ref[...].astype(F32)
        dnext = jnp.where(i < nsteps - 1, dnext_ref[...].astype(F32), 0.0)
        dext = jnp.concatenate([dpm_t, dnext], axis=0)
        sc = sc_ref[...]
        pos = i * tm + lax.broadcasted_iota(jnp.int32, (tm, 1), 0)
        pos_ext = i * tm + lax.broadcasted_iota(jnp.int32, (ext_rows, 1), 0)
        dus, dscs = [], []
        for g, w in enumerate(POOL_WINDOWS):
            sl = slice(g * GROUP, (g + 1) * GROUP)
            dc = _window_mean_minus_token(ext, u, g, w, pos).astype(CDT)
            y = _dot(dc, pw_ref[g])
            dscs.append(jnp.sum(dpm_t[:, sl] * y, axis=0, keepdims=True))
            dy_ext = (dext[:, sl] * sc[:, sl]).astype(CDT)
            dpw_ref[g] += _dot_tn(dc, dy_ext[:tm])
            dd = _dot_nt(dy_ext, pw_ref[g])
            r = dd / jnp.minimum(pos_ext + 1, w).astype(F32)
            span = 1
            while span < w:
                r = r + pltpu.roll(r, ext_rows - span, axis=0)
                span *= 2
            dus.append(r[:tm] - dd[:tm])
        du_ref[...] = jnp.concatenate(dus, axis=1).astype(du_ref.dtype)
        dsc_ref[...] += jnp.concatenate(dscs, axis=1)

    row = pl.BlockSpec((tm, POOL_DIM), lambda i: (i, 0))
    prev = pl.BlockSpec((POOL_WMAX, POOL_DIM), lambda i: (jnp.maximum(i * hb - 1, 0), 0))
    nxt = pl.BlockSpec((POOL_WMAX, POOL_DIM), lambda i: (jnp.minimum((i + 1) * hb, nsteps * hb - 1), 0))
    return pl.pallas_call(
        body, name=name, grid=(nsteps,),
        in_specs=[row, prev, row, nxt, pl.BlockSpec(pool_w.shape, lambda i: (0, 0, 0)),
                  pl.BlockSpec((1, POOL_DIM), lambda i: (0, 0))],
        out_specs=[row, pl.BlockSpec(pool_w.shape, lambda i: (0, 0, 0)), pl.BlockSpec((1, POOL_DIM), lambda i: (0, 0))],
        out_shape=[jax.ShapeDtypeStruct((T, POOL_DIM), CDT), jax.ShapeDtypeStruct(pool_w.shape, F32),
                   jax.ShapeDtypeStruct((1, POOL_DIM), F32)],
        compiler_params=_params("arbitrary"),
    )(zu, zu, dpm, dpm, pool_w, scale.reshape(1, POOL_DIM))


def _rope_tables(T):
    pos = jnp.arange(T, dtype=F32)
    inv_freq = ROPE_THETA ** (-jnp.arange(0, ROT_DIM, 2, dtype=F32) / ROT_DIM)
    ang = pos[:, None] * inv_freq[None, :]
    cos, sin = jnp.cos(ang), jnp.sin(ang)
    rest = HEAD_DIM - ROT_DIM
    cos_h = jnp.concatenate([cos, cos, jnp.ones((T, rest), F32)], axis=1)
    sin_h = jnp.concatenate([-sin, sin, jnp.zeros((T, rest), F32)], axis=1)
    return jnp.tile(cos_h, (1, 2)), jnp.tile(sin_h, (1, 2))


def _lane_masks():
    lane = lax.broadcasted_iota(jnp.int32, (1, LANES), 1)
    in_head = lane % HEAD_DIM
    return lane < HEAD_DIM, in_head < ROT_DIM // 2


def _rope_partner(v, low):
    lane = lax.broadcasted_iota(jnp.int32, (1, LANES), 1)
    swapped = jnp.where(low, pltpu.roll(v, LANES - ROT_DIM // 2, axis=1), pltpu.roll(v, ROT_DIM // 2, axis=1))
    return jnp.where(lane % HEAD_DIM < ROT_DIM, swapped, 0.0)


def _head_mean(v, first):
    lo = jnp.sum(jnp.where(first, v, 0.0), axis=-1, keepdims=True)
    hi = jnp.sum(jnp.where(first, 0.0, v), axis=-1, keepdims=True)
    return jnp.where(first, lo, hi) * (1.0 / HEAD_DIM)


def _qk_fwd(zqk, gqk, cos_t, sin_t, name):
    T = zqk.shape[0]
    tm = _tile(T, 512)

    def body(z_ref, g_ref, c_ref, s_ref, o_ref):
        first, low = _lane_masks()
        cosv, sinv = c_ref[...], s_ref[...]
        for c in range(QK_DIM // LANES):
            sl = slice(c * LANES, (c + 1) * LANES)
            xv = z_ref[:, sl]
            r = lax.rsqrt(_head_mean(xv * xv, first) + EPS)
            xn = xv * r * g_ref[:, sl]
            o_ref[:, sl] = (xn * cosv + _rope_partner(xn, low) * sinv).astype(o_ref.dtype)

    row = pl.BlockSpec((tm, QK_DIM), lambda i: (i, 0))
    tab = pl.BlockSpec((tm, LANES), lambda i: (i, 0))
    return pl.pallas_call(
        body, name=name, grid=(T // tm,),
        in_specs=[row, pl.BlockSpec((1, QK_DIM), lambda i: (0, 0)), tab, tab], out_specs=row,
        out_shape=jax.ShapeDtypeStruct((T, QK_DIM), CDT), compiler_params=_params("parallel"),
    )(zqk, gqk, cos_t, sin_t)


def _qk_bwd(dqk, zqk, gqk, cos_t, sin_t, name):
    T = zqk.shape[0]
    tm = _tile(T, 512)

    def body(d_ref, z_ref, g_ref, c_ref, s_ref, dz_ref, dg_ref):
        @pl.when(pl.program_id(0) == 0)
        def _():
            dg_ref[...] = jnp.zeros_like(dg_ref)

        first, low = _lane_masks()
        cosv, sinv = c_ref[...], s_ref[...]
        dgs = []
        for c in range(QK_DIM // LANES):
            sl = slice(c * LANES, (c + 1) * LANES)
            dout = d_ref[:, sl]
            dxn = dout * cosv + _rope_partner(dout * sinv, low)
            xv = z_ref[:, sl]
            r = lax.rsqrt(_head_mean(xv * xv, first) + EPS)
            xh = xv * r
            dgs.append(jnp.sum(dxn * xh, axis=0, keepdims=True))
            dxh = dxn * g_ref[:, sl]
            dz_ref[:, sl] = (r * (dxh - xh * _head_mean(dxh * xh, first))).astype(dz_ref.dtype)
        dg_ref[...] += jnp.concatenate(dgs, axis=1)

    row = pl.BlockSpec((tm, QK_DIM), lambda i: (i, 0))
    tab = pl.BlockSpec((tm, LANES), lambda i: (i, 0))
    vec = pl.BlockSpec((1, QK_DIM), lambda i: (0, 0))
    return pl.pallas_call(
        body, name=name, grid=(T // tm,),
        in_specs=[row, row, vec, tab, tab], out_specs=[row, vec],
        out_shape=[jax.ShapeDtypeStruct((T, QK_DIM), CDT), jax.ShapeDtypeStruct((1, QK_DIM), F32)],
        compiler_params=_params("arbitrary"),
    )(dqk, zqk, gqk, cos_t, sin_t)


def _dup_half(v, first, kv):
    swapped = pltpu.roll(v, HEAD_DIM, axis=1)
    return jnp.where(first, v, swapped) if kv == 0 else jnp.where(first, swapped, v)


def _band_mask(first_key):
    qi = lax.broadcasted_iota(jnp.int32, (BLOCK, 2 * BLOCK), 0)
    ki = lax.broadcasted_iota(jnp.int32, (BLOCK, 2 * BLOCK), 1)
    diff = qi + BLOCK - ki
    return (diff >= 0) & (diff < BLOCK) & (ki >= first_key)


def _attn_blocks(T):
    return _tile(T // BLOCK, 4, 1)


def _attn_fwd(qkn, zv, sinks, name):
    T = qkn.shape[0]
    R = _attn_blocks(T)
    tq = R * BLOCK

    def body(sink_ref, qk_ref, qkp_ref, v_ref, vp_ref, o_ref):
        i = pl.program_id(0)
        first, _ = _lane_masks()
        kall = jnp.concatenate([qkp_ref[:, ATTN_DIM:], qk_ref[:, ATTN_DIM:]], axis=0)
        vall = jnp.concatenate([vp_ref[...], v_ref[...]], axis=0).astype(CDT)
        for r in range(R):
            mask = _band_mask(jnp.where(i == 0, BLOCK, 0)) if r == 0 else _band_mask(0)
            rows = slice(r * BLOCK, (r + 2) * BLOCK)
            kdup = [_dup_half(kall[rows], first, kv) for kv in range(2)]
            vdup = [_dup_half(vall[rows], first, kv) for kv in range(2)]
            for c in range(ATTN_DIM // LANES):
                kv = c // 2
                qc = qk_ref[r * BLOCK:(r + 1) * BLOCK, c * LANES:(c + 1) * LANES]
                halves = []
                for e in range(2):
                    qm = jnp.where(first, qc, jnp.zeros_like(qc)) if e == 0 else jnp.where(first, jnp.zeros_like(qc), qc)
                    s = _dot_nt(qm, kdup[kv]) * ATTN_SCALE
                    s = jnp.where(mask, s, -jnp.inf)
                    sink = sink_ref[2 * c + e]
                    m = jnp.maximum(jnp.max(s, axis=-1, keepdims=True), sink)
                    p = jnp.exp(s - m)
                    denom = jnp.sum(p, axis=-1, keepdims=True) + jnp.exp(sink - m)
                    halves.append(_dot((p / denom).astype(CDT), vdup[kv]))
                o_ref[r * BLOCK:(r + 1) * BLOCK, c * LANES:(c + 1) * LANES] = jnp.where(first, halves[0], halves[1]).astype(o_ref.dtype)

    prev = lambda i: (jnp.maximum(i * R - 1, 0), 0)
    return pl.pallas_call(
        body, name=name, grid=(T // tq,),
        in_specs=[pl.BlockSpec(memory_space=pltpu.SMEM),
                  pl.BlockSpec((tq, QK_DIM), lambda i: (i, 0)), pl.BlockSpec((BLOCK, QK_DIM), prev),
                  pl.BlockSpec((tq, KV_DIM), lambda i: (i, 0)), pl.BlockSpec((BLOCK, KV_DIM), prev)],
        out_specs=pl.BlockSpec((tq, ATTN_DIM), lambda i: (i, 0)),
        out_shape=jax.ShapeDtypeStruct((T, ATTN_DIM), CDT), compiler_params=_params("parallel"),
    )(sinks, qkn, qkn, zv, zv)


def _attn_bwd(qkn, zv, sinks, do, name):
    T = qkn.shape[0]
    R = _attn_blocks(T)
    tq = R * BLOCK

    def body(sink_ref, qk_ref, qkp_ref, v_ref, vp_ref, do_ref, dq_ref, dkc_ref, dkp_ref, dvc_ref, dvp_ref, ds_ref):
        i = pl.program_id(0)

        @pl.when(i == 0)
        def _():
            ds_ref[...] = jnp.zeros_like(ds_ref)

        first, _ = _lane_masks()
        kall = jnp.concatenate([qkp_ref[:, ATTN_DIM:], qk_ref[:, ATTN_DIM:]], axis=0)
        vall = jnp.concatenate([vp_ref[...], v_ref[...]], axis=0).astype(CDT)
        for r in range(R):
            mask = _band_mask(jnp.where(i == 0, BLOCK, 0)) if r == 0 else _band_mask(0)
            rows = slice(r * BLOCK, (r + 2) * BLOCK)
            qrows = slice(r * BLOCK, (r + 1) * BLOCK)
            dk_out, dv_out = [], []
            for kv in range(2):
                kdup = _dup_half(kall[rows], first, kv)
                vdup = _dup_half(vall[rows], first, kv)
                dk_acc = jnp.zeros((2 * BLOCK, LANES), F32)
                dv_acc = jnp.zeros((2 * BLOCK, LANES), F32)
                for c in (2 * kv, 2 * kv + 1):
                    cols = slice(c * LANES, (c + 1) * LANES)
                    qc = qk_ref[qrows, cols]
                    doc = do_ref[qrows, cols].astype(CDT)
                    dq_halves = []
                    for e in range(2):
                        zq, zd = jnp.zeros_like(qc), jnp.zeros_like(doc)
                        qm = jnp.where(first, qc, zq) if e == 0 else jnp.where(first, zq, qc)
                        dom = jnp.where(first, doc, zd) if e == 0 else jnp.where(first, zd, doc)
                        s = _dot_nt(qm, kdup) * ATTN_SCALE
                        s = jnp.where(mask, s, -jnp.inf)
                        sink = sink_ref[2 * c + e]
                        m = jnp.maximum(jnp.max(s, axis=-1, keepdims=True), sink)
                        pu = jnp.exp(s - m)
                        es = jnp.exp(sink - m)
                        inv = 1.0 / (jnp.sum(pu, axis=-1, keepdims=True) + es)
                        p = pu * inv
                        dp = _dot_nt(dom, vdup)
                        delta = jnp.sum(p * dp, axis=-1, keepdims=True)
                        dsc = (p * (dp - delta)).astype(CDT)
                        h = 2 * c + e
                        ds_ref[h:h + 1, :] += jnp.sum(es * inv * delta, axis=0, keepdims=True)
                        dq_halves.append(_dot(dsc, kdup) * ATTN_SCALE)
                        dk_acc = dk_acc + _dot_tn(dsc, qm) * ATTN_SCALE
                        dv_acc = dv_acc + _dot_tn(p.astype(CDT), dom)
                    dq_ref[qrows, cols] = jnp.where(first, dq_halves[0], dq_halves[1])
                dk_out.append(dk_acc + pltpu.roll(dk_acc, HEAD_DIM, axis=1))
                dv_out.append(dv_acc + pltpu.roll(dv_acc, HEAD_DIM, axis=1))
            dk = jnp.where(first, dk_out[0], dk_out[1])
            dv = jnp.where(first, dv_out[0], dv_out[1])
            dkp_ref[qrows, :] = dk[:BLOCK]
            dkc_ref[qrows, :] = dk[BLOCK:]
            dvp_ref[qrows, :] = dv[:BLOCK]
            dvc_ref[qrows, :] = dv[BLOCK:]

    prev = lambda i: (jnp.maximum(i * R - 1, 0), 0)
    kvrow = pl.BlockSpec((tq, KV_DIM), lambda i: (i, 0))
    qrow = pl.BlockSpec((tq, ATTN_DIM), lambda i: (i, 0))
    kv_shape = jax.ShapeDtypeStruct((T, KV_DIM), F32)
    return pl.pallas_call(
        body, name=name, grid=(T // tq,),
        in_specs=[pl.BlockSpec(memory_space=pltpu.SMEM),
                  pl.BlockSpec((tq, QK_DIM), lambda i: (i, 0)), pl.BlockSpec((BLOCK, QK_DIM), prev),
                  kvrow, pl.BlockSpec((BLOCK, KV_DIM), prev), qrow],
        out_specs=[qrow, kvrow, kvrow, kvrow, kvrow, pl.BlockSpec((N_Q_HEADS, LANES), lambda i: (0, 0))],
        out_shape=[jax.ShapeDtypeStruct((T, ATTN_DIM), F32), kv_shape, kv_shape, kv_shape, kv_shape,
                   jax.ShapeDtypeStruct((N_Q_HEADS, LANES), F32)],
        compiler_params=_params("arbitrary"),
    )(sinks, qkn, qkn, zv, zv, do)


def _merge_fwd(pm, o, w_pb, w_ab, zg, name):
    T = pm.shape[0]
    tm = _tile(T, 512)

    def body(pm_ref, o_ref, wp_ref, wa_ref, zg_ref, a_ref, b_ref, m_ref):
        a = _dot(pm_ref[...], wp_ref[...])
        b = _dot(o_ref[...], wa_ref[...])
        gp = _sigmoid(zg_ref[:, :D_MODEL])
        ga = _sigmoid(zg_ref[:, D_MODEL:])
        a_ref[...] = a.astype(a_ref.dtype)
        b_ref[...] = b.astype(b_ref.dtype)
        m_ref[...] = (gp * a + ga * b).astype(m_ref.dtype)

    half = pl.BlockSpec((tm, POOL_DIM), lambda i: (i, 0))
    full = pl.BlockSpec((tm, D_MODEL), lambda i: (i, 0))
    wspec = pl.BlockSpec((POOL_DIM, D_MODEL), lambda i: (0, 0))
    out = jax.ShapeDtypeStruct((T, D_MODEL), CDT)
    return pl.pallas_call(
        body, name=name, grid=(T // tm,),
        in_specs=[half, half, wspec, wspec, pl.BlockSpec((tm, GATE_DIM), lambda i: (i, 0))],
        out_specs=[full, full, full], out_shape=[out, out, out], compiler_params=_params("parallel"),
    )(pm, o, w_pb, w_ab, zg)


def _merge_bwd(dxo, w_out, a, b, zg, name):
    T = dxo.shape[0]
    tm = _tile(T, 512)

    def body(dx_ref, w_ref, a_ref, b_ref, zg_ref, da_ref, db_ref, dg_ref):
        dm = _dot_nt(dx_ref[...].astype(CDT), w_ref[...])
        gp = _sigmoid(zg_ref[:, :D_MODEL])
        ga = _sigmoid(zg_ref[:, D_MODEL:])
        da_ref[...] = (dm * gp).astype(da_ref.dtype)
        db_ref[...] = (dm * ga).astype(db_ref.dtype)
        dg_ref[:, :D_MODEL] = (dm * a_ref[...].astype(F32) * (gp * (1.0 - gp))).astype(dg_ref.dtype)
        dg_ref[:, D_MODEL:] = (dm * b_ref[...].astype(F32) * (ga * (1.0 - ga))).astype(dg_ref.dtype)

    full = pl.BlockSpec((tm, D_MODEL), lambda i: (i, 0))
    gate = pl.BlockSpec((tm, GATE_DIM), lambda i: (i, 0))
    out = jax.ShapeDtypeStruct((T, D_MODEL), CDT)
    return pl.pallas_call(
        body, name=name, grid=(T // tm,),
        in_specs=[full, pl.BlockSpec((D_MODEL, D_MODEL), lambda i: (0, 0)), full, full, gate],
        out_specs=[full, full, gate], out_shape=[out, out, jax.ShapeDtypeStruct((T, GATE_DIM), CDT)],
        compiler_params=_params("parallel"),
    )(dxo, w_out, a, b, zg)


def _adamw(w, g, m, v, name):
    Rr, C = w.shape
    tr = _tile(Rr, max(8, (1 << 19) // C // 8 * 8))

    def body(w_ref, g_ref, m_ref, v_ref, d_ref, nm_ref, nv_ref):
        gv = g_ref[...]
        nm = ADAM_B1 * m_ref[...] + (1.0 - ADAM_B1) * gv
        nv = ADAM_B2 * v_ref[...] + (1.0 - ADAM_B2) * (gv * gv)
        m_hat = nm / (1.0 - ADAM_B1 ** ADAM_STEP)
        v_hat = nv / (1.0 - ADAM_B2 ** ADAM_STEP)
        d_ref[...] = -ADAM_LR * (m_hat / (jnp.sqrt(v_hat) + ADAM_EPS) + ADAM_WD * w_ref[...])
        nm_ref[...] = nm
        nv_ref[...] = nv

    blk = pl.BlockSpec((tr, C), lambda i: (i, 0))
    out = jax.ShapeDtypeStruct((Rr, C), F32)
    return pl.pallas_call(
        body, name=name, grid=(Rr // tr,), in_specs=[blk] * 4, out_specs=[blk] * 3, out_shape=[out] * 3,
        compiler_params=_params("parallel"),
    )(w, g, m, v)


def _place():
    return lax.axis_index("x"), lax.axis_index("y"), lax.axis_index("c")


def _other_chip(x, y, d):
    return (1 - x if d & 2 else x), (1 - y if d & 1 else y)


def _rcopy(src, dst, ssem, rsem, dev):
    return pltpu.make_async_remote_copy(src_ref=src, dst_ref=dst, send_sem=ssem, recv_sem=rsem, device_id=dev,
                                        device_id_type=MESH)


def _all_gather_weights(shards):
    n = len(shards)
    Lh = shards[0].shape[0] // 2

    def body(*refs):
        ins, outs = refs[:n], refs[n:2 * n]
        ssem, rsem, fssem, frsem, mssem, mrsem = refs[2 * n:]
        x, y, c = _place()
        j = 2 * x + y
        mine, other = pl.ds(c * Lh, Lh), pl.ds((1 - c) * Lh, Lh)
        sibling = (x, y, 1 - c)
        sends = []
        for w in range(n):
            for d in (1, 2, 3):
                px, py = _other_chip(x, y, d)
                k = 3 * w + d - 1
                sends.append(_rcopy(ins[w].at[mine], outs[w].at[j, mine], ssem.at[k], rsem.at[k], (px, py, c)))
                sends[-1].start()
        mirror = [_rcopy(ins[w], outs[w].at[j], mssem.at[w], mrsem.at[w], sibling) for w in range(n)]
        for cp in mirror:
            cp.start()
        fwds = []
        for w in range(n):
            for d in (1, 2, 3):
                px, py = _other_chip(x, y, d)
                k = 3 * w + d - 1
                got = outs[w].at[2 * px + py, mine]
                _rcopy(got, got, ssem.at[k], rsem.at[k], (px, py, c)).wait_recv()
                fwds.append(_rcopy(got, got, fssem.at[k], frsem.at[k], sibling))
                fwds[-1].start()
        for w in range(n):
            for d in (1, 2, 3):
                px, py = _other_chip(x, y, d)
                k = 3 * w + d - 1
                got = outs[w].at[2 * px + py, other]
                _rcopy(got, got, fssem.at[k], frsem.at[k], sibling).wait_recv()
        for cp in mirror:
            cp.wait_recv()
        for cp in sends + fwds + mirror:
            cp.wait_send()

    sems = [pltpu.SemaphoreType.DMA((3 * n,))] * 4 + [pltpu.SemaphoreType.DMA((n,))] * 2
    return pl.pallas_call(
        body, name="all_gather_weights", in_specs=[ANY] * n, out_specs=[ANY] * n,
        out_shape=[jax.ShapeDtypeStruct((N_CHIPS,) + s.shape, s.dtype) for s in shards], scratch_shapes=sems,
    )(*shards)


def _sibling_exchange(gs):
    n = len(gs)
    Lh = gs[0].shape[1] // 2

    def body(*refs):
        ins, outs = refs[:n], refs[n:2 * n]
        ssem, rsem = refs[2 * n:]
        x, y, c = _place()
        other = pl.ds((1 - c) * Lh, Lh)
        cps = [_rcopy(ins[w].at[:, other], outs[w], ssem.at[w], rsem.at[w], (x, y, 1 - c)) for w in range(n)]
        for cp in cps:
            cp.start()
        for cp in cps:
            cp.wait_recv()
        for cp in cps:
            cp.wait_send()

    return pl.pallas_call(
        body, name="grad_sibling_exchange", in_specs=[ANY] * n, out_specs=[ANY] * n,
        out_shape=[jax.ShapeDtypeStruct((N_CHIPS, Lh) + g.shape[2:], g.dtype) for g in gs],
        scratch_shapes=[pltpu.SemaphoreType.DMA((n,))] * 2,
    )(*gs)


def _chip_exchange(ps):
    n = len(ps)

    def body(*refs):
        ins, outs = refs[:n], refs[n:2 * n]
        ssem, rsem = refs[2 * n:]
        x, y, c = _place()
        j = 2 * x + y
        cps = []
        for w in range(n):
            for d in (1, 2, 3):
                px, py = _other_chip(x, y, d)
                k = 3 * w + d - 1
                cps.append(_rcopy(ins[w].at[2 * px + py], outs[w].at[j], ssem.at[k], rsem.at[k], (px, py, c)))
                cps[-1].start()
        for cp in cps:
            cp.wait_recv()
        for cp in cps:
            cp.wait_send()

    return pl.pallas_call(
        body, name="grad_chip_exchange", in_specs=[ANY] * n, out_specs=[ANY] * n,
        out_shape=[jax.ShapeDtypeStruct(p.shape, p.dtype) for p in ps],
        scratch_shapes=[pltpu.SemaphoreType.DMA((3 * n,))] * 2,
    )(*ps)


def _sibling_share(fs):
    n = len(fs)
    Lh = fs[0].shape[0] // 2

    def body(*refs):
        outs = refs[n:2 * n]
        ssem, rsem = refs[2 * n:]
        x, y, c = _place()
        mine = pl.ds(c * Lh, Lh)
        cps = [_rcopy(outs[w].at[mine], outs[w].at[mine], ssem.at[w], rsem.at[w], (x, y, 1 - c)) for w in range(n)]
        for cp in cps:
            cp.start()
        for cp in cps:
            cp.wait_recv()
        for cp in cps:
            cp.wait_send()

    return pl.pallas_call(
        body, name="grad_sibling_share", in_specs=[ANY] * n, out_specs=[ANY] * n,
        out_shape=[jax.ShapeDtypeStruct(f.shape, f.dtype) for f in fs],
        input_output_aliases={w: w for w in range(n)},
        scratch_shapes=[pltpu.SemaphoreType.DMA((n,))] * 2,
    )(*fs)


def _sum_rows(a, b):
    return _tile(a, max(16, (1 << 19) // b // 16 * 16), 16)


def _pair_sum(g, recv, place, name):
    _, _, a, b = g.shape
    Lh = recv.shape[1]
    ta = _sum_rows(a, b)

    def body(p_ref, g_ref, r_ref, o_ref):
        o_ref[...] = (g_ref[...].astype(F32) + r_ref[...].astype(F32)).astype(o_ref.dtype)

    blk = (None, None, ta, b)
    return pl.pallas_call(
        body, name=name,
        grid_spec=pltpu.PrefetchScalarGridSpec(
            num_scalar_prefetch=1, grid=(N_CHIPS, Lh, a // ta),
            in_specs=[pl.BlockSpec(blk, lambda j, l, r, p: (j, p[0] * Lh + l, r, 0)),
                      pl.BlockSpec(blk, lambda j, l, r, p: (j, l, r, 0))],
            out_specs=pl.BlockSpec(blk, lambda j, l, r, p: (j, l, r, 0))),
        out_shape=jax.ShapeDtypeStruct(recv.shape, recv.dtype),
        compiler_params=_params("parallel", "parallel", "parallel"),
    )(place, g, recv)


def _chip_sum(slots, part, place, name):
    _, Lh, a, b = slots.shape
    ta = _sum_rows(a, b)

    def body(p_ref, s_ref, own_ref, o_ref):
        j = p_ref[1]
        own = own_ref[...].astype(F32)
        term = [jnp.where(j == s, own, s_ref[s].astype(F32)) for s in range(N_CHIPS)]
        o_ref[...] = ((term[0] + term[1]) + term[2]) + term[3]

    return pl.pallas_call(
        body, name=name,
        grid_spec=pltpu.PrefetchScalarGridSpec(
            num_scalar_prefetch=1, grid=(Lh, a // ta),
            in_specs=[pl.BlockSpec((N_CHIPS, None, ta, b), lambda l, r, p: (0, l, r, 0)),
                      pl.BlockSpec((None, None, ta, b), lambda l, r, p: (p[1], l, r, 0))],
            out_specs=pl.BlockSpec((None, ta, b), lambda l, r, p: (p[0] * Lh + l, r, 0))),
        out_shape=jax.ShapeDtypeStruct((2 * Lh, a, b), F32),
        compiler_params=_params("parallel", "parallel"),
    )(place, slots, part)


def _all_reduce_small(v):
    Rr = v.shape[0]

    def body(v_ref, slots_ref, out_ref, ssem, rsem):
        x, y, c = _place()
        me = 4 * x + 2 * y + c
        slots_ref[pl.ds(me, 1)] = v_ref[...][None]
        cps = []
        for d in range(1, N_DEV):
            px, py = _other_chip(x, y, d >> 1)
            pc = 1 - c if d & 1 else c
            cps.append(_rcopy(v_ref, slots_ref.at[me], ssem.at[d - 1], rsem.at[d - 1], (px, py, pc)))
            cps[-1].start()
        for cp in cps:
            cp.wait_recv()
        for cp in cps:
            cp.wait_send()
        acc = slots_ref[0]
        for s in range(1, N_DEV):
            acc = acc + slots_ref[s]
        out_ref[...] = acc

    vm = pl.BlockSpec(memory_space=pltpu.VMEM)
    return pl.pallas_call(
        body, name="all_reduce_small", in_specs=[vm], out_specs=[vm, vm],
        out_shape=[jax.ShapeDtypeStruct((N_DEV, Rr, LANES), F32), jax.ShapeDtypeStruct((Rr, LANES), F32)],
        scratch_shapes=[pltpu.SemaphoreType.DMA((N_DEV - 1,)), pltpu.SemaphoreType.DMA((N_DEV - 1,))],
        compiler_params=pltpu.CompilerParams(vmem_limit_bytes=VMEM_LIMIT_BYTES),
    )(v)[1]


def _ffn_forward(x, ln, wgu, wd, tag):
    h = _norm_fwd(x, ln, f"{tag}_norm")
    gu, act = _ffn_up(h, wgu, f"{tag}_up")
    x_out = _mm_nn(act, wd, f"{tag}_down", F32, res=x, scale=0.5)
    return x_out, (x, h, gu)


def _ffn_backward(dxo, saved, ln, wgu, wd, tag):
    x, h, gu = saved
    dgu, act = _ffn_down_bwd(dxo, wd, gu, f"{tag}_down_bwd")
    d_wd = _mm_tn(act, dxo, f"{tag}_dwd", scale=0.5)
    d_wgu = _mm_tn(h, dgu, f"{tag}_dwgu", tn_target=1408, tm_target=1024)
    dh = _mm_nt_gu(dgu, wgu, f"{tag}_dh")
    dx, d_ln = _norm_bwd(dh, x, ln, dxo, f"{tag}_norm_bwd")
    return dx, d_ln, d_wgu, d_wd


def _mixer_forward(x, p, tabs):
    h = _norm_fwd(x, p["ln_mix"], "mix_norm")
    zu, zqk, zv, zg = _mm_in(h, p["w_in"], "mix_in")
    pm = _pool_fwd(zu, p["pool_w"], p["pool_scale"], "pool_fwd")
    qkn = _qk_fwd(zqk, p["gqk"], *tabs, "qk_fwd")
    o = _attn_fwd(qkn, zv, p["sinks"], "attn_fwd")
    a, b, m = _merge_fwd(pm, o, p["w_pool_branch"], p["w_attn_branch"], zg, "merge_fwd")
    x_out = _mm_nn(m, p["w_out"], "mix_out", F32, res=x, scale=1.0)
    return x_out, (x, h, zu, zqk, zv, zg, pm, qkn, o, a, b, m)


def _shift_up(v):
    return jnp.concatenate([v[BLOCK:], jnp.zeros((BLOCK, v.shape[1]), v.dtype)], axis=0)


def _mixer_backward(dxo, saved, p, tabs):
    x, h, zu, zqk, zv, zg, pm, qkn, o, a, b, m = saved
    g = {}
    d_a, d_b, dgl = _merge_bwd(dxo, p["w_out"], a, b, zg, "merge_bwd")
    g["w_out"] = _mm_tn(m, dxo, "mix_dwout")
    dpm = _mm_nt(d_a, p["w_pool_branch"], "pool_branch_dx", CDT)
    g["w_pool_branch"] = _mm_tn(pm, d_a, "pool_branch_dw")
    do = _mm_nt(d_b, p["w_attn_branch"], "attn_branch_dx", CDT)
    g["w_attn_branch"] = _mm_tn(o, d_b, "attn_branch_dw")
    du, g["pool_w"], g["pool_scale"] = _pool_bwd(zu, dpm, p["pool_w"], p["pool_scale"], "pool_bwd")
    dq, dkc, dkp, dvc, dvp, dsink = _attn_bwd(qkn, zv, p["sinks"], do, "attn_bwd")
    dqk = jnp.concatenate([dq, dkc + _shift_up(dkp)], axis=1)
    dv = dvc + _shift_up(dvp)
    dzqk, dgqk = _qk_bwd(dqk, zqk, p["gqk"], *tabs, "qk_bwd")
    g["q_norm"] = dgqk[0, :ATTN_DIM].reshape(N_Q_HEADS, HEAD_DIM).sum(axis=0)
    g["k_norm"] = dgqk[0, ATTN_DIM:].reshape(KV_DIM // HEAD_DIM, HEAD_DIM).sum(axis=0)
    g["sinks"] = -dsink[:, 0]
    dz = jnp.concatenate([du, dzqk, dv.astype(CDT), dgl], axis=1)
    g["w_in"] = _mm_tn(h, dz, "mix_dwin")
    dh = _mm_nt(dz, p["w_in"], "mix_in_dx", F32)
    dx, d_ln = _norm_bwd(dh, x, p["ln_mix"], dxo, "mix_norm_bwd")
    g["ln_mix"] = d_ln[0]
    return dx, g


def _local_step(x, tgt, layers):
    T = x.shape[0]
    tabs = _rope_tables(T)
    saved = []
    for p in layers:
        x, s1 = _ffn_forward(x, p["ln_ffn1"], p["w_ffn1_gu"], p["w_ffn1_down"], "ffn1")
        x, s2 = _mixer_forward(x, p, tabs)
        x, s3 = _ffn_forward(x, p["ln_ffn2"], p["w_ffn2_gu"], p["w_ffn2_down"], "ffn2")
        saved.append((s1, s2, s3))
    dx, loss = _loss_head(x, tgt, "loss_head")
    grads = [None] * len(layers)
    for l in reversed(range(len(layers))):
        p = layers[l]
        s1, s2, s3 = saved[l]
        dx, d_ln2, d_gu2, d_dn2 = _ffn_backward(dx, s3, p["ln_ffn2"], p["w_ffn2_gu"], p["w_ffn2_down"], "ffn2")
        dx, g = _mixer_backward(dx, s2, p, tabs)
        dx, d_ln1, d_gu1, d_dn1 = _ffn_backward(dx, s1, p["ln_ffn1"], p["w_ffn1_gu"], p["w_ffn1_down"], "ffn1")
        g.update(ln_ffn1=d_ln1[0], w_ffn1_gu=d_gu1, w_ffn1_down=d_dn1, ln_ffn2=d_ln2[0], w_ffn2_gu=d_gu2, w_ffn2_down=d_dn2)
        grads[l] = g
    return loss, dx, grads


def _full_from_gathered(name, gathered, l):
    blk = gathered[:, l]
    if name in COL_SHARDED:
        return jnp.transpose(blk, (1, 0, 2)).reshape(blk.shape[1], N_CHIPS * blk.shape[2])
    return blk.reshape(N_CHIPS * blk.shape[1], blk.shape[2])


def _shard_major(name, per_layer):
    full = jnp.stack(per_layer)
    L, K, N = full.shape
    if name in COL_SHARDED:
        return jnp.transpose(full.reshape(L, K, N_CHIPS, N // N_CHIPS), (2, 0, 1, 3))
    return jnp.transpose(full.reshape(L, N_CHIPS, K // N_CHIPS, N), (1, 0, 2, 3))


def _pack_small(parts):
    rows, spans, lo = [], [], 0
    for v in parts:
        flat = v.reshape(-1)
        nrow = -(-flat.shape[0] // LANES)
        flat = jnp.pad(flat, (0, nrow * LANES - flat.shape[0]))
        rows.append(flat.reshape(nrow, LANES))
        spans.append((lo, nrow))
        lo += nrow
    pad = -lo % 8
    if pad:
        rows.append(jnp.zeros((pad, LANES), F32))
    return jnp.concatenate(rows, axis=0), spans


def _unpack_small(packed, spans, shapes):
    out = []
    for (lo, nrow), shape in zip(spans, shapes):
        size = 1
        for s in shape:
            size *= s
        out.append(packed[lo:lo + nrow].reshape(-1)[:size].reshape(shape))
    return out


def kernel(x, ln_ffn1, w_ffn1_gu, w_ffn1_down, ln_mix, w_in, pool_w, pool_scale, w_pool_branch, q_norm, k_norm, sinks, w_attn_branch, w_out, ln_ffn2, w_ffn2_gu, w_ffn2_down, loss_target, m_ln_ffn1, m_w_ffn1_gu, m_w_ffn1_down, m_ln_mix, m_w_in, m_pool_w, m_pool_scale, m_w_pool_branch, m_q_norm, m_k_norm, m_sinks, m_w_attn_branch, m_w_out, m_ln_ffn2, m_w_ffn2_gu, m_w_ffn2_down, v_ln_ffn1, v_w_ffn1_gu, v_w_ffn1_down, v_ln_mix, v_w_in, v_pool_w, v_pool_scale, v_w_pool_branch, v_q_norm, v_k_norm, v_sinks, v_w_attn_branch, v_w_out, v_ln_ffn2, v_w_ffn2_gu, v_w_ffn2_down):
    w = dict(ln_ffn1=ln_ffn1, w_ffn1_gu=w_ffn1_gu, w_ffn1_down=w_ffn1_down, ln_mix=ln_mix, w_in=w_in, pool_w=pool_w,
             pool_scale=pool_scale, w_pool_branch=w_pool_branch, q_norm=q_norm, k_norm=k_norm, sinks=sinks,
             w_attn_branch=w_attn_branch, w_out=w_out, ln_ffn2=ln_ffn2, w_ffn2_gu=w_ffn2_gu, w_ffn2_down=w_ffn2_down)
    mom = dict(ln_ffn1=m_ln_ffn1, w_ffn1_gu=m_w_ffn1_gu, w_ffn1_down=m_w_ffn1_down, ln_mix=m_ln_mix, w_in=m_w_in,
               pool_w=m_pool_w, pool_scale=m_pool_scale, w_pool_branch=m_w_pool_branch, q_norm=m_q_norm, k_norm=m_k_norm,
               sinks=m_sinks, w_attn_branch=m_w_attn_branch, w_out=m_w_out, ln_ffn2=m_ln_ffn2, w_ffn2_gu=m_w_ffn2_gu,
               w_ffn2_down=m_w_ffn2_down)
    var = dict(ln_ffn1=v_ln_ffn1, w_ffn1_gu=v_w_ffn1_gu, w_ffn1_down=v_w_ffn1_down, ln_mix=v_ln_mix, w_in=v_w_in,
               pool_w=v_pool_w, pool_scale=v_pool_scale, w_pool_branch=v_w_pool_branch, q_norm=v_q_norm, k_norm=v_k_norm,
               sinks=v_sinks, w_attn_branch=v_w_attn_branch, w_out=v_w_out, ln_ffn2=v_ln_ffn2, w_ffn2_gu=v_w_ffn2_gu,
               w_ffn2_down=v_w_ffn2_down)
    L = ln_ffn1.shape[0]

    gathered = dict(zip(BIG, _all_gather_weights([w[n].astype(CDT) for n in BIG])))
    layers = []
    for l in range(L):
        p = {n: _full_from_gathered(n, gathered[n], l) for n in BIG}
        p.update(ln_ffn1=ln_ffn1[l], ln_mix=ln_mix[l], ln_ffn2=ln_ffn2[l], pool_w=pool_w[l].astype(CDT),
                 pool_scale=pool_scale[l], sinks=sinks[l],
                 gqk=jnp.concatenate([jnp.tile(q_norm[l], N_Q_HEADS), jnp.tile(k_norm[l], KV_DIM // HEAD_DIM)]).reshape(1, QK_DIM))
        layers.append(p)

    loss_part, grad_x, grads = _local_step(x[0], loss_target[0], layers)

    place = jnp.stack([lax.axis_index("c"), 2 * lax.axis_index("x") + lax.axis_index("y")]).astype(jnp.int32)
    g_major = [_shard_major(n, [g[n] for g in grads]).astype(WIRE_DT) for n in BIG]
    from_sibling = _sibling_exchange(g_major)
    chip_part = [_pair_sum(g, r, place, "grad_pair_sum") for g, r in zip(g_major, from_sibling)]
    slots = _chip_exchange(chip_part)
    reduced_half = [_chip_sum(s, p, place, "grad_chip_sum") for s, p in zip(slots, chip_part)]
    g_big = dict(zip(BIG, _sibling_share(reduced_half)))

    small_parts = [jnp.stack([g[n] for g in grads]) for n in SMALL] + [loss_part]
    packed, spans = _pack_small(small_parts)
    summed = _all_reduce_small(packed)
    *g_small_list, loss_sum = _unpack_small(summed, spans, [w[n].shape for n in SMALL] + [(1, 1)])
    g_small = dict(zip(SMALL, g_small_list))
    loss = loss_sum[0, 0]

    grad_out, delta, new_m, new_v = {}, {}, {}, {}
    for n in BIG:
        shape = w[n].shape
        flat = (shape[0] * shape[1], shape[2])
        grad_out[n] = g_big[n]
        d, nm, nv = _adamw(w[n].reshape(flat), g_big[n].reshape(flat), mom[n].reshape(flat), var[n].reshape(flat), "adamw")
        delta[n], new_m[n], new_v[n] = d.reshape(shape), nm.reshape(shape), nv.reshape(shape)
    pw, _ = _pack_small([w[n] for n in SMALL])
    pg, sp = _pack_small([g_small[n] for n in SMALL])
    pm_, _ = _pack_small([mom[n] for n in SMALL])
    pv, _ = _pack_small([var[n] for n in SMALL])
    d, nm, nv = _adamw(pw, pg, pm_, pv, "adamw_small")
    shapes = [w[n].shape for n in SMALL]
    for n, dv, mv, vv in zip(SMALL, _unpack_small(d, sp, shapes), _unpack_small(nm, sp, shapes), _unpack_small(nv, sp, shapes)):
        grad_out[n], delta[n], new_m[n], new_v[n] = g_small[n], dv, mv, vv

    return (loss, grad_x[None], *[grad_out[n] for n in WEIGHTS], *[delta[n] for n in WEIGHTS],
            *[new_m[n] for n in WEIGHTS], *[new_v[n] for n in WEIGHTS])
```

```python
import functools

import jax
import jax.numpy as jnp
from jax import lax
from jax.experimental import pallas as pl
from jax.experimental.pallas import tpu as pltpu

F32 = jnp.float32
CDT = jnp.bfloat16
WIRE_DT = jnp.bfloat16

D_MODEL = 1024
POOL_WINDOWS = (2, 4, 8, 16)
POOL_WMAX = 16
GROUP = 128
POOL_DIM = 512
HEAD_DIM = 64
N_Q_HEADS = 8
ATTN_DIM = 512
KV_DIM = 128
QK_DIM = ATTN_DIM + KV_DIM
GATE_DIM = 2 * D_MODEL
BLOCK = 128
ROPE_THETA = 500000.0
ROT_DIM = 16
EPS = 1e-6
ATTN_SCALE = HEAD_DIM ** -0.5

ADAM_LR = 0.001
ADAM_B1 = 0.9
ADAM_B2 = 0.999
ADAM_EPS = 1e-08
ADAM_WD = 0.01
ADAM_STEP = 10

N_CHIPS = 4
N_DEV = 8
LANES = 128
VMEM_LIMIT_BYTES = 48 * 1024 * 1024

MESH = pl.DeviceIdType.MESH
ANY = pl.BlockSpec(memory_space=pl.ANY)

BIG = ("w_ffn1_gu", "w_ffn1_down", "w_in", "w_pool_branch", "w_attn_branch", "w_out", "w_ffn2_gu", "w_ffn2_down")
COL_SHARDED = ("w_ffn1_gu", "w_in", "w_pool_branch", "w_attn_branch", "w_ffn2_gu")
SMALL = ("ln_ffn1", "ln_mix", "pool_w", "pool_scale", "q_norm", "k_norm", "sinks", "ln_ffn2")
WEIGHTS = ("ln_ffn1", "w_ffn1_gu", "w_ffn1_down", "ln_mix", "w_in", "pool_w", "pool_scale", "w_pool_branch",
           "q_norm", "k_norm", "sinks", "w_attn_branch", "w_out", "ln_ffn2", "w_ffn2_gu", "w_ffn2_down")


def _tile(n, target, mult=8):
    if n <= target:
        return n
    for t in range(target - target % mult, 0, -mult):
        if n % t == 0:
            return t
    raise ValueError((n, target, mult))


def _params(*sem):
    return pltpu.CompilerParams(dimension_semantics=sem, vmem_limit_bytes=VMEM_LIMIT_BYTES)


def _sigmoid(v):
    return 0.5 * jnp.tanh(0.5 * v) + 0.5


def _dot(a, b):
    return jnp.dot(a, b, preferred_element_type=F32)


def _dot_nt(a, b):
    return lax.dot_general(a, b, (((1,), (1,)), ((), ())), preferred_element_type=F32)


def _dot_tn(a, b):
    return lax.dot_general(a, b, (((0,), (0,)), ((), ())), preferred_element_type=F32)


def _norm_fwd(x, g, name):
    T, Dm = x.shape
    tm = _tile(T, 512)

    def body(x_ref, g_ref, h_ref):
        xv = x_ref[...]
        r = lax.rsqrt(jnp.mean(xv * xv, axis=-1, keepdims=True) + EPS)
        h_ref[...] = (xv * r * g_ref[...]).astype(h_ref.dtype)

    row = pl.BlockSpec((tm, Dm), lambda i: (i, 0))
    return pl.pallas_call(
        body, name=name, grid=(T // tm,),
        in_specs=[row, pl.BlockSpec((1, Dm), lambda i: (0, 0))], out_specs=row,
        out_shape=jax.ShapeDtypeStruct((T, Dm), CDT), compiler_params=_params("parallel"),
    )(x, g.reshape(1, Dm))


def _loss_head(y, tgt, name):
    T, Dm = y.shape
    tm = _tile(T, 512)

    def body(y_ref, t_ref, dy_ref, loss_ref):
        @pl.when(pl.program_id(0) == 0)
        def _():
            loss_ref[...] = jnp.zeros_like(loss_ref)

        diff = y_ref[...] - t_ref[...]
        dy_ref[...] = diff * (1.0 / Dm)
        part = jnp.sum(jnp.mean(diff * diff, axis=-1, keepdims=True), axis=0, keepdims=True)
        loss_ref[...] += 0.5 * part

    row = pl.BlockSpec((tm, Dm), lambda i: (i, 0))
    one = pl.BlockSpec((1, 1), lambda i: (0, 0))
    return pl.pallas_call(
        body, name=name, grid=(T // tm,),
        in_specs=[row, row], out_specs=[row, one],
        out_shape=[jax.ShapeDtypeStruct((T, Dm), F32), jax.ShapeDtypeStruct((1, 1), F32)],
        compiler_params=_params("arbitrary"),
    )(y, tgt)


def _mm_nn(a, b, name, out_dtype, res=None, scale=1.0, tm_target=512):
    M, K = a.shape
    N = b.shape[1]
    tm = _tile(M, tm_target)

    def body(a_ref, b_ref, *rest):
        acc = _dot(a_ref[...].astype(CDT), b_ref[...])
        if res is None:
            (o_ref,) = rest
        else:
            r_ref, o_ref = rest
            acc = r_ref[...] + scale * acc
        o_ref[...] = acc.astype(o_ref.dtype)

    in_specs = [pl.BlockSpec((tm, K), lambda i: (i, 0)), pl.BlockSpec((K, N), lambda i: (0, 0))]
    args = [a, b]
    if res is not None:
        in_specs.append(pl.BlockSpec((tm, N), lambda i: (i, 0)))
        args.append(res)
    return pl.pallas_call(
        body, name=name, grid=(M // tm,), in_specs=in_specs,
        out_specs=pl.BlockSpec((tm, N), lambda i: (i, 0)),
        out_shape=jax.ShapeDtypeStruct((M, N), out_dtype), compiler_params=_params("parallel"),
    )(*args)


def _mm_nt(a, b, name, out_dtype, tm_target=512):
    M, K = a.shape
    N = b.shape[0]
    tm = _tile(M, tm_target)

    def body(a_ref, b_ref, o_ref):
        o_ref[...] = _dot_nt(a_ref[...].astype(CDT), b_ref[...]).astype(o_ref.dtype)

    return pl.pallas_call(
        body, name=name, grid=(M // tm,),
        in_specs=[pl.BlockSpec((tm, K), lambda i: (i, 0)), pl.BlockSpec((N, K), lambda i: (0, 0))],
        out_specs=pl.BlockSpec((tm, N), lambda i: (i, 0)),
        out_shape=jax.ShapeDtypeStruct((M, N), out_dtype), compiler_params=_params("parallel"),
    )(a, b)


def _mm_tn(x, dy, name, scale=1.0, tn_target=1664, tm_target=1408, tk_target=1024):
    T, M = x.shape
    split = dy.ndim == 3
    Nh = dy.shape[-1]
    N = 2 * Nh if split else Nh
    tm = _tile(M, tm_target, LANES)
    tn = _tile(Nh, tn_target, LANES)
    tk = _tile(T, tk_target)
    nk = T // tk
    njh = Nh // tn

    def body(x_ref, dy_ref, o_ref):
        @pl.when(pl.program_id(2) == 0)
        def _():
            o_ref[...] = jnp.zeros_like(o_ref)

        part = _dot_tn(x_ref[...].astype(CDT), dy_ref[...].astype(CDT))
        o_ref[...] += part if scale == 1.0 else scale * part

    if split:
        dy_spec = pl.BlockSpec((None, tk, tn), lambda i, j, k: (j // njh, k, j % njh))
    else:
        dy_spec = pl.BlockSpec((tk, tn), lambda i, j, k: (k, j))
    return pl.pallas_call(
        body, name=name, grid=(M // tm, N // tn, nk),
        in_specs=[pl.BlockSpec((tk, tm), lambda i, j, k: (k, i)), dy_spec],
        out_specs=pl.BlockSpec((tm, tn), lambda i, j, k: (i, j)),
        out_shape=jax.ShapeDtypeStruct((M, N), F32),
        compiler_params=_params("parallel", "parallel", "arbitrary"),
    )(x, dy)


def _ffn_up(h, wgu, name):
    T, Dm = h.shape
    Fd = wgu.shape[1] // 2
    tm = _tile(T, 512)
    tn = _tile(Fd, 1408, LANES)
    nj = Fd // tn

    def body(h_ref, wg_ref, wu_ref, gu_ref, a_ref):
        hv = h_ref[...]
        g = _dot(hv, wg_ref[...])
        u = _dot(hv, wu_ref[...])
        gu_ref[0] = g.astype(gu_ref.dtype)
        gu_ref[1] = u.astype(gu_ref.dtype)
        a_ref[...] = (g * _sigmoid(g) * u).astype(a_ref.dtype)

    return pl.pallas_call(
        body, name=name, grid=(T // tm, nj),
        in_specs=[pl.BlockSpec((tm, Dm), lambda i, j: (i, 0)),
                  pl.BlockSpec((Dm, tn), lambda i, j: (0, j)),
                  pl.BlockSpec((Dm, tn), lambda i, j: (0, j + nj))],
        out_specs=[pl.BlockSpec((2, tm, tn), lambda i, j: (0, i, j)), pl.BlockSpec((tm, tn), lambda i, j: (i, j))],
        out_shape=[jax.ShapeDtypeStruct((2, T, Fd), CDT), jax.ShapeDtypeStruct((T, Fd), CDT)],
        compiler_params=_params("parallel", "parallel"),
    )(h, wgu, wgu)


def _ffn_down_bwd(dxo, wd, gu, name):
    T, Dm = dxo.shape
    Fd = wd.shape[0]
    tm = _tile(T, 512)
    tn = _tile(Fd, 1408, LANES)

    def body(dx_ref, wd_ref, gu_ref, dgu_ref, a_ref):
        da = 0.5 * _dot_nt(dx_ref[...].astype(CDT), wd_ref[...])
        g = gu_ref[0].astype(F32)
        u = gu_ref[1].astype(F32)
        sg = _sigmoid(g)
        silu = g * sg
        a_ref[...] = (silu * u).astype(a_ref.dtype)
        dgu_ref[0] = (da * u * (sg * (1.0 + g * (1.0 - sg)))).astype(dgu_ref.dtype)
        dgu_ref[1] = (da * silu).astype(dgu_ref.dtype)

    gu_spec = pl.BlockSpec((2, tm, tn), lambda i, j: (0, i, j))
    return pl.pallas_call(
        body, name=name, grid=(T // tm, Fd // tn),
        in_specs=[pl.BlockSpec((tm, Dm), lambda i, j: (i, 0)), pl.BlockSpec((tn, Dm), lambda i, j: (j, 0)), gu_spec],
        out_specs=[gu_spec, pl.BlockSpec((tm, tn), lambda i, j: (i, j))],
        out_shape=[jax.ShapeDtypeStruct((2, T, Fd), CDT), jax.ShapeDtypeStruct((T, Fd), CDT)],
        compiler_params=_params("parallel", "parallel"),
    )(dxo, wd, gu)


def _mm_nt_norm_bwd(a3, b, x, g, dres, name):
    S, T, Kh = a3.shape
    Dm = b.shape[0]
    tm = _tile(T, 256)

    def body(a_ref, b_ref, x_ref, g_ref, dres_ref, dx_ref, dg_ref):
        @pl.when(pl.program_id(0) == 0)
        def _():
            dg_ref[...] = jnp.zeros_like(dg_ref)

        dh = _dot_nt(a_ref[0], b_ref[:, :Kh])
        for s_ in range(1, S):
            dh = dh + _dot_nt(a_ref[s_], b_ref[:, s_ * Kh:(s_ + 1) * Kh])
        xv = x_ref[...]
        r = lax.rsqrt(jnp.mean(xv * xv, axis=-1, keepdims=True) + EPS)
        xh = xv * r
        dg_ref[...] += jnp.sum(dh * xh, axis=0, keepdims=True)
        dxh = dh * g_ref[...]
        dx_ref[...] = dres_ref[...] + r * (dxh - xh * jnp.mean(dxh * xh, axis=-1, keepdims=True))

    row = pl.BlockSpec((tm, Dm), lambda i: (i, 0))
    vec = pl.BlockSpec((1, Dm), lambda i: (0, 0))
    return pl.pallas_call(
        body, name=name, grid=(T // tm,),
        in_specs=[pl.BlockSpec((S, tm, Kh), lambda i: (0, i, 0)),
                  pl.BlockSpec(b.shape, lambda i: (0, 0), pipeline_mode=pl.Buffered(1)), row, vec, row],
        out_specs=[row, vec],
        out_shape=[jax.ShapeDtypeStruct((T, Dm), F32), jax.ShapeDtypeStruct((1, Dm), F32)],
        compiler_params=_params("arbitrary"),
    )(a3, b, x, g.reshape(1, Dm), dres)


def _mm_in(h, w_in, name):
    T, Dm = h.shape
    tm = _tile(T, 256)
    widths = (POOL_DIM, QK_DIM, KV_DIM, GATE_DIM)

    def body(h_ref, w_ref, *outs):
        z = _dot(h_ref[...], w_ref[...])
        lo = 0
        for o_ref, wd in zip(outs, widths):
            o_ref[...] = z[:, lo:lo + wd]
            lo += wd

    return pl.pallas_call(
        body, name=name, grid=(T // tm,),
        in_specs=[pl.BlockSpec((tm, Dm), lambda i: (i, 0)), pl.BlockSpec(w_in.shape, lambda i: (0, 0))],
        out_specs=[pl.BlockSpec((tm, wd), lambda i: (i, 0)) for wd in widths],
        out_shape=[jax.ShapeDtypeStruct((T, wd), F32) for wd in widths],
        compiler_params=_params("parallel"),
    )(h, w_in)


def _window_mean_minus_token(ext, u, g, w, pos):
    sl = slice(g * GROUP, (g + 1) * GROUP)
    s = ext[:, sl]
    span = 1
    while span < w:
        s = s + pltpu.roll(s, span, axis=0)
        span *= 2
    cnt = jnp.minimum(pos + 1, w).astype(F32)
    return s[POOL_WMAX:, :] / cnt - u[:, sl]


def _pool_fwd(zu, pool_w, scale, name):
    T = zu.shape[0]
    tm = _tile(T, 512, POOL_WMAX)
    hb = tm // POOL_WMAX

    def body(u_ref, halo_ref, pw_ref, sc_ref, pm_ref):
        i = pl.program_id(0)
        u = u_ref[...]
        halo = jnp.where(i > 0, halo_ref[...], 0.0)
        ext = jnp.concatenate([halo, u], axis=0)
        pos = i * tm + lax.broadcasted_iota(jnp.int32, (tm, 1), 0)
        ys = []
        for g, w in enumerate(POOL_WINDOWS):
            d = _window_mean_minus_token(ext, u, g, w, pos)
            ys.append(_dot(d.astype(CDT), pw_ref[g]))
        pm_ref[...] = (jnp.concatenate(ys, axis=1) * sc_ref[...]).astype(pm_ref.dtype)

    row = pl.BlockSpec((tm, POOL_DIM), lambda i: (i, 0))
    return pl.pallas_call(
        body, name=name, grid=(T // tm,),
        in_specs=[row, pl.BlockSpec((POOL_WMAX, POOL_DIM), lambda i: (jnp.maximum(i * hb - 1, 0), 0)),
                  pl.BlockSpec(pool_w.shape, lambda i: (0, 0, 0)), pl.BlockSpec((1, POOL_DIM), lambda i: (0, 0))],
        out_specs=row, out_shape=jax.ShapeDtypeStruct((T, POOL_DIM), CDT),
        compiler_params=_params("parallel"),
    )(zu, zu, pool_w, scale.reshape(1, POOL_DIM))


def _pool_bwd(zu, dpm, pool_w, scale, name):
    T = zu.shape[0]
    tm = _tile(T, 512, POOL_WMAX)
    hb = tm // POOL_WMAX
    nsteps = T // tm
    ext_rows = tm + POOL_WMAX

    def body(u_ref, halo_ref, dpm_ref, dnext_ref, pw_ref, sc_ref, du_ref, dpw_ref, dsc_ref):
        i = pl.program_id(0)

        @pl.when(i == 0)
        def _():
            dpw_ref[...] = jnp.zeros_like(dpw_ref)
            dsc_ref[...] = jnp.zeros_like(dsc_ref)

        u = u_ref[...]
        halo = jnp.where(i > 0, halo_ref[...], 0.0)
        ext = jnp.concatenate([halo, u], axis=0)
        dpm_t = dpm_ref[...].astype(F32)
        dnext = jnp.where(i < nsteps - 1, dnext_ref[...].astype(F32), 0.0)
        dext = jnp.concatenate([dpm_t, dnext], axis=0)
        sc = sc_ref[...]
        pos = i * tm + lax.broadcasted_iota(jnp.int32, (tm, 1), 0)
        pos_ext = i * tm + lax.broadcasted_iota(jnp.int32, (ext_rows, 1), 0)
        dus, dscs = [], []
        for g, w in enumerate(POOL_WINDOWS):
            sl = slice(g * GROUP, (g + 1) * GROUP)
            dc = _window_mean_minus_token(ext, u, g, w, pos).astype(CDT)
            y = _dot(dc, pw_ref[g])
            dscs.append(jnp.sum(dpm_t[:, sl] * y, axis=0, keepdims=True))
            dy_ext = (dext[:, sl] * sc[:, sl]).astype(CDT)
            dpw_ref[g] += _dot_tn(dc, dy_ext[:tm])
            dd = _dot_nt(dy_ext, pw_ref[g])
            r = dd / jnp.minimum(pos_ext + 1, w).astype(F32)
            span = 1
            while span < w:
                r = r + pltpu.roll(r, ext_rows - span, axis=0)
                span *= 2
            dus.append(r[:tm] - dd[:tm])
        du_ref[...] = jnp.concatenate(dus, axis=1).astype(du_ref.dtype)
        dsc_ref[...] += jnp.concatenate(dscs, axis=1)

    row = pl.BlockSpec((tm, POOL_DIM), lambda i: (i, 0))
    prev = pl.BlockSpec((POOL_WMAX, POOL_DIM), lambda i: (jnp.maximum(i * hb - 1, 0), 0))
    nxt = pl.BlockSpec((POOL_WMAX, POOL_DIM), lambda i: (jnp.minimum((i + 1) * hb, nsteps * hb - 1), 0))
    return pl.pallas_call(
        body, name=name, grid=(nsteps,),
        in_specs=[row, prev, row, nxt, pl.BlockSpec(pool_w.shape, lambda i: (0, 0, 0)),
                  pl.BlockSpec((1, POOL_DIM), lambda i: (0, 0))],
        out_specs=[row, pl.BlockSpec(pool_w.shape, lambda i: (0, 0, 0)), pl.BlockSpec((1, POOL_DIM), lambda i: (0, 0))],
        out_shape=[jax.ShapeDtypeStruct((T, POOL_DIM), CDT), jax.ShapeDtypeStruct(pool_w.shape, F32),
                   jax.ShapeDtypeStruct((1, POOL_DIM), F32)],
        compiler_params=_params("arbitrary"),
    )(zu, zu, dpm, dpm, pool_w, scale.reshape(1, POOL_DIM))


def _rope_tables(T):
    pos = jnp.arange(T, dtype=F32)
    inv_freq = ROPE_THETA ** (-jnp.arange(0, ROT_DIM, 2, dtype=F32) / ROT_DIM)
    ang = pos[:, None] * inv_freq[None, :]
    cos, sin = jnp.cos(ang), jnp.sin(ang)
    rest = HEAD_DIM - ROT_DIM
    cos_h = jnp.concatenate([cos, cos, jnp.ones((T, rest), F32)], axis=1)
    sin_h = jnp.concatenate([-sin, sin, jnp.zeros((T, rest), F32)], axis=1)
    return jnp.tile(cos_h, (1, 2)), jnp.tile(sin_h, (1, 2))


def _lane_masks():
    lane = lax.broadcasted_iota(jnp.int32, (1, LANES), 1)
    in_head = lane % HEAD_DIM
    return lane < HEAD_DIM, in_head < ROT_DIM // 2


def _rope_partner(v, low):
    lane = lax.broadcasted_iota(jnp.int32, (1, LANES), 1)
    swapped = jnp.where(low, pltpu.roll(v, LANES - ROT_DIM // 2, axis=1), pltpu.roll(v, ROT_DIM // 2, axis=1))
    return jnp.where(lane % HEAD_DIM < ROT_DIM, swapped, 0.0)


def _head_mean(v, first):
    lo = jnp.sum(jnp.where(first, v, 0.0), axis=-1, keepdims=True)
    hi = jnp.sum(jnp.where(first, 0.0, v), axis=-1, keepdims=True)
    return jnp.where(first, lo, hi) * (1.0 / HEAD_DIM)


def _qk_fwd(zqk, gqk, cos_t, sin_t, name):
    T = zqk.shape[0]
    tm = _tile(T, 512)

    def body(z_ref, g_ref, c_ref, s_ref, o_ref):
        first, low = _lane_masks()
        cosv, sinv = c_ref[...], s_ref[...]
        for c in range(QK_DIM // LANES):
            sl = slice(c * LANES, (c + 1) * LANES)
            xv = z_ref[:, sl]
            r = lax.rsqrt(_head_mean(xv * xv, first) + EPS)
            xn = xv * r * g_ref[:, sl]
            o_ref[:, sl] = (xn * cosv + _rope_partner(xn, low) * sinv).astype(o_ref.dtype)

    row = pl.BlockSpec((tm, QK_DIM), lambda i: (i, 0))
    tab = pl.BlockSpec((tm, LANES), lambda i: (i, 0))
    return pl.pallas_call(
        body, name=name, grid=(T // tm,),
        in_specs=[row, pl.BlockSpec((1, QK_DIM), lambda i: (0, 0)), tab, tab], out_specs=row,
        out_shape=jax.ShapeDtypeStruct((T, QK_DIM), CDT), compiler_params=_params("parallel"),
    )(zqk, gqk, cos_t, sin_t)


def _qk_bwd(dqk, zqk, gqk, cos_t, sin_t, name):
    T = zqk.shape[0]
    tm = _tile(T, 512)

    def body(d_ref, z_ref, g_ref, c_ref, s_ref, dz_ref, dg_ref):
        @pl.when(pl.program_id(0) == 0)
        def _():
            dg_ref[...] = jnp.zeros_like(dg_ref)

        first, low = _lane_masks()
        cosv, sinv = c_ref[...], s_ref[...]
        dgs = []
        for c in range(QK_DIM // LANES):
            sl = slice(c * LANES, (c + 1) * LANES)
            dout = d_ref[:, sl]
            dxn = dout * cosv + _rope_partner(dout * sinv, low)
            xv = z_ref[:, sl]
            r = lax.rsqrt(_head_mean(xv * xv, first) + EPS)
            xh = xv * r
            dgs.append(jnp.sum(dxn * xh, axis=0, keepdims=True))
            dxh = dxn * g_ref[:, sl]
            dz_ref[:, sl] = (r * (dxh - xh * _head_mean(dxh * xh, first))).astype(dz_ref.dtype)
        dg_ref[...] += jnp.concatenate(dgs, axis=1)

    row = pl.BlockSpec((tm, QK_DIM), lambda i: (i, 0))
    tab = pl.BlockSpec((tm, LANES), lambda i: (i, 0))
    vec = pl.BlockSpec((1, QK_DIM), lambda i: (0, 0))
    return pl.pallas_call(
        body, name=name, grid=(T // tm,),
        in_specs=[row, row, vec, tab, tab], out_specs=[row, vec],
        out_shape=[jax.ShapeDtypeStruct((T, QK_DIM), CDT), jax.ShapeDtypeStruct((1, QK_DIM), F32)],
        compiler_params=_params("arbitrary"),
    )(dqk, zqk, gqk, cos_t, sin_t)


def _dup_half(v, first, kv):
    swapped = pltpu.roll(v, HEAD_DIM, axis=1)
    return jnp.where(first, v, swapped) if kv == 0 else jnp.where(first, swapped, v)


HEADS_PER_KV = 4
HEAD_STACK = 1


def _attn_bias():
    qi = lax.broadcasted_iota(jnp.int32, (HEAD_STACK * BLOCK, 2 * BLOCK), 0) % BLOCK
    ki = lax.broadcasted_iota(jnp.int32, (HEAD_STACK * BLOCK, 2 * BLOCK), 1)
    diff = qi + BLOCK - ki
    band = (diff >= 0) & (diff < BLOCK)
    return jnp.stack([jnp.where(band, 0.0, -jnp.inf), jnp.where(band & (ki >= BLOCK), 0.0, -jnp.inf)]).astype(F32)


def _attn_blocks(T):
    return _tile(T // BLOCK, 4, 1)


def _stack_heads(ref, rows, kv, heads, first):
    parts = []
    for h in heads:
        c = 2 * kv + h // 2
        v = ref[rows, c * LANES:(c + 1) * LANES].astype(CDT)
        zero = jnp.zeros_like(v)
        parts.append(jnp.where(first, v, zero) if h % 2 == 0 else jnp.where(first, zero, v))
    return parts[0] if len(parts) == 1 else jnp.concatenate(parts, axis=0)


def _row_blocks(v, n):
    return [v[b * BLOCK:(b + 1) * BLOCK] for b in range(n)]


def _sink_column(sink_ref, kv, heads):
    cols = [jnp.full((BLOCK, 1), sink_ref[HEADS_PER_KV * kv + h], F32) for h in heads]
    return cols[0] if len(cols) == 1 else jnp.concatenate(cols, axis=0)


def _head_groups():
    return [tuple(range(g, g + HEAD_STACK)) for g in range(0, HEADS_PER_KV, HEAD_STACK)]


def _softmax_with_sink(qst, kdup, sinkcol, bias):
    s = _dot_nt(qst, kdup) * ATTN_SCALE + bias
    m = jnp.maximum(jnp.max(s, axis=-1, keepdims=True), sinkcol)
    pu = jnp.exp(s - m)
    es = jnp.exp(sinkcol - m)
    inv = 1.0 / (jnp.sum(pu, axis=-1, keepdims=True) + es)
    return pu * inv, es * inv


def _attn_fwd(qkn, zv, sinks, name):
    T = qkn.shape[0]
    R = _attn_blocks(T)
    tq = R * BLOCK

    def body(sink_ref, bias_ref, qk_ref, qkp_ref, v_ref, vp_ref, o_ref):
        i = pl.program_id(0)
        first, _ = _lane_masks()
        kall = jnp.concatenate([qkp_ref[:, ATTN_DIM:], qk_ref[:, ATTN_DIM:]], axis=0)
        vall = jnp.concatenate([vp_ref[...], v_ref[...]], axis=0).astype(CDT)
        for r in range(R):
            bias = bias_ref[jnp.where(i == 0, 1, 0)] if r == 0 else bias_ref[0]
            rows = slice(r * BLOCK, (r + 2) * BLOCK)
            qrows = slice(r * BLOCK, (r + 1) * BLOCK)
            for kv in range(2):
                kdup = _dup_half(kall[rows], first, kv)
                vdup = _dup_half(vall[rows], first, kv)
                res = []
                for heads in _head_groups():
                    p, _ = _softmax_with_sink(_stack_heads(qk_ref, qrows, kv, heads, first), kdup,
                                              _sink_column(sink_ref, kv, heads), bias)
                    res += _row_blocks(_dot(p.astype(CDT), vdup), len(heads))
                o_ref[qrows, 2 * kv * LANES:(2 * kv + 1) * LANES] = jnp.where(first, res[0], res[1]).astype(o_ref.dtype)
                o_ref[qrows, (2 * kv + 1) * LANES:(2 * kv + 2) * LANES] = jnp.where(first, res[2], res[3]).astype(o_ref.dtype)

    bias = _attn_bias()
    prev = lambda i: (jnp.maximum(i * R - 1, 0), 0)
    return pl.pallas_call(
        body, name=name, grid=(T // tq,),
        in_specs=[pl.BlockSpec(memory_space=pltpu.SMEM), pl.BlockSpec(bias.shape, lambda i: (0, 0, 0)),
                  pl.BlockSpec((tq, QK_DIM), lambda i: (i, 0)), pl.BlockSpec((BLOCK, QK_DIM), prev),
                  pl.BlockSpec((tq, KV_DIM), lambda i: (i, 0)), pl.BlockSpec((BLOCK, KV_DIM), prev)],
        out_specs=pl.BlockSpec((tq, ATTN_DIM), lambda i: (i, 0)),
        out_shape=jax.ShapeDtypeStruct((T, ATTN_DIM), CDT), compiler_params=_params("parallel"),
    )(sinks, bias, qkn, qkn, zv, zv)


def _attn_bwd(qkn, zv, sinks, do, name):
    T = qkn.shape[0]
    R = _attn_blocks(T)
    tq = R * BLOCK

    def body(sink_ref, bias_ref, qk_ref, qkp_ref, v_ref, vp_ref, do_ref, dq_ref, dkc_ref, dkp_ref, dvc_ref, dvp_ref, ds_ref):
        i = pl.program_id(0)

        @pl.when(i == 0)
        def _():
            ds_ref[...] = jnp.zeros_like(ds_ref)

        first, _ = _lane_masks()
        kall = jnp.concatenate([qkp_ref[:, ATTN_DIM:], qk_ref[:, ATTN_DIM:]], axis=0)
        vall = jnp.concatenate([vp_ref[...], v_ref[...]], axis=0).astype(CDT)
        for r in range(R):
            bias = bias_ref[jnp.where(i == 0, 1, 0)] if r == 0 else bias_ref[0]
            rows = slice(r * BLOCK, (r + 2) * BLOCK)
            qrows = slice(r * BLOCK, (r + 1) * BLOCK)
            dk_out, dv_out = [], []
            for kv in range(2):
                kdup = _dup_half(kall[rows], first, kv)
                vdup = _dup_half(vall[rows], first, kv)
                dq_h = []
                dk_acc = jnp.zeros((2 * BLOCK, LANES), F32)
                dv_acc = jnp.zeros((2 * BLOCK, LANES), F32)
                for heads in _head_groups():
                    qst = _stack_heads(qk_ref, qrows, kv, heads, first)
                    dost = _stack_heads(do_ref, qrows, kv, heads, first)
                    p, psink = _softmax_with_sink(qst, kdup, _sink_column(sink_ref, kv, heads), bias)
                    dp = _dot_nt(dost, vdup)
                    delta = jnp.sum(p * dp, axis=-1, keepdims=True)
                    dsc = (p * (dp - delta)).astype(CDT)
                    for b, term in enumerate(_row_blocks(psink * delta, len(heads))):
                        row = HEADS_PER_KV * kv + heads[b]
                        ds_ref[row:row + 1, :] += jnp.sum(term, axis=0, keepdims=True)
                    dq_h += _row_blocks(_dot(dsc, kdup) * ATTN_SCALE, len(heads))
                    dk_acc = dk_acc + _dot_tn(dsc, qst) * ATTN_SCALE
                    dv_acc = dv_acc + _dot_tn(p.astype(CDT), dost)
                dq_ref[qrows, 2 * kv * LANES:(2 * kv + 1) * LANES] = jnp.where(first, dq_h[0], dq_h[1])
                dq_ref[qrows, (2 * kv + 1) * LANES:(2 * kv + 2) * LANES] = jnp.where(first, dq_h[2], dq_h[3])
                dk_out.append(dk_acc + pltpu.roll(dk_acc, HEAD_DIM, axis=1))
                dv_out.append(dv_acc + pltpu.roll(dv_acc, HEAD_DIM, axis=1))
            dk = jnp.where(first, dk_out[0], dk_out[1])
            dv = jnp.where(first, dv_out[0], dv_out[1])
            dkp_ref[qrows, :] = dk[:BLOCK]
            dkc_ref[qrows, :] = dk[BLOCK:]
            dvp_ref[qrows, :] = dv[:BLOCK]
            dvc_ref[qrows, :] = dv[BLOCK:]

    bias = _attn_bias()
    prev = lambda i: (jnp.maximum(i * R - 1, 0), 0)
    kvrow = pl.BlockSpec((tq, KV_DIM), lambda i: (i, 0))
    qrow = pl.BlockSpec((tq, ATTN_DIM), lambda i: (i, 0))
    kv_shape = jax.ShapeDtypeStruct((T, KV_DIM), F32)
    return pl.pallas_call(
        body, name=name, grid=(T // tq,),
        in_specs=[pl.BlockSpec(memory_space=pltpu.SMEM), pl.BlockSpec(bias.shape, lambda i: (0, 0, 0)),
                  pl.BlockSpec((tq, QK_DIM), lambda i: (i, 0)), pl.BlockSpec((BLOCK, QK_DIM), prev),
                  kvrow, pl.BlockSpec((BLOCK, KV_DIM), prev), qrow],
        out_specs=[qrow, kvrow, kvrow, kvrow, kvrow, pl.BlockSpec((N_Q_HEADS, LANES), lambda i: (0, 0))],
        out_shape=[jax.ShapeDtypeStruct((T, ATTN_DIM), F32), kv_shape, kv_shape, kv_shape, kv_shape,
                   jax.ShapeDtypeStruct((N_Q_HEADS, LANES), F32)],
        compiler_params=_params("arbitrary"),
    )(sinks, bias, qkn, qkn, zv, zv, do)


def _merge_fwd(pm, o, w_pb, w_ab, zg, name):
    T = pm.shape[0]
    tm = _tile(T, 512)

    def body(pm_ref, o_ref, wp_ref, wa_ref, zg_ref, a_ref, b_ref, m_ref):
        a = _dot(pm_ref[...], wp_ref[...])
        b = _dot(o_ref[...], wa_ref[...])
        gp = _sigmoid(zg_ref[:, :D_MODEL])
        ga = _sigmoid(zg_ref[:, D_MODEL:])
        a_ref[...] = a.astype(a_ref.dtype)
        b_ref[...] = b.astype(b_ref.dtype)
        m_ref[...] = (gp * a + ga * b).astype(m_ref.dtype)

    half = pl.BlockSpec((tm, POOL_DIM), lambda i: (i, 0))
    full = pl.BlockSpec((tm, D_MODEL), lambda i: (i, 0))
    wspec = pl.BlockSpec((POOL_DIM, D_MODEL), lambda i: (0, 0))
    out = jax.ShapeDtypeStruct((T, D_MODEL), CDT)
    return pl.pallas_call(
        body, name=name, grid=(T // tm,),
        in_specs=[half, half, wspec, wspec, pl.BlockSpec((tm, GATE_DIM), lambda i: (i, 0))],
        out_specs=[full, full, full], out_shape=[out, out, out], compiler_params=_params("parallel"),
    )(pm, o, w_pb, w_ab, zg)


def _merge_bwd(dxo, w_out, a, b, zg, name):
    T = dxo.shape[0]
    tm = _tile(T, 512)

    def body(dx_ref, w_ref, a_ref, b_ref, zg_ref, da_ref, db_ref, dg_ref):
        dm = _dot_nt(dx_ref[...].astype(CDT), w_ref[...])
        gp = _sigmoid(zg_ref[:, :D_MODEL])
        ga = _sigmoid(zg_ref[:, D_MODEL:])
        da_ref[...] = (dm * gp).astype(da_ref.dtype)
        db_ref[...] = (dm * ga).astype(db_ref.dtype)
        dg_ref[:, :D_MODEL] = (dm * a_ref[...].astype(F32) * (gp * (1.0 - gp))).astype(dg_ref.dtype)
        dg_ref[:, D_MODEL:] = (dm * b_ref[...].astype(F32) * (ga * (1.0 - ga))).astype(dg_ref.dtype)

    full = pl.BlockSpec((tm, D_MODEL), lambda i: (i, 0))
    gate = pl.BlockSpec((tm, GATE_DIM), lambda i: (i, 0))
    out = jax.ShapeDtypeStruct((T, D_MODEL), CDT)
    return pl.pallas_call(
        body, name=name, grid=(T // tm,),
        in_specs=[full, pl.BlockSpec((D_MODEL, D_MODEL), lambda i: (0, 0)), full, full, gate],
        out_specs=[full, full, gate], out_shape=[out, out, jax.ShapeDtypeStruct((T, GATE_DIM), CDT)],
        compiler_params=_params("parallel"),
    )(dxo, w_out, a, b, zg)


def _adamw(w, g, m, v, name):
    Rr, C = w.shape
    tr = _tile(Rr, max(8, (1 << 19) // C // 8 * 8))

    def body(w_ref, g_ref, m_ref, v_ref, d_ref, nm_ref, nv_ref):
        gv = g_ref[...]
        nm = ADAM_B1 * m_ref[...] + (1.0 - ADAM_B1) * gv
        nv = ADAM_B2 * v_ref[...] + (1.0 - ADAM_B2) * (gv * gv)
        m_hat = nm / (1.0 - ADAM_B1 ** ADAM_STEP)
        v_hat = nv / (1.0 - ADAM_B2 ** ADAM_STEP)
        d_ref[...] = -ADAM_LR * (m_hat / (jnp.sqrt(v_hat) + ADAM_EPS) + ADAM_WD * w_ref[...])
        nm_ref[...] = nm
        nv_ref[...] = nv

    blk = pl.BlockSpec((tr, C), lambda i: (i, 0))
    out = jax.ShapeDtypeStruct((Rr, C), F32)
    return pl.pallas_call(
        body, name=name, grid=(Rr // tr,), in_specs=[blk] * 4, out_specs=[blk] * 3, out_shape=[out] * 3,
        compiler_params=_params("parallel"),
    )(w, g, m, v)


def _place():
    return lax.axis_index("x"), lax.axis_index("y"), lax.axis_index("c")


def _other_chip(x, y, d):
    return (1 - x if d & 2 else x), (1 - y if d & 1 else y)


def _rcopy(src, dst, ssem, rsem, dev):
    return pltpu.make_async_remote_copy(src_ref=src, dst_ref=dst, send_sem=ssem, recv_sem=rsem, device_id=dev,
                                        device_id_type=MESH)


def _all_gather_weights(shards):
    n = len(shards)
    Lh = shards[0].shape[0] // 2

    def body(*refs):
        ins, outs = refs[:n], refs[n:2 * n]
        ssem, rsem, fssem, frsem, mssem, mrsem = refs[2 * n:]
        x, y, c = _place()
        j = 2 * x + y
        mine, other = pl.ds(c * Lh, Lh), pl.ds((1 - c) * Lh, Lh)
        sibling = (x, y, 1 - c)
        sends = []
        for w in range(n):
            for d in (1, 2, 3):
                px, py = _other_chip(x, y, d)
                k = 3 * w + d - 1
                sends.append(_rcopy(ins[w].at[mine], outs[w].at[j, mine], ssem.at[k], rsem.at[k], (px, py, c)))
                sends[-1].start()
        mirror = [_rcopy(ins[w], outs[w].at[j], mssem.at[w], mrsem.at[w], sibling) for w in range(n)]
        for cp in mirror:
            cp.start()
        fwds = []
        for w in range(n):
            for d in (1, 2, 3):
                px, py = _other_chip(x, y, d)
                k = 3 * w + d - 1
                got = outs[w].at[2 * px + py, mine]
                _rcopy(got, got, ssem.at[k], rsem.at[k], (px, py, c)).wait_recv()
                fwds.append(_rcopy(got, got, fssem.at[k], frsem.at[k], sibling))
                fwds[-1].start()
        for w in range(n):
            for d in (1, 2, 3):
                px, py = _other_chip(x, y, d)
                k = 3 * w + d - 1
                got = outs[w].at[2 * px + py, other]
                _rcopy(got, got, fssem.at[k], frsem.at[k], sibling).wait_recv()
        for cp in mirror:
            cp.wait_recv()
        for cp in sends + fwds + mirror:
            cp.wait_send()

    sems = [pltpu.SemaphoreType.DMA((3 * n,))] * 4 + [pltpu.SemaphoreType.DMA((n,))] * 2
    return pl.pallas_call(
        body, name="all_gather_weights", in_specs=[ANY] * n, out_specs=[ANY] * n,
        out_shape=[jax.ShapeDtypeStruct((N_CHIPS,) + s.shape, s.dtype) for s in shards], scratch_shapes=sems,
    )(*shards)


def _sibling_exchange(gs):
    n = len(gs)
    Lh = gs[0].shape[1] // 2

    def body(*refs):
        ins, outs = refs[:n], refs[n:2 * n]
        ssem, rsem = refs[2 * n:]
        x, y, c = _place()
        other = pl.ds((1 - c) * Lh, Lh)
        cps = [_rcopy(ins[w].at[:, other], outs[w], ssem.at[w], rsem.at[w], (x, y, 1 - c)) for w in range(n)]
        for cp in cps:
            cp.start()
        for cp in cps:
            cp.wait_recv()
        for cp in cps:
            cp.wait_send()

    return pl.pallas_call(
        body, name="grad_sibling_exchange", in_specs=[ANY] * n, out_specs=[ANY] * n,
        out_shape=[jax.ShapeDtypeStruct((N_CHIPS, Lh) + g.shape[2:], g.dtype) for g in gs],
        scratch_shapes=[pltpu.SemaphoreType.DMA((n,))] * 2,
    )(*gs)


def _chip_exchange(ps):
    n = len(ps)

    def body(*refs):
        ins, outs = refs[:n], refs[n:2 * n]
        ssem, rsem = refs[2 * n:]
        x, y, c = _place()
        j = 2 * x + y
        cps = []
        for w in range(n):
            for d in (1, 2, 3):
                px, py = _other_chip(x, y, d)
                k = 3 * w + d - 1
                cps.append(_rcopy(ins[w].at[2 * px + py], outs[w].at[j], ssem.at[k], rsem.at[k], (px, py, c)))
                cps[-1].start()
        for cp in cps:
            cp.wait_recv()
        for cp in cps:
            cp.wait_send()

    return pl.pallas_call(
        body, name="grad_chip_exchange", in_specs=[ANY] * n, out_specs=[ANY] * n,
        out_shape=[jax.ShapeDtypeStruct(p.shape, p.dtype) for p in ps],
        scratch_shapes=[pltpu.SemaphoreType.DMA((3 * n,))] * 2,
    )(*ps)


def _sibling_share(fs):
    n = len(fs)
    Lh = fs[0].shape[0] // 2

    def body(*refs):
        outs = refs[n:2 * n]
        ssem, rsem = refs[2 * n:]
        x, y, c = _place()
        mine = pl.ds(c * Lh, Lh)
        cps = [_rcopy(outs[w].at[mine], outs[w].at[mine], ssem.at[w], rsem.at[w], (x, y, 1 - c)) for w in range(n)]
        for cp in cps:
            cp.start()
        for cp in cps:
            cp.wait_recv()
        for cp in cps:
            cp.wait_send()

    return pl.pallas_call(
        body, name="grad_sibling_share", in_specs=[ANY] * n, out_specs=[ANY] * n,
        out_shape=[jax.ShapeDtypeStruct(f.shape, f.dtype) for f in fs],
        input_output_aliases={w: w for w in range(n)},
        scratch_shapes=[pltpu.SemaphoreType.DMA((n,))] * 2,
    )(*fs)


def _sum_rows(a, b):
    return _tile(a, max(16, (1 << 19) // b // 16 * 16), 16)


def _pair_sum(g, recv, place, name):
    _, _, a, b = g.shape
    Lh = recv.shape[1]
    ta = _sum_rows(a, b)

    def body(p_ref, g_ref, r_ref, o_ref):
        o_ref[...] = (g_ref[...].astype(F32) + r_ref[...].astype(F32)).astype(o_ref.dtype)

    blk = (None, None, ta, b)
    return pl.pallas_call(
        body, name=name,
        grid_spec=pltpu.PrefetchScalarGridSpec(
            num_scalar_prefetch=1, grid=(N_CHIPS, Lh, a // ta),
            in_specs=[pl.BlockSpec(blk, lambda j, l, r, p: (j, p[0] * Lh + l, r, 0)),
                      pl.BlockSpec(blk, lambda j, l, r, p: (j, l, r, 0))],
            out_specs=pl.BlockSpec(blk, lambda j, l, r, p: (j, l, r, 0))),
        out_shape=jax.ShapeDtypeStruct(recv.shape, recv.dtype),
        compiler_params=_params("parallel", "parallel", "parallel"),
    )(place, g, recv)


def _chip_sum(slots, part, place, name):
    _, Lh, a, b = slots.shape
    ta = _sum_rows(a, b)

    def body(p_ref, s_ref, own_ref, o_ref):
        j = p_ref[1]
        own = own_ref[...].astype(F32)
        term = [jnp.where(j == s, own, s_ref[s].astype(F32)) for s in range(N_CHIPS)]
        o_ref[...] = ((term[0] + term[1]) + term[2]) + term[3]

    return pl.pallas_call(
        body, name=name,
        grid_spec=pltpu.PrefetchScalarGridSpec(
            num_scalar_prefetch=1, grid=(Lh, a // ta),
            in_specs=[pl.BlockSpec((N_CHIPS, None, ta, b), lambda l, r, p: (0, l, r, 0)),
                      pl.BlockSpec((None, None, ta, b), lambda l, r, p: (p[1], l, r, 0))],
            out_specs=pl.BlockSpec((None, ta, b), lambda l, r, p: (p[0] * Lh + l, r, 0))),
        out_shape=jax.ShapeDtypeStruct((2 * Lh, a, b), F32),
        compiler_params=_params("parallel", "parallel"),
    )(place, slots, part)


def _all_reduce_small(v):
    Rr = v.shape[0]

    def body(v_ref, slots_ref, out_ref, ssem, rsem):
        x, y, c = _place()
        me = 4 * x + 2 * y + c
        slots_ref[pl.ds(me, 1)] = v_ref[...][None]
        cps = []
        for d in range(1, N_DEV):
            px, py = _other_chip(x, y, d >> 1)
            pc = 1 - c if d & 1 else c
            cps.append(_rcopy(v_ref, slots_ref.at[me], ssem.at[d - 1], rsem.at[d - 1], (px, py, pc)))
            cps[-1].start()
        for cp in cps:
            cp.wait_recv()
        for cp in cps:
            cp.wait_send()
        acc = slots_ref[0]
        for s in range(1, N_DEV):
            acc = acc + slots_ref[s]
        out_ref[...] = acc

    vm = pl.BlockSpec(memory_space=pltpu.VMEM)
    return pl.pallas_call(
        body, name="all_reduce_small", in_specs=[vm], out_specs=[vm, vm],
        out_shape=[jax.ShapeDtypeStruct((N_DEV, Rr, LANES), F32), jax.ShapeDtypeStruct((Rr, LANES), F32)],
        scratch_shapes=[pltpu.SemaphoreType.DMA((N_DEV - 1,)), pltpu.SemaphoreType.DMA((N_DEV - 1,))],
        compiler_params=pltpu.CompilerParams(vmem_limit_bytes=VMEM_LIMIT_BYTES),
    )(v)[1]


def _ffn_forward(x, ln, wgu, wd, tag):
    h = _norm_fwd(x, ln, f"{tag}_norm")
    gu, act = _ffn_up(h, wgu, f"{tag}_up")
    x_out = _mm_nn(act, wd, f"{tag}_down", F32, res=x, scale=0.5)
    return x_out, (x, h, gu)


def _ffn_backward(dxo, saved, ln, wgu, wd, tag):
    x, h, gu = saved
    dgu, act = _ffn_down_bwd(dxo, wd, gu, f"{tag}_down_bwd")
    d_wd = _mm_tn(act, dxo, f"{tag}_dwd", scale=0.5)
    d_wgu = _mm_tn(h, dgu, f"{tag}_dwgu", tn_target=1408, tm_target=1024)
    dx, d_ln = _mm_nt_norm_bwd(dgu, wgu, x, ln, dxo, f"{tag}_dh_norm_bwd")
    return dx, d_ln, d_wgu, d_wd


def _mixer_forward(x, p, tabs):
    h = _norm_fwd(x, p["ln_mix"], "mix_norm")
    zu, zqk, zv, zg = _mm_in(h, p["w_in"], "mix_in")
    pm = _pool_fwd(zu, p["pool_w"], p["pool_scale"], "pool_fwd")
    qkn = _qk_fwd(zqk, p["gqk"], *tabs, "qk_fwd")
    o = _attn_fwd(qkn, zv, p["sinks"], "attn_fwd")
    a, b, m = _merge_fwd(pm, o, p["w_pool_branch"], p["w_attn_branch"], zg, "merge_fwd")
    x_out = _mm_nn(m, p["w_out"], "mix_out", F32, res=x, scale=1.0)
    return x_out, (x, h, zu, zqk, zv, zg, pm, qkn, o, a, b, m)


def _shift_up(v):
    return jnp.concatenate([v[BLOCK:], jnp.zeros((BLOCK, v.shape[1]), v.dtype)], axis=0)


def _mixer_backward(dxo, saved, p, tabs):
    x, h, zu, zqk, zv, zg, pm, qkn, o, a, b, m = saved
    g = {}
    d_a, d_b, dgl = _merge_bwd(dxo, p["w_out"], a, b, zg, "merge_bwd")
    g["w_out"] = _mm_tn(m, dxo, "mix_dwout")
    dpm = _mm_nt(d_a, p["w_pool_branch"], "pool_branch_dx", CDT)
    g["w_pool_branch"] = _mm_tn(pm, d_a, "pool_branch_dw")
    do = _mm_nt(d_b, p["w_attn_branch"], "attn_branch_dx", CDT)
    g["w_attn_branch"] = _mm_tn(o, d_b, "attn_branch_dw")
    du, g["pool_w"], g["pool_scale"] = _pool_bwd(zu, dpm, p["pool_w"], p["pool_scale"], "pool_bwd")
    dq, dkc, dkp, dvc, dvp, dsink = _attn_bwd(qkn, zv, p["sinks"], do, "attn_bwd")
    dqk = jnp.concatenate([dq, dkc + _shift_up(dkp)], axis=1)
    dv = dvc + _shift_up(dvp)
    dzqk, dgqk = _qk_bwd(dqk, zqk, p["gqk"], *tabs, "qk_bwd")
    g["q_norm"] = dgqk[0, :ATTN_DIM].reshape(N_Q_HEADS, HEAD_DIM).sum(axis=0)
    g["k_norm"] = dgqk[0, ATTN_DIM:].reshape(KV_DIM // HEAD_DIM, HEAD_DIM).sum(axis=0)
    g["sinks"] = -dsink[:, 0]
    dz = jnp.concatenate([du, dzqk, dv.astype(CDT), dgl], axis=1)
    g["w_in"] = _mm_tn(h, dz, "mix_dwin")
    dx, d_ln = _mm_nt_norm_bwd(dz[None], p["w_in"], x, p["ln_mix"], dxo, "mix_dh_norm_bwd")
    g["ln_mix"] = d_ln[0]
    return dx, g


def _local_step(x, tgt, layers):
    T = x.shape[0]
    tabs = _rope_tables(T)
    saved = []
    for p in layers:
        x, s1 = _ffn_forward(x, p["ln_ffn1"], p["w_ffn1_gu"], p["w_ffn1_down"], "ffn1")
        x, s2 = _mixer_forward(x, p, tabs)
        x, s3 = _ffn_forward(x, p["ln_ffn2"], p["w_ffn2_gu"], p["w_ffn2_down"], "ffn2")
        saved.append((s1, s2, s3))
    dx, loss = _loss_head(x, tgt, "loss_head")
    grads = [None] * len(layers)
    for l in reversed(range(len(layers))):
        p = layers[l]
        s1, s2, s3 = saved[l]
        dx, d_ln2, d_gu2, d_dn2 = _ffn_backward(dx, s3, p["ln_ffn2"], p["w_ffn2_gu"], p["w_ffn2_down"], "ffn2")
        dx, g = _mixer_backward(dx, s2, p, tabs)
        dx, d_ln1, d_gu1, d_dn1 = _ffn_backward(dx, s1, p["ln_ffn1"], p["w_ffn1_gu"], p["w_ffn1_down"], "ffn1")
        g.update(ln_ffn1=d_ln1[0], w_ffn1_gu=d_gu1, w_ffn1_down=d_dn1, ln_ffn2=d_ln2[0], w_ffn2_gu=d_gu2, w_ffn2_down=d_dn2)
        grads[l] = g
    return loss, dx, grads


def _full_from_gathered(name, gathered, l):
    blk = gathered[:, l]
    if name in COL_SHARDED:
        return jnp.transpose(blk, (1, 0, 2)).reshape(blk.shape[1], N_CHIPS * blk.shape[2])
    return blk.reshape(N_CHIPS * blk.shape[1], blk.shape[2])


def _shard_major(name, per_layer):
    full = jnp.stack(per_layer)
    L, K, N = full.shape
    if name in COL_SHARDED:
        return jnp.transpose(full.reshape(L, K, N_CHIPS, N // N_CHIPS), (2, 0, 1, 3))
    return jnp.transpose(full.reshape(L, N_CHIPS, K // N_CHIPS, N), (1, 0, 2, 3))


def _pack_small(parts):
    rows, spans, lo = [], [], 0
    for v in parts:
        flat = v.reshape(-1)
        nrow = -(-flat.shape[0] // LANES)
        flat = jnp.pad(flat, (0, nrow * LANES - flat.shape[0]))
        rows.append(flat.reshape(nrow, LANES))
        spans.append((lo, nrow))
        lo += nrow
    pad = -lo % 8
    if pad:
        rows.append(jnp.zeros((pad, LANES), F32))
    return jnp.concatenate(rows, axis=0), spans


def _unpack_small(packed, spans, shapes):
    out = []
    for (lo, nrow), shape in zip(spans, shapes):
        size = 1
        for s in shape:
            size *= s
        out.append(packed[lo:lo + nrow].reshape(-1)[:size].reshape(shape))
    return out


def kernel(x, ln_ffn1, w_ffn1_gu, w_ffn1_down, ln_mix, w_in, pool_w, pool_scale, w_pool_branch, q_norm, k_norm, sinks, w_attn_branch, w_out, ln_ffn2, w_ffn2_gu, w_ffn2_down, loss_target, m_ln_ffn1, m_w_ffn1_gu, m_w_ffn1_down, m_ln_mix, m_w_in, m_pool_w, m_pool_scale, m_w_pool_branch, m_q_norm, m_k_norm, m_sinks, m_w_attn_branch, m_w_out, m_ln_ffn2, m_w_ffn2_gu, m_w_ffn2_down, v_ln_ffn1, v_w_ffn1_gu, v_w_ffn1_down, v_ln_mix, v_w_in, v_pool_w, v_pool_scale, v_w_pool_branch, v_q_norm, v_k_norm, v_sinks, v_w_attn_branch, v_w_out, v_ln_ffn2, v_w_ffn2_gu, v_w_ffn2_down):
    w = dict(ln_ffn1=ln_ffn1, w_ffn1_gu=w_ffn1_gu, w_ffn1_down=w_ffn1_down, ln_mix=ln_mix, w_in=w_in, pool_w=pool_w,
             pool_scale=pool_scale, w_pool_branch=w_pool_branch, q_norm=q_norm, k_norm=k_norm, sinks=sinks,
             w_attn_branch=w_attn_branch, w_out=w_out, ln_ffn2=ln_ffn2, w_ffn2_gu=w_ffn2_gu, w_ffn2_down=w_ffn2_down)
    mom = dict(ln_ffn1=m_ln_ffn1, w_ffn1_gu=m_w_ffn1_gu, w_ffn1_down=m_w_ffn1_down, ln_mix=m_ln_mix, w_in=m_w_in,
               pool_w=m_pool_w, pool_scale=m_pool_scale, w_pool_branch=m_w_pool_branch, q_norm=m_q_norm, k_norm=m_k_norm,
               sinks=m_sinks, w_attn_branch=m_w_attn_branch, w_out=m_w_out, ln_ffn2=m_ln_ffn2, w_ffn2_gu=m_w_ffn2_gu,
               w_ffn2_down=m_w_ffn2_down)
    var = dict(ln_ffn1=v_ln_ffn1, w_ffn1_gu=v_w_ffn1_gu, w_ffn1_down=v_w_ffn1_down, ln_mix=v_ln_mix, w_in=v_w_in,
               pool_w=v_pool_w, pool_scale=v_pool_scale, w_pool_branch=v_w_pool_branch, q_norm=v_q_norm, k_norm=v_k_norm,
               sinks=v_sinks, w_attn_branch=v_w_attn_branch, w_out=v_w_out, ln_ffn2=v_ln_ffn2, w_ffn2_gu=v_w_ffn2_gu,
               w_ffn2_down=v_w_ffn2_down)
    L = ln_ffn1.shape[0]

    gathered = dict(zip(BIG, _all_gather_weights([w[n].astype(CDT) for n in BIG])))
    layers = []
    for l in range(L):
        p = {n: _full_from_gathered(n, gathered[n], l) for n in BIG}
        p.update(ln_ffn1=ln_ffn1[l], ln_mix=ln_mix[l], ln_ffn2=ln_ffn2[l], pool_w=pool_w[l].astype(CDT),
                 pool_scale=pool_scale[l], sinks=sinks[l],
                 gqk=jnp.concatenate([jnp.tile(q_norm[l], N_Q_HEADS), jnp.tile(k_norm[l], KV_DIM // HEAD_DIM)]).reshape(1, QK_DIM))
        layers.append(p)

    loss_part, grad_x, grads = _local_step(x[0], loss_target[0], layers)

    place = jnp.stack([lax.axis_index("c"), 2 * lax.axis_index("x") + lax.axis_index("y")]).astype(jnp.int32)
    g_major = [_shard_major(n, [g[n] for g in grads]).astype(WIRE_DT) for n in BIG]
    from_sibling = _sibling_exchange(g_major)
    chip_part = [_pair_sum(g, r, place, "grad_pair_sum") for g, r in zip(g_major, from_sibling)]
    slots = _chip_exchange(chip_part)
    reduced_half = [_chip_sum(s, p, place, "grad_chip_sum") for s, p in zip(slots, chip_part)]
    g_big = dict(zip(BIG, _sibling_share(reduced_half)))

    small_parts = [jnp.stack([g[n] for g in grads]) for n in SMALL] + [loss_part]
    packed, spans = _pack_small(small_parts)
    summed = _all_reduce_small(packed)
    *g_small_list, loss_sum = _unpack_small(summed, spans, [w[n].shape for n in SMALL] + [(1, 1)])
    g_small = dict(zip(SMALL, g_small_list))
    loss = loss_sum[0, 0]

    grad_out, delta, new_m, new_v = {}, {}, {}, {}
    for n in BIG:
        shape = w[n].shape
        flat = (shape[0] * shape[1], shape[2])
        grad_out[n] = g_big[n]
        d, nm, nv = _adamw(w[n].reshape(flat), g_big[n].reshape(flat), mom[n].reshape(flat), var[n].reshape(flat), "adamw")
        delta[n], new_m[n], new_v[n] = d.reshape(shape), nm.reshape(shape), nv.reshape(shape)
    pw, _ = _pack_small([w[n] for n in SMALL])
    pg, sp = _pack_small([g_small[n] for n in SMALL])
    pm_, _ = _pack_small([mom[n] for n in SMALL])
    pv, _ = _pack_small([var[n] for n in SMALL])
    d, nm, nv = _adamw(pw, pg, pm_, pv, "adamw_small")
    shapes = [w[n].shape for n in SMALL]
    for n, dv, mv, vv in zip(SMALL, _unpack_small(d, sp, shapes), _unpack_small(nm, sp, shapes), _unpack_small(nv, sp, shapes)):
        grad_out[n], delta[n], new_m[n], new_v[n] = g_small[n], dv, mv, vv

    return (loss, grad_x[None], *[grad_out[n] for n in WEIGHTS], *[delta[n] for n in WEIGHTS],
            *[new_m[n] for n in WEIGHTS], *[new_v[n] for n in WEIGHTS])
```

```python
import functools

import jax
import jax.numpy as jnp
from jax import lax
from jax.experimental import pallas as pl
from jax.experimental.pallas import tpu as pltpu

F32 = jnp.float32
CDT = jnp.bfloat16
WIRE_DT = jnp.bfloat16

D_MODEL = 1024
POOL_WINDOWS = (2, 4, 8, 16)
POOL_WMAX = 16
GROUP = 128
POOL_DIM = 512
HEAD_DIM = 64
N_Q_HEADS = 8
ATTN_DIM = 512
KV_DIM = 128
QK_DIM = ATTN_DIM + KV_DIM
GATE_DIM = 2 * D_MODEL
BLOCK = 128
ROPE_THETA = 500000.0
ROT_DIM = 16
EPS = 1e-6
ATTN_SCALE = HEAD_DIM ** -0.5

ADAM_LR = 0.001
ADAM_B1 = 0.9
ADAM_B2 = 0.999
ADAM_EPS = 1e-08
ADAM_WD = 0.01
ADAM_STEP = 10

N_CHIPS = 4
N_DEV = 8
LANES = 128
VMEM_LIMIT_BYTES = 48 * 1024 * 1024

MESH = pl.DeviceIdType.MESH
ANY = pl.BlockSpec(memory_space=pl.ANY)

BIG = ("w_ffn1_gu", "w_ffn1_down", "w_in", "w_pool_branch", "w_attn_branch", "w_out", "w_ffn2_gu", "w_ffn2_down")
COL_SHARDED = ("w_ffn1_gu", "w_in", "w_pool_branch", "w_attn_branch", "w_ffn2_gu")
SMALL = ("ln_ffn1", "ln_mix", "pool_w", "pool_scale", "q_norm", "k_norm", "sinks", "ln_ffn2")
WEIGHTS = ("ln_ffn1", "w_ffn1_gu", "w_ffn1_down", "ln_mix", "w_in", "pool_w", "pool_scale", "w_pool_branch",
           "q_norm", "k_norm", "sinks", "w_attn_branch", "w_out", "ln_ffn2", "w_ffn2_gu", "w_ffn2_down")


def _tile(n, target, mult=8):
    if n <= target:
        return n
    for t in range(target - target % mult, 0, -mult):
        if n % t == 0:
            return t
    raise ValueError((n, target, mult))


def _params(*sem):
    return pltpu.CompilerParams(dimension_semantics=sem, vmem_limit_bytes=VMEM_LIMIT_BYTES)


def _sigmoid(v):
    return 0.5 * jnp.tanh(0.5 * v) + 0.5


def _dot(a, b):
    return jnp.dot(a, b, preferred_element_type=F32)


def _dot_nt(a, b):
    return lax.dot_general(a, b, (((1,), (1,)), ((), ())), preferred_element_type=F32)


def _dot_tn(a, b):
    return lax.dot_general(a, b, (((0,), (0,)), ((), ())), preferred_element_type=F32)


class _Side:
    def __init__(self, ins, out_shapes, n_sems, issue, aliases=None):
        self.ins, self.out_shapes, self.n_sems, self.issue = list(ins), list(out_shapes), n_sems, issue
        self.aliases = dict(aliases or {})
        self.outs = None


def _pcall(body, name, grid, in_specs, out_specs, out_shape, args, dims, side=None):
    if side is None:
        return pl.pallas_call(body, name=name, grid=grid, in_specs=in_specs, out_specs=out_specs, out_shape=out_shape,
                              compiler_params=_params(*dims))(*args)
    n_in, n_out, s_in, s_out = len(in_specs), len(out_specs), len(side.ins), len(side.out_shapes)

    def wrapped(*refs):
        main_in, side_in = refs[:n_in], refs[n_in:n_in + s_in]
        main_out = refs[n_in + s_in:n_in + s_in + n_out]
        side_out = refs[n_in + s_in + n_out:n_in + s_in + n_out + s_out]
        ssem, rsem = refs[n_in + s_in + n_out + s_out:]
        ids = [pl.program_id(ax) for ax in range(len(grid))]
        first = functools.reduce(jnp.logical_and, [i == 0 for i in ids])
        last = functools.reduce(jnp.logical_and, [i == g - 1 for i, g in zip(ids, grid)])

        @pl.when(first)
        def _():
            for cp in side.issue(side_in, side_out, ssem, rsem):
                cp.start()

        body(*main_in, *main_out)

        @pl.when(last)
        def _():
            cps = side.issue(side_in, side_out, ssem, rsem)
            for cp in cps:
                cp.wait_recv()
            for cp in cps:
                cp.wait_send()

    outs = pl.pallas_call(
        wrapped, name=name, grid=grid, in_specs=list(in_specs) + [ANY] * s_in, out_specs=list(out_specs) + [ANY] * s_out,
        out_shape=list(out_shape) + side.out_shapes,
        input_output_aliases={n_in + i: n_out + o for i, o in side.aliases.items()},
        scratch_shapes=[pltpu.SemaphoreType.DMA((side.n_sems,))] * 2,
        compiler_params=_params(*["arbitrary"] * len(grid)),
    )(*args, *side.ins)
    side.outs = list(outs[n_out:])
    return list(outs[:n_out])


def _run_side(side, name):
    s_in = len(side.ins)

    def body(*refs):
        ssem, rsem = refs[s_in + len(side.out_shapes):]
        cps = side.issue(refs[:s_in], refs[s_in:s_in + len(side.out_shapes)], ssem, rsem)
        for cp in cps:
            cp.start()
        for cp in cps:
            cp.wait_recv()
        for cp in cps:
            cp.wait_send()

    side.outs = list(pl.pallas_call(
        body, name=name, in_specs=[ANY] * s_in, out_specs=[ANY] * len(side.out_shapes), out_shape=side.out_shapes,
        input_output_aliases=side.aliases, scratch_shapes=[pltpu.SemaphoreType.DMA((side.n_sems,))] * 2,
    )(*side.ins))
    return side.outs


def _norm_fwd(x, g, name):
    T, Dm = x.shape
    tm = _tile(T, 512)

    def body(x_ref, g_ref, h_ref):
        xv = x_ref[...]
        r = lax.rsqrt(jnp.mean(xv * xv, axis=-1, keepdims=True) + EPS)
        h_ref[...] = (xv * r * g_ref[...]).astype(h_ref.dtype)

    row = pl.BlockSpec((tm, Dm), lambda i: (i, 0))
    return pl.pallas_call(
        body, name=name, grid=(T // tm,),
        in_specs=[row, pl.BlockSpec((1, Dm), lambda i: (0, 0))], out_specs=row,
        out_shape=jax.ShapeDtypeStruct((T, Dm), CDT), compiler_params=_params("parallel"),
    )(x, g.reshape(1, Dm))


def _loss_head(y, tgt, name):
    T, Dm = y.shape
    tm = _tile(T, 512)

    def body(y_ref, t_ref, dy_ref, loss_ref):
        @pl.when(pl.program_id(0) == 0)
        def _():
            loss_ref[...] = jnp.zeros_like(loss_ref)

        diff = y_ref[...] - t_ref[...]
        dy_ref[...] = diff * (1.0 / Dm)
        part = jnp.sum(jnp.mean(diff * diff, axis=-1, keepdims=True), axis=0, keepdims=True)
        loss_ref[...] += 0.5 * part

    row = pl.BlockSpec((tm, Dm), lambda i: (i, 0))
    one = pl.BlockSpec((1, 1), lambda i: (0, 0))
    return pl.pallas_call(
        body, name=name, grid=(T // tm,),
        in_specs=[row, row], out_specs=[row, one],
        out_shape=[jax.ShapeDtypeStruct((T, Dm), F32), jax.ShapeDtypeStruct((1, 1), F32)],
        compiler_params=_params("arbitrary"),
    )(y, tgt)


def _mm_nn(a, b, name, out_dtype, res=None, scale=1.0, tm_target=512, side=None):
    M, K = a.shape
    N = b.shape[1]
    tm = _tile(M, tm_target)

    def body(a_ref, b_ref, *rest):
        acc = _dot(a_ref[...].astype(CDT), b_ref[...])
        if res is None:
            (o_ref,) = rest
        else:
            r_ref, o_ref = rest
            acc = r_ref[...] + scale * acc
        o_ref[...] = acc.astype(o_ref.dtype)

    in_specs = [pl.BlockSpec((tm, K), lambda i: (i, 0)), pl.BlockSpec((K, N), lambda i: (0, 0))]
    args = [a, b]
    if res is not None:
        in_specs.append(pl.BlockSpec((tm, N), lambda i: (i, 0)))
        args.append(res)
    return _pcall(body, name, (M // tm,), in_specs, [pl.BlockSpec((tm, N), lambda i: (i, 0))],
                  [jax.ShapeDtypeStruct((M, N), out_dtype)], args, ("parallel",), side)[0]


def _mm_nt(a, b, name, out_dtype, tm_target=512):
    M, K = a.shape
    N = b.shape[0]
    tm = _tile(M, tm_target)

    def body(a_ref, b_ref, o_ref):
        o_ref[...] = _dot_nt(a_ref[...].astype(CDT), b_ref[...]).astype(o_ref.dtype)

    return pl.pallas_call(
        body, name=name, grid=(M // tm,),
        in_specs=[pl.BlockSpec((tm, K), lambda i: (i, 0)), pl.BlockSpec((N, K), lambda i: (0, 0))],
        out_specs=pl.BlockSpec((tm, N), lambda i: (i, 0)),
        out_shape=jax.ShapeDtypeStruct((M, N), out_dtype), compiler_params=_params("parallel"),
    )(a, b)


def _mm_tn(x, dy, name, scale=1.0, tn_target=1664, tm_target=1408, tk_target=1024):
    T, M = x.shape
    split = dy.ndim == 3
    Nh = dy.shape[-1]
    N = 2 * Nh if split else Nh
    tm = _tile(M, tm_target, LANES)
    tn = _tile(Nh, tn_target, LANES)
    tk = _tile(T, tk_target)
    nk = T // tk
    njh = Nh // tn

    def body(x_ref, dy_ref, o_ref):
        @pl.when(pl.program_id(2) == 0)
        def _():
            o_ref[...] = jnp.zeros_like(o_ref)

        part = _dot_tn(x_ref[...].astype(CDT), dy_ref[...].astype(CDT))
        o_ref[...] += part if scale == 1.0 else scale * part

    if split:
        dy_spec = pl.BlockSpec((None, tk, tn), lambda i, j, k: (j // njh, k, j % njh))
    else:
        dy_spec = pl.BlockSpec((tk, tn), lambda i, j, k: (k, j))
    return pl.pallas_call(
        body, name=name, grid=(M // tm, N // tn, nk),
        in_specs=[pl.BlockSpec((tk, tm), lambda i, j, k: (k, i)), dy_spec],
        out_specs=pl.BlockSpec((tm, tn), lambda i, j, k: (i, j)),
        out_shape=jax.ShapeDtypeStruct((M, N), F32),
        compiler_params=_params("parallel", "parallel", "arbitrary"),
    )(x, dy)


def _ffn_up(h, wgu, name, side=None):
    T, Dm = h.shape
    Fd = wgu.shape[1] // 2
    tm = _tile(T, 512)
    tn = _tile(Fd, 1408, LANES)
    nj = Fd // tn

    def body(h_ref, wg_ref, wu_ref, gu_ref, a_ref):
        hv = h_ref[...]
        g = _dot(hv, wg_ref[...])
        u = _dot(hv, wu_ref[...])
        gu_ref[0] = g.astype(gu_ref.dtype)
        gu_ref[1] = u.astype(gu_ref.dtype)
        a_ref[...] = (g * _sigmoid(g) * u).astype(a_ref.dtype)

    return _pcall(
        body, name, (T // tm, nj),
        [pl.BlockSpec((tm, Dm), lambda i, j: (i, 0)), pl.BlockSpec((Dm, tn), lambda i, j: (0, j)),
         pl.BlockSpec((Dm, tn), lambda i, j: (0, j + nj))],
        [pl.BlockSpec((2, tm, tn), lambda i, j: (0, i, j)), pl.BlockSpec((tm, tn), lambda i, j: (i, j))],
        [jax.ShapeDtypeStruct((2, T, Fd), CDT), jax.ShapeDtypeStruct((T, Fd), CDT)],
        (h, wgu, wgu), ("parallel", "parallel"), side)


def _ffn_down_bwd(dxo, wd, gu, name, side=None):
    T, Dm = dxo.shape
    Fd = wd.shape[0]
    tm = _tile(T, 512)
    tn = _tile(Fd, 1408, LANES)

    def body(dx_ref, wd_ref, gu_ref, dgu_ref, a_ref):
        da = 0.5 * _dot_nt(dx_ref[...].astype(CDT), wd_ref[...])
        g = gu_ref[0].astype(F32)
        u = gu_ref[1].astype(F32)
        sg = _sigmoid(g)
        silu = g * sg
        a_ref[...] = (silu * u).astype(a_ref.dtype)
        dgu_ref[0] = (da * u * (sg * (1.0 + g * (1.0 - sg)))).astype(dgu_ref.dtype)
        dgu_ref[1] = (da * silu).astype(dgu_ref.dtype)

    gu_spec = pl.BlockSpec((2, tm, tn), lambda i, j: (0, i, j))
    return _pcall(
        body, name, (T // tm, Fd // tn),
        [pl.BlockSpec((tm, Dm), lambda i, j: (i, 0)), pl.BlockSpec((tn, Dm), lambda i, j: (j, 0)), gu_spec],
        [gu_spec, pl.BlockSpec((tm, tn), lambda i, j: (i, j))],
        [jax.ShapeDtypeStruct((2, T, Fd), CDT), jax.ShapeDtypeStruct((T, Fd), CDT)],
        (dxo, wd, gu), ("parallel", "parallel"), side)


def _mm_nt_norm_bwd(a3, b, x, g, dres, name):
    S, T, Kh = a3.shape
    Dm = b.shape[0]
    tm = _tile(T, 256)

    def body(a_ref, b_ref, x_ref, g_ref, dres_ref, dx_ref, dg_ref):
        @pl.when(pl.program_id(0) == 0)
        def _():
            dg_ref[...] = jnp.zeros_like(dg_ref)

        dh = _dot_nt(a_ref[0], b_ref[:, :Kh])
        for s_ in range(1, S):
            dh = dh + _dot_nt(a_ref[s_], b_ref[:, s_ * Kh:(s_ + 1) * Kh])
        xv = x_ref[...]
        r = lax.rsqrt(jnp.mean(xv * xv, axis=-1, keepdims=True) + EPS)
        xh = xv * r
        dg_ref[...] += jnp.sum(dh * xh, axis=0, keepdims=True)
        dxh = dh * g_ref[...]
        dx_ref[...] = dres_ref[...] + r * (dxh - xh * jnp.mean(dxh * xh, axis=-1, keepdims=True))

    row = pl.BlockSpec((tm, Dm), lambda i: (i, 0))
    vec = pl.BlockSpec((1, Dm), lambda i: (0, 0))
    return pl.pallas_call(
        body, name=name, grid=(T // tm,),
        in_specs=[pl.BlockSpec((S, tm, Kh), lambda i: (0, i, 0)),
                  pl.BlockSpec(b.shape, lambda i: (0, 0), pipeline_mode=pl.Buffered(1)), row, vec, row],
        out_specs=[row, vec],
        out_shape=[jax.ShapeDtypeStruct((T, Dm), F32), jax.ShapeDtypeStruct((1, Dm), F32)],
        compiler_params=_params("arbitrary"),
    )(a3, b, x, g.reshape(1, Dm), dres)


def _mm_in(h, w_in, name):
    T, Dm = h.shape
    tm = _tile(T, 256)
    widths = (POOL_DIM, QK_DIM, KV_DIM, GATE_DIM)

    def body(h_ref, w_ref, *outs):
        z = _dot(h_ref[...], w_ref[...])
        lo = 0
        for o_ref, wd in zip(outs, widths):
            o_ref[...] = z[:, lo:lo + wd]
            lo += wd

    return pl.pallas_call(
        body, name=name, grid=(T // tm,),
        in_specs=[pl.BlockSpec((tm, Dm), lambda i: (i, 0)), pl.BlockSpec(w_in.shape, lambda i: (0, 0))],
        out_specs=[pl.BlockSpec((tm, wd), lambda i: (i, 0)) for wd in widths],
        out_shape=[jax.ShapeDtypeStruct((T, wd), F32) for wd in widths],
        compiler_params=_params("parallel"),
    )(h, w_in)


def _window_mean_minus_token(ext, u, g, w, pos):
    sl = slice(g * GROUP, (g + 1) * GROUP)
    s = ext[:, sl]
    span = 1
    while span < w:
        s = s + pltpu.roll(s, span, axis=0)
        span *= 2
    cnt = jnp.minimum(pos + 1, w).astype(F32)
    return s[POOL_WMAX:, :] / cnt - u[:, sl]


def _pool_fwd(zu, pool_w, scale, name):
    T = zu.shape[0]
    tm = _tile(T, 512, POOL_WMAX)
    hb = tm // POOL_WMAX

    def body(u_ref, halo_ref, pw_ref, sc_ref, pm_ref):
        i = pl.program_id(0)
        u = u_ref[...]
        halo = jnp.where(i > 0, halo_ref[...], 0.0)
        ext = jnp.concatenate([halo, u], axis=0)
        pos = i * tm + lax.broadcasted_iota(jnp.int32, (tm, 1), 0)
        ys = []
        for g, w in enumerate(POOL_WINDOWS):
            d = _window_mean_minus_token(ext, u, g, w, pos)
            ys.append(_dot(d.astype(CDT), pw_ref[g]))
        pm_ref[...] = (jnp.concatenate(ys, axis=1) * sc_ref[...]).astype(pm_ref.dtype)

    row = pl.BlockSpec((tm, POOL_DIM), lambda i: (i, 0))
    return pl.pallas_call(
        body, name=name, grid=(T // tm,),
        in_specs=[row, pl.BlockSpec((POOL_WMAX, POOL_DIM), lambda i: (jnp.maximum(i * hb - 1, 0), 0)),
                  pl.BlockSpec(pool_w.shape, lambda i: (0, 0, 0)), pl.BlockSpec((1, POOL_DIM), lambda i: (0, 0))],
        out_specs=row, out_shape=jax.ShapeDtypeStruct((T, POOL_DIM), CDT),
        compiler_params=_params("parallel"),
    )(zu, zu, pool_w, scale.reshape(1, POOL_DIM))


def _pool_bwd(zu, dpm, pool_w, scale, name):
    T = zu.shape[0]
    tm = _tile(T, 512, POOL_WMAX)
    hb = tm // POOL_WMAX
    nsteps = T // tm
    ext_rows = tm + POOL_WMAX

    def body(u_ref, halo_ref, dpm_ref, dnext_ref, pw_ref, sc_ref, du_ref, dpw_ref, dsc_ref):
        i = pl.program_id(0)

        @pl.when(i == 0)
        def _():
            dpw_ref[...] = jnp.zeros_like(dpw_ref)
            dsc_ref[...] = jnp.zeros_like(dsc_ref)

        u = u_ref[...]
        halo = jnp.where(i > 0, halo_ref[...], 0.0)
        ext = jnp.concatenate([halo, u], axis=0)
        dpm_t = dpm_ref[...].astype(F32)
        dnext = jnp.where(i < nsteps - 1, dnext_ref[...].astype(F32), 0.0)
        dext = jnp.concatenate([dpm_t, dnext], axis=0)
        sc = sc_ref[...]
        pos = i * tm + lax.broadcasted_iota(jnp.int32, (tm, 1), 0)
        pos_ext = i * tm + lax.broadcasted_iota(jnp.int32, (ext_rows, 1), 0)
        dus, dscs = [], []
        for g, w in enumerate(POOL_WINDOWS):
            sl = slice(g * GROUP, (g + 1) * GROUP)
            dc = _window_mean_minus_token(ext, u, g, w, pos).astype(CDT)
            y = _dot(dc, pw_ref[g])
            dscs.append(jnp.sum(dpm_t[:, sl] * y, axis=0, keepdims=True))
            dy_ext = (dext[:, sl] * sc[:, sl]).astype(CDT)
            dpw_ref[g] += _dot_tn(dc, dy_ext[:tm])
            dd = _dot_nt(dy_ext, pw_ref[g])
            r = dd / jnp.minimum(pos_ext + 1, w).astype(F32)
            span = 1
            while span < w:
                r = r + pltpu.roll(r, ext_rows - span, axis=0)
                span *= 2
            dus.append(r[:tm] - dd[:tm])
        du_ref[...] = jnp.concatenate(dus, axis=1).astype(du_ref.dtype)
        dsc_ref[...] += jnp.concatenate(dscs, axis=1)

    row = pl.BlockSpec((tm, POOL_DIM), lambda i: (i, 0))
    prev = pl.BlockSpec((POOL_WMAX, POOL_DIM), lambda i: (jnp.maximum(i * hb - 1, 0), 0))
    nxt = pl.BlockSpec((POOL_WMAX, POOL_DIM), lambda i: (jnp.minimum((i + 1) * hb, nsteps * hb - 1), 0))
    return pl.pallas_call(
        body, name=name, grid=(nsteps,),
        in_specs=[row, prev, row, nxt, pl.BlockSpec(pool_w.shape, lambda i: (0, 0, 0)),
                  pl.BlockSpec((1, POOL_DIM), lambda i: (0, 0))],
        out_specs=[row, pl.BlockSpec(pool_w.shape, lambda i: (0, 0, 0)), pl.BlockSpec((1, POOL_DIM), lambda i: (0, 0))],
        out_shape=[jax.ShapeDtypeStruct((T, POOL_DIM), CDT), jax.ShapeDtypeStruct(pool_w.shape, F32),
                   jax.ShapeDtypeStruct((1, POOL_DIM), F32)],
        compiler_params=_params("arbitrary"),
    )(zu, zu, dpm, dpm, pool_w, scale.reshape(1, POOL_DIM))


def _rope_tables(T):
    pos = jnp.arange(T, dtype=F32)
    inv_freq = ROPE_THETA ** (-jnp.arange(0, ROT_DIM, 2, dtype=F32) / ROT_DIM)
    ang = pos[:, None] * inv_freq[None, :]
    cos, sin = jnp.cos(ang), jnp.sin(ang)
    rest = HEAD_DIM - ROT_DIM
    cos_h = jnp.concatenate([cos, cos, jnp.ones((T, rest), F32)], axis=1)
    sin_h = jnp.concatenate([-sin, sin, jnp.zeros((T, rest), F32)], axis=1)
    return jnp.tile(cos_h, (1, 2)), jnp.tile(sin_h, (1, 2))


def _lane_masks():
    lane = lax.broadcasted_iota(jnp.int32, (1, LANES), 1)
    in_head = lane % HEAD_DIM
    return lane < HEAD_DIM, in_head < ROT_DIM // 2


def _rope_partner(v, low):
    lane = lax.broadcasted_iota(jnp.int32, (1, LANES), 1)
    swapped = jnp.where(low, pltpu.roll(v, LANES - ROT_DIM // 2, axis=1), pltpu.roll(v, ROT_DIM // 2, axis=1))
    return jnp.where(lane % HEAD_DIM < ROT_DIM, swapped, 0.0)


def _head_mean(v, first):
    lo = jnp.sum(jnp.where(first, v, 0.0), axis=-1, keepdims=True)
    hi = jnp.sum(jnp.where(first, 0.0, v), axis=-1, keepdims=True)
    return jnp.where(first, lo, hi) * (1.0 / HEAD_DIM)


def _qk_fwd(zqk, gqk, cos_t, sin_t, name):
    T = zqk.shape[0]
    tm = _tile(T, 512)

    def body(z_ref, g_ref, c_ref, s_ref, o_ref):
        first, low = _lane_masks()
        cosv, sinv = c_ref[...], s_ref[...]
        for c in range(QK_DIM // LANES):
            sl = slice(c * LANES, (c + 1) * LANES)
            xv = z_ref[:, sl]
            r = lax.rsqrt(_head_mean(xv * xv, first) + EPS)
            xn = xv * r * g_ref[:, sl]
            o_ref[:, sl] = (xn * cosv + _rope_partner(xn, low) * sinv).astype(o_ref.dtype)

    row = pl.BlockSpec((tm, QK_DIM), lambda i: (i, 0))
    tab = pl.BlockSpec((tm, LANES), lambda i: (i, 0))
    return pl.pallas_call(
        body, name=name, grid=(T // tm,),
        in_specs=[row, pl.BlockSpec((1, QK_DIM), lambda i: (0, 0)), tab, tab], out_specs=row,
        out_shape=jax.ShapeDtypeStruct((T, QK_DIM), CDT), compiler_params=_params("parallel"),
    )(zqk, gqk, cos_t, sin_t)


def _qk_bwd(dqk, zqk, gqk, cos_t, sin_t, name):
    T = zqk.shape[0]
    tm = _tile(T, 512)

    def body(d_ref, z_ref, g_ref, c_ref, s_ref, dz_ref, dg_ref):
        @pl.when(pl.program_id(0) == 0)
        def _():
            dg_ref[...] = jnp.zeros_like(dg_ref)

        first, low = _lane_masks()
        cosv, sinv = c_ref[...], s_ref[...]
        dgs = []
        for c in range(QK_DIM // LANES):
            sl = slice(c * LANES, (c + 1) * LANES)
            dout = d_ref[:, sl]
            dxn = dout * cosv + _rope_partner(dout * sinv, low)
            xv = z_ref[:, sl]
            r = lax.rsqrt(_head_mean(xv * xv, first) + EPS)
            xh = xv * r
            dgs.append(jnp.sum(dxn * xh, axis=0, keepdims=True))
            dxh = dxn * g_ref[:, sl]
            dz_ref[:, sl] = (r * (dxh - xh * _head_mean(dxh * xh, first))).astype(dz_ref.dtype)
        dg_ref[...] += jnp.concatenate(dgs, axis=1)

    row = pl.BlockSpec((tm, QK_DIM), lambda i: (i, 0))
    tab = pl.BlockSpec((tm, LANES), lambda i: (i, 0))
    vec = pl.BlockSpec((1, QK_DIM), lambda i: (0, 0))
    return pl.pallas_call(
        body, name=name, grid=(T // tm,),
        in_specs=[row, row, vec, tab, tab], out_specs=[row, vec],
        out_shape=[jax.ShapeDtypeStruct((T, QK_DIM), CDT), jax.ShapeDtypeStruct((1, QK_DIM), F32)],
        compiler_params=_params("arbitrary"),
    )(dqk, zqk, gqk, cos_t, sin_t)


def _dup_half(v, first, kv):
    swapped = pltpu.roll(v, HEAD_DIM, axis=1)
    return jnp.where(first, v, swapped) if kv == 0 else jnp.where(first, swapped, v)


HEADS_PER_KV = 4
HEAD_STACK = 1


def _attn_bias():
    qi = lax.broadcasted_iota(jnp.int32, (HEAD_STACK * BLOCK, 2 * BLOCK), 0) % BLOCK
    ki = lax.broadcasted_iota(jnp.int32, (HEAD_STACK * BLOCK, 2 * BLOCK), 1)
    diff = qi + BLOCK - ki
    band = (diff >= 0) & (diff < BLOCK)
    return jnp.stack([jnp.where(band, 0.0, -jnp.inf), jnp.where(band & (ki >= BLOCK), 0.0, -jnp.inf)]).astype(F32)


def _attn_blocks(T):
    return _tile(T // BLOCK, 4, 1)


def _stack_heads(ref, rows, kv, heads, first):
    parts = []
    for h in heads:
        c = 2 * kv + h // 2
        v = ref[rows, c * LANES:(c + 1) * LANES].astype(CDT)
        zero = jnp.zeros_like(v)
        parts.append(jnp.where(first, v, zero) if h % 2 == 0 else jnp.where(first, zero, v))
    return parts[0] if len(parts) == 1 else jnp.concatenate(parts, axis=0)


def _row_blocks(v, n):
    return [v[b * BLOCK:(b + 1) * BLOCK] for b in range(n)]


def _sink_column(sink_ref, kv, heads):
    cols = [jnp.full((BLOCK, 1), sink_ref[HEADS_PER_KV * kv + h], F32) for h in heads]
    return cols[0] if len(cols) == 1 else jnp.concatenate(cols, axis=0)


def _head_groups():
    return [tuple(range(g, g + HEAD_STACK)) for g in range(0, HEADS_PER_KV, HEAD_STACK)]


def _softmax_with_sink(qst, kdup, sinkcol, bias):
    s = _dot_nt(qst, kdup) * ATTN_SCALE + bias
    m = jnp.maximum(jnp.max(s, axis=-1, keepdims=True), sinkcol)
    pu = jnp.exp(s - m)
    es = jnp.exp(sinkcol - m)
    inv = 1.0 / (jnp.sum(pu, axis=-1, keepdims=True) + es)
    return pu * inv, es * inv


def _attn_fwd(qkn, zv, sinks, name):
    T = qkn.shape[0]
    R = _attn_blocks(T)
    tq = R * BLOCK

    def body(sink_ref, bias_ref, qk_ref, qkp_ref, v_ref, vp_ref, o_ref):
        i = pl.program_id(0)
        first, _ = _lane_masks()
        kall = jnp.concatenate([qkp_ref[:, ATTN_DIM:], qk_ref[:, ATTN_DIM:]], axis=0)
        vall = jnp.concatenate([vp_ref[...], v_ref[...]], axis=0).astype(CDT)
        for r in range(R):
            bias = bias_ref[jnp.where(i == 0, 1, 0)] if r == 0 else bias_ref[0]
            rows = slice(r * BLOCK, (r + 2) * BLOCK)
            qrows = slice(r * BLOCK, (r + 1) * BLOCK)
            for kv in range(2):
                kdup = _dup_half(kall[rows], first, kv)
                vdup = _dup_half(vall[rows], first, kv)
                res = []
                for heads in _head_groups():
                    p, _ = _softmax_with_sink(_stack_heads(qk_ref, qrows, kv, heads, first), kdup,
                                              _sink_column(sink_ref, kv, heads), bias)
                    res += _row_blocks(_dot(p.astype(CDT), vdup), len(heads))
                o_ref[qrows, 2 * kv * LANES:(2 * kv + 1) * LANES] = jnp.where(first, res[0], res[1]).astype(o_ref.dtype)
                o_ref[qrows, (2 * kv + 1) * LANES:(2 * kv + 2) * LANES] = jnp.where(first, res[2], res[3]).astype(o_ref.dtype)

    bias = _attn_bias()
    prev = lambda i: (jnp.maximum(i * R - 1, 0), 0)
    return pl.pallas_call(
        body, name=name, grid=(T // tq,),
        in_specs=[pl.BlockSpec(memory_space=pltpu.SMEM), pl.BlockSpec(bias.shape, lambda i: (0, 0, 0)),
                  pl.BlockSpec((tq, QK_DIM), lambda i: (i, 0)), pl.BlockSpec((BLOCK, QK_DIM), prev),
                  pl.BlockSpec((tq, KV_DIM), lambda i: (i, 0)), pl.BlockSpec((BLOCK, KV_DIM), prev)],
        out_specs=pl.BlockSpec((tq, ATTN_DIM), lambda i: (i, 0)),
        out_shape=jax.ShapeDtypeStruct((T, ATTN_DIM), CDT), compiler_params=_params("parallel"),
    )(sinks, bias, qkn, qkn, zv, zv)


def _attn_bwd(qkn, zv, sinks, do, name, side=None):
    T = qkn.shape[0]
    R = _attn_blocks(T)
    tq = R * BLOCK

    def body(sink_ref, bias_ref, qk_ref, qkp_ref, v_ref, vp_ref, do_ref, dq_ref, dkc_ref, dkp_ref, dvc_ref, dvp_ref, ds_ref):
        i = pl.program_id(0)

        @pl.when(i == 0)
        def _():
            ds_ref[...] = jnp.zeros_like(ds_ref)

        first, _ = _lane_masks()
        kall = jnp.concatenate([qkp_ref[:, ATTN_DIM:], qk_ref[:, ATTN_DIM:]], axis=0)
        vall = jnp.concatenate([vp_ref[...], v_ref[...]], axis=0).astype(CDT)
        for r in range(R):
            bias = bias_ref[jnp.where(i == 0, 1, 0)] if r == 0 else bias_ref[0]
            rows = slice(r * BLOCK, (r + 2) * BLOCK)
            qrows = slice(r * BLOCK, (r + 1) * BLOCK)
            dk_out, dv_out = [], []
            for kv in range(2):
                kdup = _dup_half(kall[rows], first, kv)
                vdup = _dup_half(vall[rows], first, kv)
                dq_h = []
                dk_acc = jnp.zeros((2 * BLOCK, LANES), F32)
                dv_acc = jnp.zeros((2 * BLOCK, LANES), F32)
                for heads in _head_groups():
                    qst = _stack_heads(qk_ref, qrows, kv, heads, first)
                    dost = _stack_heads(do_ref, qrows, kv, heads, first)
                    p, psink = _softmax_with_sink(qst, kdup, _sink_column(sink_ref, kv, heads), bias)
                    dp = _dot_nt(dost, vdup)
                    delta = jnp.sum(p * dp, axis=-1, keepdims=True)
                    dsc = (p * (dp - delta)).astype(CDT)
                    for b, term in enumerate(_row_blocks(psink * delta, len(heads))):
                        row = HEADS_PER_KV * kv + heads[b]
                        ds_ref[row:row + 1, :] += jnp.sum(term, axis=0, keepdims=True)
                    dq_h += _row_blocks(_dot(dsc, kdup) * ATTN_SCALE, len(heads))
                    dk_acc = dk_acc + _dot_tn(dsc, qst) * ATTN_SCALE
                    dv_acc = dv_acc + _dot_tn(p.astype(CDT), dost)
                dq_ref[qrows, 2 * kv * LANES:(2 * kv + 1) * LANES] = jnp.where(first, dq_h[0], dq_h[1])
                dq_ref[qrows, (2 * kv + 1) * LANES:(2 * kv + 2) * LANES] = jnp.where(first, dq_h[2], dq_h[3])
                dk_out.append(dk_acc + pltpu.roll(dk_acc, HEAD_DIM, axis=1))
                dv_out.append(dv_acc + pltpu.roll(dv_acc, HEAD_DIM, axis=1))
            dk = jnp.where(first, dk_out[0], dk_out[1])
            dv = jnp.where(first, dv_out[0], dv_out[1])
            dkp_ref[qrows, :] = dk[:BLOCK]
            dkc_ref[qrows, :] = dk[BLOCK:]
            dvp_ref[qrows, :] = dv[:BLOCK]
            dvc_ref[qrows, :] = dv[BLOCK:]

    bias = _attn_bias()
    prev = lambda i: (jnp.maximum(i * R - 1, 0), 0)
    kvrow = pl.BlockSpec((tq, KV_DIM), lambda i: (i, 0))
    qrow = pl.BlockSpec((tq, ATTN_DIM), lambda i: (i, 0))
    kv_shape = jax.ShapeDtypeStruct((T, KV_DIM), F32)
    return _pcall(
        body, name, (T // tq,),
        [pl.BlockSpec(memory_space=pltpu.SMEM), pl.BlockSpec(bias.shape, lambda i: (0, 0, 0)),
         pl.BlockSpec((tq, QK_DIM), lambda i: (i, 0)), pl.BlockSpec((BLOCK, QK_DIM), prev),
         kvrow, pl.BlockSpec((BLOCK, KV_DIM), prev), qrow],
        [qrow, kvrow, kvrow, kvrow, kvrow, pl.BlockSpec((N_Q_HEADS, LANES), lambda i: (0, 0))],
        [jax.ShapeDtypeStruct((T, ATTN_DIM), F32), kv_shape, kv_shape, kv_shape, kv_shape,
         jax.ShapeDtypeStruct((N_Q_HEADS, LANES), F32)],
        (sinks, bias, qkn, qkn, zv, zv, do), ("arbitrary",), side)


def _merge_fwd(pm, o, w_pb, w_ab, zg, name):
    T = pm.shape[0]
    tm = _tile(T, 512)

    def body(pm_ref, o_ref, wp_ref, wa_ref, zg_ref, a_ref, b_ref, m_ref):
        a = _dot(pm_ref[...], wp_ref[...])
        b = _dot(o_ref[...], wa_ref[...])
        gp = _sigmoid(zg_ref[:, :D_MODEL])
        ga = _sigmoid(zg_ref[:, D_MODEL:])
        a_ref[...] = a.astype(a_ref.dtype)
        b_ref[...] = b.astype(b_ref.dtype)
        m_ref[...] = (gp * a + ga * b).astype(m_ref.dtype)

    half = pl.BlockSpec((tm, POOL_DIM), lambda i: (i, 0))
    full = pl.BlockSpec((tm, D_MODEL), lambda i: (i, 0))
    wspec = pl.BlockSpec((POOL_DIM, D_MODEL), lambda i: (0, 0))
    out = jax.ShapeDtypeStruct((T, D_MODEL), CDT)
    return pl.pallas_call(
        body, name=name, grid=(T // tm,),
        in_specs=[half, half, wspec, wspec, pl.BlockSpec((tm, GATE_DIM), lambda i: (i, 0))],
        out_specs=[full, full, full], out_shape=[out, out, out], compiler_params=_params("parallel"),
    )(pm, o, w_pb, w_ab, zg)


def _merge_bwd(dxo, w_out, a, b, zg, name):
    T = dxo.shape[0]
    tm = _tile(T, 512)

    def body(dx_ref, w_ref, a_ref, b_ref, zg_ref, da_ref, db_ref, dg_ref):
        dm = _dot_nt(dx_ref[...].astype(CDT), w_ref[...])
        gp = _sigmoid(zg_ref[:, :D_MODEL])
        ga = _sigmoid(zg_ref[:, D_MODEL:])
        da_ref[...] = (dm * gp).astype(da_ref.dtype)
        db_ref[...] = (dm * ga).astype(db_ref.dtype)
        dg_ref[:, :D_MODEL] = (dm * a_ref[...].astype(F32) * (gp * (1.0 - gp))).astype(dg_ref.dtype)
        dg_ref[:, D_MODEL:] = (dm * b_ref[...].astype(F32) * (ga * (1.0 - ga))).astype(dg_ref.dtype)

    full = pl.BlockSpec((tm, D_MODEL), lambda i: (i, 0))
    gate = pl.BlockSpec((tm, GATE_DIM), lambda i: (i, 0))
    out = jax.ShapeDtypeStruct((T, D_MODEL), CDT)
    return pl.pallas_call(
        body, name=name, grid=(T // tm,),
        in_specs=[full, pl.BlockSpec((D_MODEL, D_MODEL), lambda i: (0, 0)), full, full, gate],
        out_specs=[full, full, gate], out_shape=[out, out, jax.ShapeDtypeStruct((T, GATE_DIM), CDT)],
        compiler_params=_params("parallel"),
    )(dxo, w_out, a, b, zg)


def _adamw(w, g, m, v, name):
    Rr, C = w.shape
    tr = _tile(Rr, max(8, (1 << 19) // C // 8 * 8))

    def body(w_ref, g_ref, m_ref, v_ref, d_ref, nm_ref, nv_ref):
        gv = g_ref[...]
        nm = ADAM_B1 * m_ref[...] + (1.0 - ADAM_B1) * gv
        nv = ADAM_B2 * v_ref[...] + (1.0 - ADAM_B2) * (gv * gv)
        m_hat = nm / (1.0 - ADAM_B1 ** ADAM_STEP)
        v_hat = nv / (1.0 - ADAM_B2 ** ADAM_STEP)
        d_ref[...] = -ADAM_LR * (m_hat / (jnp.sqrt(v_hat) + ADAM_EPS) + ADAM_WD * w_ref[...])
        nm_ref[...] = nm
        nv_ref[...] = nv

    blk = pl.BlockSpec((tr, C), lambda i: (i, 0))
    out = jax.ShapeDtypeStruct((Rr, C), F32)
    return pl.pallas_call(
        body, name=name, grid=(Rr // tr,), in_specs=[blk] * 4, out_specs=[blk] * 3, out_shape=[out] * 3,
        compiler_params=_params("parallel"),
    )(w, g, m, v)


def _place():
    return lax.axis_index("x"), lax.axis_index("y"), lax.axis_index("c")


def _other_chip(x, y, d):
    return (1 - x if d & 2 else x), (1 - y if d & 1 else y)


def _rcopy(src, dst, ssem, rsem, dev):
    return pltpu.make_async_remote_copy(src_ref=src, dst_ref=dst, send_sem=ssem, recv_sem=rsem, device_id=dev,
                                        device_id_type=MESH)


def _row_half(rows, c):
    return pl.ds(c * (rows // 2), rows // 2)


def _gather_ici_side(shards, l):
    n = len(shards)

    def issue(ins, outs, ssem, rsem):
        x, y, c = _place()
        cps = []
        for w in range(n):
            half = _row_half(shards[w].shape[1], c)
            for d in (1, 2, 3):
                px, py = _other_chip(x, y, d)
                k = 3 * w + d - 1
                cps.append(_rcopy(ins[w].at[l, half], outs[w].at[2 * x + y, half], ssem.at[k], rsem.at[k], (px, py, c)))
        return cps

    return _Side(shards, [jax.ShapeDtypeStruct((N_CHIPS,) + s.shape[1:], s.dtype) for s in shards], 3 * n, issue)


def _gather_d2d_side(shards, gathered, l):
    n = len(shards)

    def issue(ins, outs, ssem, rsem):
        x, y, c = _place()
        sibling = (x, y, 1 - c)
        cps = []
        for w in range(n):
            half = _row_half(shards[w].shape[1], c)
            for d in (1, 2, 3):
                px, py = _other_chip(x, y, d)
                k = 3 * w + d - 1
                got = outs[w].at[2 * px + py, half]
                cps.append(_rcopy(got, got, ssem.at[k], rsem.at[k], sibling))
            cps.append(_rcopy(ins[n + w].at[l], outs[w].at[2 * x + y], ssem.at[3 * n + w], rsem.at[3 * n + w], sibling))
        return cps

    return _Side(list(gathered) + list(shards), [jax.ShapeDtypeStruct(g.shape, g.dtype) for g in gathered], 4 * n, issue,
                 aliases={w: w for w in range(n)})


def _reduce_sibling_side(gms):
    n = len(gms)

    def issue(ins, outs, ssem, rsem):
        x, y, c = _place()
        return [_rcopy(ins[w].at[:, _row_half(gms[w].shape[1], 1 - c)], outs[w], ssem.at[w], rsem.at[w], (x, y, 1 - c))
                for w in range(n)]

    return _Side(gms, [jax.ShapeDtypeStruct((N_CHIPS, g.shape[1] // 2, g.shape[2]), g.dtype) for g in gms], n, issue)


def _reduce_chip_side(ps):
    n = len(ps)

    def issue(ins, outs, ssem, rsem):
        x, y, c = _place()
        cps = []
        for w in range(n):
            for d in (1, 2, 3):
                px, py = _other_chip(x, y, d)
                k = 3 * w + d - 1
                cps.append(_rcopy(ins[w].at[2 * px + py], outs[w].at[2 * x + y], ssem.at[k], rsem.at[k], (px, py, c)))
        return cps

    return _Side(ps, [jax.ShapeDtypeStruct(p.shape, p.dtype) for p in ps], 3 * n, issue)


def _share_side(accs):
    n = len(accs)

    def issue(ins, outs, ssem, rsem):
        x, y, c = _place()
        cps = []
        for w in range(n):
            mine = outs[w].at[:, _row_half(accs[w].shape[1], c)]
            cps.append(_rcopy(mine, mine, ssem.at[w], rsem.at[w], (x, y, 1 - c)))
        return cps

    return _Side(accs, [jax.ShapeDtypeStruct(a.shape, a.dtype) for a in accs], n, issue, aliases={w: w for w in range(n)})


def _sum_rows(rows, b):
    return _tile(rows, max(16, (1 << 19) // b // 16 * 16), 16)


def _pair_sum(g, recv, place, name):
    _, ah, b = recv.shape
    ta = _sum_rows(ah, b)
    nr = ah // ta

    def body(p_ref, g_ref, r_ref, o_ref):
        o_ref[...] = (g_ref[...].astype(F32) + r_ref[...].astype(F32)).astype(o_ref.dtype)

    blk = (None, ta, b)
    return pl.pallas_call(
        body, name=name,
        grid_spec=pltpu.PrefetchScalarGridSpec(
            num_scalar_prefetch=1, grid=(N_CHIPS, nr),
            in_specs=[pl.BlockSpec(blk, lambda j, r, p: (j, p[0] * nr + r, 0)),
                      pl.BlockSpec(blk, lambda j, r, p: (j, r, 0))],
            out_specs=pl.BlockSpec(blk, lambda j, r, p: (j, r, 0))),
        out_shape=jax.ShapeDtypeStruct(recv.shape, recv.dtype),
        compiler_params=_params("parallel", "parallel"),
    )(place, g, recv)


def _chip_sum(slots, part, place, acc, l, name):
    _, ah, b = slots.shape
    ta = _sum_rows(ah, b)
    nr = ah // ta

    def body(p_ref, s_ref, own_ref, acc_ref, o_ref):
        j = p_ref[1]
        own = own_ref[...].astype(F32)
        term = [jnp.where(j == s_, own, s_ref[s_].astype(F32)) for s_ in range(N_CHIPS)]
        o_ref[...] = ((term[0] + term[1]) + term[2]) + term[3]

    return pl.pallas_call(
        body, name=name,
        grid_spec=pltpu.PrefetchScalarGridSpec(
            num_scalar_prefetch=1, grid=(nr,),
            in_specs=[pl.BlockSpec((N_CHIPS, ta, b), lambda r, p: (0, r, 0)),
                      pl.BlockSpec((None, ta, b), lambda r, p: (p[1], r, 0)), ANY],
            out_specs=pl.BlockSpec((None, ta, b), lambda r, p: (l, p[0] * nr + r, 0))),
        out_shape=jax.ShapeDtypeStruct(acc.shape, F32), input_output_aliases={3: 0},
        compiler_params=_params("parallel"),
    )(place, slots, part, acc)


def _all_reduce_small(v):
    Rr = v.shape[0]

    def body(v_ref, slots_ref, out_ref, ssem, rsem):
        x, y, c = _place()
        me = 4 * x + 2 * y + c
        slots_ref[pl.ds(me, 1)] = v_ref[...][None]
        cps = []
        for d in range(1, N_DEV):
            px, py = _other_chip(x, y, d >> 1)
            pc = 1 - c if d & 1 else c
            cps.append(_rcopy(v_ref, slots_ref.at[me], ssem.at[d - 1], rsem.at[d - 1], (px, py, pc)))
            cps[-1].start()
        for cp in cps:
            cp.wait_recv()
        for cp in cps:
            cp.wait_send()
        acc = slots_ref[0]
        for s in range(1, N_DEV):
            acc = acc + slots_ref[s]
        out_ref[...] = acc

    vm = pl.BlockSpec(memory_space=pltpu.VMEM)
    return pl.pallas_call(
        body, name="all_reduce_small", in_specs=[vm], out_specs=[vm, vm],
        out_shape=[jax.ShapeDtypeStruct((N_DEV, Rr, LANES), F32), jax.ShapeDtypeStruct((Rr, LANES), F32)],
        scratch_shapes=[pltpu.SemaphoreType.DMA((N_DEV - 1,)), pltpu.SemaphoreType.DMA((N_DEV - 1,))],
        compiler_params=pltpu.CompilerParams(vmem_limit_bytes=VMEM_LIMIT_BYTES),
    )(v)[1]


def _ffn_forward(x, ln, wgu, wd, tag, side_of):
    h = _norm_fwd(x, ln, f"{tag}_norm")
    gu, act = _ffn_up(h, wgu, f"{tag}_up", side_of(f"{tag}_up"))
    x_out = _mm_nn(act, wd, f"{tag}_down", F32, res=x, scale=0.5, side=side_of(f"{tag}_down"))
    return x_out, (x, h, gu)


def _ffn_backward(dxo, saved, ln, wgu, wd, tag, side_of):
    x, h, gu = saved
    dgu, act = _ffn_down_bwd(dxo, wd, gu, f"{tag}_down_bwd", side_of(f"{tag}_down_bwd"))
    d_wd = _mm_tn(act, dxo, f"{tag}_dwd", scale=0.5)
    d_wgu = _mm_tn(h, dgu, f"{tag}_dwgu", tn_target=1408, tm_target=1024)
    dx, d_ln = _mm_nt_norm_bwd(dgu, wgu, x, ln, dxo, f"{tag}_dh_norm_bwd")
    return dx, d_ln, d_wgu, d_wd


def _mixer_forward(x, p, tabs):
    h = _norm_fwd(x, p["ln_mix"], "mix_norm")
    zu, zqk, zv, zg = _mm_in(h, p["w_in"], "mix_in")
    pm = _pool_fwd(zu, p["pool_w"], p["pool_scale"], "pool_fwd")
    qkn = _qk_fwd(zqk, p["gqk"], *tabs, "qk_fwd")
    o = _attn_fwd(qkn, zv, p["sinks"], "attn_fwd")
    a, b, m = _merge_fwd(pm, o, p["w_pool_branch"], p["w_attn_branch"], zg, "merge_fwd")
    x_out = _mm_nn(m, p["w_out"], "mix_out", F32, res=x, scale=1.0)
    return x_out, (x, h, zu, zqk, zv, zg, pm, qkn, o, a, b, m)


def _shift_up(v):
    return jnp.concatenate([v[BLOCK:], jnp.zeros((BLOCK, v.shape[1]), v.dtype)], axis=0)


def _mixer_backward(dxo, saved, p, tabs, side_of):
    x, h, zu, zqk, zv, zg, pm, qkn, o, a, b, m = saved
    g = {}
    d_a, d_b, dgl = _merge_bwd(dxo, p["w_out"], a, b, zg, "merge_bwd")
    g["w_out"] = _mm_tn(m, dxo, "mix_dwout")
    dpm = _mm_nt(d_a, p["w_pool_branch"], "pool_branch_dx", CDT)
    g["w_pool_branch"] = _mm_tn(pm, d_a, "pool_branch_dw")
    do = _mm_nt(d_b, p["w_attn_branch"], "attn_branch_dx", CDT)
    g["w_attn_branch"] = _mm_tn(o, d_b, "attn_branch_dw")
    du, g["pool_w"], g["pool_scale"] = _pool_bwd(zu, dpm, p["pool_w"], p["pool_scale"], "pool_bwd")
    dq, dkc, dkp, dvc, dvp, dsink = _attn_bwd(qkn, zv, p["sinks"], do, "attn_bwd", side_of("attn_bwd"))
    dqk = jnp.concatenate([dq, dkc + _shift_up(dkp)], axis=1)
    dv = dvc + _shift_up(dvp)
    dzqk, dgqk = _qk_bwd(dqk, zqk, p["gqk"], *tabs, "qk_bwd")
    g["q_norm"] = dgqk[0, :ATTN_DIM].reshape(N_Q_HEADS, HEAD_DIM).sum(axis=0)
    g["k_norm"] = dgqk[0, ATTN_DIM:].reshape(KV_DIM // HEAD_DIM, HEAD_DIM).sum(axis=0)
    g["sinks"] = -dsink[:, 0]
    dz = jnp.concatenate([du, dzqk, dv.astype(CDT), dgl], axis=1)
    g["w_in"] = _mm_tn(h, dz, "mix_dwin")
    dx, d_ln = _mm_nt_norm_bwd(dz[None], p["w_in"], x, p["ln_mix"], dxo, "mix_dh_norm_bwd")
    g["ln_mix"] = d_ln[0]
    return dx, g


class _NoComm:
    def __init__(self, layers):
        self.layers, self.grads = layers, [None] * len(layers)

    def layer(self, l):
        return self.layers[l]

    def side(self, phase, l, host):
        return None

    def layer_grads(self, l, g):
        self.grads[l] = g


def _local_step(x, tgt, n_layers, hooks):
    T = x.shape[0]
    tabs = _rope_tables(T)
    saved, params = [], []
    for l in range(n_layers):
        p = hooks.layer(l)
        side_of = functools.partial(hooks.side, "fwd", l)
        x, s1 = _ffn_forward(x, p["ln_ffn1"], p["w_ffn1_gu"], p["w_ffn1_down"], "ffn1", side_of)
        x, s2 = _mixer_forward(x, p, tabs)
        x, s3 = _ffn_forward(x, p["ln_ffn2"], p["w_ffn2_gu"], p["w_ffn2_down"], "ffn2", side_of)
        saved.append((s1, s2, s3))
        params.append(p)
    dx, loss = _loss_head(x, tgt, "loss_head")
    for l in reversed(range(n_layers)):
        p = params[l]
        s1, s2, s3 = saved[l]
        side_of = functools.partial(hooks.side, "bwd", l)
        none = lambda host: None
        dx, d_ln2, d_gu2, d_dn2 = _ffn_backward(dx, s3, p["ln_ffn2"], p["w_ffn2_gu"], p["w_ffn2_down"], "ffn2", side_of)
        dx, g = _mixer_backward(dx, s2, p, tabs, side_of)
        dx, d_ln1, d_gu1, d_dn1 = _ffn_backward(dx, s1, p["ln_ffn1"], p["w_ffn1_gu"], p["w_ffn1_down"], "ffn1", none)
        g.update(ln_ffn1=d_ln1[0], w_ffn1_gu=d_gu1, w_ffn1_down=d_dn1, ln_ffn2=d_ln2[0], w_ffn2_gu=d_gu2, w_ffn2_down=d_dn2)
        hooks.layer_grads(l, g)
    return loss, dx


def _full_from_blocks(name, blocks):
    if name in COL_SHARDED:
        return jnp.transpose(blocks, (1, 0, 2)).reshape(blocks.shape[1], N_CHIPS * blocks.shape[2])
    return blocks.reshape(N_CHIPS * blocks.shape[1], blocks.shape[2])


def _blocks_from_full(name, full):
    K, N = full.shape
    if name in COL_SHARDED:
        return jnp.transpose(full.reshape(K, N_CHIPS, N // N_CHIPS), (1, 0, 2))
    return full.reshape(N_CHIPS, K // N_CHIPS, N)


GATHER_GROUPS = {"ffn1": ("w_ffn1_gu", "w_ffn1_down", "w_in"),
                 "ffn2": ("w_pool_branch", "w_attn_branch", "w_out", "w_ffn2_gu", "w_ffn2_down")}


class _Exchange:
    def __init__(self, shards, small, place, n_layers):
        self.shards, self.small, self.place, self.n_layers = shards, small, place, n_layers
        self.blocks = {}
        for names in GATHER_GROUPS.values():
            got = _run_side(_gather_ici_side([shards[n] for n in names], 0), "gather_ici")
            got = _run_side(_gather_d2d_side([shards[n] for n in names], got, 0), "gather_d2d")
            self.blocks.update({(n, 0): g for n, g in zip(names, got)})
        self.pending = {}
        self.acc = {n: lax.empty(shards[n].shape, F32) for n in BIG}
        self.small_grads = [None] * n_layers
        self.waiting = None

    def layer(self, l):
        self._collect_gathered()
        p = {n: _full_from_blocks(n, self.blocks.pop((n, l))) for n in BIG}
        p.update(self.small(l))
        return p

    def side(self, phase, l, host):
        if phase == "fwd" and l + 1 < self.n_layers:
            group, step = host.split("_")
            sh = [self.shards[n] for n in GATHER_GROUPS[group]]
            if step == "up":
                self.pending[group] = _gather_ici_side(sh, l + 1)
                return self.pending[group]
            done = _gather_d2d_side(sh, self.pending.pop(group).outs, l + 1)
            self.pending[group + "_done"] = (done, l + 1)
            return done
        if phase == "bwd" and self.waiting is not None:
            if host == "ffn2_down_bwd":
                self.waiting["sib"] = _reduce_sibling_side(self.waiting["gm"])
                return self.waiting["sib"]
            if host == "attn_bwd":
                self._pair_sums()
                self.waiting["chip"] = _reduce_chip_side(self.waiting["part"])
                return self.waiting["chip"]
        return None

    def _collect_gathered(self):
        for key in [k for k in self.pending if k.endswith("_done")]:
            done, l = self.pending.pop(key)
            self.blocks.update({(n, l): g for n, g in zip(GATHER_GROUPS[key[:-5]], done.outs)})

    def _pair_sums(self):
        w = self.waiting
        w["part"] = [_pair_sum(g, r, self.place, "grad_pair_sum") for g, r in zip(w["gm"], w["sib"].outs)]

    def _finish(self):
        w, self.waiting = self.waiting, None
        for n, slots, part in zip(BIG, w["chip"].outs, w["part"]):
            self.acc[n] = _chip_sum(slots, part, self.place, self.acc[n], w["l"], "grad_chip_sum")

    def layer_grads(self, l, g):
        if self.waiting is not None:
            self._finish()
        self.small_grads[l] = {n: g[n] for n in SMALL}
        self.waiting = dict(l=l, gm=[_blocks_from_full(n, g[n]).astype(WIRE_DT) for n in BIG])

    def reduced(self):
        w = self.waiting
        w["sib"] = _reduce_sibling_side(w["gm"])
        _run_side(w["sib"], "grad_sibling_exchange")
        self._pair_sums()
        w["chip"] = _reduce_chip_side(w["part"])
        _run_side(w["chip"], "grad_chip_exchange")
        self._finish()
        return dict(zip(BIG, _run_side(_share_side([self.acc[n] for n in BIG]), "grad_sibling_share")))


def _pack_small(parts):
    rows, spans, lo = [], [], 0
    for v in parts:
        flat = v.reshape(-1)
        nrow = -(-flat.shape[0] // LANES)
        flat = jnp.pad(flat, (0, nrow * LANES - flat.shape[0]))
        rows.append(flat.reshape(nrow, LANES))
        spans.append((lo, nrow))
        lo += nrow
    pad = -lo % 8
    if pad:
        rows.append(jnp.zeros((pad, LANES), F32))
    return jnp.concatenate(rows, axis=0), spans


def _unpack_small(packed, spans, shapes):
    out = []
    for (lo, nrow), shape in zip(spans, shapes):
        size = 1
        for s in shape:
            size *= s
        out.append(packed[lo:lo + nrow].reshape(-1)[:size].reshape(shape))
    return out


def kernel(x, ln_ffn1, w_ffn1_gu, w_ffn1_down, ln_mix, w_in, pool_w, pool_scale, w_pool_branch, q_norm, k_norm, sinks, w_attn_branch, w_out, ln_ffn2, w_ffn2_gu, w_ffn2_down, loss_target, m_ln_ffn1, m_w_ffn1_gu, m_w_ffn1_down, m_ln_mix, m_w_in, m_pool_w, m_pool_scale, m_w_pool_branch, m_q_norm, m_k_norm, m_sinks, m_w_attn_branch, m_w_out, m_ln_ffn2, m_w_ffn2_gu, m_w_ffn2_down, v_ln_ffn1, v_w_ffn1_gu, v_w_ffn1_down, v_ln_mix, v_w_in, v_pool_w, v_pool_scale, v_w_pool_branch, v_q_norm, v_k_norm, v_sinks, v_w_attn_branch, v_w_out, v_ln_ffn2, v_w_ffn2_gu, v_w_ffn2_down):
    w = dict(ln_ffn1=ln_ffn1, w_ffn1_gu=w_ffn1_gu, w_ffn1_down=w_ffn1_down, ln_mix=ln_mix, w_in=w_in, pool_w=pool_w,
             pool_scale=pool_scale, w_pool_branch=w_pool_branch, q_norm=q_norm, k_norm=k_norm, sinks=sinks,
             w_attn_branch=w_attn_branch, w_out=w_out, ln_ffn2=ln_ffn2, w_ffn2_gu=w_ffn2_gu, w_ffn2_down=w_ffn2_down)
    mom = dict(ln_ffn1=m_ln_ffn1, w_ffn1_gu=m_w_ffn1_gu, w_ffn1_down=m_w_ffn1_down, ln_mix=m_ln_mix, w_in=m_w_in,
               pool_w=m_pool_w, pool_scale=m_pool_scale, w_pool_branch=m_w_pool_branch, q_norm=m_q_norm, k_norm=m_k_norm,
               sinks=m_sinks, w_attn_branch=m_w_attn_branch, w_out=m_w_out, ln_ffn2=m_ln_ffn2, w_ffn2_gu=m_w_ffn2_gu,
               w_ffn2_down=m_w_ffn2_down)
    var = dict(ln_ffn1=v_ln_ffn1, w_ffn1_gu=v_w_ffn1_gu, w_ffn1_down=v_w_ffn1_down, ln_mix=v_ln_mix, w_in=v_w_in,
               pool_w=v_pool_w, pool_scale=v_pool_scale, w_pool_branch=v_w_pool_branch, q_norm=v_q_norm, k_norm=v_k_norm,
               sinks=v_sinks, w_attn_branch=v_w_attn_branch, w_out=v_w_out, ln_ffn2=v_ln_ffn2, w_ffn2_gu=v_w_ffn2_gu,
               w_ffn2_down=v_w_ffn2_down)
    L = ln_ffn1.shape[0]

    def small(l):
        return dict(ln_ffn1=ln_ffn1[l], ln_mix=ln_mix[l], ln_ffn2=ln_ffn2[l], pool_w=pool_w[l].astype(CDT),
                    pool_scale=pool_scale[l], sinks=sinks[l],
                    gqk=jnp.concatenate([jnp.tile(q_norm[l], N_Q_HEADS), jnp.tile(k_norm[l], KV_DIM // HEAD_DIM)]).reshape(1, QK_DIM))

    place = jnp.stack([lax.axis_index("c"), 2 * lax.axis_index("x") + lax.axis_index("y")]).astype(jnp.int32)
    hooks = _Exchange({n: w[n].astype(CDT) for n in BIG}, small, place, L)
    loss_part, grad_x = _local_step(x[0], loss_target[0], L, hooks)
    g_big = hooks.reduced()
    grads = hooks.small_grads

    small_parts = [jnp.stack([g[n] for g in grads]) for n in SMALL] + [loss_part]
    packed, spans = _pack_small(small_parts)
    summed = _all_reduce_small(packed)
    *g_small_list, loss_sum = _unpack_small(summed, spans, [w[n].shape for n in SMALL] + [(1, 1)])
    g_small = dict(zip(SMALL, g_small_list))
    loss = loss_sum[0, 0]

    grad_out, delta, new_m, new_v = {}, {}, {}, {}
    for n in BIG:
        shape = w[n].shape
        flat = (shape[0] * shape[1], shape[2])
        grad_out[n] = g_big[n]
        d, nm, nv = _adamw(w[n].reshape(flat), g_big[n].reshape(flat), mom[n].reshape(flat), var[n].reshape(flat), "adamw")
        delta[n], new_m[n], new_v[n] = d.reshape(shape), nm.reshape(shape), nv.reshape(shape)
    pw, _ = _pack_small([w[n] for n in SMALL])
    pg, sp = _pack_small([g_small[n] for n in SMALL])
    pm_, _ = _pack_small([mom[n] for n in SMALL])
    pv, _ = _pack_small([var[n] for n in SMALL])
    d, nm, nv = _adamw(pw, pg, pm_, pv, "adamw_small")
    shapes = [w[n].shape for n in SMALL]
    for n, dv, mv, vv in zip(SMALL, _unpack_small(d, sp, shapes), _unpack_small(nm, sp, shapes), _unpack_small(nv, sp, shapes)):
        grad_out[n], delta[n], new_m[n], new_v[n] = g_small[n], dv, mv, vv

    return (loss, grad_x[None], *[grad_out[n] for n in WEIGHTS], *[delta[n] for n in WEIGHTS],
            *[new_m[n] for n in WEIGHTS], *[new_v[n] for n in WEIGHTS])
```

```python
import functools
import math

import jax
import jax.numpy as jnp
from jax import lax
from jax.experimental import pallas as pl
from jax.experimental.pallas import tpu as pltpu

F32 = jnp.float32
CDT = jnp.bfloat16
WIRE_DT = jnp.bfloat16

D_MODEL = 1024
POOL_WINDOWS = (2, 4, 8, 16)
POOL_WMAX = 16
GROUP = 128
POOL_DIM = 512
HEAD_DIM = 64
N_Q_HEADS = 8
ATTN_DIM = 512
KV_DIM = 128
QK_DIM = ATTN_DIM + KV_DIM
GATE_DIM = 2 * D_MODEL
BLOCK = 128
ROPE_THETA = 500000.0
ROT_DIM = 16
EPS = 1e-6
ATTN_SCALE = HEAD_DIM ** -0.5

ADAM_LR = 0.001
ADAM_B1 = 0.9
ADAM_B2 = 0.999
ADAM_EPS = 1e-08
ADAM_WD = 0.01
ADAM_STEP = 10

N_CHIPS = 4
N_DEV = 8
LANES = 128
VMEM_LIMIT_BYTES = 48 * 1024 * 1024

MESH = pl.DeviceIdType.MESH
ANY = pl.BlockSpec(memory_space=pl.ANY)

BIG = ("w_ffn1_gu", "w_ffn1_down", "w_in", "w_pool_branch", "w_attn_branch", "w_out", "w_ffn2_gu", "w_ffn2_down")
COL_SHARDED = ("w_ffn1_gu", "w_in", "w_pool_branch", "w_attn_branch", "w_ffn2_gu")
USED_AS_BLOCKS = ("w_ffn1_gu", "w_pool_branch", "w_attn_branch", "w_ffn2_gu")
SMALL = ("ln_ffn1", "ln_mix", "pool_w", "pool_scale", "q_norm", "k_norm", "sinks", "ln_ffn2")
WEIGHTS = ("ln_ffn1", "w_ffn1_gu", "w_ffn1_down", "ln_mix", "w_in", "pool_w", "pool_scale", "w_pool_branch",
           "q_norm", "k_norm", "sinks", "w_attn_branch", "w_out", "ln_ffn2", "w_ffn2_gu", "w_ffn2_down")


def _tile(n, target, mult=8):
    if n <= target:
        return n
    for t in range(target - target % mult, 0, -mult):
        if n % t == 0:
            return t
    raise ValueError((n, target, mult))


def _params(*sem):
    return pltpu.CompilerParams(dimension_semantics=sem, vmem_limit_bytes=VMEM_LIMIT_BYTES)


def _sigmoid(v):
    return 0.5 * jnp.tanh(0.5 * v) + 0.5


def _dot(a, b):
    return jnp.dot(a, b, preferred_element_type=F32)


def _dot_nt(a, b):
    return lax.dot_general(a, b, (((1,), (1,)), ((), ())), preferred_element_type=F32)


def _dot_tn(a, b):
    return lax.dot_general(a, b, (((0,), (0,)), ((), ())), preferred_element_type=F32)


class _Side:
    def __init__(self, ins, out_shapes, n_sems, issue, aliases=None):
        self.ins, self.out_shapes, self.n_sems, self.issue = list(ins), list(out_shapes), n_sems, issue
        self.aliases = dict(aliases or {})
        self.outs = None


def _pcall(body, name, grid, in_specs, out_specs, out_shape, args, dims, side=None):
    if side is None:
        return pl.pallas_call(body, name=name, grid=grid, in_specs=in_specs, out_specs=out_specs, out_shape=out_shape,
                              compiler_params=_params(*dims))(*args)
    n_in, n_out, s_in, s_out = len(in_specs), len(out_specs), len(side.ins), len(side.out_shapes)

    def wrapped(*refs):
        main_in, side_in = refs[:n_in], refs[n_in:n_in + s_in]
        main_out = refs[n_in + s_in:n_in + s_in + n_out]
        side_out = refs[n_in + s_in + n_out:n_in + s_in + n_out + s_out]
        ssem, rsem = refs[n_in + s_in + n_out + s_out:]
        ids = [pl.program_id(ax) for ax in range(len(grid))]
        first = functools.reduce(jnp.logical_and, [i == 0 for i in ids])
        last = functools.reduce(jnp.logical_and, [i == g - 1 for i, g in zip(ids, grid)])

        @pl.when(first)
        def _():
            for cp in side.issue(side_in, side_out, ssem, rsem):
                cp.start()

        body(*main_in, *main_out)

        @pl.when(last)
        def _():
            cps = side.issue(side_in, side_out, ssem, rsem)
            for cp in cps:
                cp.wait_recv()
            for cp in cps:
                cp.wait_send()

    outs = pl.pallas_call(
        wrapped, name=name, grid=grid, in_specs=list(in_specs) + [ANY] * s_in, out_specs=list(out_specs) + [ANY] * s_out,
        out_shape=list(out_shape) + side.out_shapes,
        input_output_aliases={n_in + i: n_out + o for i, o in side.aliases.items()},
        scratch_shapes=[pltpu.SemaphoreType.DMA((side.n_sems,))] * 2,
        compiler_params=_params(*["arbitrary"] * len(grid)),
    )(*args, *side.ins)
    side.outs = list(outs[n_out:])
    return list(outs[:n_out])


def _run_side(side, name):
    s_in = len(side.ins)

    def body(*refs):
        ssem, rsem = refs[s_in + len(side.out_shapes):]
        cps = side.issue(refs[:s_in], refs[s_in:s_in + len(side.out_shapes)], ssem, rsem)
        for cp in cps:
            cp.start()
        for cp in cps:
            cp.wait_recv()
        for cp in cps:
            cp.wait_send()

    side.outs = list(pl.pallas_call(
        body, name=name, in_specs=[ANY] * s_in, out_specs=[ANY] * len(side.out_shapes), out_shape=side.out_shapes,
        input_output_aliases=side.aliases, scratch_shapes=[pltpu.SemaphoreType.DMA((side.n_sems,))] * 2,
    )(*side.ins))
    return side.outs


def _norm_fwd(x, g, name):
    T, Dm = x.shape
    tm = _tile(T, 512)

    def body(x_ref, g_ref, h_ref):
        xv = x_ref[...]
        r = lax.rsqrt(jnp.mean(xv * xv, axis=-1, keepdims=True) + EPS)
        h_ref[...] = (xv * r * g_ref[...]).astype(h_ref.dtype)

    row = pl.BlockSpec((tm, Dm), lambda i: (i, 0))
    return pl.pallas_call(
        body, name=name, grid=(T // tm,),
        in_specs=[row, pl.BlockSpec((1, Dm), lambda i: (0, 0))], out_specs=row,
        out_shape=jax.ShapeDtypeStruct((T, Dm), CDT), compiler_params=_params("parallel"),
    )(x, g.reshape(1, Dm))


def _loss_head(y, tgt, name):
    T, Dm = y.shape
    tm = _tile(T, 512)

    def body(y_ref, t_ref, dy_ref, loss_ref):
        @pl.when(pl.program_id(0) == 0)
        def _():
            loss_ref[...] = jnp.zeros_like(loss_ref)

        diff = y_ref[...] - t_ref[...]
        dy_ref[...] = diff * (1.0 / Dm)
        part = jnp.sum(jnp.mean(diff * diff, axis=-1, keepdims=True), axis=0, keepdims=True)
        loss_ref[...] += 0.5 * part

    row = pl.BlockSpec((tm, Dm), lambda i: (i, 0))
    one = pl.BlockSpec((1, 1), lambda i: (0, 0))
    return pl.pallas_call(
        body, name=name, grid=(T // tm,),
        in_specs=[row, row], out_specs=[row, one],
        out_shape=[jax.ShapeDtypeStruct((T, Dm), F32), jax.ShapeDtypeStruct((1, 1), F32)],
        compiler_params=_params("arbitrary"),
    )(y, tgt)


def _mm_nn(a, b, name, out_dtype, res=None, scale=1.0, tm_target=512, side=None):
    M, K = a.shape
    N = b.shape[1]
    tm = _tile(M, tm_target)

    def body(a_ref, b_ref, *rest):
        acc = _dot(a_ref[...].astype(CDT), b_ref[...])
        if res is None:
            (o_ref,) = rest
        else:
            r_ref, o_ref = rest
            acc = r_ref[...] + scale * acc
        o_ref[...] = acc.astype(o_ref.dtype)

    in_specs = [pl.BlockSpec((tm, K), lambda i: (i, 0)), pl.BlockSpec((K, N), lambda i: (0, 0))]
    args = [a, b]
    if res is not None:
        in_specs.append(pl.BlockSpec((tm, N), lambda i: (i, 0)))
        args.append(res)
    return _pcall(body, name, (M // tm,), in_specs, [pl.BlockSpec((tm, N), lambda i: (i, 0))],
                  [jax.ShapeDtypeStruct((M, N), out_dtype)], args, ("parallel",), side)[0]


def _mm_nt_blocks(a, b4, name, out_dtype, tm_target=512):
    M, K = a.shape
    nb, N, Kb = b4.shape
    tm = _tile(M, tm_target)

    def body(a_ref, b_ref, o_ref):
        acc = _dot_nt(a_ref[:, :Kb].astype(CDT), b_ref[0])
        for j in range(1, nb):
            acc = acc + _dot_nt(a_ref[:, j * Kb:(j + 1) * Kb].astype(CDT), b_ref[j])
        o_ref[...] = acc.astype(o_ref.dtype)

    return pl.pallas_call(
        body, name=name, grid=(M // tm,),
        in_specs=[pl.BlockSpec((tm, K), lambda i: (i, 0)), pl.BlockSpec(b4.shape, lambda i: (0, 0, 0))],
        out_specs=pl.BlockSpec((tm, N), lambda i: (i, 0)),
        out_shape=jax.ShapeDtypeStruct((M, N), out_dtype), compiler_params=_params("parallel"),
    )(a, b4)


def _mm_tn(x, dy, name, scale=1.0, col_blocks=1, tn_target=1664, tm_target=1408, tk_target=1024):
    T, M = x.shape
    split = dy.ndim == 3
    Nh = dy.shape[-1]
    N = 2 * Nh if split else Nh
    nb = N // col_blocks
    tm = _tile(M, tm_target, LANES)
    tn = _tile(math.gcd(Nh, nb), tn_target, LANES)
    tk = _tile(T, tk_target)
    nk = T // tk
    njh, njb = Nh // tn, nb // tn

    def body(x_ref, dy_ref, o_ref, acc_ref):
        k = pl.program_id(2)

        @pl.when(k == 0)
        def _():
            acc_ref[...] = jnp.zeros_like(acc_ref)

        acc_ref[...] += _dot_tn(x_ref[...].astype(CDT), dy_ref[...].astype(CDT))

        @pl.when(k == nk - 1)
        def _():
            o_ref[...] = (acc_ref[...] if scale == 1.0 else scale * acc_ref[...]).astype(o_ref.dtype)

    if split:
        dy_spec = pl.BlockSpec((None, tk, tn), lambda i, j, k: (j // njh, k, j % njh))
    else:
        dy_spec = pl.BlockSpec((tk, tn), lambda i, j, k: (k, j))
    if col_blocks == 1:
        out_spec, out_dims = pl.BlockSpec((tm, tn), lambda i, j, k: (i, j)), (M, N)
    else:
        out_spec, out_dims = pl.BlockSpec((None, tm, tn), lambda i, j, k: (j // njb, i, j % njb)), (col_blocks, M, nb)
    return pl.pallas_call(
        body, name=name, grid=(M // tm, N // tn, nk),
        in_specs=[pl.BlockSpec((tk, tm), lambda i, j, k: (k, i)), dy_spec], out_specs=out_spec,
        out_shape=jax.ShapeDtypeStruct(out_dims, WIRE_DT), scratch_shapes=[pltpu.VMEM((tm, tn), F32)],
        compiler_params=_params("parallel", "parallel", "arbitrary"),
    )(x, dy)


def _mm_tn_parts(x, parts, name):
    T, M = x.shape
    widths = [p.shape[1] for p in parts]
    N = sum(widths)
    tk = _tile(T, 512)

    def body(x_ref, *refs):
        o_ref = refs[-1]

        @pl.when(pl.program_id(0) == 0)
        def _():
            o_ref[...] = jnp.zeros_like(o_ref)

        xv = x_ref[...].astype(CDT)
        lo = 0
        for p_ref, wd in zip(refs[:-1], widths):
            o_ref[:, lo:lo + wd] += _dot_tn(xv, p_ref[...].astype(CDT))
            lo += wd

    return pl.pallas_call(
        body, name=name, grid=(T // tk,),
        in_specs=[pl.BlockSpec((tk, M), lambda k: (k, 0))] + [pl.BlockSpec((tk, wd), lambda k: (k, 0)) for wd in widths],
        out_specs=pl.BlockSpec((M, N), lambda k: (0, 0)),
        out_shape=jax.ShapeDtypeStruct((M, N), F32), compiler_params=_params("arbitrary"),
    )(x, *parts)


def _ffn_up(h, wgu4, name, side=None):
    T, Dm = h.shape
    tn = wgu4.shape[2]
    nj = 2
    Fd = nj * tn
    tm = _tile(T, 512)

    def body(h_ref, wg_ref, wu_ref, gu_ref, a_ref):
        hv = h_ref[...]
        g = _dot(hv, wg_ref[...])
        u = _dot(hv, wu_ref[...])
        gu_ref[0] = g.astype(gu_ref.dtype)
        gu_ref[1] = u.astype(gu_ref.dtype)
        a_ref[...] = (g * _sigmoid(g) * u).astype(a_ref.dtype)

    return _pcall(
        body, name, (nj, T // tm),
        [pl.BlockSpec((tm, Dm), lambda j, i: (i, 0)), pl.BlockSpec((None, Dm, tn), lambda j, i: (j, 0, 0)),
         pl.BlockSpec((None, Dm, tn), lambda j, i: (j + nj, 0, 0))],
        [pl.BlockSpec((2, tm, tn), lambda j, i: (0, i, j)), pl.BlockSpec((tm, tn), lambda j, i: (i, j))],
        [jax.ShapeDtypeStruct((2, T, Fd), CDT), jax.ShapeDtypeStruct((T, Fd), CDT)],
        (h, wgu4, wgu4), ("parallel", "parallel"), side)


def _ffn_down_bwd(dxo, wd, gu, name, side=None):
    T, Dm = dxo.shape
    Fd = wd.shape[0]
    tm = _tile(T, 512)
    tn = _tile(Fd, 1408, LANES)

    def body(dx_ref, wd_ref, gu_ref, dgu_ref):
        da = 0.5 * _dot_nt(dx_ref[...].astype(CDT), wd_ref[...])
        g = gu_ref[0].astype(F32)
        u = gu_ref[1].astype(F32)
        sg = _sigmoid(g)
        dgu_ref[0] = (da * u * (sg * (1.0 + g * (1.0 - sg)))).astype(dgu_ref.dtype)
        dgu_ref[1] = (da * (g * sg)).astype(dgu_ref.dtype)

    gu_spec = pl.BlockSpec((2, tm, tn), lambda j, i: (0, i, j))
    return _pcall(
        body, name, (Fd // tn, T // tm),
        [pl.BlockSpec((tm, Dm), lambda j, i: (i, 0)), pl.BlockSpec((tn, Dm), lambda j, i: (j, 0)), gu_spec],
        [gu_spec], [jax.ShapeDtypeStruct((2, T, Fd), CDT)],
        (dxo, wd, gu), ("parallel", "parallel"), side)[0]


def _mm_nt_norm_bwd(a_parts, b, x, g, dres, name):
    T, Dm = x.shape
    tm = _tile(T, 256)

    def b_cols(b_ref, lo, wd):
        if b.ndim == 2:
            return [(0, wd, b_ref[:, lo:lo + wd])]
        kb = b.shape[2]
        return [(j * kb - lo, kb, b_ref[j]) for j in range(lo // kb, (lo + wd) // kb)]

    def body(*refs):
        a_refs, (b_ref, x_ref, g_ref, dres_ref, dx_ref, dg_ref) = refs[:len(a_parts)], refs[len(a_parts):]

        @pl.when(pl.program_id(0) == 0)
        def _():
            dg_ref[...] = jnp.zeros_like(dg_ref)

        dh, lo = None, 0
        for a_ref, part in zip(a_refs, a_parts):
            slabs = [a_ref] if part.ndim == 2 else [a_ref.at[s_] for s_ in range(part.shape[0])]
            for slab in slabs:
                for off, wd, bv in b_cols(b_ref, lo, part.shape[-1]):
                    term = _dot_nt(slab[:, off:off + wd].astype(CDT), bv)
                    dh = term if dh is None else dh + term
                lo += part.shape[-1]
        xv = x_ref[...]
        r = lax.rsqrt(jnp.mean(xv * xv, axis=-1, keepdims=True) + EPS)
        xh = xv * r
        dg_ref[...] += jnp.sum(dh * xh, axis=0, keepdims=True)
        dxh = dh * g_ref[...]
        dx_ref[...] = dres_ref[...] + r * (dxh - xh * jnp.mean(dxh * xh, axis=-1, keepdims=True))

    row = pl.BlockSpec((tm, Dm), lambda i: (i, 0))
    vec = pl.BlockSpec((1, Dm), lambda i: (0, 0))
    a_specs = [pl.BlockSpec((tm, p.shape[1]), lambda i: (i, 0)) if p.ndim == 2 else
               pl.BlockSpec((p.shape[0], tm, p.shape[2]), lambda i: (0, i, 0)) for p in a_parts]
    b_spec = pl.BlockSpec(b.shape, lambda i: (0,) * b.ndim, pipeline_mode=pl.Buffered(1))
    return pl.pallas_call(
        body, name=name, grid=(T // tm,),
        in_specs=a_specs + [b_spec, row, vec, row], out_specs=[row, vec],
        out_shape=[jax.ShapeDtypeStruct((T, Dm), F32), jax.ShapeDtypeStruct((1, Dm), F32)],
        compiler_params=_params("arbitrary"),
    )(*a_parts, b, x, g.reshape(1, Dm), dres)


def _mm_in(h, w_in, name):
    T, Dm = h.shape
    tm = _tile(T, 256)
    widths = (POOL_DIM, QK_DIM, KV_DIM, GATE_DIM)

    def body(h_ref, w_ref, *outs):
        z = _dot(h_ref[...], w_ref[...])
        lo = 0
        for o_ref, wd in zip(outs, widths):
            o_ref[...] = z[:, lo:lo + wd]
            lo += wd

    return pl.pallas_call(
        body, name=name, grid=(T // tm,),
        in_specs=[pl.BlockSpec((tm, Dm), lambda i: (i, 0)), pl.BlockSpec(w_in.shape, lambda i: (0, 0))],
        out_specs=[pl.BlockSpec((tm, wd), lambda i: (i, 0)) for wd in widths],
        out_shape=[jax.ShapeDtypeStruct((T, wd), F32) for wd in widths],
        compiler_params=_params("parallel"),
    )(h, w_in)


def _window_mean_minus_token(ext, u, g, w, pos):
    sl = slice(g * GROUP, (g + 1) * GROUP)
    s = ext[:, sl]
    span = 1
    while span < w:
        s = s + pltpu.roll(s, span, axis=0)
        span *= 2
    cnt = jnp.minimum(pos + 1, w).astype(F32)
    return s[POOL_WMAX:, :] / cnt - u[:, sl]


def _pool_fwd(zu, pool_w, scale, name):
    T = zu.shape[0]
    tm = _tile(T, 512, POOL_WMAX)
    hb = tm // POOL_WMAX

    def body(u_ref, halo_ref, pw_ref, sc_ref, pm_ref):
        i = pl.program_id(0)
        u = u_ref[...]
        halo = jnp.where(i > 0, halo_ref[...], 0.0)
        ext = jnp.concatenate([halo, u], axis=0)
        pos = i * tm + lax.broadcasted_iota(jnp.int32, (tm, 1), 0)
        ys = []
        for g, w in enumerate(POOL_WINDOWS):
            d = _window_mean_minus_token(ext, u, g, w, pos)
            ys.append(_dot(d.astype(CDT), pw_ref[g]))
        pm_ref[...] = (jnp.concatenate(ys, axis=1) * sc_ref[...]).astype(pm_ref.dtype)

    row = pl.BlockSpec((tm, POOL_DIM), lambda i: (i, 0))
    return pl.pallas_call(
        body, name=name, grid=(T // tm,),
        in_specs=[row, pl.BlockSpec((POOL_WMAX, POOL_DIM), lambda i: (jnp.maximum(i * hb - 1, 0), 0)),
                  pl.BlockSpec(pool_w.shape, lambda i: (0, 0, 0)), pl.BlockSpec((1, POOL_DIM), lambda i: (0, 0))],
        out_specs=row, out_shape=jax.ShapeDtypeStruct((T, POOL_DIM), CDT),
        compiler_params=_params("parallel"),
    )(zu, zu, pool_w, scale.reshape(1, POOL_DIM))


def _pool_bwd(zu, dpm, pool_w, scale, name):
    T = zu.shape[0]
    tm = _tile(T, 512, POOL_WMAX)
    hb = tm // POOL_WMAX
    nsteps = T // tm
    ext_rows = tm + POOL_WMAX

    def body(u_ref, halo_ref, dpm_ref, dnext_ref, pw_ref, sc_ref, du_ref, dpw_ref, dsc_ref):
        i = pl.program_id(0)

        @pl.when(i == 0)
        def _():
            dpw_ref[...] = jnp.zeros_like(dpw_ref)
            dsc_ref[...] = jnp.zeros_like(dsc_ref)

        u = u_ref[...]
        halo = jnp.where(i > 0, halo_ref[...], 0.0)
        ext = jnp.concatenate([halo, u], axis=0)
        dpm_t = dpm_ref[...].astype(F32)
        dnext = jnp.where(i < nsteps - 1, dnext_ref[...].astype(F32), 0.0)
        dext = jnp.concatenate([dpm_t, dnext], axis=0)
        sc = sc_ref[...]
        pos = i * tm + lax.broadcasted_iota(jnp.int32, (tm, 1), 0)
        pos_ext = i * tm + lax.broadcasted_iota(jnp.int32, (ext_rows, 1), 0)
        dus, dscs = [], []
        for g, w in enumerate(POOL_WINDOWS):
            sl = slice(g * GROUP, (g + 1) * GROUP)
            dc = _window_mean_minus_token(ext, u, g, w, pos).astype(CDT)
            y = _dot(dc, pw_ref[g])
            dscs.append(jnp.sum(dpm_t[:, sl] * y, axis=0, keepdims=True))
            dy_ext = (dext[:, sl] * sc[:, sl]).astype(CDT)
            dpw_ref[g] += _dot_tn(dc, dy_ext[:tm])
            dd = _dot_nt(dy_ext, pw_ref[g])
            r = dd / jnp.minimum(pos_ext + 1, w).astype(F32)
            span = 1
            while span < w:
                r = r + pltpu.roll(r, ext_rows - span, axis=0)
                span *= 2
            dus.append(r[:tm] - dd[:tm])
        du_ref[...] = jnp.concatenate(dus, axis=1).astype(du_ref.dtype)
        dsc_ref[...] += jnp.concatenate(dscs, axis=1)

    row = pl.BlockSpec((tm, POOL_DIM), lambda i: (i, 0))
    prev = pl.BlockSpec((POOL_WMAX, POOL_DIM), lambda i: (jnp.maximum(i * hb - 1, 0), 0))
    nxt = pl.BlockSpec((POOL_WMAX, POOL_DIM), lambda i: (jnp.minimum((i + 1) * hb, nsteps * hb - 1), 0))
    return pl.pallas_call(
        body, name=name, grid=(nsteps,),
        in_specs=[row, prev, row, nxt, pl.BlockSpec(pool_w.shape, lambda i: (0, 0, 0)),
                  pl.BlockSpec((1, POOL_DIM), lambda i: (0, 0))],
        out_specs=[row, pl.BlockSpec(pool_w.shape, lambda i: (0, 0, 0)), pl.BlockSpec((1, POOL_DIM), lambda i: (0, 0))],
        out_shape=[jax.ShapeDtypeStruct((T, POOL_DIM), CDT), jax.ShapeDtypeStruct(pool_w.shape, F32),
                   jax.ShapeDtypeStruct((1, POOL_DIM), F32)],
        compiler_params=_params("arbitrary"),
    )(zu, zu, dpm, dpm, pool_w, scale.reshape(1, POOL_DIM))


def _rope_tables(T):
    pos = jnp.arange(T, dtype=F32)
    inv_freq = ROPE_THETA ** (-jnp.arange(0, ROT_DIM, 2, dtype=F32) / ROT_DIM)
    ang = pos[:, None] * inv_freq[None, :]
    cos, sin = jnp.cos(ang), jnp.sin(ang)
    rest = HEAD_DIM - ROT_DIM
    cos_h = jnp.concatenate([cos, cos, jnp.ones((T, rest), F32)], axis=1)
    sin_h = jnp.concatenate([-sin, sin, jnp.zeros((T, rest), F32)], axis=1)
    return jnp.tile(cos_h, (1, 2)), jnp.tile(sin_h, (1, 2))


def _lane_masks():
    lane = lax.broadcasted_iota(jnp.int32, (1, LANES), 1)
    in_head = lane % HEAD_DIM
    return lane < HEAD_DIM, in_head < ROT_DIM // 2


def _rope_partner(v, low):
    lane = lax.broadcasted_iota(jnp.int32, (1, LANES), 1)
    swapped = jnp.where(low, pltpu.roll(v, LANES - ROT_DIM // 2, axis=1), pltpu.roll(v, ROT_DIM // 2, axis=1))
    return jnp.where(lane % HEAD_DIM < ROT_DIM, swapped, 0.0)


def _head_mean(v, first):
    lo = jnp.sum(jnp.where(first, v, 0.0), axis=-1, keepdims=True)
    hi = jnp.sum(jnp.where(first, 0.0, v), axis=-1, keepdims=True)
    return jnp.where(first, lo, hi) * (1.0 / HEAD_DIM)


def _qk_fwd(zqk, gqk, cos_t, sin_t, name):
    T = zqk.shape[0]
    tm = _tile(T, 512)

    def body(z_ref, g_ref, c_ref, s_ref, o_ref):
        first, low = _lane_masks()
        cosv, sinv = c_ref[...], s_ref[...]
        for c in range(QK_DIM // LANES):
            sl = slice(c * LANES, (c + 1) * LANES)
            xv = z_ref[:, sl]
            r = lax.rsqrt(_head_mean(xv * xv, first) + EPS)
            xn = xv * r * g_ref[:, sl]
            o_ref[:, sl] = (xn * cosv + _rope_partner(xn, low) * sinv).astype(o_ref.dtype)

    row = pl.BlockSpec((tm, QK_DIM), lambda i: (i, 0))
    tab = pl.BlockSpec((tm, LANES), lambda i: (i, 0))
    return pl.pallas_call(
        body, name=name, grid=(T // tm,),
        in_specs=[row, pl.BlockSpec((1, QK_DIM), lambda i: (0, 0)), tab, tab], out_specs=row,
        out_shape=jax.ShapeDtypeStruct((T, QK_DIM), CDT), compiler_params=_params("parallel"),
    )(zqk, gqk, cos_t, sin_t)


def _qk_bwd(dqk, zqk, gqk, cos_t, sin_t, name):
    T = zqk.shape[0]
    tm = _tile(T, 512)

    def body(d_ref, z_ref, g_ref, c_ref, s_ref, dz_ref, dg_ref):
        @pl.when(pl.program_id(0) == 0)
        def _():
            dg_ref[...] = jnp.zeros_like(dg_ref)

        first, low = _lane_masks()
        cosv, sinv = c_ref[...], s_ref[...]
        dgs = []
        for c in range(QK_DIM // LANES):
            sl = slice(c * LANES, (c + 1) * LANES)
            dout = d_ref[:, sl]
            dxn = dout * cosv + _rope_partner(dout * sinv, low)
            xv = z_ref[:, sl]
            r = lax.rsqrt(_head_mean(xv * xv, first) + EPS)
            xh = xv * r
            dgs.append(jnp.sum(dxn * xh, axis=0, keepdims=True))
            dxh = dxn * g_ref[:, sl]
            dz_ref[:, sl] = (r * (dxh - xh * _head_mean(dxh * xh, first))).astype(dz_ref.dtype)
        dg_ref[...] += jnp.concatenate(dgs, axis=1)

    row = pl.BlockSpec((tm, QK_DIM), lambda i: (i, 0))
    tab = pl.BlockSpec((tm, LANES), lambda i: (i, 0))
    vec = pl.BlockSpec((1, QK_DIM), lambda i: (0, 0))
    return pl.pallas_call(
        body, name=name, grid=(T // tm,),
        in_specs=[row, row, vec, tab, tab], out_specs=[row, vec],
        out_shape=[jax.ShapeDtypeStruct((T, QK_DIM), CDT), jax.ShapeDtypeStruct((1, QK_DIM), F32)],
        compiler_params=_params("arbitrary"),
    )(dqk, zqk, gqk, cos_t, sin_t)


def _dup_half(v, first, kv):
    swapped = pltpu.roll(v, HEAD_DIM, axis=1)
    return jnp.where(first, v, swapped) if kv == 0 else jnp.where(first, swapped, v)


HEADS_PER_KV = 4
HEAD_STACK = 1


def _attn_bias():
    qi = lax.broadcasted_iota(jnp.int32, (HEAD_STACK * BLOCK, 2 * BLOCK), 0) % BLOCK
    ki = lax.broadcasted_iota(jnp.int32, (HEAD_STACK * BLOCK, 2 * BLOCK), 1)
    diff = qi + BLOCK - ki
    band = (diff >= 0) & (diff < BLOCK)
    return jnp.stack([jnp.where(band, 0.0, -jnp.inf), jnp.where(band & (ki >= BLOCK), 0.0, -jnp.inf)]).astype(F32)


def _attn_blocks(T):
    return _tile(T // BLOCK, 4, 1)


def _stack_heads(ref, rows, kv, heads, first):
    parts = []
    for h in heads:
        c = 2 * kv + h // 2
        v = ref[rows, c * LANES:(c + 1) * LANES].astype(CDT)
        zero = jnp.zeros_like(v)
        parts.append(jnp.where(first, v, zero) if h % 2 == 0 else jnp.where(first, zero, v))
    return parts[0] if len(parts) == 1 else jnp.concatenate(parts, axis=0)


def _row_blocks(v, n):
    return [v[b * BLOCK:(b + 1) * BLOCK] for b in range(n)]


def _sink_column(sink_ref, kv, heads):
    cols = [jnp.full((BLOCK, 1), sink_ref[HEADS_PER_KV * kv + h], F32) for h in heads]
    return cols[0] if len(cols) == 1 else jnp.concatenate(cols, axis=0)


def _head_groups():
    return [tuple(range(g, g + HEAD_STACK)) for g in range(0, HEADS_PER_KV, HEAD_STACK)]


def _softmax_with_sink(qst, kdup, sinkcol, bias):
    s = _dot_nt(qst, kdup) * ATTN_SCALE + bias
    m = jnp.maximum(jnp.max(s, axis=-1, keepdims=True), sinkcol)
    pu = jnp.exp(s - m)
    es = jnp.exp(sinkcol - m)
    inv = 1.0 / (jnp.sum(pu, axis=-1, keepdims=True) + es)
    return pu * inv, es * inv


def _attn_fwd(qkn, zv, sinks, name):
    T = qkn.shape[0]
    R = _attn_blocks(T)
    tq = R * BLOCK

    def body(sink_ref, bias_ref, qk_ref, qkp_ref, v_ref, vp_ref, o_ref):
        i = pl.program_id(0)
        first, _ = _lane_masks()
        kall = jnp.concatenate([qkp_ref[:, ATTN_DIM:], qk_ref[:, ATTN_DIM:]], axis=0)
        vall = jnp.concatenate([vp_ref[...], v_ref[...]], axis=0).astype(CDT)
        for r in range(R):
            bias = bias_ref[jnp.where(i == 0, 1, 0)] if r == 0 else bias_ref[0]
            rows = slice(r * BLOCK, (r + 2) * BLOCK)
            qrows = slice(r * BLOCK, (r + 1) * BLOCK)
            for kv in range(2):
                kdup = _dup_half(kall[rows], first, kv)
                vdup = _dup_half(vall[rows], first, kv)
                res = []
                for heads in _head_groups():
                    p, _ = _softmax_with_sink(_stack_heads(qk_ref, qrows, kv, heads, first), kdup,
                                              _sink_column(sink_ref, kv, heads), bias)
                    res += _row_blocks(_dot(p.astype(CDT), vdup), len(heads))
                o_ref[qrows, 2 * kv * LANES:(2 * kv + 1) * LANES] = jnp.where(first, res[0], res[1]).astype(o_ref.dtype)
                o_ref[qrows, (2 * kv + 1) * LANES:(2 * kv + 2) * LANES] = jnp.where(first, res[2], res[3]).astype(o_ref.dtype)

    bias = _attn_bias()
    prev = lambda i: (jnp.maximum(i * R - 1, 0), 0)
    return pl.pallas_call(
        body, name=name, grid=(T // tq,),
        in_specs=[pl.BlockSpec(memory_space=pltpu.SMEM), pl.BlockSpec(bias.shape, lambda i: (0, 0, 0)),
                  pl.BlockSpec((tq, QK_DIM), lambda i: (i, 0)), pl.BlockSpec((BLOCK, QK_DIM), prev),
                  pl.BlockSpec((tq, KV_DIM), lambda i: (i, 0)), pl.BlockSpec((BLOCK, KV_DIM), prev)],
        out_specs=pl.BlockSpec((tq, ATTN_DIM), lambda i: (i, 0)),
        out_shape=jax.ShapeDtypeStruct((T, ATTN_DIM), CDT), compiler_params=_params("parallel"),
    )(sinks, bias, qkn, qkn, zv, zv)


def _attn_bwd(qkn, zv, sinks, do, name, side=None):
    T = qkn.shape[0]
    R = _attn_blocks(T)
    tq = R * BLOCK

    def body(sink_ref, bias_ref, qk_ref, qkp_ref, v_ref, vp_ref, do_ref, dq_ref, dkc_ref, dkp_ref, dvc_ref, dvp_ref, ds_ref):
        i = pl.program_id(0)

        @pl.when(i == 0)
        def _():
            ds_ref[...] = jnp.zeros_like(ds_ref)

        first, _ = _lane_masks()
        kall = jnp.concatenate([qkp_ref[:, ATTN_DIM:], qk_ref[:, ATTN_DIM:]], axis=0)
        vall = jnp.concatenate([vp_ref[...], v_ref[...]], axis=0).astype(CDT)
        for r in range(R):
            bias = bias_ref[jnp.where(i == 0, 1, 0)] if r == 0 else bias_ref[0]
            rows = slice(r * BLOCK, (r + 2) * BLOCK)
            qrows = slice(r * BLOCK, (r + 1) * BLOCK)
            dk_out, dv_out = [], []
            for kv in range(2):
                kdup = _dup_half(kall[rows], first, kv)
                vdup = _dup_half(vall[rows], first, kv)
                dq_h = []
                dk_acc = jnp.zeros((2 * BLOCK, LANES), F32)
                dv_acc = jnp.zeros((2 * BLOCK, LANES), F32)
                for heads in _head_groups():
                    qst = _stack_heads(qk_ref, qrows, kv, heads, first)
                    dost = _stack_heads(do_ref, qrows, kv, heads, first)
                    p, psink = _softmax_with_sink(qst, kdup, _sink_column(sink_ref, kv, heads), bias)
                    dp = _dot_nt(dost, vdup)
                    delta = jnp.sum(p * dp, axis=-1, keepdims=True)
                    dsc = (p * (dp - delta)).astype(CDT)
                    for b, term in enumerate(_row_blocks(psink * delta, len(heads))):
                        row = HEADS_PER_KV * kv + heads[b]
                        ds_ref[row:row + 1, :] += jnp.sum(term, axis=0, keepdims=True)
                    dq_h += _row_blocks(_dot(dsc, kdup) * ATTN_SCALE, len(heads))
                    dk_acc = dk_acc + _dot_tn(dsc, qst) * ATTN_SCALE
                    dv_acc = dv_acc + _dot_tn(p.astype(CDT), dost)
                dq_ref[qrows, 2 * kv * LANES:(2 * kv + 1) * LANES] = jnp.where(first, dq_h[0], dq_h[1])
                dq_ref[qrows, (2 * kv + 1) * LANES:(2 * kv + 2) * LANES] = jnp.where(first, dq_h[2], dq_h[3])
                dk_out.append(dk_acc + pltpu.roll(dk_acc, HEAD_DIM, axis=1))
                dv_out.append(dv_acc + pltpu.roll(dv_acc, HEAD_DIM, axis=1))
            dk = jnp.where(first, dk_out[0], dk_out[1])
            dv = jnp.where(first, dv_out[0], dv_out[1])
            dkp_ref[qrows, :] = dk[:BLOCK]
            dkc_ref[qrows, :] = dk[BLOCK:]
            dvp_ref[qrows, :] = dv[:BLOCK]
            dvc_ref[qrows, :] = dv[BLOCK:]

    bias = _attn_bias()
    prev = lambda i: (jnp.maximum(i * R - 1, 0), 0)
    kvrow = pl.BlockSpec((tq, KV_DIM), lambda i: (i, 0))
    qrow = pl.BlockSpec((tq, ATTN_DIM), lambda i: (i, 0))
    kv_shape = jax.ShapeDtypeStruct((T, KV_DIM), F32)
    return _pcall(
        body, name, (T // tq,),
        [pl.BlockSpec(memory_space=pltpu.SMEM), pl.BlockSpec(bias.shape, lambda i: (0, 0, 0)),
         pl.BlockSpec((tq, QK_DIM), lambda i: (i, 0)), pl.BlockSpec((BLOCK, QK_DIM), prev),
         kvrow, pl.BlockSpec((BLOCK, KV_DIM), prev), qrow],
        [qrow, kvrow, kvrow, kvrow, kvrow, pl.BlockSpec((N_Q_HEADS, LANES), lambda i: (0, 0))],
        [jax.ShapeDtypeStruct((T, ATTN_DIM), F32), kv_shape, kv_shape, kv_shape, kv_shape,
         jax.ShapeDtypeStruct((N_Q_HEADS, LANES), F32)],
        (sinks, bias, qkn, qkn, zv, zv, do), ("arbitrary",), side)


def _merge_fwd(pm, o, w_pb, w_ab, zg, name):
    T = pm.shape[0]
    tm = _tile(T, 512)

    def body(pm_ref, o_ref, wp_ref, wa_ref, zg_ref, a_ref, b_ref, m_ref):
        pmv, ov = pm_ref[...], o_ref[...]
        a = jnp.concatenate([_dot(pmv, wp_ref[j]) for j in range(N_CHIPS)], axis=1)
        b = jnp.concatenate([_dot(ov, wa_ref[j]) for j in range(N_CHIPS)], axis=1)
        gp = _sigmoid(zg_ref[:, :D_MODEL])
        ga = _sigmoid(zg_ref[:, D_MODEL:])
        a_ref[...] = a.astype(a_ref.dtype)
        b_ref[...] = b.astype(b_ref.dtype)
        m_ref[...] = (gp * a + ga * b).astype(m_ref.dtype)

    half = pl.BlockSpec((tm, POOL_DIM), lambda i: (i, 0))
    full = pl.BlockSpec((tm, D_MODEL), lambda i: (i, 0))
    wspec = pl.BlockSpec(w_pb.shape, lambda i: (0, 0, 0))
    out = jax.ShapeDtypeStruct((T, D_MODEL), CDT)
    return pl.pallas_call(
        body, name=name, grid=(T // tm,),
        in_specs=[half, half, wspec, wspec, pl.BlockSpec((tm, GATE_DIM), lambda i: (i, 0))],
        out_specs=[full, full, full], out_shape=[out, out, out], compiler_params=_params("parallel"),
    )(pm, o, w_pb, w_ab, zg)


def _merge_bwd(dxo, w_out, a, b, zg, name):
    T = dxo.shape[0]
    tm = _tile(T, 512)

    def body(dx_ref, w_ref, a_ref, b_ref, zg_ref, da_ref, db_ref, dg_ref):
        dm = _dot_nt(dx_ref[...].astype(CDT), w_ref[...])
        gp = _sigmoid(zg_ref[:, :D_MODEL])
        ga = _sigmoid(zg_ref[:, D_MODEL:])
        da_ref[...] = (dm * gp).astype(da_ref.dtype)
        db_ref[...] = (dm * ga).astype(db_ref.dtype)
        dg_ref[:, :D_MODEL] = (dm * a_ref[...].astype(F32) * (gp * (1.0 - gp))).astype(dg_ref.dtype)
        dg_ref[:, D_MODEL:] = (dm * b_ref[...].astype(F32) * (ga * (1.0 - ga))).astype(dg_ref.dtype)

    full = pl.BlockSpec((tm, D_MODEL), lambda i: (i, 0))
    gate = pl.BlockSpec((tm, GATE_DIM), lambda i: (i, 0))
    out = jax.ShapeDtypeStruct((T, D_MODEL), CDT)
    return pl.pallas_call(
        body, name=name, grid=(T // tm,),
        in_specs=[full, pl.BlockSpec((D_MODEL, D_MODEL), lambda i: (0, 0)), full, full, gate],
        out_specs=[full, full, gate], out_shape=[out, out, jax.ShapeDtypeStruct((T, GATE_DIM), CDT)],
        compiler_params=_params("parallel"),
    )(dxo, w_out, a, b, zg)


def _adamw(w, g, m, v, name):
    Rr, C = w.shape
    tr = _tile(Rr, max(8, (1 << 19) // C // 8 * 8))

    def body(w_ref, g_ref, m_ref, v_ref, d_ref, nm_ref, nv_ref):
        gv = g_ref[...]
        nm = ADAM_B1 * m_ref[...] + (1.0 - ADAM_B1) * gv
        nv = ADAM_B2 * v_ref[...] + (1.0 - ADAM_B2) * (gv * gv)
        m_hat = nm / (1.0 - ADAM_B1 ** ADAM_STEP)
        v_hat = nv / (1.0 - ADAM_B2 ** ADAM_STEP)
        d_ref[...] = -ADAM_LR * (m_hat / (jnp.sqrt(v_hat) + ADAM_EPS) + ADAM_WD * w_ref[...])
        nm_ref[...] = nm
        nv_ref[...] = nv

    blk = pl.BlockSpec((tr, C), lambda i: (i, 0))
    out = jax.ShapeDtypeStruct((Rr, C), F32)
    return pl.pallas_call(
        body, name=name, grid=(Rr // tr,), in_specs=[blk] * 4, out_specs=[blk] * 3, out_shape=[out] * 3,
        compiler_params=_params("parallel"),
    )(w, g, m, v)


def _place():
    return lax.axis_index("x"), lax.axis_index("y"), lax.axis_index("c")


def _other_chip(x, y, d):
    return (1 - x if d & 2 else x), (1 - y if d & 1 else y)


def _rcopy(src, dst, ssem, rsem, dev):
    return pltpu.make_async_remote_copy(src_ref=src, dst_ref=dst, send_sem=ssem, recv_sem=rsem, device_id=dev,
                                        device_id_type=MESH)


def _row_half(rows, c):
    return pl.ds(c * (rows // 2), rows // 2)


def _gather_ici_side(shards, l):
    n = len(shards)

    def issue(ins, outs, ssem, rsem):
        x, y, c = _place()
        cps = []
        for w in range(n):
            half = _row_half(shards[w].shape[1], c)
            for d in (1, 2, 3):
                px, py = _other_chip(x, y, d)
                k = 3 * w + d - 1
                cps.append(_rcopy(ins[w].at[l, half], outs[w].at[2 * x + y, half], ssem.at[k], rsem.at[k], (px, py, c)))
        return cps

    return _Side(shards, [jax.ShapeDtypeStruct((N_CHIPS,) + s.shape[1:], s.dtype) for s in shards], 3 * n, issue)


def _gather_d2d_side(shards, gathered, l):
    n = len(shards)

    def issue(ins, outs, ssem, rsem):
        x, y, c = _place()
        sibling = (x, y, 1 - c)
        cps = []
        for w in range(n):
            half = _row_half(shards[w].shape[1], c)
            for d in (1, 2, 3):
                px, py = _other_chip(x, y, d)
                k = 3 * w + d - 1
                got = outs[w].at[2 * px + py, half]
                cps.append(_rcopy(got, got, ssem.at[k], rsem.at[k], sibling))
            cps.append(_rcopy(ins[n + w].at[l], outs[w].at[2 * x + y], ssem.at[3 * n + w], rsem.at[3 * n + w], sibling))
        return cps

    return _Side(list(gathered) + list(shards), [jax.ShapeDtypeStruct(g.shape, g.dtype) for g in gathered], 4 * n, issue,
                 aliases={w: w for w in range(n)})


def _reduce_sibling_side(gms):
    n = len(gms)

    def issue(ins, outs, ssem, rsem):
        x, y, c = _place()
        return [_rcopy(ins[w].at[:, _row_half(gms[w].shape[1], 1 - c)], outs[w], ssem.at[w], rsem.at[w], (x, y, 1 - c))
                for w in range(n)]

    return _Side(gms, [jax.ShapeDtypeStruct((N_CHIPS, g.shape[1] // 2, g.shape[2]), g.dtype) for g in gms], n, issue)


def _reduce_chip_side(ps):
    n = len(ps)

    def issue(ins, outs, ssem, rsem):
        x, y, c = _place()
        cps = []
        for w in range(n):
            for d in (1, 2, 3):
                px, py = _other_chip(x, y, d)
                k = 3 * w + d - 1
                cps.append(_rcopy(ins[w].at[2 * px + py], outs[w].at[2 * x + y], ssem.at[k], rsem.at[k], (px, py, c)))
        return cps

    return _Side(ps, [jax.ShapeDtypeStruct(p.shape, p.dtype) for p in ps], 3 * n, issue)


def _share_side(accs):
    n = len(accs)

    def issue(ins, outs, ssem, rsem):
        x, y, c = _place()
        cps = []
        for w in range(n):
            mine = outs[w].at[:, _row_half(accs[w].shape[1], c)]
            cps.append(_rcopy(mine, mine, ssem.at[w], rsem.at[w], (x, y, 1 - c)))
        return cps

    return _Side(accs, [jax.ShapeDtypeStruct(a.shape, a.dtype) for a in accs], n, issue, aliases={w: w for w in range(n)})


def _sum_rows(rows, b):
    return _tile(rows, max(16, (1 << 19) // b // 16 * 16), 16)


def _pair_sum(g, recv, place, name):
    _, ah, b = recv.shape
    ta = _sum_rows(ah, b)
    nr = ah // ta

    def body(p_ref, g_ref, r_ref, o_ref):
        o_ref[...] = (g_ref[...].astype(F32) + r_ref[...].astype(F32)).astype(o_ref.dtype)

    blk = (None, ta, b)
    return pl.pallas_call(
        body, name=name,
        grid_spec=pltpu.PrefetchScalarGridSpec(
            num_scalar_prefetch=1, grid=(N_CHIPS, nr),
            in_specs=[pl.BlockSpec(blk, lambda j, r, p: (j, p[0] * nr + r, 0)),
                      pl.BlockSpec(blk, lambda j, r, p: (j, r, 0))],
            out_specs=pl.BlockSpec(blk, lambda j, r, p: (j, r, 0))),
        out_shape=jax.ShapeDtypeStruct(recv.shape, recv.dtype),
        compiler_params=_params("parallel", "parallel"),
    )(place, g, recv)


def _chip_sum(slots, part, place, acc, l, name):
    _, ah, b = slots.shape
    ta = _sum_rows(ah, b)
    nr = ah // ta

    def body(p_ref, s_ref, own_ref, acc_ref, o_ref):
        j = p_ref[1]
        own = own_ref[...].astype(F32)
        term = [jnp.where(j == s_, own, s_ref[s_].astype(F32)) for s_ in range(N_CHIPS)]
        o_ref[...] = ((term[0] + term[1]) + term[2]) + term[3]

    return pl.pallas_call(
        body, name=name,
        grid_spec=pltpu.PrefetchScalarGridSpec(
            num_scalar_prefetch=1, grid=(nr,),
            in_specs=[pl.BlockSpec((N_CHIPS, ta, b), lambda r, p: (0, r, 0)),
                      pl.BlockSpec((None, ta, b), lambda r, p: (p[1], r, 0)), ANY],
            out_specs=pl.BlockSpec((None, ta, b), lambda r, p: (l, p[0] * nr + r, 0))),
        out_shape=jax.ShapeDtypeStruct(acc.shape, F32), input_output_aliases={3: 0},
        compiler_params=_params("parallel"),
    )(place, slots, part, acc)


def _all_reduce_small(v):
    Rr = v.shape[0]

    def body(v_ref, slots_ref, out_ref, ssem, rsem):
        x, y, c = _place()
        me = 4 * x + 2 * y + c
        slots_ref[pl.ds(me, 1)] = v_ref[...][None]
        cps = []
        for d in range(1, N_DEV):
            px, py = _other_chip(x, y, d >> 1)
            pc = 1 - c if d & 1 else c
            cps.append(_rcopy(v_ref, slots_ref.at[me], ssem.at[d - 1], rsem.at[d - 1], (px, py, pc)))
            cps[-1].start()
        for cp in cps:
            cp.wait_recv()
        for cp in cps:
            cp.wait_send()
        acc = slots_ref[0]
        for s in range(1, N_DEV):
            acc = acc + slots_ref[s]
        out_ref[...] = acc

    vm = pl.BlockSpec(memory_space=pltpu.VMEM)
    return pl.pallas_call(
        body, name="all_reduce_small", in_specs=[vm], out_specs=[vm, vm],
        out_shape=[jax.ShapeDtypeStruct((N_DEV, Rr, LANES), F32), jax.ShapeDtypeStruct((Rr, LANES), F32)],
        scratch_shapes=[pltpu.SemaphoreType.DMA((N_DEV - 1,)), pltpu.SemaphoreType.DMA((N_DEV - 1,))],
        compiler_params=pltpu.CompilerParams(vmem_limit_bytes=VMEM_LIMIT_BYTES),
    )(v)[1]


def _ffn_forward(x, ln, wgu, wd, tag, side_of):
    h = _norm_fwd(x, ln, f"{tag}_norm")
    gu, act = _ffn_up(h, wgu, f"{tag}_up", side_of(f"{tag}_up"))
    x_out = _mm_nn(act, wd, f"{tag}_down", F32, res=x, scale=0.5, side=side_of(f"{tag}_down"))
    return x_out, (x, h, gu, act)


def _row_blocks_of(dw):
    return dw.reshape(N_CHIPS, dw.shape[0] // N_CHIPS, dw.shape[1])


def _ffn_backward(dxo, saved, ln, wgu, wd, tag, side_of):
    x, h, gu, act = saved
    dgu = _ffn_down_bwd(dxo, wd, gu, f"{tag}_down_bwd", side_of(f"{tag}_down_bwd"))
    d_wd = _row_blocks_of(_mm_tn(act, dxo, f"{tag}_dwd", scale=0.5))
    d_wgu = _mm_tn(h, dgu, f"{tag}_dwgu", col_blocks=N_CHIPS, tn_target=1408, tm_target=1024)
    dx, d_ln = _mm_nt_norm_bwd([dgu], wgu, x, ln, dxo, f"{tag}_dh_norm_bwd")
    return dx, d_ln, d_wgu, d_wd


def _mixer_forward(x, p, tabs):
    h = _norm_fwd(x, p["ln_mix"], "mix_norm")
    zu, zqk, zv, zg = _mm_in(h, p["w_in"], "mix_in")
    pm = _pool_fwd(zu, p["pool_w"], p["pool_scale"], "pool_fwd")
    qkn = _qk_fwd(zqk, p["gqk"], *tabs, "qk_fwd")
    o = _attn_fwd(qkn, zv, p["sinks"], "attn_fwd")
    a, b, m = _merge_fwd(pm, o, p["w_pool_branch"], p["w_attn_branch"], zg, "merge_fwd")
    x_out = _mm_nn(m, p["w_out"], "mix_out", F32, res=x, scale=1.0)
    return x_out, (x, h, zu, zqk, zv, zg, pm, qkn, o, a, b, m)


def _shift_up(v):
    return jnp.concatenate([v[BLOCK:], jnp.zeros((BLOCK, v.shape[1]), v.dtype)], axis=0)


def _mixer_backward(dxo, saved, p, tabs, side_of):
    x, h, zu, zqk, zv, zg, pm, qkn, o, a, b, m = saved
    g = {}
    d_a, d_b, dgl = _merge_bwd(dxo, p["w_out"], a, b, zg, "merge_bwd")
    g["w_out"] = _row_blocks_of(_mm_tn(m, dxo, "mix_dwout"))
    dpm = _mm_nt_blocks(d_a, p["w_pool_branch"], "pool_branch_dx", CDT)
    g["w_pool_branch"] = _mm_tn(pm, d_a, "pool_branch_dw", col_blocks=N_CHIPS)
    do = _mm_nt_blocks(d_b, p["w_attn_branch"], "attn_branch_dx", CDT)
    g["w_attn_branch"] = _mm_tn(o, d_b, "attn_branch_dw", col_blocks=N_CHIPS)
    du, g["pool_w"], g["pool_scale"] = _pool_bwd(zu, dpm, p["pool_w"], p["pool_scale"], "pool_bwd")
    dq, dkc, dkp, dvc, dvp, dsink = _attn_bwd(qkn, zv, p["sinks"], do, "attn_bwd", side_of("attn_bwd"))
    dqk = jnp.concatenate([dq, dkc + _shift_up(dkp)], axis=1)
    dv = dvc + _shift_up(dvp)
    dzqk, dgqk = _qk_bwd(dqk, zqk, p["gqk"], *tabs, "qk_bwd")
    g["q_norm"] = dgqk[0, :ATTN_DIM].reshape(N_Q_HEADS, HEAD_DIM).sum(axis=0)
    g["k_norm"] = dgqk[0, ATTN_DIM:].reshape(KV_DIM // HEAD_DIM, HEAD_DIM).sum(axis=0)
    g["sinks"] = -dsink[:, 0]
    dz = [du, dzqk, dv, dgl]
    g["w_in"] = _blocks_from_full("w_in", _mm_tn_parts(h, dz, "mix_dwin")).astype(WIRE_DT)
    dx, d_ln = _mm_nt_norm_bwd(dz, p["w_in"], x, p["ln_mix"], dxo, "mix_dh_norm_bwd")
    g["ln_mix"] = d_ln[0]
    return dx, g


class _NoComm:
    def __init__(self, layers):
        self.layers, self.grads = layers, [None] * len(layers)

    def layer(self, l):
        return self.layers[l]

    def side(self, phase, l, host):
        return None

    def layer_grads(self, l, g):
        self.grads[l] = g


def _local_step(x, tgt, n_layers, hooks):
    T = x.shape[0]
    tabs = _rope_tables(T)
    saved, params = [], []
    for l in range(n_layers):
        p = hooks.layer(l)
        side_of = functools.partial(hooks.side, "fwd", l)
        x, s1 = _ffn_forward(x, p["ln_ffn1"], p["w_ffn1_gu"], p["w_ffn1_down"], "ffn1", side_of)
        x, s2 = _mixer_forward(x, p, tabs)
        x, s3 = _ffn_forward(x, p["ln_ffn2"], p["w_ffn2_gu"], p["w_ffn2_down"], "ffn2", side_of)
        saved.append((s1, s2, s3))
        params.append(p)
    dx, loss = _loss_head(x, tgt, "loss_head")
    for l in reversed(range(n_layers)):
        p = params[l]
        s1, s2, s3 = saved[l]
        side_of = functools.partial(hooks.side, "bwd", l)
        none = lambda host: None
        dx, d_ln2, d_gu2, d_dn2 = _ffn_backward(dx, s3, p["ln_ffn2"], p["w_ffn2_gu"], p["w_ffn2_down"], "ffn2", side_of)
        dx, g = _mixer_backward(dx, s2, p, tabs, side_of)
        dx, d_ln1, d_gu1, d_dn1 = _ffn_backward(dx, s1, p["ln_ffn1"], p["w_ffn1_gu"], p["w_ffn1_down"], "ffn1", none)
        g.update(ln_ffn1=d_ln1[0], w_ffn1_gu=d_gu1, w_ffn1_down=d_dn1, ln_ffn2=d_ln2[0], w_ffn2_gu=d_gu2, w_ffn2_down=d_dn2)
        hooks.layer_grads(l, g)
    return loss, dx


def _full_from_blocks(name, blocks):
    if name in COL_SHARDED:
        return jnp.transpose(blocks, (1, 0, 2)).reshape(blocks.shape[1], N_CHIPS * blocks.shape[2])
    return blocks.reshape(N_CHIPS * blocks.shape[1], blocks.shape[2])


def _blocks_from_full(name, full):
    K, N = full.shape
    if name in COL_SHARDED:
        return jnp.transpose(full.reshape(K, N_CHIPS, N // N_CHIPS), (1, 0, 2))
    return full.reshape(N_CHIPS, K // N_CHIPS, N)


GATHER_GROUPS = {"ffn1": ("w_ffn1_gu", "w_ffn1_down", "w_in"),
                 "ffn2": ("w_pool_branch", "w_attn_branch", "w_out", "w_ffn2_gu", "w_ffn2_down")}


class _Exchange:
    def __init__(self, shards, small, place, n_layers):
        self.shards, self.small, self.place, self.n_layers = shards, small, place, n_layers
        self.blocks = {}
        for names in GATHER_GROUPS.values():
            got = _run_side(_gather_ici_side([shards[n] for n in names], 0), "gather_ici")
            got = _run_side(_gather_d2d_side([shards[n] for n in names], got, 0), "gather_d2d")
            self.blocks.update({(n, 0): g for n, g in zip(names, got)})
        self.pending = {}
        self.acc = {n: lax.empty(shards[n].shape, F32) for n in BIG}
        self.small_grads = [None] * n_layers
        self.waiting = None

    def layer(self, l):
        self._collect_gathered()
        p = {n: self.blocks.pop((n, l)) for n in BIG}
        p.update({n: _full_from_blocks(n, p[n]) for n in BIG if n not in USED_AS_BLOCKS})
        p.update(self.small(l))
        return p

    def side(self, phase, l, host):
        if phase == "fwd" and l + 1 < self.n_layers:
            group, step = host.split("_")
            sh = [self.shards[n] for n in GATHER_GROUPS[group]]
            if step == "up":
                self.pending[group] = _gather_ici_side(sh, l + 1)
                return self.pending[group]
            done = _gather_d2d_side(sh, self.pending.pop(group).outs, l + 1)
            self.pending[group + "_done"] = (done, l + 1)
            return done
        if phase == "bwd" and self.waiting is not None:
            if host == "ffn2_down_bwd":
                self.waiting["sib"] = _reduce_sibling_side(self.waiting["gm"])
                return self.waiting["sib"]
            if host == "attn_bwd":
                self._pair_sums()
                self.waiting["chip"] = _reduce_chip_side(self.waiting["part"])
                return self.waiting["chip"]
        return None

    def _collect_gathered(self):
        for key in [k for k in self.pending if k.endswith("_done")]:
            done, l = self.pending.pop(key)
            self.blocks.update({(n, l): g for n, g in zip(GATHER_GROUPS[key[:-5]], done.outs)})

    def _pair_sums(self):
        w = self.waiting
        w["part"] = [_pair_sum(g, r, self.place, "grad_pair_sum") for g, r in zip(w["gm"], w["sib"].outs)]

    def _finish(self):
        w, self.waiting = self.waiting, None
        for n, slots, part in zip(BIG, w["chip"].outs, w["part"]):
            self.acc[n] = _chip_sum(slots, part, self.place, self.acc[n], w["l"], "grad_chip_sum")

    def layer_grads(self, l, g):
        if self.waiting is not None:
            self._finish()
        self.small_grads[l] = {n: g[n] for n in SMALL}
        self.waiting = dict(l=l, gm=[g[n] for n in BIG])

    def reduced(self):
        w = self.waiting
        w["sib"] = _reduce_sibling_side(w["gm"])
        _run_side(w["sib"], "grad_sibling_exchange")
        self._pair_sums()
        w["chip"] = _reduce_chip_side(w["part"])
        _run_side(w["chip"], "grad_chip_exchange")
        self._finish()
        return dict(zip(BIG, _run_side(_share_side([self.acc[n] for n in BIG]), "grad_sibling_share")))


def _pack_small(parts):
    rows, spans, lo = [], [], 0
    for v in parts:
        flat = v.reshape(-1)
        nrow = -(-flat.shape[0] // LANES)
        flat = jnp.pad(flat, (0, nrow * LANES - flat.shape[0]))
        rows.append(flat.reshape(nrow, LANES))
        spans.append((lo, nrow))
        lo += nrow
    pad = -lo % 8
    if pad:
        rows.append(jnp.zeros((pad, LANES), F32))
    return jnp.concatenate(rows, axis=0), spans


def _unpack_small(packed, spans, shapes):
    out = []
    for (lo, nrow), shape in zip(spans, shapes):
        size = 1
        for s in shape:
            size *= s
        out.append(packed[lo:lo + nrow].reshape(-1)[:size].reshape(shape))
    return out


def kernel(x, ln_ffn1, w_ffn1_gu, w_ffn1_down, ln_mix, w_in, pool_w, pool_scale, w_pool_branch, q_norm, k_norm, sinks, w_attn_branch, w_out, ln_ffn2, w_ffn2_gu, w_ffn2_down, loss_target, m_ln_ffn1, m_w_ffn1_gu, m_w_ffn1_down, m_ln_mix, m_w_in, m_pool_w, m_pool_scale, m_w_pool_branch, m_q_norm, m_k_norm, m_sinks, m_w_attn_branch, m_w_out, m_ln_ffn2, m_w_ffn2_gu, m_w_ffn2_down, v_ln_ffn1, v_w_ffn1_gu, v_w_ffn1_down, v_ln_mix, v_w_in, v_pool_w, v_pool_scale, v_w_pool_branch, v_q_norm, v_k_norm, v_sinks, v_w_attn_branch, v_w_out, v_ln_ffn2, v_w_ffn2_gu, v_w_ffn2_down):
    w = dict(ln_ffn1=ln_ffn1, w_ffn1_gu=w_ffn1_gu, w_ffn1_down=w_ffn1_down, ln_mix=ln_mix, w_in=w_in, pool_w=pool_w,
             pool_scale=pool_scale, w_pool_branch=w_pool_branch, q_norm=q_norm, k_norm=k_norm, sinks=sinks,
             w_attn_branch=w_attn_branch, w_out=w_out, ln_ffn2=ln_ffn2, w_ffn2_gu=w_ffn2_gu, w_ffn2_down=w_ffn2_down)
    mom = dict(ln_ffn1=m_ln_ffn1, w_ffn1_gu=m_w_ffn1_gu, w_ffn1_down=m_w_ffn1_down, ln_mix=m_ln_mix, w_in=m_w_in,
               pool_w=m_pool_w, pool_scale=m_pool_scale, w_pool_branch=m_w_pool_branch, q_norm=m_q_norm, k_norm=m_k_norm,
               sinks=m_sinks, w_attn_branch=m_w_attn_branch, w_out=m_w_out, ln_ffn2=m_ln_ffn2, w_ffn2_gu=m_w_ffn2_gu,
               w_ffn2_down=m_w_ffn2_down)
    var = dict(ln_ffn1=v_ln_ffn1, w_ffn1_gu=v_w_ffn1_gu, w_ffn1_down=v_w_ffn1_down, ln_mix=v_ln_mix, w_in=v_w_in,
               pool_w=v_pool_w, pool_scale=v_pool_scale, w_pool_branch=v_w_pool_branch, q_norm=v_q_norm, k_norm=v_k_norm,
               sinks=v_sinks, w_attn_branch=v_w_attn_branch, w_out=v_w_out, ln_ffn2=v_ln_ffn2, w_ffn2_gu=v_w_ffn2_gu,
               w_ffn2_down=v_w_ffn2_down)
    L = ln_ffn1.shape[0]

    def small(l):
        return dict(ln_ffn1=ln_ffn1[l], ln_mix=ln_mix[l], ln_ffn2=ln_ffn2[l], pool_w=pool_w[l].astype(CDT),
                    pool_scale=pool_scale[l], sinks=sinks[l],
                    gqk=jnp.concatenate([jnp.tile(q_norm[l], N_Q_HEADS), jnp.tile(k_norm[l], KV_DIM // HEAD_DIM)]).reshape(1, QK_DIM))

    place = jnp.stack([lax.axis_index("c"), 2 * lax.axis_index("x") + lax.axis_index("y")]).astype(jnp.int32)
    hooks = _Exchange({n: w[n].astype(CDT) for n in BIG}, small, place, L)
    loss_part, grad_x = _local_step(x[0], loss_target[0], L, hooks)
    g_big = hooks.reduced()
    grads = hooks.small_grads

    small_parts = [jnp.stack([g[n] for g in grads]) for n in SMALL] + [loss_part]
    packed, spans = _pack_small(small_parts)
    summed = _all_reduce_small(packed)
    *g_small_list, loss_sum = _unpack_small(summed, spans, [w[n].shape for n in SMALL] + [(1, 1)])
    g_small = dict(zip(SMALL, g_small_list))
    loss = loss_sum[0, 0]

    grad_out, delta, new_m, new_v = {}, {}, {}, {}
    for n in BIG:
        shape = w[n].shape
        flat = (shape[0] * shape[1], shape[2])
        grad_out[n] = g_big[n]
        d, nm, nv = _adamw(w[n].reshape(flat), g_big[n].reshape(flat), mom[n].reshape(flat), var[n].reshape(flat), "adamw")
        delta[n], new_m[n], new_v[n] = d.reshape(shape), nm.reshape(shape), nv.reshape(shape)
    pw, _ = _pack_small([w[n] for n in SMALL])
    pg, sp = _pack_small([g_small[n] for n in SMALL])
    pm_, _ = _pack_small([mom[n] for n in SMALL])
    pv, _ = _pack_small([var[n] for n in SMALL])
    d, nm, nv = _adamw(pw, pg, pm_, pv, "adamw_small")
    shapes = [w[n].shape for n in SMALL]
    for n, dv, mv, vv in zip(SMALL, _unpack_small(d, sp, shapes), _unpack_small(nm, sp, shapes), _unpack_small(nv, sp, shapes)):
        grad_out[n], delta[n], new_m[n], new_v[n] = g_small[n], dv, mv, vv

    return (loss, grad_x[None], *[grad_out[n] for n in WEIGHTS], *[delta[n] for n in WEIGHTS],
            *[new_m[n] for n in WEIGHTS], *[new_v[n] for n in WEIGHTS])
```

```python
import functools
import math

import jax
import jax.numpy as jnp
from jax import lax
from jax.experimental import pallas as pl
from jax.experimental.pallas import tpu as pltpu

F32 = jnp.float32
CDT = jnp.bfloat16
WIRE_DT = jnp.bfloat16

D_MODEL = 1024
POOL_WINDOWS = (2, 4, 8, 16)
POOL_WMAX = 16
GROUP = 128
POOL_DIM = 512
HEAD_DIM = 64
N_Q_HEADS = 8
ATTN_DIM = 512
KV_DIM = 128
QK_DIM = ATTN_DIM + KV_DIM
GATE_DIM = 2 * D_MODEL
BLOCK = 128
ROPE_THETA = 500000.0
ROT_DIM = 16
EPS = 1e-6
ATTN_SCALE = HEAD_DIM ** -0.5

ADAM_LR = 0.001
ADAM_B1 = 0.9
ADAM_B2 = 0.999
ADAM_EPS = 1e-08
ADAM_WD = 0.01
ADAM_STEP = 10

N_CHIPS = 4
N_DEV = 8
LANES = 128
VMEM_LIMIT_BYTES = 48 * 1024 * 1024

MESH = pl.DeviceIdType.MESH
ANY = pl.BlockSpec(memory_space=pl.ANY)

BIG = ("w_ffn1_gu", "w_ffn1_down", "w_in", "w_pool_branch", "w_attn_branch", "w_out", "w_ffn2_gu", "w_ffn2_down")
COL_SHARDED = ("w_ffn1_gu", "w_in", "w_pool_branch", "w_attn_branch", "w_ffn2_gu")
USED_AS_BLOCKS = ("w_ffn1_gu", "w_pool_branch", "w_attn_branch", "w_ffn2_gu")
SMALL = ("ln_ffn1", "ln_mix", "pool_w", "pool_scale", "q_norm", "k_norm", "sinks", "ln_ffn2")
WEIGHTS = ("ln_ffn1", "w_ffn1_gu", "w_ffn1_down", "ln_mix", "w_in", "pool_w", "pool_scale", "w_pool_branch",
           "q_norm", "k_norm", "sinks", "w_attn_branch", "w_out", "ln_ffn2", "w_ffn2_gu", "w_ffn2_down")


def _tile(n, target, mult=8):
    if n <= target:
        return n
    for t in range(target - target % mult, 0, -mult):
        if n % t == 0:
            return t
    raise ValueError((n, target, mult))


def _params(*sem):
    return pltpu.CompilerParams(dimension_semantics=sem, vmem_limit_bytes=VMEM_LIMIT_BYTES)


def _sigmoid(v):
    return 0.5 * jnp.tanh(0.5 * v) + 0.5


def _dot(a, b):
    return jnp.dot(a, b, preferred_element_type=F32)


def _dot_nt(a, b):
    return lax.dot_general(a, b, (((1,), (1,)), ((), ())), preferred_element_type=F32)


def _dot_tn(a, b):
    return lax.dot_general(a, b, (((0,), (0,)), ((), ())), preferred_element_type=F32)


class _Side:
    def __init__(self, ins, out_shapes, n_sems, issue, aliases=None):
        self.ins, self.out_shapes, self.n_sems, self.issue = list(ins), list(out_shapes), n_sems, issue
        self.aliases = dict(aliases or {})
        self.outs = None


def _pcall(body, name, grid, in_specs, out_specs, out_shape, args, dims, side=None):
    if side is None:
        return pl.pallas_call(body, name=name, grid=grid, in_specs=in_specs, out_specs=out_specs, out_shape=out_shape,
                              compiler_params=_params(*dims))(*args)
    n_in, n_out, s_in, s_out = len(in_specs), len(out_specs), len(side.ins), len(side.out_shapes)

    def wrapped(*refs):
        main_in, side_in = refs[:n_in], refs[n_in:n_in + s_in]
        main_out = refs[n_in + s_in:n_in + s_in + n_out]
        side_out = refs[n_in + s_in + n_out:n_in + s_in + n_out + s_out]
        ssem, rsem = refs[n_in + s_in + n_out + s_out:]
        ids = [pl.program_id(ax) for ax in range(len(grid))]
        first = functools.reduce(jnp.logical_and, [i == 0 for i in ids])
        last = functools.reduce(jnp.logical_and, [i == g - 1 for i, g in zip(ids, grid)])

        @pl.when(first)
        def _():
            for cp in side.issue(side_in, side_out, ssem, rsem):
                cp.start()

        body(*main_in, *main_out)

        @pl.when(last)
        def _():
            cps = side.issue(side_in, side_out, ssem, rsem)
            for cp in cps:
                cp.wait_recv()
            for cp in cps:
                cp.wait_send()

    outs = pl.pallas_call(
        wrapped, name=name, grid=grid, in_specs=list(in_specs) + [ANY] * s_in, out_specs=list(out_specs) + [ANY] * s_out,
        out_shape=list(out_shape) + side.out_shapes,
        input_output_aliases={n_in + i: n_out + o for i, o in side.aliases.items()},
        scratch_shapes=[pltpu.SemaphoreType.DMA((side.n_sems,))] * 2,
        compiler_params=_params(*["arbitrary"] * len(grid)),
    )(*args, *side.ins)
    side.outs = list(outs[n_out:])
    return list(outs[:n_out])


def _run_side(side, name):
    s_in = len(side.ins)

    def body(*refs):
        ssem, rsem = refs[s_in + len(side.out_shapes):]
        cps = side.issue(refs[:s_in], refs[s_in:s_in + len(side.out_shapes)], ssem, rsem)
        for cp in cps:
            cp.start()
        for cp in cps:
            cp.wait_recv()
        for cp in cps:
            cp.wait_send()

    side.outs = list(pl.pallas_call(
        body, name=name, in_specs=[ANY] * s_in, out_specs=[ANY] * len(side.out_shapes), out_shape=side.out_shapes,
        input_output_aliases=side.aliases, scratch_shapes=[pltpu.SemaphoreType.DMA((side.n_sems,))] * 2,
    )(*side.ins))
    return side.outs


def _norm_fwd(x, g, name):
    T, Dm = x.shape
    tm = _tile(T, 512)

    def body(x_ref, g_ref, h_ref):
        xv = x_ref[...]
        r = lax.rsqrt(jnp.mean(xv * xv, axis=-1, keepdims=True) + EPS)
        h_ref[...] = (xv * r * g_ref[...]).astype(h_ref.dtype)

    row = pl.BlockSpec((tm, Dm), lambda i: (i, 0))
    return pl.pallas_call(
        body, name=name, grid=(T // tm,),
        in_specs=[row, pl.BlockSpec((1, Dm), lambda i: (0, 0))], out_specs=row,
        out_shape=jax.ShapeDtypeStruct((T, Dm), CDT), compiler_params=_params("parallel"),
    )(x, g.reshape(1, Dm))


def _loss_head(y, tgt, name):
    T, Dm = y.shape
    tm = _tile(T, 512)

    def body(y_ref, t_ref, dy_ref, loss_ref):
        @pl.when(pl.program_id(0) == 0)
        def _():
            loss_ref[...] = jnp.zeros_like(loss_ref)

        diff = y_ref[...] - t_ref[...]
        dy_ref[...] = diff * (1.0 / Dm)
        part = jnp.sum(jnp.mean(diff * diff, axis=-1, keepdims=True), axis=0, keepdims=True)
        loss_ref[...] += 0.5 * part

    row = pl.BlockSpec((tm, Dm), lambda i: (i, 0))
    one = pl.BlockSpec((1, 1), lambda i: (0, 0))
    return pl.pallas_call(
        body, name=name, grid=(T // tm,),
        in_specs=[row, row], out_specs=[row, one],
        out_shape=[jax.ShapeDtypeStruct((T, Dm), F32), jax.ShapeDtypeStruct((1, 1), F32)],
        compiler_params=_params("arbitrary"),
    )(y, tgt)


def _mm_nn(a, b, name, out_dtype, res=None, scale=1.0, tm_target=512, side=None):
    M, K = a.shape
    N = b.shape[1]
    tm = _tile(M, tm_target)

    def body(a_ref, b_ref, *rest):
        acc = _dot(a_ref[...].astype(CDT), b_ref[...])
        if res is None:
            (o_ref,) = rest
        else:
            r_ref, o_ref = rest
            acc = r_ref[...] + scale * acc
        o_ref[...] = acc.astype(o_ref.dtype)

    in_specs = [pl.BlockSpec((tm, K), lambda i: (i, 0)), pl.BlockSpec((K, N), lambda i: (0, 0))]
    args = [a, b]
    if res is not None:
        in_specs.append(pl.BlockSpec((tm, N), lambda i: (i, 0)))
        args.append(res)
    return _pcall(body, name, (M // tm,), in_specs, [pl.BlockSpec((tm, N), lambda i: (i, 0))],
                  [jax.ShapeDtypeStruct((M, N), out_dtype)], args, ("parallel",), side)[0]


def _mm_nt_blocks(a, b4, name, out_dtype, tm_target=512):
    M, K = a.shape
    nb, N, Kb = b4.shape
    tm = _tile(M, tm_target)

    def body(a_ref, b_ref, o_ref):
        acc = _dot_nt(a_ref[:, :Kb].astype(CDT), b_ref[0])
        for j in range(1, nb):
            acc = acc + _dot_nt(a_ref[:, j * Kb:(j + 1) * Kb].astype(CDT), b_ref[j])
        o_ref[...] = acc.astype(o_ref.dtype)

    return pl.pallas_call(
        body, name=name, grid=(M // tm,),
        in_specs=[pl.BlockSpec((tm, K), lambda i: (i, 0)), pl.BlockSpec(b4.shape, lambda i: (0, 0, 0))],
        out_specs=pl.BlockSpec((tm, N), lambda i: (i, 0)),
        out_shape=jax.ShapeDtypeStruct((M, N), out_dtype), compiler_params=_params("parallel"),
    )(a, b4)


def _mm_tn(x, dy, name, scale=1.0, col_blocks=1, tn_target=1664, tm_target=1408, tk_target=1024):
    T, M = x.shape
    split = dy.ndim == 3
    Nh = dy.shape[-1]
    N = 2 * Nh if split else Nh
    nb = N // col_blocks
    whole = col_blocks > 1 and not split and N <= tn_target
    tm = _tile(M, tm_target, LANES)
    tn = N if whole else _tile(math.gcd(Nh, nb), tn_target, LANES)
    tk = _tile(T, tk_target)
    nk = T // tk
    njh, njb = Nh // tn, max(nb // tn, 1)

    def body(x_ref, dy_ref, o_ref, acc_ref):
        k = pl.program_id(2)

        @pl.when(k == 0)
        def _():
            acc_ref[...] = jnp.zeros_like(acc_ref)

        acc_ref[...] += _dot_tn(x_ref[...].astype(CDT), dy_ref[...].astype(CDT))

        @pl.when(k == nk - 1)
        def _():
            res = (acc_ref[...] if scale == 1.0 else scale * acc_ref[...]).astype(o_ref.dtype)
            if whole:
                for b in range(col_blocks):
                    o_ref[b] = res[:, b * nb:(b + 1) * nb]
            else:
                o_ref[...] = res

    if split:
        dy_spec = pl.BlockSpec((None, tk, tn), lambda i, j, k: (j // njh, k, j % njh))
    else:
        dy_spec = pl.BlockSpec((tk, tn), lambda i, j, k: (k, j))
    if col_blocks == 1:
        out_spec, out_dims = pl.BlockSpec((tm, tn), lambda i, j, k: (i, j)), (M, N)
    elif whole:
        out_spec, out_dims = pl.BlockSpec((col_blocks, tm, nb), lambda i, j, k: (0, i, 0)), (col_blocks, M, nb)
    else:
        out_spec, out_dims = pl.BlockSpec((None, tm, tn), lambda i, j, k: (j // njb, i, j % njb)), (col_blocks, M, nb)
    return pl.pallas_call(
        body, name=name, grid=(M // tm, N // tn, nk),
        in_specs=[pl.BlockSpec((tk, tm), lambda i, j, k: (k, i)), dy_spec], out_specs=out_spec,
        out_shape=jax.ShapeDtypeStruct(out_dims, WIRE_DT), scratch_shapes=[pltpu.VMEM((tm, tn), F32)],
        compiler_params=_params("parallel", "parallel", "arbitrary"),
    )(x, dy)


def _mm_tn_parts(x, parts, name):
    T, M = x.shape
    widths = [p.shape[1] for p in parts]
    N = sum(widths)
    tk = _tile(T, 512)

    def body(x_ref, *refs):
        o_ref = refs[-1]

        @pl.when(pl.program_id(0) == 0)
        def _():
            o_ref[...] = jnp.zeros_like(o_ref)

        xv = x_ref[...].astype(CDT)
        lo = 0
        for p_ref, wd in zip(refs[:-1], widths):
            o_ref[:, lo:lo + wd] += _dot_tn(xv, p_ref[...].astype(CDT))
            lo += wd

    return pl.pallas_call(
        body, name=name, grid=(T // tk,),
        in_specs=[pl.BlockSpec((tk, M), lambda k: (k, 0))] + [pl.BlockSpec((tk, wd), lambda k: (k, 0)) for wd in widths],
        out_specs=pl.BlockSpec((M, N), lambda k: (0, 0)),
        out_shape=jax.ShapeDtypeStruct((M, N), F32), compiler_params=_params("arbitrary"),
    )(x, *parts)


def _ffn_up(h, wgu4, name, side=None):
    T, Dm = h.shape
    tn = wgu4.shape[2]
    nj = 2
    Fd = nj * tn
    tm = _tile(T, 512)

    def body(h_ref, wg_ref, wu_ref, gu_ref, a_ref):
        hv = h_ref[...]
        g = _dot(hv, wg_ref[...])
        u = _dot(hv, wu_ref[...])
        gu_ref[0] = g.astype(gu_ref.dtype)
        gu_ref[1] = u.astype(gu_ref.dtype)
        a_ref[...] = (g * _sigmoid(g) * u).astype(a_ref.dtype)

    return _pcall(
        body, name, (nj, T // tm),
        [pl.BlockSpec((tm, Dm), lambda j, i: (i, 0)), pl.BlockSpec((None, Dm, tn), lambda j, i: (j, 0, 0)),
         pl.BlockSpec((None, Dm, tn), lambda j, i: (j + nj, 0, 0))],
        [pl.BlockSpec((2, tm, tn), lambda j, i: (0, i, j)), pl.BlockSpec((tm, tn), lambda j, i: (i, j))],
        [jax.ShapeDtypeStruct((2, T, Fd), CDT), jax.ShapeDtypeStruct((T, Fd), CDT)],
        (h, wgu4, wgu4), ("parallel", "parallel"), side)


def _ffn_down_bwd(dxo, wd, gu, name, side=None):
    T, Dm = dxo.shape
    Fd = wd.shape[0]
    tm = _tile(T, 512)
    tn = _tile(Fd, 1408, LANES)

    def body(dx_ref, wd_ref, gu_ref, dgu_ref):
        da = 0.5 * _dot_nt(dx_ref[...].astype(CDT), wd_ref[...])
        g = gu_ref[0].astype(F32)
        u = gu_ref[1].astype(F32)
        sg = _sigmoid(g)
        dgu_ref[0] = (da * u * (sg * (1.0 + g * (1.0 - sg)))).astype(dgu_ref.dtype)
        dgu_ref[1] = (da * (g * sg)).astype(dgu_ref.dtype)

    gu_spec = pl.BlockSpec((2, tm, tn), lambda j, i: (0, i, j))
    return _pcall(
        body, name, (Fd // tn, T // tm),
        [pl.BlockSpec((tm, Dm), lambda j, i: (i, 0)), pl.BlockSpec((tn, Dm), lambda j, i: (j, 0)), gu_spec],
        [gu_spec], [jax.ShapeDtypeStruct((2, T, Fd), CDT)],
        (dxo, wd, gu), ("parallel", "parallel"), side)[0]


def _mm_nt_norm_bwd(a_parts, b, x, g, dres, name):
    T, Dm = x.shape
    tm = _tile(T, 256)

    def b_cols(b_ref, lo, wd):
        if b.ndim == 2:
            return [(0, wd, b_ref[:, lo:lo + wd])]
        kb = b.shape[2]
        return [(j * kb - lo, kb, b_ref[j]) for j in range(lo // kb, (lo + wd) // kb)]

    def body(*refs):
        a_refs, (b_ref, x_ref, g_ref, dres_ref, dx_ref, dg_ref) = refs[:len(a_parts)], refs[len(a_parts):]

        @pl.when(pl.program_id(0) == 0)
        def _():
            dg_ref[...] = jnp.zeros_like(dg_ref)

        dh, lo = None, 0
        for a_ref, part in zip(a_refs, a_parts):
            slabs = [a_ref] if part.ndim == 2 else [a_ref.at[s_] for s_ in range(part.shape[0])]
            for slab in slabs:
                for off, wd, bv in b_cols(b_ref, lo, part.shape[-1]):
                    term = _dot_nt(slab[:, off:off + wd].astype(CDT), bv)
                    dh = term if dh is None else dh + term
                lo += part.shape[-1]
        xv = x_ref[...]
        r = lax.rsqrt(jnp.mean(xv * xv, axis=-1, keepdims=True) + EPS)
        xh = xv * r
        dg_ref[...] += jnp.sum(dh * xh, axis=0, keepdims=True)
        dxh = dh * g_ref[...]
        dx_ref[...] = dres_ref[...] + r * (dxh - xh * jnp.mean(dxh * xh, axis=-1, keepdims=True))

    row = pl.BlockSpec((tm, Dm), lambda i: (i, 0))
    vec = pl.BlockSpec((1, Dm), lambda i: (0, 0))
    a_specs = [pl.BlockSpec((tm, p.shape[1]), lambda i: (i, 0)) if p.ndim == 2 else
               pl.BlockSpec((p.shape[0], tm, p.shape[2]), lambda i: (0, i, 0)) for p in a_parts]
    b_spec = pl.BlockSpec(b.shape, lambda i: (0,) * b.ndim, pipeline_mode=pl.Buffered(1))
    return pl.pallas_call(
        body, name=name, grid=(T // tm,),
        in_specs=a_specs + [b_spec, row, vec, row], out_specs=[row, vec],
        out_shape=[jax.ShapeDtypeStruct((T, Dm), F32), jax.ShapeDtypeStruct((1, Dm), F32)],
        compiler_params=_params("arbitrary"),
    )(*a_parts, b, x, g.reshape(1, Dm), dres)


def _mm_in(h, w_in, name):
    T, Dm = h.shape
    tm = _tile(T, 256)
    widths = (POOL_DIM, QK_DIM, KV_DIM, GATE_DIM)

    def body(h_ref, w_ref, *outs):
        z = _dot(h_ref[...], w_ref[...])
        lo = 0
        for o_ref, wd in zip(outs, widths):
            o_ref[...] = z[:, lo:lo + wd]
            lo += wd

    return pl.pallas_call(
        body, name=name, grid=(T // tm,),
        in_specs=[pl.BlockSpec((tm, Dm), lambda i: (i, 0)), pl.BlockSpec(w_in.shape, lambda i: (0, 0))],
        out_specs=[pl.BlockSpec((tm, wd), lambda i: (i, 0)) for wd in widths],
        out_shape=[jax.ShapeDtypeStruct((T, wd), F32) for wd in widths],
        compiler_params=_params("parallel"),
    )(h, w_in)


def _window_mean_minus_token(ext, u, g, w, pos):
    sl = slice(g * GROUP, (g + 1) * GROUP)
    s = ext[:, sl]
    span = 1
    while span < w:
        s = s + pltpu.roll(s, span, axis=0)
        span *= 2
    cnt = jnp.minimum(pos + 1, w).astype(F32)
    return s[POOL_WMAX:, :] / cnt - u[:, sl]


def _pool_fwd(zu, pool_w, scale, name):
    T = zu.shape[0]
    tm = _tile(T, 512, POOL_WMAX)
    hb = tm // POOL_WMAX

    def body(u_ref, halo_ref, pw_ref, sc_ref, pm_ref):
        i = pl.program_id(0)
        u = u_ref[...]
        halo = jnp.where(i > 0, halo_ref[...], 0.0)
        ext = jnp.concatenate([halo, u], axis=0)
        pos = i * tm + lax.broadcasted_iota(jnp.int32, (tm, 1), 0)
        ys = []
        for g, w in enumerate(POOL_WINDOWS):
            d = _window_mean_minus_token(ext, u, g, w, pos)
            ys.append(_dot(d.astype(CDT), pw_ref[g]))
        pm_ref[...] = (jnp.concatenate(ys, axis=1) * sc_ref[...]).astype(pm_ref.dtype)

    row = pl.BlockSpec((tm, POOL_DIM), lambda i: (i, 0))
    return pl.pallas_call(
        body, name=name, grid=(T // tm,),
        in_specs=[row, pl.BlockSpec((POOL_WMAX, POOL_DIM), lambda i: (jnp.maximum(i * hb - 1, 0), 0)),
                  pl.BlockSpec(pool_w.shape, lambda i: (0, 0, 0)), pl.BlockSpec((1, POOL_DIM), lambda i: (0, 0))],
        out_specs=row, out_shape=jax.ShapeDtypeStruct((T, POOL_DIM), CDT),
        compiler_params=_params("parallel"),
    )(zu, zu, pool_w, scale.reshape(1, POOL_DIM))


def _pool_bwd(zu, dpm, pool_w, scale, name):
    T = zu.shape[0]
    tm = _tile(T, 512, POOL_WMAX)
    hb = tm // POOL_WMAX
    nsteps = T // tm
    ext_rows = tm + POOL_WMAX

    def body(u_ref, halo_ref, dpm_ref, dnext_ref, pw_ref, sc_ref, du_ref, dpw_ref, dsc_ref):
        i = pl.program_id(0)

        @pl.when(i == 0)
        def _():
            dpw_ref[...] = jnp.zeros_like(dpw_ref)
            dsc_ref[...] = jnp.zeros_like(dsc_ref)

        u = u_ref[...]
        halo = jnp.where(i > 0, halo_ref[...], 0.0)
        ext = jnp.concatenate([halo, u], axis=0)
        dpm_t = dpm_ref[...].astype(F32)
        dnext = jnp.where(i < nsteps - 1, dnext_ref[...].astype(F32), 0.0)
        dext = jnp.concatenate([dpm_t, dnext], axis=0)
        sc = sc_ref[...]
        pos = i * tm + lax.broadcasted_iota(jnp.int32, (tm, 1), 0)
        pos_ext = i * tm + lax.broadcasted_iota(jnp.int32, (ext_rows, 1), 0)
        dus, dscs = [], []
        for g, w in enumerate(POOL_WINDOWS):
            sl = slice(g * GROUP, (g + 1) * GROUP)
            dc = _window_mean_minus_token(ext, u, g, w, pos).astype(CDT)
            y = _dot(dc, pw_ref[g])
            dscs.append(jnp.sum(dpm_t[:, sl] * y, axis=0, keepdims=True))
            dy_ext = (dext[:, sl] * sc[:, sl]).astype(CDT)
            dpw_ref[g] += _dot_tn(dc, dy_ext[:tm])
            dd = _dot_nt(dy_ext, pw_ref[g])
            r = dd / jnp.minimum(pos_ext + 1, w).astype(F32)
            span = 1
            while span < w:
                r = r + pltpu.roll(r, ext_rows - span, axis=0)
                span *= 2
            dus.append(r[:tm] - dd[:tm])
        du_ref[...] = jnp.concatenate(dus, axis=1).astype(du_ref.dtype)
        dsc_ref[...] += jnp.concatenate(dscs, axis=1)

    row = pl.BlockSpec((tm, POOL_DIM), lambda i: (i, 0))
    prev = pl.BlockSpec((POOL_WMAX, POOL_DIM), lambda i: (jnp.maximum(i * hb - 1, 0), 0))
    nxt = pl.BlockSpec((POOL_WMAX, POOL_DIM), lambda i: (jnp.minimum((i + 1) * hb, nsteps * hb - 1), 0))
    return pl.pallas_call(
        body, name=name, grid=(nsteps,),
        in_specs=[row, prev, row, nxt, pl.BlockSpec(pool_w.shape, lambda i: (0, 0, 0)),
                  pl.BlockSpec((1, POOL_DIM), lambda i: (0, 0))],
        out_specs=[row, pl.BlockSpec(pool_w.shape, lambda i: (0, 0, 0)), pl.BlockSpec((1, POOL_DIM), lambda i: (0, 0))],
        out_shape=[jax.ShapeDtypeStruct((T, POOL_DIM), CDT), jax.ShapeDtypeStruct(pool_w.shape, F32),
                   jax.ShapeDtypeStruct((1, POOL_DIM), F32)],
        compiler_params=_params("arbitrary"),
    )(zu, zu, dpm, dpm, pool_w, scale.reshape(1, POOL_DIM))


def _rope_tables(T):
    pos = jnp.arange(T, dtype=F32)
    inv_freq = ROPE_THETA ** (-jnp.arange(0, ROT_DIM, 2, dtype=F32) / ROT_DIM)
    ang = pos[:, None] * inv_freq[None, :]
    cos, sin = jnp.cos(ang), jnp.sin(ang)
    rest = HEAD_DIM - ROT_DIM
    cos_h = jnp.concatenate([cos, cos, jnp.ones((T, rest), F32)], axis=1)
    sin_h = jnp.concatenate([-sin, sin, jnp.zeros((T, rest), F32)], axis=1)
    return jnp.tile(cos_h, (1, 2)), jnp.tile(sin_h, (1, 2))


def _lane_masks():
    lane = lax.broadcasted_iota(jnp.int32, (1, LANES), 1)
    in_head = lane % HEAD_DIM
    return lane < HEAD_DIM, in_head < ROT_DIM // 2


def _rope_partner(v, low):
    lane = lax.broadcasted_iota(jnp.int32, (1, LANES), 1)
    swapped = jnp.where(low, pltpu.roll(v, LANES - ROT_DIM // 2, axis=1), pltpu.roll(v, ROT_DIM // 2, axis=1))
    return jnp.where(lane % HEAD_DIM < ROT_DIM, swapped, 0.0)


def _head_mean(v, first):
    lo = jnp.sum(jnp.where(first, v, 0.0), axis=-1, keepdims=True)
    hi = jnp.sum(jnp.where(first, 0.0, v), axis=-1, keepdims=True)
    return jnp.where(first, lo, hi) * (1.0 / HEAD_DIM)


def _qk_fwd(zqk, gqk, cos_t, sin_t, name):
    T = zqk.shape[0]
    tm = _tile(T, 512)

    def body(z_ref, g_ref, c_ref, s_ref, o_ref):
        first, low = _lane_masks()
        cosv, sinv = c_ref[...], s_ref[...]
        for c in range(QK_DIM // LANES):
            sl = slice(c * LANES, (c + 1) * LANES)
            xv = z_ref[:, sl]
            r = lax.rsqrt(_head_mean(xv * xv, first) + EPS)
            xn = xv * r * g_ref[:, sl]
            o_ref[:, sl] = (xn * cosv + _rope_partner(xn, low) * sinv).astype(o_ref.dtype)

    row = pl.BlockSpec((tm, QK_DIM), lambda i: (i, 0))
    tab = pl.BlockSpec((tm, LANES), lambda i: (i, 0))
    return pl.pallas_call(
        body, name=name, grid=(T // tm,),
        in_specs=[row, pl.BlockSpec((1, QK_DIM), lambda i: (0, 0)), tab, tab], out_specs=row,
        out_shape=jax.ShapeDtypeStruct((T, QK_DIM), CDT), compiler_params=_params("parallel"),
    )(zqk, gqk, cos_t, sin_t)


def _qk_bwd(dqk, zqk, gqk, cos_t, sin_t, name):
    T = zqk.shape[0]
    tm = _tile(T, 512)

    def body(d_ref, z_ref, g_ref, c_ref, s_ref, dz_ref, dg_ref):
        @pl.when(pl.program_id(0) == 0)
        def _():
            dg_ref[...] = jnp.zeros_like(dg_ref)

        first, low = _lane_masks()
        cosv, sinv = c_ref[...], s_ref[...]
        dgs = []
        for c in range(QK_DIM // LANES):
            sl = slice(c * LANES, (c + 1) * LANES)
            dout = d_ref[:, sl]
            dxn = dout * cosv + _rope_partner(dout * sinv, low)
            xv = z_ref[:, sl]
            r = lax.rsqrt(_head_mean(xv * xv, first) + EPS)
            xh = xv * r
            dgs.append(jnp.sum(dxn * xh, axis=0, keepdims=True))
            dxh = dxn * g_ref[:, sl]
            dz_ref[:, sl] = (r * (dxh - xh * _head_mean(dxh * xh, first))).astype(dz_ref.dtype)
        dg_ref[...] += jnp.concatenate(dgs, axis=1)

    row = pl.BlockSpec((tm, QK_DIM), lambda i: (i, 0))
    tab = pl.BlockSpec((tm, LANES), lambda i: (i, 0))
    vec = pl.BlockSpec((1, QK_DIM), lambda i: (0, 0))
    return pl.pallas_call(
        body, name=name, grid=(T // tm,),
        in_specs=[row, row, vec, tab, tab], out_specs=[row, vec],
        out_shape=[jax.ShapeDtypeStruct((T, QK_DIM), CDT), jax.ShapeDtypeStruct((1, QK_DIM), F32)],
        compiler_params=_params("arbitrary"),
    )(dqk, zqk, gqk, cos_t, sin_t)


def _dup_half(v, first, kv):
    swapped = pltpu.roll(v, HEAD_DIM, axis=1)
    return jnp.where(first, v, swapped) if kv == 0 else jnp.where(first, swapped, v)


HEADS_PER_KV = 4
HEAD_STACK = 1


def _attn_bias():
    qi = lax.broadcasted_iota(jnp.int32, (HEAD_STACK * BLOCK, 2 * BLOCK), 0) % BLOCK
    ki = lax.broadcasted_iota(jnp.int32, (HEAD_STACK * BLOCK, 2 * BLOCK), 1)
    diff = qi + BLOCK - ki
    band = (diff >= 0) & (diff < BLOCK)
    return jnp.stack([jnp.where(band, 0.0, -jnp.inf), jnp.where(band & (ki >= BLOCK), 0.0, -jnp.inf)]).astype(F32)


def _attn_blocks(T):
    return _tile(T // BLOCK, 4, 1)


def _stack_heads(ref, rows, kv, heads, first):
    parts = []
    for h in heads:
        c = 2 * kv + h // 2
        v = ref[rows, c * LANES:(c + 1) * LANES].astype(CDT)
        zero = jnp.zeros_like(v)
        parts.append(jnp.where(first, v, zero) if h % 2 == 0 else jnp.where(first, zero, v))
    return parts[0] if len(parts) == 1 else jnp.concatenate(parts, axis=0)


def _row_blocks(v, n):
    return [v[b * BLOCK:(b + 1) * BLOCK] for b in range(n)]


def _sink_column(sink_ref, kv, heads):
    cols = [jnp.full((BLOCK, 1), sink_ref[HEADS_PER_KV * kv + h], F32) for h in heads]
    return cols[0] if len(cols) == 1 else jnp.concatenate(cols, axis=0)


def _head_groups():
    return [tuple(range(g, g + HEAD_STACK)) for g in range(0, HEADS_PER_KV, HEAD_STACK)]


def _softmax_with_sink(qst, kdup, sinkcol, bias):
    s = _dot_nt(qst, kdup) * ATTN_SCALE + bias
    m = jnp.maximum(jnp.max(s, axis=-1, keepdims=True), sinkcol)
    pu = jnp.exp(s - m)
    denom = jnp.sum(pu, axis=-1, keepdims=True) + jnp.exp(sinkcol - m)
    return pu * (1.0 / denom), m + jnp.log(denom)


def _attn_fwd(qkn, zv, sinks, name):
    T = qkn.shape[0]
    R = _attn_blocks(T)
    tq = R * BLOCK

    def body(sink_ref, bias_ref, qk_ref, qkp_ref, v_ref, vp_ref, o_ref, lse_ref):
        i = pl.program_id(0)
        first, _ = _lane_masks()
        lane = lax.broadcasted_iota(jnp.int32, (1, LANES), 1)
        kall = jnp.concatenate([qkp_ref[:, ATTN_DIM:], qk_ref[:, ATTN_DIM:]], axis=0)
        vall = jnp.concatenate([vp_ref[...], v_ref[...]], axis=0).astype(CDT)
        for r in range(R):
            bias = bias_ref[jnp.where(i == 0, 1, 0)] if r == 0 else bias_ref[0]
            rows = slice(r * BLOCK, (r + 2) * BLOCK)
            qrows = slice(r * BLOCK, (r + 1) * BLOCK)
            lse_rows = jnp.zeros((BLOCK, LANES), F32)
            for kv in range(2):
                kdup = _dup_half(kall[rows], first, kv)
                vdup = _dup_half(vall[rows], first, kv)
                res = []
                for heads in _head_groups():
                    p, lse = _softmax_with_sink(_stack_heads(qk_ref, qrows, kv, heads, first), kdup,
                                                _sink_column(sink_ref, kv, heads), bias)
                    res += _row_blocks(_dot(p.astype(CDT), vdup), len(heads))
                    for b, col in enumerate(_row_blocks(lse, len(heads))):
                        lse_rows = jnp.where(lane == HEADS_PER_KV * kv + heads[b], col, lse_rows)
                o_ref[qrows, 2 * kv * LANES:(2 * kv + 1) * LANES] = jnp.where(first, res[0], res[1]).astype(o_ref.dtype)
                o_ref[qrows, (2 * kv + 1) * LANES:(2 * kv + 2) * LANES] = jnp.where(first, res[2], res[3]).astype(o_ref.dtype)
            lse_ref[qrows, :] = lse_rows

    bias = _attn_bias()
    prev = lambda i: (jnp.maximum(i * R - 1, 0), 0)
    return pl.pallas_call(
        body, name=name, grid=(T // tq,),
        in_specs=[pl.BlockSpec(memory_space=pltpu.SMEM), pl.BlockSpec(bias.shape, lambda i: (0, 0, 0)),
                  pl.BlockSpec((tq, QK_DIM), lambda i: (i, 0)), pl.BlockSpec((BLOCK, QK_DIM), prev),
                  pl.BlockSpec((tq, KV_DIM), lambda i: (i, 0)), pl.BlockSpec((BLOCK, KV_DIM), prev)],
        out_specs=[pl.BlockSpec((tq, ATTN_DIM), lambda i: (i, 0)), pl.BlockSpec((tq, LANES), lambda i: (i, 0))],
        out_shape=[jax.ShapeDtypeStruct((T, ATTN_DIM), CDT), jax.ShapeDtypeStruct((T, LANES), F32)],
        compiler_params=_params("parallel"),
    )(sinks, bias, qkn, qkn, zv, zv)


def _attn_bwd(qkn, zv, sinks, do, o, lse, name, side=None):
    T = qkn.shape[0]
    R = _attn_blocks(T)
    tq = R * BLOCK

    def body(sink_ref, bias_ref, qk_ref, qkp_ref, v_ref, vp_ref, do_ref, o_ref, lse_ref,
             dq_ref, dkc_ref, dkp_ref, dvc_ref, dvp_ref, ds_ref):
        i = pl.program_id(0)

        @pl.when(i == 0)
        def _():
            ds_ref[...] = jnp.zeros_like(ds_ref)

        first, _ = _lane_masks()
        lane = lax.broadcasted_iota(jnp.int32, (1, LANES), 1)
        kall = jnp.concatenate([qkp_ref[:, ATTN_DIM:], qk_ref[:, ATTN_DIM:]], axis=0)
        vall = jnp.concatenate([vp_ref[...], v_ref[...]], axis=0).astype(CDT)
        for r in range(R):
            bias = bias_ref[jnp.where(i == 0, 1, 0)] if r == 0 else bias_ref[0]
            rows = slice(r * BLOCK, (r + 2) * BLOCK)
            qrows = slice(r * BLOCK, (r + 1) * BLOCK)
            dk_out, dv_out = [], []
            lse_rows = lse_ref[qrows, :]
            for kv in range(2):
                kdup = _dup_half(kall[rows], first, kv)
                vdup = _dup_half(vall[rows], first, kv)
                dq_h = []
                dk_acc = jnp.zeros((2 * BLOCK, LANES), F32)
                dv_acc = jnp.zeros((2 * BLOCK, LANES), F32)
                for heads in _head_groups():
                    qst = _stack_heads(qk_ref, qrows, kv, heads, first)
                    dost = _stack_heads(do_ref, qrows, kv, heads, first)
                    lse_cols, delta_cols = [], []
                    for h in heads:
                        cols = slice((2 * kv + h // 2) * LANES, (2 * kv + h // 2 + 1) * LANES)
                        prod = do_ref[qrows, cols].astype(F32) * o_ref[qrows, cols].astype(F32)
                        own = first if h % 2 == 0 else jnp.logical_not(first)
                        delta_cols.append(jnp.sum(jnp.where(own, prod, 0.0), axis=-1, keepdims=True))
                        lse_cols.append(jnp.sum(jnp.where(lane == HEADS_PER_KV * kv + h, lse_rows, 0.0), axis=-1, keepdims=True))
                    lse_col = lse_cols[0] if len(heads) == 1 else jnp.concatenate(lse_cols, axis=0)
                    delta = delta_cols[0] if len(heads) == 1 else jnp.concatenate(delta_cols, axis=0)
                    p = jnp.exp(_dot_nt(qst, kdup) * ATTN_SCALE + bias - lse_col)
                    dsc = (p * (_dot_nt(dost, vdup) - delta)).astype(CDT)
                    psink = jnp.exp(_sink_column(sink_ref, kv, heads) - lse_col)
                    for b, term in enumerate(_row_blocks(psink * delta, len(heads))):
                        row = HEADS_PER_KV * kv + heads[b]
                        ds_ref[row:row + 1, :] += jnp.sum(term, axis=0, keepdims=True)
                    dq_h += _row_blocks(_dot(dsc, kdup) * ATTN_SCALE, len(heads))
                    dk_acc = dk_acc + _dot_tn(dsc, qst) * ATTN_SCALE
                    dv_acc = dv_acc + _dot_tn(p.astype(CDT), dost)
                dq_ref[qrows, 2 * kv * LANES:(2 * kv + 1) * LANES] = jnp.where(first, dq_h[0], dq_h[1])
                dq_ref[qrows, (2 * kv + 1) * LANES:(2 * kv + 2) * LANES] = jnp.where(first, dq_h[2], dq_h[3])
                dk_out.append(dk_acc + pltpu.roll(dk_acc, HEAD_DIM, axis=1))
                dv_out.append(dv_acc + pltpu.roll(dv_acc, HEAD_DIM, axis=1))
            dk = jnp.where(first, dk_out[0], dk_out[1])
            dv = jnp.where(first, dv_out[0], dv_out[1])
            dkp_ref[qrows, :] = dk[:BLOCK]
            dkc_ref[qrows, :] = dk[BLOCK:]
            dvp_ref[qrows, :] = dv[:BLOCK]
            dvc_ref[qrows, :] = dv[BLOCK:]

    bias = _attn_bias()
    prev = lambda i: (jnp.maximum(i * R - 1, 0), 0)
    kvrow = pl.BlockSpec((tq, KV_DIM), lambda i: (i, 0))
    qrow = pl.BlockSpec((tq, ATTN_DIM), lambda i: (i, 0))
    kv_shape = jax.ShapeDtypeStruct((T, KV_DIM), F32)
    return _pcall(
        body, name, (T // tq,),
        [pl.BlockSpec(memory_space=pltpu.SMEM), pl.BlockSpec(bias.shape, lambda i: (0, 0, 0)),
         pl.BlockSpec((tq, QK_DIM), lambda i: (i, 0)), pl.BlockSpec((BLOCK, QK_DIM), prev),
         kvrow, pl.BlockSpec((BLOCK, KV_DIM), prev), qrow, qrow, kvrow],
        [qrow, kvrow, kvrow, kvrow, kvrow, pl.BlockSpec((N_Q_HEADS, LANES), lambda i: (0, 0))],
        [jax.ShapeDtypeStruct((T, ATTN_DIM), F32), kv_shape, kv_shape, kv_shape, kv_shape,
         jax.ShapeDtypeStruct((N_Q_HEADS, LANES), F32)],
        (sinks, bias, qkn, qkn, zv, zv, do, o, lse), ("arbitrary",), side)


def _merge_fwd(pm, o, w_pb, w_ab, zg, name):
    T = pm.shape[0]
    tm = _tile(T, 512)

    def body(pm_ref, o_ref, wp_ref, wa_ref, zg_ref, a_ref, b_ref, m_ref):
        pmv, ov = pm_ref[...], o_ref[...]
        a = jnp.concatenate([_dot(pmv, wp_ref[j]) for j in range(N_CHIPS)], axis=1)
        b = jnp.concatenate([_dot(ov, wa_ref[j]) for j in range(N_CHIPS)], axis=1)
        gp = _sigmoid(zg_ref[:, :D_MODEL])
        ga = _sigmoid(zg_ref[:, D_MODEL:])
        a_ref[...] = a.astype(a_ref.dtype)
        b_ref[...] = b.astype(b_ref.dtype)
        m_ref[...] = (gp * a + ga * b).astype(m_ref.dtype)

    half = pl.BlockSpec((tm, POOL_DIM), lambda i: (i, 0))
    full = pl.BlockSpec((tm, D_MODEL), lambda i: (i, 0))
    wspec = pl.BlockSpec(w_pb.shape, lambda i: (0, 0, 0))
    out = jax.ShapeDtypeStruct((T, D_MODEL), CDT)
    return pl.pallas_call(
        body, name=name, grid=(T // tm,),
        in_specs=[half, half, wspec, wspec, pl.BlockSpec((tm, GATE_DIM), lambda i: (i, 0))],
        out_specs=[full, full, full], out_shape=[out, out, out], compiler_params=_params("parallel"),
    )(pm, o, w_pb, w_ab, zg)


def _merge_bwd(dxo, w_out, a, b, zg, name):
    T = dxo.shape[0]
    tm = _tile(T, 512)

    def body(dx_ref, w_ref, a_ref, b_ref, zg_ref, da_ref, db_ref, dg_ref):
        dm = _dot_nt(dx_ref[...].astype(CDT), w_ref[...])
        gp = _sigmoid(zg_ref[:, :D_MODEL])
        ga = _sigmoid(zg_ref[:, D_MODEL:])
        da_ref[...] = (dm * gp).astype(da_ref.dtype)
        db_ref[...] = (dm * ga).astype(db_ref.dtype)
        dg_ref[:, :D_MODEL] = (dm * a_ref[...].astype(F32) * (gp * (1.0 - gp))).astype(dg_ref.dtype)
        dg_ref[:, D_MODEL:] = (dm * b_ref[...].astype(F32) * (ga * (1.0 - ga))).astype(dg_ref.dtype)

    full = pl.BlockSpec((tm, D_MODEL), lambda i: (i, 0))
    gate = pl.BlockSpec((tm, GATE_DIM), lambda i: (i, 0))
    out = jax.ShapeDtypeStruct((T, D_MODEL), CDT)
    return pl.pallas_call(
        body, name=name, grid=(T // tm,),
        in_specs=[full, pl.BlockSpec((D_MODEL, D_MODEL), lambda i: (0, 0)), full, full, gate],
        out_specs=[full, full, gate], out_shape=[out, out, jax.ShapeDtypeStruct((T, GATE_DIM), CDT)],
        compiler_params=_params("parallel"),
    )(dxo, w_out, a, b, zg)


def _adamw(w, g, m, v, name):
    Rr, C = w.shape
    tr = _tile(Rr, max(8, (1 << 19) // C // 8 * 8))

    def body(w_ref, g_ref, m_ref, v_ref, d_ref, nm_ref, nv_ref):
        gv = g_ref[...]
        nm = ADAM_B1 * m_ref[...] + (1.0 - ADAM_B1) * gv
        nv = ADAM_B2 * v_ref[...] + (1.0 - ADAM_B2) * (gv * gv)
        m_hat = nm / (1.0 - ADAM_B1 ** ADAM_STEP)
        v_hat = nv / (1.0 - ADAM_B2 ** ADAM_STEP)
        d_ref[...] = -ADAM_LR * (m_hat / (jnp.sqrt(v_hat) + ADAM_EPS) + ADAM_WD * w_ref[...])
        nm_ref[...] = nm
        nv_ref[...] = nv

    blk = pl.BlockSpec((tr, C), lambda i: (i, 0))
    out = jax.ShapeDtypeStruct((Rr, C), F32)
    return pl.pallas_call(
        body, name=name, grid=(Rr // tr,), in_specs=[blk] * 4, out_specs=[blk] * 3, out_shape=[out] * 3,
        compiler_params=_params("parallel"),
    )(w, g, m, v)


def _place():
    return lax.axis_index("x"), lax.axis_index("y"), lax.axis_index("c")


def _other_chip(x, y, d):
    return (1 - x if d & 2 else x), (1 - y if d & 1 else y)


def _rcopy(src, dst, ssem, rsem, dev):
    return pltpu.make_async_remote_copy(src_ref=src, dst_ref=dst, send_sem=ssem, recv_sem=rsem, device_id=dev,
                                        device_id_type=MESH)


def _row_half(rows, c):
    return pl.ds(c * (rows // 2), rows // 2)


def _gather_ici_side(shards, l):
    n = len(shards)

    def issue(ins, outs, ssem, rsem):
        x, y, c = _place()
        cps = []
        for w in range(n):
            half = _row_half(shards[w].shape[1], c)
            for d in (1, 2, 3):
                px, py = _other_chip(x, y, d)
                k = 3 * w + d - 1
                cps.append(_rcopy(ins[w].at[l, half], outs[w].at[2 * x + y, half], ssem.at[k], rsem.at[k], (px, py, c)))
        return cps

    return _Side(shards, [jax.ShapeDtypeStruct((N_CHIPS,) + s.shape[1:], s.dtype) for s in shards], 3 * n, issue)


def _gather_d2d_side(shards, gathered, l):
    n = len(shards)

    def issue(ins, outs, ssem, rsem):
        x, y, c = _place()
        sibling = (x, y, 1 - c)
        cps = []
        for w in range(n):
            half = _row_half(shards[w].shape[1], c)
            for d in (1, 2, 3):
                px, py = _other_chip(x, y, d)
                k = 3 * w + d - 1
                got = outs[w].at[2 * px + py, half]
                cps.append(_rcopy(got, got, ssem.at[k], rsem.at[k], sibling))
            cps.append(_rcopy(ins[n + w].at[l], outs[w].at[2 * x + y], ssem.at[3 * n + w], rsem.at[3 * n + w], sibling))
        return cps

    return _Side(list(gathered) + list(shards), [jax.ShapeDtypeStruct(g.shape, g.dtype) for g in gathered], 4 * n, issue,
                 aliases={w: w for w in range(n)})


def _reduce_sibling_side(gms):
    n = len(gms)

    def issue(ins, outs, ssem, rsem):
        x, y, c = _place()
        return [_rcopy(ins[w].at[:, _row_half(gms[w].shape[1], 1 - c)], outs[w], ssem.at[w], rsem.at[w], (x, y, 1 - c))
                for w in range(n)]

    return _Side(gms, [jax.ShapeDtypeStruct((N_CHIPS, g.shape[1] // 2, g.shape[2]), g.dtype) for g in gms], n, issue)


def _reduce_chip_side(ps):
    n = len(ps)

    def issue(ins, outs, ssem, rsem):
        x, y, c = _place()
        cps = []
        for w in range(n):
            for d in (1, 2, 3):
                px, py = _other_chip(x, y, d)
                k = 3 * w + d - 1
                cps.append(_rcopy(ins[w].at[2 * px + py], outs[w].at[2 * x + y], ssem.at[k], rsem.at[k], (px, py, c)))
        return cps

    return _Side(ps, [jax.ShapeDtypeStruct(p.shape, p.dtype) for p in ps], 3 * n, issue)


def _share_side(accs):
    n = len(accs)

    def issue(ins, outs, ssem, rsem):
        x, y, c = _place()
        cps = []
        for w in range(n):
            mine = outs[w].at[:, _row_half(accs[w].shape[1], c)]
            cps.append(_rcopy(mine, mine, ssem.at[w], rsem.at[w], (x, y, 1 - c)))
        return cps

    return _Side(accs, [jax.ShapeDtypeStruct(a.shape, a.dtype) for a in accs], n, issue, aliases={w: w for w in range(n)})


def _sum_rows(rows, b):
    return _tile(rows, max(16, (1 << 19) // b // 16 * 16), 16)


def _pair_sum(g, recv, place, name):
    _, ah, b = recv.shape
    ta = _sum_rows(ah, b)
    nr = ah // ta

    def body(p_ref, g_ref, r_ref, o_ref):
        o_ref[...] = (g_ref[...].astype(F32) + r_ref[...].astype(F32)).astype(o_ref.dtype)

    blk = (None, ta, b)
    return pl.pallas_call(
        body, name=name,
        grid_spec=pltpu.PrefetchScalarGridSpec(
            num_scalar_prefetch=1, grid=(N_CHIPS, nr),
            in_specs=[pl.BlockSpec(blk, lambda j, r, p: (j, p[0] * nr + r, 0)),
                      pl.BlockSpec(blk, lambda j, r, p: (j, r, 0))],
            out_specs=pl.BlockSpec(blk, lambda j, r, p: (j, r, 0))),
        out_shape=jax.ShapeDtypeStruct(recv.shape, recv.dtype),
        compiler_params=_params("parallel", "parallel"),
    )(place, g, recv)


def _chip_sum(slots, part, place, acc, l, name):
    _, ah, b = slots.shape
    ta = _sum_rows(ah, b)
    nr = ah // ta

    def body(p_ref, s_ref, own_ref, acc_ref, o_ref):
        j = p_ref[1]
        own = own_ref[...].astype(F32)
        term = [jnp.where(j == s_, own, s_ref[s_].astype(F32)) for s_ in range(N_CHIPS)]
        o_ref[...] = ((term[0] + term[1]) + term[2]) + term[3]

    return pl.pallas_call(
        body, name=name,
        grid_spec=pltpu.PrefetchScalarGridSpec(
            num_scalar_prefetch=1, grid=(nr,),
            in_specs=[pl.BlockSpec((N_CHIPS, ta, b), lambda r, p: (0, r, 0)),
                      pl.BlockSpec((None, ta, b), lambda r, p: (p[1], r, 0)), ANY],
            out_specs=pl.BlockSpec((None, ta, b), lambda r, p: (l, p[0] * nr + r, 0))),
        out_shape=jax.ShapeDtypeStruct(acc.shape, F32), input_output_aliases={3: 0},
        compiler_params=_params("parallel"),
    )(place, slots, part, acc)


def _all_reduce_small(v):
    Rr = v.shape[0]

    def body(v_ref, slots_ref, out_ref, ssem, rsem):
        x, y, c = _place()
        me = 4 * x + 2 * y + c
        slots_ref[pl.ds(me, 1)] = v_ref[...][None]
        cps = []
        for d in range(1, N_DEV):
            px, py = _other_chip(x, y, d >> 1)
            pc = 1 - c if d & 1 else c
            cps.append(_rcopy(v_ref, slots_ref.at[me], ssem.at[d - 1], rsem.at[d - 1], (px, py, pc)))
            cps[-1].start()
        for cp in cps:
            cp.wait_recv()
        for cp in cps:
            cp.wait_send()
        acc = slots_ref[0]
        for s in range(1, N_DEV):
            acc = acc + slots_ref[s]
        out_ref[...] = acc

    vm = pl.BlockSpec(memory_space=pltpu.VMEM)
    return pl.pallas_call(
        body, name="all_reduce_small", in_specs=[vm], out_specs=[vm, vm],
        out_shape=[jax.ShapeDtypeStruct((N_DEV, Rr, LANES), F32), jax.ShapeDtypeStruct((Rr, LANES), F32)],
        scratch_shapes=[pltpu.SemaphoreType.DMA((N_DEV - 1,)), pltpu.SemaphoreType.DMA((N_DEV - 1,))],
        compiler_params=pltpu.CompilerParams(vmem_limit_bytes=VMEM_LIMIT_BYTES),
    )(v)[1]


def _ffn_forward(x, ln, wgu, wd, tag, side_of):
    h = _norm_fwd(x, ln, f"{tag}_norm")
    gu, act = _ffn_up(h, wgu, f"{tag}_up", side_of(f"{tag}_up"))
    x_out = _mm_nn(act, wd, f"{tag}_down", F32, res=x, scale=0.5, side=side_of(f"{tag}_down"))
    return x_out, (x, h, gu, act)


def _row_blocks_of(dw):
    return dw.reshape(N_CHIPS, dw.shape[0] // N_CHIPS, dw.shape[1])


def _ffn_backward(dxo, saved, ln, wgu, wd, tag, side_of):
    x, h, gu, act = saved
    dgu = _ffn_down_bwd(dxo, wd, gu, f"{tag}_down_bwd", side_of(f"{tag}_down_bwd"))
    d_wd = _row_blocks_of(_mm_tn(act, dxo, f"{tag}_dwd", scale=0.5))
    d_wgu = _mm_tn(h, dgu, f"{tag}_dwgu", col_blocks=N_CHIPS, tn_target=1408, tm_target=1024)
    dx, d_ln = _mm_nt_norm_bwd([dgu], wgu, x, ln, dxo, f"{tag}_dh_norm_bwd")
    return dx, d_ln, d_wgu, d_wd


def _mixer_forward(x, p, tabs):
    h = _norm_fwd(x, p["ln_mix"], "mix_norm")
    zu, zqk, zv, zg = _mm_in(h, p["w_in"], "mix_in")
    pm = _pool_fwd(zu, p["pool_w"], p["pool_scale"], "pool_fwd")
    qkn = _qk_fwd(zqk, p["gqk"], *tabs, "qk_fwd")
    o, lse = _attn_fwd(qkn, zv, p["sinks"], "attn_fwd")
    a, b, m = _merge_fwd(pm, o, p["w_pool_branch"], p["w_attn_branch"], zg, "merge_fwd")
    x_out = _mm_nn(m, p["w_out"], "mix_out", F32, res=x, scale=1.0)
    return x_out, (x, h, zu, zqk, zv, zg, pm, qkn, o, lse, a, b, m)


def _shift_up(v):
    return jnp.concatenate([v[BLOCK:], jnp.zeros((BLOCK, v.shape[1]), v.dtype)], axis=0)


def _mixer_backward(dxo, saved, p, tabs, side_of):
    x, h, zu, zqk, zv, zg, pm, qkn, o, lse, a, b, m = saved
    g = {}
    d_a, d_b, dgl = _merge_bwd(dxo, p["w_out"], a, b, zg, "merge_bwd")
    g["w_out"] = _row_blocks_of(_mm_tn(m, dxo, "mix_dwout"))
    dpm = _mm_nt_blocks(d_a, p["w_pool_branch"], "pool_branch_dx", CDT)
    g["w_pool_branch"] = _mm_tn(pm, d_a, "pool_branch_dw", col_blocks=N_CHIPS)
    do = _mm_nt_blocks(d_b, p["w_attn_branch"], "attn_branch_dx", CDT)
    g["w_attn_branch"] = _mm_tn(o, d_b, "attn_branch_dw", col_blocks=N_CHIPS)
    du, g["pool_w"], g["pool_scale"] = _pool_bwd(zu, dpm, p["pool_w"], p["pool_scale"], "pool_bwd")
    dq, dkc, dkp, dvc, dvp, dsink = _attn_bwd(qkn, zv, p["sinks"], do, o, lse, "attn_bwd", side_of("attn_bwd"))
    dqk = jnp.concatenate([dq, dkc + _shift_up(dkp)], axis=1)
    dv = dvc + _shift_up(dvp)
    dzqk, dgqk = _qk_bwd(dqk, zqk, p["gqk"], *tabs, "qk_bwd")
    g["q_norm"] = dgqk[0, :ATTN_DIM].reshape(N_Q_HEADS, HEAD_DIM).sum(axis=0)
    g["k_norm"] = dgqk[0, ATTN_DIM:].reshape(KV_DIM // HEAD_DIM, HEAD_DIM).sum(axis=0)
    g["sinks"] = -dsink[:, 0]
    dz = [du, dzqk, dv, dgl]
    g["w_in"] = _blocks_from_full("w_in", _mm_tn_parts(h, dz, "mix_dwin")).astype(WIRE_DT)
    dx, d_ln = _mm_nt_norm_bwd(dz, p["w_in"], x, p["ln_mix"], dxo, "mix_dh_norm_bwd")
    g["ln_mix"] = d_ln[0]
    return dx, g


class _NoComm:
    def __init__(self, layers):
        self.layers, self.grads = layers, [None] * len(layers)

    def layer(self, l):
        return self.layers[l]

    def side(self, phase, l, host):
        return None

    def layer_grads(self, l, g):
        self.grads[l] = g


def _local_step(x, tgt, n_layers, hooks):
    T = x.shape[0]
    tabs = _rope_tables(T)
    saved, params = [], []
    for l in range(n_layers):
        p = hooks.layer(l)
        side_of = functools.partial(hooks.side, "fwd", l)
        x, s1 = _ffn_forward(x, p["ln_ffn1"], p["w_ffn1_gu"], p["w_ffn1_down"], "ffn1", side_of)
        x, s2 = _mixer_forward(x, p, tabs)
        x, s3 = _ffn_forward(x, p["ln_ffn2"], p["w_ffn2_gu"], p["w_ffn2_down"], "ffn2", side_of)
        saved.append((s1, s2, s3))
        params.append(p)
    dx, loss = _loss_head(x, tgt, "loss_head")
    for l in reversed(range(n_layers)):
        p = params[l]
        s1, s2, s3 = saved[l]
        side_of = functools.partial(hooks.side, "bwd", l)
        none = lambda host: None
        dx, d_ln2, d_gu2, d_dn2 = _ffn_backward(dx, s3, p["ln_ffn2"], p["w_ffn2_gu"], p["w_ffn2_down"], "ffn2", side_of)
        dx, g = _mixer_backward(dx, s2, p, tabs, side_of)
        dx, d_ln1, d_gu1, d_dn1 = _ffn_backward(dx, s1, p["ln_ffn1"], p["w_ffn1_gu"], p["w_ffn1_down"], "ffn1", none)
        g.update(ln_ffn1=d_ln1[0], w_ffn1_gu=d_gu1, w_ffn1_down=d_dn1, ln_ffn2=d_ln2[0], w_ffn2_gu=d_gu2, w_ffn2_down=d_dn2)
        hooks.layer_grads(l, g)
    return loss, dx


def _full_from_blocks(name, blocks):
    if name in COL_SHARDED:
        return jnp.transpose(blocks, (1, 0, 2)).reshape(blocks.shape[1], N_CHIPS * blocks.shape[2])
    return blocks.reshape(N_CHIPS * blocks.shape[1], blocks.shape[2])


def _blocks_from_full(name, full):
    K, N = full.shape
    if name in COL_SHARDED:
        return jnp.transpose(full.reshape(K, N_CHIPS, N // N_CHIPS), (1, 0, 2))
    return full.reshape(N_CHIPS, K // N_CHIPS, N)


GATHER_GROUPS = {"ffn1": ("w_ffn1_gu", "w_ffn1_down", "w_in"),
                 "ffn2": ("w_pool_branch", "w_attn_branch", "w_out", "w_ffn2_gu", "w_ffn2_down")}


class _Exchange:
    def __init__(self, shards, small, place, n_layers):
        self.shards, self.small, self.place, self.n_layers = shards, small, place, n_layers
        self.blocks = {}
        for names in GATHER_GROUPS.values():
            got = _run_side(_gather_ici_side([shards[n] for n in names], 0), "gather_ici")
            got = _run_side(_gather_d2d_side([shards[n] for n in names], got, 0), "gather_d2d")
            self.blocks.update({(n, 0): g for n, g in zip(names, got)})
        self.pending = {}
        self.acc = {n: lax.empty(shards[n].shape, F32) for n in BIG}
        self.small_grads = [None] * n_layers
        self.waiting = None

    def layer(self, l):
        self._collect_gathered()
        p = {n: self.blocks.pop((n, l)) for n in BIG}
        p.update({n: _full_from_blocks(n, p[n]) for n in BIG if n not in USED_AS_BLOCKS})
        p.update(self.small(l))
        return p

    def side(self, phase, l, host):
        if phase == "fwd" and l + 1 < self.n_layers:
            group, step = host.split("_")
            sh = [self.shards[n] for n in GATHER_GROUPS[group]]
            if step == "up":
                self.pending[group] = _gather_ici_side(sh, l + 1)
                return self.pending[group]
            done = _gather_d2d_side(sh, self.pending.pop(group).outs, l + 1)
            self.pending[group + "_done"] = (done, l + 1)
            return done
        if phase == "bwd" and self.waiting is not None:
            if host == "ffn2_down_bwd":
                self.waiting["sib"] = _reduce_sibling_side(self.waiting["gm"])
                return self.waiting["sib"]
            if host == "attn_bwd":
                self._pair_sums()
                self.waiting["chip"] = _reduce_chip_side(self.waiting["part"])
                return self.waiting["chip"]
        return None

    def _collect_gathered(self):
        for key in [k for k in self.pending if k.endswith("_done")]:
            done, l = self.pending.pop(key)
            self.blocks.update({(n, l): g for n, g in zip(GATHER_GROUPS[key[:-5]], done.outs)})

    def _pair_sums(self):
        w = self.waiting
        w["part"] = [_pair_sum(g, r, self.place, "grad_pair_sum") for g, r in zip(w["gm"], w["sib"].outs)]

    def _finish(self):
        w, self.waiting = self.waiting, None
        for n, slots, part in zip(BIG, w["chip"].outs, w["part"]):
            self.acc[n] = _chip_sum(slots, part, self.place, self.acc[n], w["l"], "grad_chip_sum")

    def layer_grads(self, l, g):
        if self.waiting is not None:
            self._finish()
        self.small_grads[l] = {n: g[n] for n in SMALL}
        self.waiting = dict(l=l, gm=[g[n] for n in BIG])

    def reduced(self):
        w = self.waiting
        w["sib"] = _reduce_sibling_side(w["gm"])
        _run_side(w["sib"], "grad_sibling_exchange")
        self._pair_sums()
        w["chip"] = _reduce_chip_side(w["part"])
        _run_side(w["chip"], "grad_chip_exchange")
        self._finish()
        return dict(zip(BIG, _run_side(_share_side([self.acc[n] for n in BIG]), "grad_sibling_share")))


def _pack_small(parts):
    rows, spans, lo = [], [], 0
    for v in parts:
        flat = v.reshape(-1)
        nrow = -(-flat.shape[0] // LANES)
        flat = jnp.pad(flat, (0, nrow * LANES - flat.shape[0]))
        rows.append(flat.reshape(nrow, LANES))
        spans.append((lo, nrow))
        lo += nrow
    pad = -lo % 8
    if pad:
        rows.append(jnp.zeros((pad, LANES), F32))
    return jnp.concatenate(rows, axis=0), spans


def _unpack_small(packed, spans, shapes):
    out = []
    for (lo, nrow), shape in zip(spans, shapes):
        size = 1
        for s in shape:
            size *= s
        out.append(packed[lo:lo + nrow].reshape(-1)[:size].reshape(shape))
    return out


def kernel(x, ln_ffn1, w_ffn1_gu, w_ffn1_down, ln_mix, w_in, pool_w, pool_scale, w_pool_branch, q_norm, k_norm, sinks, w_attn_branch, w_out, ln_ffn2, w_ffn2_gu, w_ffn2_down, loss_target, m_ln_ffn1, m_w_ffn1_gu, m_w_ffn1_down, m_ln_mix, m_w_in, m_pool_w, m_pool_scale, m_w_pool_branch, m_q_norm, m_k_norm, m_sinks, m_w_attn_branch, m_w_out, m_ln_ffn2, m_w_ffn2_gu, m_w_ffn2_down, v_ln_ffn1, v_w_ffn1_gu, v_w_ffn1_down, v_ln_mix, v_w_in, v_pool_w, v_pool_scale, v_w_pool_branch, v_q_norm, v_k_norm, v_sinks, v_w_attn_branch, v_w_out, v_ln_ffn2, v_w_ffn2_gu, v_w_ffn2_down):
    w = dict(ln_ffn1=ln_ffn1, w_ffn1_gu=w_ffn1_gu, w_ffn1_down=w_ffn1_down, ln_mix=ln_mix, w_in=w_in, pool_w=pool_w,
             pool_scale=pool_scale, w_pool_branch=w_pool_branch, q_norm=q_norm, k_norm=k_norm, sinks=sinks,
             w_attn_branch=w_attn_branch, w_out=w_out, ln_ffn2=ln_ffn2, w_ffn2_gu=w_ffn2_gu, w_ffn2_down=w_ffn2_down)
    mom = dict(ln_ffn1=m_ln_ffn1, w_ffn1_gu=m_w_ffn1_gu, w_ffn1_down=m_w_ffn1_down, ln_mix=m_ln_mix, w_in=m_w_in,
               pool_w=m_pool_w, pool_scale=m_pool_scale, w_pool_branch=m_w_pool_branch, q_norm=m_q_norm, k_norm=m_k_norm,
               sinks=m_sinks, w_attn_branch=m_w_attn_branch, w_out=m_w_out, ln_ffn2=m_ln_ffn2, w_ffn2_gu=m_w_ffn2_gu,
               w_ffn2_down=m_w_ffn2_down)
    var = dict(ln_ffn1=v_ln_ffn1, w_ffn1_gu=v_w_ffn1_gu, w_ffn1_down=v_w_ffn1_down, ln_mix=v_ln_mix, w_in=v_w_in,
               pool_w=v_pool_w, pool_scale=v_pool_scale, w_pool_branch=v_w_pool_branch, q_norm=v_q_norm, k_norm=v_k_norm,
               sinks=v_sinks, w_attn_branch=v_w_attn_branch, w_out=v_w_out, ln_ffn2=v_ln_ffn2, w_ffn2_gu=v_w_ffn2_gu,
               w_ffn2_down=v_w_ffn2_down)
    L = ln_ffn1.shape[0]

    def small(l):
        return dict(ln_ffn1=ln_ffn1[l], ln_mix=ln_mix[l], ln_ffn2=ln_ffn2[l], pool_w=pool_w[l].astype(CDT),
                    pool_scale=pool_scale[l], sinks=sinks[l],
                    gqk=jnp.concatenate([jnp.tile(q_norm[l], N_Q_HEADS), jnp.tile(k_norm[l], KV_DIM // HEAD_DIM)]).reshape(1, QK_DIM))

    place = jnp.stack([lax.axis_index("c"), 2 * lax.axis_index("x") + lax.axis_index("y")]).astype(jnp.int32)
    hooks = _Exchange({n: w[n].astype(CDT) for n in BIG}, small, place, L)
    loss_part, grad_x = _local_step(x[0], loss_target[0], L, hooks)
    g_big = hooks.reduced()
    grads = hooks.small_grads

    small_parts = [jnp.stack([g[n] for g in grads]) for n in SMALL] + [loss_part]
    packed, spans = _pack_small(small_parts)
    summed = _all_reduce_small(packed)
    *g_small_list, loss_sum = _unpack_small(summed, spans, [w[n].shape for n in SMALL] + [(1, 1)])
    g_small = dict(zip(SMALL, g_small_list))
    loss = loss_sum[0, 0]

    grad_out, delta, new_m, new_v = {}, {}, {}, {}
    for n in BIG:
        shape = w[n].shape
        flat = (shape[0] * shape[1], shape[2])
        grad_out[n] = g_big[n]
        d, nm, nv = _adamw(w[n].reshape(flat), g_big[n].reshape(flat), mom[n].reshape(flat), var[n].reshape(flat), "adamw")
        delta[n], new_m[n], new_v[n] = d.reshape(shape), nm.reshape(shape), nv.reshape(shape)
    pw, _ = _pack_small([w[n] for n in SMALL])
    pg, sp = _pack_small([g_small[n] for n in SMALL])
    pm_, _ = _pack_small([mom[n] for n in SMALL])
    pv, _ = _pack_small([var[n] for n in SMALL])
    d, nm, nv = _adamw(pw, pg, pm_, pv, "adamw_small")
    shapes = [w[n].shape for n in SMALL]
    for n, dv, mv, vv in zip(SMALL, _unpack_small(d, sp, shapes), _unpack_small(nm, sp, shapes), _unpack_small(nv, sp, shapes)):
        grad_out[n], delta[n], new_m[n], new_v[n] = g_small[n], dv, mv, vv

    return (loss, grad_x[None], *[grad_out[n] for n in WEIGHTS], *[delta[n] for n in WEIGHTS],
            *[new_m[n] for n in WEIGHTS], *[new_v[n] for n in WEIGHTS])
```

```python
import functools
import math

import jax
import jax.numpy as jnp
from jax import lax
from jax.experimental import pallas as pl
from jax.experimental.pallas import tpu as pltpu

F32 = jnp.float32
CDT = jnp.bfloat16
WIRE_DT = jnp.bfloat16

D_MODEL = 1024
POOL_WINDOWS = (2, 4, 8, 16)
POOL_WMAX = 16
GROUP = 128
POOL_DIM = 512
HEAD_DIM = 64
N_Q_HEADS = 8
ATTN_DIM = 512
KV_DIM = 128
QK_DIM = ATTN_DIM + KV_DIM
GATE_DIM = 2 * D_MODEL
BLOCK = 128
ROPE_THETA = 500000.0
ROT_DIM = 16
EPS = 1e-6
ATTN_SCALE = HEAD_DIM ** -0.5

ADAM_LR = 0.001
ADAM_B1 = 0.9
ADAM_B2 = 0.999
ADAM_EPS = 1e-08
ADAM_WD = 0.01
ADAM_STEP = 10

N_CHIPS = 4
N_DEV = 8
LANES = 128
VMEM_LIMIT_BYTES = 48 * 1024 * 1024

MESH = pl.DeviceIdType.MESH
ANY = pl.BlockSpec(memory_space=pl.ANY)

BIG = ("w_ffn1_gu", "w_ffn1_down", "w_in", "w_pool_branch", "w_attn_branch", "w_out", "w_ffn2_gu", "w_ffn2_down")
COL_SHARDED = ("w_ffn1_gu", "w_in", "w_pool_branch", "w_attn_branch", "w_ffn2_gu")
USED_AS_BLOCKS = ("w_ffn1_gu", "w_pool_branch", "w_attn_branch", "w_ffn2_gu")
SMALL = ("ln_ffn1", "ln_mix", "pool_w", "pool_scale", "q_norm", "k_norm", "sinks", "ln_ffn2")
WEIGHTS = ("ln_ffn1", "w_ffn1_gu", "w_ffn1_down", "ln_mix", "w_in", "pool_w", "pool_scale", "w_pool_branch",
           "q_norm", "k_norm", "sinks", "w_attn_branch", "w_out", "ln_ffn2", "w_ffn2_gu", "w_ffn2_down")


def _tile(n, target, mult=8):
    if n <= target:
        return n
    for t in range(target - target % mult, 0, -mult):
        if n % t == 0:
            return t
    raise ValueError((n, target, mult))


def _params(*sem):
    return pltpu.CompilerParams(dimension_semantics=sem, vmem_limit_bytes=VMEM_LIMIT_BYTES)


def _sigmoid(v):
    return 0.5 * jnp.tanh(0.5 * v) + 0.5


def _dot(a, b):
    return jnp.dot(a, b, preferred_element_type=F32)


def _dot_nt(a, b):
    return lax.dot_general(a, b, (((1,), (1,)), ((), ())), preferred_element_type=F32)


def _dot_tn(a, b):
    return lax.dot_general(a, b, (((0,), (0,)), ((), ())), preferred_element_type=F32)


class _Side:
    def __init__(self, ins, out_shapes, n_sems, issue, aliases=None):
        self.ins, self.out_shapes, self.n_sems, self.issue = list(ins), list(out_shapes), n_sems, issue
        self.aliases = dict(aliases or {})
        self.outs = None


def _pcall(body, name, grid, in_specs, out_specs, out_shape, args, dims, side=None, scratch=()):
    scratch = list(scratch)
    if side is None:
        return pl.pallas_call(body, name=name, grid=grid, in_specs=in_specs, out_specs=out_specs, out_shape=out_shape,
                              scratch_shapes=scratch, compiler_params=_params(*dims))(*args)
    n_in, n_out, s_in, s_out = len(in_specs), len(out_specs), len(side.ins), len(side.out_shapes)

    def wrapped(*refs):
        main_in, side_in = refs[:n_in], refs[n_in:n_in + s_in]
        main_out = refs[n_in + s_in:n_in + s_in + n_out]
        side_out = refs[n_in + s_in + n_out:n_in + s_in + n_out + s_out]
        rest = refs[n_in + s_in + n_out + s_out:]
        main_scratch, (ssem, rsem) = rest[:len(scratch)], rest[len(scratch):]
        ids = [pl.program_id(ax) for ax in range(len(grid))]
        first = functools.reduce(jnp.logical_and, [i == 0 for i in ids])
        last = functools.reduce(jnp.logical_and, [i == g - 1 for i, g in zip(ids, grid)])

        @pl.when(first)
        def _():
            for cp in side.issue(side_in, side_out, ssem, rsem):
                cp.start()

        body(*main_in, *main_out, *main_scratch)

        @pl.when(last)
        def _():
            cps = side.issue(side_in, side_out, ssem, rsem)
            for cp in cps:
                cp.wait_recv()
            for cp in cps:
                cp.wait_send()

    outs = pl.pallas_call(
        wrapped, name=name, grid=grid, in_specs=list(in_specs) + [ANY] * s_in, out_specs=list(out_specs) + [ANY] * s_out,
        out_shape=list(out_shape) + side.out_shapes,
        input_output_aliases={n_in + i: n_out + o for i, o in side.aliases.items()},
        scratch_shapes=scratch + [pltpu.SemaphoreType.DMA((side.n_sems,))] * 2,
        compiler_params=_params(*["arbitrary"] * len(grid)),
    )(*args, *side.ins)
    side.outs = list(outs[n_out:])
    return list(outs[:n_out])


def _run_side(side, name):
    s_in = len(side.ins)

    def body(*refs):
        ssem, rsem = refs[s_in + len(side.out_shapes):]
        cps = side.issue(refs[:s_in], refs[s_in:s_in + len(side.out_shapes)], ssem, rsem)
        for cp in cps:
            cp.start()
        for cp in cps:
            cp.wait_recv()
        for cp in cps:
            cp.wait_send()

    side.outs = list(pl.pallas_call(
        body, name=name, in_specs=[ANY] * s_in, out_specs=[ANY] * len(side.out_shapes), out_shape=side.out_shapes,
        input_output_aliases=side.aliases, scratch_shapes=[pltpu.SemaphoreType.DMA((side.n_sems,))] * 2,
    )(*side.ins))
    return side.outs


def _norm_fwd(x, g, name):
    T, Dm = x.shape
    tm = _tile(T, 512)

    def body(x_ref, g_ref, h_ref):
        xv = x_ref[...]
        r = lax.rsqrt(jnp.mean(xv * xv, axis=-1, keepdims=True) + EPS)
        h_ref[...] = (xv * r * g_ref[...]).astype(h_ref.dtype)

    row = pl.BlockSpec((tm, Dm), lambda i: (i, 0))
    return pl.pallas_call(
        body, name=name, grid=(T // tm,),
        in_specs=[row, pl.BlockSpec((1, Dm), lambda i: (0, 0))], out_specs=row,
        out_shape=jax.ShapeDtypeStruct((T, Dm), CDT), compiler_params=_params("parallel"),
    )(x, g.reshape(1, Dm))


def _loss_head(y, tgt, name):
    T, Dm = y.shape
    tm = _tile(T, 512)

    def body(y_ref, t_ref, dy_ref, loss_ref):
        @pl.when(pl.program_id(0) == 0)
        def _():
            loss_ref[...] = jnp.zeros_like(loss_ref)

        diff = y_ref[...] - t_ref[...]
        dy_ref[...] = diff * (1.0 / Dm)
        part = jnp.sum(jnp.mean(diff * diff, axis=-1, keepdims=True), axis=0, keepdims=True)
        loss_ref[...] += 0.5 * part

    row = pl.BlockSpec((tm, Dm), lambda i: (i, 0))
    one = pl.BlockSpec((1, 1), lambda i: (0, 0))
    return pl.pallas_call(
        body, name=name, grid=(T // tm,),
        in_specs=[row, row], out_specs=[row, one],
        out_shape=[jax.ShapeDtypeStruct((T, Dm), F32), jax.ShapeDtypeStruct((1, 1), F32)],
        compiler_params=_params("arbitrary"),
    )(y, tgt)


def _mm_nn(a, b, name, out_dtype, res=None, scale=1.0, tm_target=512, side=None):
    M, K = a.shape
    N = b.shape[1]
    tm = _tile(M, tm_target)

    def body(a_ref, b_ref, *rest):
        acc = _dot(a_ref[...].astype(CDT), b_ref[...])
        if res is None:
            (o_ref,) = rest
        else:
            r_ref, o_ref = rest
            acc = r_ref[...] + scale * acc
        o_ref[...] = acc.astype(o_ref.dtype)

    in_specs = [pl.BlockSpec((tm, K), lambda i: (i, 0)), pl.BlockSpec((K, N), lambda i: (0, 0))]
    args = [a, b]
    if res is not None:
        in_specs.append(pl.BlockSpec((tm, N), lambda i: (i, 0)))
        args.append(res)
    return _pcall(body, name, (M // tm,), in_specs, [pl.BlockSpec((tm, N), lambda i: (i, 0))],
                  [jax.ShapeDtypeStruct((M, N), out_dtype)], args, ("parallel",), side)[0]


def _mm_nt_blocks(a, b4, name, out_dtype, tm_target=512):
    M, K = a.shape
    nb, N, Kb = b4.shape
    tm = _tile(M, tm_target)

    def body(a_ref, b_ref, o_ref):
        acc = _dot_nt(a_ref[:, :Kb].astype(CDT), b_ref[0])
        for j in range(1, nb):
            acc = acc + _dot_nt(a_ref[:, j * Kb:(j + 1) * Kb].astype(CDT), b_ref[j])
        o_ref[...] = acc.astype(o_ref.dtype)

    return pl.pallas_call(
        body, name=name, grid=(M // tm,),
        in_specs=[pl.BlockSpec((tm, K), lambda i: (i, 0)), pl.BlockSpec(b4.shape, lambda i: (0, 0, 0))],
        out_specs=pl.BlockSpec((tm, N), lambda i: (i, 0)),
        out_shape=jax.ShapeDtypeStruct((M, N), out_dtype), compiler_params=_params("parallel"),
    )(a, b4)


def _mm_tn(x, dy, name, scale=1.0, col_blocks=1, tn_target=1664, tm_target=1408, tk_target=1024, side=None):
    T, M = x.shape
    split = dy.ndim == 3
    Nh = dy.shape[-1]
    N = 2 * Nh if split else Nh
    nb = N // col_blocks
    whole = col_blocks > 1 and not split and N <= tn_target
    tm = _tile(M, tm_target, LANES)
    tn = N if whole else _tile(math.gcd(Nh, nb), tn_target, LANES)
    tk = _tile(T, tk_target)
    nk = T // tk
    njh, njb = Nh // tn, max(nb // tn, 1)

    def body(x_ref, dy_ref, o_ref, acc_ref):
        k = pl.program_id(2)

        @pl.when(k == 0)
        def _():
            acc_ref[...] = jnp.zeros_like(acc_ref)

        acc_ref[...] += _dot_tn(x_ref[...].astype(CDT), dy_ref[...].astype(CDT))

        @pl.when(k == nk - 1)
        def _():
            res = (acc_ref[...] if scale == 1.0 else scale * acc_ref[...]).astype(o_ref.dtype)
            if whole:
                for b in range(col_blocks):
                    o_ref[b] = res[:, b * nb:(b + 1) * nb]
            else:
                o_ref[...] = res

    if split:
        dy_spec = pl.BlockSpec((None, tk, tn), lambda i, j, k: (j // njh, k, j % njh))
    else:
        dy_spec = pl.BlockSpec((tk, tn), lambda i, j, k: (k, j))
    if col_blocks == 1:
        out_spec, out_dims = pl.BlockSpec((tm, tn), lambda i, j, k: (i, j)), (M, N)
    elif whole:
        out_spec, out_dims = pl.BlockSpec((col_blocks, tm, nb), lambda i, j, k: (0, i, 0)), (col_blocks, M, nb)
    else:
        out_spec, out_dims = pl.BlockSpec((None, tm, tn), lambda i, j, k: (j // njb, i, j % njb)), (col_blocks, M, nb)
    return _pcall(body, name, (M // tm, N // tn, nk), [pl.BlockSpec((tk, tm), lambda i, j, k: (k, i)), dy_spec],
                  [out_spec], [jax.ShapeDtypeStruct(out_dims, WIRE_DT)], (x, dy), ("parallel", "parallel", "arbitrary"),
                  side, [pltpu.VMEM((tm, tn), F32)])[0]


def _mm_tn_parts(x, parts, name):
    T, M = x.shape
    widths = [p.shape[1] for p in parts]
    N = sum(widths)
    tk = _tile(T, 512)

    def body(x_ref, *refs):
        o_ref = refs[-1]

        @pl.when(pl.program_id(0) == 0)
        def _():
            o_ref[...] = jnp.zeros_like(o_ref)

        xv = x_ref[...].astype(CDT)
        lo = 0
        for p_ref, wd in zip(refs[:-1], widths):
            o_ref[:, lo:lo + wd] += _dot_tn(xv, p_ref[...].astype(CDT))
            lo += wd

    return pl.pallas_call(
        body, name=name, grid=(T // tk,),
        in_specs=[pl.BlockSpec((tk, M), lambda k: (k, 0))] + [pl.BlockSpec((tk, wd), lambda k: (k, 0)) for wd in widths],
        out_specs=pl.BlockSpec((M, N), lambda k: (0, 0)),
        out_shape=jax.ShapeDtypeStruct((M, N), F32), compiler_params=_params("arbitrary"),
    )(x, *parts)


def _ffn_up(h, wgu4, name, side=None):
    T, Dm = h.shape
    tn = wgu4.shape[2]
    nj = 2
    Fd = nj * tn
    tm = _tile(T, 512)

    def body(h_ref, wg_ref, wu_ref, gu_ref, a_ref):
        hv = h_ref[...]
        g = _dot(hv, wg_ref[...])
        u = _dot(hv, wu_ref[...])
        gu_ref[0] = g.astype(gu_ref.dtype)
        gu_ref[1] = u.astype(gu_ref.dtype)
        a_ref[...] = (g * _sigmoid(g) * u).astype(a_ref.dtype)

    return _pcall(
        body, name, (nj, T // tm),
        [pl.BlockSpec((tm, Dm), lambda j, i: (i, 0)), pl.BlockSpec((None, Dm, tn), lambda j, i: (j, 0, 0)),
         pl.BlockSpec((None, Dm, tn), lambda j, i: (j + nj, 0, 0))],
        [pl.BlockSpec((2, tm, tn), lambda j, i: (0, i, j)), pl.BlockSpec((tm, tn), lambda j, i: (i, j))],
        [jax.ShapeDtypeStruct((2, T, Fd), CDT), jax.ShapeDtypeStruct((T, Fd), CDT)],
        (h, wgu4, wgu4), ("parallel", "parallel"), side)


def _ffn_down_bwd(dxo, wd, gu, name, side=None):
    T, Dm = dxo.shape
    Fd = wd.shape[0]
    tm = _tile(T, 512)
    tn = _tile(Fd, 1408, LANES)

    def body(dx_ref, wd_ref, gu_ref, dgu_ref):
        da = 0.5 * _dot_nt(dx_ref[...].astype(CDT), wd_ref[...])
        g = gu_ref[0].astype(F32)
        u = gu_ref[1].astype(F32)
        sg = _sigmoid(g)
        dgu_ref[0] = (da * u * (sg * (1.0 + g * (1.0 - sg)))).astype(dgu_ref.dtype)
        dgu_ref[1] = (da * (g * sg)).astype(dgu_ref.dtype)

    gu_spec = pl.BlockSpec((2, tm, tn), lambda j, i: (0, i, j))
    return _pcall(
        body, name, (Fd // tn, T // tm),
        [pl.BlockSpec((tm, Dm), lambda j, i: (i, 0)), pl.BlockSpec((tn, Dm), lambda j, i: (j, 0)), gu_spec],
        [gu_spec], [jax.ShapeDtypeStruct((2, T, Fd), CDT)],
        (dxo, wd, gu), ("parallel", "parallel"), side)[0]


def _mm_nt_norm_bwd(a_parts, b, x, g, dres, name, side=None):
    T, Dm = x.shape
    tm = _tile(T, 256)

    def b_cols(b_ref, lo, wd):
        if b.ndim == 2:
            return [(0, wd, b_ref[:, lo:lo + wd])]
        kb = b.shape[2]
        return [(j * kb - lo, kb, b_ref[j]) for j in range(lo // kb, (lo + wd) // kb)]

    def body(*refs):
        a_refs, (b_ref, x_ref, g_ref, dres_ref, dx_ref, dg_ref) = refs[:len(a_parts)], refs[len(a_parts):]

        @pl.when(pl.program_id(0) == 0)
        def _():
            dg_ref[...] = jnp.zeros_like(dg_ref)

        dh, lo = None, 0
        for a_ref, part in zip(a_refs, a_parts):
            slabs = [a_ref] if part.ndim == 2 else [a_ref.at[s_] for s_ in range(part.shape[0])]
            for slab in slabs:
                for off, wd, bv in b_cols(b_ref, lo, part.shape[-1]):
                    term = _dot_nt(slab[:, off:off + wd].astype(CDT), bv)
                    dh = term if dh is None else dh + term
                lo += part.shape[-1]
        xv = x_ref[...]
        r = lax.rsqrt(jnp.mean(xv * xv, axis=-1, keepdims=True) + EPS)
        xh = xv * r
        dg_ref[...] += jnp.sum(dh * xh, axis=0, keepdims=True)
        dxh = dh * g_ref[...]
        dx_ref[...] = dres_ref[...] + r * (dxh - xh * jnp.mean(dxh * xh, axis=-1, keepdims=True))

    row = pl.BlockSpec((tm, Dm), lambda i: (i, 0))
    vec = pl.BlockSpec((1, Dm), lambda i: (0, 0))
    a_specs = [pl.BlockSpec((tm, p.shape[1]), lambda i: (i, 0)) if p.ndim == 2 else
               pl.BlockSpec((p.shape[0], tm, p.shape[2]), lambda i: (0, i, 0)) for p in a_parts]
    b_spec = pl.BlockSpec(b.shape, lambda i: (0,) * b.ndim, pipeline_mode=pl.Buffered(1))
    return _pcall(body, name, (T // tm,), a_specs + [b_spec, row, vec, row], [row, vec],
                  [jax.ShapeDtypeStruct((T, Dm), F32), jax.ShapeDtypeStruct((1, Dm), F32)],
                  (*a_parts, b, x, g.reshape(1, Dm), dres), ("arbitrary",), side)


def _mm_in(h, w_in, name, side=None):
    T, Dm = h.shape
    tm = _tile(T, 256)
    widths = (POOL_DIM, QK_DIM, KV_DIM, GATE_DIM)

    def body(h_ref, w_ref, *outs):
        z = _dot(h_ref[...], w_ref[...])
        lo = 0
        for o_ref, wd in zip(outs, widths):
            o_ref[...] = z[:, lo:lo + wd]
            lo += wd

    return _pcall(body, name, (T // tm,),
                  [pl.BlockSpec((tm, Dm), lambda i: (i, 0)), pl.BlockSpec(w_in.shape, lambda i: (0, 0))],
                  [pl.BlockSpec((tm, wd), lambda i: (i, 0)) for wd in widths],
                  [jax.ShapeDtypeStruct((T, wd), F32) for wd in widths], (h, w_in), ("parallel",), side)


def _window_mean_minus_token(ext, u, g, w, pos):
    sl = slice(g * GROUP, (g + 1) * GROUP)
    s = ext[:, sl]
    span = 1
    while span < w:
        s = s + pltpu.roll(s, span, axis=0)
        span *= 2
    cnt = jnp.minimum(pos + 1, w).astype(F32)
    return s[POOL_WMAX:, :] / cnt - u[:, sl]


def _pool_fwd(zu, pool_w, scale, name):
    T = zu.shape[0]
    tm = _tile(T, 512, POOL_WMAX)
    hb = tm // POOL_WMAX

    def body(u_ref, halo_ref, pw_ref, sc_ref, pm_ref):
        i = pl.program_id(0)
        u = u_ref[...]
        halo = jnp.where(i > 0, halo_ref[...], 0.0)
        ext = jnp.concatenate([halo, u], axis=0)
        pos = i * tm + lax.broadcasted_iota(jnp.int32, (tm, 1), 0)
        ys = []
        for g, w in enumerate(POOL_WINDOWS):
            d = _window_mean_minus_token(ext, u, g, w, pos)
            ys.append(_dot(d.astype(CDT), pw_ref[g]))
        pm_ref[...] = (jnp.concatenate(ys, axis=1) * sc_ref[...]).astype(pm_ref.dtype)

    row = pl.BlockSpec((tm, POOL_DIM), lambda i: (i, 0))
    return pl.pallas_call(
        body, name=name, grid=(T // tm,),
        in_specs=[row, pl.BlockSpec((POOL_WMAX, POOL_DIM), lambda i: (jnp.maximum(i * hb - 1, 0), 0)),
                  pl.BlockSpec(pool_w.shape, lambda i: (0, 0, 0)), pl.BlockSpec((1, POOL_DIM), lambda i: (0, 0))],
        out_specs=row, out_shape=jax.ShapeDtypeStruct((T, POOL_DIM), CDT),
        compiler_params=_params("parallel"),
    )(zu, zu, pool_w, scale.reshape(1, POOL_DIM))


def _pool_bwd(zu, dpm, pool_w, scale, name):
    T = zu.shape[0]
    tm = _tile(T, 512, POOL_WMAX)
    hb = tm // POOL_WMAX
    nsteps = T // tm
    ext_rows = tm + POOL_WMAX

    def body(u_ref, halo_ref, dpm_ref, dnext_ref, pw_ref, sc_ref, du_ref, dpw_ref, dsc_ref):
        i = pl.program_id(0)

        @pl.when(i == 0)
        def _():
            dpw_ref[...] = jnp.zeros_like(dpw_ref)
            dsc_ref[...] = jnp.zeros_like(dsc_ref)

        u = u_ref[...]
        halo = jnp.where(i > 0, halo_ref[...], 0.0)
        ext = jnp.concatenate([halo, u], axis=0)
        dpm_t = dpm_ref[...].astype(F32)
        dnext = jnp.where(i < nsteps - 1, dnext_ref[...].astype(F32), 0.0)
        dext = jnp.concatenate([dpm_t, dnext], axis=0)
        sc = sc_ref[...]
        pos = i * tm + lax.broadcasted_iota(jnp.int32, (tm, 1), 0)
        pos_ext = i * tm + lax.broadcasted_iota(jnp.int32, (ext_rows, 1), 0)
        dus, dscs = [], []
        for g, w in enumerate(POOL_WINDOWS):
            sl = slice(g * GROUP, (g + 1) * GROUP)
            dc = _window_mean_minus_token(ext, u, g, w, pos).astype(CDT)
            y = _dot(dc, pw_ref[g])
            dscs.append(jnp.sum(dpm_t[:, sl] * y, axis=0, keepdims=True))
            dy_ext = (dext[:, sl] * sc[:, sl]).astype(CDT)
            dpw_ref[g] += _dot_tn(dc, dy_ext[:tm])
            dd = _dot_nt(dy_ext, pw_ref[g])
            r = dd / jnp.minimum(pos_ext + 1, w).astype(F32)
            span = 1
            while span < w:
                r = r + pltpu.roll(r, ext_rows - span, axis=0)
                span *= 2
            dus.append(r[:tm] - dd[:tm])
        du_ref[...] = jnp.concatenate(dus, axis=1).astype(du_ref.dtype)
        dsc_ref[...] += jnp.concatenate(dscs, axis=1)

    row = pl.BlockSpec((tm, POOL_DIM), lambda i: (i, 0))
    prev = pl.BlockSpec((POOL_WMAX, POOL_DIM), lambda i: (jnp.maximum(i * hb - 1, 0), 0))
    nxt = pl.BlockSpec((POOL_WMAX, POOL_DIM), lambda i: (jnp.minimum((i + 1) * hb, nsteps * hb - 1), 0))
    return pl.pallas_call(
        body, name=name, grid=(nsteps,),
        in_specs=[row, prev, row, nxt, pl.BlockSpec(pool_w.shape, lambda i: (0, 0, 0)),
                  pl.BlockSpec((1, POOL_DIM), lambda i: (0, 0))],
        out_specs=[row, pl.BlockSpec(pool_w.shape, lambda i: (0, 0, 0)), pl.BlockSpec((1, POOL_DIM), lambda i: (0, 0))],
        out_shape=[jax.ShapeDtypeStruct((T, POOL_DIM), CDT), jax.ShapeDtypeStruct(pool_w.shape, F32),
                   jax.ShapeDtypeStruct((1, POOL_DIM), F32)],
        compiler_params=_params("arbitrary"),
    )(zu, zu, dpm, dpm, pool_w, scale.reshape(1, POOL_DIM))


def _rope_tables(T):
    pos = jnp.arange(T, dtype=F32)
    inv_freq = ROPE_THETA ** (-jnp.arange(0, ROT_DIM, 2, dtype=F32) / ROT_DIM)
    ang = pos[:, None] * inv_freq[None, :]
    cos, sin = jnp.cos(ang), jnp.sin(ang)
    rest = HEAD_DIM - ROT_DIM
    cos_h = jnp.concatenate([cos, cos, jnp.ones((T, rest), F32)], axis=1)
    sin_h = jnp.concatenate([-sin, sin, jnp.zeros((T, rest), F32)], axis=1)
    return jnp.tile(cos_h, (1, 2)), jnp.tile(sin_h, (1, 2))


def _lane_masks():
    lane = lax.broadcasted_iota(jnp.int32, (1, LANES), 1)
    in_head = lane % HEAD_DIM
    return lane < HEAD_DIM, in_head < ROT_DIM // 2


def _rope_partner(v, low):
    lane = lax.broadcasted_iota(jnp.int32, (1, LANES), 1)
    swapped = jnp.where(low, pltpu.roll(v, LANES - ROT_DIM // 2, axis=1), pltpu.roll(v, ROT_DIM // 2, axis=1))
    return jnp.where(lane % HEAD_DIM < ROT_DIM, swapped, 0.0)


def _head_mean(v, first):
    lo = jnp.sum(jnp.where(first, v, 0.0), axis=-1, keepdims=True)
    hi = jnp.sum(jnp.where(first, 0.0, v), axis=-1, keepdims=True)
    return jnp.where(first, lo, hi) * (1.0 / HEAD_DIM)


def _qk_fwd(zqk, gqk, cos_t, sin_t, name):
    T = zqk.shape[0]
    tm = _tile(T, 512)

    def body(z_ref, g_ref, c_ref, s_ref, o_ref):
        first, low = _lane_masks()
        cosv, sinv = c_ref[...], s_ref[...]
        for c in range(QK_DIM // LANES):
            sl = slice(c * LANES, (c + 1) * LANES)
            xv = z_ref[:, sl]
            r = lax.rsqrt(_head_mean(xv * xv, first) + EPS)
            xn = xv * r * g_ref[:, sl]
            o_ref[:, sl] = (xn * cosv + _rope_partner(xn, low) * sinv).astype(o_ref.dtype)

    row = pl.BlockSpec((tm, QK_DIM), lambda i: (i, 0))
    tab = pl.BlockSpec((tm, LANES), lambda i: (i, 0))
    return pl.pallas_call(
        body, name=name, grid=(T // tm,),
        in_specs=[row, pl.BlockSpec((1, QK_DIM), lambda i: (0, 0)), tab, tab], out_specs=row,
        out_shape=jax.ShapeDtypeStruct((T, QK_DIM), CDT), compiler_params=_params("parallel"),
    )(zqk, gqk, cos_t, sin_t)


def _qk_bwd(dqk, zqk, gqk, cos_t, sin_t, name):
    T = zqk.shape[0]
    tm = _tile(T, 512)

    def body(d_ref, z_ref, g_ref, c_ref, s_ref, dz_ref, dg_ref):
        @pl.when(pl.program_id(0) == 0)
        def _():
            dg_ref[...] = jnp.zeros_like(dg_ref)

        first, low = _lane_masks()
        cosv, sinv = c_ref[...], s_ref[...]
        dgs = []
        for c in range(QK_DIM // LANES):
            sl = slice(c * LANES, (c + 1) * LANES)
            dout = d_ref[:, sl]
            dxn = dout * cosv + _rope_partner(dout * sinv, low)
            xv = z_ref[:, sl]
            r = lax.rsqrt(_head_mean(xv * xv, first) + EPS)
            xh = xv * r
            dgs.append(jnp.sum(dxn * xh, axis=0, keepdims=True))
            dxh = dxn * g_ref[:, sl]
            dz_ref[:, sl] = (r * (dxh - xh * _head_mean(dxh * xh, first))).astype(dz_ref.dtype)
        dg_ref[...] += jnp.concatenate(dgs, axis=1)

    row = pl.BlockSpec((tm, QK_DIM), lambda i: (i, 0))
    tab = pl.BlockSpec((tm, LANES), lambda i: (i, 0))
    vec = pl.BlockSpec((1, QK_DIM), lambda i: (0, 0))
    return pl.pallas_call(
        body, name=name, grid=(T // tm,),
        in_specs=[row, row, vec, tab, tab], out_specs=[row, vec],
        out_shape=[jax.ShapeDtypeStruct((T, QK_DIM), CDT), jax.ShapeDtypeStruct((1, QK_DIM), F32)],
        compiler_params=_params("arbitrary"),
    )(dqk, zqk, gqk, cos_t, sin_t)


def _dup_half(v, first, kv):
    swapped = pltpu.roll(v, HEAD_DIM, axis=1)
    return jnp.where(first, v, swapped) if kv == 0 else jnp.where(first, swapped, v)


HEADS_PER_KV = 4
HEAD_STACK = 1


def _attn_bias():
    qi = lax.broadcasted_iota(jnp.int32, (HEAD_STACK * BLOCK, 2 * BLOCK), 0) % BLOCK
    ki = lax.broadcasted_iota(jnp.int32, (HEAD_STACK * BLOCK, 2 * BLOCK), 1)
    diff = qi + BLOCK - ki
    band = (diff >= 0) & (diff < BLOCK)
    return jnp.stack([jnp.where(band, 0.0, -jnp.inf), jnp.where(band & (ki >= BLOCK), 0.0, -jnp.inf)]).astype(F32)


def _attn_blocks(T):
    return _tile(T // BLOCK, 4, 1)


def _stack_heads(ref, rows, kv, heads, first):
    parts = []
    for h in heads:
        c = 2 * kv + h // 2
        v = ref[rows, c * LANES:(c + 1) * LANES].astype(CDT)
        zero = jnp.zeros_like(v)
        parts.append(jnp.where(first, v, zero) if h % 2 == 0 else jnp.where(first, zero, v))
    return parts[0] if len(parts) == 1 else jnp.concatenate(parts, axis=0)


def _row_blocks(v, n):
    return [v[b * BLOCK:(b + 1) * BLOCK] for b in range(n)]


def _sink_column(sink_ref, kv, heads):
    cols = [jnp.full((BLOCK, 1), sink_ref[HEADS_PER_KV * kv + h], F32) for h in heads]
    return cols[0] if len(cols) == 1 else jnp.concatenate(cols, axis=0)


def _head_groups():
    return [tuple(range(g, g + HEAD_STACK)) for g in range(0, HEADS_PER_KV, HEAD_STACK)]


def _softmax_with_sink(qst, kdup, sinkcol, bias):
    s = _dot_nt(qst, kdup) * ATTN_SCALE + bias
    m = jnp.maximum(jnp.max(s, axis=-1, keepdims=True), sinkcol)
    pu = jnp.exp(s - m)
    denom = jnp.sum(pu, axis=-1, keepdims=True) + jnp.exp(sinkcol - m)
    return pu * (1.0 / denom), m + jnp.log(denom)


def _attn_fwd(qkn, zv, sinks, name, side=None):
    T = qkn.shape[0]
    R = _attn_blocks(T)
    tq = R * BLOCK

    def body(sink_ref, bias_ref, qk_ref, qkp_ref, v_ref, vp_ref, o_ref, lse_ref):
        i = pl.program_id(0)
        first, _ = _lane_masks()
        lane = lax.broadcasted_iota(jnp.int32, (1, LANES), 1)
        kall = jnp.concatenate([qkp_ref[:, ATTN_DIM:], qk_ref[:, ATTN_DIM:]], axis=0)
        vall = jnp.concatenate([vp_ref[...], v_ref[...]], axis=0).astype(CDT)
        for r in range(R):
            bias = bias_ref[jnp.where(i == 0, 1, 0)] if r == 0 else bias_ref[0]
            rows = slice(r * BLOCK, (r + 2) * BLOCK)
            qrows = slice(r * BLOCK, (r + 1) * BLOCK)
            lse_rows = jnp.zeros((BLOCK, LANES), F32)
            for kv in range(2):
                kdup = _dup_half(kall[rows], first, kv)
                vdup = _dup_half(vall[rows], first, kv)
                res = []
                for heads in _head_groups():
                    p, lse = _softmax_with_sink(_stack_heads(qk_ref, qrows, kv, heads, first), kdup,
                                                _sink_column(sink_ref, kv, heads), bias)
                    res += _row_blocks(_dot(p.astype(CDT), vdup), len(heads))
                    for b, col in enumerate(_row_blocks(lse, len(heads))):
                        lse_rows = jnp.where(lane == HEADS_PER_KV * kv + heads[b], col, lse_rows)
                o_ref[qrows, 2 * kv * LANES:(2 * kv + 1) * LANES] = jnp.where(first, res[0], res[1]).astype(o_ref.dtype)
                o_ref[qrows, (2 * kv + 1) * LANES:(2 * kv + 2) * LANES] = jnp.where(first, res[2], res[3]).astype(o_ref.dtype)
            lse_ref[qrows, :] = lse_rows

    bias = _attn_bias()
    prev = lambda i: (jnp.maximum(i * R - 1, 0), 0)
    return _pcall(
        body, name, (T // tq,),
        [pl.BlockSpec(memory_space=pltpu.SMEM), pl.BlockSpec(bias.shape, lambda i: (0, 0, 0)),
         pl.BlockSpec((tq, QK_DIM), lambda i: (i, 0)), pl.BlockSpec((BLOCK, QK_DIM), prev),
         pl.BlockSpec((tq, KV_DIM), lambda i: (i, 0)), pl.BlockSpec((BLOCK, KV_DIM), prev)],
        [pl.BlockSpec((tq, ATTN_DIM), lambda i: (i, 0)), pl.BlockSpec((tq, LANES), lambda i: (i, 0))],
        [jax.ShapeDtypeStruct((T, ATTN_DIM), CDT), jax.ShapeDtypeStruct((T, LANES), F32)],
        (sinks, bias, qkn, qkn, zv, zv), ("parallel",), side)


def _attn_bwd(qkn, zv, sinks, do, o, lse, name, side=None):
    T = qkn.shape[0]
    R = _attn_blocks(T)
    tq = R * BLOCK

    def body(sink_ref, bias_ref, qk_ref, qkp_ref, v_ref, vp_ref, do_ref, o_ref, lse_ref,
             dq_ref, dkc_ref, dkp_ref, dvc_ref, dvp_ref, ds_ref):
        i = pl.program_id(0)

        @pl.when(i == 0)
        def _():
            ds_ref[...] = jnp.zeros_like(ds_ref)

        first, _ = _lane_masks()
        lane = lax.broadcasted_iota(jnp.int32, (1, LANES), 1)
        kall = jnp.concatenate([qkp_ref[:, ATTN_DIM:], qk_ref[:, ATTN_DIM:]], axis=0)
        vall = jnp.concatenate([vp_ref[...], v_ref[...]], axis=0).astype(CDT)
        for r in range(R):
            bias = bias_ref[jnp.where(i == 0, 1, 0)] if r == 0 else bias_ref[0]
            rows = slice(r * BLOCK, (r + 2) * BLOCK)
            qrows = slice(r * BLOCK, (r + 1) * BLOCK)
            dk_out, dv_out = [], []
            lse_rows = lse_ref[qrows, :]
            for kv in range(2):
                kdup = _dup_half(kall[rows], first, kv)
                vdup = _dup_half(vall[rows], first, kv)
                dq_h = []
                dk_acc = jnp.zeros((2 * BLOCK, LANES), F32)
                dv_acc = jnp.zeros((2 * BLOCK, LANES), F32)
                for heads in _head_groups():
                    qst = _stack_heads(qk_ref, qrows, kv, heads, first)
                    dost = _stack_heads(do_ref, qrows, kv, heads, first)
                    lse_cols, delta_cols = [], []
                    for h in heads:
                        cols = slice((2 * kv + h // 2) * LANES, (2 * kv + h // 2 + 1) * LANES)
                        prod = do_ref[qrows, cols].astype(F32) * o_ref[qrows, cols].astype(F32)
                        own = first if h % 2 == 0 else jnp.logical_not(first)
                        delta_cols.append(jnp.sum(jnp.where(own, prod, 0.0), axis=-1, keepdims=True))
                        lse_cols.append(jnp.sum(jnp.where(lane == HEADS_PER_KV * kv + h, lse_rows, 0.0), axis=-1, keepdims=True))
                    lse_col = lse_cols[0] if len(heads) == 1 else jnp.concatenate(lse_cols, axis=0)
                    delta = delta_cols[0] if len(heads) == 1 else jnp.concatenate(delta_cols, axis=0)
                    p = jnp.exp(_dot_nt(qst, kdup) * ATTN_SCALE + bias - lse_col)
                    dsc = (p * (_dot_nt(dost, vdup) - delta)).astype(CDT)
                    psink = jnp.exp(_sink_column(sink_ref, kv, heads) - lse_col)
                    for b, term in enumerate(_row_blocks(psink * delta, len(heads))):
                        row = HEADS_PER_KV * kv + heads[b]
                        ds_ref[row:row + 1, :] += jnp.sum(term, axis=0, keepdims=True)
                    dq_h += _row_blocks(_dot(dsc, kdup) * ATTN_SCALE, len(heads))
                    dk_acc = dk_acc + _dot_tn(dsc, qst) * ATTN_SCALE
                    dv_acc = dv_acc + _dot_tn(p.astype(CDT), dost)
                dq_ref[qrows, 2 * kv * LANES:(2 * kv + 1) * LANES] = jnp.where(first, dq_h[0], dq_h[1])
                dq_ref[qrows, (2 * kv + 1) * LANES:(2 * kv + 2) * LANES] = jnp.where(first, dq_h[2], dq_h[3])
                dk_out.append(dk_acc + pltpu.roll(dk_acc, HEAD_DIM, axis=1))
                dv_out.append(dv_acc + pltpu.roll(dv_acc, HEAD_DIM, axis=1))
            dk = jnp.where(first, dk_out[0], dk_out[1])
            dv = jnp.where(first, dv_out[0], dv_out[1])
            dkp_ref[qrows, :] = dk[:BLOCK]
            dkc_ref[qrows, :] = dk[BLOCK:]
            dvp_ref[qrows, :] = dv[:BLOCK]
            dvc_ref[qrows, :] = dv[BLOCK:]

    bias = _attn_bias()
    prev = lambda i: (jnp.maximum(i * R - 1, 0), 0)
    kvrow = pl.BlockSpec((tq, KV_DIM), lambda i: (i, 0))
    qrow = pl.BlockSpec((tq, ATTN_DIM), lambda i: (i, 0))
    kv_shape = jax.ShapeDtypeStruct((T, KV_DIM), F32)
    return _pcall(
        body, name, (T // tq,),
        [pl.BlockSpec(memory_space=pltpu.SMEM), pl.BlockSpec(bias.shape, lambda i: (0, 0, 0)),
         pl.BlockSpec((tq, QK_DIM), lambda i: (i, 0)), pl.BlockSpec((BLOCK, QK_DIM), prev),
         kvrow, pl.BlockSpec((BLOCK, KV_DIM), prev), qrow, qrow, kvrow],
        [qrow, kvrow, kvrow, kvrow, kvrow, pl.BlockSpec((N_Q_HEADS, LANES), lambda i: (0, 0))],
        [jax.ShapeDtypeStruct((T, ATTN_DIM), F32), kv_shape, kv_shape, kv_shape, kv_shape,
         jax.ShapeDtypeStruct((N_Q_HEADS, LANES), F32)],
        (sinks, bias, qkn, qkn, zv, zv, do, o, lse), ("arbitrary",), side)


def _merge_fwd(pm, o, w_pb, w_ab, zg, name, side=None):
    T = pm.shape[0]
    tm = _tile(T, 512)

    def body(pm_ref, o_ref, wp_ref, wa_ref, zg_ref, a_ref, b_ref, m_ref):
        pmv, ov = pm_ref[...], o_ref[...]
        a = jnp.concatenate([_dot(pmv, wp_ref[j]) for j in range(N_CHIPS)], axis=1)
        b = jnp.concatenate([_dot(ov, wa_ref[j]) for j in range(N_CHIPS)], axis=1)
        gp = _sigmoid(zg_ref[:, :D_MODEL])
        ga = _sigmoid(zg_ref[:, D_MODEL:])
        a_ref[...] = a.astype(a_ref.dtype)
        b_ref[...] = b.astype(b_ref.dtype)
        m_ref[...] = (gp * a + ga * b).astype(m_ref.dtype)

    half = pl.BlockSpec((tm, POOL_DIM), lambda i: (i, 0))
    full = pl.BlockSpec((tm, D_MODEL), lambda i: (i, 0))
    wspec = pl.BlockSpec(w_pb.shape, lambda i: (0, 0, 0))
    out = jax.ShapeDtypeStruct((T, D_MODEL), CDT)
    return _pcall(body, name, (T // tm,), [half, half, wspec, wspec, pl.BlockSpec((tm, GATE_DIM), lambda i: (i, 0))],
                  [full, full, full], [out, out, out], (pm, o, w_pb, w_ab, zg), ("parallel",), side)


def _merge_bwd(dxo, w_out, a, b, zg, name):
    T = dxo.shape[0]
    tm = _tile(T, 512)

    def body(dx_ref, w_ref, a_ref, b_ref, zg_ref, da_ref, db_ref, dg_ref):
        dm = _dot_nt(dx_ref[...].astype(CDT), w_ref[...])
        gp = _sigmoid(zg_ref[:, :D_MODEL])
        ga = _sigmoid(zg_ref[:, D_MODEL:])
        da_ref[...] = (dm * gp).astype(da_ref.dtype)
        db_ref[...] = (dm * ga).astype(db_ref.dtype)
        dg_ref[:, :D_MODEL] = (dm * a_ref[...].astype(F32) * (gp * (1.0 - gp))).astype(dg_ref.dtype)
        dg_ref[:, D_MODEL:] = (dm * b_ref[...].astype(F32) * (ga * (1.0 - ga))).astype(dg_ref.dtype)

    full = pl.BlockSpec((tm, D_MODEL), lambda i: (i, 0))
    gate = pl.BlockSpec((tm, GATE_DIM), lambda i: (i, 0))
    out = jax.ShapeDtypeStruct((T, D_MODEL), CDT)
    return pl.pallas_call(
        body, name=name, grid=(T // tm,),
        in_specs=[full, pl.BlockSpec((D_MODEL, D_MODEL), lambda i: (0, 0)), full, full, gate],
        out_specs=[full, full, gate], out_shape=[out, out, jax.ShapeDtypeStruct((T, GATE_DIM), CDT)],
        compiler_params=_params("parallel"),
    )(dxo, w_out, a, b, zg)


def _adamw(w, g, m, v, name):
    Rr, C = w.shape
    tr = _tile(Rr, max(8, (1 << 19) // C // 8 * 8))

    def body(w_ref, g_ref, m_ref, v_ref, d_ref, nm_ref, nv_ref):
        gv = g_ref[...]
        nm = ADAM_B1 * m_ref[...] + (1.0 - ADAM_B1) * gv
        nv = ADAM_B2 * v_ref[...] + (1.0 - ADAM_B2) * (gv * gv)
        m_hat = nm / (1.0 - ADAM_B1 ** ADAM_STEP)
        v_hat = nv / (1.0 - ADAM_B2 ** ADAM_STEP)
        d_ref[...] = -ADAM_LR * (m_hat / (jnp.sqrt(v_hat) + ADAM_EPS) + ADAM_WD * w_ref[...])
        nm_ref[...] = nm
        nv_ref[...] = nv

    blk = pl.BlockSpec((tr, C), lambda i: (i, 0))
    out = jax.ShapeDtypeStruct((Rr, C), F32)
    return pl.pallas_call(
        body, name=name, grid=(Rr // tr,), in_specs=[blk] * 4, out_specs=[blk] * 3, out_shape=[out] * 3,
        compiler_params=_params("parallel"),
    )(w, g, m, v)


def _place():
    return lax.axis_index("x"), lax.axis_index("y"), lax.axis_index("c")


def _other_chip(x, y, d):
    return (1 - x if d & 2 else x), (1 - y if d & 1 else y)


def _rcopy(src, dst, ssem, rsem, dev):
    return pltpu.make_async_remote_copy(src_ref=src, dst_ref=dst, send_sem=ssem, recv_sem=rsem, device_id=dev,
                                        device_id_type=MESH)


def _row_half(rows, c):
    return pl.ds(c * (rows // 2), rows // 2)


def _gather_ici_side(shards, l):
    n = len(shards)

    def issue(ins, outs, ssem, rsem):
        x, y, c = _place()
        cps = []
        for w in range(n):
            half = _row_half(shards[w].shape[1], c)
            for d in (1, 2, 3):
                px, py = _other_chip(x, y, d)
                k = 3 * w + d - 1
                cps.append(_rcopy(ins[w].at[l, half], outs[w].at[2 * x + y, half], ssem.at[k], rsem.at[k], (px, py, c)))
        return cps

    return _Side(shards, [jax.ShapeDtypeStruct((N_CHIPS,) + s.shape[1:], s.dtype) for s in shards], 3 * n, issue)


def _gather_d2d_side(shards, gathered, l):
    n = len(shards)

    def issue(ins, outs, ssem, rsem):
        x, y, c = _place()
        sibling = (x, y, 1 - c)
        cps = []
        for w in range(n):
            half = _row_half(shards[w].shape[1], c)
            for d in (1, 2, 3):
                px, py = _other_chip(x, y, d)
                k = 3 * w + d - 1
                got = outs[w].at[2 * px + py, half]
                cps.append(_rcopy(got, got, ssem.at[k], rsem.at[k], sibling))
            cps.append(_rcopy(ins[n + w].at[l], outs[w].at[2 * x + y], ssem.at[3 * n + w], rsem.at[3 * n + w], sibling))
        return cps

    return _Side(list(gathered) + list(shards), [jax.ShapeDtypeStruct(g.shape, g.dtype) for g in gathered], 4 * n, issue,
                 aliases={w: w for w in range(n)})


def _reduce_sibling_side(gms):
    n = len(gms)

    def issue(ins, outs, ssem, rsem):
        x, y, c = _place()
        return [_rcopy(ins[w].at[:, _row_half(gms[w].shape[1], 1 - c)], outs[w], ssem.at[w], rsem.at[w], (x, y, 1 - c))
                for w in range(n)]

    return _Side(gms, [jax.ShapeDtypeStruct((N_CHIPS, g.shape[1] // 2, g.shape[2]), g.dtype) for g in gms], n, issue)


def _reduce_chip_side(ps):
    n = len(ps)

    def issue(ins, outs, ssem, rsem):
        x, y, c = _place()
        cps = []
        for w in range(n):
            for d in (1, 2, 3):
                px, py = _other_chip(x, y, d)
                k = 3 * w + d - 1
                cps.append(_rcopy(ins[w].at[2 * px + py], outs[w].at[2 * x + y], ssem.at[k], rsem.at[k], (px, py, c)))
        return cps

    return _Side(ps, [jax.ShapeDtypeStruct(p.shape, p.dtype) for p in ps], 3 * n, issue)


def _share_side(accs):
    n = len(accs)

    def issue(ins, outs, ssem, rsem):
        x, y, c = _place()
        cps = []
        for w in range(n):
            mine = outs[w].at[:, _row_half(accs[w].shape[1], c)]
            cps.append(_rcopy(mine, mine, ssem.at[w], rsem.at[w], (x, y, 1 - c)))
        return cps

    return _Side(accs, [jax.ShapeDtypeStruct(a.shape, a.dtype) for a in accs], n, issue, aliases={w: w for w in range(n)})


def _sum_rows(rows, b):
    return _tile(rows, max(16, (1 << 19) // b // 16 * 16), 16)


def _pair_sum(g, recv, place, name):
    _, ah, b = recv.shape
    ta = _sum_rows(ah, b)
    nr = ah // ta

    def body(p_ref, g_ref, r_ref, o_ref):
        o_ref[...] = (g_ref[...].astype(F32) + r_ref[...].astype(F32)).astype(o_ref.dtype)

    blk = (None, ta, b)
    return pl.pallas_call(
        body, name=name,
        grid_spec=pltpu.PrefetchScalarGridSpec(
            num_scalar_prefetch=1, grid=(N_CHIPS, nr),
            in_specs=[pl.BlockSpec(blk, lambda j, r, p: (j, p[0] * nr + r, 0)),
                      pl.BlockSpec(blk, lambda j, r, p: (j, r, 0))],
            out_specs=pl.BlockSpec(blk, lambda j, r, p: (j, r, 0))),
        out_shape=jax.ShapeDtypeStruct(recv.shape, recv.dtype),
        compiler_params=_params("parallel", "parallel"),
    )(place, g, recv)


def _chip_sum(slots, part, place, acc, l, name):
    _, ah, b = slots.shape
    ta = _sum_rows(ah, b)
    nr = ah // ta

    def body(p_ref, s_ref, own_ref, acc_ref, o_ref):
        j = p_ref[1]
        own = own_ref[...].astype(F32)
        term = [jnp.where(j == s_, own, s_ref[s_].astype(F32)) for s_ in range(N_CHIPS)]
        o_ref[...] = ((term[0] + term[1]) + term[2]) + term[3]

    return pl.pallas_call(
        body, name=name,
        grid_spec=pltpu.PrefetchScalarGridSpec(
            num_scalar_prefetch=1, grid=(nr,),
            in_specs=[pl.BlockSpec((N_CHIPS, ta, b), lambda r, p: (0, r, 0)),
                      pl.BlockSpec((None, ta, b), lambda r, p: (p[1], r, 0)), ANY],
            out_specs=pl.BlockSpec((None, ta, b), lambda r, p: (l, p[0] * nr + r, 0))),
        out_shape=jax.ShapeDtypeStruct(acc.shape, F32), input_output_aliases={3: 0},
        compiler_params=_params("parallel"),
    )(place, slots, part, acc)


def _all_reduce_small(v):
    Rr = v.shape[0]

    def body(v_ref, slots_ref, out_ref, ssem, rsem):
        x, y, c = _place()
        me = 4 * x + 2 * y + c
        slots_ref[pl.ds(me, 1)] = v_ref[...][None]
        cps = []
        for d in range(1, N_DEV):
            px, py = _other_chip(x, y, d >> 1)
            pc = 1 - c if d & 1 else c
            cps.append(_rcopy(v_ref, slots_ref.at[me], ssem.at[d - 1], rsem.at[d - 1], (px, py, pc)))
            cps[-1].start()
        for cp in cps:
            cp.wait_recv()
        for cp in cps:
            cp.wait_send()
        acc = slots_ref[0]
        for s in range(1, N_DEV):
            acc = acc + slots_ref[s]
        out_ref[...] = acc

    vm = pl.BlockSpec(memory_space=pltpu.VMEM)
    return pl.pallas_call(
        body, name="all_reduce_small", in_specs=[vm], out_specs=[vm, vm],
        out_shape=[jax.ShapeDtypeStruct((N_DEV, Rr, LANES), F32), jax.ShapeDtypeStruct((Rr, LANES), F32)],
        scratch_shapes=[pltpu.SemaphoreType.DMA((N_DEV - 1,)), pltpu.SemaphoreType.DMA((N_DEV - 1,))],
        compiler_params=pltpu.CompilerParams(vmem_limit_bytes=VMEM_LIMIT_BYTES),
    )(v)[1]


def _ffn_forward(x, p, tag, side_of):
    h = _norm_fwd(x, p[f"ln_{tag}"], f"{tag}_norm")
    gu, act = _ffn_up(h, p[f"w_{tag}_gu"], f"{tag}_up", side_of(f"{tag}_up"))
    x_out = _mm_nn(act, p[f"w_{tag}_down"], f"{tag}_down", F32, res=x, scale=0.5, side=side_of(f"{tag}_down"))
    return x_out, (x, h, gu, act)


def _row_blocks_of(dw):
    return dw.reshape(N_CHIPS, dw.shape[0] // N_CHIPS, dw.shape[1])


def _ffn_backward(dxo, saved, p, tag, side_of, grad):
    x, h, gu, act = saved
    dgu = _ffn_down_bwd(dxo, p[f"w_{tag}_down"], gu, f"{tag}_down_bwd", side_of(f"{tag}_down_bwd"))
    grad(f"w_{tag}_down", _row_blocks_of(_mm_tn(act, dxo, f"{tag}_dwd", scale=0.5)))
    grad(f"w_{tag}_gu", _mm_tn(h, dgu, f"{tag}_dwgu", col_blocks=N_CHIPS, tn_target=1408, tm_target=1024,
                               side=side_of(f"{tag}_dwgu")))
    dx, d_ln = _mm_nt_norm_bwd([dgu], p[f"w_{tag}_gu"], x, p[f"ln_{tag}"], dxo, f"{tag}_dh_norm_bwd",
                               side_of(f"{tag}_dh_norm_bwd"))
    grad(f"ln_{tag}", d_ln[0])
    return dx


def _mixer_forward(x, p, tabs, side_of):
    h = _norm_fwd(x, p["ln_mix"], "mix_norm")
    zu, zqk, zv, zg = _mm_in(h, p["w_in"], "mix_in", side_of("mix_in"))
    pm = _pool_fwd(zu, p["pool_w"], p["pool_scale"], "pool_fwd")
    qkn = _qk_fwd(zqk, p["gqk"], *tabs, "qk_fwd")
    o, lse = _attn_fwd(qkn, zv, p["sinks"], "attn_fwd", side_of("attn_fwd"))
    a, b, m = _merge_fwd(pm, o, p["w_pool_branch"], p["w_attn_branch"], zg, "merge_fwd", side_of("merge_fwd"))
    x_out = _mm_nn(m, p["w_out"], "mix_out", F32, res=x, scale=1.0)
    return x_out, (x, h, zu, zqk, zv, zg, pm, qkn, o, lse, a, b, m)


def _shift_up(v):
    return jnp.concatenate([v[BLOCK:], jnp.zeros((BLOCK, v.shape[1]), v.dtype)], axis=0)


def _mixer_backward(dxo, saved, p, tabs, side_of, grad):
    x, h, zu, zqk, zv, zg, pm, qkn, o, lse, a, b, m = saved
    d_a, d_b, dgl = _merge_bwd(dxo, p["w_out"], a, b, zg, "merge_bwd")
    grad("w_out", _row_blocks_of(_mm_tn(m, dxo, "mix_dwout")))
    dpm = _mm_nt_blocks(d_a, p["w_pool_branch"], "pool_branch_dx", CDT)
    grad("w_pool_branch", _mm_tn(pm, d_a, "pool_branch_dw", col_blocks=N_CHIPS))
    do = _mm_nt_blocks(d_b, p["w_attn_branch"], "attn_branch_dx", CDT)
    grad("w_attn_branch", _mm_tn(o, d_b, "attn_branch_dw", col_blocks=N_CHIPS))
    du, d_pool_w, d_pool_scale = _pool_bwd(zu, dpm, p["pool_w"], p["pool_scale"], "pool_bwd")
    grad("pool_w", d_pool_w)
    grad("pool_scale", d_pool_scale)
    dq, dkc, dkp, dvc, dvp, dsink = _attn_bwd(qkn, zv, p["sinks"], do, o, lse, "attn_bwd", side_of("attn_bwd"))
    dqk = jnp.concatenate([dq, dkc + _shift_up(dkp)], axis=1)
    dv = dvc + _shift_up(dvp)
    dzqk, dgqk = _qk_bwd(dqk, zqk, p["gqk"], *tabs, "qk_bwd")
    grad("q_norm", dgqk[0, :ATTN_DIM].reshape(N_Q_HEADS, HEAD_DIM).sum(axis=0))
    grad("k_norm", dgqk[0, ATTN_DIM:].reshape(KV_DIM // HEAD_DIM, HEAD_DIM).sum(axis=0))
    grad("sinks", -dsink[:, 0])
    dz = [du, dzqk, dv, dgl]
    grad("w_in", _blocks_from_full("w_in", _mm_tn_parts(h, dz, "mix_dwin")).astype(WIRE_DT))
    dx, d_ln = _mm_nt_norm_bwd(dz, p["w_in"], x, p["ln_mix"], dxo, "mix_dh_norm_bwd", side_of("mix_dh_norm_bwd"))
    grad("ln_mix", d_ln[0])
    return dx


class _NoComm:
    def __init__(self, layers):
        self.layers, self.grads = layers, [dict() for _ in layers]

    def weight(self, l, name):
        return self.layers[l][name]

    def side(self, phase, l, host):
        return None

    def grad(self, l, name, value):
        self.grads[l][name] = value


class _Layer:
    def __init__(self, hooks, l):
        self.hooks, self.l, self.got = hooks, l, {}

    def __getitem__(self, name):
        if name not in self.got:
            self.got[name] = self.hooks.weight(self.l, name)
        return self.got[name]


def _local_step(x, tgt, n_layers, hooks):
    T = x.shape[0]
    tabs = _rope_tables(T)
    saved, params = [], []
    for l in range(n_layers):
        p = _Layer(hooks, l)
        side_of = functools.partial(hooks.side, "fwd", l)
        x, s1 = _ffn_forward(x, p, "ffn1", side_of)
        x, s2 = _mixer_forward(x, p, tabs, side_of)
        x, s3 = _ffn_forward(x, p, "ffn2", side_of)
        saved.append((s1, s2, s3))
        params.append(p)
    dx, loss = _loss_head(x, tgt, "loss_head")
    for l in reversed(range(n_layers)):
        p = params[l]
        s1, s2, s3 = saved[l]
        side_of = functools.partial(hooks.side, "bwd", l)
        grad = functools.partial(hooks.grad, l)
        dx = _ffn_backward(dx, s3, p, "ffn2", side_of, grad)
        dx = _mixer_backward(dx, s2, p, tabs, side_of, grad)
        dx = _ffn_backward(dx, s1, p, "ffn1", side_of, grad)
    return loss, dx


def _full_from_blocks(name, blocks):
    if name in COL_SHARDED:
        return jnp.transpose(blocks, (1, 0, 2)).reshape(blocks.shape[1], N_CHIPS * blocks.shape[2])
    return blocks.reshape(N_CHIPS * blocks.shape[1], blocks.shape[2])


def _blocks_from_full(name, full):
    K, N = full.shape
    if name in COL_SHARDED:
        return jnp.transpose(full.reshape(K, N_CHIPS, N // N_CHIPS), (1, 0, 2))
    return full.reshape(N_CHIPS, K // N_CHIPS, N)


JOBS = {"a": ("w_ffn1_gu", "w_ffn1_down"), "b": ("w_in", "w_pool_branch", "w_attn_branch", "w_out"),
        "c": ("w_ffn2_gu", "w_ffn2_down")}
GATHER_PLAN = {"ffn1_up": ("ici", "b", JOBS["b"], 0), "ffn1_down": ("d2d", "b", JOBS["b"], 0),
               "mix_in": ("ici", "c", JOBS["c"][:1], 0), "attn_fwd": ("ici", "c", JOBS["c"][1:], 0),
               "merge_fwd": ("d2d", "c", JOBS["c"], 0),
               "ffn2_up": ("ici", "a", JOBS["a"], 1), "ffn2_down": ("d2d", "a", JOBS["a"], 1)}
REDUCE_PLAN = {"ffn2_down_bwd": ("sibling", "a", 1), "ffn2_dwgu": ("chip", "a", 1),
               "ffn2_dh_norm_bwd": ("sibling", "c", 0), "attn_bwd": ("chip", "c", 0),
               "mix_dh_norm_bwd": ("sibling", "b", 0), "ffn1_down_bwd": ("chip", "b", 0)}


class _Exchange:
    def __init__(self, shards, small, place, n_layers):
        self.shards, self.small, self.place, self.n_layers = shards, small, place, n_layers
        first = [shards[n] for n in JOBS["a"]]
        got = _run_side(_gather_ici_side(first, 0), "gather_ici")
        got = _run_side(_gather_d2d_side(first, got, 0), "gather_d2d")
        self.blocks = {(n, 0): g for n, g in zip(JOBS["a"], got)}
        self.landed = {}
        self.handed = []
        self.acc = {n: lax.empty(shards[n].shape, F32) for n in BIG}
        self.grads = [dict() for _ in range(n_layers)]
        self.reduce = {}
        self.summed = set()

    def weight(self, l, name):
        if name not in BIG:
            return self.small(l)[name]
        for names, layer, done in self.handed:
            self.blocks.update({(n, layer): g for n, g in zip(names, done.outs)})
        self.handed.clear()
        blocks = self.blocks.pop((name, l))
        return blocks if name in USED_AS_BLOCKS else _full_from_blocks(name, blocks)

    def _gather_side(self, l, host):
        step, job, names, ahead = GATHER_PLAN[host]
        layer = l + ahead
        if layer >= self.n_layers:
            return None
        if step == "ici":
            side = _gather_ici_side([self.shards[n] for n in names], layer)
            self.landed.setdefault((job, layer), []).append((names, side))
            return side
        names, gathered = JOBS[job], {}
        for part_names, side in self.landed.pop((job, layer)):
            gathered.update(zip(part_names, side.outs))
        done = _gather_d2d_side([self.shards[n] for n in names], [gathered[n] for n in names], layer)
        self.handed.append((names, layer, done))
        return done

    def grad(self, l, name, value):
        self.grads[l][name] = value

    def _reduce_side(self, l, host):
        step, job, ahead = REDUCE_PLAN[host]
        layer = l + ahead
        if layer >= self.n_layers:
            return None
        return self._reduce_step(step, job, layer)

    def _reduce_step(self, step, job, layer):
        if step == "sibling":
            st = self.reduce[(job, layer)] = dict(gm=[self.grads[layer][n] for n in JOBS[job]])
            st["sibling"] = _reduce_sibling_side(st["gm"])
            return st["sibling"]
        st = self.reduce[(job, layer)]
        st["part"] = [_pair_sum(g, r, self.place, "grad_pair_sum") for g, r in zip(st["gm"], st["sibling"].outs)]
        st["chip"] = _reduce_chip_side(st["part"])
        return st["chip"]

    def _chip_sums(self):
        for (job, layer), st in self.reduce.items():
            if (job, layer) not in self.summed and "chip" in st and st["chip"].outs is not None:
                self.summed.add((job, layer))
                for n, slots, part in zip(JOBS[job], st["chip"].outs, st["part"]):
                    self.acc[n] = _chip_sum(slots, part, self.place, self.acc[n], layer, "grad_chip_sum")

    def side(self, phase, l, host):
        if phase == "fwd":
            return self._gather_side(l, host) if host in GATHER_PLAN else None
        self._chip_sums()
        return self._reduce_side(l, host) if host in REDUCE_PLAN else None

    def reduced(self):
        _run_side(self._reduce_step("sibling", "a", 0), "grad_sibling_exchange")
        _run_side(self._reduce_step("chip", "a", 0), "grad_chip_exchange")
        self._chip_sums()
        return dict(zip(BIG, _run_side(_share_side([self.acc[n] for n in BIG]), "grad_sibling_share")))


def _pack_small(parts):
    rows, spans, lo = [], [], 0
    for v in parts:
        flat = v.reshape(-1)
        nrow = -(-flat.shape[0] // LANES)
        flat = jnp.pad(flat, (0, nrow * LANES - flat.shape[0]))
        rows.append(flat.reshape(nrow, LANES))
        spans.append((lo, nrow))
        lo += nrow
    pad = -lo % 8
    if pad:
        rows.append(jnp.zeros((pad, LANES), F32))
    return jnp.concatenate(rows, axis=0), spans


def _unpack_small(packed, spans, shapes):
    out = []
    for (lo, nrow), shape in zip(spans, shapes):
        size = 1
        for s in shape:
            size *= s
        out.append(packed[lo:lo + nrow].reshape(-1)[:size].reshape(shape))
    return out


def kernel(x, ln_ffn1, w_ffn1_gu, w_ffn1_down, ln_mix, w_in, pool_w, pool_scale, w_pool_branch, q_norm, k_norm, sinks, w_attn_branch, w_out, ln_ffn2, w_ffn2_gu, w_ffn2_down, loss_target, m_ln_ffn1, m_w_ffn1_gu, m_w_ffn1_down, m_ln_mix, m_w_in, m_pool_w, m_pool_scale, m_w_pool_branch, m_q_norm, m_k_norm, m_sinks, m_w_attn_branch, m_w_out, m_ln_ffn2, m_w_ffn2_gu, m_w_ffn2_down, v_ln_ffn1, v_w_ffn1_gu, v_w_ffn1_down, v_ln_mix, v_w_in, v_pool_w, v_pool_scale, v_w_pool_branch, v_q_norm, v_k_norm, v_sinks, v_w_attn_branch, v_w_out, v_ln_ffn2, v_w_ffn2_gu, v_w_ffn2_down):
    w = dict(ln_ffn1=ln_ffn1, w_ffn1_gu=w_ffn1_gu, w_ffn1_down=w_ffn1_down, ln_mix=ln_mix, w_in=w_in, pool_w=pool_w,
             pool_scale=pool_scale, w_pool_branch=w_pool_branch, q_norm=q_norm, k_norm=k_norm, sinks=sinks,
             w_attn_branch=w_attn_branch, w_out=w_out, ln_ffn2=ln_ffn2, w_ffn2_gu=w_ffn2_gu, w_ffn2_down=w_ffn2_down)
    mom = dict(ln_ffn1=m_ln_ffn1, w_ffn1_gu=m_w_ffn1_gu, w_ffn1_down=m_w_ffn1_down, ln_mix=m_ln_mix, w_in=m_w_in,
               pool_w=m_pool_w, pool_scale=m_pool_scale, w_pool_branch=m_w_pool_branch, q_norm=m_q_norm, k_norm=m_k_norm,
               sinks=m_sinks, w_attn_branch=m_w_attn_branch, w_out=m_w_out, ln_ffn2=m_ln_ffn2, w_ffn2_gu=m_w_ffn2_gu,
               w_ffn2_down=m_w_ffn2_down)
    var = dict(ln_ffn1=v_ln_ffn1, w_ffn1_gu=v_w_ffn1_gu, w_ffn1_down=v_w_ffn1_down, ln_mix=v_ln_mix, w_in=v_w_in,
               pool_w=v_pool_w, pool_scale=v_pool_scale, w_pool_branch=v_w_pool_branch, q_norm=v_q_norm, k_norm=v_k_norm,
               sinks=v_sinks, w_attn_branch=v_w_attn_branch, w_out=v_w_out, ln_ffn2=v_ln_ffn2, w_ffn2_gu=v_w_ffn2_gu,
               w_ffn2_down=v_w_ffn2_down)
    L = ln_ffn1.shape[0]

    def small(l):
        return dict(ln_ffn1=ln_ffn1[l], ln_mix=ln_mix[l], ln_ffn2=ln_ffn2[l], pool_w=pool_w[l].astype(CDT),
                    pool_scale=pool_scale[l], sinks=sinks[l],
                    gqk=jnp.concatenate([jnp.tile(q_norm[l], N_Q_HEADS), jnp.tile(k_norm[l], KV_DIM // HEAD_DIM)]).reshape(1, QK_DIM))

    place = jnp.stack([lax.axis_index("c"), 2 * lax.axis_index("x") + lax.axis_index("y")]).astype(jnp.int32)
    hooks = _Exchange({n: w[n].astype(CDT) for n in BIG}, small, place, L)
    loss_part, grad_x = _local_step(x[0], loss_target[0], L, hooks)
    g_big = hooks.reduced()
    grads = hooks.grads

    small_parts = [jnp.stack([g[n] for g in grads]) for n in SMALL] + [loss_part]
    packed, spans = _pack_small(small_parts)
    summed = _all_reduce_small(packed)
    *g_small_list, loss_sum = _unpack_small(summed, spans, [w[n].shape for n in SMALL] + [(1, 1)])
    g_small = dict(zip(SMALL, g_small_list))
    loss = loss_sum[0, 0]

    grad_out, delta, new_m, new_v = {}, {}, {}, {}
    for n in BIG:
        shape = w[n].shape
        flat = (shape[0] * shape[1], shape[2])
        grad_out[n] = g_big[n]
        d, nm, nv = _adamw(w[n].reshape(flat), g_big[n].reshape(flat), mom[n].reshape(flat), var[n].reshape(flat), "adamw")
        delta[n], new_m[n], new_v[n] = d.reshape(shape), nm.reshape(shape), nv.reshape(shape)
    pw, _ = _pack_small([w[n] for n in SMALL])
    pg, sp = _pack_small([g_small[n] for n in SMALL])
    pm_, _ = _pack_small([mom[n] for n in SMALL])
    pv, _ = _pack_small([var[n] for n in SMALL])
    d, nm, nv = _adamw(pw, pg, pm_, pv, "adamw_small")
    shapes = [w[n].shape for n in SMALL]
    for n, dv, mv, vv in zip(SMALL, _unpack_small(d, sp, shapes), _unpack_small(nm, sp, shapes), _unpack_small(nv, sp, shapes)):
        grad_out[n], delta[n], new_m[n], new_v[n] = g_small[n], dv, mv, vv

    return (loss, grad_x[None], *[grad_out[n] for n in WEIGHTS], *[delta[n] for n in WEIGHTS],
            *[new_m[n] for n in WEIGHTS], *[new_v[n] for n in WEIGHTS])
```

```python
import functools
import math

import jax
import jax.numpy as jnp
from jax import lax
from jax.experimental import pallas as pl
from jax.experimental.pallas import tpu as pltpu

F32 = jnp.float32
CDT = jnp.bfloat16
WIRE_DT = jnp.bfloat16

D_MODEL = 1024
POOL_WINDOWS = (2, 4, 8, 16)
POOL_WMAX = 16
GROUP = 128
POOL_DIM = 512
HEAD_DIM = 64
N_Q_HEADS = 8
ATTN_DIM = 512
KV_DIM = 128
QK_DIM = ATTN_DIM + KV_DIM
GATE_DIM = 2 * D_MODEL
BLOCK = 128
ROPE_THETA = 500000.0
ROT_DIM = 16
EPS = 1e-6
ATTN_SCALE = HEAD_DIM ** -0.5

ADAM_LR = 0.001
ADAM_B1 = 0.9
ADAM_B2 = 0.999
ADAM_EPS = 1e-08
ADAM_WD = 0.01
ADAM_STEP = 10

N_CHIPS = 4
N_DEV = 8
LANES = 128
VMEM_LIMIT_BYTES = 48 * 1024 * 1024

MESH = pl.DeviceIdType.MESH
ANY = pl.BlockSpec(memory_space=pl.ANY)

BIG = ("w_ffn1_gu", "w_ffn1_down", "w_in", "w_pool_branch", "w_attn_branch", "w_out", "w_ffn2_gu", "w_ffn2_down")
COL_SHARDED = ("w_ffn1_gu", "w_in", "w_pool_branch", "w_attn_branch", "w_ffn2_gu")
USED_AS_BLOCKS = ("w_pool_branch", "w_attn_branch")
WIDE = ("w_ffn1_gu", "w_ffn2_gu")
SMALL = ("ln_ffn1", "ln_mix", "pool_w", "pool_scale", "q_norm", "k_norm", "sinks", "ln_ffn2")
WEIGHTS = ("ln_ffn1", "w_ffn1_gu", "w_ffn1_down", "ln_mix", "w_in", "pool_w", "pool_scale", "w_pool_branch",
           "q_norm", "k_norm", "sinks", "w_attn_branch", "w_out", "ln_ffn2", "w_ffn2_gu", "w_ffn2_down")


def _tile(n, target, mult=8):
    if n <= target:
        return n
    for t in range(target - target % mult, 0, -mult):
        if n % t == 0:
            return t
    raise ValueError((n, target, mult))


def _params(*sem):
    return pltpu.CompilerParams(dimension_semantics=sem, vmem_limit_bytes=VMEM_LIMIT_BYTES)


def _sigmoid(v):
    return 0.5 * jnp.tanh(0.5 * v) + 0.5


def _dot(a, b):
    return jnp.dot(a, b, preferred_element_type=F32)


def _dot_nt(a, b):
    return lax.dot_general(a, b, (((1,), (1,)), ((), ())), preferred_element_type=F32)


def _dot_tn(a, b):
    return lax.dot_general(a, b, (((0,), (0,)), ((), ())), preferred_element_type=F32)


class _Side:
    def __init__(self, ins, out_shapes, n_sems, issue, aliases=None):
        self.ins, self.out_shapes, self.n_sems, self.issue = list(ins), list(out_shapes), n_sems, issue
        self.aliases = dict(aliases or {})
        self.outs = None


def _pcall(body, name, grid, in_specs, out_specs, out_shape, args, dims, side=None, scratch=()):
    scratch = list(scratch)
    if side is None:
        return pl.pallas_call(body, name=name, grid=grid, in_specs=in_specs, out_specs=out_specs, out_shape=out_shape,
                              scratch_shapes=scratch, compiler_params=_params(*dims))(*args)
    n_in, n_out, s_in, s_out = len(in_specs), len(out_specs), len(side.ins), len(side.out_shapes)

    def wrapped(*refs):
        main_in, side_in = refs[:n_in], refs[n_in:n_in + s_in]
        main_out = refs[n_in + s_in:n_in + s_in + n_out]
        side_out = refs[n_in + s_in + n_out:n_in + s_in + n_out + s_out]
        rest = refs[n_in + s_in + n_out + s_out:]
        main_scratch, (ssem, rsem) = rest[:len(scratch)], rest[len(scratch):]
        ids = [pl.program_id(ax) for ax in range(len(grid))]
        first = functools.reduce(jnp.logical_and, [i == 0 for i in ids])
        last = functools.reduce(jnp.logical_and, [i == g - 1 for i, g in zip(ids, grid)])

        @pl.when(first)
        def _():
            for cp in side.issue(side_in, side_out, ssem, rsem):
                cp.start()

        body(*main_in, *main_out, *main_scratch)

        @pl.when(last)
        def _():
            cps = side.issue(side_in, side_out, ssem, rsem)
            for cp in cps:
                cp.wait_recv()
            for cp in cps:
                cp.wait_send()

    outs = pl.pallas_call(
        wrapped, name=name, grid=grid, in_specs=list(in_specs) + [ANY] * s_in, out_specs=list(out_specs) + [ANY] * s_out,
        out_shape=list(out_shape) + side.out_shapes,
        input_output_aliases={n_in + i: n_out + o for i, o in side.aliases.items()},
        scratch_shapes=scratch + [pltpu.SemaphoreType.DMA((side.n_sems,))] * 2,
        compiler_params=_params(*["arbitrary"] * len(grid)),
    )(*args, *side.ins)
    side.outs = list(outs[n_out:])
    return list(outs[:n_out])


def _run_side(side, name):
    s_in = len(side.ins)

    def body(*refs):
        ssem, rsem = refs[s_in + len(side.out_shapes):]
        cps = side.issue(refs[:s_in], refs[s_in:s_in + len(side.out_shapes)], ssem, rsem)
        for cp in cps:
            cp.start()
        for cp in cps:
            cp.wait_recv()
        for cp in cps:
            cp.wait_send()

    side.outs = list(pl.pallas_call(
        body, name=name, in_specs=[ANY] * s_in, out_specs=[ANY] * len(side.out_shapes), out_shape=side.out_shapes,
        input_output_aliases=side.aliases, scratch_shapes=[pltpu.SemaphoreType.DMA((side.n_sems,))] * 2,
    )(*side.ins))
    return side.outs


def _norm_fwd(x, g, name):
    T, Dm = x.shape
    tm = _tile(T, 512)

    def body(x_ref, g_ref, h_ref):
        xv = x_ref[...]
        r = lax.rsqrt(jnp.mean(xv * xv, axis=-1, keepdims=True) + EPS)
        h_ref[...] = (xv * r * g_ref[...]).astype(h_ref.dtype)

    row = pl.BlockSpec((tm, Dm), lambda i: (i, 0))
    return pl.pallas_call(
        body, name=name, grid=(T // tm,),
        in_specs=[row, pl.BlockSpec((1, Dm), lambda i: (0, 0))], out_specs=row,
        out_shape=jax.ShapeDtypeStruct((T, Dm), CDT), compiler_params=_params("parallel"),
    )(x, g.reshape(1, Dm))


def _loss_head(y, tgt, name):
    T, Dm = y.shape
    tm = _tile(T, 512)

    def body(y_ref, t_ref, dy_ref, loss_ref):
        @pl.when(pl.program_id(0) == 0)
        def _():
            loss_ref[...] = jnp.zeros_like(loss_ref)

        diff = y_ref[...] - t_ref[...]
        dy_ref[...] = diff * (1.0 / Dm)
        part = jnp.sum(jnp.mean(diff * diff, axis=-1, keepdims=True), axis=0, keepdims=True)
        loss_ref[...] += 0.5 * part

    row = pl.BlockSpec((tm, Dm), lambda i: (i, 0))
    one = pl.BlockSpec((1, 1), lambda i: (0, 0))
    return pl.pallas_call(
        body, name=name, grid=(T // tm,),
        in_specs=[row, row], out_specs=[row, one],
        out_shape=[jax.ShapeDtypeStruct((T, Dm), F32), jax.ShapeDtypeStruct((1, 1), F32)],
        compiler_params=_params("arbitrary"),
    )(y, tgt)


def _mm_nn(a, b, name, out_dtype, res=None, scale=1.0, tm_target=512, side=None):
    M, K = a.shape
    N = b.shape[1]
    tm = _tile(M, tm_target)

    def body(a_ref, b_ref, *rest):
        acc = _dot(a_ref[...].astype(CDT), b_ref[...])
        if res is None:
            (o_ref,) = rest
        else:
            r_ref, o_ref = rest
            acc = r_ref[...] + scale * acc
        o_ref[...] = acc.astype(o_ref.dtype)

    in_specs = [pl.BlockSpec((tm, K), lambda i: (i, 0)), pl.BlockSpec((K, N), lambda i: (0, 0))]
    args = [a, b]
    if res is not None:
        in_specs.append(pl.BlockSpec((tm, N), lambda i: (i, 0)))
        args.append(res)
    return _pcall(body, name, (M // tm,), in_specs, [pl.BlockSpec((tm, N), lambda i: (i, 0))],
                  [jax.ShapeDtypeStruct((M, N), out_dtype)], args, ("parallel",), side)[0]


def _mm_nt_blocks(a, b4, name, out_dtype, tm_target=512):
    M, K = a.shape
    nb, N, Kb = b4.shape
    tm = _tile(M, tm_target)

    def body(a_ref, b_ref, o_ref):
        acc = _dot_nt(a_ref[:, :Kb].astype(CDT), b_ref[0])
        for j in range(1, nb):
            acc = acc + _dot_nt(a_ref[:, j * Kb:(j + 1) * Kb].astype(CDT), b_ref[j])
        o_ref[...] = acc.astype(o_ref.dtype)

    return pl.pallas_call(
        body, name=name, grid=(M // tm,),
        in_specs=[pl.BlockSpec((tm, K), lambda i: (i, 0)), pl.BlockSpec(b4.shape, lambda i: (0, 0, 0))],
        out_specs=pl.BlockSpec((tm, N), lambda i: (i, 0)),
        out_shape=jax.ShapeDtypeStruct((M, N), out_dtype), compiler_params=_params("parallel"),
    )(a, b4)


def _mm_tn(x, dy, name, scale=1.0, col_blocks=1, tn_target=1664, tm_target=1408, tk_target=1024, side=None):
    T, M = x.shape
    split = dy.ndim == 3
    Nh = dy.shape[-1]
    N = 2 * Nh if split else Nh
    nb = N // col_blocks
    whole = col_blocks > 1 and not split and N <= tn_target
    tm = _tile(M, tm_target, LANES)
    tn = N if whole else _tile(math.gcd(Nh, nb), tn_target, LANES)
    tk = _tile(T, tk_target)
    nk = T // tk
    njh, njb = Nh // tn, max(nb // tn, 1)

    def body(x_ref, dy_ref, o_ref, acc_ref):
        k = pl.program_id(2)

        @pl.when(k == 0)
        def _():
            acc_ref[...] = jnp.zeros_like(acc_ref)

        acc_ref[...] += _dot_tn(x_ref[...].astype(CDT), dy_ref[...].astype(CDT))

        @pl.when(k == nk - 1)
        def _():
            res = (acc_ref[...] if scale == 1.0 else scale * acc_ref[...]).astype(o_ref.dtype)
            if whole:
                for b in range(col_blocks):
                    o_ref[b] = res[:, b * nb:(b + 1) * nb]
            else:
                o_ref[...] = res

    if split:
        dy_spec = pl.BlockSpec((None, tk, tn), lambda i, j, k: (j // njh, k, j % njh))
    else:
        dy_spec = pl.BlockSpec((tk, tn), lambda i, j, k: (k, j))
    if col_blocks == 1:
        out_spec, out_dims = pl.BlockSpec((tm, tn), lambda i, j, k: (i, j)), (M, N)
    elif whole:
        out_spec, out_dims = pl.BlockSpec((col_blocks, tm, nb), lambda i, j, k: (0, i, 0)), (col_blocks, M, nb)
    else:
        out_spec, out_dims = pl.BlockSpec((None, tm, tn), lambda i, j, k: (j // njb, i, j % njb)), (col_blocks, M, nb)
    return _pcall(body, name, (M // tm, N // tn, nk), [pl.BlockSpec((tk, tm), lambda i, j, k: (k, i)), dy_spec],
                  [out_spec], [jax.ShapeDtypeStruct(out_dims, WIRE_DT)], (x, dy), ("parallel", "parallel", "arbitrary"),
                  side, [pltpu.VMEM((tm, tn), F32)])[0]


def _mm_tn_parts(x, parts, name):
    T, M = x.shape
    widths = [p.shape[1] for p in parts]
    N = sum(widths)
    tk = _tile(T, 512)

    def body(x_ref, *refs):
        o_ref = refs[-1]

        @pl.when(pl.program_id(0) == 0)
        def _():
            o_ref[...] = jnp.zeros_like(o_ref)

        xv = x_ref[...].astype(CDT)
        lo = 0
        for p_ref, wd in zip(refs[:-1], widths):
            o_ref[:, lo:lo + wd] += _dot_tn(xv, p_ref[...].astype(CDT))
            lo += wd

    return pl.pallas_call(
        body, name=name, grid=(T // tk,),
        in_specs=[pl.BlockSpec((tk, M), lambda k: (k, 0))] + [pl.BlockSpec((tk, wd), lambda k: (k, 0)) for wd in widths],
        out_specs=pl.BlockSpec((M, N), lambda k: (0, 0)),
        out_shape=jax.ShapeDtypeStruct((M, N), F32), compiler_params=_params("arbitrary"),
    )(x, *parts)


def _ffn_up(h, wgu, name, side=None):
    T, Dm = h.shape
    Fd = wgu.shape[1] // 2
    tm = _tile(T, 256)

    def body(h_ref, wg_ref, wu_ref, gu_ref, a_ref):
        hv = h_ref[...]
        g = _dot(hv, wg_ref[...])
        u = _dot(hv, wu_ref[...])
        gu_ref[0] = g.astype(gu_ref.dtype)
        gu_ref[1] = u.astype(gu_ref.dtype)
        a_ref[...] = (g * _sigmoid(g) * u).astype(a_ref.dtype)

    return _pcall(
        body, name, (T // tm,),
        [pl.BlockSpec((tm, Dm), lambda i: (i, 0)),
         pl.BlockSpec((Dm, Fd), lambda i: (0, 0), pipeline_mode=pl.Buffered(1)),
         pl.BlockSpec((Dm, Fd), lambda i: (0, 1), pipeline_mode=pl.Buffered(1))],
        [pl.BlockSpec((2, tm, Fd), lambda i: (0, i, 0)), pl.BlockSpec((tm, Fd), lambda i: (i, 0))],
        [jax.ShapeDtypeStruct((2, T, Fd), CDT), jax.ShapeDtypeStruct((T, Fd), CDT)],
        (h, wgu, wgu), ("parallel",), side)


def _ffn_down_bwd(dxo, wd, gu, name, side=None):
    T, Dm = dxo.shape
    Fd = wd.shape[0]
    tm = _tile(T, 256)

    def body(dx_ref, wd_ref, gu_ref, dgu_ref):
        da = 0.5 * _dot_nt(dx_ref[...].astype(CDT), wd_ref[...])
        g = gu_ref[0].astype(F32)
        u = gu_ref[1].astype(F32)
        sg = _sigmoid(g)
        dgu_ref[0] = (da * u * (sg * (1.0 + g * (1.0 - sg)))).astype(dgu_ref.dtype)
        dgu_ref[1] = (da * (g * sg)).astype(dgu_ref.dtype)

    gu_spec = pl.BlockSpec((2, tm, Fd), lambda i: (0, i, 0))
    return _pcall(
        body, name, (T // tm,),
        [pl.BlockSpec((tm, Dm), lambda i: (i, 0)),
         pl.BlockSpec((Fd, Dm), lambda i: (0, 0), pipeline_mode=pl.Buffered(1)), gu_spec],
        [gu_spec], [jax.ShapeDtypeStruct((2, T, Fd), CDT)],
        (dxo, wd, gu), ("parallel",), side)[0]


def _mm_nt_norm_bwd(a_parts, b, x, g, dres, name, side=None):
    T, Dm = x.shape
    tm = _tile(T, 256)

    def b_cols(b_ref, lo, wd):
        if b.ndim == 2:
            return [(0, wd, b_ref[:, lo:lo + wd])]
        kb = b.shape[2]
        return [(j * kb - lo, kb, b_ref[j]) for j in range(lo // kb, (lo + wd) // kb)]

    def body(*refs):
        a_refs, (b_ref, x_ref, g_ref, dres_ref, dx_ref, dg_ref) = refs[:len(a_parts)], refs[len(a_parts):]

        @pl.when(pl.program_id(0) == 0)
        def _():
            dg_ref[...] = jnp.zeros_like(dg_ref)

        dh, lo = None, 0
        for a_ref, part in zip(a_refs, a_parts):
            slabs = [a_ref] if part.ndim == 2 else [a_ref.at[s_] for s_ in range(part.shape[0])]
            for slab in slabs:
                for off, wd, bv in b_cols(b_ref, lo, part.shape[-1]):
                    term = _dot_nt(slab[:, off:off + wd].astype(CDT), bv)
                    dh = term if dh is None else dh + term
                lo += part.shape[-1]
        xv = x_ref[...]
        r = lax.rsqrt(jnp.mean(xv * xv, axis=-1, keepdims=True) + EPS)
        xh = xv * r
        dg_ref[...] += jnp.sum(dh * xh, axis=0, keepdims=True)
        dxh = dh * g_ref[...]
        dx_ref[...] = dres_ref[...] + r * (dxh - xh * jnp.mean(dxh * xh, axis=-1, keepdims=True))

    row = pl.BlockSpec((tm, Dm), lambda i: (i, 0))
    vec = pl.BlockSpec((1, Dm), lambda i: (0, 0))
    a_specs = [pl.BlockSpec((tm, p.shape[1]), lambda i: (i, 0)) if p.ndim == 2 else
               pl.BlockSpec((p.shape[0], tm, p.shape[2]), lambda i: (0, i, 0)) for p in a_parts]
    b_spec = pl.BlockSpec(b.shape, lambda i: (0,) * b.ndim, pipeline_mode=pl.Buffered(1))
    return _pcall(body, name, (T // tm,), a_specs + [b_spec, row, vec, row], [row, vec],
                  [jax.ShapeDtypeStruct((T, Dm), F32), jax.ShapeDtypeStruct((1, Dm), F32)],
                  (*a_parts, b, x, g.reshape(1, Dm), dres), ("arbitrary",), side)


def _mm_in(h, w_in, name, side=None):
    T, Dm = h.shape
    tm = _tile(T, 256)
    widths = (POOL_DIM, QK_DIM, KV_DIM, GATE_DIM)

    def body(h_ref, w_ref, *outs):
        z = _dot(h_ref[...], w_ref[...])
        lo = 0
        for o_ref, wd in zip(outs, widths):
            o_ref[...] = z[:, lo:lo + wd]
            lo += wd

    return _pcall(body, name, (T // tm,),
                  [pl.BlockSpec((tm, Dm), lambda i: (i, 0)), pl.BlockSpec(w_in.shape, lambda i: (0, 0))],
                  [pl.BlockSpec((tm, wd), lambda i: (i, 0)) for wd in widths],
                  [jax.ShapeDtypeStruct((T, wd), F32) for wd in widths], (h, w_in), ("parallel",), side)


def _window_mean_minus_token(ext, u, g, w, pos):
    sl = slice(g * GROUP, (g + 1) * GROUP)
    s = ext[:, sl]
    span = 1
    while span < w:
        s = s + pltpu.roll(s, span, axis=0)
        span *= 2
    cnt = jnp.minimum(pos + 1, w).astype(F32)
    return s[POOL_WMAX:, :] / cnt - u[:, sl]


def _pool_fwd(zu, pool_w, scale, name):
    T = zu.shape[0]
    tm = _tile(T, 512, POOL_WMAX)
    hb = tm // POOL_WMAX

    def body(u_ref, halo_ref, pw_ref, sc_ref, pm_ref):
        i = pl.program_id(0)
        u = u_ref[...]
        halo = jnp.where(i > 0, halo_ref[...], 0.0)
        ext = jnp.concatenate([halo, u], axis=0)
        pos = i * tm + lax.broadcasted_iota(jnp.int32, (tm, 1), 0)
        ys = []
        for g, w in enumerate(POOL_WINDOWS):
            d = _window_mean_minus_token(ext, u, g, w, pos)
            ys.append(_dot(d.astype(CDT), pw_ref[g]))
        pm_ref[...] = (jnp.concatenate(ys, axis=1) * sc_ref[...]).astype(pm_ref.dtype)

    row = pl.BlockSpec((tm, POOL_DIM), lambda i: (i, 0))
    return pl.pallas_call(
        body, name=name, grid=(T // tm,),
        in_specs=[row, pl.BlockSpec((POOL_WMAX, POOL_DIM), lambda i: (jnp.maximum(i * hb - 1, 0), 0)),
                  pl.BlockSpec(pool_w.shape, lambda i: (0, 0, 0)), pl.BlockSpec((1, POOL_DIM), lambda i: (0, 0))],
        out_specs=row, out_shape=jax.ShapeDtypeStruct((T, POOL_DIM), CDT),
        compiler_params=_params("parallel"),
    )(zu, zu, pool_w, scale.reshape(1, POOL_DIM))


def _pool_bwd(zu, dpm, pool_w, scale, name):
    T = zu.shape[0]
    tm = _tile(T, 512, POOL_WMAX)
    hb = tm // POOL_WMAX
    nsteps = T // tm
    ext_rows = tm + POOL_WMAX

    def body(u_ref, halo_ref, dpm_ref, dnext_ref, pw_ref, sc_ref, du_ref, dpw_ref, dsc_ref):
        i = pl.program_id(0)

        @pl.when(i == 0)
        def _():
            dpw_ref[...] = jnp.zeros_like(dpw_ref)
            dsc_ref[...] = jnp.zeros_like(dsc_ref)

        u = u_ref[...]
        halo = jnp.where(i > 0, halo_ref[...], 0.0)
        ext = jnp.concatenate([halo, u], axis=0)
        dpm_t = dpm_ref[...].astype(F32)
        dnext = jnp.where(i < nsteps - 1, dnext_ref[...].astype(F32), 0.0)
        dext = jnp.concatenate([dpm_t, dnext], axis=0)
        sc = sc_ref[...]
        pos = i * tm + lax.broadcasted_iota(jnp.int32, (tm, 1), 0)
        pos_ext = i * tm + lax.broadcasted_iota(jnp.int32, (ext_rows, 1), 0)
        dus, dscs = [], []
        for g, w in enumerate(POOL_WINDOWS):
            sl = slice(g * GROUP, (g + 1) * GROUP)
            dc = _window_mean_minus_token(ext, u, g, w, pos).astype(CDT)
            y = _dot(dc, pw_ref[g])
            dscs.append(jnp.sum(dpm_t[:, sl] * y, axis=0, keepdims=True))
            dy_ext = (dext[:, sl] * sc[:, sl]).astype(CDT)
            dpw_ref[g] += _dot_tn(dc, dy_ext[:tm])
            dd = _dot_nt(dy_ext, pw_ref[g])
            r = dd / jnp.minimum(pos_ext + 1, w).astype(F32)
            span = 1
            while span < w:
                r = r + pltpu.roll(r, ext_rows - span, axis=0)
                span *= 2
            dus.append(r[:tm] - dd[:tm])
        du_ref[...] = jnp.concatenate(dus, axis=1).astype(du_ref.dtype)
        dsc_ref[...] += jnp.concatenate(dscs, axis=1)

    row = pl.BlockSpec((tm, POOL_DIM), lambda i: (i, 0))
    prev = pl.BlockSpec((POOL_WMAX, POOL_DIM), lambda i: (jnp.maximum(i * hb - 1, 0), 0))
    nxt = pl.BlockSpec((POOL_WMAX, POOL_DIM), lambda i: (jnp.minimum((i + 1) * hb, nsteps * hb - 1), 0))
    return pl.pallas_call(
        body, name=name, grid=(nsteps,),
        in_specs=[row, prev, row, nxt, pl.BlockSpec(pool_w.shape, lambda i: (0, 0, 0)),
                  pl.BlockSpec((1, POOL_DIM), lambda i: (0, 0))],
        out_specs=[row, pl.BlockSpec(pool_w.shape, lambda i: (0, 0, 0)), pl.BlockSpec((1, POOL_DIM), lambda i: (0, 0))],
        out_shape=[jax.ShapeDtypeStruct((T, POOL_DIM), CDT), jax.ShapeDtypeStruct(pool_w.shape, F32),
                   jax.ShapeDtypeStruct((1, POOL_DIM), F32)],
        compiler_params=_params("arbitrary"),
    )(zu, zu, dpm, dpm, pool_w, scale.reshape(1, POOL_DIM))


def _rope_tables(T):
    pos = jnp.arange(T, dtype=F32)
    inv_freq = ROPE_THETA ** (-jnp.arange(0, ROT_DIM, 2, dtype=F32) / ROT_DIM)
    ang = pos[:, None] * inv_freq[None, :]
    cos, sin = jnp.cos(ang), jnp.sin(ang)
    rest = HEAD_DIM - ROT_DIM
    cos_h = jnp.concatenate([cos, cos, jnp.ones((T, rest), F32)], axis=1)
    sin_h = jnp.concatenate([-sin, sin, jnp.zeros((T, rest), F32)], axis=1)
    return jnp.tile(cos_h, (1, 2)), jnp.tile(sin_h, (1, 2))


def _lane_masks():
    lane = lax.broadcasted_iota(jnp.int32, (1, LANES), 1)
    in_head = lane % HEAD_DIM
    return lane < HEAD_DIM, in_head < ROT_DIM // 2


def _rope_partner(v, low):
    lane = lax.broadcasted_iota(jnp.int32, (1, LANES), 1)
    swapped = jnp.where(low, pltpu.roll(v, LANES - ROT_DIM // 2, axis=1), pltpu.roll(v, ROT_DIM // 2, axis=1))
    return jnp.where(lane % HEAD_DIM < ROT_DIM, swapped, 0.0)


def _head_mean(v, first):
    lo = jnp.sum(jnp.where(first, v, 0.0), axis=-1, keepdims=True)
    hi = jnp.sum(jnp.where(first, 0.0, v), axis=-1, keepdims=True)
    return jnp.where(first, lo, hi) * (1.0 / HEAD_DIM)


def _qk_fwd(zqk, gqk, cos_t, sin_t, name):
    T = zqk.shape[0]
    tm = _tile(T, 512)

    def body(z_ref, g_ref, c_ref, s_ref, o_ref):
        first, low = _lane_masks()
        cosv, sinv = c_ref[...], s_ref[...]
        for c in range(QK_DIM // LANES):
            sl = slice(c * LANES, (c + 1) * LANES)
            xv = z_ref[:, sl]
            r = lax.rsqrt(_head_mean(xv * xv, first) + EPS)
            xn = xv * r * g_ref[:, sl]
            o_ref[:, sl] = (xn * cosv + _rope_partner(xn, low) * sinv).astype(o_ref.dtype)

    row = pl.BlockSpec((tm, QK_DIM), lambda i: (i, 0))
    tab = pl.BlockSpec((tm, LANES), lambda i: (i, 0))
    return pl.pallas_call(
        body, name=name, grid=(T // tm,),
        in_specs=[row, pl.BlockSpec((1, QK_DIM), lambda i: (0, 0)), tab, tab], out_specs=row,
        out_shape=jax.ShapeDtypeStruct((T, QK_DIM), CDT), compiler_params=_params("parallel"),
    )(zqk, gqk, cos_t, sin_t)


def _qk_bwd(dqk, zqk, gqk, cos_t, sin_t, name):
    T = zqk.shape[0]
    tm = _tile(T, 512)

    def body(d_ref, z_ref, g_ref, c_ref, s_ref, dz_ref, dg_ref):
        @pl.when(pl.program_id(0) == 0)
        def _():
            dg_ref[...] = jnp.zeros_like(dg_ref)

        first, low = _lane_masks()
        cosv, sinv = c_ref[...], s_ref[...]
        dgs = []
        for c in range(QK_DIM // LANES):
            sl = slice(c * LANES, (c + 1) * LANES)
            dout = d_ref[:, sl]
            dxn = dout * cosv + _rope_partner(dout * sinv, low)
            xv = z_ref[:, sl]
            r = lax.rsqrt(_head_mean(xv * xv, first) + EPS)
            xh = xv * r
            dgs.append(jnp.sum(dxn * xh, axis=0, keepdims=True))
            dxh = dxn * g_ref[:, sl]
            dz_ref[:, sl] = (r * (dxh - xh * _head_mean(dxh * xh, first))).astype(dz_ref.dtype)
        dg_ref[...] += jnp.concatenate(dgs, axis=1)

    row = pl.BlockSpec((tm, QK_DIM), lambda i: (i, 0))
    tab = pl.BlockSpec((tm, LANES), lambda i: (i, 0))
    vec = pl.BlockSpec((1, QK_DIM), lambda i: (0, 0))
    return pl.pallas_call(
        body, name=name, grid=(T // tm,),
        in_specs=[row, row, vec, tab, tab], out_specs=[row, vec],
        out_shape=[jax.ShapeDtypeStruct((T, QK_DIM), CDT), jax.ShapeDtypeStruct((1, QK_DIM), F32)],
        compiler_params=_params("arbitrary"),
    )(dqk, zqk, gqk, cos_t, sin_t)


def _dup_half(v, first, kv):
    swapped = pltpu.roll(v, HEAD_DIM, axis=1)
    return jnp.where(first, v, swapped) if kv == 0 else jnp.where(first, swapped, v)


HEADS_PER_KV = 4
HEAD_STACK = 1


def _attn_bias():
    qi = lax.broadcasted_iota(jnp.int32, (HEAD_STACK * BLOCK, 2 * BLOCK), 0) % BLOCK
    ki = lax.broadcasted_iota(jnp.int32, (HEAD_STACK * BLOCK, 2 * BLOCK), 1)
    diff = qi + BLOCK - ki
    band = (diff >= 0) & (diff < BLOCK)
    return jnp.stack([jnp.where(band, 0.0, -jnp.inf), jnp.where(band & (ki >= BLOCK), 0.0, -jnp.inf)]).astype(F32)


def _attn_blocks(T):
    return _tile(T // BLOCK, 4, 1)


def _stack_heads(ref, rows, kv, heads, first):
    parts = []
    for h in heads:
        c = 2 * kv + h // 2
        v = ref[rows, c * LANES:(c + 1) * LANES].astype(CDT)
        zero = jnp.zeros_like(v)
        parts.append(jnp.where(first, v, zero) if h % 2 == 0 else jnp.where(first, zero, v))
    return parts[0] if len(parts) == 1 else jnp.concatenate(parts, axis=0)


def _row_blocks(v, n):
    return [v[b * BLOCK:(b + 1) * BLOCK] for b in range(n)]


def _sink_column(sink_ref, kv, heads):
    cols = [jnp.full((BLOCK, 1), sink_ref[HEADS_PER_KV * kv + h], F32) for h in heads]
    return cols[0] if len(cols) == 1 else jnp.concatenate(cols, axis=0)


def _head_groups():
    return [tuple(range(g, g + HEAD_STACK)) for g in range(0, HEADS_PER_KV, HEAD_STACK)]


def _softmax_with_sink(qst, kdup, sinkcol, bias):
    s = _dot_nt(qst, kdup) * ATTN_SCALE + bias
    m = jnp.maximum(jnp.max(s, axis=-1, keepdims=True), sinkcol)
    pu = jnp.exp(s - m)
    denom = jnp.sum(pu, axis=-1, keepdims=True) + jnp.exp(sinkcol - m)
    return pu * (1.0 / denom), m + jnp.log(denom)


def _attn_fwd(qkn, zv, sinks, name, side=None):
    T = qkn.shape[0]
    R = _attn_blocks(T)
    tq = R * BLOCK

    def body(sink_ref, bias_ref, qk_ref, qkp_ref, v_ref, vp_ref, o_ref, lse_ref):
        i = pl.program_id(0)
        first, _ = _lane_masks()
        lane = lax.broadcasted_iota(jnp.int32, (1, LANES), 1)
        kall = jnp.concatenate([qkp_ref[:, ATTN_DIM:], qk_ref[:, ATTN_DIM:]], axis=0)
        vall = jnp.concatenate([vp_ref[...], v_ref[...]], axis=0).astype(CDT)
        for r in range(R):
            bias = bias_ref[jnp.where(i == 0, 1, 0)] if r == 0 else bias_ref[0]
            rows = slice(r * BLOCK, (r + 2) * BLOCK)
            qrows = slice(r * BLOCK, (r + 1) * BLOCK)
            lse_rows = jnp.zeros((BLOCK, LANES), F32)
            for kv in range(2):
                kdup = _dup_half(kall[rows], first, kv)
                vdup = _dup_half(vall[rows], first, kv)
                res = []
                for heads in _head_groups():
                    p, lse = _softmax_with_sink(_stack_heads(qk_ref, qrows, kv, heads, first), kdup,
                                                _sink_column(sink_ref, kv, heads), bias)
                    res += _row_blocks(_dot(p.astype(CDT), vdup), len(heads))
                    for b, col in enumerate(_row_blocks(lse, len(heads))):
                        lse_rows = jnp.where(lane == HEADS_PER_KV * kv + heads[b], col, lse_rows)
                o_ref[qrows, 2 * kv * LANES:(2 * kv + 1) * LANES] = jnp.where(first, res[0], res[1]).astype(o_ref.dtype)
                o_ref[qrows, (2 * kv + 1) * LANES:(2 * kv + 2) * LANES] = jnp.where(first, res[2], res[3]).astype(o_ref.dtype)
            lse_ref[qrows, :] = lse_rows

    bias = _attn_bias()
    prev = lambda i: (jnp.maximum(i * R - 1, 0), 0)
    return _pcall(
        body, name, (T // tq,),
        [pl.BlockSpec(memory_space=pltpu.SMEM), pl.BlockSpec(bias.shape, lambda i: (0, 0, 0)),
         pl.BlockSpec((tq, QK_DIM), lambda i: (i, 0)), pl.BlockSpec((BLOCK, QK_DIM), prev),
         pl.BlockSpec((tq, KV_DIM), lambda i: (i, 0)), pl.BlockSpec((BLOCK, KV_DIM), prev)],
        [pl.BlockSpec((tq, ATTN_DIM), lambda i: (i, 0)), pl.BlockSpec((tq, LANES), lambda i: (i, 0))],
        [jax.ShapeDtypeStruct((T, ATTN_DIM), CDT), jax.ShapeDtypeStruct((T, LANES), F32)],
        (sinks, bias, qkn, qkn, zv, zv), ("parallel",), side)


def _attn_bwd(qkn, zv, sinks, do, o, lse, name, side=None):
    T = qkn.shape[0]
    R = _attn_blocks(T)
    tq = R * BLOCK

    def body(sink_ref, bias_ref, qk_ref, qkp_ref, v_ref, vp_ref, do_ref, o_ref, lse_ref,
             dq_ref, dkc_ref, dkp_ref, dvc_ref, dvp_ref, ds_ref):
        i = pl.program_id(0)

        @pl.when(i == 0)
        def _():
            ds_ref[...] = jnp.zeros_like(ds_ref)

        first, _ = _lane_masks()
        lane = lax.broadcasted_iota(jnp.int32, (1, LANES), 1)
        kall = jnp.concatenate([qkp_ref[:, ATTN_DIM:], qk_ref[:, ATTN_DIM:]], axis=0)
        vall = jnp.concatenate([vp_ref[...], v_ref[...]], axis=0).astype(CDT)
        for r in range(R):
            bias = bias_ref[jnp.where(i == 0, 1, 0)] if r == 0 else bias_ref[0]
            rows = slice(r * BLOCK, (r + 2) * BLOCK)
            qrows = slice(r * BLOCK, (r + 1) * BLOCK)
            dk_out, dv_out = [], []
            lse_rows = lse_ref[qrows, :]
            for kv in range(2):
                kdup = _dup_half(kall[rows], first, kv)
                vdup = _dup_half(vall[rows], first, kv)
                dq_h = []
                dk_acc = jnp.zeros((2 * BLOCK, LANES), F32)
                dv_acc = jnp.zeros((2 * BLOCK, LANES), F32)
                for heads in _head_groups():
                    qst = _stack_heads(qk_ref, qrows, kv, heads, first)
                    dost = _stack_heads(do_ref, qrows, kv, heads, first)
                    lse_cols, delta_cols = [], []
                    for h in heads:
                        cols = slice((2 * kv + h // 2) * LANES, (2 * kv + h // 2 + 1) * LANES)
                        prod = do_ref[qrows, cols].astype(F32) * o_ref[qrows, cols].astype(F32)
                        own = first if h % 2 == 0 else jnp.logical_not(first)
                        delta_cols.append(jnp.sum(jnp.where(own, prod, 0.0), axis=-1, keepdims=True))
                        lse_cols.append(jnp.sum(jnp.where(lane == HEADS_PER_KV * kv + h, lse_rows, 0.0), axis=-1, keepdims=True))
                    lse_col = lse_cols[0] if len(heads) == 1 else jnp.concatenate(lse_cols, axis=0)
                    delta = delta_cols[0] if len(heads) == 1 else jnp.concatenate(delta_cols, axis=0)
                    p = jnp.exp(_dot_nt(qst, kdup) * ATTN_SCALE + bias - lse_col)
                    dsc = (p * (_dot_nt(dost, vdup) - delta)).astype(CDT)
                    psink = jnp.exp(_sink_column(sink_ref, kv, heads) - lse_col)
                    for b, term in enumerate(_row_blocks(psink * delta, len(heads))):
                        row = HEADS_PER_KV * kv + heads[b]
                        ds_ref[row:row + 1, :] += jnp.sum(term, axis=0, keepdims=True)
                    dq_h += _row_blocks(_dot(dsc, kdup) * ATTN_SCALE, len(heads))
                    dk_acc = dk_acc + _dot_tn(dsc, qst) * ATTN_SCALE
                    dv_acc = dv_acc + _dot_tn(p.astype(CDT), dost)
                dq_ref[qrows, 2 * kv * LANES:(2 * kv + 1) * LANES] = jnp.where(first, dq_h[0], dq_h[1])
                dq_ref[qrows, (2 * kv + 1) * LANES:(2 * kv + 2) * LANES] = jnp.where(first, dq_h[2], dq_h[3])
                dk_out.append(dk_acc + pltpu.roll(dk_acc, HEAD_DIM, axis=1))
                dv_out.append(dv_acc + pltpu.roll(dv_acc, HEAD_DIM, axis=1))
            dk = jnp.where(first, dk_out[0], dk_out[1])
            dv = jnp.where(first, dv_out[0], dv_out[1])
            dkp_ref[qrows, :] = dk[:BLOCK]
            dkc_ref[qrows, :] = dk[BLOCK:]
            dvp_ref[qrows, :] = dv[:BLOCK]
            dvc_ref[qrows, :] = dv[BLOCK:]

    bias = _attn_bias()
    prev = lambda i: (jnp.maximum(i * R - 1, 0), 0)
    kvrow = pl.BlockSpec((tq, KV_DIM), lambda i: (i, 0))
    qrow = pl.BlockSpec((tq, ATTN_DIM), lambda i: (i, 0))
    kv_shape = jax.ShapeDtypeStruct((T, KV_DIM), F32)
    return _pcall(
        body, name, (T // tq,),
        [pl.BlockSpec(memory_space=pltpu.SMEM), pl.BlockSpec(bias.shape, lambda i: (0, 0, 0)),
         pl.BlockSpec((tq, QK_DIM), lambda i: (i, 0)), pl.BlockSpec((BLOCK, QK_DIM), prev),
         kvrow, pl.BlockSpec((BLOCK, KV_DIM), prev), qrow, qrow, kvrow],
        [qrow, kvrow, kvrow, kvrow, kvrow, pl.BlockSpec((N_Q_HEADS, LANES), lambda i: (0, 0))],
        [jax.ShapeDtypeStruct((T, ATTN_DIM), F32), kv_shape, kv_shape, kv_shape, kv_shape,
         jax.ShapeDtypeStruct((N_Q_HEADS, LANES), F32)],
        (sinks, bias, qkn, qkn, zv, zv, do, o, lse), ("arbitrary",), side)


def _merge_fwd(pm, o, w_pb, w_ab, zg, name, side=None):
    T = pm.shape[0]
    tm = _tile(T, 512)

    def body(pm_ref, o_ref, wp_ref, wa_ref, zg_ref, a_ref, b_ref, m_ref):
        pmv, ov = pm_ref[...], o_ref[...]
        a = jnp.concatenate([_dot(pmv, wp_ref[j]) for j in range(N_CHIPS)], axis=1)
        b = jnp.concatenate([_dot(ov, wa_ref[j]) for j in range(N_CHIPS)], axis=1)
        gp = _sigmoid(zg_ref[:, :D_MODEL])
        ga = _sigmoid(zg_ref[:, D_MODEL:])
        a_ref[...] = a.astype(a_ref.dtype)
        b_ref[...] = b.astype(b_ref.dtype)
        m_ref[...] = (gp * a + ga * b).astype(m_ref.dtype)

    half = pl.BlockSpec((tm, POOL_DIM), lambda i: (i, 0))
    full = pl.BlockSpec((tm, D_MODEL), lambda i: (i, 0))
    wspec = pl.BlockSpec(w_pb.shape, lambda i: (0, 0, 0))
    out = jax.ShapeDtypeStruct((T, D_MODEL), CDT)
    return _pcall(body, name, (T // tm,), [half, half, wspec, wspec, pl.BlockSpec((tm, GATE_DIM), lambda i: (i, 0))],
                  [full, full, full], [out, out, out], (pm, o, w_pb, w_ab, zg), ("parallel",), side)


def _merge_bwd(dxo, w_out, a, b, zg, name):
    T = dxo.shape[0]
    tm = _tile(T, 512)

    def body(dx_ref, w_ref, a_ref, b_ref, zg_ref, da_ref, db_ref, dg_ref):
        dm = _dot_nt(dx_ref[...].astype(CDT), w_ref[...])
        gp = _sigmoid(zg_ref[:, :D_MODEL])
        ga = _sigmoid(zg_ref[:, D_MODEL:])
        da_ref[...] = (dm * gp).astype(da_ref.dtype)
        db_ref[...] = (dm * ga).astype(db_ref.dtype)
        dg_ref[:, :D_MODEL] = (dm * a_ref[...].astype(F32) * (gp * (1.0 - gp))).astype(dg_ref.dtype)
        dg_ref[:, D_MODEL:] = (dm * b_ref[...].astype(F32) * (ga * (1.0 - ga))).astype(dg_ref.dtype)

    full = pl.BlockSpec((tm, D_MODEL), lambda i: (i, 0))
    gate = pl.BlockSpec((tm, GATE_DIM), lambda i: (i, 0))
    out = jax.ShapeDtypeStruct((T, D_MODEL), CDT)
    return pl.pallas_call(
        body, name=name, grid=(T // tm,),
        in_specs=[full, pl.BlockSpec((D_MODEL, D_MODEL), lambda i: (0, 0)), full, full, gate],
        out_specs=[full, full, gate], out_shape=[out, out, jax.ShapeDtypeStruct((T, GATE_DIM), CDT)],
        compiler_params=_params("parallel"),
    )(dxo, w_out, a, b, zg)


def _adamw(w, g, m, v, name):
    Rr, C = w.shape
    tr = _tile(Rr, max(8, (1 << 19) // C // 8 * 8))

    def body(w_ref, g_ref, m_ref, v_ref, d_ref, nm_ref, nv_ref):
        gv = g_ref[...]
        nm = ADAM_B1 * m_ref[...] + (1.0 - ADAM_B1) * gv
        nv = ADAM_B2 * v_ref[...] + (1.0 - ADAM_B2) * (gv * gv)
        m_hat = nm / (1.0 - ADAM_B1 ** ADAM_STEP)
        v_hat = nv / (1.0 - ADAM_B2 ** ADAM_STEP)
        d_ref[...] = -ADAM_LR * (m_hat / (jnp.sqrt(v_hat) + ADAM_EPS) + ADAM_WD * w_ref[...])
        nm_ref[...] = nm
        nv_ref[...] = nv

    blk = pl.BlockSpec((tr, C), lambda i: (i, 0))
    out = jax.ShapeDtypeStruct((Rr, C), F32)
    return pl.pallas_call(
        body, name=name, grid=(Rr // tr,), in_specs=[blk] * 4, out_specs=[blk] * 3, out_shape=[out] * 3,
        compiler_params=_params("parallel"),
    )(w, g, m, v)


def _place():
    return lax.axis_index("x"), lax.axis_index("y"), lax.axis_index("c")


def _other_chip(x, y, d):
    return (1 - x if d & 2 else x), (1 - y if d & 1 else y)


def _rcopy(src, dst, ssem, rsem, dev):
    return pltpu.make_async_remote_copy(src_ref=src, dst_ref=dst, send_sem=ssem, recv_sem=rsem, device_id=dev,
                                        device_id_type=MESH)


def _row_half(rows, c):
    return pl.ds(c * (rows // 2), rows // 2)


def _is_wide(name):
    return name in WIDE


def _block(ref, wide, j, rows, n):
    if wide:
        return ref.at[rows, pl.ds(pl.multiple_of(j * n, LANES), n)]
    return ref.at[j, rows]


def _gathered_shape(shard, wide):
    _, a, n = shard.shape
    return jax.ShapeDtypeStruct((a, N_CHIPS * n) if wide else (N_CHIPS, a, n), shard.dtype)


def _gather_ici_side(shards, wides, l):
    k_of = lambda w, d: 3 * w + d - 1

    def issue(ins, outs, ssem, rsem):
        x, y, c = _place()
        cps = []
        for w, (shard, wide) in enumerate(zip(shards, wides)):
            _, a, n = shard.shape
            half = _row_half(a, c)
            for d in (1, 2, 3):
                px, py = _other_chip(x, y, d)
                cps.append(_rcopy(ins[w].at[l, half], _block(outs[w], wide, 2 * x + y, half, n),
                                  ssem.at[k_of(w, d)], rsem.at[k_of(w, d)], (px, py, c)))
        return cps

    return _Side(shards, [_gathered_shape(s_, wd) for s_, wd in zip(shards, wides)], 3 * len(shards), issue)


def _gather_d2d_side(shards, wides, gathered, l):
    nw = len(shards)

    def issue(ins, outs, ssem, rsem):
        x, y, c = _place()
        sibling = (x, y, 1 - c)
        cps = []
        for w, (shard, wide) in enumerate(zip(shards, wides)):
            _, a, n = shard.shape
            half = _row_half(a, c)
            for d in (1, 2, 3):
                px, py = _other_chip(x, y, d)
                k = 3 * w + d - 1
                got = _block(outs[w], wide, 2 * px + py, half, n)
                cps.append(_rcopy(got, got, ssem.at[k], rsem.at[k], sibling))
            cps.append(_rcopy(ins[nw + w].at[l], _block(outs[w], wide, 2 * x + y, pl.ds(0, a), n),
                              ssem.at[3 * nw + w], rsem.at[3 * nw + w], sibling))
        return cps

    return _Side(list(gathered) + list(shards), [jax.ShapeDtypeStruct(g.shape, g.dtype) for g in gathered], 4 * nw, issue,
                 aliases={w: w for w in range(nw)})


def _half_shape(g, wide):
    if wide:
        return jax.ShapeDtypeStruct((g.shape[0] // 2, g.shape[1]), g.dtype)
    return jax.ShapeDtypeStruct((N_CHIPS, g.shape[1] // 2, g.shape[2]), g.dtype)


def _reduce_sibling_side(gms, wides):
    def issue(ins, outs, ssem, rsem):
        x, y, c = _place()
        cps = []
        for w, (g, wide) in enumerate(zip(gms, wides)):
            src = ins[w].at[_row_half(g.shape[0], 1 - c)] if wide else ins[w].at[:, _row_half(g.shape[1], 1 - c)]
            cps.append(_rcopy(src, outs[w], ssem.at[w], rsem.at[w], (x, y, 1 - c)))
        return cps

    return _Side(gms, [_half_shape(g, wd) for g, wd in zip(gms, wides)], len(gms), issue)


def _reduce_chip_side(ps, wides):
    def slot_shape(p, wide):
        return jax.ShapeDtypeStruct((N_CHIPS, p.shape[0], p.shape[1] // N_CHIPS) if wide else p.shape, p.dtype)

    def issue(ins, outs, ssem, rsem):
        x, y, c = _place()
        cps = []
        for w, (p, wide) in enumerate(zip(ps, wides)):
            ah, n = (p.shape[0], p.shape[1] // N_CHIPS) if wide else p.shape[1:]
            for d in (1, 2, 3):
                px, py = _other_chip(x, y, d)
                k = 3 * w + d - 1
                cps.append(_rcopy(_block(ins[w], wide, 2 * px + py, pl.ds(0, ah), n), outs[w].at[2 * x + y],
                                  ssem.at[k], rsem.at[k], (px, py, c)))
        return cps

    return _Side(ps, [slot_shape(p, wd) for p, wd in zip(ps, wides)], 3 * len(ps), issue)


def _share_side(accs):
    n = len(accs)

    def issue(ins, outs, ssem, rsem):
        x, y, c = _place()
        cps = []
        for w in range(n):
            mine = outs[w].at[:, _row_half(accs[w].shape[1], c)]
            cps.append(_rcopy(mine, mine, ssem.at[w], rsem.at[w], (x, y, 1 - c)))
        return cps

    return _Side(accs, [jax.ShapeDtypeStruct(a.shape, a.dtype) for a in accs], n, issue, aliases={w: w for w in range(n)})


def _sum_rows(rows, b):
    return _tile(rows, max(16, (1 << 19) // b // 16 * 16), 16)


def _pair_sum(g, recv, wide, place, name):
    ah, b = recv.shape[-2:]
    ta = _sum_rows(ah, b)
    nr = ah // ta

    def body(p_ref, g_ref, r_ref, o_ref):
        o_ref[...] = (g_ref[...].astype(F32) + r_ref[...].astype(F32)).astype(o_ref.dtype)

    if wide:
        grid = (nr,)
        specs = [pl.BlockSpec((ta, b), lambda r, p: (p[0] * nr + r, 0)), pl.BlockSpec((ta, b), lambda r, p: (r, 0))]
        out_spec = pl.BlockSpec((ta, b), lambda r, p: (r, 0))
    else:
        grid = (N_CHIPS, nr)
        specs = [pl.BlockSpec((None, ta, b), lambda j, r, p: (j, p[0] * nr + r, 0)),
                 pl.BlockSpec((None, ta, b), lambda j, r, p: (j, r, 0))]
        out_spec = pl.BlockSpec((None, ta, b), lambda j, r, p: (j, r, 0))
    return pl.pallas_call(
        body, name=name,
        grid_spec=pltpu.PrefetchScalarGridSpec(num_scalar_prefetch=1, grid=grid, in_specs=specs, out_specs=out_spec),
        out_shape=jax.ShapeDtypeStruct(recv.shape, recv.dtype), compiler_params=_params(*["parallel"] * len(grid)),
    )(place, g, recv)


def _chip_sum(slots, part, wide, place, acc, l, name):
    _, ah, b = slots.shape
    ta = _sum_rows(ah, b)
    nr = ah // ta

    def body(p_ref, s_ref, own_ref, acc_ref, o_ref):
        j = p_ref[1]
        own = own_ref[...].astype(F32)
        term = [jnp.where(j == s_, own, s_ref[s_].astype(F32)) for s_ in range(N_CHIPS)]
        o_ref[...] = ((term[0] + term[1]) + term[2]) + term[3]

    own_spec = (pl.BlockSpec((ta, b), lambda r, p: (r, p[1])) if wide else
                pl.BlockSpec((None, ta, b), lambda r, p: (p[1], r, 0)))
    return pl.pallas_call(
        body, name=name,
        grid_spec=pltpu.PrefetchScalarGridSpec(
            num_scalar_prefetch=1, grid=(nr,),
            in_specs=[pl.BlockSpec((N_CHIPS, ta, b), lambda r, p: (0, r, 0)), own_spec, ANY],
            out_specs=pl.BlockSpec((None, ta, b), lambda r, p: (l, p[0] * nr + r, 0))),
        out_shape=jax.ShapeDtypeStruct(acc.shape, F32), input_output_aliases={3: 0},
        compiler_params=_params("parallel"),
    )(place, slots, part, acc)


def _all_reduce_small(v):
    Rr = v.shape[0]

    def body(v_ref, slots_ref, out_ref, ssem, rsem):
        x, y, c = _place()
        me = 4 * x + 2 * y + c
        slots_ref[pl.ds(me, 1)] = v_ref[...][None]
        cps = []
        for d in range(1, N_DEV):
            px, py = _other_chip(x, y, d >> 1)
            pc = 1 - c if d & 1 else c
            cps.append(_rcopy(v_ref, slots_ref.at[me], ssem.at[d - 1], rsem.at[d - 1], (px, py, pc)))
            cps[-1].start()
        for cp in cps:
            cp.wait_recv()
        for cp in cps:
            cp.wait_send()
        acc = slots_ref[0]
        for s in range(1, N_DEV):
            acc = acc + slots_ref[s]
        out_ref[...] = acc

    vm = pl.BlockSpec(memory_space=pltpu.VMEM)
    return pl.pallas_call(
        body, name="all_reduce_small", in_specs=[vm], out_specs=[vm, vm],
        out_shape=[jax.ShapeDtypeStruct((N_DEV, Rr, LANES), F32), jax.ShapeDtypeStruct((Rr, LANES), F32)],
        scratch_shapes=[pltpu.SemaphoreType.DMA((N_DEV - 1,)), pltpu.SemaphoreType.DMA((N_DEV - 1,))],
        compiler_params=pltpu.CompilerParams(vmem_limit_bytes=VMEM_LIMIT_BYTES),
    )(v)[1]


def _ffn_forward(x, p, tag, side_of):
    h = _norm_fwd(x, p[f"ln_{tag}"], f"{tag}_norm")
    gu, act = _ffn_up(h, p[f"w_{tag}_gu"], f"{tag}_up", side_of(f"{tag}_up"))
    x_out = _mm_nn(act, p[f"w_{tag}_down"], f"{tag}_down", F32, res=x, scale=0.5, side=side_of(f"{tag}_down"))
    return x_out, (x, h, gu, act)


def _row_blocks_of(dw):
    return dw.reshape(N_CHIPS, dw.shape[0] // N_CHIPS, dw.shape[1])


def _ffn_backward(dxo, saved, p, tag, side_of, grad):
    x, h, gu, act = saved
    dgu = _ffn_down_bwd(dxo, p[f"w_{tag}_down"], gu, f"{tag}_down_bwd", side_of(f"{tag}_down_bwd"))
    grad(f"w_{tag}_down", _row_blocks_of(_mm_tn(act, dxo, f"{tag}_dwd", scale=0.5)))
    grad(f"w_{tag}_gu", _mm_tn(h, dgu, f"{tag}_dwgu", tn_target=2816, tm_target=1024, side=side_of(f"{tag}_dwgu")))
    dx, d_ln = _mm_nt_norm_bwd([dgu], p[f"w_{tag}_gu"], x, p[f"ln_{tag}"], dxo, f"{tag}_dh_norm_bwd",
                               side_of(f"{tag}_dh_norm_bwd"))
    grad(f"ln_{tag}", d_ln[0])
    return dx


def _mixer_forward(x, p, tabs, side_of):
    h = _norm_fwd(x, p["ln_mix"], "mix_norm")
    zu, zqk, zv, zg = _mm_in(h, p["w_in"], "mix_in", side_of("mix_in"))
    pm = _pool_fwd(zu, p["pool_w"], p["pool_scale"], "pool_fwd")
    qkn = _qk_fwd(zqk, p["gqk"], *tabs, "qk_fwd")
    o, lse = _attn_fwd(qkn, zv, p["sinks"], "attn_fwd", side_of("attn_fwd"))
    a, b, m = _merge_fwd(pm, o, p["w_pool_branch"], p["w_attn_branch"], zg, "merge_fwd", side_of("merge_fwd"))
    x_out = _mm_nn(m, p["w_out"], "mix_out", F32, res=x, scale=1.0)
    return x_out, (x, h, zu, zqk, zv, zg, pm, qkn, o, lse, a, b, m)


def _shift_up(v):
    return jnp.concatenate([v[BLOCK:], jnp.zeros((BLOCK, v.shape[1]), v.dtype)], axis=0)


def _mixer_backward(dxo, saved, p, tabs, side_of, grad):
    x, h, zu, zqk, zv, zg, pm, qkn, o, lse, a, b, m = saved
    d_a, d_b, dgl = _merge_bwd(dxo, p["w_out"], a, b, zg, "merge_bwd")
    grad("w_out", _row_blocks_of(_mm_tn(m, dxo, "mix_dwout")))
    dpm = _mm_nt_blocks(d_a, p["w_pool_branch"], "pool_branch_dx", CDT)
    grad("w_pool_branch", _mm_tn(pm, d_a, "pool_branch_dw", col_blocks=N_CHIPS))
    do = _mm_nt_blocks(d_b, p["w_attn_branch"], "attn_branch_dx", CDT)
    grad("w_attn_branch", _mm_tn(o, d_b, "attn_branch_dw", col_blocks=N_CHIPS))
    du, d_pool_w, d_pool_scale = _pool_bwd(zu, dpm, p["pool_w"], p["pool_scale"], "pool_bwd")
    grad("pool_w", d_pool_w)
    grad("pool_scale", d_pool_scale)
    dq, dkc, dkp, dvc, dvp, dsink = _attn_bwd(qkn, zv, p["sinks"], do, o, lse, "attn_bwd", side_of("attn_bwd"))
    dqk = jnp.concatenate([dq, dkc + _shift_up(dkp)], axis=1)
    dv = dvc + _shift_up(dvp)
    dzqk, dgqk = _qk_bwd(dqk, zqk, p["gqk"], *tabs, "qk_bwd")
    grad("q_norm", dgqk[0, :ATTN_DIM].reshape(N_Q_HEADS, HEAD_DIM).sum(axis=0))
    grad("k_norm", dgqk[0, ATTN_DIM:].reshape(KV_DIM // HEAD_DIM, HEAD_DIM).sum(axis=0))
    grad("sinks", -dsink[:, 0])
    dz = [du, dzqk, dv, dgl]
    grad("w_in", _blocks_from_full("w_in", _mm_tn_parts(h, dz, "mix_dwin")).astype(WIRE_DT))
    dx, d_ln = _mm_nt_norm_bwd(dz, p["w_in"], x, p["ln_mix"], dxo, "mix_dh_norm_bwd", side_of("mix_dh_norm_bwd"))
    grad("ln_mix", d_ln[0])
    return dx


class _NoComm:
    def __init__(self, layers):
        self.layers, self.grads = layers, [dict() for _ in layers]

    def weight(self, l, name):
        return self.layers[l][name]

    def side(self, phase, l, host):
        return None

    def grad(self, l, name, value):
        self.grads[l][name] = value


class _Layer:
    def __init__(self, hooks, l):
        self.hooks, self.l, self.got = hooks, l, {}

    def __getitem__(self, name):
        if name not in self.got:
            self.got[name] = self.hooks.weight(self.l, name)
        return self.got[name]


def _local_step(x, tgt, n_layers, hooks):
    T = x.shape[0]
    tabs = _rope_tables(T)
    saved, params = [], []
    for l in range(n_layers):
        p = _Layer(hooks, l)
        side_of = functools.partial(hooks.side, "fwd", l)
        x, s1 = _ffn_forward(x, p, "ffn1", side_of)
        x, s2 = _mixer_forward(x, p, tabs, side_of)
        x, s3 = _ffn_forward(x, p, "ffn2", side_of)
        saved.append((s1, s2, s3))
        params.append(p)
    dx, loss = _loss_head(x, tgt, "loss_head")
    for l in reversed(range(n_layers)):
        p = params[l]
        s1, s2, s3 = saved[l]
        side_of = functools.partial(hooks.side, "bwd", l)
        grad = functools.partial(hooks.grad, l)
        dx = _ffn_backward(dx, s3, p, "ffn2", side_of, grad)
        dx = _mixer_backward(dx, s2, p, tabs, side_of, grad)
        dx = _ffn_backward(dx, s1, p, "ffn1", side_of, grad)
    return loss, dx


def _full_from_blocks(name, blocks):
    if name in COL_SHARDED:
        return jnp.transpose(blocks, (1, 0, 2)).reshape(blocks.shape[1], N_CHIPS * blocks.shape[2])
    return blocks.reshape(N_CHIPS * blocks.shape[1], blocks.shape[2])


def _blocks_from_full(name, full):
    K, N = full.shape
    if name in COL_SHARDED:
        return jnp.transpose(full.reshape(K, N_CHIPS, N // N_CHIPS), (1, 0, 2))
    return full.reshape(N_CHIPS, K // N_CHIPS, N)


JOBS = {"a": ("w_ffn1_gu", "w_ffn1_down"), "b": ("w_in", "w_pool_branch", "w_attn_branch", "w_out"),
        "c": ("w_ffn2_gu", "w_ffn2_down")}
GATHER_PLAN = {"ffn1_up": ("ici", "b", JOBS["b"], 0), "ffn1_down": ("d2d", "b", JOBS["b"], 0),
               "mix_in": ("ici", "c", JOBS["c"][:1], 0), "attn_fwd": ("ici", "c", JOBS["c"][1:], 0),
               "merge_fwd": ("d2d", "c", JOBS["c"], 0),
               "ffn2_up": ("ici", "a", JOBS["a"], 1), "ffn2_down": ("d2d", "a", JOBS["a"], 1)}
REDUCE_PLAN = {"ffn2_down_bwd": ("sibling", "a", 1), "ffn2_dwgu": ("chip", "a", 1),
               "ffn2_dh_norm_bwd": ("sibling", "c", 0), "attn_bwd": ("chip", "c", 0),
               "mix_dh_norm_bwd": ("sibling", "b", 0), "ffn1_down_bwd": ("chip", "b", 0)}


class _Exchange:
    def __init__(self, shards, small, place, n_layers):
        self.shards, self.small, self.place, self.n_layers = shards, small, place, n_layers
        first, wides = [shards[n] for n in JOBS["a"]], [_is_wide(n) for n in JOBS["a"]]
        got = _run_side(_gather_ici_side(first, wides, 0), "gather_ici")
        got = _run_side(_gather_d2d_side(first, wides, got, 0), "gather_d2d")
        self.blocks = {(n, 0): g for n, g in zip(JOBS["a"], got)}
        self.landed = {}
        self.handed = []
        self.acc = {n: lax.empty(shards[n].shape, F32) for n in BIG}
        self.grads = [dict() for _ in range(n_layers)]
        self.reduce = {}
        self.summed = set()

    def weight(self, l, name):
        if name not in BIG:
            return self.small(l)[name]
        for names, layer, done in self.handed:
            self.blocks.update({(n, layer): g for n, g in zip(names, done.outs)})
        self.handed.clear()
        blocks = self.blocks.pop((name, l))
        return blocks if name in USED_AS_BLOCKS + WIDE else _full_from_blocks(name, blocks)

    def _gather_side(self, l, host):
        step, job, names, ahead = GATHER_PLAN[host]
        layer = l + ahead
        if layer >= self.n_layers:
            return None
        if step == "ici":
            side = _gather_ici_side([self.shards[n] for n in names], [_is_wide(n) for n in names], layer)
            self.landed.setdefault((job, layer), []).append((names, side))
            return side
        names, gathered = JOBS[job], {}
        for part_names, side in self.landed.pop((job, layer)):
            gathered.update(zip(part_names, side.outs))
        done = _gather_d2d_side([self.shards[n] for n in names], [_is_wide(n) for n in names],
                                [gathered[n] for n in names], layer)
        self.handed.append((names, layer, done))
        return done

    def grad(self, l, name, value):
        self.grads[l][name] = value

    def _reduce_side(self, l, host):
        step, job, ahead = REDUCE_PLAN[host]
        layer = l + ahead
        if layer >= self.n_layers:
            return None
        return self._reduce_step(step, job, layer)

    def _reduce_step(self, step, job, layer):
        if step == "sibling":
            st = self.reduce[(job, layer)] = dict(gm=[self.grads[layer][n] for n in JOBS[job]],
                                                  wide=[_is_wide(n) for n in JOBS[job]])
            st["sibling"] = _reduce_sibling_side(st["gm"], st["wide"])
            return st["sibling"]
        st = self.reduce[(job, layer)]
        st["part"] = [_pair_sum(g, r, wd, self.place, "grad_pair_sum")
                      for g, r, wd in zip(st["gm"], st["sibling"].outs, st["wide"])]
        st["chip"] = _reduce_chip_side(st["part"], st["wide"])
        return st["chip"]

    def _chip_sums(self):
        for (job, layer), st in self.reduce.items():
            if (job, layer) not in self.summed and "chip" in st and st["chip"].outs is not None:
                self.summed.add((job, layer))
                for n, slots, part, wd in zip(JOBS[job], st["chip"].outs, st["part"], st["wide"]):
                    self.acc[n] = _chip_sum(slots, part, wd, self.place, self.acc[n], layer, "grad_chip_sum")

    def side(self, phase, l, host):
        if phase == "fwd":
            return self._gather_side(l, host) if host in GATHER_PLAN else None
        self._chip_sums()
        return self._reduce_side(l, host) if host in REDUCE_PLAN else None

    def reduced(self):
        _run_side(self._reduce_step("sibling", "a", 0), "grad_sibling_exchange")
        _run_side(self._reduce_step("chip", "a", 0), "grad_chip_exchange")
        self._chip_sums()
        return dict(zip(BIG, _run_side(_share_side([self.acc[n] for n in BIG]), "grad_sibling_share")))


def _pack_small(parts):
    rows, spans, lo = [], [], 0
    for v in parts:
        flat = v.reshape(-1)
        nrow = -(-flat.shape[0] // LANES)
        flat = jnp.pad(flat, (0, nrow * LANES - flat.shape[0]))
        rows.append(flat.reshape(nrow, LANES))
        spans.append((lo, nrow))
        lo += nrow
    pad = -lo % 8
    if pad:
        rows.append(jnp.zeros((pad, LANES), F32))
    return jnp.concatenate(rows, axis=0), spans


def _unpack_small(packed, spans, shapes):
    out = []
    for (lo, nrow), shape in zip(spans, shapes):
        size = 1
        for s in shape:
            size *= s
        out.append(packed[lo:lo + nrow].reshape(-1)[:size].reshape(shape))
    return out


def kernel(x, ln_ffn1, w_ffn1_gu, w_ffn1_down, ln_mix, w_in, pool_w, pool_scale, w_pool_branch, q_norm, k_norm, sinks, w_attn_branch, w_out, ln_ffn2, w_ffn2_gu, w_ffn2_down, loss_target, m_ln_ffn1, m_w_ffn1_gu, m_w_ffn1_down, m_ln_mix, m_w_in, m_pool_w, m_pool_scale, m_w_pool_branch, m_q_norm, m_k_norm, m_sinks, m_w_attn_branch, m_w_out, m_ln_ffn2, m_w_ffn2_gu, m_w_ffn2_down, v_ln_ffn1, v_w_ffn1_gu, v_w_ffn1_down, v_ln_mix, v_w_in, v_pool_w, v_pool_scale, v_w_pool_branch, v_q_norm, v_k_norm, v_sinks, v_w_attn_branch, v_w_out, v_ln_ffn2, v_w_ffn2_gu, v_w_ffn2_down):
    w = dict(ln_ffn1=ln_ffn1, w_ffn1_gu=w_ffn1_gu, w_ffn1_down=w_ffn1_down, ln_mix=ln_mix, w_in=w_in, pool_w=pool_w,
             pool_scale=pool_scale, w_pool_branch=w_pool_branch, q_norm=q_norm, k_norm=k_norm, sinks=sinks,
             w_attn_branch=w_attn_branch, w_out=w_out, ln_ffn2=ln_ffn2, w_ffn2_gu=w_ffn2_gu, w_ffn2_down=w_ffn2_down)
    mom = dict(ln_ffn1=m_ln_ffn1, w_ffn1_gu=m_w_ffn1_gu, w_ffn1_down=m_w_ffn1_down, ln_mix=m_ln_mix, w_in=m_w_in,
               pool_w=m_pool_w, pool_scale=m_pool_scale, w_pool_branch=m_w_pool_branch, q_norm=m_q_norm, k_norm=m_k_norm,
               sinks=m_sinks, w_attn_branch=m_w_attn_branch, w_out=m_w_out, ln_ffn2=m_ln_ffn2, w_ffn2_gu=m_w_ffn2_gu,
               w_ffn2_down=m_w_ffn2_down)
    var = dict(ln_ffn1=v_ln_ffn1, w_ffn1_gu=v_w_ffn1_gu, w_ffn1_down=v_w_ffn1_down, ln_mix=v_ln_mix, w_in=v_w_in,
               pool_w=v_pool_w, pool_scale=v_pool_scale, w_pool_branch=v_w_pool_branch, q_norm=v_q_norm, k_norm=v_k_norm,
               sinks=v_sinks, w_attn_branch=v_w_attn_branch, w_out=v_w_out, ln_ffn2=v_ln_ffn2, w_ffn2_gu=v_w_ffn2_gu,
               w_ffn2_down=v_w_ffn2_down)
    L = ln_ffn1.shape[0]

    def small(l):
        return dict(ln_ffn1=ln_ffn1[l], ln_mix=ln_mix[l], ln_ffn2=ln_ffn2[l], pool_w=pool_w[l].astype(CDT),
                    pool_scale=pool_scale[l], sinks=sinks[l],
                    gqk=jnp.concatenate([jnp.tile(q_norm[l], N_Q_HEADS), jnp.tile(k_norm[l], KV_DIM // HEAD_DIM)]).reshape(1, QK_DIM))

    place = jnp.stack([lax.axis_index("c"), 2 * lax.axis_index("x") + lax.axis_index("y")]).astype(jnp.int32)
    hooks = _Exchange({n: w[n].astype(CDT) for n in BIG}, small, place, L)
    loss_part, grad_x = _local_step(x[0], loss_target[0], L, hooks)
    g_big = hooks.reduced()
    grads = hooks.grads

    small_parts = [jnp.stack([g[n] for g in grads]) for n in SMALL] + [loss_part]
    packed, spans = _pack_small(small_parts)
    summed = _all_reduce_small(packed)
    *g_small_list, loss_sum = _unpack_small(summed, spans, [w[n].shape for n in SMALL] + [(1, 1)])
    g_small = dict(zip(SMALL, g_small_list))
    loss = loss_sum[0, 0]

    grad_out, delta, new_m, new_v = {}, {}, {}, {}
    for n in BIG:
        shape = w[n].shape
        flat = (shape[0] * shape[1], shape[2])
        grad_out[n] = g_big[n]
        d, nm, nv = _adamw(w[n].reshape(flat), g_big[n].reshape(flat), mom[n].reshape(flat), var[n].reshape(flat), "adamw")
        delta[n], new_m[n], new_v[n] = d.reshape(shape), nm.reshape(shape), nv.reshape(shape)
    pw, _ = _pack_small([w[n] for n in SMALL])
    pg, sp = _pack_small([g_small[n] for n in SMALL])
    pm_, _ = _pack_small([mom[n] for n in SMALL])
    pv, _ = _pack_small([var[n] for n in SMALL])
    d, nm, nv = _adamw(pw, pg, pm_, pv, "adamw_small")
    shapes = [w[n].shape for n in SMALL]
    for n, dv, mv, vv in zip(SMALL, _unpack_small(d, sp, shapes), _unpack_small(nm, sp, shapes), _unpack_small(nv, sp, shapes)):
        grad_out[n], delta[n], new_m[n], new_v[n] = g_small[n], dv, mv, vv

    return (loss, grad_x[None], *[grad_out[n] for n in WEIGHTS], *[delta[n] for n in WEIGHTS],
            *[new_m[n] for n in WEIGHTS], *[new_v[n] for n in WEIGHTS])
```

```python
import functools
import math

import jax
import jax.numpy as jnp
from jax import lax
from jax.experimental import pallas as pl
from jax.experimental.pallas import tpu as pltpu

F32 = jnp.float32
CDT = jnp.bfloat16
WIRE_DT = jnp.bfloat16

D_MODEL = 1024
POOL_WINDOWS = (2, 4, 8, 16)
POOL_WMAX = 16
GROUP = 128
POOL_DIM = 512
HEAD_DIM = 64
N_Q_HEADS = 8
ATTN_DIM = 512
KV_DIM = 128
QK_DIM = ATTN_DIM + KV_DIM
GATE_DIM = 2 * D_MODEL
BLOCK = 128
ROPE_THETA = 500000.0
ROT_DIM = 16
EPS = 1e-6
ATTN_SCALE = HEAD_DIM ** -0.5

ADAM_LR = 0.001
ADAM_B1 = 0.9
ADAM_B2 = 0.999
ADAM_EPS = 1e-08
ADAM_WD = 0.01
ADAM_STEP = 10

N_CHIPS = 4
N_DEV = 8
LANES = 128
VMEM_LIMIT_BYTES = 48 * 1024 * 1024

MESH = pl.DeviceIdType.MESH
ANY = pl.BlockSpec(memory_space=pl.ANY)

BIG = ("w_ffn1_gu", "w_ffn1_down", "w_in", "w_pool_branch", "w_attn_branch", "w_out", "w_ffn2_gu", "w_ffn2_down")
COL_SHARDED = ("w_ffn1_gu", "w_in", "w_pool_branch", "w_attn_branch", "w_ffn2_gu")
USED_AS_BLOCKS = ("w_pool_branch", "w_attn_branch")
WIDE = ("w_ffn1_gu", "w_ffn2_gu")
SMALL = ("ln_ffn1", "ln_mix", "pool_w", "pool_scale", "q_norm", "k_norm", "sinks", "ln_ffn2")
WEIGHTS = ("ln_ffn1", "w_ffn1_gu", "w_ffn1_down", "ln_mix", "w_in", "pool_w", "pool_scale", "w_pool_branch",
           "q_norm", "k_norm", "sinks", "w_attn_branch", "w_out", "ln_ffn2", "w_ffn2_gu", "w_ffn2_down")


def _tile(n, target, mult=8):
    if n <= target:
        return n
    for t in range(target - target % mult, 0, -mult):
        if n % t == 0:
            return t
    raise ValueError((n, target, mult))


def _params(*sem):
    return pltpu.CompilerParams(dimension_semantics=sem, vmem_limit_bytes=VMEM_LIMIT_BYTES)


def _sigmoid(v):
    return 0.5 * jnp.tanh(0.5 * v) + 0.5


def _dot(a, b):
    return jnp.dot(a, b, preferred_element_type=F32)


def _dot_nt(a, b):
    return lax.dot_general(a, b, (((1,), (1,)), ((), ())), preferred_element_type=F32)


def _dot_tn(a, b):
    return lax.dot_general(a, b, (((0,), (0,)), ((), ())), preferred_element_type=F32)


class _Side:
    def __init__(self, ins, out_shapes, n_sems, issue, aliases=None):
        self.ins, self.out_shapes, self.n_sems, self.issue = list(ins), list(out_shapes), n_sems, issue
        self.aliases = dict(aliases or {})
        self.outs = None


def _pcall(body, name, grid, in_specs, out_specs, out_shape, args, dims, side=None, scratch=()):
    scratch = list(scratch)
    if side is None:
        return pl.pallas_call(body, name=name, grid=grid, in_specs=in_specs, out_specs=out_specs, out_shape=out_shape,
                              scratch_shapes=scratch, compiler_params=_params(*dims))(*args)
    n_in, n_out, s_in, s_out = len(in_specs), len(out_specs), len(side.ins), len(side.out_shapes)

    def wrapped(*refs):
        main_in, side_in = refs[:n_in], refs[n_in:n_in + s_in]
        main_out = refs[n_in + s_in:n_in + s_in + n_out]
        side_out = refs[n_in + s_in + n_out:n_in + s_in + n_out + s_out]
        rest = refs[n_in + s_in + n_out + s_out:]
        main_scratch, (ssem, rsem) = rest[:len(scratch)], rest[len(scratch):]
        ids = [pl.program_id(ax) for ax in range(len(grid))]
        first = functools.reduce(jnp.logical_and, [i == 0 for i in ids])
        last = functools.reduce(jnp.logical_and, [i == g - 1 for i, g in zip(ids, grid)])

        @pl.when(first)
        def _():
            for cp in side.issue(side_in, side_out, ssem, rsem):
                cp.start()

        body(*main_in, *main_out, *main_scratch)

        @pl.when(last)
        def _():
            cps = side.issue(side_in, side_out, ssem, rsem)
            for cp in cps:
                cp.wait_recv()
            for cp in cps:
                cp.wait_send()

    outs = pl.pallas_call(
        wrapped, name=name, grid=grid, in_specs=list(in_specs) + [ANY] * s_in, out_specs=list(out_specs) + [ANY] * s_out,
        out_shape=list(out_shape) + side.out_shapes,
        input_output_aliases={n_in + i: n_out + o for i, o in side.aliases.items()},
        scratch_shapes=scratch + [pltpu.SemaphoreType.DMA((side.n_sems,))] * 2,
        compiler_params=_params(*["arbitrary"] * len(grid)),
    )(*args, *side.ins)
    side.outs = list(outs[n_out:])
    return list(outs[:n_out])


def _run_side(side, name):
    s_in = len(side.ins)

    def body(*refs):
        ssem, rsem = refs[s_in + len(side.out_shapes):]
        cps = side.issue(refs[:s_in], refs[s_in:s_in + len(side.out_shapes)], ssem, rsem)
        for cp in cps:
            cp.start()
        for cp in cps:
            cp.wait_recv()
        for cp in cps:
            cp.wait_send()

    side.outs = list(pl.pallas_call(
        body, name=name, in_specs=[ANY] * s_in, out_specs=[ANY] * len(side.out_shapes), out_shape=side.out_shapes,
        input_output_aliases=side.aliases, scratch_shapes=[pltpu.SemaphoreType.DMA((side.n_sems,))] * 2,
    )(*side.ins))
    return side.outs


def _loss_head(y, tgt, name):
    T, Dm = y.shape
    tm = _tile(T, 512)

    def body(y_ref, t_ref, dy_ref, loss_ref):
        @pl.when(pl.program_id(0) == 0)
        def _():
            loss_ref[...] = jnp.zeros_like(loss_ref)

        diff = y_ref[...] - t_ref[...]
        dy_ref[...] = diff * (1.0 / Dm)
        part = jnp.sum(jnp.mean(diff * diff, axis=-1, keepdims=True), axis=0, keepdims=True)
        loss_ref[...] += 0.5 * part

    row = pl.BlockSpec((tm, Dm), lambda i: (i, 0))
    one = pl.BlockSpec((1, 1), lambda i: (0, 0))
    return pl.pallas_call(
        body, name=name, grid=(T // tm,),
        in_specs=[row, row], out_specs=[row, one],
        out_shape=[jax.ShapeDtypeStruct((T, Dm), F32), jax.ShapeDtypeStruct((1, 1), F32)],
        compiler_params=_params("arbitrary"),
    )(y, tgt)


def _mm_nn(a, b, name, out_dtype, res=None, scale=1.0, tm_target=512, side=None):
    M, K = a.shape
    N = b.shape[1]
    tm = _tile(M, tm_target)

    def body(a_ref, b_ref, *rest):
        acc = _dot(a_ref[...].astype(CDT), b_ref[...])
        if res is None:
            (o_ref,) = rest
        else:
            r_ref, o_ref = rest
            acc = r_ref[...] + scale * acc
        o_ref[...] = acc.astype(o_ref.dtype)

    in_specs = [pl.BlockSpec((tm, K), lambda i: (i, 0)), pl.BlockSpec((K, N), lambda i: (0, 0))]
    args = [a, b]
    if res is not None:
        in_specs.append(pl.BlockSpec((tm, N), lambda i: (i, 0)))
        args.append(res)
    return _pcall(body, name, (M // tm,), in_specs, [pl.BlockSpec((tm, N), lambda i: (i, 0))],
                  [jax.ShapeDtypeStruct((M, N), out_dtype)], args, ("parallel",), side)[0]


def _mm_nt_blocks(a, b4, name, out_dtype, tm_target=512):
    M, K = a.shape
    nb, N, Kb = b4.shape
    tm = _tile(M, tm_target)

    def body(a_ref, b_ref, o_ref):
        acc = _dot_nt(a_ref[:, :Kb].astype(CDT), b_ref[0])
        for j in range(1, nb):
            acc = acc + _dot_nt(a_ref[:, j * Kb:(j + 1) * Kb].astype(CDT), b_ref[j])
        o_ref[...] = acc.astype(o_ref.dtype)

    return pl.pallas_call(
        body, name=name, grid=(M // tm,),
        in_specs=[pl.BlockSpec((tm, K), lambda i: (i, 0)), pl.BlockSpec(b4.shape, lambda i: (0, 0, 0))],
        out_specs=pl.BlockSpec((tm, N), lambda i: (i, 0)),
        out_shape=jax.ShapeDtypeStruct((M, N), out_dtype), compiler_params=_params("parallel"),
    )(a, b4)


def _mm_tn(x, dy, name, scale=1.0, col_blocks=1, tn_target=1664, tm_target=1408, tk_target=1024, side=None):
    T, M = x.shape
    split = dy.ndim == 3
    Nh = dy.shape[-1]
    N = 2 * Nh if split else Nh
    nb = N // col_blocks
    whole = col_blocks > 1 and not split and N <= tn_target
    tm = _tile(M, tm_target, LANES)
    tn = N if whole else _tile(math.gcd(Nh, nb), tn_target, LANES)
    tk = _tile(T, tk_target)
    nk = T // tk
    njh, njb = Nh // tn, max(nb // tn, 1)

    def body(x_ref, dy_ref, o_ref, acc_ref):
        k = pl.program_id(2)

        @pl.when(k == 0)
        def _():
            acc_ref[...] = jnp.zeros_like(acc_ref)

        acc_ref[...] += _dot_tn(x_ref[...].astype(CDT), dy_ref[...].astype(CDT))

        @pl.when(k == nk - 1)
        def _():
            res = (acc_ref[...] if scale == 1.0 else scale * acc_ref[...]).astype(o_ref.dtype)
            if whole:
                for b in range(col_blocks):
                    o_ref[b] = res[:, b * nb:(b + 1) * nb]
            else:
                o_ref[...] = res

    if split:
        dy_spec = pl.BlockSpec((None, tk, tn), lambda i, j, k: (j // njh, k, j % njh))
    else:
        dy_spec = pl.BlockSpec((tk, tn), lambda i, j, k: (k, j))
    if col_blocks == 1:
        out_spec, out_dims = pl.BlockSpec((tm, tn), lambda i, j, k: (i, j)), (M, N)
    elif whole:
        out_spec, out_dims = pl.BlockSpec((col_blocks, tm, nb), lambda i, j, k: (0, i, 0)), (col_blocks, M, nb)
    else:
        out_spec, out_dims = pl.BlockSpec((None, tm, tn), lambda i, j, k: (j // njb, i, j % njb)), (col_blocks, M, nb)
    return _pcall(body, name, (M // tm, N // tn, nk), [pl.BlockSpec((tk, tm), lambda i, j, k: (k, i)), dy_spec],
                  [out_spec], [jax.ShapeDtypeStruct(out_dims, WIRE_DT)], (x, dy), ("parallel", "parallel", "arbitrary"),
                  side, [pltpu.VMEM((tm, tn), F32)])[0]


def _mm_tn_parts(x, parts, name):
    T, M = x.shape
    widths = [p.shape[1] for p in parts]
    N = sum(widths)
    tk = _tile(T, 512)

    def body(x_ref, *refs):
        o_ref = refs[-1]

        @pl.when(pl.program_id(0) == 0)
        def _():
            o_ref[...] = jnp.zeros_like(o_ref)

        xv = x_ref[...].astype(CDT)
        lo = 0
        for p_ref, wd in zip(refs[:-1], widths):
            o_ref[:, lo:lo + wd] += _dot_tn(xv, p_ref[...].astype(CDT))
            lo += wd

    return pl.pallas_call(
        body, name=name, grid=(T // tk,),
        in_specs=[pl.BlockSpec((tk, M), lambda k: (k, 0))] + [pl.BlockSpec((tk, wd), lambda k: (k, 0)) for wd in widths],
        out_specs=pl.BlockSpec((M, N), lambda k: (0, 0)),
        out_shape=jax.ShapeDtypeStruct((M, N), F32), compiler_params=_params("arbitrary"),
    )(x, *parts)


def _rmsnorm_rows(x_ref, g_ref):
    xv = x_ref[...]
    r = lax.rsqrt(jnp.mean(xv * xv, axis=-1, keepdims=True) + EPS)
    return (xv * r * g_ref[...]).astype(CDT)


def _ffn_up(x, ln, wgu, name, side=None):
    T, Dm = x.shape
    Fd = wgu.shape[1] // 2
    tm = _tile(T, 256)

    def body(x_ref, ln_ref, wg_ref, wu_ref, h_ref, gu_ref, a_ref):
        hv = _rmsnorm_rows(x_ref, ln_ref)
        h_ref[...] = hv
        g = _dot(hv, wg_ref[...])
        u = _dot(hv, wu_ref[...])
        sg = _sigmoid(g)
        silu = g * sg
        a_ref[...] = (silu * u).astype(a_ref.dtype)
        gu_ref[0] = (0.5 * u * (sg * (1.0 + g * (1.0 - sg)))).astype(gu_ref.dtype)
        gu_ref[1] = (0.5 * silu).astype(gu_ref.dtype)

    row = pl.BlockSpec((tm, Dm), lambda i: (i, 0))
    return _pcall(
        body, name, (T // tm,),
        [row, pl.BlockSpec((1, Dm), lambda i: (0, 0)),
         pl.BlockSpec((Dm, Fd), lambda i: (0, 0), pipeline_mode=pl.Buffered(1)),
         pl.BlockSpec((Dm, Fd), lambda i: (0, 1), pipeline_mode=pl.Buffered(1))],
        [row, pl.BlockSpec((2, tm, Fd), lambda i: (0, i, 0)), pl.BlockSpec((tm, Fd), lambda i: (i, 0))],
        [jax.ShapeDtypeStruct((T, Dm), CDT), jax.ShapeDtypeStruct((2, T, Fd), CDT), jax.ShapeDtypeStruct((T, Fd), CDT)],
        (x, ln.reshape(1, Dm), wgu, wgu), ("parallel",), side)


def _ffn_down_bwd(dxo, wd, gu, name, side=None):
    T, Dm = dxo.shape
    Fd = wd.shape[0]
    tm = _tile(T, 256)

    def body(dx_ref, wd_ref, gu_ref, dgu_ref):
        da = _dot_nt(dx_ref[...].astype(CDT), wd_ref[...])
        dgu_ref[0] = (da * gu_ref[0].astype(F32)).astype(dgu_ref.dtype)
        dgu_ref[1] = (da * gu_ref[1].astype(F32)).astype(dgu_ref.dtype)

    gu_spec = pl.BlockSpec((2, tm, Fd), lambda i: (0, i, 0))
    return _pcall(
        body, name, (T // tm,),
        [pl.BlockSpec((tm, Dm), lambda i: (i, 0)),
         pl.BlockSpec((Fd, Dm), lambda i: (0, 0), pipeline_mode=pl.Buffered(1)), gu_spec],
        [gu_spec], [jax.ShapeDtypeStruct((2, T, Fd), CDT)],
        (dxo, wd, gu), ("parallel",), side)[0]


def _mm_nt_norm_bwd(a_parts, b, x, g, dres, name, side=None):
    T, Dm = x.shape
    tm = _tile(T, 256)

    def b_cols(b_ref, lo, wd):
        if b.ndim == 2:
            return [(0, wd, b_ref[:, lo:lo + wd])]
        kb = b.shape[2]
        return [(j * kb - lo, kb, b_ref[j]) for j in range(lo // kb, (lo + wd) // kb)]

    def body(*refs):
        a_refs, (b_ref, x_ref, g_ref, dres_ref, dx_ref, dg_ref) = refs[:len(a_parts)], refs[len(a_parts):]

        @pl.when(pl.program_id(0) == 0)
        def _():
            dg_ref[...] = jnp.zeros_like(dg_ref)

        dh, lo = None, 0
        for a_ref, part in zip(a_refs, a_parts):
            slabs = [a_ref] if part.ndim == 2 else [a_ref.at[s_] for s_ in range(part.shape[0])]
            for slab in slabs:
                for off, wd, bv in b_cols(b_ref, lo, part.shape[-1]):
                    term = _dot_nt(slab[:, off:off + wd].astype(CDT), bv)
                    dh = term if dh is None else dh + term
                lo += part.shape[-1]
        xv = x_ref[...]
        r = lax.rsqrt(jnp.mean(xv * xv, axis=-1, keepdims=True) + EPS)
        xh = xv * r
        dg_ref[...] += jnp.sum(dh * xh, axis=0, keepdims=True)
        dxh = dh * g_ref[...]
        dx_ref[...] = dres_ref[...] + r * (dxh - xh * jnp.mean(dxh * xh, axis=-1, keepdims=True))

    row = pl.BlockSpec((tm, Dm), lambda i: (i, 0))
    vec = pl.BlockSpec((1, Dm), lambda i: (0, 0))
    a_specs = [pl.BlockSpec((tm, p.shape[1]), lambda i: (i, 0)) if p.ndim == 2 else
               pl.BlockSpec((p.shape[0], tm, p.shape[2]), lambda i: (0, i, 0)) for p in a_parts]
    b_spec = pl.BlockSpec(b.shape, lambda i: (0,) * b.ndim, pipeline_mode=pl.Buffered(1))
    return _pcall(body, name, (T // tm,), a_specs + [b_spec, row, vec, row], [row, vec],
                  [jax.ShapeDtypeStruct((T, Dm), F32), jax.ShapeDtypeStruct((1, Dm), F32)],
                  (*a_parts, b, x, g.reshape(1, Dm), dres), ("arbitrary",), side)


def _mm_in(x, ln, w_in, name, side=None):
    T, Dm = x.shape
    tm = _tile(T, 256)
    widths = (POOL_DIM, QK_DIM, KV_DIM, GATE_DIM)

    def body(x_ref, ln_ref, w_ref, h_ref, *outs):
        hv = _rmsnorm_rows(x_ref, ln_ref)
        h_ref[...] = hv
        z = _dot(hv, w_ref[...])
        lo = 0
        for o_ref, wd in zip(outs, widths):
            o_ref[...] = z[:, lo:lo + wd]
            lo += wd

    row = pl.BlockSpec((tm, Dm), lambda i: (i, 0))
    return _pcall(body, name, (T // tm,),
                  [row, pl.BlockSpec((1, Dm), lambda i: (0, 0)),
                   pl.BlockSpec(w_in.shape, lambda i: (0, 0), pipeline_mode=pl.Buffered(1))],
                  [row] + [pl.BlockSpec((tm, wd), lambda i: (i, 0)) for wd in widths],
                  [jax.ShapeDtypeStruct((T, Dm), CDT)] + [jax.ShapeDtypeStruct((T, wd), F32) for wd in widths],
                  (x, ln.reshape(1, Dm), w_in), ("parallel",), side)


def _window_mean_minus_token(ext, u, g, w, pos):
    sl = slice(g * GROUP, (g + 1) * GROUP)
    s = ext[:, sl]
    span = 1
    while span < w:
        s = s + pltpu.roll(s, span, axis=0)
        span *= 2
    cnt = jnp.minimum(pos + 1, w).astype(F32)
    return s[POOL_WMAX:, :] / cnt - u[:, sl]


def _pool_fwd(zu, pool_w, scale, name):
    T = zu.shape[0]
    tm = _tile(T, 512, POOL_WMAX)
    hb = tm // POOL_WMAX

    def body(u_ref, halo_ref, pw_ref, sc_ref, pm_ref):
        i = pl.program_id(0)
        u = u_ref[...]
        halo = jnp.where(i > 0, halo_ref[...], 0.0)
        ext = jnp.concatenate([halo, u], axis=0)
        pos = i * tm + lax.broadcasted_iota(jnp.int32, (tm, 1), 0)
        ys = []
        for g, w in enumerate(POOL_WINDOWS):
            d = _window_mean_minus_token(ext, u, g, w, pos)
            ys.append(_dot(d.astype(CDT), pw_ref[g]))
        pm_ref[...] = (jnp.concatenate(ys, axis=1) * sc_ref[...]).astype(pm_ref.dtype)

    row = pl.BlockSpec((tm, POOL_DIM), lambda i: (i, 0))
    return pl.pallas_call(
        body, name=name, grid=(T // tm,),
        in_specs=[row, pl.BlockSpec((POOL_WMAX, POOL_DIM), lambda i: (jnp.maximum(i * hb - 1, 0), 0)),
                  pl.BlockSpec(pool_w.shape, lambda i: (0, 0, 0)), pl.BlockSpec((1, POOL_DIM), lambda i: (0, 0))],
        out_specs=row, out_shape=jax.ShapeDtypeStruct((T, POOL_DIM), CDT),
        compiler_params=_params("parallel"),
    )(zu, zu, pool_w, scale.reshape(1, POOL_DIM))


def _pool_bwd(zu, dpm, pool_w, scale, name):
    T = zu.shape[0]
    tm = _tile(T, 512, POOL_WMAX)
    hb = tm // POOL_WMAX
    nsteps = T // tm
    ext_rows = tm + POOL_WMAX

    def body(u_ref, halo_ref, dpm_ref, dnext_ref, pw_ref, sc_ref, du_ref, dpw_ref, dsc_ref):
        i = pl.program_id(0)

        @pl.when(i == 0)
        def _():
            dpw_ref[...] = jnp.zeros_like(dpw_ref)
            dsc_ref[...] = jnp.zeros_like(dsc_ref)

        u = u_ref[...]
        halo = jnp.where(i > 0, halo_ref[...], 0.0)
        ext = jnp.concatenate([halo, u], axis=0)
        dpm_t = dpm_ref[...].astype(F32)
        dnext = jnp.where(i < nsteps - 1, dnext_ref[...].astype(F32), 0.0)
        dext = jnp.concatenate([dpm_t, dnext], axis=0)
        sc = sc_ref[...]
        pos = i * tm + lax.broadcasted_iota(jnp.int32, (tm, 1), 0)
        pos_ext = i * tm + lax.broadcasted_iota(jnp.int32, (ext_rows, 1), 0)
        dus, dscs = [], []
        for g, w in enumerate(POOL_WINDOWS):
            sl = slice(g * GROUP, (g + 1) * GROUP)
            dc = _window_mean_minus_token(ext, u, g, w, pos).astype(CDT)
            y = _dot(dc, pw_ref[g])
            dscs.append(jnp.sum(dpm_t[:, sl] * y, axis=0, keepdims=True))
            dy_ext = (dext[:, sl] * sc[:, sl]).astype(CDT)
            dpw_ref[g] += _dot_tn(dc, dy_ext[:tm])
            dd = _dot_nt(dy_ext, pw_ref[g])
            r = dd / jnp.minimum(pos_ext + 1, w).astype(F32)
            span = 1
            while span < w:
                r = r + pltpu.roll(r, ext_rows - span, axis=0)
                span *= 2
            dus.append(r[:tm] - dd[:tm])
        du_ref[...] = jnp.concatenate(dus, axis=1).astype(du_ref.dtype)
        dsc_ref[...] += jnp.concatenate(dscs, axis=1)

    row = pl.BlockSpec((tm, POOL_DIM), lambda i: (i, 0))
    prev = pl.BlockSpec((POOL_WMAX, POOL_DIM), lambda i: (jnp.maximum(i * hb - 1, 0), 0))
    nxt = pl.BlockSpec((POOL_WMAX, POOL_DIM), lambda i: (jnp.minimum((i + 1) * hb, nsteps * hb - 1), 0))
    return pl.pallas_call(
        body, name=name, grid=(nsteps,),
        in_specs=[row, prev, row, nxt, pl.BlockSpec(pool_w.shape, lambda i: (0, 0, 0)),
                  pl.BlockSpec((1, POOL_DIM), lambda i: (0, 0))],
        out_specs=[row, pl.BlockSpec(pool_w.shape, lambda i: (0, 0, 0)), pl.BlockSpec((1, POOL_DIM), lambda i: (0, 0))],
        out_shape=[jax.ShapeDtypeStruct((T, POOL_DIM), CDT), jax.ShapeDtypeStruct(pool_w.shape, F32),
                   jax.ShapeDtypeStruct((1, POOL_DIM), F32)],
        compiler_params=_params("arbitrary"),
    )(zu, zu, dpm, dpm, pool_w, scale.reshape(1, POOL_DIM))


def _rope_tables(T):
    pos = jnp.arange(T, dtype=F32)
    inv_freq = ROPE_THETA ** (-jnp.arange(0, ROT_DIM, 2, dtype=F32) / ROT_DIM)
    ang = pos[:, None] * inv_freq[None, :]
    cos, sin = jnp.cos(ang), jnp.sin(ang)
    rest = HEAD_DIM - ROT_DIM
    cos_h = jnp.concatenate([cos, cos, jnp.ones((T, rest), F32)], axis=1)
    sin_h = jnp.concatenate([-sin, sin, jnp.zeros((T, rest), F32)], axis=1)
    return jnp.tile(cos_h, (1, 2)), jnp.tile(sin_h, (1, 2))


def _lane_masks():
    lane = lax.broadcasted_iota(jnp.int32, (1, LANES), 1)
    in_head = lane % HEAD_DIM
    return lane < HEAD_DIM, in_head < ROT_DIM // 2


def _rope_partner(v, low):
    lane = lax.broadcasted_iota(jnp.int32, (1, LANES), 1)
    swapped = jnp.where(low, pltpu.roll(v, LANES - ROT_DIM // 2, axis=1), pltpu.roll(v, ROT_DIM // 2, axis=1))
    return jnp.where(lane % HEAD_DIM < ROT_DIM, swapped, 0.0)


def _head_mean(v, first):
    lo = jnp.sum(jnp.where(first, v, 0.0), axis=-1, keepdims=True)
    hi = jnp.sum(jnp.where(first, 0.0, v), axis=-1, keepdims=True)
    return jnp.where(first, lo, hi) * (1.0 / HEAD_DIM)


def _qk_fwd(zqk, gqk, cos_t, sin_t, name):
    T = zqk.shape[0]
    tm = _tile(T, 512)

    def body(z_ref, g_ref, c_ref, s_ref, o_ref):
        first, low = _lane_masks()
        cosv, sinv = c_ref[...], s_ref[...]
        for c in range(QK_DIM // LANES):
            sl = slice(c * LANES, (c + 1) * LANES)
            xv = z_ref[:, sl]
            r = lax.rsqrt(_head_mean(xv * xv, first) + EPS)
            xn = xv * r * g_ref[:, sl]
            o_ref[:, sl] = (xn * cosv + _rope_partner(xn, low) * sinv).astype(o_ref.dtype)

    row = pl.BlockSpec((tm, QK_DIM), lambda i: (i, 0))
    tab = pl.BlockSpec((tm, LANES), lambda i: (i, 0))
    return pl.pallas_call(
        body, name=name, grid=(T // tm,),
        in_specs=[row, pl.BlockSpec((1, QK_DIM), lambda i: (0, 0)), tab, tab], out_specs=row,
        out_shape=jax.ShapeDtypeStruct((T, QK_DIM), CDT), compiler_params=_params("parallel"),
    )(zqk, gqk, cos_t, sin_t)


def _qk_bwd(dqk, zqk, gqk, cos_t, sin_t, name):
    T = zqk.shape[0]
    tm = _tile(T, 512)

    def body(d_ref, z_ref, g_ref, c_ref, s_ref, dz_ref, dg_ref):
        @pl.when(pl.program_id(0) == 0)
        def _():
            dg_ref[...] = jnp.zeros_like(dg_ref)

        first, low = _lane_masks()
        cosv, sinv = c_ref[...], s_ref[...]
        dgs = []
        for c in range(QK_DIM // LANES):
            sl = slice(c * LANES, (c + 1) * LANES)
            dout = d_ref[:, sl]
            dxn = dout * cosv + _rope_partner(dout * sinv, low)
            xv = z_ref[:, sl]
            r = lax.rsqrt(_head_mean(xv * xv, first) + EPS)
            xh = xv * r
            dgs.append(jnp.sum(dxn * xh, axis=0, keepdims=True))
            dxh = dxn * g_ref[:, sl]
            dz_ref[:, sl] = (r * (dxh - xh * _head_mean(dxh * xh, first))).astype(dz_ref.dtype)
        dg_ref[...] += jnp.concatenate(dgs, axis=1)

    row = pl.BlockSpec((tm, QK_DIM), lambda i: (i, 0))
    tab = pl.BlockSpec((tm, LANES), lambda i: (i, 0))
    vec = pl.BlockSpec((1, QK_DIM), lambda i: (0, 0))
    return pl.pallas_call(
        body, name=name, grid=(T // tm,),
        in_specs=[row, row, vec, tab, tab], out_specs=[row, vec],
        out_shape=[jax.ShapeDtypeStruct((T, QK_DIM), CDT), jax.ShapeDtypeStruct((1, QK_DIM), F32)],
        compiler_params=_params("arbitrary"),
    )(dqk, zqk, gqk, cos_t, sin_t)


def _dup_half(v, first, kv):
    swapped = pltpu.roll(v, HEAD_DIM, axis=1)
    return jnp.where(first, v, swapped) if kv == 0 else jnp.where(first, swapped, v)


HEADS_PER_KV = 4
HEAD_STACK = 1


def _attn_bias():
    qi = lax.broadcasted_iota(jnp.int32, (HEAD_STACK * BLOCK, 2 * BLOCK), 0) % BLOCK
    ki = lax.broadcasted_iota(jnp.int32, (HEAD_STACK * BLOCK, 2 * BLOCK), 1)
    diff = qi + BLOCK - ki
    band = (diff >= 0) & (diff < BLOCK)
    return jnp.stack([jnp.where(band, 0.0, -jnp.inf), jnp.where(band & (ki >= BLOCK), 0.0, -jnp.inf)]).astype(F32)


def _attn_blocks(T):
    return _tile(T // BLOCK, 4, 1)


def _stack_heads(ref, rows, kv, heads, first):
    parts = []
    for h in heads:
        c = 2 * kv + h // 2
        v = ref[rows, c * LANES:(c + 1) * LANES].astype(CDT)
        zero = jnp.zeros_like(v)
        parts.append(jnp.where(first, v, zero) if h % 2 == 0 else jnp.where(first, zero, v))
    return parts[0] if len(parts) == 1 else jnp.concatenate(parts, axis=0)


def _row_blocks(v, n):
    return [v[b * BLOCK:(b + 1) * BLOCK] for b in range(n)]


def _sink_column(sink_ref, kv, heads):
    cols = [jnp.full((BLOCK, 1), sink_ref[HEADS_PER_KV * kv + h], F32) for h in heads]
    return cols[0] if len(cols) == 1 else jnp.concatenate(cols, axis=0)


def _head_groups():
    return [tuple(range(g, g + HEAD_STACK)) for g in range(0, HEADS_PER_KV, HEAD_STACK)]


def _softmax_with_sink(qst, kdup, sinkcol, bias):
    s = _dot_nt(qst, kdup) * ATTN_SCALE + bias
    m = jnp.maximum(jnp.max(s, axis=-1, keepdims=True), sinkcol)
    pu = jnp.exp(s - m)
    denom = jnp.sum(pu, axis=-1, keepdims=True) + jnp.exp(sinkcol - m)
    return pu * (1.0 / denom), m + jnp.log(denom)


def _attn_fwd(qkn, zv, sinks, name, side=None):
    T = qkn.shape[0]
    R = _attn_blocks(T)
    tq = R * BLOCK

    def body(sink_ref, bias_ref, qk_ref, qkp_ref, v_ref, vp_ref, o_ref, lse_ref):
        i = pl.program_id(0)
        first, _ = _lane_masks()
        lane = lax.broadcasted_iota(jnp.int32, (1, LANES), 1)
        kall = jnp.concatenate([qkp_ref[:, ATTN_DIM:], qk_ref[:, ATTN_DIM:]], axis=0)
        vall = jnp.concatenate([vp_ref[...], v_ref[...]], axis=0).astype(CDT)
        for r in range(R):
            bias = bias_ref[jnp.where(i == 0, 1, 0)] if r == 0 else bias_ref[0]
            rows = slice(r * BLOCK, (r + 2) * BLOCK)
            qrows = slice(r * BLOCK, (r + 1) * BLOCK)
            lse_rows = jnp.zeros((BLOCK, LANES), F32)
            for kv in range(2):
                kdup = _dup_half(kall[rows], first, kv)
                vdup = _dup_half(vall[rows], first, kv)
                res = []
                for heads in _head_groups():
                    p, lse = _softmax_with_sink(_stack_heads(qk_ref, qrows, kv, heads, first), kdup,
                                                _sink_column(sink_ref, kv, heads), bias)
                    res += _row_blocks(_dot(p.astype(CDT), vdup), len(heads))
                    for b, col in enumerate(_row_blocks(lse, len(heads))):
                        lse_rows = jnp.where(lane == HEADS_PER_KV * kv + heads[b], col, lse_rows)
                o_ref[qrows, 2 * kv * LANES:(2 * kv + 1) * LANES] = jnp.where(first, res[0], res[1]).astype(o_ref.dtype)
                o_ref[qrows, (2 * kv + 1) * LANES:(2 * kv + 2) * LANES] = jnp.where(first, res[2], res[3]).astype(o_ref.dtype)
            lse_ref[qrows, :] = lse_rows

    bias = _attn_bias()
    prev = lambda i: (jnp.maximum(i * R - 1, 0), 0)
    return _pcall(
        body, name, (T // tq,),
        [pl.BlockSpec(memory_space=pltpu.SMEM), pl.BlockSpec(bias.shape, lambda i: (0, 0, 0)),
         pl.BlockSpec((tq, QK_DIM), lambda i: (i, 0)), pl.BlockSpec((BLOCK, QK_DIM), prev),
         pl.BlockSpec((tq, KV_DIM), lambda i: (i, 0)), pl.BlockSpec((BLOCK, KV_DIM), prev)],
        [pl.BlockSpec((tq, ATTN_DIM), lambda i: (i, 0)), pl.BlockSpec((tq, LANES), lambda i: (i, 0))],
        [jax.ShapeDtypeStruct((T, ATTN_DIM), CDT), jax.ShapeDtypeStruct((T, LANES), F32)],
        (sinks, bias, qkn, qkn, zv, zv), ("parallel",), side)


def _attn_bwd(qkn, zv, sinks, do, o, lse, name, side=None):
    T = qkn.shape[0]
    R = _attn_blocks(T)
    tq = R * BLOCK

    def body(sink_ref, bias_ref, qk_ref, qkp_ref, v_ref, vp_ref, do_ref, o_ref, lse_ref,
             dq_ref, dkc_ref, dkp_ref, dvc_ref, dvp_ref, ds_ref):
        i = pl.program_id(0)

        @pl.when(i == 0)
        def _():
            ds_ref[...] = jnp.zeros_like(ds_ref)

        first, _ = _lane_masks()
        lane = lax.broadcasted_iota(jnp.int32, (1, LANES), 1)
        kall = jnp.concatenate([qkp_ref[:, ATTN_DIM:], qk_ref[:, ATTN_DIM:]], axis=0)
        vall = jnp.concatenate([vp_ref[...], v_ref[...]], axis=0).astype(CDT)
        for r in range(R):
            bias = bias_ref[jnp.where(i == 0, 1, 0)] if r == 0 else bias_ref[0]
            rows = slice(r * BLOCK, (r + 2) * BLOCK)
            qrows = slice(r * BLOCK, (r + 1) * BLOCK)
            dk_out, dv_out = [], []
            lse_rows = lse_ref[qrows, :]
            for kv in range(2):
                kdup = _dup_half(kall[rows], first, kv)
                vdup = _dup_half(vall[rows], first, kv)
                dq_h = []
                dk_acc = jnp.zeros((2 * BLOCK, LANES), F32)
                dv_acc = jnp.zeros((2 * BLOCK, LANES), F32)
                for heads in _head_groups():
                    qst = _stack_heads(qk_ref, qrows, kv, heads, first)
                    dost = _stack_heads(do_ref, qrows, kv, heads, first)
                    lse_cols, delta_cols = [], []
                    for h in heads:
                        cols = slice((2 * kv + h // 2) * LANES, (2 * kv + h // 2 + 1) * LANES)
                        prod = do_ref[qrows, cols].astype(F32) * o_ref[qrows, cols].astype(F32)
                        own = first if h % 2 == 0 else jnp.logical_not(first)
                        delta_cols.append(jnp.sum(jnp.where(own, prod, 0.0), axis=-1, keepdims=True))
                        lse_cols.append(jnp.sum(jnp.where(lane == HEADS_PER_KV * kv + h, lse_rows, 0.0), axis=-1, keepdims=True))
                    lse_col = lse_cols[0] if len(heads) == 1 else jnp.concatenate(lse_cols, axis=0)
                    delta = delta_cols[0] if len(heads) == 1 else jnp.concatenate(delta_cols, axis=0)
                    p = jnp.exp(_dot_nt(qst, kdup) * ATTN_SCALE + bias - lse_col)
                    dsc = (p * (_dot_nt(dost, vdup) - delta)).astype(CDT)
                    psink = jnp.exp(_sink_column(sink_ref, kv, heads) - lse_col)
                    for b, term in enumerate(_row_blocks(psink * delta, len(heads))):
                        row = HEADS_PER_KV * kv + heads[b]
                        ds_ref[row:row + 1, :] += jnp.sum(term, axis=0, keepdims=True)
                    dq_h += _row_blocks(_dot(dsc, kdup) * ATTN_SCALE, len(heads))
                    dk_acc = dk_acc + _dot_tn(dsc, qst) * ATTN_SCALE
                    dv_acc = dv_acc + _dot_tn(p.astype(CDT), dost)
                dq_ref[qrows, 2 * kv * LANES:(2 * kv + 1) * LANES] = jnp.where(first, dq_h[0], dq_h[1])
                dq_ref[qrows, (2 * kv + 1) * LANES:(2 * kv + 2) * LANES] = jnp.where(first, dq_h[2], dq_h[3])
                dk_out.append(dk_acc + pltpu.roll(dk_acc, HEAD_DIM, axis=1))
                dv_out.append(dv_acc + pltpu.roll(dv_acc, HEAD_DIM, axis=1))
            dk = jnp.where(first, dk_out[0], dk_out[1])
            dv = jnp.where(first, dv_out[0], dv_out[1])
            dkp_ref[qrows, :] = dk[:BLOCK]
            dkc_ref[qrows, :] = dk[BLOCK:]
            dvp_ref[qrows, :] = dv[:BLOCK]
            dvc_ref[qrows, :] = dv[BLOCK:]

    bias = _attn_bias()
    prev = lambda i: (jnp.maximum(i * R - 1, 0), 0)
    kvrow = pl.BlockSpec((tq, KV_DIM), lambda i: (i, 0))
    qrow = pl.BlockSpec((tq, ATTN_DIM), lambda i: (i, 0))
    kv_shape = jax.ShapeDtypeStruct((T, KV_DIM), F32)
    return _pcall(
        body, name, (T // tq,),
        [pl.BlockSpec(memory_space=pltpu.SMEM), pl.BlockSpec(bias.shape, lambda i: (0, 0, 0)),
         pl.BlockSpec((tq, QK_DIM), lambda i: (i, 0)), pl.BlockSpec((BLOCK, QK_DIM), prev),
         kvrow, pl.BlockSpec((BLOCK, KV_DIM), prev), qrow, qrow, kvrow],
        [qrow, kvrow, kvrow, kvrow, kvrow, pl.BlockSpec((N_Q_HEADS, LANES), lambda i: (0, 0))],
        [jax.ShapeDtypeStruct((T, ATTN_DIM), F32), kv_shape, kv_shape, kv_shape, kv_shape,
         jax.ShapeDtypeStruct((N_Q_HEADS, LANES), F32)],
        (sinks, bias, qkn, qkn, zv, zv, do, o, lse), ("arbitrary",), side)


def _merge_fwd(pm, o, w_pb, w_ab, zg, name, side=None):
    T = pm.shape[0]
    tm = _tile(T, 512)

    def body(pm_ref, o_ref, wp_ref, wa_ref, zg_ref, a_ref, b_ref, m_ref):
        pmv, ov = pm_ref[...], o_ref[...]
        a = jnp.concatenate([_dot(pmv, wp_ref[j]) for j in range(N_CHIPS)], axis=1)
        b = jnp.concatenate([_dot(ov, wa_ref[j]) for j in range(N_CHIPS)], axis=1)
        gp = _sigmoid(zg_ref[:, :D_MODEL])
        ga = _sigmoid(zg_ref[:, D_MODEL:])
        a_ref[...] = a.astype(a_ref.dtype)
        b_ref[...] = b.astype(b_ref.dtype)
        m_ref[...] = (gp * a + ga * b).astype(m_ref.dtype)

    half = pl.BlockSpec((tm, POOL_DIM), lambda i: (i, 0))
    full = pl.BlockSpec((tm, D_MODEL), lambda i: (i, 0))
    wspec = pl.BlockSpec(w_pb.shape, lambda i: (0, 0, 0))
    out = jax.ShapeDtypeStruct((T, D_MODEL), CDT)
    return _pcall(body, name, (T // tm,), [half, half, wspec, wspec, pl.BlockSpec((tm, GATE_DIM), lambda i: (i, 0))],
                  [full, full, full], [out, out, out], (pm, o, w_pb, w_ab, zg), ("parallel",), side)


def _merge_bwd(dxo, w_out, a, b, zg, name):
    T = dxo.shape[0]
    tm = _tile(T, 512)

    def body(dx_ref, w_ref, a_ref, b_ref, zg_ref, da_ref, db_ref, dg_ref):
        dm = _dot_nt(dx_ref[...].astype(CDT), w_ref[...])
        gp = _sigmoid(zg_ref[:, :D_MODEL])
        ga = _sigmoid(zg_ref[:, D_MODEL:])
        da_ref[...] = (dm * gp).astype(da_ref.dtype)
        db_ref[...] = (dm * ga).astype(db_ref.dtype)
        dg_ref[:, :D_MODEL] = (dm * a_ref[...].astype(F32) * (gp * (1.0 - gp))).astype(dg_ref.dtype)
        dg_ref[:, D_MODEL:] = (dm * b_ref[...].astype(F32) * (ga * (1.0 - ga))).astype(dg_ref.dtype)

    full = pl.BlockSpec((tm, D_MODEL), lambda i: (i, 0))
    gate = pl.BlockSpec((tm, GATE_DIM), lambda i: (i, 0))
    out = jax.ShapeDtypeStruct((T, D_MODEL), CDT)
    return pl.pallas_call(
        body, name=name, grid=(T // tm,),
        in_specs=[full, pl.BlockSpec((D_MODEL, D_MODEL), lambda i: (0, 0)), full, full, gate],
        out_specs=[full, full, gate], out_shape=[out, out, jax.ShapeDtypeStruct((T, GATE_DIM), CDT)],
        compiler_params=_params("parallel"),
    )(dxo, w_out, a, b, zg)


def _adamw(w, g, m, v, name):
    Rr, C = w.shape
    tr = _tile(Rr, max(8, (1 << 19) // C // 8 * 8))

    def body(w_ref, g_ref, m_ref, v_ref, d_ref, nm_ref, nv_ref):
        gv = g_ref[...]
        nm = ADAM_B1 * m_ref[...] + (1.0 - ADAM_B1) * gv
        nv = ADAM_B2 * v_ref[...] + (1.0 - ADAM_B2) * (gv * gv)
        m_hat = nm / (1.0 - ADAM_B1 ** ADAM_STEP)
        v_hat = nv / (1.0 - ADAM_B2 ** ADAM_STEP)
        d_ref[...] = -ADAM_LR * (m_hat / (jnp.sqrt(v_hat) + ADAM_EPS) + ADAM_WD * w_ref[...])
        nm_ref[...] = nm
        nv_ref[...] = nv

    blk = pl.BlockSpec((tr, C), lambda i: (i, 0))
    out = jax.ShapeDtypeStruct((Rr, C), F32)
    return pl.pallas_call(
        body, name=name, grid=(Rr // tr,), in_specs=[blk] * 4, out_specs=[blk] * 3, out_shape=[out] * 3,
        compiler_params=_params("parallel"),
    )(w, g, m, v)


def _place():
    return lax.axis_index("x"), lax.axis_index("y"), lax.axis_index("c")


def _other_chip(x, y, d):
    return (1 - x if d & 2 else x), (1 - y if d & 1 else y)


def _rcopy(src, dst, ssem, rsem, dev):
    return pltpu.make_async_remote_copy(src_ref=src, dst_ref=dst, send_sem=ssem, recv_sem=rsem, device_id=dev,
                                        device_id_type=MESH)


def _row_half(rows, c):
    return pl.ds(c * (rows // 2), rows // 2)


def _is_wide(name):
    return name in WIDE


def _block(ref, wide, j, rows, n):
    if wide:
        return ref.at[rows, pl.ds(pl.multiple_of(j * n, LANES), n)]
    return ref.at[j, rows]


def _gathered_shape(shard, wide):
    _, a, n = shard.shape
    return jax.ShapeDtypeStruct((a, N_CHIPS * n) if wide else (N_CHIPS, a, n), shard.dtype)


def _gather_ici_side(shards, wides, l):
    k_of = lambda w, d: 3 * w + d - 1

    def issue(ins, outs, ssem, rsem):
        x, y, c = _place()
        cps = []
        for w, (shard, wide) in enumerate(zip(shards, wides)):
            _, a, n = shard.shape
            half = _row_half(a, c)
            for d in (1, 2, 3):
                px, py = _other_chip(x, y, d)
                cps.append(_rcopy(ins[w].at[l, half], _block(outs[w], wide, 2 * x + y, half, n),
                                  ssem.at[k_of(w, d)], rsem.at[k_of(w, d)], (px, py, c)))
        return cps

    return _Side(shards, [_gathered_shape(s_, wd) for s_, wd in zip(shards, wides)], 3 * len(shards), issue)


def _gather_d2d_side(shards, wides, gathered, l):
    nw = len(shards)

    def issue(ins, outs, ssem, rsem):
        x, y, c = _place()
        sibling = (x, y, 1 - c)
        cps = []
        for w, (shard, wide) in enumerate(zip(shards, wides)):
            _, a, n = shard.shape
            half = _row_half(a, c)
            for d in (1, 2, 3):
                px, py = _other_chip(x, y, d)
                k = 3 * w + d - 1
                got = _block(outs[w], wide, 2 * px + py, half, n)
                cps.append(_rcopy(got, got, ssem.at[k], rsem.at[k], sibling))
            cps.append(_rcopy(ins[nw + w].at[l], _block(outs[w], wide, 2 * x + y, pl.ds(0, a), n),
                              ssem.at[3 * nw + w], rsem.at[3 * nw + w], sibling))
        return cps

    return _Side(list(gathered) + list(shards), [jax.ShapeDtypeStruct(g.shape, g.dtype) for g in gathered], 4 * nw, issue,
                 aliases={w: w for w in range(nw)})


def _half_shape(g, wide):
    if wide:
        return jax.ShapeDtypeStruct((g.shape[0] // 2, g.shape[1]), g.dtype)
    return jax.ShapeDtypeStruct((N_CHIPS, g.shape[1] // 2, g.shape[2]), g.dtype)


def _reduce_sibling_side(gms, wides):
    def issue(ins, outs, ssem, rsem):
        x, y, c = _place()
        cps = []
        for w, (g, wide) in enumerate(zip(gms, wides)):
            src = ins[w].at[_row_half(g.shape[0], 1 - c)] if wide else ins[w].at[:, _row_half(g.shape[1], 1 - c)]
            cps.append(_rcopy(src, outs[w], ssem.at[w], rsem.at[w], (x, y, 1 - c)))
        return cps

    return _Side(gms, [_half_shape(g, wd) for g, wd in zip(gms, wides)], len(gms), issue)


def _reduce_chip_side(ps, wides):
    def slot_shape(p, wide):
        return jax.ShapeDtypeStruct((N_CHIPS, p.shape[0], p.shape[1] // N_CHIPS) if wide else p.shape, p.dtype)

    def issue(ins, outs, ssem, rsem):
        x, y, c = _place()
        cps = []
        for w, (p, wide) in enumerate(zip(ps, wides)):
            ah, n = (p.shape[0], p.shape[1] // N_CHIPS) if wide else p.shape[1:]
            for d in (1, 2, 3):
                px, py = _other_chip(x, y, d)
                k = 3 * w + d - 1
                cps.append(_rcopy(_block(ins[w], wide, 2 * px + py, pl.ds(0, ah), n), outs[w].at[2 * x + y],
                                  ssem.at[k], rsem.at[k], (px, py, c)))
        return cps

    return _Side(ps, [slot_shape(p, wd) for p, wd in zip(ps, wides)], 3 * len(ps), issue)


def _share_side(accs, items):
    def issue(ins, outs, ssem, rsem):
        x, y, c = _place()
        cps = []
        for k, (w, layer) in enumerate(items):
            mine = outs[w].at[layer, _row_half(accs[w].shape[1], c)]
            cps.append(_rcopy(mine, mine, ssem.at[k], rsem.at[k], (x, y, 1 - c)))
        return cps

    return _Side(accs, [jax.ShapeDtypeStruct(a.shape, a.dtype) for a in accs], len(items), issue,
                 aliases={w: w for w in range(len(accs))})


def _sum_rows(rows, b):
    return _tile(rows, max(16, (1 << 19) // b // 16 * 16), 16)


def _pair_sum(g, recv, wide, place, name):
    ah, b = recv.shape[-2:]
    ta = _sum_rows(ah, b)
    nr = ah // ta

    def body(p_ref, g_ref, r_ref, o_ref):
        o_ref[...] = (g_ref[...].astype(F32) + r_ref[...].astype(F32)).astype(o_ref.dtype)

    if wide:
        grid = (nr,)
        specs = [pl.BlockSpec((ta, b), lambda r, p: (p[0] * nr + r, 0)), pl.BlockSpec((ta, b), lambda r, p: (r, 0))]
        out_spec = pl.BlockSpec((ta, b), lambda r, p: (r, 0))
    else:
        grid = (N_CHIPS, nr)
        specs = [pl.BlockSpec((None, ta, b), lambda j, r, p: (j, p[0] * nr + r, 0)),
                 pl.BlockSpec((None, ta, b), lambda j, r, p: (j, r, 0))]
        out_spec = pl.BlockSpec((None, ta, b), lambda j, r, p: (j, r, 0))
    return pl.pallas_call(
        body, name=name,
        grid_spec=pltpu.PrefetchScalarGridSpec(num_scalar_prefetch=1, grid=grid, in_specs=specs, out_specs=out_spec),
        out_shape=jax.ShapeDtypeStruct(recv.shape, recv.dtype), compiler_params=_params(*["parallel"] * len(grid)),
    )(place, g, recv)


def _chip_sum(slots, part, wide, place, acc, l, name):
    _, ah, b = slots.shape
    ta = _sum_rows(ah, b)
    nr = ah // ta

    def body(p_ref, s_ref, own_ref, acc_ref, o_ref):
        j = p_ref[1]
        own = own_ref[...].astype(F32)
        term = [jnp.where(j == s_, own, s_ref[s_].astype(F32)) for s_ in range(N_CHIPS)]
        o_ref[...] = ((term[0] + term[1]) + term[2]) + term[3]

    own_spec = (pl.BlockSpec((ta, b), lambda r, p: (r, p[1])) if wide else
                pl.BlockSpec((None, ta, b), lambda r, p: (p[1], r, 0)))
    return pl.pallas_call(
        body, name=name,
        grid_spec=pltpu.PrefetchScalarGridSpec(
            num_scalar_prefetch=1, grid=(nr,),
            in_specs=[pl.BlockSpec((N_CHIPS, ta, b), lambda r, p: (0, r, 0)), own_spec, ANY],
            out_specs=pl.BlockSpec((None, ta, b), lambda r, p: (l, p[0] * nr + r, 0))),
        out_shape=jax.ShapeDtypeStruct(acc.shape, F32), input_output_aliases={3: 0},
        compiler_params=_params("parallel"),
    )(place, slots, part, acc)


def _all_reduce_small(v):
    Rr = v.shape[0]

    def body(v_ref, slots_ref, out_ref, ssem, rsem):
        x, y, c = _place()
        me = 4 * x + 2 * y + c
        slots_ref[pl.ds(me, 1)] = v_ref[...][None]
        cps = []
        for d in range(1, N_DEV):
            px, py = _other_chip(x, y, d >> 1)
            pc = 1 - c if d & 1 else c
            cps.append(_rcopy(v_ref, slots_ref.at[me], ssem.at[d - 1], rsem.at[d - 1], (px, py, pc)))
            cps[-1].start()
        for cp in cps:
            cp.wait_recv()
        for cp in cps:
            cp.wait_send()
        acc = slots_ref[0]
        for s in range(1, N_DEV):
            acc = acc + slots_ref[s]
        out_ref[...] = acc

    vm = pl.BlockSpec(memory_space=pltpu.VMEM)
    return pl.pallas_call(
        body, name="all_reduce_small", in_specs=[vm], out_specs=[vm, vm],
        out_shape=[jax.ShapeDtypeStruct((N_DEV, Rr, LANES), F32), jax.ShapeDtypeStruct((Rr, LANES), F32)],
        scratch_shapes=[pltpu.SemaphoreType.DMA((N_DEV - 1,)), pltpu.SemaphoreType.DMA((N_DEV - 1,))],
        compiler_params=pltpu.CompilerParams(vmem_limit_bytes=VMEM_LIMIT_BYTES),
    )(v)[1]


def _ffn_forward(x, p, tag, side_of):
    h, gu, act = _ffn_up(x, p[f"ln_{tag}"], p[f"w_{tag}_gu"], f"{tag}_up", side_of(f"{tag}_up"))
    x_out = _mm_nn(act, p[f"w_{tag}_down"], f"{tag}_down", F32, res=x, scale=0.5, side=side_of(f"{tag}_down"))
    return x_out, (x, h, gu, act)


def _row_blocks_of(dw):
    return dw.reshape(N_CHIPS, dw.shape[0] // N_CHIPS, dw.shape[1])


def _ffn_backward(dxo, saved, p, tag, side_of, grad):
    x, h, gu, act = saved
    dgu = _ffn_down_bwd(dxo, p[f"w_{tag}_down"], gu, f"{tag}_down_bwd", side_of(f"{tag}_down_bwd"))
    grad(f"w_{tag}_down", _row_blocks_of(_mm_tn(act, dxo, f"{tag}_dwd", scale=0.5)))
    grad(f"w_{tag}_gu", _mm_tn(h, dgu, f"{tag}_dwgu", tn_target=2816, tm_target=1024, side=side_of(f"{tag}_dwgu")))
    dx, d_ln = _mm_nt_norm_bwd([dgu], p[f"w_{tag}_gu"], x, p[f"ln_{tag}"], dxo, f"{tag}_dh_norm_bwd",
                               side_of(f"{tag}_dh_norm_bwd"))
    grad(f"ln_{tag}", d_ln[0])
    return dx


def _mixer_forward(x, p, tabs, side_of):
    h, zu, zqk, zv, zg = _mm_in(x, p["ln_mix"], p["w_in"], "mix_in", side_of("mix_in"))
    pm = _pool_fwd(zu, p["pool_w"], p["pool_scale"], "pool_fwd")
    qkn = _qk_fwd(zqk, p["gqk"], *tabs, "qk_fwd")
    o, lse = _attn_fwd(qkn, zv, p["sinks"], "attn_fwd", side_of("attn_fwd"))
    a, b, m = _merge_fwd(pm, o, p["w_pool_branch"], p["w_attn_branch"], zg, "merge_fwd", side_of("merge_fwd"))
    x_out = _mm_nn(m, p["w_out"], "mix_out", F32, res=x, scale=1.0)
    return x_out, (x, h, zu, zqk, zv, zg, pm, qkn, o, lse, a, b, m)


def _shift_up(v):
    return jnp.concatenate([v[BLOCK:], jnp.zeros((BLOCK, v.shape[1]), v.dtype)], axis=0)


def _mixer_backward(dxo, saved, p, tabs, side_of, grad):
    x, h, zu, zqk, zv, zg, pm, qkn, o, lse, a, b, m = saved
    d_a, d_b, dgl = _merge_bwd(dxo, p["w_out"], a, b, zg, "merge_bwd")
    grad("w_out", _row_blocks_of(_mm_tn(m, dxo, "mix_dwout")))
    dpm = _mm_nt_blocks(d_a, p["w_pool_branch"], "pool_branch_dx", CDT)
    grad("w_pool_branch", _mm_tn(pm, d_a, "pool_branch_dw", col_blocks=N_CHIPS))
    do = _mm_nt_blocks(d_b, p["w_attn_branch"], "attn_branch_dx", CDT)
    grad("w_attn_branch", _mm_tn(o, d_b, "attn_branch_dw", col_blocks=N_CHIPS))
    du, d_pool_w, d_pool_scale = _pool_bwd(zu, dpm, p["pool_w"], p["pool_scale"], "pool_bwd")
    grad("pool_w", d_pool_w)
    grad("pool_scale", d_pool_scale)
    dq, dkc, dkp, dvc, dvp, dsink = _attn_bwd(qkn, zv, p["sinks"], do, o, lse, "attn_bwd", side_of("attn_bwd"))
    dqk = jnp.concatenate([dq, dkc + _shift_up(dkp)], axis=1)
    dv = dvc + _shift_up(dvp)
    dzqk, dgqk = _qk_bwd(dqk, zqk, p["gqk"], *tabs, "qk_bwd")
    grad("q_norm", dgqk[0, :ATTN_DIM].reshape(N_Q_HEADS, HEAD_DIM).sum(axis=0))
    grad("k_norm", dgqk[0, ATTN_DIM:].reshape(KV_DIM // HEAD_DIM, HEAD_DIM).sum(axis=0))
    grad("sinks", -dsink[:, 0])
    dz = [du, dzqk, dv, dgl]
    grad("w_in", _blocks_from_full("w_in", _mm_tn_parts(h, dz, "mix_dwin")).astype(WIRE_DT))
    dx, d_ln = _mm_nt_norm_bwd(dz, p["w_in"], x, p["ln_mix"], dxo, "mix_dh_norm_bwd", side_of("mix_dh_norm_bwd"))
    grad("ln_mix", d_ln[0])
    return dx


class _NoComm:
    def __init__(self, layers):
        self.layers, self.grads = layers, [dict() for _ in layers]

    def weight(self, l, name):
        return self.layers[l][name]

    def side(self, phase, l, host):
        return None

    def grad(self, l, name, value):
        self.grads[l][name] = value


class _Layer:
    def __init__(self, hooks, l):
        self.hooks, self.l, self.got = hooks, l, {}

    def __getitem__(self, name):
        if name not in self.got:
            self.got[name] = self.hooks.weight(self.l, name)
        return self.got[name]


def _local_step(x, tgt, n_layers, hooks):
    T = x.shape[0]
    tabs = _rope_tables(T)
    saved, params = [], []
    for l in range(n_layers):
        p = _Layer(hooks, l)
        side_of = functools.partial(hooks.side, "fwd", l)
        x, s1 = _ffn_forward(x, p, "ffn1", side_of)
        x, s2 = _mixer_forward(x, p, tabs, side_of)
        x, s3 = _ffn_forward(x, p, "ffn2", side_of)
        saved.append((s1, s2, s3))
        params.append(p)
    dx, loss = _loss_head(x, tgt, "loss_head")
    for l in reversed(range(n_layers)):
        p = params[l]
        s1, s2, s3 = saved[l]
        side_of = functools.partial(hooks.side, "bwd", l)
        grad = functools.partial(hooks.grad, l)
        dx = _ffn_backward(dx, s3, p, "ffn2", side_of, grad)
        dx = _mixer_backward(dx, s2, p, tabs, side_of, grad)
        dx = _ffn_backward(dx, s1, p, "ffn1", side_of, grad)
    return loss, dx


def _full_from_blocks(name, blocks):
    if name in COL_SHARDED:
        return jnp.transpose(blocks, (1, 0, 2)).reshape(blocks.shape[1], N_CHIPS * blocks.shape[2])
    return blocks.reshape(N_CHIPS * blocks.shape[1], blocks.shape[2])


def _blocks_from_full(name, full):
    K, N = full.shape
    if name in COL_SHARDED:
        return jnp.transpose(full.reshape(K, N_CHIPS, N // N_CHIPS), (1, 0, 2))
    return full.reshape(N_CHIPS, K // N_CHIPS, N)


JOBS = {"a": ("w_ffn1_gu", "w_ffn1_down"), "b": ("w_in", "w_pool_branch", "w_attn_branch", "w_out"),
        "c": ("w_ffn2_gu", "w_ffn2_down")}
GATHER_PLAN = {"ffn1_up": ("ici", "b", JOBS["b"], 0), "ffn1_down": ("d2d", "b", JOBS["b"], 0),
               "mix_in": ("ici", "c", JOBS["c"][:1], 0), "attn_fwd": ("ici", "c", JOBS["c"][1:], 0),
               "merge_fwd": ("d2d", "c", JOBS["c"], 0),
               "ffn2_up": ("ici", "a", JOBS["a"], 1), "ffn2_down": ("d2d", "a", JOBS["a"], 1)}
REDUCE_PLAN = {"ffn2_down_bwd": ("sibling", "a", 1), "ffn2_dwgu": ("chip", "a", 1),
               "ffn2_dh_norm_bwd": ("sibling", "c", 0), "attn_bwd": ("chip", "c", 0),
               "mix_dh_norm_bwd": ("sibling", "b", 0), "ffn1_down_bwd": ("chip", "b", 0)}
SHARE_HOST = "ffn1_dwgu"


class _Exchange:
    def __init__(self, shards, small, place, n_layers):
        self.shards, self.small, self.place, self.n_layers = shards, small, place, n_layers
        first, wides = [shards[n] for n in JOBS["a"]], [_is_wide(n) for n in JOBS["a"]]
        got = _run_side(_gather_ici_side(first, wides, 0), "gather_ici")
        got = _run_side(_gather_d2d_side(first, wides, got, 0), "gather_d2d")
        self.blocks = {(n, 0): g for n, g in zip(JOBS["a"], got)}
        self.landed = {}
        self.handed = []
        self.acc = {n: lax.empty(shards[n].shape, F32) for n in BIG}
        self.grads = [dict() for _ in range(n_layers)]
        self.reduce = {}
        self.summed = set()
        self.unshared, self.sharing = [], None

    def weight(self, l, name):
        if name not in BIG:
            return self.small(l)[name]
        for names, layer, done in self.handed:
            self.blocks.update({(n, layer): g for n, g in zip(names, done.outs)})
        self.handed.clear()
        blocks = self.blocks.pop((name, l))
        return blocks if name in USED_AS_BLOCKS + WIDE else _full_from_blocks(name, blocks)

    def _gather_side(self, l, host):
        step, job, names, ahead = GATHER_PLAN[host]
        layer = l + ahead
        if layer >= self.n_layers:
            return None
        if step == "ici":
            side = _gather_ici_side([self.shards[n] for n in names], [_is_wide(n) for n in names], layer)
            self.landed.setdefault((job, layer), []).append((names, side))
            return side
        names, gathered = JOBS[job], {}
        for part_names, side in self.landed.pop((job, layer)):
            gathered.update(zip(part_names, side.outs))
        done = _gather_d2d_side([self.shards[n] for n in names], [_is_wide(n) for n in names],
                                [gathered[n] for n in names], layer)
        self.handed.append((names, layer, done))
        return done

    def grad(self, l, name, value):
        self.grads[l][name] = value

    def _reduce_side(self, l, host):
        step, job, ahead = REDUCE_PLAN[host]
        layer = l + ahead
        if layer >= self.n_layers:
            return None
        return self._reduce_step(step, job, layer)

    def _reduce_step(self, step, job, layer):
        if step == "sibling":
            st = self.reduce[(job, layer)] = dict(gm=[self.grads[layer][n] for n in JOBS[job]],
                                                  wide=[_is_wide(n) for n in JOBS[job]])
            st["sibling"] = _reduce_sibling_side(st["gm"], st["wide"])
            return st["sibling"]
        st = self.reduce[(job, layer)]
        st["part"] = [_pair_sum(g, r, wd, self.place, "grad_pair_sum")
                      for g, r, wd in zip(st["gm"], st["sibling"].outs, st["wide"])]
        st["chip"] = _reduce_chip_side(st["part"], st["wide"])
        return st["chip"]

    def _chip_sums(self):
        if self.sharing is not None:
            self.acc.update(zip(BIG, self.sharing.outs))
            self.sharing = None
        for (job, layer), st in self.reduce.items():
            if (job, layer) not in self.summed and "chip" in st and st["chip"].outs is not None:
                self.summed.add((job, layer))
                for n, slots, part, wd in zip(JOBS[job], st["chip"].outs, st["part"], st["wide"]):
                    self.acc[n] = _chip_sum(slots, part, wd, self.place, self.acc[n], layer, "grad_chip_sum")
                    self.unshared.append((BIG.index(n), layer))

    def _share(self):
        side = _share_side([self.acc[n] for n in BIG], self.unshared)
        self.unshared = []
        return side

    def side(self, phase, l, host):
        if phase == "fwd":
            return self._gather_side(l, host) if host in GATHER_PLAN else None
        self._chip_sums()
        if host == SHARE_HOST and self.unshared:
            self.sharing = self._share()
            return self.sharing
        return self._reduce_side(l, host) if host in REDUCE_PLAN else None

    def reduced(self):
        _run_side(self._reduce_step("sibling", "a", 0), "grad_sibling_exchange")
        _run_side(self._reduce_step("chip", "a", 0), "grad_chip_exchange")
        self._chip_sums()
        return dict(zip(BIG, _run_side(self._share(), "grad_sibling_share")))


def _pack_small(parts):
    rows, spans, lo = [], [], 0
    for v in parts:
        flat = v.reshape(-1)
        nrow = -(-flat.shape[0] // LANES)
        flat = jnp.pad(flat, (0, nrow * LANES - flat.shape[0]))
        rows.append(flat.reshape(nrow, LANES))
        spans.append((lo, nrow))
        lo += nrow
    pad = -lo % 8
    if pad:
        rows.append(jnp.zeros((pad, LANES), F32))
    return jnp.concatenate(rows, axis=0), spans


def _unpack_small(packed, spans, shapes):
    out = []
    for (lo, nrow), shape in zip(spans, shapes):
        size = 1
        for s in shape:
            size *= s
        out.append(packed[lo:lo + nrow].reshape(-1)[:size].reshape(shape))
    return out


def kernel(x, ln_ffn1, w_ffn1_gu, w_ffn1_down, ln_mix, w_in, pool_w, pool_scale, w_pool_branch, q_norm, k_norm, sinks, w_attn_branch, w_out, ln_ffn2, w_ffn2_gu, w_ffn2_down, loss_target, m_ln_ffn1, m_w_ffn1_gu, m_w_ffn1_down, m_ln_mix, m_w_in, m_pool_w, m_pool_scale, m_w_pool_branch, m_q_norm, m_k_norm, m_sinks, m_w_attn_branch, m_w_out, m_ln_ffn2, m_w_ffn2_gu, m_w_ffn2_down, v_ln_ffn1, v_w_ffn1_gu, v_w_ffn1_down, v_ln_mix, v_w_in, v_pool_w, v_pool_scale, v_w_pool_branch, v_q_norm, v_k_norm, v_sinks, v_w_attn_branch, v_w_out, v_ln_ffn2, v_w_ffn2_gu, v_w_ffn2_down):
    w = dict(ln_ffn1=ln_ffn1, w_ffn1_gu=w_ffn1_gu, w_ffn1_down=w_ffn1_down, ln_mix=ln_mix, w_in=w_in, pool_w=pool_w,
             pool_scale=pool_scale, w_pool_branch=w_pool_branch, q_norm=q_norm, k_norm=k_norm, sinks=sinks,
             w_attn_branch=w_attn_branch, w_out=w_out, ln_ffn2=ln_ffn2, w_ffn2_gu=w_ffn2_gu, w_ffn2_down=w_ffn2_down)
    mom = dict(ln_ffn1=m_ln_ffn1, w_ffn1_gu=m_w_ffn1_gu, w_ffn1_down=m_w_ffn1_down, ln_mix=m_ln_mix, w_in=m_w_in,
               pool_w=m_pool_w, pool_scale=m_pool_scale, w_pool_branch=m_w_pool_branch, q_norm=m_q_norm, k_norm=m_k_norm,
               sinks=m_sinks, w_attn_branch=m_w_attn_branch, w_out=m_w_out, ln_ffn2=m_ln_ffn2, w_ffn2_gu=m_w_ffn2_gu,
               w_ffn2_down=m_w_ffn2_down)
    var = dict(ln_ffn1=v_ln_ffn1, w_ffn1_gu=v_w_ffn1_gu, w_ffn1_down=v_w_ffn1_down, ln_mix=v_ln_mix, w_in=v_w_in,
               pool_w=v_pool_w, pool_scale=v_pool_scale, w_pool_branch=v_w_pool_branch, q_norm=v_q_norm, k_norm=v_k_norm,
               sinks=v_sinks, w_attn_branch=v_w_attn_branch, w_out=v_w_out, ln_ffn2=v_ln_ffn2, w_ffn2_gu=v_w_ffn2_gu,
               w_ffn2_down=v_w_ffn2_down)
    L = ln_ffn1.shape[0]

    def small(l):
        return dict(ln_ffn1=ln_ffn1[l], ln_mix=ln_mix[l], ln_ffn2=ln_ffn2[l], pool_w=pool_w[l].astype(CDT),
                    pool_scale=pool_scale[l], sinks=sinks[l],
                    gqk=jnp.concatenate([jnp.tile(q_norm[l], N_Q_HEADS), jnp.tile(k_norm[l], KV_DIM // HEAD_DIM)]).reshape(1, QK_DIM))

    place = jnp.stack([lax.axis_index("c"), 2 * lax.axis_index("x") + lax.axis_index("y")]).astype(jnp.int32)
    hooks = _Exchange({n: w[n].astype(CDT) for n in BIG}, small, place, L)
    loss_part, grad_x = _local_step(x[0], loss_target[0], L, hooks)
    g_big = hooks.reduced()
    grads = hooks.grads

    small_parts = [jnp.stack([g[n] for g in grads]) for n in SMALL] + [loss_part]
    packed, spans = _pack_small(small_parts)
    summed = _all_reduce_small(packed)
    *g_small_list, loss_sum = _unpack_small(summed, spans, [w[n].shape for n in SMALL] + [(1, 1)])
    g_small = dict(zip(SMALL, g_small_list))
    loss = loss_sum[0, 0]

    grad_out, delta, new_m, new_v = {}, {}, {}, {}
    for n in BIG:
        shape = w[n].shape
        flat = (shape[0] * shape[1], shape[2])
        grad_out[n] = g_big[n]
        d, nm, nv = _adamw(w[n].reshape(flat), g_big[n].reshape(flat), mom[n].reshape(flat), var[n].reshape(flat), "adamw")
        delta[n], new_m[n], new_v[n] = d.reshape(shape), nm.reshape(shape), nv.reshape(shape)
    pw, _ = _pack_small([w[n] for n in SMALL])
    pg, sp = _pack_small([g_small[n] for n in SMALL])
    pm_, _ = _pack_small([mom[n] for n in SMALL])
    pv, _ = _pack_small([var[n] for n in SMALL])
    d, nm, nv = _adamw(pw, pg, pm_, pv, "adamw_small")
    shapes = [w[n].shape for n in SMALL]
    for n, dv, mv, vv in zip(SMALL, _unpack_small(d, sp, shapes), _unpack_small(nm, sp, shapes), _unpack_small(nv, sp, shapes)):
        grad_out[n], delta[n], new_m[n], new_v[n] = g_small[n], dv, mv, vv

    return (loss, grad_x[None], *[grad_out[n] for n in WEIGHTS], *[delta[n] for n in WEIGHTS],
            *[new_m[n] for n in WEIGHTS], *[new_v[n] for n in WEIGHTS])
```

```python
import functools
import math

import jax
import jax.numpy as jnp
from jax import lax
from jax.experimental import pallas as pl
from jax.experimental.pallas import tpu as pltpu

F32 = jnp.float32
CDT = jnp.bfloat16
WIRE_DT = jnp.bfloat16

D_MODEL = 1024
POOL_WINDOWS = (2, 4, 8, 16)
POOL_WMAX = 16
GROUP = 128
POOL_DIM = 512
HEAD_DIM = 64
N_Q_HEADS = 8
ATTN_DIM = 512
KV_DIM = 128
QK_DIM = ATTN_DIM + KV_DIM
GATE_DIM = 2 * D_MODEL
BLOCK = 128
ROPE_THETA = 500000.0
ROT_DIM = 16
EPS = 1e-6
ATTN_SCALE = HEAD_DIM ** -0.5

ADAM_LR = 0.001
ADAM_B1 = 0.9
ADAM_B2 = 0.999
ADAM_EPS = 1e-08
ADAM_WD = 0.01
ADAM_STEP = 10

N_CHIPS = 4
N_DEV = 8
LANES = 128
VMEM_LIMIT_BYTES = 48 * 1024 * 1024

MESH = pl.DeviceIdType.MESH
ANY = pl.BlockSpec(memory_space=pl.ANY)

BIG = ("w_ffn1_gu", "w_ffn1_down", "w_in", "w_pool_branch", "w_attn_branch", "w_out", "w_ffn2_gu", "w_ffn2_down")
COL_SHARDED = ("w_ffn1_gu", "w_in", "w_pool_branch", "w_attn_branch", "w_ffn2_gu")
USED_AS_BLOCKS = ("w_pool_branch", "w_attn_branch")
WIDE = ("w_ffn1_gu", "w_ffn2_gu")
SMALL = ("ln_ffn1", "ln_mix", "pool_w", "pool_scale", "q_norm", "k_norm", "sinks", "ln_ffn2")
WEIGHTS = ("ln_ffn1", "w_ffn1_gu", "w_ffn1_down", "ln_mix", "w_in", "pool_w", "pool_scale", "w_pool_branch",
           "q_norm", "k_norm", "sinks", "w_attn_branch", "w_out", "ln_ffn2", "w_ffn2_gu", "w_ffn2_down")


def _tile(n, target, mult=8):
    if n <= target:
        return n
    for t in range(target - target % mult, 0, -mult):
        if n % t == 0:
            return t
    raise ValueError((n, target, mult))


def _params(*sem):
    return pltpu.CompilerParams(dimension_semantics=sem, vmem_limit_bytes=VMEM_LIMIT_BYTES)


def _sigmoid(v):
    return 0.5 * jnp.tanh(0.5 * v) + 0.5


def _dot(a, b):
    return jnp.dot(a, b, preferred_element_type=F32)


def _dot_nt(a, b):
    return lax.dot_general(a, b, (((1,), (1,)), ((), ())), preferred_element_type=F32)


def _dot_tn(a, b):
    return lax.dot_general(a, b, (((0,), (0,)), ((), ())), preferred_element_type=F32)


class _Side:
    def __init__(self, ins, out_shapes, n_sems, issue, aliases=None):
        self.ins, self.out_shapes, self.n_sems, self.issue = list(ins), list(out_shapes), n_sems, issue
        self.aliases = dict(aliases or {})
        self.outs = None


def _pcall(body, name, grid, in_specs, out_specs, out_shape, args, dims, side=None, scratch=()):
    scratch = list(scratch)
    if side is None:
        return pl.pallas_call(body, name=name, grid=grid, in_specs=in_specs, out_specs=out_specs, out_shape=out_shape,
                              scratch_shapes=scratch, compiler_params=_params(*dims))(*args)
    n_in, n_out, s_in, s_out = len(in_specs), len(out_specs), len(side.ins), len(side.out_shapes)

    def wrapped(*refs):
        main_in, side_in = refs[:n_in], refs[n_in:n_in + s_in]
        main_out = refs[n_in + s_in:n_in + s_in + n_out]
        side_out = refs[n_in + s_in + n_out:n_in + s_in + n_out + s_out]
        rest = refs[n_in + s_in + n_out + s_out:]
        main_scratch, (ssem, rsem) = rest[:len(scratch)], rest[len(scratch):]
        ids = [pl.program_id(ax) for ax in range(len(grid))]
        first = functools.reduce(jnp.logical_and, [i == 0 for i in ids])
        last = functools.reduce(jnp.logical_and, [i == g - 1 for i, g in zip(ids, grid)])

        @pl.when(first)
        def _():
            for cp in side.issue(side_in, side_out, ssem, rsem):
                cp.start()

        body(*main_in, *main_out, *main_scratch)

        @pl.when(last)
        def _():
            cps = side.issue(side_in, side_out, ssem, rsem)
            for cp in cps:
                cp.wait_recv()
            for cp in cps:
                cp.wait_send()

    outs = pl.pallas_call(
        wrapped, name=name, grid=grid, in_specs=list(in_specs) + [ANY] * s_in, out_specs=list(out_specs) + [ANY] * s_out,
        out_shape=list(out_shape) + side.out_shapes,
        input_output_aliases={n_in + i: n_out + o for i, o in side.aliases.items()},
        scratch_shapes=scratch + [pltpu.SemaphoreType.DMA((side.n_sems,))] * 2,
        compiler_params=_params(*["arbitrary"] * len(grid)),
    )(*args, *side.ins)
    side.outs = list(outs[n_out:])
    return list(outs[:n_out])


def _run_side(side, name):
    s_in = len(side.ins)

    def body(*refs):
        ssem, rsem = refs[s_in + len(side.out_shapes):]
        cps = side.issue(refs[:s_in], refs[s_in:s_in + len(side.out_shapes)], ssem, rsem)
        for cp in cps:
            cp.start()
        for cp in cps:
            cp.wait_recv()
        for cp in cps:
            cp.wait_send()

    side.outs = list(pl.pallas_call(
        body, name=name, in_specs=[ANY] * s_in, out_specs=[ANY] * len(side.out_shapes), out_shape=side.out_shapes,
        input_output_aliases=side.aliases, scratch_shapes=[pltpu.SemaphoreType.DMA((side.n_sems,))] * 2,
    )(*side.ins))
    return side.outs


def _loss_head(y, tgt, name):
    T, Dm = y.shape
    tm = _tile(T, 512)

    def body(y_ref, t_ref, dy_ref, loss_ref):
        @pl.when(pl.program_id(0) == 0)
        def _():
            loss_ref[...] = jnp.zeros_like(loss_ref)

        diff = y_ref[...] - t_ref[...]
        dy_ref[...] = diff * (1.0 / Dm)
        part = jnp.sum(jnp.mean(diff * diff, axis=-1, keepdims=True), axis=0, keepdims=True)
        loss_ref[...] += 0.5 * part

    row = pl.BlockSpec((tm, Dm), lambda i: (i, 0))
    one = pl.BlockSpec((1, 1), lambda i: (0, 0))
    return pl.pallas_call(
        body, name=name, grid=(T // tm,),
        in_specs=[row, row], out_specs=[row, one],
        out_shape=[jax.ShapeDtypeStruct((T, Dm), F32), jax.ShapeDtypeStruct((1, 1), F32)],
        compiler_params=_params("arbitrary"),
    )(y, tgt)


def _mm_nn(a, b, name, out_dtype, res=None, scale=1.0, tm_target=512, side=None):
    M, K = a.shape
    N = b.shape[1]
    tm = _tile(M, tm_target)

    def body(a_ref, b_ref, *rest):
        acc = _dot(a_ref[...].astype(CDT), b_ref[...])
        if res is None:
            (o_ref,) = rest
        else:
            r_ref, o_ref = rest
            acc = r_ref[...] + scale * acc
        o_ref[...] = acc.astype(o_ref.dtype)

    in_specs = [pl.BlockSpec((tm, K), lambda i: (i, 0)), pl.BlockSpec((K, N), lambda i: (0, 0))]
    args = [a, b]
    if res is not None:
        in_specs.append(pl.BlockSpec((tm, N), lambda i: (i, 0)))
        args.append(res)
    return _pcall(body, name, (M // tm,), in_specs, [pl.BlockSpec((tm, N), lambda i: (i, 0))],
                  [jax.ShapeDtypeStruct((M, N), out_dtype)], args, ("parallel",), side)[0]


def _mm_nt_blocks(a, b4, name, out_dtype, tm_target=512):
    M, K = a.shape
    nb, N, Kb = b4.shape
    tm = _tile(M, tm_target)

    def body(a_ref, b_ref, o_ref):
        acc = _dot_nt(a_ref[:, :Kb].astype(CDT), b_ref[0])
        for j in range(1, nb):
            acc = acc + _dot_nt(a_ref[:, j * Kb:(j + 1) * Kb].astype(CDT), b_ref[j])
        o_ref[...] = acc.astype(o_ref.dtype)

    return pl.pallas_call(
        body, name=name, grid=(M // tm,),
        in_specs=[pl.BlockSpec((tm, K), lambda i: (i, 0)), pl.BlockSpec(b4.shape, lambda i: (0, 0, 0))],
        out_specs=pl.BlockSpec((tm, N), lambda i: (i, 0)),
        out_shape=jax.ShapeDtypeStruct((M, N), out_dtype), compiler_params=_params("parallel"),
    )(a, b4)


def _mm_tn(x, dy, name, scale=1.0, col_blocks=1, tn_target=1664, tm_target=1408, tk_target=1024, side=None):
    T, M = x.shape
    split = dy.ndim == 3
    Nh = dy.shape[-1]
    N = 2 * Nh if split else Nh
    nb = N // col_blocks
    whole = col_blocks > 1 and not split and N <= tn_target
    tm = _tile(M, tm_target, LANES)
    tn = N if whole else _tile(math.gcd(Nh, nb), tn_target, LANES)
    tk = _tile(T, tk_target)
    nk = T // tk
    njh, njb = Nh // tn, max(nb // tn, 1)

    def body(x_ref, dy_ref, o_ref, acc_ref):
        k = pl.program_id(2)

        @pl.when(k == 0)
        def _():
            acc_ref[...] = jnp.zeros_like(acc_ref)

        acc_ref[...] += _dot_tn(x_ref[...].astype(CDT), dy_ref[...].astype(CDT))

        @pl.when(k == nk - 1)
        def _():
            res = (acc_ref[...] if scale == 1.0 else scale * acc_ref[...]).astype(o_ref.dtype)
            if whole:
                for b in range(col_blocks):
                    o_ref[b] = res[:, b * nb:(b + 1) * nb]
            else:
                o_ref[...] = res

    if split:
        dy_spec = pl.BlockSpec((None, tk, tn), lambda i, j, k: (j // njh, k, j % njh))
    else:
        dy_spec = pl.BlockSpec((tk, tn), lambda i, j, k: (k, j))
    if col_blocks == 1:
        out_spec, out_dims = pl.BlockSpec((tm, tn), lambda i, j, k: (i, j)), (M, N)
    elif whole:
        out_spec, out_dims = pl.BlockSpec((col_blocks, tm, nb), lambda i, j, k: (0, i, 0)), (col_blocks, M, nb)
    else:
        out_spec, out_dims = pl.BlockSpec((None, tm, tn), lambda i, j, k: (j // njb, i, j % njb)), (col_blocks, M, nb)
    return _pcall(body, name, (M // tm, N // tn, nk), [pl.BlockSpec((tk, tm), lambda i, j, k: (k, i)), dy_spec],
                  [out_spec], [jax.ShapeDtypeStruct(out_dims, WIRE_DT)], (x, dy), ("parallel", "parallel", "arbitrary"),
                  side, [pltpu.VMEM((tm, tn), F32)])[0]


def _mm_tn_parts(x, parts, name):
    T, M = x.shape
    widths = [p.shape[1] for p in parts]
    N = sum(widths)
    tk = _tile(T, 512)

    def body(x_ref, *refs):
        o_ref = refs[-1]

        @pl.when(pl.program_id(0) == 0)
        def _():
            o_ref[...] = jnp.zeros_like(o_ref)

        xv = x_ref[...].astype(CDT)
        lo = 0
        for p_ref, wd in zip(refs[:-1], widths):
            o_ref[:, lo:lo + wd] += _dot_tn(xv, p_ref[...].astype(CDT))
            lo += wd

    return pl.pallas_call(
        body, name=name, grid=(T // tk,),
        in_specs=[pl.BlockSpec((tk, M), lambda k: (k, 0))] + [pl.BlockSpec((tk, wd), lambda k: (k, 0)) for wd in widths],
        out_specs=pl.BlockSpec((M, N), lambda k: (0, 0)),
        out_shape=jax.ShapeDtypeStruct((M, N), F32), compiler_params=_params("arbitrary"),
    )(x, *parts)


def _rmsnorm_rows(x_ref, g_ref):
    xv = x_ref[...]
    r = lax.rsqrt(jnp.mean(xv * xv, axis=-1, keepdims=True) + EPS)
    return (xv * r * g_ref[...]).astype(CDT)


def _ffn_up(x, ln, wgu, name, side=None):
    T, Dm = x.shape
    Fd = wgu.shape[1] // 2
    tm = _tile(T, 256)

    def body(x_ref, ln_ref, wg_ref, wu_ref, h_ref, gu_ref, a_ref):
        hv = _rmsnorm_rows(x_ref, ln_ref)
        h_ref[...] = hv
        g = _dot(hv, wg_ref[...])
        u = _dot(hv, wu_ref[...])
        sg = _sigmoid(g)
        silu = g * sg
        a_ref[...] = (silu * u).astype(a_ref.dtype)
        gu_ref[0] = (0.5 * u * (sg * (1.0 + g * (1.0 - sg)))).astype(gu_ref.dtype)
        gu_ref[1] = (0.5 * silu).astype(gu_ref.dtype)

    row = pl.BlockSpec((tm, Dm), lambda i: (i, 0))
    return _pcall(
        body, name, (T // tm,),
        [row, pl.BlockSpec((1, Dm), lambda i: (0, 0)),
         pl.BlockSpec((Dm, Fd), lambda i: (0, 0), pipeline_mode=pl.Buffered(1)),
         pl.BlockSpec((Dm, Fd), lambda i: (0, 1), pipeline_mode=pl.Buffered(1))],
        [row, pl.BlockSpec((2, tm, Fd), lambda i: (0, i, 0)), pl.BlockSpec((tm, Fd), lambda i: (i, 0))],
        [jax.ShapeDtypeStruct((T, Dm), CDT), jax.ShapeDtypeStruct((2, T, Fd), CDT), jax.ShapeDtypeStruct((T, Fd), CDT)],
        (x, ln.reshape(1, Dm), wgu, wgu), ("parallel",), side)


def _ffn_down_bwd(dxo, wd, gu, name, side=None):
    T, Dm = dxo.shape
    Fd = wd.shape[0]
    tm = _tile(T, 256)

    def body(dx_ref, wd_ref, gu_ref, dgu_ref):
        da = _dot_nt(dx_ref[...].astype(CDT), wd_ref[...])
        dgu_ref[0] = (da * gu_ref[0].astype(F32)).astype(dgu_ref.dtype)
        dgu_ref[1] = (da * gu_ref[1].astype(F32)).astype(dgu_ref.dtype)

    gu_spec = pl.BlockSpec((2, tm, Fd), lambda i: (0, i, 0))
    return _pcall(
        body, name, (T // tm,),
        [pl.BlockSpec((tm, Dm), lambda i: (i, 0)),
         pl.BlockSpec((Fd, Dm), lambda i: (0, 0), pipeline_mode=pl.Buffered(1)), gu_spec],
        [gu_spec], [jax.ShapeDtypeStruct((2, T, Fd), CDT)],
        (dxo, wd, gu), ("parallel",), side)[0]


def _mm_nt_norm_bwd(a_parts, b, x, g, dres, name, side=None):
    T, Dm = x.shape
    tm = _tile(T, 256)

    def b_cols(b_ref, lo, wd):
        if b.ndim == 2:
            return [(0, wd, b_ref[:, lo:lo + wd])]
        kb = b.shape[2]
        return [(j * kb - lo, kb, b_ref[j]) for j in range(lo // kb, (lo + wd) // kb)]

    def body(*refs):
        a_refs, (b_ref, x_ref, g_ref, dres_ref, dx_ref, dg_ref) = refs[:len(a_parts)], refs[len(a_parts):]

        @pl.when(pl.program_id(0) == 0)
        def _():
            dg_ref[...] = jnp.zeros_like(dg_ref)

        dh, lo = None, 0
        for a_ref, part in zip(a_refs, a_parts):
            slabs = [a_ref] if part.ndim == 2 else [a_ref.at[s_] for s_ in range(part.shape[0])]
            for slab in slabs:
                for off, wd, bv in b_cols(b_ref, lo, part.shape[-1]):
                    term = _dot_nt(slab[:, off:off + wd].astype(CDT), bv)
                    dh = term if dh is None else dh + term
                lo += part.shape[-1]
        xv = x_ref[...]
        r = lax.rsqrt(jnp.mean(xv * xv, axis=-1, keepdims=True) + EPS)
        xh = xv * r
        dg_ref[...] += jnp.sum(dh * xh, axis=0, keepdims=True)
        dxh = dh * g_ref[...]
        dx_ref[...] = dres_ref[...] + r * (dxh - xh * jnp.mean(dxh * xh, axis=-1, keepdims=True))

    row = pl.BlockSpec((tm, Dm), lambda i: (i, 0))
    vec = pl.BlockSpec((1, Dm), lambda i: (0, 0))
    a_specs = [pl.BlockSpec((tm, p.shape[1]), lambda i: (i, 0)) if p.ndim == 2 else
               pl.BlockSpec((p.shape[0], tm, p.shape[2]), lambda i: (0, i, 0)) for p in a_parts]
    b_spec = pl.BlockSpec(b.shape, lambda i: (0,) * b.ndim, pipeline_mode=pl.Buffered(1))
    return _pcall(body, name, (T // tm,), a_specs + [b_spec, row, vec, row], [row, vec],
                  [jax.ShapeDtypeStruct((T, Dm), F32), jax.ShapeDtypeStruct((1, Dm), F32)],
                  (*a_parts, b, x, g.reshape(1, Dm), dres), ("arbitrary",), side)


def _mm_in(x, ln, w_in, name, side=None):
    T, Dm = x.shape
    tm = _tile(T, 256)
    widths = (POOL_DIM, QK_DIM, KV_DIM, GATE_DIM)

    def body(x_ref, ln_ref, w_ref, h_ref, *outs):
        hv = _rmsnorm_rows(x_ref, ln_ref)
        h_ref[...] = hv
        z = _dot(hv, w_ref[...])
        lo = 0
        for o_ref, wd in zip(outs, widths):
            o_ref[...] = z[:, lo:lo + wd]
            lo += wd

    row = pl.BlockSpec((tm, Dm), lambda i: (i, 0))
    return _pcall(body, name, (T // tm,),
                  [row, pl.BlockSpec((1, Dm), lambda i: (0, 0)),
                   pl.BlockSpec(w_in.shape, lambda i: (0, 0), pipeline_mode=pl.Buffered(1))],
                  [row] + [pl.BlockSpec((tm, wd), lambda i: (i, 0)) for wd in widths],
                  [jax.ShapeDtypeStruct((T, Dm), CDT)] + [jax.ShapeDtypeStruct((T, wd), F32) for wd in widths],
                  (x, ln.reshape(1, Dm), w_in), ("parallel",), side)


def _window_mean_minus_token(ext, u, g, w, pos):
    sl = slice(g * GROUP, (g + 1) * GROUP)
    s = ext[:, sl]
    span = 1
    while span < w:
        s = s + pltpu.roll(s, span, axis=0)
        span *= 2
    cnt = jnp.minimum(pos + 1, w).astype(F32)
    return s[POOL_WMAX:, :] / cnt - u[:, sl]


def _pool_fwd(zu, pool_w, scale, name):
    T = zu.shape[0]
    tm = _tile(T, 512, POOL_WMAX)
    hb = tm // POOL_WMAX

    def body(u_ref, halo_ref, pw_ref, sc_ref, pm_ref):
        i = pl.program_id(0)
        u = u_ref[...]
        halo = jnp.where(i > 0, halo_ref[...], 0.0)
        ext = jnp.concatenate([halo, u], axis=0)
        pos = i * tm + lax.broadcasted_iota(jnp.int32, (tm, 1), 0)
        ys = []
        for g, w in enumerate(POOL_WINDOWS):
            d = _window_mean_minus_token(ext, u, g, w, pos)
            ys.append(_dot(d.astype(CDT), pw_ref[g]))
        pm_ref[...] = (jnp.concatenate(ys, axis=1) * sc_ref[...]).astype(pm_ref.dtype)

    row = pl.BlockSpec((tm, POOL_DIM), lambda i: (i, 0))
    return pl.pallas_call(
        body, name=name, grid=(T // tm,),
        in_specs=[row, pl.BlockSpec((POOL_WMAX, POOL_DIM), lambda i: (jnp.maximum(i * hb - 1, 0), 0)),
                  pl.BlockSpec(pool_w.shape, lambda i: (0, 0, 0)), pl.BlockSpec((1, POOL_DIM), lambda i: (0, 0))],
        out_specs=row, out_shape=jax.ShapeDtypeStruct((T, POOL_DIM), CDT),
        compiler_params=_params("parallel"),
    )(zu, zu, pool_w, scale.reshape(1, POOL_DIM))


def _pool_bwd(zu, dpm, pool_w, scale, name):
    T = zu.shape[0]
    tm = _tile(T, 512, POOL_WMAX)
    hb = tm // POOL_WMAX
    nsteps = T // tm
    ext_rows = tm + POOL_WMAX

    def body(u_ref, halo_ref, dpm_ref, dnext_ref, pw_ref, sc_ref, du_ref, dpw_ref, dsc_ref):
        i = pl.program_id(0)

        @pl.when(i == 0)
        def _():
            dpw_ref[...] = jnp.zeros_like(dpw_ref)
            dsc_ref[...] = jnp.zeros_like(dsc_ref)

        u = u_ref[...]
        halo = jnp.where(i > 0, halo_ref[...], 0.0)
        ext = jnp.concatenate([halo, u], axis=0)
        dpm_t = dpm_ref[...].astype(F32)
        dnext = jnp.where(i < nsteps - 1, dnext_ref[...].astype(F32), 0.0)
        dext = jnp.concatenate([dpm_t, dnext], axis=0)
        sc = sc_ref[...]
        pos = i * tm + lax.broadcasted_iota(jnp.int32, (tm, 1), 0)
        pos_ext = i * tm + lax.broadcasted_iota(jnp.int32, (ext_rows, 1), 0)
        dus, dscs = [], []
        for g, w in enumerate(POOL_WINDOWS):
            sl = slice(g * GROUP, (g + 1) * GROUP)
            dc = _window_mean_minus_token(ext, u, g, w, pos).astype(CDT)
            y = _dot(dc, pw_ref[g])
            dscs.append(jnp.sum(dpm_t[:, sl] * y, axis=0, keepdims=True))
            dy_ext = (dext[:, sl] * sc[:, sl]).astype(CDT)
            dpw_ref[g] += _dot_tn(dc, dy_ext[:tm])
            dd = _dot_nt(dy_ext, pw_ref[g])
            r = dd / jnp.minimum(pos_ext + 1, w).astype(F32)
            span = 1
            while span < w:
                r = r + pltpu.roll(r, ext_rows - span, axis=0)
                span *= 2
            dus.append(r[:tm] - dd[:tm])
        du_ref[...] = jnp.concatenate(dus, axis=1).astype(du_ref.dtype)
        dsc_ref[...] += jnp.concatenate(dscs, axis=1)

    row = pl.BlockSpec((tm, POOL_DIM), lambda i: (i, 0))
    prev = pl.BlockSpec((POOL_WMAX, POOL_DIM), lambda i: (jnp.maximum(i * hb - 1, 0), 0))
    nxt = pl.BlockSpec((POOL_WMAX, POOL_DIM), lambda i: (jnp.minimum((i + 1) * hb, nsteps * hb - 1), 0))
    return pl.pallas_call(
        body, name=name, grid=(nsteps,),
        in_specs=[row, prev, row, nxt, pl.BlockSpec(pool_w.shape, lambda i: (0, 0, 0)),
                  pl.BlockSpec((1, POOL_DIM), lambda i: (0, 0))],
        out_specs=[row, pl.BlockSpec(pool_w.shape, lambda i: (0, 0, 0)), pl.BlockSpec((1, POOL_DIM), lambda i: (0, 0))],
        out_shape=[jax.ShapeDtypeStruct((T, POOL_DIM), CDT), jax.ShapeDtypeStruct(pool_w.shape, F32),
                   jax.ShapeDtypeStruct((1, POOL_DIM), F32)],
        compiler_params=_params("arbitrary"),
    )(zu, zu, dpm, dpm, pool_w, scale.reshape(1, POOL_DIM))


def _rope_tables(T):
    pos = jnp.arange(T, dtype=F32)
    inv_freq = ROPE_THETA ** (-jnp.arange(0, ROT_DIM, 2, dtype=F32) / ROT_DIM)
    ang = pos[:, None] * inv_freq[None, :]
    cos, sin = jnp.cos(ang), jnp.sin(ang)
    rest = HEAD_DIM - ROT_DIM
    cos_h = jnp.concatenate([cos, cos, jnp.ones((T, rest), F32)], axis=1)
    sin_h = jnp.concatenate([-sin, sin, jnp.zeros((T, rest), F32)], axis=1)
    return jnp.tile(cos_h, (1, 2)), jnp.tile(sin_h, (1, 2))


def _lane_masks():
    lane = lax.broadcasted_iota(jnp.int32, (1, LANES), 1)
    in_head = lane % HEAD_DIM
    return lane < HEAD_DIM, in_head < ROT_DIM // 2


def _rope_partner(v, low):
    lane = lax.broadcasted_iota(jnp.int32, (1, LANES), 1)
    swapped = jnp.where(low, pltpu.roll(v, LANES - ROT_DIM // 2, axis=1), pltpu.roll(v, ROT_DIM // 2, axis=1))
    return jnp.where(lane % HEAD_DIM < ROT_DIM, swapped, 0.0)


def _head_mean(v, first):
    lo = jnp.sum(jnp.where(first, v, 0.0), axis=-1, keepdims=True)
    hi = jnp.sum(jnp.where(first, 0.0, v), axis=-1, keepdims=True)
    return jnp.where(first, lo, hi) * (1.0 / HEAD_DIM)


def _qk_fwd(zqk, gqk, cos_t, sin_t, name):
    T = zqk.shape[0]
    tm = _tile(T, 512)

    def body(z_ref, g_ref, c_ref, s_ref, o_ref):
        first, low = _lane_masks()
        cosv, sinv = c_ref[...], s_ref[...]
        for c in range(QK_DIM // LANES):
            sl = slice(c * LANES, (c + 1) * LANES)
            xv = z_ref[:, sl]
            r = lax.rsqrt(_head_mean(xv * xv, first) + EPS)
            xn = xv * r * g_ref[:, sl]
            o_ref[:, sl] = (xn * cosv + _rope_partner(xn, low) * sinv).astype(o_ref.dtype)

    row = pl.BlockSpec((tm, QK_DIM), lambda i: (i, 0))
    tab = pl.BlockSpec((tm, LANES), lambda i: (i, 0))
    return pl.pallas_call(
        body, name=name, grid=(T // tm,),
        in_specs=[row, pl.BlockSpec((1, QK_DIM), lambda i: (0, 0)), tab, tab], out_specs=row,
        out_shape=jax.ShapeDtypeStruct((T, QK_DIM), CDT), compiler_params=_params("parallel"),
    )(zqk, gqk, cos_t, sin_t)


def _qk_bwd(dqk, zqk, gqk, cos_t, sin_t, name):
    T = zqk.shape[0]
    tm = _tile(T, 512)

    def body(d_ref, z_ref, g_ref, c_ref, s_ref, dz_ref, dg_ref):
        @pl.when(pl.program_id(0) == 0)
        def _():
            dg_ref[...] = jnp.zeros_like(dg_ref)

        first, low = _lane_masks()
        cosv, sinv = c_ref[...], s_ref[...]
        dgs = []
        for c in range(QK_DIM // LANES):
            sl = slice(c * LANES, (c + 1) * LANES)
            dout = d_ref[:, sl]
            dxn = dout * cosv + _rope_partner(dout * sinv, low)
            xv = z_ref[:, sl]
            r = lax.rsqrt(_head_mean(xv * xv, first) + EPS)
            xh = xv * r
            dgs.append(jnp.sum(dxn * xh, axis=0, keepdims=True))
            dxh = dxn * g_ref[:, sl]
            dz_ref[:, sl] = (r * (dxh - xh * _head_mean(dxh * xh, first))).astype(dz_ref.dtype)
        dg_ref[...] += jnp.concatenate(dgs, axis=1)

    row = pl.BlockSpec((tm, QK_DIM), lambda i: (i, 0))
    tab = pl.BlockSpec((tm, LANES), lambda i: (i, 0))
    vec = pl.BlockSpec((1, QK_DIM), lambda i: (0, 0))
    return pl.pallas_call(
        body, name=name, grid=(T // tm,),
        in_specs=[row, row, vec, tab, tab], out_specs=[row, vec],
        out_shape=[jax.ShapeDtypeStruct((T, QK_DIM), CDT), jax.ShapeDtypeStruct((1, QK_DIM), F32)],
        compiler_params=_params("arbitrary"),
    )(dqk, zqk, gqk, cos_t, sin_t)


def _dup_half(v, first, kv):
    swapped = pltpu.roll(v, HEAD_DIM, axis=1)
    return jnp.where(first, v, swapped) if kv == 0 else jnp.where(first, swapped, v)


HEADS_PER_KV = 4
HEAD_STACK_FWD = 1
HEAD_STACK_BWD = 2


def _attn_bias(stack):
    qi = lax.broadcasted_iota(jnp.int32, (stack * BLOCK, 2 * BLOCK), 0) % BLOCK
    ki = lax.broadcasted_iota(jnp.int32, (stack * BLOCK, 2 * BLOCK), 1)
    diff = qi + BLOCK - ki
    band = (diff >= 0) & (diff < BLOCK)
    return jnp.stack([jnp.where(band, 0.0, -jnp.inf), jnp.where(band & (ki >= BLOCK), 0.0, -jnp.inf)]).astype(F32)


def _attn_blocks(T):
    return _tile(T // BLOCK, 4, 1)


def _stack_heads(ref, rows, kv, heads, first):
    parts = []
    for h in heads:
        c = 2 * kv + h // 2
        v = ref[rows, c * LANES:(c + 1) * LANES].astype(CDT)
        zero = jnp.zeros_like(v)
        parts.append(jnp.where(first, v, zero) if h % 2 == 0 else jnp.where(first, zero, v))
    return parts[0] if len(parts) == 1 else jnp.concatenate(parts, axis=0)


def _row_blocks(v, n):
    return [v[b * BLOCK:(b + 1) * BLOCK] for b in range(n)]


def _sink_column(sink_ref, kv, heads):
    cols = [jnp.full((BLOCK, 1), sink_ref[HEADS_PER_KV * kv + h], F32) for h in heads]
    return cols[0] if len(cols) == 1 else jnp.concatenate(cols, axis=0)


def _head_groups(stack):
    return [tuple(range(g, g + stack)) for g in range(0, HEADS_PER_KV, stack)]


def _softmax_with_sink(qst, kdup, sinkcol, bias):
    s = _dot_nt(qst, kdup) * ATTN_SCALE + bias
    m = jnp.maximum(jnp.max(s, axis=-1, keepdims=True), sinkcol)
    pu = jnp.exp(s - m)
    denom = jnp.sum(pu, axis=-1, keepdims=True) + jnp.exp(sinkcol - m)
    return pu * (1.0 / denom), m + jnp.log(denom)


def _attn_fwd(qkn, zv, sinks, name, side=None):
    T = qkn.shape[0]
    R = _attn_blocks(T)
    tq = R * BLOCK

    def body(sink_ref, bias_ref, qk_ref, qkp_ref, v_ref, vp_ref, o_ref, lse_ref):
        i = pl.program_id(0)
        first, _ = _lane_masks()
        lane = lax.broadcasted_iota(jnp.int32, (1, LANES), 1)
        kall = jnp.concatenate([qkp_ref[:, ATTN_DIM:], qk_ref[:, ATTN_DIM:]], axis=0)
        vall = jnp.concatenate([vp_ref[...], v_ref[...]], axis=0).astype(CDT)
        for r in range(R):
            bias = bias_ref[jnp.where(i == 0, 1, 0)] if r == 0 else bias_ref[0]
            rows = slice(r * BLOCK, (r + 2) * BLOCK)
            qrows = slice(r * BLOCK, (r + 1) * BLOCK)
            lse_rows = jnp.zeros((BLOCK, LANES), F32)
            for kv in range(2):
                kdup = _dup_half(kall[rows], first, kv)
                vdup = _dup_half(vall[rows], first, kv)
                res = []
                for heads in _head_groups(HEAD_STACK_FWD):
                    p, lse = _softmax_with_sink(_stack_heads(qk_ref, qrows, kv, heads, first), kdup,
                                                _sink_column(sink_ref, kv, heads), bias)
                    res += _row_blocks(_dot(p.astype(CDT), vdup), len(heads))
                    for b, col in enumerate(_row_blocks(lse, len(heads))):
                        lse_rows = jnp.where(lane == HEADS_PER_KV * kv + heads[b], col, lse_rows)
                o_ref[qrows, 2 * kv * LANES:(2 * kv + 1) * LANES] = jnp.where(first, res[0], res[1]).astype(o_ref.dtype)
                o_ref[qrows, (2 * kv + 1) * LANES:(2 * kv + 2) * LANES] = jnp.where(first, res[2], res[3]).astype(o_ref.dtype)
            lse_ref[qrows, :] = lse_rows

    bias = _attn_bias(HEAD_STACK_FWD)
    prev = lambda i: (jnp.maximum(i * R - 1, 0), 0)
    return _pcall(
        body, name, (T // tq,),
        [pl.BlockSpec(memory_space=pltpu.SMEM), pl.BlockSpec(bias.shape, lambda i: (0, 0, 0)),
         pl.BlockSpec((tq, QK_DIM), lambda i: (i, 0)), pl.BlockSpec((BLOCK, QK_DIM), prev),
         pl.BlockSpec((tq, KV_DIM), lambda i: (i, 0)), pl.BlockSpec((BLOCK, KV_DIM), prev)],
        [pl.BlockSpec((tq, ATTN_DIM), lambda i: (i, 0)), pl.BlockSpec((tq, LANES), lambda i: (i, 0))],
        [jax.ShapeDtypeStruct((T, ATTN_DIM), CDT), jax.ShapeDtypeStruct((T, LANES), F32)],
        (sinks, bias, qkn, qkn, zv, zv), ("parallel",), side)


def _attn_bwd(qkn, zv, sinks, do, o, lse, name, side=None):
    T = qkn.shape[0]
    R = _attn_blocks(T)
    tq = R * BLOCK

    def body(sink_ref, bias_ref, qk_ref, qkp_ref, v_ref, vp_ref, do_ref, o_ref, lse_ref,
             dq_ref, dkc_ref, dkp_ref, dvc_ref, dvp_ref, ds_ref):
        i = pl.program_id(0)

        @pl.when(i == 0)
        def _():
            ds_ref[...] = jnp.zeros_like(ds_ref)

        first, _ = _lane_masks()
        lane = lax.broadcasted_iota(jnp.int32, (1, LANES), 1)
        kall = jnp.concatenate([qkp_ref[:, ATTN_DIM:], qk_ref[:, ATTN_DIM:]], axis=0)
        vall = jnp.concatenate([vp_ref[...], v_ref[...]], axis=0).astype(CDT)
        for r in range(R):
            bias = bias_ref[jnp.where(i == 0, 1, 0)] if r == 0 else bias_ref[0]
            rows = slice(r * BLOCK, (r + 2) * BLOCK)
            qrows = slice(r * BLOCK, (r + 1) * BLOCK)
            dk_out, dv_out = [], []
            lse_rows = lse_ref[qrows, :]
            for kv in range(2):
                kdup = _dup_half(kall[rows], first, kv)
                vdup = _dup_half(vall[rows], first, kv)
                dq_h = []
                dk_acc = jnp.zeros((2 * BLOCK, LANES), F32)
                dv_acc = jnp.zeros((2 * BLOCK, LANES), F32)
                for heads in _head_groups(HEAD_STACK_BWD):
                    qst = _stack_heads(qk_ref, qrows, kv, heads, first)
                    dost = _stack_heads(do_ref, qrows, kv, heads, first)
                    lse_cols, delta_cols = [], []
                    for h in heads:
                        cols = slice((2 * kv + h // 2) * LANES, (2 * kv + h // 2 + 1) * LANES)
                        prod = do_ref[qrows, cols].astype(F32) * o_ref[qrows, cols].astype(F32)
                        own = first if h % 2 == 0 else jnp.logical_not(first)
                        delta_cols.append(jnp.sum(jnp.where(own, prod, 0.0), axis=-1, keepdims=True))
                        lse_cols.append(jnp.sum(jnp.where(lane == HEADS_PER_KV * kv + h, lse_rows, 0.0), axis=-1, keepdims=True))
                    lse_col = lse_cols[0] if len(heads) == 1 else jnp.concatenate(lse_cols, axis=0)
                    delta = delta_cols[0] if len(heads) == 1 else jnp.concatenate(delta_cols, axis=0)
                    p = jnp.exp(_dot_nt(qst, kdup) * ATTN_SCALE + bias - lse_col)
                    dsc = (p * (_dot_nt(dost, vdup) - delta)).astype(CDT)
                    psink = jnp.exp(_sink_column(sink_ref, kv, heads) - lse_col)
                    for b, term in enumerate(_row_blocks(psink * delta, len(heads))):
                        row = HEADS_PER_KV * kv + heads[b]
                        ds_ref[row:row + 1, :] += jnp.sum(term, axis=0, keepdims=True)
                    dq_h += _row_blocks(_dot(dsc, kdup) * ATTN_SCALE, len(heads))
                    dk_acc = dk_acc + _dot_tn(dsc, qst) * ATTN_SCALE
                    dv_acc = dv_acc + _dot_tn(p.astype(CDT), dost)
                dq_ref[qrows, 2 * kv * LANES:(2 * kv + 1) * LANES] = jnp.where(first, dq_h[0], dq_h[1])
                dq_ref[qrows, (2 * kv + 1) * LANES:(2 * kv + 2) * LANES] = jnp.where(first, dq_h[2], dq_h[3])
                dk_out.append(dk_acc + pltpu.roll(dk_acc, HEAD_DIM, axis=1))
                dv_out.append(dv_acc + pltpu.roll(dv_acc, HEAD_DIM, axis=1))
            dk = jnp.where(first, dk_out[0], dk_out[1])
            dv = jnp.where(first, dv_out[0], dv_out[1])
            dkp_ref[qrows, :] = dk[:BLOCK]
            dkc_ref[qrows, :] = dk[BLOCK:]
            dvp_ref[qrows, :] = dv[:BLOCK]
            dvc_ref[qrows, :] = dv[BLOCK:]

    bias = _attn_bias(HEAD_STACK_BWD)
    prev = lambda i: (jnp.maximum(i * R - 1, 0), 0)
    kvrow = pl.BlockSpec((tq, KV_DIM), lambda i: (i, 0))
    qrow = pl.BlockSpec((tq, ATTN_DIM), lambda i: (i, 0))
    kv_shape = jax.ShapeDtypeStruct((T, KV_DIM), F32)
    return _pcall(
        body, name, (T // tq,),
        [pl.BlockSpec(memory_space=pltpu.SMEM), pl.BlockSpec(bias.shape, lambda i: (0, 0, 0)),
         pl.BlockSpec((tq, QK_DIM), lambda i: (i, 0)), pl.BlockSpec((BLOCK, QK_DIM), prev),
         kvrow, pl.BlockSpec((BLOCK, KV_DIM), prev), qrow, qrow, kvrow],
        [qrow, kvrow, kvrow, kvrow, kvrow, pl.BlockSpec((N_Q_HEADS, LANES), lambda i: (0, 0))],
        [jax.ShapeDtypeStruct((T, ATTN_DIM), F32), kv_shape, kv_shape, kv_shape, kv_shape,
         jax.ShapeDtypeStruct((N_Q_HEADS, LANES), F32)],
        (sinks, bias, qkn, qkn, zv, zv, do, o, lse), ("arbitrary",), side)


def _merge_fwd(pm, o, w_pb, w_ab, zg, name, side=None):
    T = pm.shape[0]
    tm = _tile(T, 512)

    def body(pm_ref, o_ref, wp_ref, wa_ref, zg_ref, m_ref, gp_ref, ga_ref, fp_ref, fa_ref):
        pmv, ov = pm_ref[...], o_ref[...]
        a = jnp.concatenate([_dot(pmv, wp_ref[j]) for j in range(N_CHIPS)], axis=1)
        b = jnp.concatenate([_dot(ov, wa_ref[j]) for j in range(N_CHIPS)], axis=1)
        gp = _sigmoid(zg_ref[:, :D_MODEL])
        ga = _sigmoid(zg_ref[:, D_MODEL:])
        ap, ba = gp * a, ga * b
        m_ref[...] = (ap + ba).astype(m_ref.dtype)
        gp_ref[...] = gp.astype(gp_ref.dtype)
        ga_ref[...] = ga.astype(ga_ref.dtype)
        fp_ref[...] = (ap * (1.0 - gp)).astype(fp_ref.dtype)
        fa_ref[...] = (ba * (1.0 - ga)).astype(fa_ref.dtype)

    half = pl.BlockSpec((tm, POOL_DIM), lambda i: (i, 0))
    full = pl.BlockSpec((tm, D_MODEL), lambda i: (i, 0))
    wspec = pl.BlockSpec(w_pb.shape, lambda i: (0, 0, 0))
    out = jax.ShapeDtypeStruct((T, D_MODEL), CDT)
    return _pcall(body, name, (T // tm,), [half, half, wspec, wspec, pl.BlockSpec((tm, GATE_DIM), lambda i: (i, 0))],
                  [full] * 5, [out] * 5, (pm, o, w_pb, w_ab, zg), ("parallel",), side)


def _merge_bwd(dxo, w_out, factors, name):
    T = dxo.shape[0]
    tm = _tile(T, 512)

    def body(dx_ref, w_ref, gp_ref, ga_ref, fp_ref, fa_ref, da_ref, db_ref, dg_ref):
        dm = _dot_nt(dx_ref[...].astype(CDT), w_ref[...])
        da_ref[...] = (dm * gp_ref[...].astype(F32)).astype(da_ref.dtype)
        db_ref[...] = (dm * ga_ref[...].astype(F32)).astype(db_ref.dtype)
        dg_ref[:, :D_MODEL] = (dm * fp_ref[...].astype(F32)).astype(dg_ref.dtype)
        dg_ref[:, D_MODEL:] = (dm * fa_ref[...].astype(F32)).astype(dg_ref.dtype)

    full = pl.BlockSpec((tm, D_MODEL), lambda i: (i, 0))
    gate = pl.BlockSpec((tm, GATE_DIM), lambda i: (i, 0))
    out = jax.ShapeDtypeStruct((T, D_MODEL), CDT)
    return pl.pallas_call(
        body, name=name, grid=(T // tm,),
        in_specs=[full, pl.BlockSpec((D_MODEL, D_MODEL), lambda i: (0, 0))] + [full] * 4,
        out_specs=[full, full, gate], out_shape=[out, out, jax.ShapeDtypeStruct((T, GATE_DIM), CDT)],
        compiler_params=_params("parallel"),
    )(dxo, w_out, *factors)


def _adamw(w, g, m, v, name):
    Rr, C = w.shape
    tr = _tile(Rr, max(8, (1 << 19) // C // 8 * 8))

    def body(w_ref, g_ref, m_ref, v_ref, d_ref, nm_ref, nv_ref):
        gv = g_ref[...]
        nm = ADAM_B1 * m_ref[...] + (1.0 - ADAM_B1) * gv
        nv = ADAM_B2 * v_ref[...] + (1.0 - ADAM_B2) * (gv * gv)
        m_hat = nm / (1.0 - ADAM_B1 ** ADAM_STEP)
        v_hat = nv / (1.0 - ADAM_B2 ** ADAM_STEP)
        d_ref[...] = -ADAM_LR * (m_hat / (jnp.sqrt(v_hat) + ADAM_EPS) + ADAM_WD * w_ref[...])
        nm_ref[...] = nm
        nv_ref[...] = nv

    blk = pl.BlockSpec((tr, C), lambda i: (i, 0))
    out = jax.ShapeDtypeStruct((Rr, C), F32)
    return pl.pallas_call(
        body, name=name, grid=(Rr // tr,), in_specs=[blk] * 4, out_specs=[blk] * 3, out_shape=[out] * 3,
        compiler_params=_params("parallel"),
    )(w, g, m, v)


def _place():
    return lax.axis_index("x"), lax.axis_index("y"), lax.axis_index("c")


def _other_chip(x, y, d):
    return (1 - x if d & 2 else x), (1 - y if d & 1 else y)


def _rcopy(src, dst, ssem, rsem, dev):
    return pltpu.make_async_remote_copy(src_ref=src, dst_ref=dst, send_sem=ssem, recv_sem=rsem, device_id=dev,
                                        device_id_type=MESH)


def _row_half(rows, c):
    return pl.ds(c * (rows // 2), rows // 2)


def _is_wide(name):
    return name in WIDE


def _block(ref, wide, j, rows, n):
    if wide:
        return ref.at[rows, pl.ds(pl.multiple_of(j * n, LANES), n)]
    return ref.at[j, rows]


def _gathered_shape(shard, wide):
    _, a, n = shard.shape
    return jax.ShapeDtypeStruct((a, N_CHIPS * n) if wide else (N_CHIPS, a, n), shard.dtype)


def _gather_ici_side(shards, wides, l):
    k_of = lambda w, d: 3 * w + d - 1

    def issue(ins, outs, ssem, rsem):
        x, y, c = _place()
        cps = []
        for w, (shard, wide) in enumerate(zip(shards, wides)):
            _, a, n = shard.shape
            half = _row_half(a, c)
            for d in (1, 2, 3):
                px, py = _other_chip(x, y, d)
                cps.append(_rcopy(ins[w].at[l, half], _block(outs[w], wide, 2 * x + y, half, n),
                                  ssem.at[k_of(w, d)], rsem.at[k_of(w, d)], (px, py, c)))
        return cps

    return _Side(shards, [_gathered_shape(s_, wd) for s_, wd in zip(shards, wides)], 3 * len(shards), issue)


def _gather_d2d_side(shards, wides, gathered, l):
    nw = len(shards)

    def issue(ins, outs, ssem, rsem):
        x, y, c = _place()
        sibling = (x, y, 1 - c)
        cps = []
        for w, (shard, wide) in enumerate(zip(shards, wides)):
            _, a, n = shard.shape
            half = _row_half(a, c)
            for d in (1, 2, 3):
                px, py = _other_chip(x, y, d)
                k = 3 * w + d - 1
                got = _block(outs[w], wide, 2 * px + py, half, n)
                cps.append(_rcopy(got, got, ssem.at[k], rsem.at[k], sibling))
            cps.append(_rcopy(ins[nw + w].at[l], _block(outs[w], wide, 2 * x + y, pl.ds(0, a), n),
                              ssem.at[3 * nw + w], rsem.at[3 * nw + w], sibling))
        return cps

    return _Side(list(gathered) + list(shards), [jax.ShapeDtypeStruct(g.shape, g.dtype) for g in gathered], 4 * nw, issue,
                 aliases={w: w for w in range(nw)})


def _half_shape(g, wide):
    if wide:
        return jax.ShapeDtypeStruct((g.shape[0] // 2, g.shape[1]), g.dtype)
    return jax.ShapeDtypeStruct((N_CHIPS, g.shape[1] // 2, g.shape[2]), g.dtype)


def _reduce_sibling_side(gms, wides):
    def issue(ins, outs, ssem, rsem):
        x, y, c = _place()
        cps = []
        for w, (g, wide) in enumerate(zip(gms, wides)):
            src = ins[w].at[_row_half(g.shape[0], 1 - c)] if wide else ins[w].at[:, _row_half(g.shape[1], 1 - c)]
            cps.append(_rcopy(src, outs[w], ssem.at[w], rsem.at[w], (x, y, 1 - c)))
        return cps

    return _Side(gms, [_half_shape(g, wd) for g, wd in zip(gms, wides)], len(gms), issue)


def _reduce_chip_side(ps, wides):
    def slot_shape(p, wide):
        return jax.ShapeDtypeStruct((N_CHIPS, p.shape[0], p.shape[1] // N_CHIPS) if wide else p.shape, p.dtype)

    def issue(ins, outs, ssem, rsem):
        x, y, c = _place()
        cps = []
        for w, (p, wide) in enumerate(zip(ps, wides)):
            ah, n = (p.shape[0], p.shape[1] // N_CHIPS) if wide else p.shape[1:]
            for d in (1, 2, 3):
                px, py = _other_chip(x, y, d)
                k = 3 * w + d - 1
                cps.append(_rcopy(_block(ins[w], wide, 2 * px + py, pl.ds(0, ah), n), outs[w].at[2 * x + y],
                                  ssem.at[k], rsem.at[k], (px, py, c)))
        return cps

    return _Side(ps, [slot_shape(p, wd) for p, wd in zip(ps, wides)], 3 * len(ps), issue)


def _share_side(accs, items):
    def issue(ins, outs, ssem, rsem):
        x, y, c = _place()
        cps = []
        for k, (w, layer) in enumerate(items):
            mine = outs[w].at[layer, _row_half(accs[w].shape[1], c)]
            cps.append(_rcopy(mine, mine, ssem.at[k], rsem.at[k], (x, y, 1 - c)))
        return cps

    return _Side(accs, [jax.ShapeDtypeStruct(a.shape, a.dtype) for a in accs], len(items), issue,
                 aliases={w: w for w in range(len(accs))})


def _sum_rows(rows, b):
    return _tile(rows, max(16, (1 << 19) // b // 16 * 16), 16)


def _pair_sum(g, recv, wide, place, name):
    ah, b = recv.shape[-2:]
    ta = _sum_rows(ah, b)
    nr = ah // ta

    def body(p_ref, g_ref, r_ref, o_ref):
        o_ref[...] = (g_ref[...].astype(F32) + r_ref[...].astype(F32)).astype(o_ref.dtype)

    if wide:
        grid = (nr,)
        specs = [pl.BlockSpec((ta, b), lambda r, p: (p[0] * nr + r, 0)), pl.BlockSpec((ta, b), lambda r, p: (r, 0))]
        out_spec = pl.BlockSpec((ta, b), lambda r, p: (r, 0))
    else:
        grid = (N_CHIPS, nr)
        specs = [pl.BlockSpec((None, ta, b), lambda j, r, p: (j, p[0] * nr + r, 0)),
                 pl.BlockSpec((None, ta, b), lambda j, r, p: (j, r, 0))]
        out_spec = pl.BlockSpec((None, ta, b), lambda j, r, p: (j, r, 0))
    return pl.pallas_call(
        body, name=name,
        grid_spec=pltpu.PrefetchScalarGridSpec(num_scalar_prefetch=1, grid=grid, in_specs=specs, out_specs=out_spec),
        out_shape=jax.ShapeDtypeStruct(recv.shape, recv.dtype), compiler_params=_params(*["parallel"] * len(grid)),
    )(place, g, recv)


def _chip_sum(slots, part, wide, place, acc, l, name):
    _, ah, b = slots.shape
    ta = _sum_rows(ah, b)
    nr = ah // ta

    def body(p_ref, s_ref, own_ref, acc_ref, o_ref):
        j = p_ref[1]
        own = own_ref[...].astype(F32)
        term = [jnp.where(j == s_, own, s_ref[s_].astype(F32)) for s_ in range(N_CHIPS)]
        o_ref[...] = ((term[0] + term[1]) + term[2]) + term[3]

    own_spec = (pl.BlockSpec((ta, b), lambda r, p: (r, p[1])) if wide else
                pl.BlockSpec((None, ta, b), lambda r, p: (p[1], r, 0)))
    return pl.pallas_call(
        body, name=name,
        grid_spec=pltpu.PrefetchScalarGridSpec(
            num_scalar_prefetch=1, grid=(nr,),
            in_specs=[pl.BlockSpec((N_CHIPS, ta, b), lambda r, p: (0, r, 0)), own_spec, ANY],
            out_specs=pl.BlockSpec((None, ta, b), lambda r, p: (l, p[0] * nr + r, 0))),
        out_shape=jax.ShapeDtypeStruct(acc.shape, F32), input_output_aliases={3: 0},
        compiler_params=_params("parallel"),
    )(place, slots, part, acc)


def _small_side(v):
    def issue(ins, outs, ssem, rsem):
        x, y, c = _place()
        cps = []
        for d in range(1, N_DEV):
            px, py = _other_chip(x, y, d >> 1)
            pc = 1 - c if d & 1 else c
            cps.append(_rcopy(ins[0], outs[0].at[4 * x + 2 * y + c], ssem.at[d - 1], rsem.at[d - 1], (px, py, pc)))
        return cps

    return _Side([v], [jax.ShapeDtypeStruct((N_DEV,) + v.shape, v.dtype)], N_DEV - 1, issue)


def _small_sum(slots, v, place, name):
    def body(p_ref, s_ref, v_ref, o_ref):
        me = 2 * p_ref[1] + p_ref[0]
        acc = jnp.where(me == 0, v_ref[...], s_ref[0])
        for s_ in range(1, N_DEV):
            acc = acc + jnp.where(me == s_, v_ref[...], s_ref[s_])
        o_ref[...] = acc

    return pl.pallas_call(
        body, name=name,
        grid_spec=pltpu.PrefetchScalarGridSpec(
            num_scalar_prefetch=1, grid=(1,),
            in_specs=[pl.BlockSpec(slots.shape, lambda i, p: (0, 0, 0)), pl.BlockSpec(v.shape, lambda i, p: (0, 0))],
            out_specs=pl.BlockSpec(v.shape, lambda i, p: (0, 0))),
        out_shape=jax.ShapeDtypeStruct(v.shape, F32), compiler_params=_params("arbitrary"),
    )(place, slots, v)


def _ffn_forward(x, p, tag, side_of):
    h, gu, act = _ffn_up(x, p[f"ln_{tag}"], p[f"w_{tag}_gu"], f"{tag}_up", side_of(f"{tag}_up"))
    x_out = _mm_nn(act, p[f"w_{tag}_down"], f"{tag}_down", F32, res=x, scale=0.5, side=side_of(f"{tag}_down"))
    return x_out, (x, h, gu, act)


def _row_blocks_of(dw):
    return dw.reshape(N_CHIPS, dw.shape[0] // N_CHIPS, dw.shape[1])


def _ffn_backward(dxo, saved, p, tag, side_of, grad):
    x, h, gu, act = saved
    dgu = _ffn_down_bwd(dxo, p[f"w_{tag}_down"], gu, f"{tag}_down_bwd", side_of(f"{tag}_down_bwd"))
    grad(f"w_{tag}_down", _row_blocks_of(_mm_tn(act, dxo, f"{tag}_dwd", scale=0.5, side=side_of(f"{tag}_dwd"))))
    grad(f"w_{tag}_gu", _mm_tn(h, dgu, f"{tag}_dwgu", tn_target=2816, tm_target=1024, side=side_of(f"{tag}_dwgu")))
    dx, d_ln = _mm_nt_norm_bwd([dgu], p[f"w_{tag}_gu"], x, p[f"ln_{tag}"], dxo, f"{tag}_dh_norm_bwd",
                               side_of(f"{tag}_dh_norm_bwd"))
    grad(f"ln_{tag}", d_ln[0])
    return dx


def _mixer_forward(x, p, tabs, side_of):
    h, zu, zqk, zv, zg = _mm_in(x, p["ln_mix"], p["w_in"], "mix_in", side_of("mix_in"))
    pm = _pool_fwd(zu, p["pool_w"], p["pool_scale"], "pool_fwd")
    qkn = _qk_fwd(zqk, p["gqk"], *tabs, "qk_fwd")
    o, lse = _attn_fwd(qkn, zv, p["sinks"], "attn_fwd", side_of("attn_fwd"))
    m, *factors = _merge_fwd(pm, o, p["w_pool_branch"], p["w_attn_branch"], zg, "merge_fwd", side_of("merge_fwd"))
    x_out = _mm_nn(m, p["w_out"], "mix_out", F32, res=x, scale=1.0)
    return x_out, (x, h, zu, zqk, zv, pm, qkn, o, lse, factors, m)


def _shift_up(v):
    return jnp.concatenate([v[BLOCK:], jnp.zeros((BLOCK, v.shape[1]), v.dtype)], axis=0)


def _mixer_backward(dxo, saved, p, tabs, side_of, grad):
    x, h, zu, zqk, zv, pm, qkn, o, lse, factors, m = saved
    d_a, d_b, dgl = _merge_bwd(dxo, p["w_out"], factors, "merge_bwd")
    grad("w_out", _row_blocks_of(_mm_tn(m, dxo, "mix_dwout")))
    dpm = _mm_nt_blocks(d_a, p["w_pool_branch"], "pool_branch_dx", CDT)
    grad("w_pool_branch", _mm_tn(pm, d_a, "pool_branch_dw", col_blocks=N_CHIPS))
    do = _mm_nt_blocks(d_b, p["w_attn_branch"], "attn_branch_dx", CDT)
    grad("w_attn_branch", _mm_tn(o, d_b, "attn_branch_dw", col_blocks=N_CHIPS))
    du, d_pool_w, d_pool_scale = _pool_bwd(zu, dpm, p["pool_w"], p["pool_scale"], "pool_bwd")
    grad("pool_w", d_pool_w)
    grad("pool_scale", d_pool_scale)
    dq, dkc, dkp, dvc, dvp, dsink = _attn_bwd(qkn, zv, p["sinks"], do, o, lse, "attn_bwd", side_of("attn_bwd"))
    dqk = jnp.concatenate([dq, dkc + _shift_up(dkp)], axis=1)
    dv = dvc + _shift_up(dvp)
    dzqk, dgqk = _qk_bwd(dqk, zqk, p["gqk"], *tabs, "qk_bwd")
    grad("q_norm", dgqk[0, :ATTN_DIM].reshape(N_Q_HEADS, HEAD_DIM).sum(axis=0))
    grad("k_norm", dgqk[0, ATTN_DIM:].reshape(KV_DIM // HEAD_DIM, HEAD_DIM).sum(axis=0))
    grad("sinks", -dsink[:, 0])
    dz = [du, dzqk, dv, dgl]
    grad("w_in", _blocks_from_full("w_in", _mm_tn_parts(h, dz, "mix_dwin")).astype(WIRE_DT))
    dx, d_ln = _mm_nt_norm_bwd(dz, p["w_in"], x, p["ln_mix"], dxo, "mix_dh_norm_bwd", side_of("mix_dh_norm_bwd"))
    grad("ln_mix", d_ln[0])
    return dx


class _NoComm:
    def __init__(self, layers):
        self.layers, self.grads = layers, [dict() for _ in layers]

    def weight(self, l, name):
        return self.layers[l][name]

    def side(self, phase, l, host):
        return None

    def grad(self, l, name, value):
        self.grads[l][name] = value


class _Layer:
    def __init__(self, hooks, l):
        self.hooks, self.l, self.got = hooks, l, {}

    def __getitem__(self, name):
        if name not in self.got:
            self.got[name] = self.hooks.weight(self.l, name)
        return self.got[name]


def _local_step(x, tgt, n_layers, hooks):
    T = x.shape[0]
    tabs = _rope_tables(T)
    saved, params = [], []
    for l in range(n_layers):
        p = _Layer(hooks, l)
        side_of = functools.partial(hooks.side, "fwd", l)
        x, s1 = _ffn_forward(x, p, "ffn1", side_of)
        x, s2 = _mixer_forward(x, p, tabs, side_of)
        x, s3 = _ffn_forward(x, p, "ffn2", side_of)
        saved.append((s1, s2, s3))
        params.append(p)
    dx, loss = _loss_head(x, tgt, "loss_head")
    for l in reversed(range(n_layers)):
        p = params[l]
        s1, s2, s3 = saved[l]
        side_of = functools.partial(hooks.side, "bwd", l)
        grad = functools.partial(hooks.grad, l)
        dx = _ffn_backward(dx, s3, p, "ffn2", side_of, grad)
        dx = _mixer_backward(dx, s2, p, tabs, side_of, grad)
        dx = _ffn_backward(dx, s1, p, "ffn1", side_of, grad)
    return loss, dx


def _full_from_blocks(name, blocks):
    if name in COL_SHARDED:
        return jnp.transpose(blocks, (1, 0, 2)).reshape(blocks.shape[1], N_CHIPS * blocks.shape[2])
    return blocks.reshape(N_CHIPS * blocks.shape[1], blocks.shape[2])


def _blocks_from_full(name, full):
    K, N = full.shape
    if name in COL_SHARDED:
        return jnp.transpose(full.reshape(K, N_CHIPS, N // N_CHIPS), (1, 0, 2))
    return full.reshape(N_CHIPS, K // N_CHIPS, N)


JOBS = {"a": ("w_ffn1_gu", "w_ffn1_down"), "b": ("w_in", "w_pool_branch", "w_attn_branch", "w_out"),
        "c": ("w_ffn2_gu", "w_ffn2_down")}
GATHER_PLAN = {"ffn1_up": ("ici", "b", JOBS["b"], 0), "ffn1_down": ("d2d", "b", JOBS["b"], 0),
               "mix_in": ("ici", "c", JOBS["c"][:1], 0), "attn_fwd": ("ici", "c", JOBS["c"][1:], 0),
               "merge_fwd": ("d2d", "c", JOBS["c"], 0),
               "ffn2_up": ("ici", "a", JOBS["a"], 1), "ffn2_down": ("d2d", "a", JOBS["a"], 1)}
REDUCE_PLAN = {"ffn2_down_bwd": ("sibling", "a", 1), "ffn2_dwgu": ("chip", "a", 1),
               "ffn2_dh_norm_bwd": ("sibling", "c", 0), "attn_bwd": ("chip", "c", 0),
               "mix_dh_norm_bwd": ("sibling", "b", 0), "ffn1_down_bwd": ("chip", "b", 0)}
SHARE_HOST = "ffn1_dwgu"
SMALL_HOST = "ffn2_dwd"
LAST_GRAD = "ln_ffn1"


class _Exchange:
    def __init__(self, shards, small, place, n_layers):
        self.shards, self.small, self.place, self.n_layers = shards, small, place, n_layers
        first, wides = [shards[n] for n in JOBS["a"]], [_is_wide(n) for n in JOBS["a"]]
        got = _run_side(_gather_ici_side(first, wides, 0), "gather_ici")
        got = _run_side(_gather_d2d_side(first, wides, got, 0), "gather_d2d")
        self.blocks = {(n, 0): g for n, g in zip(JOBS["a"], got)}
        self.landed = {}
        self.handed = []
        self.acc = {n: lax.empty(shards[n].shape, F32) for n in BIG}
        self.grads = [dict() for _ in range(n_layers)]
        self.reduce = {}
        self.summed = set()
        self.unshared, self.sharing = [], None
        self.small_sides, self.small_waiting = {}, None

    def weight(self, l, name):
        if name not in BIG:
            return self.small(l)[name]
        for names, layer, done in self.handed:
            self.blocks.update({(n, layer): g for n, g in zip(names, done.outs)})
        self.handed.clear()
        blocks = self.blocks.pop((name, l))
        return blocks if name in USED_AS_BLOCKS + WIDE else _full_from_blocks(name, blocks)

    def _gather_side(self, l, host):
        step, job, names, ahead = GATHER_PLAN[host]
        layer = l + ahead
        if layer >= self.n_layers:
            return None
        if step == "ici":
            side = _gather_ici_side([self.shards[n] for n in names], [_is_wide(n) for n in names], layer)
            self.landed.setdefault((job, layer), []).append((names, side))
            return side
        names, gathered = JOBS[job], {}
        for part_names, side in self.landed.pop((job, layer)):
            gathered.update(zip(part_names, side.outs))
        done = _gather_d2d_side([self.shards[n] for n in names], [_is_wide(n) for n in names],
                                [gathered[n] for n in names], layer)
        self.handed.append((names, layer, done))
        return done

    def grad(self, l, name, value):
        self.grads[l][name] = value
        if name == LAST_GRAD and l > 0:
            packed, self.small_spans = _pack_small([self.grads[l][n] for n in SMALL])
            self.small_sides[l] = _small_side(packed)
            self.small_waiting = l

    def reduced_small(self, loss_part):
        packed, spans = _pack_small([self.grads[0][n] for n in SMALL] + [loss_part])
        self.small_sides[0] = _small_side(packed)
        _run_side(self.small_sides[0], "all_reduce_small")
        shapes = [self.grads[0][n].shape for n in SMALL]
        per_layer = []
        for l in range(self.n_layers):
            side = self.small_sides[l]
            summed = _small_sum(side.outs[0], side.ins[0], self.place, "small_sum")
            per_layer.append(_unpack_small(summed, spans, shapes + [(1, 1)] * (l == 0)))
        loss = per_layer[0][-1][0, 0]
        return {n: jnp.stack([vals[k] for vals in per_layer]) for k, n in enumerate(SMALL)}, loss

    def _reduce_side(self, l, host):
        step, job, ahead = REDUCE_PLAN[host]
        layer = l + ahead
        if layer >= self.n_layers:
            return None
        return self._reduce_step(step, job, layer)

    def _reduce_step(self, step, job, layer):
        if step == "sibling":
            st = self.reduce[(job, layer)] = dict(gm=[self.grads[layer][n] for n in JOBS[job]],
                                                  wide=[_is_wide(n) for n in JOBS[job]])
            st["sibling"] = _reduce_sibling_side(st["gm"], st["wide"])
            return st["sibling"]
        st = self.reduce[(job, layer)]
        st["part"] = [_pair_sum(g, r, wd, self.place, "grad_pair_sum")
                      for g, r, wd in zip(st["gm"], st["sibling"].outs, st["wide"])]
        st["chip"] = _reduce_chip_side(st["part"], st["wide"])
        return st["chip"]

    def _chip_sums(self):
        if self.sharing is not None:
            self.acc.update(zip(BIG, self.sharing.outs))
            self.sharing = None
        for (job, layer), st in self.reduce.items():
            if (job, layer) not in self.summed and "chip" in st and st["chip"].outs is not None:
                self.summed.add((job, layer))
                for n, slots, part, wd in zip(JOBS[job], st["chip"].outs, st["part"], st["wide"]):
                    self.acc[n] = _chip_sum(slots, part, wd, self.place, self.acc[n], layer, "grad_chip_sum")
                    self.unshared.append((BIG.index(n), layer))

    def _share(self):
        side = _share_side([self.acc[n] for n in BIG], self.unshared)
        self.unshared = []
        return side

    def side(self, phase, l, host):
        if phase == "fwd":
            return self._gather_side(l, host) if host in GATHER_PLAN else None
        self._chip_sums()
        if host == SMALL_HOST and self.small_waiting is not None:
            side, self.small_waiting = self.small_sides[self.small_waiting], None
            return side
        if host == SHARE_HOST and self.unshared:
            self.sharing = self._share()
            return self.sharing
        return self._reduce_side(l, host) if host in REDUCE_PLAN else None

    def reduced(self):
        _run_side(self._reduce_step("sibling", "a", 0), "grad_sibling_exchange")
        _run_side(self._reduce_step("chip", "a", 0), "grad_chip_exchange")
        self._chip_sums()
        return dict(zip(BIG, _run_side(self._share(), "grad_sibling_share")))


def _pack_small(parts):
    rows, spans, lo = [], [], 0
    for v in parts:
        flat = v.reshape(-1)
        nrow = -(-flat.shape[0] // LANES)
        flat = jnp.pad(flat, (0, nrow * LANES - flat.shape[0]))
        rows.append(flat.reshape(nrow, LANES))
        spans.append((lo, nrow))
        lo += nrow
    pad = -lo % 8
    if pad:
        rows.append(jnp.zeros((pad, LANES), F32))
    return jnp.concatenate(rows, axis=0), spans


def _unpack_small(packed, spans, shapes):
    out = []
    for (lo, nrow), shape in zip(spans, shapes):
        size = 1
        for s in shape:
            size *= s
        out.append(packed[lo:lo + nrow].reshape(-1)[:size].reshape(shape))
    return out


def kernel(x, ln_ffn1, w_ffn1_gu, w_ffn1_down, ln_mix, w_in, pool_w, pool_scale, w_pool_branch, q_norm, k_norm, sinks, w_attn_branch, w_out, ln_ffn2, w_ffn2_gu, w_ffn2_down, loss_target, m_ln_ffn1, m_w_ffn1_gu, m_w_ffn1_down, m_ln_mix, m_w_in, m_pool_w, m_pool_scale, m_w_pool_branch, m_q_norm, m_k_norm, m_sinks, m_w_attn_branch, m_w_out, m_ln_ffn2, m_w_ffn2_gu, m_w_ffn2_down, v_ln_ffn1, v_w_ffn1_gu, v_w_ffn1_down, v_ln_mix, v_w_in, v_pool_w, v_pool_scale, v_w_pool_branch, v_q_norm, v_k_norm, v_sinks, v_w_attn_branch, v_w_out, v_ln_ffn2, v_w_ffn2_gu, v_w_ffn2_down):
    w = dict(ln_ffn1=ln_ffn1, w_ffn1_gu=w_ffn1_gu, w_ffn1_down=w_ffn1_down, ln_mix=ln_mix, w_in=w_in, pool_w=pool_w,
             pool_scale=pool_scale, w_pool_branch=w_pool_branch, q_norm=q_norm, k_norm=k_norm, sinks=sinks,
             w_attn_branch=w_attn_branch, w_out=w_out, ln_ffn2=ln_ffn2, w_ffn2_gu=w_ffn2_gu, w_ffn2_down=w_ffn2_down)
    mom = dict(ln_ffn1=m_ln_ffn1, w_ffn1_gu=m_w_ffn1_gu, w_ffn1_down=m_w_ffn1_down, ln_mix=m_ln_mix, w_in=m_w_in,
               pool_w=m_pool_w, pool_scale=m_pool_scale, w_pool_branch=m_w_pool_branch, q_norm=m_q_norm, k_norm=m_k_norm,
               sinks=m_sinks, w_attn_branch=m_w_attn_branch, w_out=m_w_out, ln_ffn2=m_ln_ffn2, w_ffn2_gu=m_w_ffn2_gu,
               w_ffn2_down=m_w_ffn2_down)
    var = dict(ln_ffn1=v_ln_ffn1, w_ffn1_gu=v_w_ffn1_gu, w_ffn1_down=v_w_ffn1_down, ln_mix=v_ln_mix, w_in=v_w_in,
               pool_w=v_pool_w, pool_scale=v_pool_scale, w_pool_branch=v_w_pool_branch, q_norm=v_q_norm, k_norm=v_k_norm,
               sinks=v_sinks, w_attn_branch=v_w_attn_branch, w_out=v_w_out, ln_ffn2=v_ln_ffn2, w_ffn2_gu=v_w_ffn2_gu,
               w_ffn2_down=v_w_ffn2_down)
    L = ln_ffn1.shape[0]

    def small(l):
        return dict(ln_ffn1=ln_ffn1[l], ln_mix=ln_mix[l], ln_ffn2=ln_ffn2[l], pool_w=pool_w[l].astype(CDT),
                    pool_scale=pool_scale[l], sinks=sinks[l],
                    gqk=jnp.concatenate([jnp.tile(q_norm[l], N_Q_HEADS), jnp.tile(k_norm[l], KV_DIM // HEAD_DIM)]).reshape(1, QK_DIM))

    place = jnp.stack([lax.axis_index("c"), 2 * lax.axis_index("x") + lax.axis_index("y")]).astype(jnp.int32)
    hooks = _Exchange({n: w[n].astype(CDT) for n in BIG}, small, place, L)
    loss_part, grad_x = _local_step(x[0], loss_target[0], L, hooks)
    g_big = hooks.reduced()
    g_small, loss = hooks.reduced_small(loss_part)
    g_small = {n: v.reshape(w[n].shape) for n, v in g_small.items()}

    grad_out, delta, new_m, new_v = {}, {}, {}, {}
    for n in BIG:
        shape = w[n].shape
        flat = (shape[0] * shape[1], shape[2])
        grad_out[n] = g_big[n]
        d, nm, nv = _adamw(w[n].reshape(flat), g_big[n].reshape(flat), mom[n].reshape(flat), var[n].reshape(flat), "adamw")
        delta[n], new_m[n], new_v[n] = d.reshape(shape), nm.reshape(shape), nv.reshape(shape)
    pw, _ = _pack_small([w[n] for n in SMALL])
    pg, sp = _pack_small([g_small[n] for n in SMALL])
    pm_, _ = _pack_small([mom[n] for n in SMALL])
    pv, _ = _pack_small([var[n] for n in SMALL])
    d, nm, nv = _adamw(pw, pg, pm_, pv, "adamw_small")
    shapes = [w[n].shape for n in SMALL]
    for n, dv, mv, vv in zip(SMALL, _unpack_small(d, sp, shapes), _unpack_small(nm, sp, shapes), _unpack_small(nv, sp, shapes)):
        grad_out[n], delta[n], new_m[n], new_v[n] = g_small[n], dv, mv, vv

    return (loss, grad_x[None], *[grad_out[n] for n in WEIGHTS], *[delta[n] for n in WEIGHTS],
            *[new_m[n] for n in WEIGHTS], *[new_v[n] for n in WEIGHTS])
```

```python
import functools
import math

import jax
import jax.numpy as jnp
from jax import lax
from jax.experimental import pallas as pl
from jax.experimental.pallas import tpu as pltpu

F32 = jnp.float32
CDT = jnp.bfloat16
WIRE_DT = jnp.bfloat16

D_MODEL = 1024
POOL_WINDOWS = (2, 4, 8, 16)
POOL_WMAX = 16
GROUP = 128
POOL_DIM = 512
HEAD_DIM = 64
N_Q_HEADS = 8
ATTN_DIM = 512
KV_DIM = 128
QK_DIM = ATTN_DIM + KV_DIM
GATE_DIM = 2 * D_MODEL
BLOCK = 128
ROPE_THETA = 500000.0
ROT_DIM = 16
EPS = 1e-6
ATTN_SCALE = HEAD_DIM ** -0.5

ADAM_LR = 0.001
ADAM_B1 = 0.9
ADAM_B2 = 0.999
ADAM_EPS = 1e-08
ADAM_WD = 0.01
ADAM_STEP = 10

N_CHIPS = 4
N_DEV = 8
LANES = 128
VMEM_LIMIT_BYTES = 48 * 1024 * 1024

MESH = pl.DeviceIdType.MESH
ANY = pl.BlockSpec(memory_space=pl.ANY)

BIG = ("w_ffn1_gu", "w_ffn1_down", "w_in", "w_pool_branch", "w_attn_branch", "w_out", "w_ffn2_gu", "w_ffn2_down")
COL_SHARDED = ("w_ffn1_gu", "w_in", "w_pool_branch", "w_attn_branch", "w_ffn2_gu")
USED_AS_BLOCKS = ("w_pool_branch", "w_attn_branch")
WIDE = ("w_ffn1_gu", "w_ffn2_gu")
SMALL = ("ln_ffn1", "ln_mix", "pool_w", "pool_scale", "q_norm", "k_norm", "sinks", "ln_ffn2")
WEIGHTS = ("ln_ffn1", "w_ffn1_gu", "w_ffn1_down", "ln_mix", "w_in", "pool_w", "pool_scale", "w_pool_branch",
           "q_norm", "k_norm", "sinks", "w_attn_branch", "w_out", "ln_ffn2", "w_ffn2_gu", "w_ffn2_down")


def _tile(n, target, mult=8):
    if n <= target:
        return n
    for t in range(target - target % mult, 0, -mult):
        if n % t == 0:
            return t
    raise ValueError((n, target, mult))


def _params(*sem):
    return pltpu.CompilerParams(dimension_semantics=sem, vmem_limit_bytes=VMEM_LIMIT_BYTES)


def _sigmoid(v):
    return 0.5 * jnp.tanh(0.5 * v) + 0.5


def _dot(a, b):
    return jnp.dot(a, b, preferred_element_type=F32)


def _dot_nt(a, b):
    return lax.dot_general(a, b, (((1,), (1,)), ((), ())), preferred_element_type=F32)


def _dot_tn(a, b):
    return lax.dot_general(a, b, (((0,), (0,)), ((), ())), preferred_element_type=F32)


class _Side:
    def __init__(self, ins, out_shapes, n_sems, issue, aliases=None):
        self.ins, self.out_shapes, self.n_sems, self.issue = list(ins), list(out_shapes), n_sems, issue
        self.aliases = dict(aliases or {})
        self.outs = None


def _pcall(body, name, grid, in_specs, out_specs, out_shape, args, dims, side=None, scratch=()):
    scratch = list(scratch)
    if side is None:
        return pl.pallas_call(body, name=name, grid=grid, in_specs=in_specs, out_specs=out_specs, out_shape=out_shape,
                              scratch_shapes=scratch, compiler_params=_params(*dims))(*args)
    n_in, n_out, s_in, s_out = len(in_specs), len(out_specs), len(side.ins), len(side.out_shapes)

    def wrapped(*refs):
        main_in, side_in = refs[:n_in], refs[n_in:n_in + s_in]
        main_out = refs[n_in + s_in:n_in + s_in + n_out]
        side_out = refs[n_in + s_in + n_out:n_in + s_in + n_out + s_out]
        rest = refs[n_in + s_in + n_out + s_out:]
        main_scratch, (ssem, rsem) = rest[:len(scratch)], rest[len(scratch):]
        ids = [pl.program_id(ax) for ax in range(len(grid))]
        first = functools.reduce(jnp.logical_and, [i == 0 for i in ids])
        last = functools.reduce(jnp.logical_and, [i == g - 1 for i, g in zip(ids, grid)])

        @pl.when(first)
        def _():
            for cp in side.issue(side_in, side_out, ssem, rsem):
                cp.start()

        body(*main_in, *main_out, *main_scratch)

        @pl.when(last)
        def _():
            cps = side.issue(side_in, side_out, ssem, rsem)
            for cp in cps:
                cp.wait_recv()
            for cp in cps:
                cp.wait_send()

    outs = pl.pallas_call(
        wrapped, name=name, grid=grid, in_specs=list(in_specs) + [ANY] * s_in, out_specs=list(out_specs) + [ANY] * s_out,
        out_shape=list(out_shape) + side.out_shapes,
        input_output_aliases={n_in + i: n_out + o for i, o in side.aliases.items()},
        scratch_shapes=scratch + [pltpu.SemaphoreType.DMA((side.n_sems,))] * 2,
        compiler_params=_params(*["arbitrary"] * len(grid)),
    )(*args, *side.ins)
    side.outs = list(outs[n_out:])
    return list(outs[:n_out])


def _run_side(side, name):
    s_in = len(side.ins)

    def body(*refs):
        ssem, rsem = refs[s_in + len(side.out_shapes):]
        cps = side.issue(refs[:s_in], refs[s_in:s_in + len(side.out_shapes)], ssem, rsem)
        for cp in cps:
            cp.start()
        for cp in cps:
            cp.wait_recv()
        for cp in cps:
            cp.wait_send()

    side.outs = list(pl.pallas_call(
        body, name=name, in_specs=[ANY] * s_in, out_specs=[ANY] * len(side.out_shapes), out_shape=side.out_shapes,
        input_output_aliases=side.aliases, scratch_shapes=[pltpu.SemaphoreType.DMA((side.n_sems,))] * 2,
    )(*side.ins))
    return side.outs


def _loss_head(y, tgt, name):
    T, Dm = y.shape
    tm = _tile(T, 512)

    def body(y_ref, t_ref, dy_ref, loss_ref):
        @pl.when(pl.program_id(0) == 0)
        def _():
            loss_ref[...] = jnp.zeros_like(loss_ref)

        diff = y_ref[...] - t_ref[...]
        dy_ref[...] = diff * (1.0 / Dm)
        part = jnp.sum(jnp.mean(diff * diff, axis=-1, keepdims=True), axis=0, keepdims=True)
        loss_ref[...] += 0.5 * part

    row = pl.BlockSpec((tm, Dm), lambda i: (i, 0))
    one = pl.BlockSpec((1, 1), lambda i: (0, 0))
    return pl.pallas_call(
        body, name=name, grid=(T // tm,),
        in_specs=[row, row], out_specs=[row, one],
        out_shape=[jax.ShapeDtypeStruct((T, Dm), F32), jax.ShapeDtypeStruct((1, 1), F32)],
        compiler_params=_params("arbitrary"),
    )(y, tgt)


def _mm_nn(a, b, name, out_dtype, res=None, scale=1.0, tm_target=512, side=None):
    M, K = a.shape
    N = b.shape[1]
    tm = _tile(M, tm_target)

    def body(a_ref, b_ref, *rest):
        acc = _dot(a_ref[...].astype(CDT), b_ref[...])
        if res is None:
            (o_ref,) = rest
        else:
            r_ref, o_ref = rest
            acc = r_ref[...] + scale * acc
        o_ref[...] = acc.astype(o_ref.dtype)

    in_specs = [pl.BlockSpec((tm, K), lambda i: (i, 0)), pl.BlockSpec((K, N), lambda i: (0, 0))]
    args = [a, b]
    if res is not None:
        in_specs.append(pl.BlockSpec((tm, N), lambda i: (i, 0)))
        args.append(res)
    return _pcall(body, name, (M // tm,), in_specs, [pl.BlockSpec((tm, N), lambda i: (i, 0))],
                  [jax.ShapeDtypeStruct((M, N), out_dtype)], args, ("parallel",), side)[0]


def _mm_nt_blocks(a, b4, name, out_dtype, tm_target=512):
    M, K = a.shape
    nb, N, Kb = b4.shape
    tm = _tile(M, tm_target)

    def body(a_ref, b_ref, o_ref):
        acc = _dot_nt(a_ref[:, :Kb].astype(CDT), b_ref[0])
        for j in range(1, nb):
            acc = acc + _dot_nt(a_ref[:, j * Kb:(j + 1) * Kb].astype(CDT), b_ref[j])
        o_ref[...] = acc.astype(o_ref.dtype)

    return pl.pallas_call(
        body, name=name, grid=(M // tm,),
        in_specs=[pl.BlockSpec((tm, K), lambda i: (i, 0)), pl.BlockSpec(b4.shape, lambda i: (0, 0, 0))],
        out_specs=pl.BlockSpec((tm, N), lambda i: (i, 0)),
        out_shape=jax.ShapeDtypeStruct((M, N), out_dtype), compiler_params=_params("parallel"),
    )(a, b4)


def _mm_tn(x, dy, name, scale=1.0, col_blocks=1, tn_target=1664, tm_target=1408, tk_target=1024, side=None):
    T, M = x.shape
    split = dy.ndim == 3
    Nh = dy.shape[-1]
    N = 2 * Nh if split else Nh
    nb = N // col_blocks
    whole = col_blocks > 1 and not split and N <= tn_target
    tm = _tile(M, tm_target, LANES)
    tn = N if whole else _tile(math.gcd(Nh, nb), tn_target, LANES)
    tk = _tile(T, tk_target)
    nk = T // tk
    njh, njb = Nh // tn, max(nb // tn, 1)

    def body(x_ref, dy_ref, o_ref, acc_ref):
        k = pl.program_id(2)

        @pl.when(k == 0)
        def _():
            acc_ref[...] = jnp.zeros_like(acc_ref)

        acc_ref[...] += _dot_tn(x_ref[...].astype(CDT), dy_ref[...].astype(CDT))

        @pl.when(k == nk - 1)
        def _():
            res = (acc_ref[...] if scale == 1.0 else scale * acc_ref[...]).astype(o_ref.dtype)
            if whole:
                for b in range(col_blocks):
                    o_ref[b] = res[:, b * nb:(b + 1) * nb]
            else:
                o_ref[...] = res

    if split:
        dy_spec = pl.BlockSpec((None, tk, tn), lambda i, j, k: (j // njh, k, j % njh))
    else:
        dy_spec = pl.BlockSpec((tk, tn), lambda i, j, k: (k, j))
    if col_blocks == 1:
        out_spec, out_dims = pl.BlockSpec((tm, tn), lambda i, j, k: (i, j)), (M, N)
    elif whole:
        out_spec, out_dims = pl.BlockSpec((col_blocks, tm, nb), lambda i, j, k: (0, i, 0)), (col_blocks, M, nb)
    else:
        out_spec, out_dims = pl.BlockSpec((None, tm, tn), lambda i, j, k: (j // njb, i, j % njb)), (col_blocks, M, nb)
    return _pcall(body, name, (M // tm, N // tn, nk), [pl.BlockSpec((tk, tm), lambda i, j, k: (k, i)), dy_spec],
                  [out_spec], [jax.ShapeDtypeStruct(out_dims, WIRE_DT)], (x, dy), ("parallel", "parallel", "arbitrary"),
                  side, [pltpu.VMEM((tm, tn), F32)])[0]


def _mm_tn_parts(x, parts, name):
    T, M = x.shape
    widths = [p.shape[1] for p in parts]
    N = sum(widths)
    tk = _tile(T, 512)

    def body(x_ref, *refs):
        o_ref = refs[-1]

        @pl.when(pl.program_id(0) == 0)
        def _():
            o_ref[...] = jnp.zeros_like(o_ref)

        parts_cat = jnp.concatenate([p_ref[...].astype(CDT) for p_ref in refs[:-1]], axis=1)
        o_ref[...] += _dot_tn(x_ref[...].astype(CDT), parts_cat)

    return pl.pallas_call(
        body, name=name, grid=(T // tk,),
        in_specs=[pl.BlockSpec((tk, M), lambda k: (k, 0))] + [pl.BlockSpec((tk, wd), lambda k: (k, 0)) for wd in widths],
        out_specs=pl.BlockSpec((M, N), lambda k: (0, 0)),
        out_shape=jax.ShapeDtypeStruct((M, N), F32), compiler_params=_params("arbitrary"),
    )(x, *parts)


def _rmsnorm_rows(x_ref, g_ref):
    xv = x_ref[...]
    r = lax.rsqrt(jnp.mean(xv * xv, axis=-1, keepdims=True) + EPS)
    return (xv * r * g_ref[...]).astype(CDT)


def _ffn_up(x, ln, wgu, name, side=None):
    T, Dm = x.shape
    Fd = wgu.shape[1] // 2
    tm = _tile(T, 256)

    def body(x_ref, ln_ref, wg_ref, wu_ref, h_ref, gu_ref, a_ref):
        hv = _rmsnorm_rows(x_ref, ln_ref)
        h_ref[...] = hv
        g = _dot(hv, wg_ref[...])
        u = _dot(hv, wu_ref[...])
        sg = _sigmoid(g)
        silu = g * sg
        a_ref[...] = (silu * u).astype(a_ref.dtype)
        gu_ref[0] = (0.5 * u * (sg * (1.0 + g * (1.0 - sg)))).astype(gu_ref.dtype)
        gu_ref[1] = (0.5 * silu).astype(gu_ref.dtype)

    row = pl.BlockSpec((tm, Dm), lambda i: (i, 0))
    return _pcall(
        body, name, (T // tm,),
        [row, pl.BlockSpec((1, Dm), lambda i: (0, 0)),
         pl.BlockSpec((Dm, Fd), lambda i: (0, 0), pipeline_mode=pl.Buffered(1)),
         pl.BlockSpec((Dm, Fd), lambda i: (0, 1), pipeline_mode=pl.Buffered(1))],
        [row, pl.BlockSpec((2, tm, Fd), lambda i: (0, i, 0)), pl.BlockSpec((tm, Fd), lambda i: (i, 0))],
        [jax.ShapeDtypeStruct((T, Dm), CDT), jax.ShapeDtypeStruct((2, T, Fd), CDT), jax.ShapeDtypeStruct((T, Fd), CDT)],
        (x, ln.reshape(1, Dm), wgu, wgu), ("parallel",), side)


def _ffn_down_bwd(dxo, wd, gu, name, side=None):
    T, Dm = dxo.shape
    Fd = wd.shape[0]
    tm = _tile(T, 256)

    def body(dx_ref, wd_ref, gu_ref, dgu_ref):
        da = _dot_nt(dx_ref[...].astype(CDT), wd_ref[...])
        dgu_ref[0] = (da * gu_ref[0].astype(F32)).astype(dgu_ref.dtype)
        dgu_ref[1] = (da * gu_ref[1].astype(F32)).astype(dgu_ref.dtype)

    gu_spec = pl.BlockSpec((2, tm, Fd), lambda i: (0, i, 0))
    return _pcall(
        body, name, (T // tm,),
        [pl.BlockSpec((tm, Dm), lambda i: (i, 0)),
         pl.BlockSpec((Fd, Dm), lambda i: (0, 0), pipeline_mode=pl.Buffered(1)), gu_spec],
        [gu_spec], [jax.ShapeDtypeStruct((2, T, Fd), CDT)],
        (dxo, wd, gu), ("parallel",), side)[0]


def _mm_nt_norm_bwd(a_parts, b, x, g, dres, name, side=None):
    T, Dm = x.shape
    tm = _tile(T, 256)

    def b_cols(b_ref, lo, wd):
        if b.ndim == 2:
            return [(0, wd, b_ref[:, lo:lo + wd])]
        kb = b.shape[2]
        return [(j * kb - lo, kb, b_ref[j]) for j in range(lo // kb, (lo + wd) // kb)]

    def body(*refs):
        a_refs, (b_ref, x_ref, g_ref, dres_ref, dx_ref, dg_ref) = refs[:len(a_parts)], refs[len(a_parts):]

        @pl.when(pl.program_id(0) == 0)
        def _():
            dg_ref[...] = jnp.zeros_like(dg_ref)

        dh, lo = None, 0
        if b.ndim == 2 and len(a_parts) > 1 and all(p.ndim == 2 for p in a_parts):
            dh = _dot_nt(jnp.concatenate([a_ref[...].astype(CDT) for a_ref in a_refs], axis=1), b_ref[...])
            a_refs = ()
        for a_ref, part in zip(a_refs, a_parts):
            slabs = [a_ref] if part.ndim == 2 else [a_ref.at[s_] for s_ in range(part.shape[0])]
            for slab in slabs:
                for off, wd, bv in b_cols(b_ref, lo, part.shape[-1]):
                    term = _dot_nt(slab[:, off:off + wd].astype(CDT), bv)
                    dh = term if dh is None else dh + term
                lo += part.shape[-1]
        xv = x_ref[...]
        r = lax.rsqrt(jnp.mean(xv * xv, axis=-1, keepdims=True) + EPS)
        xh = xv * r
        dg_ref[...] += jnp.sum(dh * xh, axis=0, keepdims=True)
        dxh = dh * g_ref[...]
        dx_ref[...] = dres_ref[...] + r * (dxh - xh * jnp.mean(dxh * xh, axis=-1, keepdims=True))

    row = pl.BlockSpec((tm, Dm), lambda i: (i, 0))
    vec = pl.BlockSpec((1, Dm), lambda i: (0, 0))
    a_specs = [pl.BlockSpec((tm, p.shape[1]), lambda i: (i, 0)) if p.ndim == 2 else
               pl.BlockSpec((p.shape[0], tm, p.shape[2]), lambda i: (0, i, 0)) for p in a_parts]
    b_spec = pl.BlockSpec(b.shape, lambda i: (0,) * b.ndim, pipeline_mode=pl.Buffered(1))
    return _pcall(body, name, (T // tm,), a_specs + [b_spec, row, vec, row], [row, vec],
                  [jax.ShapeDtypeStruct((T, Dm), F32), jax.ShapeDtypeStruct((1, Dm), F32)],
                  (*a_parts, b, x, g.reshape(1, Dm), dres), ("arbitrary",), side)


def _mm_in(x, ln, w_in, gqk, cos_t, sin_t, name, side=None):
    T, Dm = x.shape
    tm = _tile(T, 256)
    widths = (POOL_DIM, QK_DIM, KV_DIM, GATE_DIM)

    def body(x_ref, ln_ref, w_ref, g_ref, c_ref, s_ref, h_ref, zu_ref, zqk_ref, zv_ref, zg_ref, qkn_ref):
        hv = _rmsnorm_rows(x_ref, ln_ref)
        h_ref[...] = hv
        z = _dot(hv, w_ref[...])
        lo = 0
        for o_ref, wd in zip((zu_ref, zqk_ref, zv_ref, zg_ref), widths):
            o_ref[...] = z[:, lo:lo + wd]
            lo += wd
        first, low = _lane_masks()
        cosv, sinv = c_ref[...], s_ref[...]
        for c in range(QK_DIM // LANES):
            sl = slice(c * LANES, (c + 1) * LANES)
            xv = z[:, POOL_DIM + c * LANES:POOL_DIM + (c + 1) * LANES]
            r = lax.rsqrt(_head_mean(xv * xv, first) + EPS)
            xn = xv * r * g_ref[:, sl]
            qkn_ref[:, sl] = (xn * cosv + _rope_partner(xn, low) * sinv).astype(qkn_ref.dtype)

    row = pl.BlockSpec((tm, Dm), lambda i: (i, 0))
    tab = pl.BlockSpec((tm, LANES), lambda i: (i, 0))
    return _pcall(body, name, (T // tm,),
                  [row, pl.BlockSpec((1, Dm), lambda i: (0, 0)),
                   pl.BlockSpec(w_in.shape, lambda i: (0, 0), pipeline_mode=pl.Buffered(1)),
                   pl.BlockSpec((1, QK_DIM), lambda i: (0, 0)), tab, tab],
                  [row] + [pl.BlockSpec((tm, wd), lambda i: (i, 0)) for wd in widths + (QK_DIM,)],
                  [jax.ShapeDtypeStruct((T, Dm), CDT)] + [jax.ShapeDtypeStruct((T, wd), F32) for wd in widths]
                  + [jax.ShapeDtypeStruct((T, QK_DIM), CDT)],
                  (x, ln.reshape(1, Dm), w_in, gqk, cos_t, sin_t), ("parallel",), side)


def _window_mean_minus_token(ext, u, g, w, pos):
    sl = slice(g * GROUP, (g + 1) * GROUP)
    s = ext[:, sl]
    span = 1
    while span < w:
        s = s + pltpu.roll(s, span, axis=0)
        span *= 2
    cnt = jnp.minimum(pos + 1, w).astype(F32)
    return s[POOL_WMAX:, :] / cnt - u[:, sl]


def _pool_fwd(zu, pool_w, scale, name):
    T = zu.shape[0]
    tm = _tile(T, 512, POOL_WMAX)
    hb = tm // POOL_WMAX

    def body(u_ref, halo_ref, pw_ref, sc_ref, pm_ref):
        i = pl.program_id(0)
        u = u_ref[...]
        halo = jnp.where(i > 0, halo_ref[...], 0.0)
        ext = jnp.concatenate([halo, u], axis=0)
        pos = i * tm + lax.broadcasted_iota(jnp.int32, (tm, 1), 0)
        ys = []
        for g, w in enumerate(POOL_WINDOWS):
            d = _window_mean_minus_token(ext, u, g, w, pos)
            ys.append(_dot(d.astype(CDT), pw_ref[g]))
        pm_ref[...] = (jnp.concatenate(ys, axis=1) * sc_ref[...]).astype(pm_ref.dtype)

    row = pl.BlockSpec((tm, POOL_DIM), lambda i: (i, 0))
    return pl.pallas_call(
        body, name=name, grid=(T // tm,),
        in_specs=[row, pl.BlockSpec((POOL_WMAX, POOL_DIM), lambda i: (jnp.maximum(i * hb - 1, 0), 0)),
                  pl.BlockSpec(pool_w.shape, lambda i: (0, 0, 0)), pl.BlockSpec((1, POOL_DIM), lambda i: (0, 0))],
        out_specs=row, out_shape=jax.ShapeDtypeStruct((T, POOL_DIM), CDT),
        compiler_params=_params("parallel"),
    )(zu, zu, pool_w, scale.reshape(1, POOL_DIM))


def _pool_bwd(zu, dpm, pool_w, scale, name):
    T = zu.shape[0]
    tm = _tile(T, 512, POOL_WMAX)
    hb = tm // POOL_WMAX
    nsteps = T // tm
    ext_rows = tm + POOL_WMAX

    def body(u_ref, halo_ref, dpm_ref, dnext_ref, pw_ref, sc_ref, du_ref, dpw_ref, dsc_ref):
        i = pl.program_id(0)

        @pl.when(i == 0)
        def _():
            dpw_ref[...] = jnp.zeros_like(dpw_ref)
            dsc_ref[...] = jnp.zeros_like(dsc_ref)

        u = u_ref[...]
        halo = jnp.where(i > 0, halo_ref[...], 0.0)
        ext = jnp.concatenate([halo, u], axis=0)
        dpm_t = dpm_ref[...].astype(F32)
        dnext = jnp.where(i < nsteps - 1, dnext_ref[...].astype(F32), 0.0)
        dext = jnp.concatenate([dpm_t, dnext], axis=0)
        sc = sc_ref[...]
        pos = i * tm + lax.broadcasted_iota(jnp.int32, (tm, 1), 0)
        pos_ext = i * tm + lax.broadcasted_iota(jnp.int32, (ext_rows, 1), 0)
        dus, dscs = [], []
        for g, w in enumerate(POOL_WINDOWS):
            sl = slice(g * GROUP, (g + 1) * GROUP)
            dc = _window_mean_minus_token(ext, u, g, w, pos).astype(CDT)
            y = _dot(dc, pw_ref[g])
            dscs.append(jnp.sum(dpm_t[:, sl] * y, axis=0, keepdims=True))
            dy_ext = (dext[:, sl] * sc[:, sl]).astype(CDT)
            dpw_ref[g] += _dot_tn(dc, dy_ext[:tm])
            dd = _dot_nt(dy_ext, pw_ref[g])
            r = dd / jnp.minimum(pos_ext + 1, w).astype(F32)
            span = 1
            while span < w:
                r = r + pltpu.roll(r, ext_rows - span, axis=0)
                span *= 2
            dus.append(r[:tm] - dd[:tm])
        du_ref[...] = jnp.concatenate(dus, axis=1).astype(du_ref.dtype)
        dsc_ref[...] += jnp.concatenate(dscs, axis=1)

    row = pl.BlockSpec((tm, POOL_DIM), lambda i: (i, 0))
    prev = pl.BlockSpec((POOL_WMAX, POOL_DIM), lambda i: (jnp.maximum(i * hb - 1, 0), 0))
    nxt = pl.BlockSpec((POOL_WMAX, POOL_DIM), lambda i: (jnp.minimum((i + 1) * hb, nsteps * hb - 1), 0))
    return pl.pallas_call(
        body, name=name, grid=(nsteps,),
        in_specs=[row, prev, row, nxt, pl.BlockSpec(pool_w.shape, lambda i: (0, 0, 0)),
                  pl.BlockSpec((1, POOL_DIM), lambda i: (0, 0))],
        out_specs=[row, pl.BlockSpec(pool_w.shape, lambda i: (0, 0, 0)), pl.BlockSpec((1, POOL_DIM), lambda i: (0, 0))],
        out_shape=[jax.ShapeDtypeStruct((T, POOL_DIM), CDT), jax.ShapeDtypeStruct(pool_w.shape, F32),
                   jax.ShapeDtypeStruct((1, POOL_DIM), F32)],
        compiler_params=_params("arbitrary"),
    )(zu, zu, dpm, dpm, pool_w, scale.reshape(1, POOL_DIM))


def _rope_tables(T):
    pos = jnp.arange(T, dtype=F32)
    inv_freq = ROPE_THETA ** (-jnp.arange(0, ROT_DIM, 2, dtype=F32) / ROT_DIM)
    ang = pos[:, None] * inv_freq[None, :]
    cos, sin = jnp.cos(ang), jnp.sin(ang)
    rest = HEAD_DIM - ROT_DIM
    cos_h = jnp.concatenate([cos, cos, jnp.ones((T, rest), F32)], axis=1)
    sin_h = jnp.concatenate([-sin, sin, jnp.zeros((T, rest), F32)], axis=1)
    return jnp.tile(cos_h, (1, 2)), jnp.tile(sin_h, (1, 2))


def _lane_masks():
    lane = lax.broadcasted_iota(jnp.int32, (1, LANES), 1)
    in_head = lane % HEAD_DIM
    return lane < HEAD_DIM, in_head < ROT_DIM // 2


def _rope_partner(v, low):
    lane = lax.broadcasted_iota(jnp.int32, (1, LANES), 1)
    swapped = jnp.where(low, pltpu.roll(v, LANES - ROT_DIM // 2, axis=1), pltpu.roll(v, ROT_DIM // 2, axis=1))
    return jnp.where(lane % HEAD_DIM < ROT_DIM, swapped, 0.0)


def _head_mean(v, first):
    lo = jnp.sum(jnp.where(first, v, 0.0), axis=-1, keepdims=True)
    hi = jnp.sum(jnp.where(first, 0.0, v), axis=-1, keepdims=True)
    return jnp.where(first, lo, hi) * (1.0 / HEAD_DIM)


def _qk_bwd(dqk, zqk, gqk, cos_t, sin_t, name):
    T = zqk.shape[0]
    tm = _tile(T, 512)

    def body(d_ref, z_ref, g_ref, c_ref, s_ref, dz_ref, dg_ref):
        @pl.when(pl.program_id(0) == 0)
        def _():
            dg_ref[...] = jnp.zeros_like(dg_ref)

        first, low = _lane_masks()
        cosv, sinv = c_ref[...], s_ref[...]
        dgs = []
        for c in range(QK_DIM // LANES):
            sl = slice(c * LANES, (c + 1) * LANES)
            dout = d_ref[:, sl]
            dxn = dout * cosv + _rope_partner(dout * sinv, low)
            xv = z_ref[:, sl]
            r = lax.rsqrt(_head_mean(xv * xv, first) + EPS)
            xh = xv * r
            dgs.append(jnp.sum(dxn * xh, axis=0, keepdims=True))
            dxh = dxn * g_ref[:, sl]
            dz_ref[:, sl] = (r * (dxh - xh * _head_mean(dxh * xh, first))).astype(dz_ref.dtype)
        dg_ref[...] += jnp.concatenate(dgs, axis=1)

    row = pl.BlockSpec((tm, QK_DIM), lambda i: (i, 0))
    tab = pl.BlockSpec((tm, LANES), lambda i: (i, 0))
    vec = pl.BlockSpec((1, QK_DIM), lambda i: (0, 0))
    return pl.pallas_call(
        body, name=name, grid=(T // tm,),
        in_specs=[row, row, vec, tab, tab], out_specs=[row, vec],
        out_shape=[jax.ShapeDtypeStruct((T, QK_DIM), CDT), jax.ShapeDtypeStruct((1, QK_DIM), F32)],
        compiler_params=_params("arbitrary"),
    )(dqk, zqk, gqk, cos_t, sin_t)


def _dup_half(v, first, kv):
    swapped = pltpu.roll(v, HEAD_DIM, axis=1)
    return jnp.where(first, v, swapped) if kv == 0 else jnp.where(first, swapped, v)


HEADS_PER_KV = 4
HEAD_STACK_FWD = 1
HEAD_STACK_BWD = 2


def _attn_bias(stack):
    qi = lax.broadcasted_iota(jnp.int32, (stack * BLOCK, 2 * BLOCK), 0) % BLOCK
    ki = lax.broadcasted_iota(jnp.int32, (stack * BLOCK, 2 * BLOCK), 1)
    diff = qi + BLOCK - ki
    band = (diff >= 0) & (diff < BLOCK)
    return jnp.stack([jnp.where(band, 0.0, -jnp.inf), jnp.where(band & (ki >= BLOCK), 0.0, -jnp.inf)]).astype(F32)


def _attn_blocks(T):
    return _tile(T // BLOCK, 4, 1)


def _stack_heads(ref, rows, kv, heads, first):
    parts = []
    for h in heads:
        c = 2 * kv + h // 2
        v = ref[rows, c * LANES:(c + 1) * LANES].astype(CDT)
        zero = jnp.zeros_like(v)
        parts.append(jnp.where(first, v, zero) if h % 2 == 0 else jnp.where(first, zero, v))
    return parts[0] if len(parts) == 1 else jnp.concatenate(parts, axis=0)


def _row_blocks(v, n):
    return [v[b * BLOCK:(b + 1) * BLOCK] for b in range(n)]


def _sink_column(sink_ref, kv, heads):
    cols = [jnp.full((BLOCK, 1), sink_ref[HEADS_PER_KV * kv + h], F32) for h in heads]
    return cols[0] if len(cols) == 1 else jnp.concatenate(cols, axis=0)


def _head_groups(stack):
    return [tuple(range(g, g + stack)) for g in range(0, HEADS_PER_KV, stack)]


def _softmax_with_sink(qst, kdup, sinkcol, bias):
    s = _dot_nt(qst, kdup) * ATTN_SCALE + bias
    m = jnp.maximum(jnp.max(s, axis=-1, keepdims=True), sinkcol)
    pu = jnp.exp(s - m)
    denom = jnp.sum(pu, axis=-1, keepdims=True) + jnp.exp(sinkcol - m)
    return pu * (1.0 / denom), m + jnp.log(denom)


def _attn_fwd(qkn, zv, sinks, name, side=None):
    T = qkn.shape[0]
    R = _attn_blocks(T)
    tq = R * BLOCK

    def body(sink_ref, bias_ref, qk_ref, qkp_ref, v_ref, vp_ref, o_ref, lse_ref):
        i = pl.program_id(0)
        first, _ = _lane_masks()
        lane = lax.broadcasted_iota(jnp.int32, (1, LANES), 1)
        kall = jnp.concatenate([qkp_ref[:, ATTN_DIM:], qk_ref[:, ATTN_DIM:]], axis=0)
        vall = jnp.concatenate([vp_ref[...], v_ref[...]], axis=0).astype(CDT)
        for r in range(R):
            bias = bias_ref[jnp.where(i == 0, 1, 0)] if r == 0 else bias_ref[0]
            rows = slice(r * BLOCK, (r + 2) * BLOCK)
            qrows = slice(r * BLOCK, (r + 1) * BLOCK)
            lse_rows = jnp.zeros((BLOCK, LANES), F32)
            for kv in range(2):
                kdup = _dup_half(kall[rows], first, kv)
                vdup = _dup_half(vall[rows], first, kv)
                res = []
                for heads in _head_groups(HEAD_STACK_FWD):
                    p, lse = _softmax_with_sink(_stack_heads(qk_ref, qrows, kv, heads, first), kdup,
                                                _sink_column(sink_ref, kv, heads), bias)
                    res += _row_blocks(_dot(p.astype(CDT), vdup), len(heads))
                    for b, col in enumerate(_row_blocks(lse, len(heads))):
                        lse_rows = jnp.where(lane == HEADS_PER_KV * kv + heads[b], col, lse_rows)
                o_ref[qrows, 2 * kv * LANES:(2 * kv + 1) * LANES] = jnp.where(first, res[0], res[1]).astype(o_ref.dtype)
                o_ref[qrows, (2 * kv + 1) * LANES:(2 * kv + 2) * LANES] = jnp.where(first, res[2], res[3]).astype(o_ref.dtype)
            lse_ref[qrows, :] = lse_rows

    bias = _attn_bias(HEAD_STACK_FWD)
    prev = lambda i: (jnp.maximum(i * R - 1, 0), 0)
    return _pcall(
        body, name, (T // tq,),
        [pl.BlockSpec(memory_space=pltpu.SMEM), pl.BlockSpec(bias.shape, lambda i: (0, 0, 0)),
         pl.BlockSpec((tq, QK_DIM), lambda i: (i, 0)), pl.BlockSpec((BLOCK, QK_DIM), prev),
         pl.BlockSpec((tq, KV_DIM), lambda i: (i, 0)), pl.BlockSpec((BLOCK, KV_DIM), prev)],
        [pl.BlockSpec((tq, ATTN_DIM), lambda i: (i, 0)), pl.BlockSpec((tq, LANES), lambda i: (i, 0))],
        [jax.ShapeDtypeStruct((T, ATTN_DIM), CDT), jax.ShapeDtypeStruct((T, LANES), F32)],
        (sinks, bias, qkn, qkn, zv, zv), ("parallel",), side)


def _attn_bwd(qkn, zv, sinks, do, o, lse, name, side=None):
    T = qkn.shape[0]
    R = _attn_blocks(T)
    tq = R * BLOCK

    def body(sink_ref, bias_ref, qk_ref, qkp_ref, v_ref, vp_ref, do_ref, o_ref, lse_ref,
             dq_ref, dkc_ref, dkp_ref, dvc_ref, dvp_ref, ds_ref):
        i = pl.program_id(0)

        @pl.when(i == 0)
        def _():
            ds_ref[...] = jnp.zeros_like(ds_ref)

        first, _ = _lane_masks()
        lane = lax.broadcasted_iota(jnp.int32, (1, LANES), 1)
        kall = jnp.concatenate([qkp_ref[:, ATTN_DIM:], qk_ref[:, ATTN_DIM:]], axis=0)
        vall = jnp.concatenate([vp_ref[...], v_ref[...]], axis=0).astype(CDT)
        for r in range(R):
            bias = bias_ref[jnp.where(i == 0, 1, 0)] if r == 0 else bias_ref[0]
            rows = slice(r * BLOCK, (r + 2) * BLOCK)
            qrows = slice(r * BLOCK, (r + 1) * BLOCK)
            dk_out, dv_out = [], []
            lse_rows = lse_ref[qrows, :]
            for kv in range(2):
                kdup = _dup_half(kall[rows], first, kv)
                vdup = _dup_half(vall[rows], first, kv)
                dq_h = []
                dk_acc = jnp.zeros((2 * BLOCK, LANES), F32)
                dv_acc = jnp.zeros((2 * BLOCK, LANES), F32)
                for heads in _head_groups(HEAD_STACK_BWD):
                    qst = _stack_heads(qk_ref, qrows, kv, heads, first)
                    dost = _stack_heads(do_ref, qrows, kv, heads, first)
                    lse_cols, delta_cols = [], []
                    for h in heads:
                        cols = slice((2 * kv + h // 2) * LANES, (2 * kv + h // 2 + 1) * LANES)
                        prod = do_ref[qrows, cols].astype(F32) * o_ref[qrows, cols].astype(F32)
                        own = first if h % 2 == 0 else jnp.logical_not(first)
                        delta_cols.append(jnp.sum(jnp.where(own, prod, 0.0), axis=-1, keepdims=True))
                        lse_cols.append(jnp.sum(jnp.where(lane == HEADS_PER_KV * kv + h, lse_rows, 0.0), axis=-1, keepdims=True))
                    lse_col = lse_cols[0] if len(heads) == 1 else jnp.concatenate(lse_cols, axis=0)
                    delta = delta_cols[0] if len(heads) == 1 else jnp.concatenate(delta_cols, axis=0)
                    p = jnp.exp(_dot_nt(qst, kdup) * ATTN_SCALE + bias - lse_col)
                    dsc = (p * (_dot_nt(dost, vdup) - delta)).astype(CDT)
                    psink = jnp.exp(_sink_column(sink_ref, kv, heads) - lse_col)
                    for b, term in enumerate(_row_blocks(psink * delta, len(heads))):
                        row = HEADS_PER_KV * kv + heads[b]
                        ds_ref[row:row + 1, :] += jnp.sum(term, axis=0, keepdims=True)
                    dq_h += _row_blocks(_dot(dsc, kdup) * ATTN_SCALE, len(heads))
                    dk_acc = dk_acc + _dot_tn(dsc, qst) * ATTN_SCALE
                    dv_acc = dv_acc + _dot_tn(p.astype(CDT), dost)
                dq_ref[qrows, 2 * kv * LANES:(2 * kv + 1) * LANES] = jnp.where(first, dq_h[0], dq_h[1])
                dq_ref[qrows, (2 * kv + 1) * LANES:(2 * kv + 2) * LANES] = jnp.where(first, dq_h[2], dq_h[3])
                dk_out.append(dk_acc + pltpu.roll(dk_acc, HEAD_DIM, axis=1))
                dv_out.append(dv_acc + pltpu.roll(dv_acc, HEAD_DIM, axis=1))
            dk = jnp.where(first, dk_out[0], dk_out[1])
            dv = jnp.where(first, dv_out[0], dv_out[1])
            dkp_ref[qrows, :] = dk[:BLOCK]
            dkc_ref[qrows, :] = dk[BLOCK:]
            dvp_ref[qrows, :] = dv[:BLOCK]
            dvc_ref[qrows, :] = dv[BLOCK:]

    bias = _attn_bias(HEAD_STACK_BWD)
    prev = lambda i: (jnp.maximum(i * R - 1, 0), 0)
    kvrow = pl.BlockSpec((tq, KV_DIM), lambda i: (i, 0))
    qrow = pl.BlockSpec((tq, ATTN_DIM), lambda i: (i, 0))
    kv_shape = jax.ShapeDtypeStruct((T, KV_DIM), F32)
    return _pcall(
        body, name, (T // tq,),
        [pl.BlockSpec(memory_space=pltpu.SMEM), pl.BlockSpec(bias.shape, lambda i: (0, 0, 0)),
         pl.BlockSpec((tq, QK_DIM), lambda i: (i, 0)), pl.BlockSpec((BLOCK, QK_DIM), prev),
         kvrow, pl.BlockSpec((BLOCK, KV_DIM), prev), qrow, qrow, kvrow],
        [qrow, kvrow, kvrow, kvrow, kvrow, pl.BlockSpec((N_Q_HEADS, LANES), lambda i: (0, 0))],
        [jax.ShapeDtypeStruct((T, ATTN_DIM), F32), kv_shape, kv_shape, kv_shape, kv_shape,
         jax.ShapeDtypeStruct((N_Q_HEADS, LANES), F32)],
        (sinks, bias, qkn, qkn, zv, zv, do, o, lse), ("arbitrary",), side)


def _merge_fwd(pm, o, w_pb, w_ab, zg, name, side=None):
    T = pm.shape[0]
    tm = _tile(T, 512)

    def body(pm_ref, o_ref, wp_ref, wa_ref, zg_ref, m_ref, gp_ref, ga_ref, fp_ref, fa_ref):
        pmv, ov = pm_ref[...], o_ref[...]
        a = jnp.concatenate([_dot(pmv, wp_ref[j]) for j in range(N_CHIPS)], axis=1)
        b = jnp.concatenate([_dot(ov, wa_ref[j]) for j in range(N_CHIPS)], axis=1)
        gp = _sigmoid(zg_ref[:, :D_MODEL])
        ga = _sigmoid(zg_ref[:, D_MODEL:])
        ap, ba = gp * a, ga * b
        m_ref[...] = (ap + ba).astype(m_ref.dtype)
        gp_ref[...] = gp.astype(gp_ref.dtype)
        ga_ref[...] = ga.astype(ga_ref.dtype)
        fp_ref[...] = (ap * (1.0 - gp)).astype(fp_ref.dtype)
        fa_ref[...] = (ba * (1.0 - ga)).astype(fa_ref.dtype)

    half = pl.BlockSpec((tm, POOL_DIM), lambda i: (i, 0))
    full = pl.BlockSpec((tm, D_MODEL), lambda i: (i, 0))
    wspec = pl.BlockSpec(w_pb.shape, lambda i: (0, 0, 0))
    out = jax.ShapeDtypeStruct((T, D_MODEL), CDT)
    return _pcall(body, name, (T // tm,), [half, half, wspec, wspec, pl.BlockSpec((tm, GATE_DIM), lambda i: (i, 0))],
                  [full] * 5, [out] * 5, (pm, o, w_pb, w_ab, zg), ("parallel",), side)


def _merge_bwd(dxo, w_out, factors, name):
    T = dxo.shape[0]
    tm = _tile(T, 512)

    def body(dx_ref, w_ref, gp_ref, ga_ref, fp_ref, fa_ref, da_ref, db_ref, dg_ref):
        dm = _dot_nt(dx_ref[...].astype(CDT), w_ref[...])
        da_ref[...] = (dm * gp_ref[...].astype(F32)).astype(da_ref.dtype)
        db_ref[...] = (dm * ga_ref[...].astype(F32)).astype(db_ref.dtype)
        dg_ref[:, :D_MODEL] = (dm * fp_ref[...].astype(F32)).astype(dg_ref.dtype)
        dg_ref[:, D_MODEL:] = (dm * fa_ref[...].astype(F32)).astype(dg_ref.dtype)

    full = pl.BlockSpec((tm, D_MODEL), lambda i: (i, 0))
    gate = pl.BlockSpec((tm, GATE_DIM), lambda i: (i, 0))
    out = jax.ShapeDtypeStruct((T, D_MODEL), CDT)
    return pl.pallas_call(
        body, name=name, grid=(T // tm,),
        in_specs=[full, pl.BlockSpec((D_MODEL, D_MODEL), lambda i: (0, 0))] + [full] * 4,
        out_specs=[full, full, gate], out_shape=[out, out, jax.ShapeDtypeStruct((T, GATE_DIM), CDT)],
        compiler_params=_params("parallel"),
    )(dxo, w_out, *factors)


def _adamw(w, g, m, v, name):
    Rr, C = w.shape
    tr = _tile(Rr, max(8, (1 << 19) // C // 8 * 8))

    def body(w_ref, g_ref, m_ref, v_ref, go_ref, d_ref, nm_ref, nv_ref):
        gv = g_ref[...]
        go_ref[...] = gv
        nm = ADAM_B1 * m_ref[...] + (1.0 - ADAM_B1) * gv
        nv = ADAM_B2 * v_ref[...] + (1.0 - ADAM_B2) * (gv * gv)
        m_hat = nm / (1.0 - ADAM_B1 ** ADAM_STEP)
        v_hat = nv / (1.0 - ADAM_B2 ** ADAM_STEP)
        d_ref[...] = -ADAM_LR * (m_hat / (jnp.sqrt(v_hat) + ADAM_EPS) + ADAM_WD * w_ref[...])
        nm_ref[...] = nm
        nv_ref[...] = nv

    blk = pl.BlockSpec((tr, C), lambda i: (i, 0))
    out = jax.ShapeDtypeStruct((Rr, C), F32)
    return pl.pallas_call(
        body, name=name, grid=(Rr // tr,), in_specs=[blk] * 4, out_specs=[blk] * 4, out_shape=[out] * 4,
        compiler_params=_params("parallel"),
    )(w, g, m, v)


def _place():
    return lax.axis_index("x"), lax.axis_index("y"), lax.axis_index("c")


def _other_chip(x, y, d):
    return (1 - x if d & 2 else x), (1 - y if d & 1 else y)


def _rcopy(src, dst, ssem, rsem, dev):
    return pltpu.make_async_remote_copy(src_ref=src, dst_ref=dst, send_sem=ssem, recv_sem=rsem, device_id=dev,
                                        device_id_type=MESH)


def _row_half(rows, c):
    return pl.ds(c * (rows // 2), rows // 2)


def _is_wide(name):
    return name in WIDE


def _block(ref, wide, j, rows, n):
    if wide:
        return ref.at[rows, pl.ds(pl.multiple_of(j * n, LANES), n)]
    return ref.at[j, rows]


def _gathered_shape(shard, wide):
    _, a, n = shard.shape
    return jax.ShapeDtypeStruct((a, N_CHIPS * n) if wide else (N_CHIPS, a, n), shard.dtype)


def _gather_ici_side(shards, wides, l):
    k_of = lambda w, d: 3 * w + d - 1

    def issue(ins, outs, ssem, rsem):
        x, y, c = _place()
        cps = []
        for w, (shard, wide) in enumerate(zip(shards, wides)):
            _, a, n = shard.shape
            half = _row_half(a, c)
            for d in (1, 2, 3):
                px, py = _other_chip(x, y, d)
                cps.append(_rcopy(ins[w].at[l, half], _block(outs[w], wide, 2 * x + y, half, n),
                                  ssem.at[k_of(w, d)], rsem.at[k_of(w, d)], (px, py, c)))
        return cps

    return _Side(shards, [_gathered_shape(s_, wd) for s_, wd in zip(shards, wides)], 3 * len(shards), issue)


def _gather_d2d_side(shards, wides, gathered, l):
    nw = len(shards)

    def issue(ins, outs, ssem, rsem):
        x, y, c = _place()
        sibling = (x, y, 1 - c)
        cps = []
        for w, (shard, wide) in enumerate(zip(shards, wides)):
            _, a, n = shard.shape
            half = _row_half(a, c)
            for d in (1, 2, 3):
                px, py = _other_chip(x, y, d)
                k = 3 * w + d - 1
                got = _block(outs[w], wide, 2 * px + py, half, n)
                cps.append(_rcopy(got, got, ssem.at[k], rsem.at[k], sibling))
            cps.append(_rcopy(ins[nw + w].at[l], _block(outs[w], wide, 2 * x + y, pl.ds(0, a), n),
                              ssem.at[3 * nw + w], rsem.at[3 * nw + w], sibling))
        return cps

    return _Side(list(gathered) + list(shards), [jax.ShapeDtypeStruct(g.shape, g.dtype) for g in gathered], 4 * nw, issue,
                 aliases={w: w for w in range(nw)})


def _half_shape(g, wide):
    if wide:
        return jax.ShapeDtypeStruct((g.shape[0] // 2, g.shape[1]), g.dtype)
    return jax.ShapeDtypeStruct((N_CHIPS, g.shape[1] // 2, g.shape[2]), g.dtype)


def _reduce_sibling_side(gms, wides):
    def issue(ins, outs, ssem, rsem):
        x, y, c = _place()
        cps = []
        for w, (g, wide) in enumerate(zip(gms, wides)):
            src = ins[w].at[_row_half(g.shape[0], 1 - c)] if wide else ins[w].at[:, _row_half(g.shape[1], 1 - c)]
            cps.append(_rcopy(src, outs[w], ssem.at[w], rsem.at[w], (x, y, 1 - c)))
        return cps

    return _Side(gms, [_half_shape(g, wd) for g, wd in zip(gms, wides)], len(gms), issue)


def _reduce_chip_side(ps, wides):
    def slot_shape(p, wide):
        return jax.ShapeDtypeStruct((N_CHIPS, p.shape[0], p.shape[1] // N_CHIPS) if wide else p.shape, p.dtype)

    def issue(ins, outs, ssem, rsem):
        x, y, c = _place()
        cps = []
        for w, (p, wide) in enumerate(zip(ps, wides)):
            ah, n = (p.shape[0], p.shape[1] // N_CHIPS) if wide else p.shape[1:]
            for d in (1, 2, 3):
                px, py = _other_chip(x, y, d)
                k = 3 * w + d - 1
                cps.append(_rcopy(_block(ins[w], wide, 2 * px + py, pl.ds(0, ah), n), outs[w].at[2 * x + y],
                                  ssem.at[k], rsem.at[k], (px, py, c)))
        return cps

    return _Side(ps, [slot_shape(p, wd) for p, wd in zip(ps, wides)], 3 * len(ps), issue)


def _share_side(accs, items):
    def issue(ins, outs, ssem, rsem):
        x, y, c = _place()
        cps = []
        for k, (w, layer) in enumerate(items):
            mine = outs[w].at[layer, _row_half(accs[w].shape[1], c)]
            cps.append(_rcopy(mine, mine, ssem.at[k], rsem.at[k], (x, y, 1 - c)))
        return cps

    return _Side(accs, [jax.ShapeDtypeStruct(a.shape, a.dtype) for a in accs], len(items), issue,
                 aliases={w: w for w in range(len(accs))})


def _sum_rows(rows, b):
    return _tile(rows, max(16, (1 << 19) // b // 16 * 16), 16)


def _pair_sum(g, recv, wide, place, name):
    ah, b = recv.shape[-2:]
    ta = _sum_rows(ah, b)
    nr = ah // ta

    def body(p_ref, g_ref, r_ref, o_ref):
        o_ref[...] = (g_ref[...].astype(F32) + r_ref[...].astype(F32)).astype(o_ref.dtype)

    if wide:
        grid = (nr,)
        specs = [pl.BlockSpec((ta, b), lambda r, p: (p[0] * nr + r, 0)), pl.BlockSpec((ta, b), lambda r, p: (r, 0))]
        out_spec = pl.BlockSpec((ta, b), lambda r, p: (r, 0))
    else:
        grid = (N_CHIPS, nr)
        specs = [pl.BlockSpec((None, ta, b), lambda j, r, p: (j, p[0] * nr + r, 0)),
                 pl.BlockSpec((None, ta, b), lambda j, r, p: (j, r, 0))]
        out_spec = pl.BlockSpec((None, ta, b), lambda j, r, p: (j, r, 0))
    return pl.pallas_call(
        body, name=name,
        grid_spec=pltpu.PrefetchScalarGridSpec(num_scalar_prefetch=1, grid=grid, in_specs=specs, out_specs=out_spec),
        out_shape=jax.ShapeDtypeStruct(recv.shape, recv.dtype), compiler_params=_params(*["parallel"] * len(grid)),
    )(place, g, recv)


def _chip_sum(slots, part, wide, place, acc, l, name):
    _, ah, b = slots.shape
    ta = _sum_rows(ah, b)
    nr = ah // ta

    def body(p_ref, s_ref, own_ref, acc_ref, o_ref):
        j = p_ref[1]
        own = own_ref[...].astype(F32)
        term = [jnp.where(j == s_, own, s_ref[s_].astype(F32)) for s_ in range(N_CHIPS)]
        o_ref[...] = ((term[0] + term[1]) + term[2]) + term[3]

    own_spec = (pl.BlockSpec((ta, b), lambda r, p: (r, p[1])) if wide else
                pl.BlockSpec((None, ta, b), lambda r, p: (p[1], r, 0)))
    return pl.pallas_call(
        body, name=name,
        grid_spec=pltpu.PrefetchScalarGridSpec(
            num_scalar_prefetch=1, grid=(nr,),
            in_specs=[pl.BlockSpec((N_CHIPS, ta, b), lambda r, p: (0, r, 0)), own_spec, ANY],
            out_specs=pl.BlockSpec((None, ta, b), lambda r, p: (l, p[0] * nr + r, 0))),
        out_shape=jax.ShapeDtypeStruct(acc.shape, F32), input_output_aliases={3: 0},
        compiler_params=_params("parallel"),
    )(place, slots, part, acc)


def _small_side(v):
    def issue(ins, outs, ssem, rsem):
        x, y, c = _place()
        cps = []
        for d in range(1, N_DEV):
            px, py = _other_chip(x, y, d >> 1)
            pc = 1 - c if d & 1 else c
            cps.append(_rcopy(ins[0], outs[0].at[4 * x + 2 * y + c], ssem.at[d - 1], rsem.at[d - 1], (px, py, pc)))
        return cps

    return _Side([v], [jax.ShapeDtypeStruct((N_DEV,) + v.shape, v.dtype)], N_DEV - 1, issue)


def _small_sum(slots, v, place, name):
    def body(p_ref, s_ref, v_ref, o_ref):
        me = 2 * p_ref[1] + p_ref[0]
        acc = jnp.where(me == 0, v_ref[...], s_ref[0])
        for s_ in range(1, N_DEV):
            acc = acc + jnp.where(me == s_, v_ref[...], s_ref[s_])
        o_ref[...] = acc

    return pl.pallas_call(
        body, name=name,
        grid_spec=pltpu.PrefetchScalarGridSpec(
            num_scalar_prefetch=1, grid=(1,),
            in_specs=[pl.BlockSpec(slots.shape, lambda i, p: (0, 0, 0)), pl.BlockSpec(v.shape, lambda i, p: (0, 0))],
            out_specs=pl.BlockSpec(v.shape, lambda i, p: (0, 0))),
        out_shape=jax.ShapeDtypeStruct(v.shape, F32), compiler_params=_params("arbitrary"),
    )(place, slots, v)


def _ffn_forward(x, p, tag, side_of):
    h, gu, act = _ffn_up(x, p[f"ln_{tag}"], p[f"w_{tag}_gu"], f"{tag}_up", side_of(f"{tag}_up"))
    x_out = _mm_nn(act, p[f"w_{tag}_down"], f"{tag}_down", F32, res=x, scale=0.5, side=side_of(f"{tag}_down"))
    return x_out, (x, h, gu, act)


def _row_blocks_of(dw):
    return dw.reshape(N_CHIPS, dw.shape[0] // N_CHIPS, dw.shape[1])


def _ffn_backward(dxo, saved, p, tag, side_of, grad):
    x, h, gu, act = saved
    dgu = _ffn_down_bwd(dxo, p[f"w_{tag}_down"], gu, f"{tag}_down_bwd", side_of(f"{tag}_down_bwd"))
    grad(f"w_{tag}_down", _row_blocks_of(_mm_tn(act, dxo, f"{tag}_dwd", scale=0.5, side=side_of(f"{tag}_dwd"))))
    grad(f"w_{tag}_gu", _mm_tn(h, dgu, f"{tag}_dwgu", tn_target=2816, tm_target=1024, side=side_of(f"{tag}_dwgu")))
    dx, d_ln = _mm_nt_norm_bwd([dgu], p[f"w_{tag}_gu"], x, p[f"ln_{tag}"], dxo, f"{tag}_dh_norm_bwd",
                               side_of(f"{tag}_dh_norm_bwd"))
    grad(f"ln_{tag}", d_ln[0])
    return dx


def _mixer_forward(x, p, tabs, side_of):
    h, zu, zqk, zv, zg, qkn = _mm_in(x, p["ln_mix"], p["w_in"], p["gqk"], *tabs, "mix_in", side_of("mix_in"))
    pm = _pool_fwd(zu, p["pool_w"], p["pool_scale"], "pool_fwd")
    o, lse = _attn_fwd(qkn, zv, p["sinks"], "attn_fwd", side_of("attn_fwd"))
    m, *factors = _merge_fwd(pm, o, p["w_pool_branch"], p["w_attn_branch"], zg, "merge_fwd", side_of("merge_fwd"))
    x_out = _mm_nn(m, p["w_out"], "mix_out", F32, res=x, scale=1.0)
    return x_out, (x, h, zu, zqk, zv, pm, qkn, o, lse, factors, m)


def _shift_up(v):
    return jnp.concatenate([v[BLOCK:], jnp.zeros((BLOCK, v.shape[1]), v.dtype)], axis=0)


def _mixer_backward(dxo, saved, p, tabs, side_of, grad):
    x, h, zu, zqk, zv, pm, qkn, o, lse, factors, m = saved
    d_a, d_b, dgl = _merge_bwd(dxo, p["w_out"], factors, "merge_bwd")
    grad("w_out", _row_blocks_of(_mm_tn(m, dxo, "mix_dwout")))
    dpm = _mm_nt_blocks(d_a, p["w_pool_branch"], "pool_branch_dx", CDT)
    grad("w_pool_branch", _mm_tn(pm, d_a, "pool_branch_dw", col_blocks=N_CHIPS))
    do = _mm_nt_blocks(d_b, p["w_attn_branch"], "attn_branch_dx", CDT)
    grad("w_attn_branch", _mm_tn(o, d_b, "attn_branch_dw", col_blocks=N_CHIPS))
    du, d_pool_w, d_pool_scale = _pool_bwd(zu, dpm, p["pool_w"], p["pool_scale"], "pool_bwd")
    grad("pool_w", d_pool_w)
    grad("pool_scale", d_pool_scale)
    dq, dkc, dkp, dvc, dvp, dsink = _attn_bwd(qkn, zv, p["sinks"], do, o, lse, "attn_bwd", side_of("attn_bwd"))
    dqk = jnp.concatenate([dq, dkc + _shift_up(dkp)], axis=1)
    dv = dvc + _shift_up(dvp)
    dzqk, dgqk = _qk_bwd(dqk, zqk, p["gqk"], *tabs, "qk_bwd")
    grad("q_norm", dgqk[0, :ATTN_DIM].reshape(N_Q_HEADS, HEAD_DIM).sum(axis=0))
    grad("k_norm", dgqk[0, ATTN_DIM:].reshape(KV_DIM // HEAD_DIM, HEAD_DIM).sum(axis=0))
    grad("sinks", -dsink[:, 0])
    dz = [du, dzqk, dv, dgl]
    grad("w_in", _blocks_from_full("w_in", _mm_tn_parts(h, dz, "mix_dwin")).astype(WIRE_DT))
    dx, d_ln = _mm_nt_norm_bwd(dz, p["w_in"], x, p["ln_mix"], dxo, "mix_dh_norm_bwd", side_of("mix_dh_norm_bwd"))
    grad("ln_mix", d_ln[0])
    return dx


class _NoComm:
    def __init__(self, layers):
        self.layers, self.grads = layers, [dict() for _ in layers]

    def weight(self, l, name):
        return self.layers[l][name]

    def side(self, phase, l, host):
        return None

    def grad(self, l, name, value):
        self.grads[l][name] = value


class _Layer:
    def __init__(self, hooks, l):
        self.hooks, self.l, self.got = hooks, l, {}

    def __getitem__(self, name):
        if name not in self.got:
            self.got[name] = self.hooks.weight(self.l, name)
        return self.got[name]


def _local_step(x, tgt, n_layers, hooks):
    T = x.shape[0]
    tabs = _rope_tables(T)
    saved, params = [], []
    for l in range(n_layers):
        p = _Layer(hooks, l)
        side_of = functools.partial(hooks.side, "fwd", l)
        x, s1 = _ffn_forward(x, p, "ffn1", side_of)
        x, s2 = _mixer_forward(x, p, tabs, side_of)
        x, s3 = _ffn_forward(x, p, "ffn2", side_of)
        saved.append((s1, s2, s3))
        params.append(p)
    dx, loss = _loss_head(x, tgt, "loss_head")
    for l in reversed(range(n_layers)):
        p = params[l]
        s1, s2, s3 = saved[l]
        side_of = functools.partial(hooks.side, "bwd", l)
        grad = functools.partial(hooks.grad, l)
        dx = _ffn_backward(dx, s3, p, "ffn2", side_of, grad)
        dx = _mixer_backward(dx, s2, p, tabs, side_of, grad)
        dx = _ffn_backward(dx, s1, p, "ffn1", side_of, grad)
    return loss, dx


def _full_from_blocks(name, blocks):
    if name in COL_SHARDED:
        return jnp.transpose(blocks, (1, 0, 2)).reshape(blocks.shape[1], N_CHIPS * blocks.shape[2])
    return blocks.reshape(N_CHIPS * blocks.shape[1], blocks.shape[2])


def _blocks_from_full(name, full):
    K, N = full.shape
    if name in COL_SHARDED:
        return jnp.transpose(full.reshape(K, N_CHIPS, N // N_CHIPS), (1, 0, 2))
    return full.reshape(N_CHIPS, K // N_CHIPS, N)


JOBS = {"a": ("w_ffn1_gu", "w_ffn1_down"), "b": ("w_in", "w_pool_branch", "w_attn_branch", "w_out"),
        "c": ("w_ffn2_gu", "w_ffn2_down")}
GATHER_PLAN = {"ffn1_up": ("ici", "b", JOBS["b"], 0), "ffn1_down": ("d2d", "b", JOBS["b"], 0),
               "mix_in": ("ici", "c", JOBS["c"][:1], 0), "attn_fwd": ("ici", "c", JOBS["c"][1:], 0),
               "merge_fwd": ("d2d", "c", JOBS["c"], 0),
               "ffn2_up": ("ici", "a", JOBS["a"], 1), "ffn2_down": ("d2d", "a", JOBS["a"], 1)}
REDUCE_PLAN = {"ffn2_down_bwd": ("sibling", "a", 1), "ffn2_dwgu": ("chip", "a", 1),
               "ffn2_dh_norm_bwd": ("sibling", "c", 0), "attn_bwd": ("chip", "c", 0),
               "mix_dh_norm_bwd": ("sibling", "b", 0), "ffn1_down_bwd": ("chip", "b", 0)}
SHARE_HOST = "ffn1_dwgu"
SMALL_HOST = "ffn2_dwd"
LAST_GRAD = "ln_ffn1"


class _Exchange:
    def __init__(self, shards, small, place, n_layers):
        self.shards, self.small, self.place, self.n_layers = shards, small, place, n_layers
        first, wides = [shards[n] for n in JOBS["a"]], [_is_wide(n) for n in JOBS["a"]]
        got = _run_side(_gather_ici_side(first, wides, 0), "gather_ici")
        got = _run_side(_gather_d2d_side(first, wides, got, 0), "gather_d2d")
        self.blocks = {(n, 0): g for n, g in zip(JOBS["a"], got)}
        self.landed = {}
        self.handed = []
        self.acc = {n: lax.empty(shards[n].shape, F32) for n in BIG}
        self.grads = [dict() for _ in range(n_layers)]
        self.reduce = {}
        self.summed = set()
        self.unshared, self.sharing = [], None
        self.small_sides, self.small_waiting = {}, None

    def weight(self, l, name):
        if name not in BIG:
            return self.small(l)[name]
        for names, layer, done in self.handed:
            self.blocks.update({(n, layer): g for n, g in zip(names, done.outs)})
        self.handed.clear()
        blocks = self.blocks.pop((name, l))
        return blocks if name in USED_AS_BLOCKS + WIDE else _full_from_blocks(name, blocks)

    def _gather_side(self, l, host):
        step, job, names, ahead = GATHER_PLAN[host]
        layer = l + ahead
        if layer >= self.n_layers:
            return None
        if step == "ici":
            side = _gather_ici_side([self.shards[n] for n in names], [_is_wide(n) for n in names], layer)
            self.landed.setdefault((job, layer), []).append((names, side))
            return side
        names, gathered = JOBS[job], {}
        for part_names, side in self.landed.pop((job, layer)):
            gathered.update(zip(part_names, side.outs))
        done = _gather_d2d_side([self.shards[n] for n in names], [_is_wide(n) for n in names],
                                [gathered[n] for n in names], layer)
        self.handed.append((names, layer, done))
        return done

    def grad(self, l, name, value):
        self.grads[l][name] = value
        if name == LAST_GRAD and l > 0:
            packed, self.small_spans = _pack_small([self.grads[l][n] for n in SMALL])
            self.small_sides[l] = _small_side(packed)
            self.small_waiting = l

    def reduced_small(self, loss_part):
        packed, spans = _pack_small([self.grads[0][n] for n in SMALL] + [loss_part])
        self.small_sides[0] = _small_side(packed)
        _run_side(self.small_sides[0], "all_reduce_small")
        shapes = [self.grads[0][n].shape for n in SMALL]
        per_layer = []
        for l in range(self.n_layers):
            side = self.small_sides[l]
            summed = _small_sum(side.outs[0], side.ins[0], self.place, "small_sum")
            per_layer.append(_unpack_small(summed, spans, shapes + [(1, 1)] * (l == 0)))
        loss = per_layer[0][-1][0, 0]
        return {n: jnp.stack([vals[k] for vals in per_layer]) for k, n in enumerate(SMALL)}, loss

    def _reduce_side(self, l, host):
        step, job, ahead = REDUCE_PLAN[host]
        layer = l + ahead
        if layer >= self.n_layers:
            return None
        return self._reduce_step(step, job, layer)

    def _reduce_step(self, step, job, layer):
        if step == "sibling":
            st = self.reduce[(job, layer)] = dict(gm=[self.grads[layer][n] for n in JOBS[job]],
                                                  wide=[_is_wide(n) for n in JOBS[job]])
            st["sibling"] = _reduce_sibling_side(st["gm"], st["wide"])
            return st["sibling"]
        st = self.reduce[(job, layer)]
        st["part"] = [_pair_sum(g, r, wd, self.place, "grad_pair_sum")
                      for g, r, wd in zip(st["gm"], st["sibling"].outs, st["wide"])]
        st["chip"] = _reduce_chip_side(st["part"], st["wide"])
        return st["chip"]

    def _chip_sums(self):
        if self.sharing is not None:
            self.acc.update(zip(BIG, self.sharing.outs))
            self.sharing = None
        for (job, layer), st in self.reduce.items():
            if (job, layer) not in self.summed and "chip" in st and st["chip"].outs is not None:
                self.summed.add((job, layer))
                for n, slots, part, wd in zip(JOBS[job], st["chip"].outs, st["part"], st["wide"]):
                    self.acc[n] = _chip_sum(slots, part, wd, self.place, self.acc[n], layer, "grad_chip_sum")
                    self.unshared.append((BIG.index(n), layer))

    def _share(self):
        side = _share_side([self.acc[n] for n in BIG], self.unshared)
        self.unshared = []
        return side

    def side(self, phase, l, host):
        if phase == "fwd":
            return self._gather_side(l, host) if host in GATHER_PLAN else None
        self._chip_sums()
        if host == SMALL_HOST and self.small_waiting is not None:
            side, self.small_waiting = self.small_sides[self.small_waiting], None
            return side
        if host == SHARE_HOST and self.unshared:
            self.sharing = self._share()
            return self.sharing
        return self._reduce_side(l, host) if host in REDUCE_PLAN else None

    def reduced(self):
        _run_side(self._reduce_step("sibling", "a", 0), "grad_sibling_exchange")
        _run_side(self._reduce_step("chip", "a", 0), "grad_chip_exchange")
        self._chip_sums()
        return dict(zip(BIG, _run_side(self._share(), "grad_sibling_share")))


def _pack_small(parts):
    rows, spans, lo = [], [], 0
    for v in parts:
        flat = v.reshape(-1)
        nrow = -(-flat.shape[0] // LANES)
        flat = jnp.pad(flat, (0, nrow * LANES - flat.shape[0]))
        rows.append(flat.reshape(nrow, LANES))
        spans.append((lo, nrow))
        lo += nrow
    pad = -lo % 8
    if pad:
        rows.append(jnp.zeros((pad, LANES), F32))
    return jnp.concatenate(rows, axis=0), spans


def _unpack_small(packed, spans, shapes):
    out = []
    for (lo, nrow), shape in zip(spans, shapes):
        size = 1
        for s in shape:
            size *= s
        out.append(packed[lo:lo + nrow].reshape(-1)[:size].reshape(shape))
    return out


def kernel(x, ln_ffn1, w_ffn1_gu, w_ffn1_down, ln_mix, w_in, pool_w, pool_scale, w_pool_branch, q_norm, k_norm, sinks, w_attn_branch, w_out, ln_ffn2, w_ffn2_gu, w_ffn2_down, loss_target, m_ln_ffn1, m_w_ffn1_gu, m_w_ffn1_down, m_ln_mix, m_w_in, m_pool_w, m_pool_scale, m_w_pool_branch, m_q_norm, m_k_norm, m_sinks, m_w_attn_branch, m_w_out, m_ln_ffn2, m_w_ffn2_gu, m_w_ffn2_down, v_ln_ffn1, v_w_ffn1_gu, v_w_ffn1_down, v_ln_mix, v_w_in, v_pool_w, v_pool_scale, v_w_pool_branch, v_q_norm, v_k_norm, v_sinks, v_w_attn_branch, v_w_out, v_ln_ffn2, v_w_ffn2_gu, v_w_ffn2_down):
    w = dict(ln_ffn1=ln_ffn1, w_ffn1_gu=w_ffn1_gu, w_ffn1_down=w_ffn1_down, ln_mix=ln_mix, w_in=w_in, pool_w=pool_w,
             pool_scale=pool_scale, w_pool_branch=w_pool_branch, q_norm=q_norm, k_norm=k_norm, sinks=sinks,
             w_attn_branch=w_attn_branch, w_out=w_out, ln_ffn2=ln_ffn2, w_ffn2_gu=w_ffn2_gu, w_ffn2_down=w_ffn2_down)
    mom = dict(ln_ffn1=m_ln_ffn1, w_ffn1_gu=m_w_ffn1_gu, w_ffn1_down=m_w_ffn1_down, ln_mix=m_ln_mix, w_in=m_w_in,
               pool_w=m_pool_w, pool_scale=m_pool_scale, w_pool_branch=m_w_pool_branch, q_norm=m_q_norm, k_norm=m_k_norm,
               sinks=m_sinks, w_attn_branch=m_w_attn_branch, w_out=m_w_out, ln_ffn2=m_ln_ffn2, w_ffn2_gu=m_w_ffn2_gu,
               w_ffn2_down=m_w_ffn2_down)
    var = dict(ln_ffn1=v_ln_ffn1, w_ffn1_gu=v_w_ffn1_gu, w_ffn1_down=v_w_ffn1_down, ln_mix=v_ln_mix, w_in=v_w_in,
               pool_w=v_pool_w, pool_scale=v_pool_scale, w_pool_branch=v_w_pool_branch, q_norm=v_q_norm, k_norm=v_k_norm,
               sinks=v_sinks, w_attn_branch=v_w_attn_branch, w_out=v_w_out, ln_ffn2=v_ln_ffn2, w_ffn2_gu=v_w_ffn2_gu,
               w_ffn2_down=v_w_ffn2_down)
    L = ln_ffn1.shape[0]

    def small(l):
        return dict(ln_ffn1=ln_ffn1[l], ln_mix=ln_mix[l], ln_ffn2=ln_ffn2[l], pool_w=pool_w[l].astype(CDT),
                    pool_scale=pool_scale[l], sinks=sinks[l],
                    gqk=jnp.concatenate([jnp.tile(q_norm[l], N_Q_HEADS), jnp.tile(k_norm[l], KV_DIM // HEAD_DIM)]).reshape(1, QK_DIM))

    place = jnp.stack([lax.axis_index("c"), 2 * lax.axis_index("x") + lax.axis_index("y")]).astype(jnp.int32)
    hooks = _Exchange({n: w[n].astype(CDT) for n in BIG}, small, place, L)
    loss_part, grad_x = _local_step(x[0], loss_target[0], L, hooks)
    g_big = hooks.reduced()
    g_small, loss = hooks.reduced_small(loss_part)
    g_small = {n: v.reshape(w[n].shape) for n, v in g_small.items()}

    grad_out, delta, new_m, new_v = {}, {}, {}, {}
    for n in BIG:
        shape = w[n].shape
        flat = (shape[0] * shape[1], shape[2])
        go, d, nm, nv = _adamw(w[n].reshape(flat), g_big[n].reshape(flat), mom[n].reshape(flat), var[n].reshape(flat), "adamw")
        grad_out[n], delta[n], new_m[n], new_v[n] = go.reshape(shape), d.reshape(shape), nm.reshape(shape), nv.reshape(shape)
    pw, _ = _pack_small([w[n] for n in SMALL])
    pg, sp = _pack_small([g_small[n] for n in SMALL])
    pm_, _ = _pack_small([mom[n] for n in SMALL])
    pv, _ = _pack_small([var[n] for n in SMALL])
    _, d, nm, nv = _adamw(pw, pg, pm_, pv, "adamw_small")
    shapes = [w[n].shape for n in SMALL]
    for n, dv, mv, vv in zip(SMALL, _unpack_small(d, sp, shapes), _unpack_small(nm, sp, shapes), _unpack_small(nv, sp, shapes)):
        grad_out[n], delta[n], new_m[n], new_v[n] = g_small[n], dv, mv, vv

    return (loss, grad_x[None], *[grad_out[n] for n in WEIGHTS], *[delta[n] for n in WEIGHTS],
            *[new_m[n] for n in WEIGHTS], *[new_v[n] for n in WEIGHTS])
```

```python
import functools
import math

import jax
import jax.numpy as jnp
from jax import lax
from jax.experimental import pallas as pl
from jax.experimental.pallas import tpu as pltpu

F32 = jnp.float32
CDT = jnp.bfloat16
WIRE_DT = jnp.bfloat16

D_MODEL = 1024
POOL_WINDOWS = (2, 4, 8, 16)
POOL_WMAX = 16
GROUP = 128
POOL_DIM = 512
HEAD_DIM = 64
N_Q_HEADS = 8
ATTN_DIM = 512
KV_DIM = 128
QK_DIM = ATTN_DIM + KV_DIM
GATE_DIM = 2 * D_MODEL
BLOCK = 128
ROPE_THETA = 500000.0
ROT_DIM = 16
EPS = 1e-6
ATTN_SCALE = HEAD_DIM ** -0.5

ADAM_LR = 0.001
ADAM_B1 = 0.9
ADAM_B2 = 0.999
ADAM_EPS = 1e-08
ADAM_WD = 0.01
ADAM_STEP = 10

N_CHIPS = 4
N_DEV = 8
LANES = 128
VMEM_LIMIT_BYTES = 48 * 1024 * 1024

MESH = pl.DeviceIdType.MESH
ANY = pl.BlockSpec(memory_space=pl.ANY)

BIG = ("w_ffn1_gu", "w_ffn1_down", "w_in", "w_pool_branch", "w_attn_branch", "w_out", "w_ffn2_gu", "w_ffn2_down")
COL_SHARDED = ("w_ffn1_gu", "w_in", "w_pool_branch", "w_attn_branch", "w_ffn2_gu")
USED_AS_BLOCKS = ("w_pool_branch", "w_attn_branch")
WIDE = ("w_ffn1_gu", "w_ffn2_gu")
SMALL = ("ln_ffn1", "ln_mix", "pool_w", "pool_scale", "q_norm", "k_norm", "sinks", "ln_ffn2")
WEIGHTS = ("ln_ffn1", "w_ffn1_gu", "w_ffn1_down", "ln_mix", "w_in", "pool_w", "pool_scale", "w_pool_branch",
           "q_norm", "k_norm", "sinks", "w_attn_branch", "w_out", "ln_ffn2", "w_ffn2_gu", "w_ffn2_down")


def _tile(n, target, mult=8):
    if n <= target:
        return n
    for t in range(target - target % mult, 0, -mult):
        if n % t == 0:
            return t
    raise ValueError((n, target, mult))


def _params(*sem):
    return pltpu.CompilerParams(dimension_semantics=sem, vmem_limit_bytes=VMEM_LIMIT_BYTES)


def _sigmoid(v):
    return 0.5 * jnp.tanh(0.5 * v) + 0.5


def _dot(a, b):
    return jnp.dot(a, b, preferred_element_type=F32)


def _dot_nt(a, b):
    return lax.dot_general(a, b, (((1,), (1,)), ((), ())), preferred_element_type=F32)


def _dot_tn(a, b):
    return lax.dot_general(a, b, (((0,), (0,)), ((), ())), preferred_element_type=F32)


class _Side:
    def __init__(self, ins, out_shapes, n_sems, issue, aliases=None):
        self.ins, self.out_shapes, self.n_sems, self.issue = list(ins), list(out_shapes), n_sems, issue
        self.aliases = dict(aliases or {})
        self.outs = None


def _pcall(body, name, grid, in_specs, out_specs, out_shape, args, dims, side=None, scratch=()):
    scratch = list(scratch)
    if side is None:
        return pl.pallas_call(body, name=name, grid=grid, in_specs=in_specs, out_specs=out_specs, out_shape=out_shape,
                              scratch_shapes=scratch, compiler_params=_params(*dims))(*args)
    n_in, n_out, s_in, s_out = len(in_specs), len(out_specs), len(side.ins), len(side.out_shapes)

    def wrapped(*refs):
        main_in, side_in = refs[:n_in], refs[n_in:n_in + s_in]
        main_out = refs[n_in + s_in:n_in + s_in + n_out]
        side_out = refs[n_in + s_in + n_out:n_in + s_in + n_out + s_out]
        rest = refs[n_in + s_in + n_out + s_out:]
        main_scratch, (ssem, rsem) = rest[:len(scratch)], rest[len(scratch):]
        ids = [pl.program_id(ax) for ax in range(len(grid))]
        first = functools.reduce(jnp.logical_and, [i == 0 for i in ids])
        last = functools.reduce(jnp.logical_and, [i == g - 1 for i, g in zip(ids, grid)])

        @pl.when(first)
        def _():
            for cp in side.issue(side_in, side_out, ssem, rsem):
                cp.start()

        body(*main_in, *main_out, *main_scratch)

        @pl.when(last)
        def _():
            cps = side.issue(side_in, side_out, ssem, rsem)
            for cp in cps:
                cp.wait_recv()
            for cp in cps:
                cp.wait_send()

    outs = pl.pallas_call(
        wrapped, name=name, grid=grid, in_specs=list(in_specs) + [ANY] * s_in, out_specs=list(out_specs) + [ANY] * s_out,
        out_shape=list(out_shape) + side.out_shapes,
        input_output_aliases={n_in + i: n_out + o for i, o in side.aliases.items()},
        scratch_shapes=scratch + [pltpu.SemaphoreType.DMA((side.n_sems,))] * 2,
        compiler_params=_params(*["arbitrary"] * len(grid)),
    )(*args, *side.ins)
    side.outs = list(outs[n_out:])
    return list(outs[:n_out])


def _run_side(side, name):
    s_in = len(side.ins)

    def body(*refs):
        ssem, rsem = refs[s_in + len(side.out_shapes):]
        cps = side.issue(refs[:s_in], refs[s_in:s_in + len(side.out_shapes)], ssem, rsem)
        for cp in cps:
            cp.start()
        for cp in cps:
            cp.wait_recv()
        for cp in cps:
            cp.wait_send()

    side.outs = list(pl.pallas_call(
        body, name=name, in_specs=[ANY] * s_in, out_specs=[ANY] * len(side.out_shapes), out_shape=side.out_shapes,
        input_output_aliases=side.aliases, scratch_shapes=[pltpu.SemaphoreType.DMA((side.n_sems,))] * 2,
    )(*side.ins))
    return side.outs


def _loss_head(y, tgt, name):
    T, Dm = y.shape
    tm = _tile(T, 512)

    def body(y_ref, t_ref, dy_ref, loss_ref):
        @pl.when(pl.program_id(0) == 0)
        def _():
            loss_ref[...] = jnp.zeros_like(loss_ref)

        diff = y_ref[...] - t_ref[...]
        dy_ref[...] = diff * (1.0 / Dm)
        part = jnp.sum(jnp.mean(diff * diff, axis=-1, keepdims=True), axis=0, keepdims=True)
        loss_ref[...] += 0.5 * part

    row = pl.BlockSpec((tm, Dm), lambda i: (i, 0))
    one = pl.BlockSpec((1, 1), lambda i: (0, 0))
    return pl.pallas_call(
        body, name=name, grid=(T // tm,),
        in_specs=[row, row], out_specs=[row, one],
        out_shape=[jax.ShapeDtypeStruct((T, Dm), F32), jax.ShapeDtypeStruct((1, 1), F32)],
        compiler_params=_params("arbitrary"),
    )(y, tgt)


def _mm_nn(a, b, name, out_dtype, res=None, scale=1.0, tm_target=512, side=None):
    M, K = a.shape
    N = b.shape[1]
    tm = _tile(M, tm_target)

    def body(a_ref, b_ref, *rest):
        acc = _dot(a_ref[...].astype(CDT), b_ref[...])
        if res is None:
            (o_ref,) = rest
        else:
            r_ref, o_ref = rest
            acc = r_ref[...] + scale * acc
        o_ref[...] = acc.astype(o_ref.dtype)

    in_specs = [pl.BlockSpec((tm, K), lambda i: (i, 0)), pl.BlockSpec((K, N), lambda i: (0, 0))]
    args = [a, b]
    if res is not None:
        in_specs.append(pl.BlockSpec((tm, N), lambda i: (i, 0)))
        args.append(res)
    return _pcall(body, name, (M // tm,), in_specs, [pl.BlockSpec((tm, N), lambda i: (i, 0))],
                  [jax.ShapeDtypeStruct((M, N), out_dtype)], args, ("parallel",), side)[0]


def _mm_tn(x, dy, name, scale=1.0, col_blocks=1, tn_target=1664, tm_target=1408, tk_target=1024, side=None):
    T, M = x.shape
    split = dy.ndim == 3
    Nh = dy.shape[-1]
    N = 2 * Nh if split else Nh
    nb = N // col_blocks
    whole = col_blocks > 1 and not split and N <= tn_target
    tm = _tile(M, tm_target, LANES)
    tn = N if whole else _tile(math.gcd(Nh, nb), tn_target, LANES)
    tk = _tile(T, tk_target)
    nk = T // tk
    njh, njb = Nh // tn, max(nb // tn, 1)

    def body(x_ref, dy_ref, o_ref, acc_ref):
        k = pl.program_id(2)

        @pl.when(k == 0)
        def _():
            acc_ref[...] = jnp.zeros_like(acc_ref)

        acc_ref[...] += _dot_tn(x_ref[...].astype(CDT), dy_ref[...].astype(CDT))

        @pl.when(k == nk - 1)
        def _():
            res = (acc_ref[...] if scale == 1.0 else scale * acc_ref[...]).astype(o_ref.dtype)
            if whole:
                for b in range(col_blocks):
                    o_ref[b] = res[:, b * nb:(b + 1) * nb]
            else:
                o_ref[...] = res

    if split:
        dy_spec = pl.BlockSpec((None, tk, tn), lambda i, j, k: (j // njh, k, j % njh))
    else:
        dy_spec = pl.BlockSpec((tk, tn), lambda i, j, k: (k, j))
    if col_blocks == 1:
        out_spec, out_dims = pl.BlockSpec((tm, tn), lambda i, j, k: (i, j)), (M, N)
    elif whole:
        out_spec, out_dims = pl.BlockSpec((col_blocks, tm, nb), lambda i, j, k: (0, i, 0)), (col_blocks, M, nb)
    else:
        out_spec, out_dims = pl.BlockSpec((None, tm, tn), lambda i, j, k: (j // njb, i, j % njb)), (col_blocks, M, nb)
    return _pcall(body, name, (M // tm, N // tn, nk), [pl.BlockSpec((tk, tm), lambda i, j, k: (k, i)), dy_spec],
                  [out_spec], [jax.ShapeDtypeStruct(out_dims, WIRE_DT)], (x, dy), ("parallel", "parallel", "arbitrary"),
                  side, [pltpu.VMEM((tm, tn), F32)])[0]


def _mm_tn_parts(x, parts, name):
    T, M = x.shape
    widths = [p.shape[1] for p in parts]
    N = sum(widths)
    tk = _tile(T, 512)

    def body(x_ref, *refs):
        o_ref = refs[-1]

        @pl.when(pl.program_id(0) == 0)
        def _():
            o_ref[...] = jnp.zeros_like(o_ref)

        parts_cat = jnp.concatenate([p_ref[...].astype(CDT) for p_ref in refs[:-1]], axis=1)
        o_ref[...] += _dot_tn(x_ref[...].astype(CDT), parts_cat)

    return pl.pallas_call(
        body, name=name, grid=(T // tk,),
        in_specs=[pl.BlockSpec((tk, M), lambda k: (k, 0))] + [pl.BlockSpec((tk, wd), lambda k: (k, 0)) for wd in widths],
        out_specs=pl.BlockSpec((M, N), lambda k: (0, 0)),
        out_shape=jax.ShapeDtypeStruct((M, N), F32), compiler_params=_params("arbitrary"),
    )(x, *parts)


def _rmsnorm_rows(x_ref, g_ref):
    xv = x_ref[...]
    r = lax.rsqrt(jnp.mean(xv * xv, axis=-1, keepdims=True) + EPS)
    return (xv * r * g_ref[...]).astype(CDT)


def _ffn_up(x, ln, wgu, name, side=None):
    T, Dm = x.shape
    Fd = wgu.shape[1] // 2
    tm = _tile(T, 256)

    def body(x_ref, ln_ref, wg_ref, wu_ref, h_ref, gu_ref, a_ref):
        hv = _rmsnorm_rows(x_ref, ln_ref)
        h_ref[...] = hv
        g = _dot(hv, wg_ref[...])
        u = _dot(hv, wu_ref[...])
        sg = _sigmoid(g)
        silu = g * sg
        a_ref[...] = (silu * u).astype(a_ref.dtype)
        gu_ref[0] = (0.5 * u * (sg * (1.0 + g * (1.0 - sg)))).astype(gu_ref.dtype)
        gu_ref[1] = (0.5 * silu).astype(gu_ref.dtype)

    row = pl.BlockSpec((tm, Dm), lambda i: (i, 0))
    return _pcall(
        body, name, (T // tm,),
        [row, pl.BlockSpec((1, Dm), lambda i: (0, 0)),
         pl.BlockSpec((Dm, Fd), lambda i: (0, 0), pipeline_mode=pl.Buffered(1)),
         pl.BlockSpec((Dm, Fd), lambda i: (0, 1), pipeline_mode=pl.Buffered(1))],
        [row, pl.BlockSpec((2, tm, Fd), lambda i: (0, i, 0)), pl.BlockSpec((tm, Fd), lambda i: (i, 0))],
        [jax.ShapeDtypeStruct((T, Dm), CDT), jax.ShapeDtypeStruct((2, T, Fd), CDT), jax.ShapeDtypeStruct((T, Fd), CDT)],
        (x, ln.reshape(1, Dm), wgu, wgu), ("parallel",), side)


def _ffn_down_bwd(dxo, wd, gu, name, side=None):
    T, Dm = dxo.shape
    Fd = wd.shape[0]
    tm = _tile(T, 256)

    def body(dx_ref, wd_ref, gu_ref, dgu_ref):
        da = _dot_nt(dx_ref[...].astype(CDT), wd_ref[...])
        dgu_ref[0] = (da * gu_ref[0].astype(F32)).astype(dgu_ref.dtype)
        dgu_ref[1] = (da * gu_ref[1].astype(F32)).astype(dgu_ref.dtype)

    gu_spec = pl.BlockSpec((2, tm, Fd), lambda i: (0, i, 0))
    return _pcall(
        body, name, (T // tm,),
        [pl.BlockSpec((tm, Dm), lambda i: (i, 0)),
         pl.BlockSpec((Fd, Dm), lambda i: (0, 0), pipeline_mode=pl.Buffered(1)), gu_spec],
        [gu_spec], [jax.ShapeDtypeStruct((2, T, Fd), CDT)],
        (dxo, wd, gu), ("parallel",), side)[0]


def _mm_nt_norm_bwd(a_parts, b, x, g, dres, name, side=None):
    T, Dm = x.shape
    tm = _tile(T, 256)

    def b_cols(b_ref, lo, wd):
        if b.ndim == 2:
            return [(0, wd, b_ref[:, lo:lo + wd])]
        kb = b.shape[2]
        return [(j * kb - lo, kb, b_ref[j]) for j in range(lo // kb, (lo + wd) // kb)]

    def body(*refs):
        a_refs, (b_ref, x_ref, g_ref, dres_ref, dx_ref, dg_ref) = refs[:len(a_parts)], refs[len(a_parts):]

        @pl.when(pl.program_id(0) == 0)
        def _():
            dg_ref[...] = jnp.zeros_like(dg_ref)

        dh, lo = None, 0
        if b.ndim == 2 and len(a_parts) > 1 and all(p.ndim == 2 for p in a_parts):
            dh = _dot_nt(jnp.concatenate([a_ref[...].astype(CDT) for a_ref in a_refs], axis=1), b_ref[...])
            a_refs = ()
        for a_ref, part in zip(a_refs, a_parts):
            slabs = [a_ref] if part.ndim == 2 else [a_ref.at[s_] for s_ in range(part.shape[0])]
            for slab in slabs:
                for off, wd, bv in b_cols(b_ref, lo, part.shape[-1]):
                    term = _dot_nt(slab[:, off:off + wd].astype(CDT), bv)
                    dh = term if dh is None else dh + term
                lo += part.shape[-1]
        xv = x_ref[...]
        r = lax.rsqrt(jnp.mean(xv * xv, axis=-1, keepdims=True) + EPS)
        xh = xv * r
        dg_ref[...] += jnp.sum(dh * xh, axis=0, keepdims=True)
        dxh = dh * g_ref[...]
        dx_ref[...] = dres_ref[...] + r * (dxh - xh * jnp.mean(dxh * xh, axis=-1, keepdims=True))

    row = pl.BlockSpec((tm, Dm), lambda i: (i, 0))
    vec = pl.BlockSpec((1, Dm), lambda i: (0, 0))
    a_specs = [pl.BlockSpec((tm, p.shape[1]), lambda i: (i, 0)) if p.ndim == 2 else
               pl.BlockSpec((p.shape[0], tm, p.shape[2]), lambda i: (0, i, 0)) for p in a_parts]
    b_spec = pl.BlockSpec(b.shape, lambda i: (0,) * b.ndim, pipeline_mode=pl.Buffered(1))
    return _pcall(body, name, (T // tm,), a_specs + [b_spec, row, vec, row], [row, vec],
                  [jax.ShapeDtypeStruct((T, Dm), F32), jax.ShapeDtypeStruct((1, Dm), F32)],
                  (*a_parts, b, x, g.reshape(1, Dm), dres), ("arbitrary",), side)


def _mm_in(x, ln, w_in, gqk, cos_t, sin_t, pool_w, pool_scale, name, side=None):
    T, Dm = x.shape
    tm = _tile(T, 256, POOL_WMAX)
    widths = (POOL_DIM, QK_DIM, KV_DIM, GATE_DIM)

    def body(x_ref, ln_ref, w_ref, g_ref, c_ref, s_ref, pw_ref, sc_ref,
             h_ref, zu_ref, zqk_ref, zv_ref, zg_ref, qkn_ref, pm_ref, halo_ref):
        i = pl.program_id(0)
        hv = _rmsnorm_rows(x_ref, ln_ref)
        h_ref[...] = hv
        z = _dot(hv, w_ref[...])
        lo = 0
        for o_ref, wd in zip((zu_ref, zqk_ref, zv_ref, zg_ref), widths):
            o_ref[...] = z[:, lo:lo + wd]
            lo += wd
        first, low = _lane_masks()
        cosv, sinv = c_ref[...], s_ref[...]
        for c in range(QK_DIM // LANES):
            sl = slice(c * LANES, (c + 1) * LANES)
            xv = z[:, POOL_DIM + c * LANES:POOL_DIM + (c + 1) * LANES]
            r = lax.rsqrt(_head_mean(xv * xv, first) + EPS)
            xn = xv * r * g_ref[:, sl]
            qkn_ref[:, sl] = (xn * cosv + _rope_partner(xn, low) * sinv).astype(qkn_ref.dtype)
        @pl.when(i == 0)
        def _():
            halo_ref[...] = jnp.zeros_like(halo_ref)

        zu = z[:, :POOL_DIM]
        ext = jnp.concatenate([halo_ref[...], zu], axis=0)
        halo_ref[...] = zu[tm - POOL_WMAX:, :]
        pos = i * tm + lax.broadcasted_iota(jnp.int32, (tm, 1), 0)
        ys = [_dot(_window_mean_minus_token(ext, zu, g, w, pos).astype(CDT), pw_ref[g])
              for g, w in enumerate(POOL_WINDOWS)]
        pm_ref[...] = (jnp.concatenate(ys, axis=1) * sc_ref[...]).astype(pm_ref.dtype)

    row = pl.BlockSpec((tm, Dm), lambda i: (i, 0))
    tab = pl.BlockSpec((tm, LANES), lambda i: (i, 0))
    return _pcall(body, name, (T // tm,),
                  [row, pl.BlockSpec((1, Dm), lambda i: (0, 0)),
                   pl.BlockSpec(w_in.shape, lambda i: (0, 0), pipeline_mode=pl.Buffered(1)),
                   pl.BlockSpec((1, QK_DIM), lambda i: (0, 0)), tab, tab,
                   pl.BlockSpec(pool_w.shape, lambda i: (0, 0, 0)), pl.BlockSpec((1, POOL_DIM), lambda i: (0, 0))],
                  [row] + [pl.BlockSpec((tm, wd), lambda i: (i, 0)) for wd in widths + (QK_DIM, POOL_DIM)],
                  [jax.ShapeDtypeStruct((T, Dm), CDT)] + [jax.ShapeDtypeStruct((T, wd), F32) for wd in widths]
                  + [jax.ShapeDtypeStruct((T, QK_DIM), CDT), jax.ShapeDtypeStruct((T, POOL_DIM), CDT)],
                  (x, ln.reshape(1, Dm), w_in, gqk, cos_t, sin_t, pool_w, pool_scale.reshape(1, POOL_DIM)),
                  ("arbitrary",), side, [pltpu.VMEM((POOL_WMAX, POOL_DIM), F32)])


def _window_mean_minus_token(ext, u, g, w, pos):
    sl = slice(g * GROUP, (g + 1) * GROUP)
    s = ext[:, sl]
    span = 1
    while span < w:
        s = s + pltpu.roll(s, span, axis=0)
        span *= 2
    cnt = jnp.minimum(pos + 1, w).astype(F32)
    return s[POOL_WMAX:, :] / cnt - u[:, sl]


def _pool_bwd(zu, dpm, pool_w, scale, name):
    T = zu.shape[0]
    tm = _tile(T, 512, POOL_WMAX)
    hb = tm // POOL_WMAX
    nsteps = T // tm
    ext_rows = tm + POOL_WMAX

    def body(u_ref, halo_ref, dpm_ref, dnext_ref, pw_ref, sc_ref, du_ref, dpw_ref, dsc_ref):
        i = pl.program_id(0)

        @pl.when(i == 0)
        def _():
            dpw_ref[...] = jnp.zeros_like(dpw_ref)
            dsc_ref[...] = jnp.zeros_like(dsc_ref)

        u = u_ref[...]
        halo = jnp.where(i > 0, halo_ref[...], 0.0)
        ext = jnp.concatenate([halo, u], axis=0)
        dpm_t = dpm_ref[...].astype(F32)
        dnext = jnp.where(i < nsteps - 1, dnext_ref[...].astype(F32), 0.0)
        dext = jnp.concatenate([dpm_t, dnext], axis=0)
        sc = sc_ref[...]
        pos = i * tm + lax.broadcasted_iota(jnp.int32, (tm, 1), 0)
        pos_ext = i * tm + lax.broadcasted_iota(jnp.int32, (ext_rows, 1), 0)
        dus, dscs = [], []
        for g, w in enumerate(POOL_WINDOWS):
            sl = slice(g * GROUP, (g + 1) * GROUP)
            dc = _window_mean_minus_token(ext, u, g, w, pos).astype(CDT)
            y = _dot(dc, pw_ref[g])
            dscs.append(jnp.sum(dpm_t[:, sl] * y, axis=0, keepdims=True))
            dy_ext = (dext[:, sl] * sc[:, sl]).astype(CDT)
            dpw_ref[g] += _dot_tn(dc, dy_ext[:tm])
            dd = _dot_nt(dy_ext, pw_ref[g])
            r = dd / jnp.minimum(pos_ext + 1, w).astype(F32)
            span = 1
            while span < w:
                r = r + pltpu.roll(r, ext_rows - span, axis=0)
                span *= 2
            dus.append(r[:tm] - dd[:tm])
        du_ref[...] = jnp.concatenate(dus, axis=1).astype(du_ref.dtype)
        dsc_ref[...] += jnp.concatenate(dscs, axis=1)

    row = pl.BlockSpec((tm, POOL_DIM), lambda i: (i, 0))
    prev = pl.BlockSpec((POOL_WMAX, POOL_DIM), lambda i: (jnp.maximum(i * hb - 1, 0), 0))
    nxt = pl.BlockSpec((POOL_WMAX, POOL_DIM), lambda i: (jnp.minimum((i + 1) * hb, nsteps * hb - 1), 0))
    return pl.pallas_call(
        body, name=name, grid=(nsteps,),
        in_specs=[row, prev, row, nxt, pl.BlockSpec(pool_w.shape, lambda i: (0, 0, 0)),
                  pl.BlockSpec((1, POOL_DIM), lambda i: (0, 0))],
        out_specs=[row, pl.BlockSpec(pool_w.shape, lambda i: (0, 0, 0)), pl.BlockSpec((1, POOL_DIM), lambda i: (0, 0))],
        out_shape=[jax.ShapeDtypeStruct((T, POOL_DIM), CDT), jax.ShapeDtypeStruct(pool_w.shape, F32),
                   jax.ShapeDtypeStruct((1, POOL_DIM), F32)],
        compiler_params=_params("arbitrary"),
    )(zu, zu, dpm, dpm, pool_w, scale.reshape(1, POOL_DIM))


def _rope_tables(T):
    pos = jnp.arange(T, dtype=F32)
    inv_freq = ROPE_THETA ** (-jnp.arange(0, ROT_DIM, 2, dtype=F32) / ROT_DIM)
    ang = pos[:, None] * inv_freq[None, :]
    cos, sin = jnp.cos(ang), jnp.sin(ang)
    rest = HEAD_DIM - ROT_DIM
    cos_h = jnp.concatenate([cos, cos, jnp.ones((T, rest), F32)], axis=1)
    sin_h = jnp.concatenate([-sin, sin, jnp.zeros((T, rest), F32)], axis=1)
    return jnp.tile(cos_h, (1, 2)), jnp.tile(sin_h, (1, 2))


def _lane_masks():
    lane = lax.broadcasted_iota(jnp.int32, (1, LANES), 1)
    in_head = lane % HEAD_DIM
    return lane < HEAD_DIM, in_head < ROT_DIM // 2


def _rope_partner(v, low):
    lane = lax.broadcasted_iota(jnp.int32, (1, LANES), 1)
    swapped = jnp.where(low, pltpu.roll(v, LANES - ROT_DIM // 2, axis=1), pltpu.roll(v, ROT_DIM // 2, axis=1))
    return jnp.where(lane % HEAD_DIM < ROT_DIM, swapped, 0.0)


def _head_mean(v, first):
    lo = jnp.sum(jnp.where(first, v, 0.0), axis=-1, keepdims=True)
    hi = jnp.sum(jnp.where(first, 0.0, v), axis=-1, keepdims=True)
    return jnp.where(first, lo, hi) * (1.0 / HEAD_DIM)


def _qk_bwd(dqk, zqk, gqk, cos_t, sin_t, name):
    T = zqk.shape[0]
    tm = _tile(T, 512)

    def body(d_ref, z_ref, g_ref, c_ref, s_ref, dz_ref, dg_ref):
        @pl.when(pl.program_id(0) == 0)
        def _():
            dg_ref[...] = jnp.zeros_like(dg_ref)

        first, low = _lane_masks()
        cosv, sinv = c_ref[...], s_ref[...]
        dgs = []
        for c in range(QK_DIM // LANES):
            sl = slice(c * LANES, (c + 1) * LANES)
            dout = d_ref[:, sl]
            dxn = dout * cosv + _rope_partner(dout * sinv, low)
            xv = z_ref[:, sl]
            r = lax.rsqrt(_head_mean(xv * xv, first) + EPS)
            xh = xv * r
            dgs.append(jnp.sum(dxn * xh, axis=0, keepdims=True))
            dxh = dxn * g_ref[:, sl]
            dz_ref[:, sl] = (r * (dxh - xh * _head_mean(dxh * xh, first))).astype(dz_ref.dtype)
        dg_ref[...] += jnp.concatenate(dgs, axis=1)

    row = pl.BlockSpec((tm, QK_DIM), lambda i: (i, 0))
    tab = pl.BlockSpec((tm, LANES), lambda i: (i, 0))
    vec = pl.BlockSpec((1, QK_DIM), lambda i: (0, 0))
    return pl.pallas_call(
        body, name=name, grid=(T // tm,),
        in_specs=[row, row, vec, tab, tab], out_specs=[row, vec],
        out_shape=[jax.ShapeDtypeStruct((T, QK_DIM), CDT), jax.ShapeDtypeStruct((1, QK_DIM), F32)],
        compiler_params=_params("arbitrary"),
    )(dqk, zqk, gqk, cos_t, sin_t)


def _dup_half(v, first, kv):
    swapped = pltpu.roll(v, HEAD_DIM, axis=1)
    return jnp.where(first, v, swapped) if kv == 0 else jnp.where(first, swapped, v)


HEADS_PER_KV = 4
HEAD_STACK_FWD = 1
HEAD_STACK_BWD = 2


def _attn_bias(stack):
    qi = lax.broadcasted_iota(jnp.int32, (stack * BLOCK, 2 * BLOCK), 0) % BLOCK
    ki = lax.broadcasted_iota(jnp.int32, (stack * BLOCK, 2 * BLOCK), 1)
    diff = qi + BLOCK - ki
    band = (diff >= 0) & (diff < BLOCK)
    return jnp.stack([jnp.where(band, 0.0, -jnp.inf), jnp.where(band & (ki >= BLOCK), 0.0, -jnp.inf)]).astype(F32)


def _attn_blocks(T):
    return _tile(T // BLOCK, 4, 1)


def _stack_heads(ref, rows, kv, heads, first):
    parts = []
    for h in heads:
        c = 2 * kv + h // 2
        v = ref[rows, c * LANES:(c + 1) * LANES].astype(CDT)
        zero = jnp.zeros_like(v)
        parts.append(jnp.where(first, v, zero) if h % 2 == 0 else jnp.where(first, zero, v))
    return parts[0] if len(parts) == 1 else jnp.concatenate(parts, axis=0)


def _row_blocks(v, n):
    return [v[b * BLOCK:(b + 1) * BLOCK] for b in range(n)]


def _sink_column(sink_ref, kv, heads):
    cols = [jnp.full((BLOCK, 1), sink_ref[HEADS_PER_KV * kv + h], F32) for h in heads]
    return cols[0] if len(cols) == 1 else jnp.concatenate(cols, axis=0)


def _head_groups(stack):
    return [tuple(range(g, g + stack)) for g in range(0, HEADS_PER_KV, stack)]


def _softmax_with_sink(qst, kdup, sinkcol, bias):
    s = _dot_nt(qst, kdup) * ATTN_SCALE + bias
    m = jnp.maximum(jnp.max(s, axis=-1, keepdims=True), sinkcol)
    pu = jnp.exp(s - m)
    denom = jnp.sum(pu, axis=-1, keepdims=True) + jnp.exp(sinkcol - m)
    return pu * (1.0 / denom), m + jnp.log(denom)


def _attn_fwd(qkn, zv, sinks, name, side=None):
    T = qkn.shape[0]
    R = _attn_blocks(T)
    tq = R * BLOCK

    def body(sink_ref, bias_ref, qk_ref, qkp_ref, v_ref, vp_ref, o_ref, lse_ref):
        i = pl.program_id(0)
        first, _ = _lane_masks()
        lane = lax.broadcasted_iota(jnp.int32, (1, LANES), 1)
        kall = jnp.concatenate([qkp_ref[:, ATTN_DIM:], qk_ref[:, ATTN_DIM:]], axis=0)
        vall = jnp.concatenate([vp_ref[...], v_ref[...]], axis=0).astype(CDT)
        for r in range(R):
            bias = bias_ref[jnp.where(i == 0, 1, 0)] if r == 0 else bias_ref[0]
            rows = slice(r * BLOCK, (r + 2) * BLOCK)
            qrows = slice(r * BLOCK, (r + 1) * BLOCK)
            lse_rows = jnp.zeros((BLOCK, LANES), F32)
            for kv in range(2):
                kdup = _dup_half(kall[rows], first, kv)
                vdup = _dup_half(vall[rows], first, kv)
                res = []
                for heads in _head_groups(HEAD_STACK_FWD):
                    p, lse = _softmax_with_sink(_stack_heads(qk_ref, qrows, kv, heads, first), kdup,
                                                _sink_column(sink_ref, kv, heads), bias)
                    res += _row_blocks(_dot(p.astype(CDT), vdup), len(heads))
                    for b, col in enumerate(_row_blocks(lse, len(heads))):
                        lse_rows = jnp.where(lane == HEADS_PER_KV * kv + heads[b], col, lse_rows)
                o_ref[qrows, 2 * kv * LANES:(2 * kv + 1) * LANES] = jnp.where(first, res[0], res[1]).astype(o_ref.dtype)
                o_ref[qrows, (2 * kv + 1) * LANES:(2 * kv + 2) * LANES] = jnp.where(first, res[2], res[3]).astype(o_ref.dtype)
            lse_ref[qrows, :] = lse_rows

    bias = _attn_bias(HEAD_STACK_FWD)
    prev = lambda i: (jnp.maximum(i * R - 1, 0), 0)
    return _pcall(
        body, name, (T // tq,),
        [pl.BlockSpec(memory_space=pltpu.SMEM), pl.BlockSpec(bias.shape, lambda i: (0, 0, 0)),
         pl.BlockSpec((tq, QK_DIM), lambda i: (i, 0)), pl.BlockSpec((BLOCK, QK_DIM), prev),
         pl.BlockSpec((tq, KV_DIM), lambda i: (i, 0)), pl.BlockSpec((BLOCK, KV_DIM), prev)],
        [pl.BlockSpec((tq, ATTN_DIM), lambda i: (i, 0)), pl.BlockSpec((tq, LANES), lambda i: (i, 0))],
        [jax.ShapeDtypeStruct((T, ATTN_DIM), CDT), jax.ShapeDtypeStruct((T, LANES), F32)],
        (sinks, bias, qkn, qkn, zv, zv), ("parallel",), side)


def _attn_bwd(qkn, zv, sinks, do, o, lse, name, side=None):
    T = qkn.shape[0]
    R = _attn_blocks(T)
    tq = R * BLOCK

    def body(sink_ref, bias_ref, qk_ref, qkp_ref, v_ref, vp_ref, do_ref, o_ref, lse_ref,
             dq_ref, dkc_ref, dkp_ref, dvc_ref, dvp_ref, ds_ref):
        i = pl.program_id(0)

        @pl.when(i == 0)
        def _():
            ds_ref[...] = jnp.zeros_like(ds_ref)

        first, _ = _lane_masks()
        lane = lax.broadcasted_iota(jnp.int32, (1, LANES), 1)
        kall = jnp.concatenate([qkp_ref[:, ATTN_DIM:], qk_ref[:, ATTN_DIM:]], axis=0)
        vall = jnp.concatenate([vp_ref[...], v_ref[...]], axis=0).astype(CDT)
        for r in range(R):
            bias = bias_ref[jnp.where(i == 0, 1, 0)] if r == 0 else bias_ref[0]
            rows = slice(r * BLOCK, (r + 2) * BLOCK)
            qrows = slice(r * BLOCK, (r + 1) * BLOCK)
            dk_out, dv_out = [], []
            lse_rows = lse_ref[qrows, :]
            for kv in range(2):
                kdup = _dup_half(kall[rows], first, kv)
                vdup = _dup_half(vall[rows], first, kv)
                dq_h = []
                dk_acc = jnp.zeros((2 * BLOCK, LANES), F32)
                dv_acc = jnp.zeros((2 * BLOCK, LANES), F32)
                for heads in _head_groups(HEAD_STACK_BWD):
                    qst = _stack_heads(qk_ref, qrows, kv, heads, first)
                    dost = _stack_heads(do_ref, qrows, kv, heads, first)
                    lse_cols, delta_cols = [], []
                    for h in heads:
                        cols = slice((2 * kv + h // 2) * LANES, (2 * kv + h // 2 + 1) * LANES)
                        prod = do_ref[qrows, cols].astype(F32) * o_ref[qrows, cols].astype(F32)
                        own = first if h % 2 == 0 else jnp.logical_not(first)
                        delta_cols.append(jnp.sum(jnp.where(own, prod, 0.0), axis=-1, keepdims=True))
                        lse_cols.append(jnp.sum(jnp.where(lane == HEADS_PER_KV * kv + h, lse_rows, 0.0), axis=-1, keepdims=True))
                    lse_col = lse_cols[0] if len(heads) == 1 else jnp.concatenate(lse_cols, axis=0)
                    delta = delta_cols[0] if len(heads) == 1 else jnp.concatenate(delta_cols, axis=0)
                    p = jnp.exp(_dot_nt(qst, kdup) * ATTN_SCALE + bias - lse_col)
                    dsc = (p * (_dot_nt(dost, vdup) - delta)).astype(CDT)
                    psink = jnp.exp(_sink_column(sink_ref, kv, heads) - lse_col)
                    for b, term in enumerate(_row_blocks(psink * delta, len(heads))):
                        row = HEADS_PER_KV * kv + heads[b]
                        ds_ref[row:row + 1, :] += jnp.sum(term, axis=0, keepdims=True)
                    dq_h += _row_blocks(_dot(dsc, kdup) * ATTN_SCALE, len(heads))
                    dk_acc = dk_acc + _dot_tn(dsc, qst) * ATTN_SCALE
                    dv_acc = dv_acc + _dot_tn(p.astype(CDT), dost)
                dq_ref[qrows, 2 * kv * LANES:(2 * kv + 1) * LANES] = jnp.where(first, dq_h[0], dq_h[1])
                dq_ref[qrows, (2 * kv + 1) * LANES:(2 * kv + 2) * LANES] = jnp.where(first, dq_h[2], dq_h[3])
                dk_out.append(dk_acc + pltpu.roll(dk_acc, HEAD_DIM, axis=1))
                dv_out.append(dv_acc + pltpu.roll(dv_acc, HEAD_DIM, axis=1))
            dk = jnp.where(first, dk_out[0], dk_out[1])
            dv = jnp.where(first, dv_out[0], dv_out[1])
            dkp_ref[qrows, :] = dk[:BLOCK]
            dkc_ref[qrows, :] = dk[BLOCK:]
            dvp_ref[qrows, :] = dv[:BLOCK]
            dvc_ref[qrows, :] = dv[BLOCK:]

    bias = _attn_bias(HEAD_STACK_BWD)
    prev = lambda i: (jnp.maximum(i * R - 1, 0), 0)
    kvrow = pl.BlockSpec((tq, KV_DIM), lambda i: (i, 0))
    qrow = pl.BlockSpec((tq, ATTN_DIM), lambda i: (i, 0))
    kv_shape = jax.ShapeDtypeStruct((T, KV_DIM), F32)
    return _pcall(
        body, name, (T // tq,),
        [pl.BlockSpec(memory_space=pltpu.SMEM), pl.BlockSpec(bias.shape, lambda i: (0, 0, 0)),
         pl.BlockSpec((tq, QK_DIM), lambda i: (i, 0)), pl.BlockSpec((BLOCK, QK_DIM), prev),
         kvrow, pl.BlockSpec((BLOCK, KV_DIM), prev), qrow, qrow, kvrow],
        [qrow, kvrow, kvrow, kvrow, kvrow, pl.BlockSpec((N_Q_HEADS, LANES), lambda i: (0, 0))],
        [jax.ShapeDtypeStruct((T, ATTN_DIM), F32), kv_shape, kv_shape, kv_shape, kv_shape,
         jax.ShapeDtypeStruct((N_Q_HEADS, LANES), F32)],
        (sinks, bias, qkn, qkn, zv, zv, do, o, lse), ("arbitrary",), side)


def _merge_fwd(pm, o, w_pb, w_ab, zg, name, side=None):
    T = pm.shape[0]
    tm = _tile(T, 512)

    def body(pm_ref, o_ref, wp_ref, wa_ref, zg_ref, m_ref, gp_ref, ga_ref, fp_ref, fa_ref):
        pmv, ov = pm_ref[...], o_ref[...]
        a = jnp.concatenate([_dot(pmv, wp_ref[j]) for j in range(N_CHIPS)], axis=1)
        b = jnp.concatenate([_dot(ov, wa_ref[j]) for j in range(N_CHIPS)], axis=1)
        gp = _sigmoid(zg_ref[:, :D_MODEL])
        ga = _sigmoid(zg_ref[:, D_MODEL:])
        ap, ba = gp * a, ga * b
        m_ref[...] = (ap + ba).astype(m_ref.dtype)
        gp_ref[...] = gp.astype(gp_ref.dtype)
        ga_ref[...] = ga.astype(ga_ref.dtype)
        fp_ref[...] = (ap * (1.0 - gp)).astype(fp_ref.dtype)
        fa_ref[...] = (ba * (1.0 - ga)).astype(fa_ref.dtype)

    half = pl.BlockSpec((tm, POOL_DIM), lambda i: (i, 0))
    full = pl.BlockSpec((tm, D_MODEL), lambda i: (i, 0))
    wspec = pl.BlockSpec(w_pb.shape, lambda i: (0, 0, 0))
    out = jax.ShapeDtypeStruct((T, D_MODEL), CDT)
    return _pcall(body, name, (T // tm,), [half, half, wspec, wspec, pl.BlockSpec((tm, GATE_DIM), lambda i: (i, 0))],
                  [full] * 5, [out] * 5, (pm, o, w_pb, w_ab, zg), ("parallel",), side)


def _merge_bwd(dxo, w_out, factors, w_pb, w_ab, name):
    T = dxo.shape[0]
    tm = _tile(T, 512)
    kb = w_pb.shape[2]

    def branch_dx(dv, b_ref):
        acc = _dot_nt(dv[:, :kb], b_ref[0])
        for j in range(1, N_CHIPS):
            acc = acc + _dot_nt(dv[:, j * kb:(j + 1) * kb], b_ref[j])
        return acc

    def body(dx_ref, w_ref, gp_ref, ga_ref, fp_ref, fa_ref, wp_ref, wa_ref, da_ref, db_ref, dg_ref, dpm_ref, do_ref):
        dm = _dot_nt(dx_ref[...].astype(CDT), w_ref[...])
        da = (dm * gp_ref[...].astype(F32)).astype(da_ref.dtype)
        db = (dm * ga_ref[...].astype(F32)).astype(db_ref.dtype)
        da_ref[...] = da
        db_ref[...] = db
        dg_ref[:, :D_MODEL] = (dm * fp_ref[...].astype(F32)).astype(dg_ref.dtype)
        dg_ref[:, D_MODEL:] = (dm * fa_ref[...].astype(F32)).astype(dg_ref.dtype)
        dpm_ref[...] = branch_dx(da, wp_ref).astype(dpm_ref.dtype)
        do_ref[...] = branch_dx(db, wa_ref).astype(do_ref.dtype)

    full = pl.BlockSpec((tm, D_MODEL), lambda i: (i, 0))
    half = pl.BlockSpec((tm, POOL_DIM), lambda i: (i, 0))
    gate = pl.BlockSpec((tm, GATE_DIM), lambda i: (i, 0))
    wspec = pl.BlockSpec(w_pb.shape, lambda i: (0, 0, 0))
    out = jax.ShapeDtypeStruct((T, D_MODEL), CDT)
    out_half = jax.ShapeDtypeStruct((T, POOL_DIM), CDT)
    return pl.pallas_call(
        body, name=name, grid=(T // tm,),
        in_specs=[full, pl.BlockSpec((D_MODEL, D_MODEL), lambda i: (0, 0))] + [full] * 4 + [wspec, wspec],
        out_specs=[full, full, gate, half, half],
        out_shape=[out, out, jax.ShapeDtypeStruct((T, GATE_DIM), CDT), out_half, out_half],
        compiler_params=_params("parallel"),
    )(dxo, w_out, *factors, w_pb, w_ab)


def _adamw(w, g, m, v, name):
    Rr, C = w.shape
    tr = _tile(Rr, max(8, (1 << 19) // C // 8 * 8))

    def body(w_ref, g_ref, m_ref, v_ref, go_ref, d_ref, nm_ref, nv_ref):
        gv = g_ref[...]
        go_ref[...] = gv
        nm = ADAM_B1 * m_ref[...] + (1.0 - ADAM_B1) * gv
        nv = ADAM_B2 * v_ref[...] + (1.0 - ADAM_B2) * (gv * gv)
        m_hat = nm / (1.0 - ADAM_B1 ** ADAM_STEP)
        v_hat = nv / (1.0 - ADAM_B2 ** ADAM_STEP)
        d_ref[...] = -ADAM_LR * (m_hat / (jnp.sqrt(v_hat) + ADAM_EPS) + ADAM_WD * w_ref[...])
        nm_ref[...] = nm
        nv_ref[...] = nv

    blk = pl.BlockSpec((tr, C), lambda i: (i, 0))
    out = jax.ShapeDtypeStruct((Rr, C), F32)
    return pl.pallas_call(
        body, name=name, grid=(Rr // tr,), in_specs=[blk] * 4, out_specs=[blk] * 4, out_shape=[out] * 4,
        compiler_params=_params("parallel"),
    )(w, g, m, v)


def _place():
    return lax.axis_index("x"), lax.axis_index("y"), lax.axis_index("c")


def _other_chip(x, y, d):
    return (1 - x if d & 2 else x), (1 - y if d & 1 else y)


def _rcopy(src, dst, ssem, rsem, dev):
    return pltpu.make_async_remote_copy(src_ref=src, dst_ref=dst, send_sem=ssem, recv_sem=rsem, device_id=dev,
                                        device_id_type=MESH)


def _row_half(rows, c):
    return pl.ds(c * (rows // 2), rows // 2)


def _is_wide(name):
    return name in WIDE


def _block(ref, wide, j, rows, n):
    if wide:
        return ref.at[rows, pl.ds(pl.multiple_of(j * n, LANES), n)]
    return ref.at[j, rows]


def _gathered_shape(shard, wide):
    _, a, n = shard.shape
    return jax.ShapeDtypeStruct((a, N_CHIPS * n) if wide else (N_CHIPS, a, n), shard.dtype)


def _gather_ici_side(shards, wides, l):
    k_of = lambda w, d: 3 * w + d - 1

    def issue(ins, outs, ssem, rsem):
        x, y, c = _place()
        cps = []
        for w, (shard, wide) in enumerate(zip(shards, wides)):
            _, a, n = shard.shape
            half = _row_half(a, c)
            for d in (1, 2, 3):
                px, py = _other_chip(x, y, d)
                cps.append(_rcopy(ins[w].at[l, half], _block(outs[w], wide, 2 * x + y, half, n),
                                  ssem.at[k_of(w, d)], rsem.at[k_of(w, d)], (px, py, c)))
        return cps

    return _Side(shards, [_gathered_shape(s_, wd) for s_, wd in zip(shards, wides)], 3 * len(shards), issue)


def _gather_d2d_side(shards, wides, gathered, l):
    nw = len(shards)

    def issue(ins, outs, ssem, rsem):
        x, y, c = _place()
        sibling = (x, y, 1 - c)
        cps = []
        for w, (shard, wide) in enumerate(zip(shards, wides)):
            _, a, n = shard.shape
            half = _row_half(a, c)
            for d in (1, 2, 3):
                px, py = _other_chip(x, y, d)
                k = 3 * w + d - 1
                got = _block(outs[w], wide, 2 * px + py, half, n)
                cps.append(_rcopy(got, got, ssem.at[k], rsem.at[k], sibling))
            cps.append(_rcopy(ins[nw + w].at[l], _block(outs[w], wide, 2 * x + y, pl.ds(0, a), n),
                              ssem.at[3 * nw + w], rsem.at[3 * nw + w], sibling))
        return cps

    return _Side(list(gathered) + list(shards), [jax.ShapeDtypeStruct(g.shape, g.dtype) for g in gathered], 4 * nw, issue,
                 aliases={w: w for w in range(nw)})


def _half_shape(g, wide):
    if wide:
        return jax.ShapeDtypeStruct((g.shape[0] // 2, g.shape[1]), g.dtype)
    return jax.ShapeDtypeStruct((N_CHIPS, g.shape[1] // 2, g.shape[2]), g.dtype)


def _reduce_sibling_side(gms, wides):
    def issue(ins, outs, ssem, rsem):
        x, y, c = _place()
        cps = []
        for w, (g, wide) in enumerate(zip(gms, wides)):
            src = ins[w].at[_row_half(g.shape[0], 1 - c)] if wide else ins[w].at[:, _row_half(g.shape[1], 1 - c)]
            cps.append(_rcopy(src, outs[w], ssem.at[w], rsem.at[w], (x, y, 1 - c)))
        return cps

    return _Side(gms, [_half_shape(g, wd) for g, wd in zip(gms, wides)], len(gms), issue)


def _reduce_chip_side(ps, wides):
    def slot_shape(p, wide):
        return jax.ShapeDtypeStruct((N_CHIPS, p.shape[0], p.shape[1] // N_CHIPS) if wide else p.shape, p.dtype)

    def issue(ins, outs, ssem, rsem):
        x, y, c = _place()
        cps = []
        for w, (p, wide) in enumerate(zip(ps, wides)):
            ah, n = (p.shape[0], p.shape[1] // N_CHIPS) if wide else p.shape[1:]
            for d in (1, 2, 3):
                px, py = _other_chip(x, y, d)
                k = 3 * w + d - 1
                cps.append(_rcopy(_block(ins[w], wide, 2 * px + py, pl.ds(0, ah), n), outs[w].at[2 * x + y],
                                  ssem.at[k], rsem.at[k], (px, py, c)))
        return cps

    return _Side(ps, [slot_shape(p, wd) for p, wd in zip(ps, wides)], 3 * len(ps), issue)


def _share_side(accs, items):
    def issue(ins, outs, ssem, rsem):
        x, y, c = _place()
        cps = []
        for k, (w, layer) in enumerate(items):
            mine = outs[w].at[layer, _row_half(accs[w].shape[1], c)]
            cps.append(_rcopy(mine, mine, ssem.at[k], rsem.at[k], (x, y, 1 - c)))
        return cps

    return _Side(accs, [jax.ShapeDtypeStruct(a.shape, a.dtype) for a in accs], len(items), issue,
                 aliases={w: w for w in range(len(accs))})


def _sum_rows(rows, b):
    return _tile(rows, max(16, (1 << 19) // b // 16 * 16), 16)


def _pair_sum(g, recv, wide, place, name):
    ah, b = recv.shape[-2:]
    ta = _sum_rows(ah, b)
    nr = ah // ta

    def body(p_ref, g_ref, r_ref, o_ref):
        o_ref[...] = (g_ref[...].astype(F32) + r_ref[...].astype(F32)).astype(o_ref.dtype)

    if wide:
        grid = (nr,)
        specs = [pl.BlockSpec((ta, b), lambda r, p: (p[0] * nr + r, 0)), pl.BlockSpec((ta, b), lambda r, p: (r, 0))]
        out_spec = pl.BlockSpec((ta, b), lambda r, p: (r, 0))
    else:
        grid = (N_CHIPS, nr)
        specs = [pl.BlockSpec((None, ta, b), lambda j, r, p: (j, p[0] * nr + r, 0)),
                 pl.BlockSpec((None, ta, b), lambda j, r, p: (j, r, 0))]
        out_spec = pl.BlockSpec((None, ta, b), lambda j, r, p: (j, r, 0))
    return pl.pallas_call(
        body, name=name,
        grid_spec=pltpu.PrefetchScalarGridSpec(num_scalar_prefetch=1, grid=grid, in_specs=specs, out_specs=out_spec),
        out_shape=jax.ShapeDtypeStruct(recv.shape, recv.dtype), compiler_params=_params(*["parallel"] * len(grid)),
    )(place, g, recv)


def _chip_sum(slots, part, wide, place, acc, l, name):
    _, ah, b = slots.shape
    ta = _sum_rows(ah, b)
    nr = ah // ta

    def body(p_ref, s_ref, own_ref, acc_ref, o_ref):
        j = p_ref[1]
        own = own_ref[...].astype(F32)
        term = [jnp.where(j == s_, own, s_ref[s_].astype(F32)) for s_ in range(N_CHIPS)]
        o_ref[...] = ((term[0] + term[1]) + term[2]) + term[3]

    own_spec = (pl.BlockSpec((ta, b), lambda r, p: (r, p[1])) if wide else
                pl.BlockSpec((None, ta, b), lambda r, p: (p[1], r, 0)))
    return pl.pallas_call(
        body, name=name,
        grid_spec=pltpu.PrefetchScalarGridSpec(
            num_scalar_prefetch=1, grid=(nr,),
            in_specs=[pl.BlockSpec((N_CHIPS, ta, b), lambda r, p: (0, r, 0)), own_spec, ANY],
            out_specs=pl.BlockSpec((None, ta, b), lambda r, p: (l, p[0] * nr + r, 0))),
        out_shape=jax.ShapeDtypeStruct(acc.shape, F32), input_output_aliases={3: 0},
        compiler_params=_params("parallel"),
    )(place, slots, part, acc)


def _small_side(v):
    def issue(ins, outs, ssem, rsem):
        x, y, c = _place()
        cps = []
        for d in range(1, N_DEV):
            px, py = _other_chip(x, y, d >> 1)
            pc = 1 - c if d & 1 else c
            cps.append(_rcopy(ins[0], outs[0].at[4 * x + 2 * y + c], ssem.at[d - 1], rsem.at[d - 1], (px, py, pc)))
        return cps

    return _Side([v], [jax.ShapeDtypeStruct((N_DEV,) + v.shape, v.dtype)], N_DEV - 1, issue)


def _small_sum(slots, v, place, name):
    def body(p_ref, s_ref, v_ref, o_ref):
        me = 2 * p_ref[1] + p_ref[0]
        acc = jnp.where(me == 0, v_ref[...], s_ref[0])
        for s_ in range(1, N_DEV):
            acc = acc + jnp.where(me == s_, v_ref[...], s_ref[s_])
        o_ref[...] = acc

    return pl.pallas_call(
        body, name=name,
        grid_spec=pltpu.PrefetchScalarGridSpec(
            num_scalar_prefetch=1, grid=(1,),
            in_specs=[pl.BlockSpec(slots.shape, lambda i, p: (0, 0, 0)), pl.BlockSpec(v.shape, lambda i, p: (0, 0))],
            out_specs=pl.BlockSpec(v.shape, lambda i, p: (0, 0))),
        out_shape=jax.ShapeDtypeStruct(v.shape, F32), compiler_params=_params("arbitrary"),
    )(place, slots, v)


def _ffn_forward(x, p, tag, side_of):
    h, gu, act = _ffn_up(x, p[f"ln_{tag}"], p[f"w_{tag}_gu"], f"{tag}_up", side_of(f"{tag}_up"))
    x_out = _mm_nn(act, p[f"w_{tag}_down"], f"{tag}_down", F32, res=x, scale=0.5, side=side_of(f"{tag}_down"))
    return x_out, (x, h, gu, act)


def _row_blocks_of(dw):
    return dw.reshape(N_CHIPS, dw.shape[0] // N_CHIPS, dw.shape[1])


def _ffn_backward(dxo, saved, p, tag, side_of, grad):
    x, h, gu, act = saved
    dgu = _ffn_down_bwd(dxo, p[f"w_{tag}_down"], gu, f"{tag}_down_bwd", side_of(f"{tag}_down_bwd"))
    grad(f"w_{tag}_down", _row_blocks_of(_mm_tn(act, dxo, f"{tag}_dwd", scale=0.5, side=side_of(f"{tag}_dwd"))))
    grad(f"w_{tag}_gu", _mm_tn(h, dgu, f"{tag}_dwgu", tn_target=2816, tm_target=1024, side=side_of(f"{tag}_dwgu")))
    dx, d_ln = _mm_nt_norm_bwd([dgu], p[f"w_{tag}_gu"], x, p[f"ln_{tag}"], dxo, f"{tag}_dh_norm_bwd",
                               side_of(f"{tag}_dh_norm_bwd"))
    grad(f"ln_{tag}", d_ln[0])
    return dx


def _mixer_forward(x, p, tabs, side_of):
    h, zu, zqk, zv, zg, qkn, pm = _mm_in(x, p["ln_mix"], p["w_in"], p["gqk"], *tabs, p["pool_w"], p["pool_scale"],
                                         "mix_in", side_of("mix_in"))
    o, lse = _attn_fwd(qkn, zv, p["sinks"], "attn_fwd", side_of("attn_fwd"))
    m, *factors = _merge_fwd(pm, o, p["w_pool_branch"], p["w_attn_branch"], zg, "merge_fwd", side_of("merge_fwd"))
    x_out = _mm_nn(m, p["w_out"], "mix_out", F32, res=x, scale=1.0)
    return x_out, (x, h, zu, zqk, zv, pm, qkn, o, lse, factors, m)


def _shift_up(v):
    return jnp.concatenate([v[BLOCK:], jnp.zeros((BLOCK, v.shape[1]), v.dtype)], axis=0)


def _mixer_backward(dxo, saved, p, tabs, side_of, grad):
    x, h, zu, zqk, zv, pm, qkn, o, lse, factors, m = saved
    d_a, d_b, dgl, dpm, do = _merge_bwd(dxo, p["w_out"], factors, p["w_pool_branch"], p["w_attn_branch"], "merge_bwd")
    grad("w_out", _row_blocks_of(_mm_tn(m, dxo, "mix_dwout")))
    grad("w_pool_branch", _mm_tn(pm, d_a, "pool_branch_dw", col_blocks=N_CHIPS))
    grad("w_attn_branch", _mm_tn(o, d_b, "attn_branch_dw", col_blocks=N_CHIPS))
    du, d_pool_w, d_pool_scale = _pool_bwd(zu, dpm, p["pool_w"], p["pool_scale"], "pool_bwd")
    grad("pool_w", d_pool_w)
    grad("pool_scale", d_pool_scale)
    dq, dkc, dkp, dvc, dvp, dsink = _attn_bwd(qkn, zv, p["sinks"], do, o, lse, "attn_bwd", side_of("attn_bwd"))
    dqk = jnp.concatenate([dq, dkc + _shift_up(dkp)], axis=1)
    dv = dvc + _shift_up(dvp)
    dzqk, dgqk = _qk_bwd(dqk, zqk, p["gqk"], *tabs, "qk_bwd")
    grad("q_norm", dgqk[0, :ATTN_DIM].reshape(N_Q_HEADS, HEAD_DIM).sum(axis=0))
    grad("k_norm", dgqk[0, ATTN_DIM:].reshape(KV_DIM // HEAD_DIM, HEAD_DIM).sum(axis=0))
    grad("sinks", -dsink[:, 0])
    dz = [du, dzqk, dv, dgl]
    grad("w_in", _blocks_from_full("w_in", _mm_tn_parts(h, dz, "mix_dwin")).astype(WIRE_DT))
    dx, d_ln = _mm_nt_norm_bwd(dz, p["w_in"], x, p["ln_mix"], dxo, "mix_dh_norm_bwd", side_of("mix_dh_norm_bwd"))
    grad("ln_mix", d_ln[0])
    return dx


class _NoComm:
    def __init__(self, layers):
        self.layers, self.grads = layers, [dict() for _ in layers]

    def weight(self, l, name):
        return self.layers[l][name]

    def side(self, phase, l, host):
        return None

    def grad(self, l, name, value):
        self.grads[l][name] = value


class _Layer:
    def __init__(self, hooks, l):
        self.hooks, self.l, self.got = hooks, l, {}

    def __getitem__(self, name):
        if name not in self.got:
            self.got[name] = self.hooks.weight(self.l, name)
        return self.got[name]


def _local_step(x, tgt, n_layers, hooks):
    T = x.shape[0]
    tabs = _rope_tables(T)
    saved, params = [], []
    for l in range(n_layers):
        p = _Layer(hooks, l)
        side_of = functools.partial(hooks.side, "fwd", l)
        x, s1 = _ffn_forward(x, p, "ffn1", side_of)
        x, s2 = _mixer_forward(x, p, tabs, side_of)
        x, s3 = _ffn_forward(x, p, "ffn2", side_of)
        saved.append((s1, s2, s3))
        params.append(p)
    dx, loss = _loss_head(x, tgt, "loss_head")
    for l in reversed(range(n_layers)):
        p = params[l]
        s1, s2, s3 = saved[l]
        side_of = functools.partial(hooks.side, "bwd", l)
        grad = functools.partial(hooks.grad, l)
        dx = _ffn_backward(dx, s3, p, "ffn2", side_of, grad)
        dx = _mixer_backward(dx, s2, p, tabs, side_of, grad)
        dx = _ffn_backward(dx, s1, p, "ffn1", side_of, grad)
    return loss, dx


def _full_from_blocks(name, blocks):
    if name in COL_SHARDED:
        return jnp.transpose(blocks, (1, 0, 2)).reshape(blocks.shape[1], N_CHIPS * blocks.shape[2])
    return blocks.reshape(N_CHIPS * blocks.shape[1], blocks.shape[2])


def _blocks_from_full(name, full):
    K, N = full.shape
    if name in COL_SHARDED:
        return jnp.transpose(full.reshape(K, N_CHIPS, N // N_CHIPS), (1, 0, 2))
    return full.reshape(N_CHIPS, K // N_CHIPS, N)


JOBS = {"a": ("w_ffn1_gu", "w_ffn1_down"), "b": ("w_in", "w_pool_branch", "w_attn_branch", "w_out"),
        "c": ("w_ffn2_gu", "w_ffn2_down")}
GATHER_PLAN = {"ffn1_up": ("ici", "b", JOBS["b"], 0), "ffn1_down": ("d2d", "b", JOBS["b"], 0),
               "mix_in": ("ici", "c", JOBS["c"][:1], 0), "attn_fwd": ("ici", "c", JOBS["c"][1:], 0),
               "merge_fwd": ("d2d", "c", JOBS["c"], 0),
               "ffn2_up": ("ici", "a", JOBS["a"], 1), "ffn2_down": ("d2d", "a", JOBS["a"], 1)}
REDUCE_PLAN = {"ffn2_down_bwd": ("sibling", "a", 1), "ffn2_dwgu": ("chip", "a", 1),
               "ffn2_dh_norm_bwd": ("sibling", "c", 0), "attn_bwd": ("chip", "c", 0),
               "mix_dh_norm_bwd": ("sibling", "b", 0), "ffn1_down_bwd": ("chip", "b", 0)}
SHARE_HOST = "ffn1_dwgu"
SMALL_HOST = "ffn2_dwd"
LAST_GRAD = "ln_ffn1"


class _Exchange:
    def __init__(self, shards, small, place, n_layers):
        self.shards, self.small, self.place, self.n_layers = shards, small, place, n_layers
        first, wides = [shards[n] for n in JOBS["a"]], [_is_wide(n) for n in JOBS["a"]]
        got = _run_side(_gather_ici_side(first, wides, 0), "gather_ici")
        got = _run_side(_gather_d2d_side(first, wides, got, 0), "gather_d2d")
        self.blocks = {(n, 0): g for n, g in zip(JOBS["a"], got)}
        self.landed = {}
        self.handed = []
        self.acc = {n: lax.empty(shards[n].shape, F32) for n in BIG}
        self.grads = [dict() for _ in range(n_layers)]
        self.reduce = {}
        self.summed = set()
        self.unshared, self.sharing = [], None
        self.small_sides, self.small_waiting = {}, None

    def weight(self, l, name):
        if name not in BIG:
            return self.small(l)[name]
        for names, layer, done in self.handed:
            self.blocks.update({(n, layer): g for n, g in zip(names, done.outs)})
        self.handed.clear()
        blocks = self.blocks.pop((name, l))
        return blocks if name in USED_AS_BLOCKS + WIDE else _full_from_blocks(name, blocks)

    def _gather_side(self, l, host):
        step, job, names, ahead = GATHER_PLAN[host]
        layer = l + ahead
        if layer >= self.n_layers:
            return None
        if step == "ici":
            side = _gather_ici_side([self.shards[n] for n in names], [_is_wide(n) for n in names], layer)
            self.landed.setdefault((job, layer), []).append((names, side))
            return side
        names, gathered = JOBS[job], {}
        for part_names, side in self.landed.pop((job, layer)):
            gathered.update(zip(part_names, side.outs))
        done = _gather_d2d_side([self.shards[n] for n in names], [_is_wide(n) for n in names],
                                [gathered[n] for n in names], layer)
        self.handed.append((names, layer, done))
        return done

    def grad(self, l, name, value):
        self.grads[l][name] = value
        if name == LAST_GRAD and l > 0:
            packed, self.small_spans = _pack_small([self.grads[l][n] for n in SMALL])
            self.small_sides[l] = _small_side(packed)
            self.small_waiting = l

    def reduced_small(self, loss_part):
        packed, spans = _pack_small([self.grads[0][n] for n in SMALL] + [loss_part])
        self.small_sides[0] = _small_side(packed)
        _run_side(self.small_sides[0], "all_reduce_small")
        shapes = [self.grads[0][n].shape for n in SMALL]
        per_layer = []
        for l in range(self.n_layers):
            side = self.small_sides[l]
            summed = _small_sum(side.outs[0], side.ins[0], self.place, "small_sum")
            per_layer.append(_unpack_small(summed, spans, shapes + [(1, 1)] * (l == 0)))
        loss = per_layer[0][-1][0, 0]
        return {n: jnp.stack([vals[k] for vals in per_layer]) for k, n in enumerate(SMALL)}, loss

    def _reduce_side(self, l, host):
        step, job, ahead = REDUCE_PLAN[host]
        layer = l + ahead
        if layer >= self.n_layers:
            return None
        return self._reduce_step(step, job, layer)

    def _reduce_step(self, step, job, layer):
        if step == "sibling":
            st = self.reduce[(job, layer)] = dict(gm=[self.grads[layer][n] for n in JOBS[job]],
                                                  wide=[_is_wide(n) for n in JOBS[job]])
            st["sibling"] = _reduce_sibling_side(st["gm"], st["wide"])
            return st["sibling"]
        st = self.reduce[(job, layer)]
        st["part"] = [_pair_sum(g, r, wd, self.place, "grad_pair_sum")
                      for g, r, wd in zip(st["gm"], st["sibling"].outs, st["wide"])]
        st["chip"] = _reduce_chip_side(st["part"], st["wide"])
        return st["chip"]

    def _chip_sums(self):
        if self.sharing is not None:
            self.acc.update(zip(BIG, self.sharing.outs))
            self.sharing = None
        for (job, layer), st in self.reduce.items():
            if (job, layer) not in self.summed and "chip" in st and st["chip"].outs is not None:
                self.summed.add((job, layer))
                for n, slots, part, wd in zip(JOBS[job], st["chip"].outs, st["part"], st["wide"]):
                    self.acc[n] = _chip_sum(slots, part, wd, self.place, self.acc[n], layer, "grad_chip_sum")
                    self.unshared.append((BIG.index(n), layer))

    def _share(self):
        side = _share_side([self.acc[n] for n in BIG], self.unshared)
        self.unshared = []
        return side

    def side(self, phase, l, host):
        if phase == "fwd":
            return self._gather_side(l, host) if host in GATHER_PLAN else None
        self._chip_sums()
        if host == SMALL_HOST and self.small_waiting is not None:
            side, self.small_waiting = self.small_sides[self.small_waiting], None
            return side
        if host == SHARE_HOST and self.unshared:
            self.sharing = self._share()
            return self.sharing
        return self._reduce_side(l, host) if host in REDUCE_PLAN else None

    def reduced(self):
        _run_side(self._reduce_step("sibling", "a", 0), "grad_sibling_exchange")
        _run_side(self._reduce_step("chip", "a", 0), "grad_chip_exchange")
        self._chip_sums()
        return dict(zip(BIG, _run_side(self._share(), "grad_sibling_share")))


def _pack_small(parts):
    rows, spans, lo = [], [], 0
    for v in parts:
        flat = v.reshape(-1)
        nrow = -(-flat.shape[0] // LANES)
        flat = jnp.pad(flat, (0, nrow * LANES - flat.shape[0]))
        rows.append(flat.reshape(nrow, LANES))
        spans.append((lo, nrow))
        lo += nrow
    pad = -lo % 8
    if pad:
        rows.append(jnp.zeros((pad, LANES), F32))
    return jnp.concatenate(rows, axis=0), spans


def _unpack_small(packed, spans, shapes):
    out = []
    for (lo, nrow), shape in zip(spans, shapes):
        size = 1
        for s in shape:
            size *= s
        out.append(packed[lo:lo + nrow].reshape(-1)[:size].reshape(shape))
    return out


def kernel(x, ln_ffn1, w_ffn1_gu, w_ffn1_down, ln_mix, w_in, pool_w, pool_scale, w_pool_branch, q_norm, k_norm, sinks, w_attn_branch, w_out, ln_ffn2, w_ffn2_gu, w_ffn2_down, loss_target, m_ln_ffn1, m_w_ffn1_gu, m_w_ffn1_down, m_ln_mix, m_w_in, m_pool_w, m_pool_scale, m_w_pool_branch, m_q_norm, m_k_norm, m_sinks, m_w_attn_branch, m_w_out, m_ln_ffn2, m_w_ffn2_gu, m_w_ffn2_down, v_ln_ffn1, v_w_ffn1_gu, v_w_ffn1_down, v_ln_mix, v_w_in, v_pool_w, v_pool_scale, v_w_pool_branch, v_q_norm, v_k_norm, v_sinks, v_w_attn_branch, v_w_out, v_ln_ffn2, v_w_ffn2_gu, v_w_ffn2_down):
    w = dict(ln_ffn1=ln_ffn1, w_ffn1_gu=w_ffn1_gu, w_ffn1_down=w_ffn1_down, ln_mix=ln_mix, w_in=w_in, pool_w=pool_w,
             pool_scale=pool_scale, w_pool_branch=w_pool_branch, q_norm=q_norm, k_norm=k_norm, sinks=sinks,
             w_attn_branch=w_attn_branch, w_out=w_out, ln_ffn2=ln_ffn2, w_ffn2_gu=w_ffn2_gu, w_ffn2_down=w_ffn2_down)
    mom = dict(ln_ffn1=m_ln_ffn1, w_ffn1_gu=m_w_ffn1_gu, w_ffn1_down=m_w_ffn1_down, ln_mix=m_ln_mix, w_in=m_w_in,
               pool_w=m_pool_w, pool_scale=m_pool_scale, w_pool_branch=m_w_pool_branch, q_norm=m_q_norm, k_norm=m_k_norm,
               sinks=m_sinks, w_attn_branch=m_w_attn_branch, w_out=m_w_out, ln_ffn2=m_ln_ffn2, w_ffn2_gu=m_w_ffn2_gu,
               w_ffn2_down=m_w_ffn2_down)
    var = dict(ln_ffn1=v_ln_ffn1, w_ffn1_gu=v_w_ffn1_gu, w_ffn1_down=v_w_ffn1_down, ln_mix=v_ln_mix, w_in=v_w_in,
               pool_w=v_pool_w, pool_scale=v_pool_scale, w_pool_branch=v_w_pool_branch, q_norm=v_q_norm, k_norm=v_k_norm,
               sinks=v_sinks, w_attn_branch=v_w_attn_branch, w_out=v_w_out, ln_ffn2=v_ln_ffn2, w_ffn2_gu=v_w_ffn2_gu,
               w_ffn2_down=v_w_ffn2_down)
    L = ln_ffn1.shape[0]

    def small(l):
        return dict(ln_ffn1=ln_ffn1[l], ln_mix=ln_mix[l], ln_ffn2=ln_ffn2[l], pool_w=pool_w[l].astype(CDT),
                    pool_scale=pool_scale[l], sinks=sinks[l],
                    gqk=jnp.concatenate([jnp.tile(q_norm[l], N_Q_HEADS), jnp.tile(k_norm[l], KV_DIM // HEAD_DIM)]).reshape(1, QK_DIM))

    place = jnp.stack([lax.axis_index("c"), 2 * lax.axis_index("x") + lax.axis_index("y")]).astype(jnp.int32)
    hooks = _Exchange({n: w[n].astype(CDT) for n in BIG}, small, place, L)
    loss_part, grad_x = _local_step(x[0], loss_target[0], L, hooks)
    g_big = hooks.reduced()
    g_small, loss = hooks.reduced_small(loss_part)
    g_small = {n: v.reshape(w[n].shape) for n, v in g_small.items()}

    grad_out, delta, new_m, new_v = {}, {}, {}, {}
    for n in BIG:
        shape = w[n].shape
        flat = (shape[0] * shape[1], shape[2])
        go, d, nm, nv = _adamw(w[n].reshape(flat), g_big[n].reshape(flat), mom[n].reshape(flat), var[n].reshape(flat), "adamw")
        grad_out[n], delta[n], new_m[n], new_v[n] = go.reshape(shape), d.reshape(shape), nm.reshape(shape), nv.reshape(shape)
    pw, _ = _pack_small([w[n] for n in SMALL])
    pg, sp = _pack_small([g_small[n] for n in SMALL])
    pm_, _ = _pack_small([mom[n] for n in SMALL])
    pv, _ = _pack_small([var[n] for n in SMALL])
    _, d, nm, nv = _adamw(pw, pg, pm_, pv, "adamw_small")
    shapes = [w[n].shape for n in SMALL]
    for n, dv, mv, vv in zip(SMALL, _unpack_small(d, sp, shapes), _unpack_small(nm, sp, shapes), _unpack_small(nv, sp, shapes)):
        grad_out[n], delta[n], new_m[n], new_v[n] = g_small[n], dv, mv, vv

    return (loss, grad_x[None], *[grad_out[n] for n in WEIGHTS], *[delta[n] for n in WEIGHTS],
            *[new_m[n] for n in WEIGHTS], *[new_v[n] for n in WEIGHTS])
```

```python
import functools
import math

import jax
import jax.numpy as jnp
from jax import lax
from jax.experimental import pallas as pl
from jax.experimental.pallas import tpu as pltpu

F32 = jnp.float32
CDT = jnp.bfloat16
WIRE_DT = jnp.bfloat16

D_MODEL = 1024
POOL_WINDOWS = (2, 4, 8, 16)
POOL_WMAX = 16
GROUP = 128
POOL_DIM = 512
HEAD_DIM = 64
N_Q_HEADS = 8
ATTN_DIM = 512
KV_DIM = 128
QK_DIM = ATTN_DIM + KV_DIM
GATE_DIM = 2 * D_MODEL
BLOCK = 128
ROPE_THETA = 500000.0
ROT_DIM = 16
EPS = 1e-6
ATTN_SCALE = HEAD_DIM ** -0.5

ADAM_LR = 0.001
ADAM_B1 = 0.9
ADAM_B2 = 0.999
ADAM_EPS = 1e-08
ADAM_WD = 0.01
ADAM_STEP = 10

N_CHIPS = 4
N_DEV = 8
LANES = 128
VMEM_LIMIT_BYTES = 48 * 1024 * 1024

MESH = pl.DeviceIdType.MESH
ANY = pl.BlockSpec(memory_space=pl.ANY)

BIG = ("w_ffn1_gu", "w_ffn1_down", "w_in", "w_pool_branch", "w_attn_branch", "w_out", "w_ffn2_gu", "w_ffn2_down")
COL_SHARDED = ("w_ffn1_gu", "w_in", "w_pool_branch", "w_attn_branch", "w_ffn2_gu")
USED_AS_BLOCKS = ("w_pool_branch", "w_attn_branch")
WIDE = ("w_ffn1_gu", "w_ffn2_gu")
SMALL = ("ln_ffn1", "ln_mix", "pool_w", "pool_scale", "q_norm", "k_norm", "sinks", "ln_ffn2")
WEIGHTS = ("ln_ffn1", "w_ffn1_gu", "w_ffn1_down", "ln_mix", "w_in", "pool_w", "pool_scale", "w_pool_branch",
           "q_norm", "k_norm", "sinks", "w_attn_branch", "w_out", "ln_ffn2", "w_ffn2_gu", "w_ffn2_down")


def _tile(n, target, mult=8):
    if n <= target:
        return n
    for t in range(target - target % mult, 0, -mult):
        if n % t == 0:
            return t
    raise ValueError((n, target, mult))


def _params(*sem):
    return pltpu.CompilerParams(dimension_semantics=sem, vmem_limit_bytes=VMEM_LIMIT_BYTES)


def _sigmoid(v):
    return 0.5 * jnp.tanh(0.5 * v) + 0.5


def _dot(a, b):
    return jnp.dot(a, b, preferred_element_type=F32)


def _dot_nt(a, b):
    return lax.dot_general(a, b, (((1,), (1,)), ((), ())), preferred_element_type=F32)


def _dot_tn(a, b):
    return lax.dot_general(a, b, (((0,), (0,)), ((), ())), preferred_element_type=F32)


class _Side:
    def __init__(self, ins, out_shapes, n_sems, issue, aliases=None):
        self.ins, self.out_shapes, self.n_sems, self.issue = list(ins), list(out_shapes), n_sems, issue
        self.aliases = dict(aliases or {})
        self.outs = None


def _pcall(body, name, grid, in_specs, out_specs, out_shape, args, dims, side=None, scratch=()):
    scratch = list(scratch)
    if side is None:
        return pl.pallas_call(body, name=name, grid=grid, in_specs=in_specs, out_specs=out_specs, out_shape=out_shape,
                              scratch_shapes=scratch, compiler_params=_params(*dims))(*args)
    n_in, n_out, s_in, s_out = len(in_specs), len(out_specs), len(side.ins), len(side.out_shapes)

    def wrapped(*refs):
        main_in, side_in = refs[:n_in], refs[n_in:n_in + s_in]
        main_out = refs[n_in + s_in:n_in + s_in + n_out]
        side_out = refs[n_in + s_in + n_out:n_in + s_in + n_out + s_out]
        rest = refs[n_in + s_in + n_out + s_out:]
        main_scratch, (ssem, rsem) = rest[:len(scratch)], rest[len(scratch):]
        ids = [pl.program_id(ax) for ax in range(len(grid))]
        first = functools.reduce(jnp.logical_and, [i == 0 for i in ids])
        last = functools.reduce(jnp.logical_and, [i == g - 1 for i, g in zip(ids, grid)])

        @pl.when(first)
        def _():
            for cp in side.issue(side_in, side_out, ssem, rsem):
                cp.start()

        body(*main_in, *main_out, *main_scratch)

        @pl.when(last)
        def _():
            cps = side.issue(side_in, side_out, ssem, rsem)
            for cp in cps:
                cp.wait_recv()
            for cp in cps:
                cp.wait_send()

    outs = pl.pallas_call(
        wrapped, name=name, grid=grid, in_specs=list(in_specs) + [ANY] * s_in, out_specs=list(out_specs) + [ANY] * s_out,
        out_shape=list(out_shape) + side.out_shapes,
        input_output_aliases={n_in + i: n_out + o for i, o in side.aliases.items()},
        scratch_shapes=scratch + [pltpu.SemaphoreType.DMA((side.n_sems,))] * 2,
        compiler_params=_params(*["arbitrary"] * len(grid)),
    )(*args, *side.ins)
    side.outs = list(outs[n_out:])
    return list(outs[:n_out])


def _run_side(side, name):
    s_in = len(side.ins)

    def body(*refs):
        ssem, rsem = refs[s_in + len(side.out_shapes):]
        cps = side.issue(refs[:s_in], refs[s_in:s_in + len(side.out_shapes)], ssem, rsem)
        for cp in cps:
            cp.start()
        for cp in cps:
            cp.wait_recv()
        for cp in cps:
            cp.wait_send()

    side.outs = list(pl.pallas_call(
        body, name=name, in_specs=[ANY] * s_in, out_specs=[ANY] * len(side.out_shapes), out_shape=side.out_shapes,
        input_output_aliases=side.aliases, scratch_shapes=[pltpu.SemaphoreType.DMA((side.n_sems,))] * 2,
    )(*side.ins))
    return side.outs


def _loss_head(y, tgt, name):
    T, Dm = y.shape
    tm = _tile(T, 512)

    def body(y_ref, t_ref, dy_ref, loss_ref):
        @pl.when(pl.program_id(0) == 0)
        def _():
            loss_ref[...] = jnp.zeros_like(loss_ref)

        diff = y_ref[...] - t_ref[...]
        dy_ref[...] = diff * (1.0 / Dm)
        part = jnp.sum(jnp.mean(diff * diff, axis=-1, keepdims=True), axis=0, keepdims=True)
        loss_ref[...] += 0.5 * part

    row = pl.BlockSpec((tm, Dm), lambda i: (i, 0))
    one = pl.BlockSpec((1, 1), lambda i: (0, 0))
    return pl.pallas_call(
        body, name=name, grid=(T // tm,),
        in_specs=[row, row], out_specs=[row, one],
        out_shape=[jax.ShapeDtypeStruct((T, Dm), F32), jax.ShapeDtypeStruct((1, 1), F32)],
        compiler_params=_params("arbitrary"),
    )(y, tgt)


def _mm_nn(a, b, name, out_dtype, res=None, scale=1.0, tm_target=512, side=None):
    M, K = a.shape
    N = b.shape[1]
    tm = _tile(M, tm_target)

    def body(a_ref, b_ref, *rest):
        acc = _dot(a_ref[...].astype(CDT), b_ref[...])
        if res is None:
            (o_ref,) = rest
        else:
            r_ref, o_ref = rest
            acc = r_ref[...] + scale * acc
        o_ref[...] = acc.astype(o_ref.dtype)

    in_specs = [pl.BlockSpec((tm, K), lambda i: (i, 0)), pl.BlockSpec((K, N), lambda i: (0, 0))]
    args = [a, b]
    if res is not None:
        in_specs.append(pl.BlockSpec((tm, N), lambda i: (i, 0)))
        args.append(res)
    return _pcall(body, name, (M // tm,), in_specs, [pl.BlockSpec((tm, N), lambda i: (i, 0))],
                  [jax.ShapeDtypeStruct((M, N), out_dtype)], args, ("parallel",), side)[0]


def _mm_tn(x, dy, name, scale=1.0, col_blocks=1, tn_target=1664, tm_target=1408, tk_target=1024, side=None):
    T, M = x.shape
    split = dy.ndim == 3
    Nh = dy.shape[-1]
    N = 2 * Nh if split else Nh
    nb = N // col_blocks
    whole = col_blocks > 1 and not split and N <= tn_target
    tm = _tile(M, tm_target, LANES)
    tn = N if whole else _tile(math.gcd(Nh, nb), tn_target, LANES)
    tk = _tile(T, tk_target)
    nk = T // tk
    njh, njb = Nh // tn, max(nb // tn, 1)

    def body(x_ref, dy_ref, o_ref, acc_ref):
        k = pl.program_id(2)

        @pl.when(k == 0)
        def _():
            acc_ref[...] = jnp.zeros_like(acc_ref)

        acc_ref[...] += _dot_tn(x_ref[...].astype(CDT), dy_ref[...].astype(CDT))

        @pl.when(k == nk - 1)
        def _():
            res = (acc_ref[...] if scale == 1.0 else scale * acc_ref[...]).astype(o_ref.dtype)
            if whole:
                for b in range(col_blocks):
                    o_ref[b] = res[:, b * nb:(b + 1) * nb]
            else:
                o_ref[...] = res

    if split:
        dy_spec = pl.BlockSpec((None, tk, tn), lambda i, j, k: (j // njh, k, j % njh))
    else:
        dy_spec = pl.BlockSpec((tk, tn), lambda i, j, k: (k, j))
    if col_blocks == 1:
        out_spec, out_dims = pl.BlockSpec((tm, tn), lambda i, j, k: (i, j)), (M, N)
    elif whole:
        out_spec, out_dims = pl.BlockSpec((col_blocks, tm, nb), lambda i, j, k: (0, i, 0)), (col_blocks, M, nb)
    else:
        out_spec, out_dims = pl.BlockSpec((None, tm, tn), lambda i, j, k: (j // njb, i, j % njb)), (col_blocks, M, nb)
    return _pcall(body, name, (M // tm, N // tn, nk), [pl.BlockSpec((tk, tm), lambda i, j, k: (k, i)), dy_spec],
                  [out_spec], [jax.ShapeDtypeStruct(out_dims, WIRE_DT)], (x, dy), ("parallel", "parallel", "arbitrary"),
                  side, [pltpu.VMEM((tm, tn), F32)])[0]


def _mm_tn_parts(x, parts, name):
    T, M = x.shape
    widths = [p.shape[1] for p in parts]
    N = sum(widths)
    tk = _tile(T, 512)

    def body(x_ref, *refs):
        o_ref = refs[-1]

        @pl.when(pl.program_id(0) == 0)
        def _():
            o_ref[...] = jnp.zeros_like(o_ref)

        parts_cat = jnp.concatenate([p_ref[...].astype(CDT) for p_ref in refs[:-1]], axis=1)
        o_ref[...] += _dot_tn(x_ref[...].astype(CDT), parts_cat)

    return pl.pallas_call(
        body, name=name, grid=(T // tk,),
        in_specs=[pl.BlockSpec((tk, M), lambda k: (k, 0))] + [pl.BlockSpec((tk, wd), lambda k: (k, 0)) for wd in widths],
        out_specs=pl.BlockSpec((M, N), lambda k: (0, 0)),
        out_shape=jax.ShapeDtypeStruct((M, N), F32), compiler_params=_params("arbitrary"),
    )(x, *parts)


def _rmsnorm_rows(x_ref, g_ref):
    xv = x_ref[...]
    r = lax.rsqrt(jnp.mean(xv * xv, axis=-1, keepdims=True) + EPS)
    return (xv * r * g_ref[...]).astype(CDT)


def _ffn_up(x, ln, wgu, name, side=None):
    T, Dm = x.shape
    Fd = wgu.shape[1] // 2
    tm = _tile(T, 256)

    def body(x_ref, ln_ref, wg_ref, wu_ref, h_ref, gu_ref, a_ref):
        hv = _rmsnorm_rows(x_ref, ln_ref)
        h_ref[...] = hv
        g = _dot(hv, wg_ref[...])
        u = _dot(hv, wu_ref[...])
        sg = _sigmoid(g)
        silu = g * sg
        a_ref[...] = (silu * u).astype(a_ref.dtype)
        gu_ref[0] = (0.5 * u * (sg * (1.0 + g * (1.0 - sg)))).astype(gu_ref.dtype)
        gu_ref[1] = (0.5 * silu).astype(gu_ref.dtype)

    row = pl.BlockSpec((tm, Dm), lambda i: (i, 0))
    return _pcall(
        body, name, (T // tm,),
        [row, pl.BlockSpec((1, Dm), lambda i: (0, 0)),
         pl.BlockSpec((Dm, Fd), lambda i: (0, 0), pipeline_mode=pl.Buffered(1)),
         pl.BlockSpec((Dm, Fd), lambda i: (0, 1), pipeline_mode=pl.Buffered(1))],
        [row, pl.BlockSpec((2, tm, Fd), lambda i: (0, i, 0)), pl.BlockSpec((tm, Fd), lambda i: (i, 0))],
        [jax.ShapeDtypeStruct((T, Dm), CDT), jax.ShapeDtypeStruct((2, T, Fd), CDT), jax.ShapeDtypeStruct((T, Fd), CDT)],
        (x, ln.reshape(1, Dm), wgu, wgu), ("parallel",), side)


def _ffn_down_bwd(dxo, wd, gu, name, side=None):
    T, Dm = dxo.shape
    Fd = wd.shape[0]
    tm = _tile(T, 256)

    def body(dx_ref, wd_ref, gu_ref, dgu_ref):
        da = _dot_nt(dx_ref[...].astype(CDT), wd_ref[...])
        dgu_ref[0] = (da * gu_ref[0].astype(F32)).astype(dgu_ref.dtype)
        dgu_ref[1] = (da * gu_ref[1].astype(F32)).astype(dgu_ref.dtype)

    gu_spec = pl.BlockSpec((2, tm, Fd), lambda i: (0, i, 0))
    return _pcall(
        body, name, (T // tm,),
        [pl.BlockSpec((tm, Dm), lambda i: (i, 0)),
         pl.BlockSpec((Fd, Dm), lambda i: (0, 0), pipeline_mode=pl.Buffered(1)), gu_spec],
        [gu_spec], [jax.ShapeDtypeStruct((2, T, Fd), CDT)],
        (dxo, wd, gu), ("parallel",), side)[0]


def _mm_nt_norm_bwd(a_parts, b, x, g, dres, name, side=None):
    T, Dm = x.shape
    tm = _tile(T, 256)

    def b_cols(b_ref, lo, wd):
        if b.ndim == 2:
            return [(0, wd, b_ref[:, lo:lo + wd])]
        kb = b.shape[2]
        return [(j * kb - lo, kb, b_ref[j]) for j in range(lo // kb, (lo + wd) // kb)]

    def body(*refs):
        a_refs, (b_ref, x_ref, g_ref, dres_ref, dx_ref, dg_ref) = refs[:len(a_parts)], refs[len(a_parts):]

        @pl.when(pl.program_id(0) == 0)
        def _():
            dg_ref[...] = jnp.zeros_like(dg_ref)

        dh, lo = None, 0
        if b.ndim == 2 and len(a_parts) > 1 and all(p.ndim == 2 for p in a_parts):
            dh = _dot_nt(jnp.concatenate([a_ref[...].astype(CDT) for a_ref in a_refs], axis=1), b_ref[...])
            a_refs = ()
        for a_ref, part in zip(a_refs, a_parts):
            slabs = [a_ref] if part.ndim == 2 else [a_ref.at[s_] for s_ in range(part.shape[0])]
            for slab in slabs:
                for off, wd, bv in b_cols(b_ref, lo, part.shape[-1]):
                    term = _dot_nt(slab[:, off:off + wd].astype(CDT), bv)
                    dh = term if dh is None else dh + term
                lo += part.shape[-1]
        xv = x_ref[...]
        r = lax.rsqrt(jnp.mean(xv * xv, axis=-1, keepdims=True) + EPS)
        xh = xv * r
        dg_ref[...] += jnp.sum(dh * xh, axis=0, keepdims=True)
        dxh = dh * g_ref[...]
        dx_ref[...] = dres_ref[...] + r * (dxh - xh * jnp.mean(dxh * xh, axis=-1, keepdims=True))

    row = pl.BlockSpec((tm, Dm), lambda i: (i, 0))
    vec = pl.BlockSpec((1, Dm), lambda i: (0, 0))
    a_specs = [pl.BlockSpec((tm, p.shape[1]), lambda i: (i, 0)) if p.ndim == 2 else
               pl.BlockSpec((p.shape[0], tm, p.shape[2]), lambda i: (0, i, 0)) for p in a_parts]
    b_spec = pl.BlockSpec(b.shape, lambda i: (0,) * b.ndim, pipeline_mode=pl.Buffered(1))
    return _pcall(body, name, (T // tm,), a_specs + [b_spec, row, vec, row], [row, vec],
                  [jax.ShapeDtypeStruct((T, Dm), F32), jax.ShapeDtypeStruct((1, Dm), F32)],
                  (*a_parts, b, x, g.reshape(1, Dm), dres), ("arbitrary",), side)


def _mm_in(x, ln, w_in, gqk, cos_t, sin_t, pool_w, pool_scale, name, side=None):
    T, Dm = x.shape
    tm = _tile(T, 256, POOL_WMAX)
    widths = (POOL_DIM, QK_DIM, KV_DIM, GATE_DIM)

    def body(x_ref, ln_ref, w_ref, g_ref, c_ref, s_ref, pw_ref, sc_ref,
             h_ref, zu_ref, zqk_ref, zv_ref, zg_ref, qkn_ref, pm_ref, halo_ref):
        i = pl.program_id(0)
        hv = _rmsnorm_rows(x_ref, ln_ref)
        h_ref[...] = hv
        z = _dot(hv, w_ref[...])
        lo = 0
        for o_ref, wd in zip((zu_ref, zqk_ref, zv_ref, zg_ref), widths):
            o_ref[...] = z[:, lo:lo + wd].astype(o_ref.dtype)
            lo += wd
        first, low = _lane_masks()
        cosv, sinv = c_ref[...], s_ref[...]
        for c in range(QK_DIM // LANES):
            sl = slice(c * LANES, (c + 1) * LANES)
            xv = z[:, POOL_DIM + c * LANES:POOL_DIM + (c + 1) * LANES]
            r = lax.rsqrt(_head_mean(xv * xv, first) + EPS)
            xn = xv * r * g_ref[:, sl]
            qkn_ref[:, sl] = (xn * cosv + _rope_partner(xn, low) * sinv).astype(qkn_ref.dtype)
        @pl.when(i == 0)
        def _():
            halo_ref[...] = jnp.zeros_like(halo_ref)

        zu = z[:, :POOL_DIM]
        ext = jnp.concatenate([halo_ref[...], zu], axis=0)
        halo_ref[...] = zu[tm - POOL_WMAX:, :]
        pos = i * tm + lax.broadcasted_iota(jnp.int32, (tm, 1), 0)
        ys = [_dot(_window_mean_minus_token(ext, zu, g, w, pos).astype(CDT), pw_ref[g])
              for g, w in enumerate(POOL_WINDOWS)]
        pm_ref[...] = (jnp.concatenate(ys, axis=1) * sc_ref[...]).astype(pm_ref.dtype)

    row = pl.BlockSpec((tm, Dm), lambda i: (i, 0))
    tab = pl.BlockSpec((tm, LANES), lambda i: (i, 0))
    return _pcall(body, name, (T // tm,),
                  [row, pl.BlockSpec((1, Dm), lambda i: (0, 0)),
                   pl.BlockSpec(w_in.shape, lambda i: (0, 0), pipeline_mode=pl.Buffered(1)),
                   pl.BlockSpec((1, QK_DIM), lambda i: (0, 0)), tab, tab,
                   pl.BlockSpec(pool_w.shape, lambda i: (0, 0, 0)), pl.BlockSpec((1, POOL_DIM), lambda i: (0, 0))],
                  [row] + [pl.BlockSpec((tm, wd), lambda i: (i, 0)) for wd in widths + (QK_DIM, POOL_DIM)],
                  [jax.ShapeDtypeStruct((T, Dm), CDT)]
                  + [jax.ShapeDtypeStruct((T, wd), dt) for wd, dt in zip(widths, (F32, F32, F32, CDT))]
                  + [jax.ShapeDtypeStruct((T, QK_DIM), CDT), jax.ShapeDtypeStruct((T, POOL_DIM), CDT)],
                  (x, ln.reshape(1, Dm), w_in, gqk, cos_t, sin_t, pool_w, pool_scale.reshape(1, POOL_DIM)),
                  ("arbitrary",), side, [pltpu.VMEM((POOL_WMAX, POOL_DIM), F32)])


def _window_mean_minus_token(ext, u, g, w, pos):
    sl = slice(g * GROUP, (g + 1) * GROUP)
    s = ext[:, sl]
    span = 1
    while span < w:
        s = s + pltpu.roll(s, span, axis=0)
        span *= 2
    cnt = jnp.minimum(pos + 1, w).astype(F32)
    return s[POOL_WMAX:, :] / cnt - u[:, sl]


def _pool_bwd(zu, dpm, pool_w, scale, name):
    T = zu.shape[0]
    tm = _tile(T, 512, POOL_WMAX)
    hb = tm // POOL_WMAX
    nsteps = T // tm
    ext_rows = tm + POOL_WMAX

    def body(u_ref, halo_ref, dpm_ref, dnext_ref, pw_ref, sc_ref, du_ref, dpw_ref, dsc_ref):
        i = pl.program_id(0)

        @pl.when(i == 0)
        def _():
            dpw_ref[...] = jnp.zeros_like(dpw_ref)
            dsc_ref[...] = jnp.zeros_like(dsc_ref)

        u = u_ref[...]
        halo = jnp.where(i > 0, halo_ref[...], 0.0)
        ext = jnp.concatenate([halo, u], axis=0)
        dpm_t = dpm_ref[...].astype(F32)
        dnext = jnp.where(i < nsteps - 1, dnext_ref[...].astype(F32), 0.0)
        dext = jnp.concatenate([dpm_t, dnext], axis=0)
        sc = sc_ref[...]
        pos = i * tm + lax.broadcasted_iota(jnp.int32, (tm, 1), 0)
        pos_ext = i * tm + lax.broadcasted_iota(jnp.int32, (ext_rows, 1), 0)
        dus, dscs = [], []
        for g, w in enumerate(POOL_WINDOWS):
            sl = slice(g * GROUP, (g + 1) * GROUP)
            dc = _window_mean_minus_token(ext, u, g, w, pos).astype(CDT)
            y = _dot(dc, pw_ref[g])
            dscs.append(jnp.sum(dpm_t[:, sl] * y, axis=0, keepdims=True))
            dy_ext = (dext[:, sl] * sc[:, sl]).astype(CDT)
            dpw_ref[g] += _dot_tn(dc, dy_ext[:tm])
            dd = _dot_nt(dy_ext, pw_ref[g])
            r = dd / jnp.minimum(pos_ext + 1, w).astype(F32)
            span = 1
            while span < w:
                r = r + pltpu.roll(r, ext_rows - span, axis=0)
                span *= 2
            dus.append(r[:tm] - dd[:tm])
        du_ref[...] = jnp.concatenate(dus, axis=1).astype(du_ref.dtype)
        dsc_ref[...] += jnp.concatenate(dscs, axis=1)

    row = pl.BlockSpec((tm, POOL_DIM), lambda i: (i, 0))
    prev = pl.BlockSpec((POOL_WMAX, POOL_DIM), lambda i: (jnp.maximum(i * hb - 1, 0), 0))
    nxt = pl.BlockSpec((POOL_WMAX, POOL_DIM), lambda i: (jnp.minimum((i + 1) * hb, nsteps * hb - 1), 0))
    return pl.pallas_call(
        body, name=name, grid=(nsteps,),
        in_specs=[row, prev, row, nxt, pl.BlockSpec(pool_w.shape, lambda i: (0, 0, 0)),
                  pl.BlockSpec((1, POOL_DIM), lambda i: (0, 0))],
        out_specs=[row, pl.BlockSpec(pool_w.shape, lambda i: (0, 0, 0)), pl.BlockSpec((1, POOL_DIM), lambda i: (0, 0))],
        out_shape=[jax.ShapeDtypeStruct((T, POOL_DIM), CDT), jax.ShapeDtypeStruct(pool_w.shape, F32),
                   jax.ShapeDtypeStruct((1, POOL_DIM), F32)],
        compiler_params=_params("arbitrary"),
    )(zu, zu, dpm, dpm, pool_w, scale.reshape(1, POOL_DIM))


def _rope_tables(T):
    pos = jnp.arange(T, dtype=F32)
    inv_freq = ROPE_THETA ** (-jnp.arange(0, ROT_DIM, 2, dtype=F32) / ROT_DIM)
    ang = pos[:, None] * inv_freq[None, :]
    cos, sin = jnp.cos(ang), jnp.sin(ang)
    rest = HEAD_DIM - ROT_DIM
    cos_h = jnp.concatenate([cos, cos, jnp.ones((T, rest), F32)], axis=1)
    sin_h = jnp.concatenate([-sin, sin, jnp.zeros((T, rest), F32)], axis=1)
    return jnp.tile(cos_h, (1, 2)), jnp.tile(sin_h, (1, 2))


def _lane_masks():
    lane = lax.broadcasted_iota(jnp.int32, (1, LANES), 1)
    in_head = lane % HEAD_DIM
    return lane < HEAD_DIM, in_head < ROT_DIM // 2


def _rope_partner(v, low):
    lane = lax.broadcasted_iota(jnp.int32, (1, LANES), 1)
    swapped = jnp.where(low, pltpu.roll(v, LANES - ROT_DIM // 2, axis=1), pltpu.roll(v, ROT_DIM // 2, axis=1))
    return jnp.where(lane % HEAD_DIM < ROT_DIM, swapped, 0.0)


def _head_mean(v, first):
    lo = jnp.sum(jnp.where(first, v, 0.0), axis=-1, keepdims=True)
    hi = jnp.sum(jnp.where(first, 0.0, v), axis=-1, keepdims=True)
    return jnp.where(first, lo, hi) * (1.0 / HEAD_DIM)


def _qk_bwd(dq, dk_parts, dv_parts, zqk, gqk, cos_t, sin_t, name):
    T = zqk.shape[0]
    tm = _tile(T, 512, BLOCK)
    nsteps = T // tm
    nq = ATTN_DIM // LANES

    def shifted_sum(cur_ref, prev_ref, next_ref, last):
        nxt = jnp.where(last, 0.0, next_ref[...])
        return cur_ref[...] + jnp.concatenate([prev_ref[BLOCK:, :], nxt], axis=0)

    def body(dq_ref, kc_ref, kp_ref, kn_ref, vc_ref, vp_ref, vn_ref, z_ref, g_ref, c_ref, s_ref, dz_ref, dv_ref, dg_ref):
        i = pl.program_id(0)

        @pl.when(i == 0)
        def _():
            dg_ref[...] = jnp.zeros_like(dg_ref)

        last = i == nsteps - 1
        dk = shifted_sum(kc_ref, kp_ref, kn_ref, last)
        dv_ref[...] = shifted_sum(vc_ref, vp_ref, vn_ref, last).astype(dv_ref.dtype)
        first, low = _lane_masks()
        cosv, sinv = c_ref[...], s_ref[...]
        dgs = []
        for c in range(QK_DIM // LANES):
            sl = slice(c * LANES, (c + 1) * LANES)
            dout = dq_ref[:, sl] if c < nq else dk
            dxn = dout * cosv + _rope_partner(dout * sinv, low)
            xv = z_ref[:, sl]
            r = lax.rsqrt(_head_mean(xv * xv, first) + EPS)
            xh = xv * r
            dgs.append(jnp.sum(dxn * xh, axis=0, keepdims=True))
            dxh = dxn * g_ref[:, sl]
            dz_ref[:, sl] = (r * (dxh - xh * _head_mean(dxh * xh, first))).astype(dz_ref.dtype)
        dg_ref[...] += jnp.concatenate(dgs, axis=1)

    row = pl.BlockSpec((tm, QK_DIM), lambda i: (i, 0))
    tab = pl.BlockSpec((tm, LANES), lambda i: (i, 0))
    vec = pl.BlockSpec((1, QK_DIM), lambda i: (0, 0))
    nxt = pl.BlockSpec((BLOCK, LANES), lambda i: (jnp.minimum((i + 1) * (tm // BLOCK), T // BLOCK - 1), 0))
    kv = [tab, tab, nxt]
    return pl.pallas_call(
        body, name=name, grid=(nsteps,),
        in_specs=[pl.BlockSpec((tm, ATTN_DIM), lambda i: (i, 0))] + kv + kv + [row, vec, tab, tab],
        out_specs=[row, tab, vec],
        out_shape=[jax.ShapeDtypeStruct((T, QK_DIM), CDT), jax.ShapeDtypeStruct((T, KV_DIM), CDT),
                   jax.ShapeDtypeStruct((1, QK_DIM), F32)],
        compiler_params=_params("arbitrary"),
    )(dq, dk_parts[0], dk_parts[1], dk_parts[1], dv_parts[0], dv_parts[1], dv_parts[1], zqk, gqk, cos_t, sin_t)


def _dup_half(v, first, kv):
    swapped = pltpu.roll(v, HEAD_DIM, axis=1)
    return jnp.where(first, v, swapped) if kv == 0 else jnp.where(first, swapped, v)


HEADS_PER_KV = 4
HEAD_STACK_FWD = 1
HEAD_STACK_BWD = 2


def _attn_bias(stack):
    qi = lax.broadcasted_iota(jnp.int32, (stack * BLOCK, 2 * BLOCK), 0) % BLOCK
    ki = lax.broadcasted_iota(jnp.int32, (stack * BLOCK, 2 * BLOCK), 1)
    diff = qi + BLOCK - ki
    band = (diff >= 0) & (diff < BLOCK)
    return jnp.stack([jnp.where(band, 0.0, -jnp.inf), jnp.where(band & (ki >= BLOCK), 0.0, -jnp.inf)]).astype(F32)


def _attn_blocks(T):
    return _tile(T // BLOCK, 4, 1)


def _stack_heads(ref, rows, kv, heads, first):
    parts = []
    for h in heads:
        c = 2 * kv + h // 2
        v = ref[rows, c * LANES:(c + 1) * LANES].astype(CDT)
        zero = jnp.zeros_like(v)
        parts.append(jnp.where(first, v, zero) if h % 2 == 0 else jnp.where(first, zero, v))
    return parts[0] if len(parts) == 1 else jnp.concatenate(parts, axis=0)


def _row_blocks(v, n):
    return [v[b * BLOCK:(b + 1) * BLOCK] for b in range(n)]


def _sink_column(sink_ref, kv, heads):
    cols = [jnp.full((BLOCK, 1), sink_ref[HEADS_PER_KV * kv + h], F32) for h in heads]
    return cols[0] if len(cols) == 1 else jnp.concatenate(cols, axis=0)


def _head_groups(stack):
    return [tuple(range(g, g + stack)) for g in range(0, HEADS_PER_KV, stack)]


def _softmax_with_sink(qst, kdup, sinkcol, bias):
    s = _dot_nt(qst, kdup) * ATTN_SCALE + bias
    m = jnp.maximum(jnp.max(s, axis=-1, keepdims=True), sinkcol)
    pu = jnp.exp(s - m)
    denom = jnp.sum(pu, axis=-1, keepdims=True) + jnp.exp(sinkcol - m)
    return pu * (1.0 / denom), m + jnp.log(denom)


def _attn_fwd(qkn, zv, sinks, name, side=None):
    T = qkn.shape[0]
    R = _attn_blocks(T)
    tq = R * BLOCK

    def body(sink_ref, bias_ref, qk_ref, qkp_ref, v_ref, vp_ref, o_ref, lse_ref):
        i = pl.program_id(0)
        first, _ = _lane_masks()
        lane = lax.broadcasted_iota(jnp.int32, (1, LANES), 1)
        kall = jnp.concatenate([qkp_ref[:, ATTN_DIM:], qk_ref[:, ATTN_DIM:]], axis=0)
        vall = jnp.concatenate([vp_ref[...], v_ref[...]], axis=0).astype(CDT)
        for r in range(R):
            bias = bias_ref[jnp.where(i == 0, 1, 0)] if r == 0 else bias_ref[0]
            rows = slice(r * BLOCK, (r + 2) * BLOCK)
            qrows = slice(r * BLOCK, (r + 1) * BLOCK)
            lse_rows = jnp.zeros((BLOCK, LANES), F32)
            for kv in range(2):
                kdup = _dup_half(kall[rows], first, kv)
                vdup = _dup_half(vall[rows], first, kv)
                res = []
                for heads in _head_groups(HEAD_STACK_FWD):
                    p, lse = _softmax_with_sink(_stack_heads(qk_ref, qrows, kv, heads, first), kdup,
                                                _sink_column(sink_ref, kv, heads), bias)
                    res += _row_blocks(_dot(p.astype(CDT), vdup), len(heads))
                    for b, col in enumerate(_row_blocks(lse, len(heads))):
                        lse_rows = jnp.where(lane == HEADS_PER_KV * kv + heads[b], col, lse_rows)
                o_ref[qrows, 2 * kv * LANES:(2 * kv + 1) * LANES] = jnp.where(first, res[0], res[1]).astype(o_ref.dtype)
                o_ref[qrows, (2 * kv + 1) * LANES:(2 * kv + 2) * LANES] = jnp.where(first, res[2], res[3]).astype(o_ref.dtype)
            lse_ref[qrows, :] = lse_rows

    bias = _attn_bias(HEAD_STACK_FWD)
    prev = lambda i: (jnp.maximum(i * R - 1, 0), 0)
    return _pcall(
        body, name, (T // tq,),
        [pl.BlockSpec(memory_space=pltpu.SMEM), pl.BlockSpec(bias.shape, lambda i: (0, 0, 0)),
         pl.BlockSpec((tq, QK_DIM), lambda i: (i, 0)), pl.BlockSpec((BLOCK, QK_DIM), prev),
         pl.BlockSpec((tq, KV_DIM), lambda i: (i, 0)), pl.BlockSpec((BLOCK, KV_DIM), prev)],
        [pl.BlockSpec((tq, ATTN_DIM), lambda i: (i, 0)), pl.BlockSpec((tq, LANES), lambda i: (i, 0))],
        [jax.ShapeDtypeStruct((T, ATTN_DIM), CDT), jax.ShapeDtypeStruct((T, LANES), F32)],
        (sinks, bias, qkn, qkn, zv, zv), ("parallel",), side)


def _attn_bwd(qkn, zv, sinks, do, o, lse, name, side=None):
    T = qkn.shape[0]
    R = _attn_blocks(T)
    tq = R * BLOCK

    def body(sink_ref, bias_ref, qk_ref, qkp_ref, v_ref, vp_ref, do_ref, o_ref, lse_ref,
             dq_ref, dkc_ref, dkp_ref, dvc_ref, dvp_ref, ds_ref):
        i = pl.program_id(0)

        @pl.when(i == 0)
        def _():
            ds_ref[...] = jnp.zeros_like(ds_ref)

        first, _ = _lane_masks()
        lane = lax.broadcasted_iota(jnp.int32, (1, LANES), 1)
        kall = jnp.concatenate([qkp_ref[:, ATTN_DIM:], qk_ref[:, ATTN_DIM:]], axis=0)
        vall = jnp.concatenate([vp_ref[...], v_ref[...]], axis=0).astype(CDT)
        for r in range(R):
            bias = bias_ref[jnp.where(i == 0, 1, 0)] if r == 0 else bias_ref[0]
            rows = slice(r * BLOCK, (r + 2) * BLOCK)
            qrows = slice(r * BLOCK, (r + 1) * BLOCK)
            dk_out, dv_out = [], []
            lse_rows = lse_ref[qrows, :]
            for kv in range(2):
                kdup = _dup_half(kall[rows], first, kv)
                vdup = _dup_half(vall[rows], first, kv)
                dq_h = []
                dk_acc = jnp.zeros((2 * BLOCK, LANES), F32)
                dv_acc = jnp.zeros((2 * BLOCK, LANES), F32)
                for heads in _head_groups(HEAD_STACK_BWD):
                    qst = _stack_heads(qk_ref, qrows, kv, heads, first)
                    dost = _stack_heads(do_ref, qrows, kv, heads, first)
                    lse_cols, delta_cols = [], []
                    for h in heads:
                        cols = slice((2 * kv + h // 2) * LANES, (2 * kv + h // 2 + 1) * LANES)
                        prod = do_ref[qrows, cols].astype(F32) * o_ref[qrows, cols].astype(F32)
                        own = first if h % 2 == 0 else jnp.logical_not(first)
                        delta_cols.append(jnp.sum(jnp.where(own, prod, 0.0), axis=-1, keepdims=True))
                        lse_cols.append(jnp.sum(jnp.where(lane == HEADS_PER_KV * kv + h, lse_rows, 0.0), axis=-1, keepdims=True))
                    lse_col = lse_cols[0] if len(heads) == 1 else jnp.concatenate(lse_cols, axis=0)
                    delta = delta_cols[0] if len(heads) == 1 else jnp.concatenate(delta_cols, axis=0)
                    p = jnp.exp(_dot_nt(qst, kdup) * ATTN_SCALE + bias - lse_col)
                    dsc = (p * (_dot_nt(dost, vdup) - delta)).astype(CDT)
                    psink = jnp.exp(_sink_column(sink_ref, kv, heads) - lse_col)
                    for b, term in enumerate(_row_blocks(psink * delta, len(heads))):
                        row = HEADS_PER_KV * kv + heads[b]
                        ds_ref[row:row + 1, :] += jnp.sum(term, axis=0, keepdims=True)
                    dq_h += _row_blocks(_dot(dsc, kdup) * ATTN_SCALE, len(heads))
                    dk_acc = dk_acc + _dot_tn(dsc, qst) * ATTN_SCALE
                    dv_acc = dv_acc + _dot_tn(p.astype(CDT), dost)
                dq_ref[qrows, 2 * kv * LANES:(2 * kv + 1) * LANES] = jnp.where(first, dq_h[0], dq_h[1])
                dq_ref[qrows, (2 * kv + 1) * LANES:(2 * kv + 2) * LANES] = jnp.where(first, dq_h[2], dq_h[3])
                dk_out.append(dk_acc + pltpu.roll(dk_acc, HEAD_DIM, axis=1))
                dv_out.append(dv_acc + pltpu.roll(dv_acc, HEAD_DIM, axis=1))
            dk = jnp.where(first, dk_out[0], dk_out[1])
            dv = jnp.where(first, dv_out[0], dv_out[1])
            dkp_ref[qrows, :] = dk[:BLOCK]
            dkc_ref[qrows, :] = dk[BLOCK:]
            dvp_ref[qrows, :] = dv[:BLOCK]
            dvc_ref[qrows, :] = dv[BLOCK:]

    bias = _attn_bias(HEAD_STACK_BWD)
    prev = lambda i: (jnp.maximum(i * R - 1, 0), 0)
    kvrow = pl.BlockSpec((tq, KV_DIM), lambda i: (i, 0))
    qrow = pl.BlockSpec((tq, ATTN_DIM), lambda i: (i, 0))
    kv_shape = jax.ShapeDtypeStruct((T, KV_DIM), F32)
    return _pcall(
        body, name, (T // tq,),
        [pl.BlockSpec(memory_space=pltpu.SMEM), pl.BlockSpec(bias.shape, lambda i: (0, 0, 0)),
         pl.BlockSpec((tq, QK_DIM), lambda i: (i, 0)), pl.BlockSpec((BLOCK, QK_DIM), prev),
         kvrow, pl.BlockSpec((BLOCK, KV_DIM), prev), qrow, qrow, kvrow],
        [qrow, kvrow, kvrow, kvrow, kvrow, pl.BlockSpec((N_Q_HEADS, LANES), lambda i: (0, 0))],
        [jax.ShapeDtypeStruct((T, ATTN_DIM), F32), kv_shape, kv_shape, kv_shape, kv_shape,
         jax.ShapeDtypeStruct((N_Q_HEADS, LANES), F32)],
        (sinks, bias, qkn, qkn, zv, zv, do, o, lse), ("arbitrary",), side)


def _merge_fwd(pm, o, w_pb, w_ab, zg, name, side=None):
    T = pm.shape[0]
    tm = _tile(T, 512)

    def body(pm_ref, o_ref, wp_ref, wa_ref, zg_ref, m_ref, gp_ref, ga_ref, fp_ref, fa_ref):
        pmv, ov = pm_ref[...], o_ref[...]
        a = jnp.concatenate([_dot(pmv, wp_ref[j]) for j in range(N_CHIPS)], axis=1)
        b = jnp.concatenate([_dot(ov, wa_ref[j]) for j in range(N_CHIPS)], axis=1)
        gp = _sigmoid(zg_ref[:, :D_MODEL].astype(F32))
        ga = _sigmoid(zg_ref[:, D_MODEL:].astype(F32))
        ap, ba = gp * a, ga * b
        m_ref[...] = (ap + ba).astype(m_ref.dtype)
        gp_ref[...] = gp.astype(gp_ref.dtype)
        ga_ref[...] = ga.astype(ga_ref.dtype)
        fp_ref[...] = (ap * (1.0 - gp)).astype(fp_ref.dtype)
        fa_ref[...] = (ba * (1.0 - ga)).astype(fa_ref.dtype)

    half = pl.BlockSpec((tm, POOL_DIM), lambda i: (i, 0))
    full = pl.BlockSpec((tm, D_MODEL), lambda i: (i, 0))
    wspec = pl.BlockSpec(w_pb.shape, lambda i: (0, 0, 0))
    out = jax.ShapeDtypeStruct((T, D_MODEL), CDT)
    return _pcall(body, name, (T // tm,), [half, half, wspec, wspec, pl.BlockSpec((tm, GATE_DIM), lambda i: (i, 0))],
                  [full] * 5, [out] * 5, (pm, o, w_pb, w_ab, zg), ("parallel",), side)


def _merge_bwd(dxo, w_out, factors, w_pb, w_ab, name):
    T = dxo.shape[0]
    tm = _tile(T, 512)
    kb = w_pb.shape[2]

    def branch_dx(dv, b_ref):
        acc = _dot_nt(dv[:, :kb], b_ref[0])
        for j in range(1, N_CHIPS):
            acc = acc + _dot_nt(dv[:, j * kb:(j + 1) * kb], b_ref[j])
        return acc

    def body(dx_ref, w_ref, gp_ref, ga_ref, fp_ref, fa_ref, wp_ref, wa_ref, da_ref, db_ref, dg_ref, dpm_ref, do_ref):
        dm = _dot_nt(dx_ref[...].astype(CDT), w_ref[...])
        da = (dm * gp_ref[...].astype(F32)).astype(da_ref.dtype)
        db = (dm * ga_ref[...].astype(F32)).astype(db_ref.dtype)
        da_ref[...] = da
        db_ref[...] = db
        dg_ref[:, :D_MODEL] = (dm * fp_ref[...].astype(F32)).astype(dg_ref.dtype)
        dg_ref[:, D_MODEL:] = (dm * fa_ref[...].astype(F32)).astype(dg_ref.dtype)
        dpm_ref[...] = branch_dx(da, wp_ref).astype(dpm_ref.dtype)
        do_ref[...] = branch_dx(db, wa_ref).astype(do_ref.dtype)

    full = pl.BlockSpec((tm, D_MODEL), lambda i: (i, 0))
    half = pl.BlockSpec((tm, POOL_DIM), lambda i: (i, 0))
    gate = pl.BlockSpec((tm, GATE_DIM), lambda i: (i, 0))
    wspec = pl.BlockSpec(w_pb.shape, lambda i: (0, 0, 0))
    out = jax.ShapeDtypeStruct((T, D_MODEL), CDT)
    out_half = jax.ShapeDtypeStruct((T, POOL_DIM), CDT)
    return pl.pallas_call(
        body, name=name, grid=(T // tm,),
        in_specs=[full, pl.BlockSpec((D_MODEL, D_MODEL), lambda i: (0, 0))] + [full] * 4 + [wspec, wspec],
        out_specs=[full, full, gate, half, half],
        out_shape=[out, out, jax.ShapeDtypeStruct((T, GATE_DIM), CDT), out_half, out_half],
        compiler_params=_params("parallel"),
    )(dxo, w_out, *factors, w_pb, w_ab)


def _adamw(w, g, m, v, name):
    Rr, C = w.shape
    tr = _tile(Rr, max(8, (1 << 19) // C // 8 * 8))

    def body(w_ref, g_ref, m_ref, v_ref, go_ref, d_ref, nm_ref, nv_ref):
        gv = g_ref[...]
        go_ref[...] = gv
        nm = ADAM_B1 * m_ref[...] + (1.0 - ADAM_B1) * gv
        nv = ADAM_B2 * v_ref[...] + (1.0 - ADAM_B2) * (gv * gv)
        m_hat = nm / (1.0 - ADAM_B1 ** ADAM_STEP)
        v_hat = nv / (1.0 - ADAM_B2 ** ADAM_STEP)
        d_ref[...] = -ADAM_LR * (m_hat / (jnp.sqrt(v_hat) + ADAM_EPS) + ADAM_WD * w_ref[...])
        nm_ref[...] = nm
        nv_ref[...] = nv

    blk = pl.BlockSpec((tr, C), lambda i: (i, 0))
    out = jax.ShapeDtypeStruct((Rr, C), F32)
    return pl.pallas_call(
        body, name=name, grid=(Rr // tr,), in_specs=[blk] * 4, out_specs=[blk] * 4, out_shape=[out] * 4,
        compiler_params=_params("parallel"),
    )(w, g, m, v)


def _place():
    return lax.axis_index("x"), lax.axis_index("y"), lax.axis_index("c")


def _other_chip(x, y, d):
    return (1 - x if d & 2 else x), (1 - y if d & 1 else y)


def _rcopy(src, dst, ssem, rsem, dev):
    return pltpu.make_async_remote_copy(src_ref=src, dst_ref=dst, send_sem=ssem, recv_sem=rsem, device_id=dev,
                                        device_id_type=MESH)


def _row_half(rows, c):
    return pl.ds(c * (rows // 2), rows // 2)


def _is_wide(name):
    return name in WIDE


def _block(ref, wide, j, rows, n):
    if wide:
        return ref.at[rows, pl.ds(pl.multiple_of(j * n, LANES), n)]
    return ref.at[j, rows]


def _gathered_shape(shard, wide):
    _, a, n = shard.shape
    return jax.ShapeDtypeStruct((a, N_CHIPS * n) if wide else (N_CHIPS, a, n), shard.dtype)


def _gather_ici_side(shards, wides, l):
    k_of = lambda w, d: 3 * w + d - 1

    def issue(ins, outs, ssem, rsem):
        x, y, c = _place()
        cps = []
        for w, (shard, wide) in enumerate(zip(shards, wides)):
            _, a, n = shard.shape
            half = _row_half(a, c)
            for d in (1, 2, 3):
                px, py = _other_chip(x, y, d)
                cps.append(_rcopy(ins[w].at[l, half], _block(outs[w], wide, 2 * x + y, half, n),
                                  ssem.at[k_of(w, d)], rsem.at[k_of(w, d)], (px, py, c)))
        return cps

    return _Side(shards, [_gathered_shape(s_, wd) for s_, wd in zip(shards, wides)], 3 * len(shards), issue)


def _gather_d2d_side(shards, wides, gathered, l):
    nw = len(shards)

    def issue(ins, outs, ssem, rsem):
        x, y, c = _place()
        sibling = (x, y, 1 - c)
        cps = []
        for w, (shard, wide) in enumerate(zip(shards, wides)):
            _, a, n = shard.shape
            half = _row_half(a, c)
            for d in (1, 2, 3):
                px, py = _other_chip(x, y, d)
                k = 3 * w + d - 1
                got = _block(outs[w], wide, 2 * px + py, half, n)
                cps.append(_rcopy(got, got, ssem.at[k], rsem.at[k], sibling))
            cps.append(_rcopy(ins[nw + w].at[l], _block(outs[w], wide, 2 * x + y, pl.ds(0, a), n),
                              ssem.at[3 * nw + w], rsem.at[3 * nw + w], sibling))
        return cps

    return _Side(list(gathered) + list(shards), [jax.ShapeDtypeStruct(g.shape, g.dtype) for g in gathered], 4 * nw, issue,
                 aliases={w: w for w in range(nw)})


def _half_shape(g, wide):
    if wide:
        return jax.ShapeDtypeStruct((g.shape[0] // 2, g.shape[1]), g.dtype)
    return jax.ShapeDtypeStruct((N_CHIPS, g.shape[1] // 2, g.shape[2]), g.dtype)


def _reduce_sibling_side(gms, wides):
    def issue(ins, outs, ssem, rsem):
        x, y, c = _place()
        cps = []
        for w, (g, wide) in enumerate(zip(gms, wides)):
            src = ins[w].at[_row_half(g.shape[0], 1 - c)] if wide else ins[w].at[:, _row_half(g.shape[1], 1 - c)]
            cps.append(_rcopy(src, outs[w], ssem.at[w], rsem.at[w], (x, y, 1 - c)))
        return cps

    return _Side(gms, [_half_shape(g, wd) for g, wd in zip(gms, wides)], len(gms), issue)


def _reduce_chip_side(ps, wides):
    def slot_shape(p, wide):
        return jax.ShapeDtypeStruct((N_CHIPS, p.shape[0], p.shape[1] // N_CHIPS) if wide else p.shape, p.dtype)

    def issue(ins, outs, ssem, rsem):
        x, y, c = _place()
        cps = []
        for w, (p, wide) in enumerate(zip(ps, wides)):
            ah, n = (p.shape[0], p.shape[1] // N_CHIPS) if wide else p.shape[1:]
            for d in (1, 2, 3):
                px, py = _other_chip(x, y, d)
                k = 3 * w + d - 1
                cps.append(_rcopy(_block(ins[w], wide, 2 * px + py, pl.ds(0, ah), n), outs[w].at[2 * x + y],
                                  ssem.at[k], rsem.at[k], (px, py, c)))
        return cps

    return _Side(ps, [slot_shape(p, wd) for p, wd in zip(ps, wides)], 3 * len(ps), issue)


def _share_side(accs, items):
    def issue(ins, outs, ssem, rsem):
        x, y, c = _place()
        cps = []
        for k, (w, layer) in enumerate(items):
            mine = outs[w].at[layer, _row_half(accs[w].shape[1], c)]
            cps.append(_rcopy(mine, mine, ssem.at[k], rsem.at[k], (x, y, 1 - c)))
        return cps

    return _Side(accs, [jax.ShapeDtypeStruct(a.shape, a.dtype) for a in accs], len(items), issue,
                 aliases={w: w for w in range(len(accs))})


def _sum_rows(rows, b):
    return _tile(rows, max(16, (1 << 19) // b // 16 * 16), 16)


def _pair_sum(g, recv, wide, place, name):
    ah, b = recv.shape[-2:]
    ta = _sum_rows(ah, b)
    nr = ah // ta

    def body(p_ref, g_ref, r_ref, o_ref):
        o_ref[...] = (g_ref[...].astype(F32) + r_ref[...].astype(F32)).astype(o_ref.dtype)

    if wide:
        grid = (nr,)
        specs = [pl.BlockSpec((ta, b), lambda r, p: (p[0] * nr + r, 0)), pl.BlockSpec((ta, b), lambda r, p: (r, 0))]
        out_spec = pl.BlockSpec((ta, b), lambda r, p: (r, 0))
    else:
        grid = (N_CHIPS, nr)
        specs = [pl.BlockSpec((None, ta, b), lambda j, r, p: (j, p[0] * nr + r, 0)),
                 pl.BlockSpec((None, ta, b), lambda j, r, p: (j, r, 0))]
        out_spec = pl.BlockSpec((None, ta, b), lambda j, r, p: (j, r, 0))
    return pl.pallas_call(
        body, name=name,
        grid_spec=pltpu.PrefetchScalarGridSpec(num_scalar_prefetch=1, grid=grid, in_specs=specs, out_specs=out_spec),
        out_shape=jax.ShapeDtypeStruct(recv.shape, recv.dtype), compiler_params=_params(*["parallel"] * len(grid)),
    )(place, g, recv)


def _chip_sum(slots, part, wide, place, acc, l, name):
    _, ah, b = slots.shape
    ta = _sum_rows(ah, b)
    nr = ah // ta

    def body(p_ref, s_ref, own_ref, acc_ref, o_ref):
        j = p_ref[1]
        own = own_ref[...].astype(F32)
        term = [jnp.where(j == s_, own, s_ref[s_].astype(F32)) for s_ in range(N_CHIPS)]
        o_ref[...] = ((term[0] + term[1]) + term[2]) + term[3]

    own_spec = (pl.BlockSpec((ta, b), lambda r, p: (r, p[1])) if wide else
                pl.BlockSpec((None, ta, b), lambda r, p: (p[1], r, 0)))
    return pl.pallas_call(
        body, name=name,
        grid_spec=pltpu.PrefetchScalarGridSpec(
            num_scalar_prefetch=1, grid=(nr,),
            in_specs=[pl.BlockSpec((N_CHIPS, ta, b), lambda r, p: (0, r, 0)), own_spec, ANY],
            out_specs=pl.BlockSpec((None, ta, b), lambda r, p: (l, p[0] * nr + r, 0))),
        out_shape=jax.ShapeDtypeStruct(acc.shape, F32), input_output_aliases={3: 0},
        compiler_params=_params("parallel"),
    )(place, slots, part, acc)


def _small_side(v):
    def issue(ins, outs, ssem, rsem):
        x, y, c = _place()
        cps = []
        for d in range(1, N_DEV):
            px, py = _other_chip(x, y, d >> 1)
            pc = 1 - c if d & 1 else c
            cps.append(_rcopy(ins[0], outs[0].at[4 * x + 2 * y + c], ssem.at[d - 1], rsem.at[d - 1], (px, py, pc)))
        return cps

    return _Side([v], [jax.ShapeDtypeStruct((N_DEV,) + v.shape, v.dtype)], N_DEV - 1, issue)


def _small_sum(slots, v, place, name):
    def body(p_ref, s_ref, v_ref, o_ref):
        me = 2 * p_ref[1] + p_ref[0]
        acc = jnp.where(me == 0, v_ref[...], s_ref[0])
        for s_ in range(1, N_DEV):
            acc = acc + jnp.where(me == s_, v_ref[...], s_ref[s_])
        o_ref[...] = acc

    return pl.pallas_call(
        body, name=name,
        grid_spec=pltpu.PrefetchScalarGridSpec(
            num_scalar_prefetch=1, grid=(1,),
            in_specs=[pl.BlockSpec(slots.shape, lambda i, p: (0, 0, 0)), pl.BlockSpec(v.shape, lambda i, p: (0, 0))],
            out_specs=pl.BlockSpec(v.shape, lambda i, p: (0, 0))),
        out_shape=jax.ShapeDtypeStruct(v.shape, F32), compiler_params=_params("arbitrary"),
    )(place, slots, v)


def _ffn_forward(x, p, tag, side_of):
    h, gu, act = _ffn_up(x, p[f"ln_{tag}"], p[f"w_{tag}_gu"], f"{tag}_up", side_of(f"{tag}_up"))
    x_out = _mm_nn(act, p[f"w_{tag}_down"], f"{tag}_down", F32, res=x, scale=0.5, side=side_of(f"{tag}_down"))
    return x_out, (x, h, gu, act)


def _row_blocks_of(dw):
    return dw.reshape(N_CHIPS, dw.shape[0] // N_CHIPS, dw.shape[1])


def _ffn_backward(dxo, saved, p, tag, side_of, grad):
    x, h, gu, act = saved
    dgu = _ffn_down_bwd(dxo, p[f"w_{tag}_down"], gu, f"{tag}_down_bwd", side_of(f"{tag}_down_bwd"))
    grad(f"w_{tag}_down", _row_blocks_of(_mm_tn(act, dxo, f"{tag}_dwd", scale=0.5, side=side_of(f"{tag}_dwd"))))
    grad(f"w_{tag}_gu", _mm_tn(h, dgu, f"{tag}_dwgu", tn_target=2816, tm_target=1024, side=side_of(f"{tag}_dwgu")))
    dx, d_ln = _mm_nt_norm_bwd([dgu], p[f"w_{tag}_gu"], x, p[f"ln_{tag}"], dxo, f"{tag}_dh_norm_bwd",
                               side_of(f"{tag}_dh_norm_bwd"))
    grad(f"ln_{tag}", d_ln[0])
    return dx


def _mixer_forward(x, p, tabs, side_of):
    h, zu, zqk, zv, zg, qkn, pm = _mm_in(x, p["ln_mix"], p["w_in"], p["gqk"], *tabs, p["pool_w"], p["pool_scale"],
                                         "mix_in", side_of("mix_in"))
    o, lse = _attn_fwd(qkn, zv, p["sinks"], "attn_fwd", side_of("attn_fwd"))
    m, *factors = _merge_fwd(pm, o, p["w_pool_branch"], p["w_attn_branch"], zg, "merge_fwd", side_of("merge_fwd"))
    x_out = _mm_nn(m, p["w_out"], "mix_out", F32, res=x, scale=1.0)
    return x_out, (x, h, zu, zqk, zv, pm, qkn, o, lse, factors, m)


def _mixer_backward(dxo, saved, p, tabs, side_of, grad):
    x, h, zu, zqk, zv, pm, qkn, o, lse, factors, m = saved
    d_a, d_b, dgl, dpm, do = _merge_bwd(dxo, p["w_out"], factors, p["w_pool_branch"], p["w_attn_branch"], "merge_bwd")
    grad("w_out", _row_blocks_of(_mm_tn(m, dxo, "mix_dwout")))
    grad("w_pool_branch", _mm_tn(pm, d_a, "pool_branch_dw", col_blocks=N_CHIPS))
    grad("w_attn_branch", _mm_tn(o, d_b, "attn_branch_dw", col_blocks=N_CHIPS))
    du, d_pool_w, d_pool_scale = _pool_bwd(zu, dpm, p["pool_w"], p["pool_scale"], "pool_bwd")
    grad("pool_w", d_pool_w)
    grad("pool_scale", d_pool_scale)
    dq, dkc, dkp, dvc, dvp, dsink = _attn_bwd(qkn, zv, p["sinks"], do, o, lse, "attn_bwd", side_of("attn_bwd"))
    dzqk, dv, dgqk = _qk_bwd(dq, (dkc, dkp), (dvc, dvp), zqk, p["gqk"], *tabs, "qk_bwd")
    grad("q_norm", dgqk[0, :ATTN_DIM].reshape(N_Q_HEADS, HEAD_DIM).sum(axis=0))
    grad("k_norm", dgqk[0, ATTN_DIM:].reshape(KV_DIM // HEAD_DIM, HEAD_DIM).sum(axis=0))
    grad("sinks", -dsink[:, 0])
    dz = [du, dzqk, dv, dgl]
    grad("w_in", _blocks_from_full("w_in", _mm_tn_parts(h, dz, "mix_dwin")).astype(WIRE_DT))
    dx, d_ln = _mm_nt_norm_bwd(dz, p["w_in"], x, p["ln_mix"], dxo, "mix_dh_norm_bwd", side_of("mix_dh_norm_bwd"))
    grad("ln_mix", d_ln[0])
    return dx


class _NoComm:
    def __init__(self, layers):
        self.layers, self.grads = layers, [dict() for _ in layers]

    def weight(self, l, name):
        return self.layers[l][name]

    def side(self, phase, l, host):
        return None

    def grad(self, l, name, value):
        self.grads[l][name] = value


class _Layer:
    def __init__(self, hooks, l):
        self.hooks, self.l, self.got = hooks, l, {}

    def __getitem__(self, name):
        if name not in self.got:
            self.got[name] = self.hooks.weight(self.l, name)
        return self.got[name]


def _local_step(x, tgt, n_layers, hooks):
    T = x.shape[0]
    tabs = _rope_tables(T)
    saved, params = [], []
    for l in range(n_layers):
        p = _Layer(hooks, l)
        side_of = functools.partial(hooks.side, "fwd", l)
        x, s1 = _ffn_forward(x, p, "ffn1", side_of)
        x, s2 = _mixer_forward(x, p, tabs, side_of)
        x, s3 = _ffn_forward(x, p, "ffn2", side_of)
        saved.append((s1, s2, s3))
        params.append(p)
    dx, loss = _loss_head(x, tgt, "loss_head")
    for l in reversed(range(n_layers)):
        p = params[l]
        s1, s2, s3 = saved[l]
        side_of = functools.partial(hooks.side, "bwd", l)
        grad = functools.partial(hooks.grad, l)
        dx = _ffn_backward(dx, s3, p, "ffn2", side_of, grad)
        dx = _mixer_backward(dx, s2, p, tabs, side_of, grad)
        dx = _ffn_backward(dx, s1, p, "ffn1", side_of, grad)
    return loss, dx


def _full_from_blocks(name, blocks):
    if name in COL_SHARDED:
        return jnp.transpose(blocks, (1, 0, 2)).reshape(blocks.shape[1], N_CHIPS * blocks.shape[2])
    return blocks.reshape(N_CHIPS * blocks.shape[1], blocks.shape[2])


def _blocks_from_full(name, full):
    K, N = full.shape
    if name in COL_SHARDED:
        return jnp.transpose(full.reshape(K, N_CHIPS, N // N_CHIPS), (1, 0, 2))
    return full.reshape(N_CHIPS, K // N_CHIPS, N)


JOBS = {"a": ("w_ffn1_gu", "w_ffn1_down"), "b": ("w_in", "w_pool_branch", "w_attn_branch", "w_out"),
        "c": ("w_ffn2_gu", "w_ffn2_down")}
GATHER_PLAN = {"ffn1_up": ("ici", "b", JOBS["b"], 0), "ffn1_down": ("d2d", "b", JOBS["b"], 0),
               "mix_in": ("ici", "c", JOBS["c"][:1], 0), "attn_fwd": ("ici", "c", JOBS["c"][1:], 0),
               "merge_fwd": ("d2d", "c", JOBS["c"], 0),
               "ffn2_up": ("ici", "a", JOBS["a"], 1), "ffn2_down": ("d2d", "a", JOBS["a"], 1)}
REDUCE_PLAN = {"ffn2_down_bwd": ("sibling", "a", 1), "ffn2_dwgu": ("chip", "a", 1),
               "ffn2_dh_norm_bwd": ("sibling", "c", 0), "attn_bwd": ("chip", "c", 0),
               "mix_dh_norm_bwd": ("sibling", "b", 0), "ffn1_down_bwd": ("chip", "b", 0)}
SHARE_HOST = "ffn1_dwgu"
SMALL_HOST = "ffn2_dwd"
LAST_GRAD = "ln_ffn1"


class _Exchange:
    def __init__(self, shards, small, place, n_layers):
        self.shards, self.small, self.place, self.n_layers = shards, small, place, n_layers
        first, wides = [shards[n] for n in JOBS["a"]], [_is_wide(n) for n in JOBS["a"]]
        got = _run_side(_gather_ici_side(first, wides, 0), "gather_ici")
        got = _run_side(_gather_d2d_side(first, wides, got, 0), "gather_d2d")
        self.blocks = {(n, 0): g for n, g in zip(JOBS["a"], got)}
        self.landed = {}
        self.handed = []
        self.acc = {n: lax.empty(shards[n].shape, F32) for n in BIG}
        self.grads = [dict() for _ in range(n_layers)]
        self.reduce = {}
        self.summed = set()
        self.unshared, self.sharing = [], None
        self.small_sides, self.small_waiting = {}, None

    def weight(self, l, name):
        if name not in BIG:
            return self.small(l)[name]
        for names, layer, done in self.handed:
            self.blocks.update({(n, layer): g for n, g in zip(names, done.outs)})
        self.handed.clear()
        blocks = self.blocks.pop((name, l))
        return blocks if name in USED_AS_BLOCKS + WIDE else _full_from_blocks(name, blocks)

    def _gather_side(self, l, host):
        step, job, names, ahead = GATHER_PLAN[host]
        layer = l + ahead
        if layer >= self.n_layers:
            return None
        if step == "ici":
            side = _gather_ici_side([self.shards[n] for n in names], [_is_wide(n) for n in names], layer)
            self.landed.setdefault((job, layer), []).append((names, side))
            return side
        names, gathered = JOBS[job], {}
        for part_names, side in self.landed.pop((job, layer)):
            gathered.update(zip(part_names, side.outs))
        done = _gather_d2d_side([self.shards[n] for n in names], [_is_wide(n) for n in names],
                                [gathered[n] for n in names], layer)
        self.handed.append((names, layer, done))
        return done

    def grad(self, l, name, value):
        self.grads[l][name] = value
        if name == LAST_GRAD and l > 0:
            packed, self.small_spans = _pack_small([self.grads[l][n] for n in SMALL])
            self.small_sides[l] = _small_side(packed)
            self.small_waiting = l

    def reduced_small(self, loss_part):
        packed, spans = _pack_small([self.grads[0][n] for n in SMALL] + [loss_part])
        self.small_sides[0] = _small_side(packed)
        _run_side(self.small_sides[0], "all_reduce_small")
        shapes = [self.grads[0][n].shape for n in SMALL]
        per_layer = []
        for l in range(self.n_layers):
            side = self.small_sides[l]
            summed = _small_sum(side.outs[0], side.ins[0], self.place, "small_sum")
            per_layer.append(_unpack_small(summed, spans, shapes + [(1, 1)] * (l == 0)))
        loss = per_layer[0][-1][0, 0]
        return {n: jnp.stack([vals[k] for vals in per_layer]) for k, n in enumerate(SMALL)}, loss

    def _reduce_side(self, l, host):
        step, job, ahead = REDUCE_PLAN[host]
        layer = l + ahead
        if layer >= self.n_layers:
            return None
        return self._reduce_step(step, job, layer)

    def _reduce_step(self, step, job, layer):
        if step == "sibling":
            st = self.reduce[(job, layer)] = dict(gm=[self.grads[layer][n] for n in JOBS[job]],
                                                  wide=[_is_wide(n) for n in JOBS[job]])
            st["sibling"] = _reduce_sibling_side(st["gm"], st["wide"])
            return st["sibling"]
        st = self.reduce[(job, layer)]
        st["part"] = [_pair_sum(g, r, wd, self.place, "grad_pair_sum")
                      for g, r, wd in zip(st["gm"], st["sibling"].outs, st["wide"])]
        st["chip"] = _reduce_chip_side(st["part"], st["wide"])
        return st["chip"]

    def _chip_sums(self):
        if self.sharing is not None:
            self.acc.update(zip(BIG, self.sharing.outs))
            self.sharing = None
        for (job, layer), st in self.reduce.items():
            if (job, layer) not in self.summed and "chip" in st and st["chip"].outs is not None:
                self.summed.add((job, layer))
                for n, slots, part, wd in zip(JOBS[job], st["chip"].outs, st["part"], st["wide"]):
                    self.acc[n] = _chip_sum(slots, part, wd, self.place, self.acc[n], layer, "grad_chip_sum")
                    self.unshared.append((BIG.index(n), layer))

    def _share(self):
        side = _share_side([self.acc[n] for n in BIG], self.unshared)
        self.unshared = []
        return side

    def side(self, phase, l, host):
        if phase == "fwd":
            return self._gather_side(l, host) if host in GATHER_PLAN else None
        self._chip_sums()
        if host == SMALL_HOST and self.small_waiting is not None:
            side, self.small_waiting = self.small_sides[self.small_waiting], None
            return side
        if host == SHARE_HOST and self.unshared:
            self.sharing = self._share()
            return self.sharing
        return self._reduce_side(l, host) if host in REDUCE_PLAN else None

    def reduced(self):
        _run_side(self._reduce_step("sibling", "a", 0), "grad_sibling_exchange")
        _run_side(self._reduce_step("chip", "a", 0), "grad_chip_exchange")
        self._chip_sums()
        return dict(zip(BIG, _run_side(self._share(), "grad_sibling_share")))


def _pack_small(parts):
    rows, spans, lo = [], [], 0
    for v in parts:
        flat = v.reshape(-1)
        nrow = -(-flat.shape[0] // LANES)
        flat = jnp.pad(flat, (0, nrow * LANES - flat.shape[0]))
        rows.append(flat.reshape(nrow, LANES))
        spans.append((lo, nrow))
        lo += nrow
    pad = -lo % 8
    if pad:
        rows.append(jnp.zeros((pad, LANES), F32))
    return jnp.concatenate(rows, axis=0), spans


def _unpack_small(packed, spans, shapes):
    out = []
    for (lo, nrow), shape in zip(spans, shapes):
        size = 1
        for s in shape:
            size *= s
        out.append(packed[lo:lo + nrow].reshape(-1)[:size].reshape(shape))
    return out


def kernel(x, ln_ffn1, w_ffn1_gu, w_ffn1_down, ln_mix, w_in, pool_w, pool_scale, w_pool_branch, q_norm, k_norm, sinks, w_attn_branch, w_out, ln_ffn2, w_ffn2_gu, w_ffn2_down, loss_target, m_ln_ffn1, m_w_ffn1_gu, m_w_ffn1_down, m_ln_mix, m_w_in, m_pool_w, m_pool_scale, m_w_pool_branch, m_q_norm, m_k_norm, m_sinks, m_w_attn_branch, m_w_out, m_ln_ffn2, m_w_ffn2_gu, m_w_ffn2_down, v_ln_ffn1, v_w_ffn1_gu, v_w_ffn1_down, v_ln_mix, v_w_in, v_pool_w, v_pool_scale, v_w_pool_branch, v_q_norm, v_k_norm, v_sinks, v_w_attn_branch, v_w_out, v_ln_ffn2, v_w_ffn2_gu, v_w_ffn2_down):
    w = dict(ln_ffn1=ln_ffn1, w_ffn1_gu=w_ffn1_gu, w_ffn1_down=w_ffn1_down, ln_mix=ln_mix, w_in=w_in, pool_w=pool_w,
             pool_scale=pool_scale, w_pool_branch=w_pool_branch, q_norm=q_norm, k_norm=k_norm, sinks=sinks,
             w_attn_branch=w_attn_branch, w_out=w_out, ln_ffn2=ln_ffn2, w_ffn2_gu=w_ffn2_gu, w_ffn2_down=w_ffn2_down)
    mom = dict(ln_ffn1=m_ln_ffn1, w_ffn1_gu=m_w_ffn1_gu, w_ffn1_down=m_w_ffn1_down, ln_mix=m_ln_mix, w_in=m_w_in,
               pool_w=m_pool_w, pool_scale=m_pool_scale, w_pool_branch=m_w_pool_branch, q_norm=m_q_norm, k_norm=m_k_norm,
               sinks=m_sinks, w_attn_branch=m_w_attn_branch, w_out=m_w_out, ln_ffn2=m_ln_ffn2, w_ffn2_gu=m_w_ffn2_gu,
               w_ffn2_down=m_w_ffn2_down)
    var = dict(ln_ffn1=v_ln_ffn1, w_ffn1_gu=v_w_ffn1_gu, w_ffn1_down=v_w_ffn1_down, ln_mix=v_ln_mix, w_in=v_w_in,
               pool_w=v_pool_w, pool_scale=v_pool_scale, w_pool_branch=v_w_pool_branch, q_norm=v_q_norm, k_norm=v_k_norm,
               sinks=v_sinks, w_attn_branch=v_w_attn_branch, w_out=v_w_out, ln_ffn2=v_ln_ffn2, w_ffn2_gu=v_w_ffn2_gu,
               w_ffn2_down=v_w_ffn2_down)
    L = ln_ffn1.shape[0]

    def small(l):
        return dict(ln_ffn1=ln_ffn1[l], ln_mix=ln_mix[l], ln_ffn2=ln_ffn2[l], pool_w=pool_w[l].astype(CDT),
                    pool_scale=pool_scale[l], sinks=sinks[l],
                    gqk=jnp.concatenate([jnp.tile(q_norm[l], N_Q_HEADS), jnp.tile(k_norm[l], KV_DIM // HEAD_DIM)]).reshape(1, QK_DIM))

    place = jnp.stack([lax.axis_index("c"), 2 * lax.axis_index("x") + lax.axis_index("y")]).astype(jnp.int32)
    hooks = _Exchange({n: w[n].astype(CDT) for n in BIG}, small, place, L)
    loss_part, grad_x = _local_step(x[0], loss_target[0], L, hooks)
    g_big = hooks.reduced()
    g_small, loss = hooks.reduced_small(loss_part)
    g_small = {n: v.reshape(w[n].shape) for n, v in g_small.items()}

    grad_out, delta, new_m, new_v = {}, {}, {}, {}
    for n in BIG:
        shape = w[n].shape
        flat = (shape[0] * shape[1], shape[2])
        go, d, nm, nv = _adamw(w[n].reshape(flat), g_big[n].reshape(flat), mom[n].reshape(flat), var[n].reshape(flat), "adamw")
        grad_out[n], delta[n], new_m[n], new_v[n] = go.reshape(shape), d.reshape(shape), nm.reshape(shape), nv.reshape(shape)
    pw, _ = _pack_small([w[n] for n in SMALL])
    pg, sp = _pack_small([g_small[n] for n in SMALL])
    pm_, _ = _pack_small([mom[n] for n in SMALL])
    pv, _ = _pack_small([var[n] for n in SMALL])
    _, d, nm, nv = _adamw(pw, pg, pm_, pv, "adamw_small")
    shapes = [w[n].shape for n in SMALL]
    for n, dv, mv, vv in zip(SMALL, _unpack_small(d, sp, shapes), _unpack_small(nm, sp, shapes), _unpack_small(nv, sp, shapes)):
        grad_out[n], delta[n], new_m[n], new_v[n] = g_small[n], dv, mv, vv

    return (loss, grad_x[None], *[grad_out[n] for n in WEIGHTS], *[delta[n] for n in WEIGHTS],
            *[new_m[n] for n in WEIGHTS], *[new_v[n] for n in WEIGHTS])
```

```python
import functools
import math

import jax
import jax.numpy as jnp
from jax import lax
from jax.experimental import pallas as pl
from jax.experimental.pallas import tpu as pltpu

F32 = jnp.float32
CDT = jnp.bfloat16
WIRE_DT = jnp.bfloat16

D_MODEL = 1024
POOL_WINDOWS = (2, 4, 8, 16)
POOL_WMAX = 16
GROUP = 128
POOL_DIM = 512
HEAD_DIM = 64
N_Q_HEADS = 8
ATTN_DIM = 512
KV_DIM = 128
QK_DIM = ATTN_DIM + KV_DIM
GATE_DIM = 2 * D_MODEL
BLOCK = 128
ROPE_THETA = 500000.0
ROT_DIM = 16
EPS = 1e-6
ATTN_SCALE = HEAD_DIM ** -0.5

ADAM_LR = 0.001
ADAM_B1 = 0.9
ADAM_B2 = 0.999
ADAM_EPS = 1e-08
ADAM_WD = 0.01
ADAM_STEP = 10

N_CHIPS = 4
N_DEV = 8
LANES = 128
VMEM_LIMIT_BYTES = 48 * 1024 * 1024

MESH = pl.DeviceIdType.MESH
ANY = pl.BlockSpec(memory_space=pl.ANY)

BIG = ("w_ffn1_gu", "w_ffn1_down", "w_in", "w_pool_branch", "w_attn_branch", "w_out", "w_ffn2_gu", "w_ffn2_down")
COL_SHARDED = ("w_ffn1_gu", "w_in", "w_pool_branch", "w_attn_branch", "w_ffn2_gu")
USED_AS_BLOCKS = ("w_pool_branch", "w_attn_branch")
WIDE = ("w_ffn1_gu", "w_ffn2_gu")
SMALL = ("ln_ffn1", "ln_mix", "pool_w", "pool_scale", "q_norm", "k_norm", "sinks", "ln_ffn2")
WEIGHTS = ("ln_ffn1", "w_ffn1_gu", "w_ffn1_down", "ln_mix", "w_in", "pool_w", "pool_scale", "w_pool_branch",
           "q_norm", "k_norm", "sinks", "w_attn_branch", "w_out", "ln_ffn2", "w_ffn2_gu", "w_ffn2_down")


def _tile(n, target, mult=8):
    if n <= target:
        return n
    for t in range(target - target % mult, 0, -mult):
        if n % t == 0:
            return t
    raise ValueError((n, target, mult))


def _params(*sem):
    return pltpu.CompilerParams(dimension_semantics=sem, vmem_limit_bytes=VMEM_LIMIT_BYTES)


def _sigmoid(v):
    return 0.5 * jnp.tanh(0.5 * v) + 0.5


def _dot(a, b):
    return jnp.dot(a, b, preferred_element_type=F32)


def _dot_nt(a, b):
    return lax.dot_general(a, b, (((1,), (1,)), ((), ())), preferred_element_type=F32)


def _dot_tn(a, b):
    return lax.dot_general(a, b, (((0,), (0,)), ((), ())), preferred_element_type=F32)


class _Side:
    def __init__(self, ins, out_shapes, n_sems, issue, aliases=None):
        self.ins, self.out_shapes, self.n_sems, self.issue = list(ins), list(out_shapes), n_sems, issue
        self.aliases = dict(aliases or {})
        self.outs = None


def _pcall(body, name, grid, in_specs, out_specs, out_shape, args, dims, side=None, scratch=()):
    scratch = list(scratch)
    if side is None:
        return pl.pallas_call(body, name=name, grid=grid, in_specs=in_specs, out_specs=out_specs, out_shape=out_shape,
                              scratch_shapes=scratch, compiler_params=_params(*dims))(*args)
    n_in, n_out, s_in, s_out = len(in_specs), len(out_specs), len(side.ins), len(side.out_shapes)

    def wrapped(*refs):
        main_in, side_in = refs[:n_in], refs[n_in:n_in + s_in]
        main_out = refs[n_in + s_in:n_in + s_in + n_out]
        side_out = refs[n_in + s_in + n_out:n_in + s_in + n_out + s_out]
        rest = refs[n_in + s_in + n_out + s_out:]
        main_scratch, (ssem, rsem) = rest[:len(scratch)], rest[len(scratch):]
        ids = [pl.program_id(ax) for ax in range(len(grid))]
        first = functools.reduce(jnp.logical_and, [i == 0 for i in ids])
        last = functools.reduce(jnp.logical_and, [i == g - 1 for i, g in zip(ids, grid)])

        @pl.when(first)
        def _():
            for cp in side.issue(side_in, side_out, ssem, rsem):
                cp.start()

        body(*main_in, *main_out, *main_scratch)

        @pl.when(last)
        def _():
            cps = side.issue(side_in, side_out, ssem, rsem)
            for cp in cps:
                cp.wait_recv()
            for cp in cps:
                cp.wait_send()

    outs = pl.pallas_call(
        wrapped, name=name, grid=grid, in_specs=list(in_specs) + [ANY] * s_in, out_specs=list(out_specs) + [ANY] * s_out,
        out_shape=list(out_shape) + side.out_shapes,
        input_output_aliases={n_in + i: n_out + o for i, o in side.aliases.items()},
        scratch_shapes=scratch + [pltpu.SemaphoreType.DMA((side.n_sems,))] * 2,
        compiler_params=_params(*["arbitrary"] * len(grid)),
    )(*args, *side.ins)
    side.outs = list(outs[n_out:])
    return list(outs[:n_out])


def _run_side(side, name):
    s_in = len(side.ins)

    def body(*refs):
        ssem, rsem = refs[s_in + len(side.out_shapes):]
        cps = side.issue(refs[:s_in], refs[s_in:s_in + len(side.out_shapes)], ssem, rsem)
        for cp in cps:
            cp.start()
        for cp in cps:
            cp.wait_recv()
        for cp in cps:
            cp.wait_send()

    side.outs = list(pl.pallas_call(
        body, name=name, in_specs=[ANY] * s_in, out_specs=[ANY] * len(side.out_shapes), out_shape=side.out_shapes,
        input_output_aliases=side.aliases, scratch_shapes=[pltpu.SemaphoreType.DMA((side.n_sems,))] * 2,
    )(*side.ins))
    return side.outs


def _loss_head(y, tgt, name):
    T, Dm = y.shape
    tm = _tile(T, 512)

    def body(y_ref, t_ref, dy_ref, loss_ref):
        @pl.when(pl.program_id(0) == 0)
        def _():
            loss_ref[...] = jnp.zeros_like(loss_ref)

        diff = y_ref[...] - t_ref[...]
        dy_ref[...] = diff * (1.0 / Dm)
        part = jnp.sum(jnp.mean(diff * diff, axis=-1, keepdims=True), axis=0, keepdims=True)
        loss_ref[...] += 0.5 * part

    row = pl.BlockSpec((tm, Dm), lambda i: (i, 0))
    one = pl.BlockSpec((1, 1), lambda i: (0, 0))
    return pl.pallas_call(
        body, name=name, grid=(T // tm,),
        in_specs=[row, row], out_specs=[row, one],
        out_shape=[jax.ShapeDtypeStruct((T, Dm), F32), jax.ShapeDtypeStruct((1, 1), F32)],
        compiler_params=_params("arbitrary"),
    )(y, tgt)


def _mm_nn(a, b, name, out_dtype, res=None, scale=1.0, tm_target=512, side=None):
    M, K = a.shape
    N = b.shape[1]
    tm = _tile(M, tm_target)

    def body(a_ref, b_ref, *rest):
        acc = _dot(a_ref[...].astype(CDT), b_ref[...])
        if res is None:
            (o_ref,) = rest
        else:
            r_ref, o_ref = rest
            acc = r_ref[...] + scale * acc
        o_ref[...] = acc.astype(o_ref.dtype)

    in_specs = [pl.BlockSpec((tm, K), lambda i: (i, 0)), pl.BlockSpec((K, N), lambda i: (0, 0))]
    args = [a, b]
    if res is not None:
        in_specs.append(pl.BlockSpec((tm, N), lambda i: (i, 0)))
        args.append(res)
    return _pcall(body, name, (M // tm,), in_specs, [pl.BlockSpec((tm, N), lambda i: (i, 0))],
                  [jax.ShapeDtypeStruct((M, N), out_dtype)], args, ("parallel",), side)[0]


def _mm_tn(x, dy, name, scale=1.0, col_blocks=1, tn_target=1664, tm_target=1408, tk_target=1024, side=None):
    T, M = x.shape
    split = dy.ndim == 3
    Nh = dy.shape[-1]
    N = 2 * Nh if split else Nh
    nb = N // col_blocks
    whole = col_blocks > 1 and not split and N <= tn_target
    tm = _tile(M, tm_target, LANES)
    tn = N if whole else _tile(math.gcd(Nh, nb), tn_target, LANES)
    tk = _tile(T, tk_target)
    nk = T // tk
    njh, njb = Nh // tn, max(nb // tn, 1)

    def body(x_ref, dy_ref, o_ref, acc_ref):
        k = pl.program_id(2)

        @pl.when(k == 0)
        def _():
            acc_ref[...] = jnp.zeros_like(acc_ref)

        acc_ref[...] += _dot_tn(x_ref[...].astype(CDT), dy_ref[...].astype(CDT))

        @pl.when(k == nk - 1)
        def _():
            res = (acc_ref[...] if scale == 1.0 else scale * acc_ref[...]).astype(o_ref.dtype)
            if whole:
                for b in range(col_blocks):
                    o_ref[b] = res[:, b * nb:(b + 1) * nb]
            else:
                o_ref[...] = res

    if split:
        dy_spec = pl.BlockSpec((None, tk, tn), lambda i, j, k: (j // njh, k, j % njh))
    else:
        dy_spec = pl.BlockSpec((tk, tn), lambda i, j, k: (k, j))
    if col_blocks == 1:
        out_spec, out_dims = pl.BlockSpec((tm, tn), lambda i, j, k: (i, j)), (M, N)
    elif whole:
        out_spec, out_dims = pl.BlockSpec((col_blocks, tm, nb), lambda i, j, k: (0, i, 0)), (col_blocks, M, nb)
    else:
        out_spec, out_dims = pl.BlockSpec((None, tm, tn), lambda i, j, k: (j // njb, i, j % njb)), (col_blocks, M, nb)
    return _pcall(body, name, (M // tm, N // tn, nk), [pl.BlockSpec((tk, tm), lambda i, j, k: (k, i)), dy_spec],
                  [out_spec], [jax.ShapeDtypeStruct(out_dims, WIRE_DT)], (x, dy), ("parallel", "parallel", "arbitrary"),
                  side, [pltpu.VMEM((tm, tn), F32)])[0]


def _mm_tn_parts(x, parts, name):
    T, M = x.shape
    widths = [p.shape[1] for p in parts]
    N = sum(widths)
    tk = _tile(T, 512)

    def body(x_ref, *refs):
        o_ref = refs[-1]

        @pl.when(pl.program_id(0) == 0)
        def _():
            o_ref[...] = jnp.zeros_like(o_ref)

        parts_cat = jnp.concatenate([p_ref[...].astype(CDT) for p_ref in refs[:-1]], axis=1)
        o_ref[...] += _dot_tn(x_ref[...].astype(CDT), parts_cat)

    return pl.pallas_call(
        body, name=name, grid=(T // tk,),
        in_specs=[pl.BlockSpec((tk, M), lambda k: (k, 0))] + [pl.BlockSpec((tk, wd), lambda k: (k, 0)) for wd in widths],
        out_specs=pl.BlockSpec((M, N), lambda k: (0, 0)),
        out_shape=jax.ShapeDtypeStruct((M, N), F32), compiler_params=_params("arbitrary"),
    )(x, *parts)


def _rmsnorm_rows(x_ref, g_ref):
    xv = x_ref[...]
    r = lax.rsqrt(jnp.mean(xv * xv, axis=-1, keepdims=True) + EPS)
    return (xv * r * g_ref[...]).astype(CDT)


def _ffn_up(x, ln, wgu, name, side=None):
    T, Dm = x.shape
    Fd = wgu.shape[1] // 2
    tm = _tile(T, 256)

    def body(x_ref, ln_ref, wg_ref, wu_ref, h_ref, gu_ref, a_ref):
        hv = _rmsnorm_rows(x_ref, ln_ref)
        h_ref[...] = hv
        g = _dot(hv, wg_ref[...])
        u = _dot(hv, wu_ref[...])
        sg = _sigmoid(g)
        silu = g * sg
        a_ref[...] = (silu * u).astype(a_ref.dtype)
        gu_ref[0] = (0.5 * u * (sg * (1.0 + g * (1.0 - sg)))).astype(gu_ref.dtype)
        gu_ref[1] = (0.5 * silu).astype(gu_ref.dtype)

    row = pl.BlockSpec((tm, Dm), lambda i: (i, 0))
    return _pcall(
        body, name, (T // tm,),
        [row, pl.BlockSpec((1, Dm), lambda i: (0, 0)),
         pl.BlockSpec((Dm, Fd), lambda i: (0, 0), pipeline_mode=pl.Buffered(1)),
         pl.BlockSpec((Dm, Fd), lambda i: (0, 1), pipeline_mode=pl.Buffered(1))],
        [row, pl.BlockSpec((2, tm, Fd), lambda i: (0, i, 0)), pl.BlockSpec((tm, Fd), lambda i: (i, 0))],
        [jax.ShapeDtypeStruct((T, Dm), CDT), jax.ShapeDtypeStruct((2, T, Fd), CDT), jax.ShapeDtypeStruct((T, Fd), CDT)],
        (x, ln.reshape(1, Dm), wgu, wgu), ("parallel",), side)


def _ffn_down_bwd(dxo, wd, gu, name, side=None):
    T, Dm = dxo.shape
    Fd = wd.shape[0]
    tm = _tile(T, 256)

    def body(dx_ref, wd_ref, gu_ref, dgu_ref):
        da = _dot_nt(dx_ref[...].astype(CDT), wd_ref[...])
        dgu_ref[0] = (da * gu_ref[0].astype(F32)).astype(dgu_ref.dtype)
        dgu_ref[1] = (da * gu_ref[1].astype(F32)).astype(dgu_ref.dtype)

    gu_spec = pl.BlockSpec((2, tm, Fd), lambda i: (0, i, 0))
    return _pcall(
        body, name, (T // tm,),
        [pl.BlockSpec((tm, Dm), lambda i: (i, 0)),
         pl.BlockSpec((Fd, Dm), lambda i: (0, 0), pipeline_mode=pl.Buffered(1)), gu_spec],
        [gu_spec], [jax.ShapeDtypeStruct((2, T, Fd), CDT)],
        (dxo, wd, gu), ("parallel",), side)[0]


def _mm_nt_norm_bwd(a_parts, b, x, g, dres, name, side=None):
    T, Dm = x.shape
    tm = _tile(T, 512)

    def b_cols(b_ref, lo, wd):
        if b.ndim == 2:
            return [(0, wd, b_ref[:, lo:lo + wd])]
        kb = b.shape[2]
        return [(j * kb - lo, kb, b_ref[j]) for j in range(lo // kb, (lo + wd) // kb)]

    def body(*refs):
        a_refs, (b_ref, x_ref, g_ref, dres_ref, dx_ref, dg_ref) = refs[:len(a_parts)], refs[len(a_parts):]

        @pl.when(pl.program_id(0) == 0)
        def _():
            dg_ref[...] = jnp.zeros_like(dg_ref)

        dh, lo = None, 0
        if b.ndim == 2 and len(a_parts) > 1 and all(p.ndim == 2 for p in a_parts):
            dh = _dot_nt(jnp.concatenate([a_ref[...].astype(CDT) for a_ref in a_refs], axis=1), b_ref[...])
            a_refs = ()
        for a_ref, part in zip(a_refs, a_parts):
            slabs = [a_ref] if part.ndim == 2 else [a_ref.at[s_] for s_ in range(part.shape[0])]
            for slab in slabs:
                for off, wd, bv in b_cols(b_ref, lo, part.shape[-1]):
                    term = _dot_nt(slab[:, off:off + wd].astype(CDT), bv)
                    dh = term if dh is None else dh + term
                lo += part.shape[-1]
        xv = x_ref[...]
        r = lax.rsqrt(jnp.mean(xv * xv, axis=-1, keepdims=True) + EPS)
        xh = xv * r
        dg_ref[...] += jnp.sum(dh * xh, axis=0, keepdims=True)
        dxh = dh * g_ref[...]
        dx_ref[...] = dres_ref[...] + r * (dxh - xh * jnp.mean(dxh * xh, axis=-1, keepdims=True))

    row = pl.BlockSpec((tm, Dm), lambda i: (i, 0))
    vec = pl.BlockSpec((1, Dm), lambda i: (0, 0))
    a_specs = [pl.BlockSpec((tm, p.shape[1]), lambda i: (i, 0)) if p.ndim == 2 else
               pl.BlockSpec((p.shape[0], tm, p.shape[2]), lambda i: (0, i, 0)) for p in a_parts]
    b_spec = pl.BlockSpec(b.shape, lambda i: (0,) * b.ndim, pipeline_mode=pl.Buffered(1))
    return _pcall(body, name, (T // tm,), a_specs + [b_spec, row, vec, row], [row, vec],
                  [jax.ShapeDtypeStruct((T, Dm), F32), jax.ShapeDtypeStruct((1, Dm), F32)],
                  (*a_parts, b, x, g.reshape(1, Dm), dres), ("arbitrary",), side)


def _mm_in(x, ln, w_in, gqk, cos_t, sin_t, pool_w, pool_scale, name, side=None):
    T, Dm = x.shape
    tm = _tile(T, 256, POOL_WMAX)
    widths = (POOL_DIM, QK_DIM, KV_DIM, GATE_DIM)

    def body(x_ref, ln_ref, w_ref, g_ref, c_ref, s_ref, pw_ref, sc_ref,
             h_ref, zu_ref, zqk_ref, zv_ref, zg_ref, qkn_ref, pm_ref, halo_ref):
        i = pl.program_id(0)
        hv = _rmsnorm_rows(x_ref, ln_ref)
        h_ref[...] = hv
        z = _dot(hv, w_ref[...])
        lo = 0
        for o_ref, wd in zip((zu_ref, zqk_ref, zv_ref, zg_ref), widths):
            o_ref[...] = z[:, lo:lo + wd].astype(o_ref.dtype)
            lo += wd
        first, low = _lane_masks()
        cosv, sinv = c_ref[...], s_ref[...]
        for c in range(QK_DIM // LANES):
            sl = slice(c * LANES, (c + 1) * LANES)
            xv = z[:, POOL_DIM + c * LANES:POOL_DIM + (c + 1) * LANES]
            r = lax.rsqrt(_head_mean(xv * xv, first) + EPS)
            xn = xv * r * g_ref[:, sl]
            qkn_ref[:, sl] = (xn * cosv + _rope_partner(xn, low) * sinv).astype(qkn_ref.dtype)
        @pl.when(i == 0)
        def _():
            halo_ref[...] = jnp.zeros_like(halo_ref)

        zu = z[:, :POOL_DIM]
        ext = jnp.concatenate([halo_ref[...], zu], axis=0)
        halo_ref[...] = zu[tm - POOL_WMAX:, :]
        pos = i * tm + lax.broadcasted_iota(jnp.int32, (tm, 1), 0)
        ys = [_dot(_window_mean_minus_token(ext, zu, g, w, pos).astype(CDT), pw_ref[g])
              for g, w in enumerate(POOL_WINDOWS)]
        pm_ref[...] = (jnp.concatenate(ys, axis=1) * sc_ref[...]).astype(pm_ref.dtype)

    row = pl.BlockSpec((tm, Dm), lambda i: (i, 0))
    tab = pl.BlockSpec((tm, LANES), lambda i: (i, 0))
    return _pcall(body, name, (T // tm,),
                  [row, pl.BlockSpec((1, Dm), lambda i: (0, 0)),
                   pl.BlockSpec(w_in.shape, lambda i: (0, 0), pipeline_mode=pl.Buffered(1)),
                   pl.BlockSpec((1, QK_DIM), lambda i: (0, 0)), tab, tab,
                   pl.BlockSpec(pool_w.shape, lambda i: (0, 0, 0)), pl.BlockSpec((1, POOL_DIM), lambda i: (0, 0))],
                  [row] + [pl.BlockSpec((tm, wd), lambda i: (i, 0)) for wd in widths + (QK_DIM, POOL_DIM)],
                  [jax.ShapeDtypeStruct((T, Dm), CDT)]
                  + [jax.ShapeDtypeStruct((T, wd), dt) for wd, dt in zip(widths, (F32, F32, F32, CDT))]
                  + [jax.ShapeDtypeStruct((T, QK_DIM), CDT), jax.ShapeDtypeStruct((T, POOL_DIM), CDT)],
                  (x, ln.reshape(1, Dm), w_in, gqk, cos_t, sin_t, pool_w, pool_scale.reshape(1, POOL_DIM)),
                  ("arbitrary",), side, [pltpu.VMEM((POOL_WMAX, POOL_DIM), F32)])


def _window_mean_minus_token(ext, u, g, w, pos):
    sl = slice(g * GROUP, (g + 1) * GROUP)
    s = ext[:, sl]
    span = 1
    while span < w:
        s = s + pltpu.roll(s, span, axis=0)
        span *= 2
    cnt = jnp.minimum(pos + 1, w).astype(F32)
    return s[POOL_WMAX:, :] / cnt - u[:, sl]


def _pool_bwd(zu, dpm, pool_w, scale, name):
    T = zu.shape[0]
    tm = _tile(T, 512, POOL_WMAX)
    hb = tm // POOL_WMAX
    nsteps = T // tm
    ext_rows = tm + POOL_WMAX

    def body(u_ref, halo_ref, dpm_ref, dnext_ref, pw_ref, sc_ref, du_ref, dpw_ref, dsc_ref):
        i = pl.program_id(0)

        @pl.when(i == 0)
        def _():
            dpw_ref[...] = jnp.zeros_like(dpw_ref)
            dsc_ref[...] = jnp.zeros_like(dsc_ref)

        u = u_ref[...]
        halo = jnp.where(i > 0, halo_ref[...], 0.0)
        ext = jnp.concatenate([halo, u], axis=0)
        dpm_t = dpm_ref[...].astype(F32)
        dnext = jnp.where(i < nsteps - 1, dnext_ref[...].astype(F32), 0.0)
        dext = jnp.concatenate([dpm_t, dnext], axis=0)
        sc = sc_ref[...]
        pos = i * tm + lax.broadcasted_iota(jnp.int32, (tm, 1), 0)
        pos_ext = i * tm + lax.broadcasted_iota(jnp.int32, (ext_rows, 1), 0)
        dus, dscs = [], []
        for g, w in enumerate(POOL_WINDOWS):
            sl = slice(g * GROUP, (g + 1) * GROUP)
            dc = _window_mean_minus_token(ext, u, g, w, pos).astype(CDT)
            y = _dot(dc, pw_ref[g])
            dscs.append(jnp.sum(dpm_t[:, sl] * y, axis=0, keepdims=True))
            dy_ext = (dext[:, sl] * sc[:, sl]).astype(CDT)
            dpw_ref[g] += _dot_tn(dc, dy_ext[:tm])
            dd = _dot_nt(dy_ext, pw_ref[g])
            r = dd / jnp.minimum(pos_ext + 1, w).astype(F32)
            span = 1
            while span < w:
                r = r + pltpu.roll(r, ext_rows - span, axis=0)
                span *= 2
            dus.append(r[:tm] - dd[:tm])
        du_ref[...] = jnp.concatenate(dus, axis=1).astype(du_ref.dtype)
        dsc_ref[...] += jnp.concatenate(dscs, axis=1)

    row = pl.BlockSpec((tm, POOL_DIM), lambda i: (i, 0))
    prev = pl.BlockSpec((POOL_WMAX, POOL_DIM), lambda i: (jnp.maximum(i * hb - 1, 0), 0))
    nxt = pl.BlockSpec((POOL_WMAX, POOL_DIM), lambda i: (jnp.minimum((i + 1) * hb, nsteps * hb - 1), 0))
    return pl.pallas_call(
        body, name=name, grid=(nsteps,),
        in_specs=[row, prev, row, nxt, pl.BlockSpec(pool_w.shape, lambda i: (0, 0, 0)),
                  pl.BlockSpec((1, POOL_DIM), lambda i: (0, 0))],
        out_specs=[row, pl.BlockSpec(pool_w.shape, lambda i: (0, 0, 0)), pl.BlockSpec((1, POOL_DIM), lambda i: (0, 0))],
        out_shape=[jax.ShapeDtypeStruct((T, POOL_DIM), CDT), jax.ShapeDtypeStruct(pool_w.shape, F32),
                   jax.ShapeDtypeStruct((1, POOL_DIM), F32)],
        compiler_params=_params("arbitrary"),
    )(zu, zu, dpm, dpm, pool_w, scale.reshape(1, POOL_DIM))


def _rope_tables(T):
    pos = jnp.arange(T, dtype=F32)
    inv_freq = ROPE_THETA ** (-jnp.arange(0, ROT_DIM, 2, dtype=F32) / ROT_DIM)
    ang = pos[:, None] * inv_freq[None, :]
    cos, sin = jnp.cos(ang), jnp.sin(ang)
    rest = HEAD_DIM - ROT_DIM
    cos_h = jnp.concatenate([cos, cos, jnp.ones((T, rest), F32)], axis=1)
    sin_h = jnp.concatenate([-sin, sin, jnp.zeros((T, rest), F32)], axis=1)
    return jnp.tile(cos_h, (1, 2)), jnp.tile(sin_h, (1, 2))


def _lane_masks():
    lane = lax.broadcasted_iota(jnp.int32, (1, LANES), 1)
    in_head = lane % HEAD_DIM
    return lane < HEAD_DIM, in_head < ROT_DIM // 2


def _rope_partner(v, low):
    lane = lax.broadcasted_iota(jnp.int32, (1, LANES), 1)
    swapped = jnp.where(low, pltpu.roll(v, LANES - ROT_DIM // 2, axis=1), pltpu.roll(v, ROT_DIM // 2, axis=1))
    return jnp.where(lane % HEAD_DIM < ROT_DIM, swapped, 0.0)


def _head_mean(v, first):
    lo = jnp.sum(jnp.where(first, v, 0.0), axis=-1, keepdims=True)
    hi = jnp.sum(jnp.where(first, 0.0, v), axis=-1, keepdims=True)
    return jnp.where(first, lo, hi) * (1.0 / HEAD_DIM)


def _qk_bwd(dq, dk_parts, dv_parts, zqk, gqk, cos_t, sin_t, name):
    T = zqk.shape[0]
    tm = _tile(T, 512, BLOCK)
    nsteps = T // tm
    nq = ATTN_DIM // LANES

    def shifted_sum(cur_ref, prev_ref, next_ref, last):
        nxt = jnp.where(last, 0.0, next_ref[...])
        return cur_ref[...] + jnp.concatenate([prev_ref[BLOCK:, :], nxt], axis=0)

    def body(dq_ref, kc_ref, kp_ref, kn_ref, vc_ref, vp_ref, vn_ref, z_ref, g_ref, c_ref, s_ref, dz_ref, dv_ref, dg_ref):
        i = pl.program_id(0)

        @pl.when(i == 0)
        def _():
            dg_ref[...] = jnp.zeros_like(dg_ref)

        last = i == nsteps - 1
        dk = shifted_sum(kc_ref, kp_ref, kn_ref, last)
        dv_ref[...] = shifted_sum(vc_ref, vp_ref, vn_ref, last).astype(dv_ref.dtype)
        first, low = _lane_masks()
        cosv, sinv = c_ref[...], s_ref[...]
        dgs = []
        for c in range(QK_DIM // LANES):
            sl = slice(c * LANES, (c + 1) * LANES)
            dout = dq_ref[:, sl] if c < nq else dk
            dxn = dout * cosv + _rope_partner(dout * sinv, low)
            xv = z_ref[:, sl]
            r = lax.rsqrt(_head_mean(xv * xv, first) + EPS)
            xh = xv * r
            dgs.append(jnp.sum(dxn * xh, axis=0, keepdims=True))
            dxh = dxn * g_ref[:, sl]
            dz_ref[:, sl] = (r * (dxh - xh * _head_mean(dxh * xh, first))).astype(dz_ref.dtype)
        dg_ref[...] += jnp.concatenate(dgs, axis=1)

    row = pl.BlockSpec((tm, QK_DIM), lambda i: (i, 0))
    tab = pl.BlockSpec((tm, LANES), lambda i: (i, 0))
    vec = pl.BlockSpec((1, QK_DIM), lambda i: (0, 0))
    nxt = pl.BlockSpec((BLOCK, LANES), lambda i: (jnp.minimum((i + 1) * (tm // BLOCK), T // BLOCK - 1), 0))
    kv = [tab, tab, nxt]
    return pl.pallas_call(
        body, name=name, grid=(nsteps,),
        in_specs=[pl.BlockSpec((tm, ATTN_DIM), lambda i: (i, 0))] + kv + kv + [row, vec, tab, tab],
        out_specs=[row, tab, vec],
        out_shape=[jax.ShapeDtypeStruct((T, QK_DIM), CDT), jax.ShapeDtypeStruct((T, KV_DIM), CDT),
                   jax.ShapeDtypeStruct((1, QK_DIM), F32)],
        compiler_params=_params("arbitrary"),
    )(dq, dk_parts[0], dk_parts[1], dk_parts[1], dv_parts[0], dv_parts[1], dv_parts[1], zqk, gqk, cos_t, sin_t)


def _dup_half(v, first, kv):
    swapped = pltpu.roll(v, HEAD_DIM, axis=1)
    return jnp.where(first, v, swapped) if kv == 0 else jnp.where(first, swapped, v)


HEADS_PER_KV = 4
HEAD_STACK_FWD = 1
HEAD_STACK_BWD = 2


def _attn_bias(stack):
    qi = lax.broadcasted_iota(jnp.int32, (stack * BLOCK, 2 * BLOCK), 0) % BLOCK
    ki = lax.broadcasted_iota(jnp.int32, (stack * BLOCK, 2 * BLOCK), 1)
    diff = qi + BLOCK - ki
    band = (diff >= 0) & (diff < BLOCK)
    return jnp.stack([jnp.where(band, 0.0, -jnp.inf), jnp.where(band & (ki >= BLOCK), 0.0, -jnp.inf)]).astype(F32)


def _attn_blocks(T):
    return _tile(T // BLOCK, 4, 1)


def _stack_heads(ref, rows, kv, heads, first):
    parts = []
    for h in heads:
        c = 2 * kv + h // 2
        v = ref[rows, c * LANES:(c + 1) * LANES].astype(CDT)
        zero = jnp.zeros_like(v)
        parts.append(jnp.where(first, v, zero) if h % 2 == 0 else jnp.where(first, zero, v))
    return parts[0] if len(parts) == 1 else jnp.concatenate(parts, axis=0)


def _row_blocks(v, n):
    return [v[b * BLOCK:(b + 1) * BLOCK] for b in range(n)]


def _sink_column(sink_ref, kv, heads):
    cols = [jnp.full((BLOCK, 1), sink_ref[HEADS_PER_KV * kv + h], F32) for h in heads]
    return cols[0] if len(cols) == 1 else jnp.concatenate(cols, axis=0)


def _head_groups(stack):
    return [tuple(range(g, g + stack)) for g in range(0, HEADS_PER_KV, stack)]


def _softmax_with_sink(qst, kdup, sinkcol, bias):
    s = _dot_nt(qst, kdup) * ATTN_SCALE + bias
    m = jnp.maximum(jnp.max(s, axis=-1, keepdims=True), sinkcol)
    pu = jnp.exp(s - m)
    denom = jnp.sum(pu, axis=-1, keepdims=True) + jnp.exp(sinkcol - m)
    return pu * (1.0 / denom), m + jnp.log(denom)


def _attn_fwd(qkn, zv, sinks, name, side=None):
    T = qkn.shape[0]
    R = _attn_blocks(T)
    tq = R * BLOCK

    def body(sink_ref, bias_ref, qk_ref, qkp_ref, v_ref, vp_ref, o_ref, lse_ref):
        i = pl.program_id(0)
        first, _ = _lane_masks()
        lane = lax.broadcasted_iota(jnp.int32, (1, LANES), 1)
        kall = jnp.concatenate([qkp_ref[:, ATTN_DIM:], qk_ref[:, ATTN_DIM:]], axis=0)
        vall = jnp.concatenate([vp_ref[...], v_ref[...]], axis=0).astype(CDT)
        for r in range(R):
            bias = bias_ref[jnp.where(i == 0, 1, 0)] if r == 0 else bias_ref[0]
            rows = slice(r * BLOCK, (r + 2) * BLOCK)
            qrows = slice(r * BLOCK, (r + 1) * BLOCK)
            lse_rows = jnp.zeros((BLOCK, LANES), F32)
            for kv in range(2):
                kdup = _dup_half(kall[rows], first, kv)
                vdup = _dup_half(vall[rows], first, kv)
                res = []
                for heads in _head_groups(HEAD_STACK_FWD):
                    p, lse = _softmax_with_sink(_stack_heads(qk_ref, qrows, kv, heads, first), kdup,
                                                _sink_column(sink_ref, kv, heads), bias)
                    res += _row_blocks(_dot(p.astype(CDT), vdup), len(heads))
                    for b, col in enumerate(_row_blocks(lse, len(heads))):
                        lse_rows = jnp.where(lane == HEADS_PER_KV * kv + heads[b], col, lse_rows)
                o_ref[qrows, 2 * kv * LANES:(2 * kv + 1) * LANES] = jnp.where(first, res[0], res[1]).astype(o_ref.dtype)
                o_ref[qrows, (2 * kv + 1) * LANES:(2 * kv + 2) * LANES] = jnp.where(first, res[2], res[3]).astype(o_ref.dtype)
            lse_ref[qrows, :] = lse_rows

    bias = _attn_bias(HEAD_STACK_FWD)
    prev = lambda i: (jnp.maximum(i * R - 1, 0), 0)
    return _pcall(
        body, name, (T // tq,),
        [pl.BlockSpec(memory_space=pltpu.SMEM), pl.BlockSpec(bias.shape, lambda i: (0, 0, 0)),
         pl.BlockSpec((tq, QK_DIM), lambda i: (i, 0)), pl.BlockSpec((BLOCK, QK_DIM), prev),
         pl.BlockSpec((tq, KV_DIM), lambda i: (i, 0)), pl.BlockSpec((BLOCK, KV_DIM), prev)],
        [pl.BlockSpec((tq, ATTN_DIM), lambda i: (i, 0)), pl.BlockSpec((tq, LANES), lambda i: (i, 0))],
        [jax.ShapeDtypeStruct((T, ATTN_DIM), CDT), jax.ShapeDtypeStruct((T, LANES), F32)],
        (sinks, bias, qkn, qkn, zv, zv), ("parallel",), side)


def _attn_bwd(qkn, zv, sinks, do, o, lse, name, side=None):
    T = qkn.shape[0]
    R = _attn_blocks(T)
    tq = R * BLOCK

    def body(sink_ref, bias_ref, qk_ref, qkp_ref, v_ref, vp_ref, do_ref, o_ref, lse_ref,
             dq_ref, dkc_ref, dkp_ref, dvc_ref, dvp_ref, ds_ref):
        i = pl.program_id(0)

        @pl.when(i == 0)
        def _():
            ds_ref[...] = jnp.zeros_like(ds_ref)

        first, _ = _lane_masks()
        lane = lax.broadcasted_iota(jnp.int32, (1, LANES), 1)
        kall = jnp.concatenate([qkp_ref[:, ATTN_DIM:], qk_ref[:, ATTN_DIM:]], axis=0)
        vall = jnp.concatenate([vp_ref[...], v_ref[...]], axis=0).astype(CDT)
        for r in range(R):
            bias = bias_ref[jnp.where(i == 0, 1, 0)] if r == 0 else bias_ref[0]
            rows = slice(r * BLOCK, (r + 2) * BLOCK)
            qrows = slice(r * BLOCK, (r + 1) * BLOCK)
            dk_out, dv_out = [], []
            lse_rows = lse_ref[qrows, :]
            for kv in range(2):
                kdup = _dup_half(kall[rows], first, kv)
                vdup = _dup_half(vall[rows], first, kv)
                dq_h = []
                dk_acc = jnp.zeros((2 * BLOCK, LANES), F32)
                dv_acc = jnp.zeros((2 * BLOCK, LANES), F32)
                for heads in _head_groups(HEAD_STACK_BWD):
                    qst = _stack_heads(qk_ref, qrows, kv, heads, first)
                    dost = _stack_heads(do_ref, qrows, kv, heads, first)
                    lse_cols, delta_cols = [], []
                    for h in heads:
                        cols = slice((2 * kv + h // 2) * LANES, (2 * kv + h // 2 + 1) * LANES)
                        prod = do_ref[qrows, cols].astype(F32) * o_ref[qrows, cols].astype(F32)
                        own = first if h % 2 == 0 else jnp.logical_not(first)
                        delta_cols.append(jnp.sum(jnp.where(own, prod, 0.0), axis=-1, keepdims=True))
                        lse_cols.append(jnp.sum(jnp.where(lane == HEADS_PER_KV * kv + h, lse_rows, 0.0), axis=-1, keepdims=True))
                    lse_col = lse_cols[0] if len(heads) == 1 else jnp.concatenate(lse_cols, axis=0)
                    delta = delta_cols[0] if len(heads) == 1 else jnp.concatenate(delta_cols, axis=0)
                    p = jnp.exp(_dot_nt(qst, kdup) * ATTN_SCALE + bias - lse_col)
                    dsc = (p * (_dot_nt(dost, vdup) - delta)).astype(CDT)
                    psink = jnp.exp(_sink_column(sink_ref, kv, heads) - lse_col)
                    for b, term in enumerate(_row_blocks(psink * delta, len(heads))):
                        row = HEADS_PER_KV * kv + heads[b]
                        ds_ref[row:row + 1, :] += jnp.sum(term, axis=0, keepdims=True)
                    dq_h += _row_blocks(_dot(dsc, kdup) * ATTN_SCALE, len(heads))
                    dk_acc = dk_acc + _dot_tn(dsc, qst) * ATTN_SCALE
                    dv_acc = dv_acc + _dot_tn(p.astype(CDT), dost)
                dq_ref[qrows, 2 * kv * LANES:(2 * kv + 1) * LANES] = jnp.where(first, dq_h[0], dq_h[1])
                dq_ref[qrows, (2 * kv + 1) * LANES:(2 * kv + 2) * LANES] = jnp.where(first, dq_h[2], dq_h[3])
                dk_out.append(dk_acc + pltpu.roll(dk_acc, HEAD_DIM, axis=1))
                dv_out.append(dv_acc + pltpu.roll(dv_acc, HEAD_DIM, axis=1))
            dk = jnp.where(first, dk_out[0], dk_out[1])
            dv = jnp.where(first, dv_out[0], dv_out[1])
            dkp_ref[qrows, :] = dk[:BLOCK]
            dkc_ref[qrows, :] = dk[BLOCK:]
            dvp_ref[qrows, :] = dv[:BLOCK]
            dvc_ref[qrows, :] = dv[BLOCK:]

    bias = _attn_bias(HEAD_STACK_BWD)
    prev = lambda i: (jnp.maximum(i * R - 1, 0), 0)
    kvrow = pl.BlockSpec((tq, KV_DIM), lambda i: (i, 0))
    qrow = pl.BlockSpec((tq, ATTN_DIM), lambda i: (i, 0))
    kv_shape = jax.ShapeDtypeStruct((T, KV_DIM), F32)
    return _pcall(
        body, name, (T // tq,),
        [pl.BlockSpec(memory_space=pltpu.SMEM), pl.BlockSpec(bias.shape, lambda i: (0, 0, 0)),
         pl.BlockSpec((tq, QK_DIM), lambda i: (i, 0)), pl.BlockSpec((BLOCK, QK_DIM), prev),
         kvrow, pl.BlockSpec((BLOCK, KV_DIM), prev), qrow, qrow, kvrow],
        [qrow, kvrow, kvrow, kvrow, kvrow, pl.BlockSpec((N_Q_HEADS, LANES), lambda i: (0, 0))],
        [jax.ShapeDtypeStruct((T, ATTN_DIM), F32), kv_shape, kv_shape, kv_shape, kv_shape,
         jax.ShapeDtypeStruct((N_Q_HEADS, LANES), F32)],
        (sinks, bias, qkn, qkn, zv, zv, do, o, lse), ("arbitrary",), side)


def _merge_fwd(pm, o, w_pb, w_ab, zg, name, side=None):
    T = pm.shape[0]
    tm = _tile(T, 512)

    def body(pm_ref, o_ref, wp_ref, wa_ref, zg_ref, m_ref, gp_ref, ga_ref, fp_ref, fa_ref):
        pmv, ov = pm_ref[...], o_ref[...]
        a = jnp.concatenate([_dot(pmv, wp_ref[j]) for j in range(N_CHIPS)], axis=1)
        b = jnp.concatenate([_dot(ov, wa_ref[j]) for j in range(N_CHIPS)], axis=1)
        gp = _sigmoid(zg_ref[:, :D_MODEL].astype(F32))
        ga = _sigmoid(zg_ref[:, D_MODEL:].astype(F32))
        ap, ba = gp * a, ga * b
        m_ref[...] = (ap + ba).astype(m_ref.dtype)
        gp_ref[...] = gp.astype(gp_ref.dtype)
        ga_ref[...] = ga.astype(ga_ref.dtype)
        fp_ref[...] = (ap * (1.0 - gp)).astype(fp_ref.dtype)
        fa_ref[...] = (ba * (1.0 - ga)).astype(fa_ref.dtype)

    half = pl.BlockSpec((tm, POOL_DIM), lambda i: (i, 0))
    full = pl.BlockSpec((tm, D_MODEL), lambda i: (i, 0))
    wspec = pl.BlockSpec(w_pb.shape, lambda i: (0, 0, 0))
    out = jax.ShapeDtypeStruct((T, D_MODEL), CDT)
    return _pcall(body, name, (T // tm,), [half, half, wspec, wspec, pl.BlockSpec((tm, GATE_DIM), lambda i: (i, 0))],
                  [full] * 5, [out] * 5, (pm, o, w_pb, w_ab, zg), ("parallel",), side)


def _merge_bwd(dxo, w_out, factors, w_pb, w_ab, name):
    T = dxo.shape[0]
    tm = _tile(T, 512)
    kb = w_pb.shape[2]

    def branch_dx(dv, b_ref):
        acc = _dot_nt(dv[:, :kb], b_ref[0])
        for j in range(1, N_CHIPS):
            acc = acc + _dot_nt(dv[:, j * kb:(j + 1) * kb], b_ref[j])
        return acc

    def body(dx_ref, w_ref, gp_ref, ga_ref, fp_ref, fa_ref, wp_ref, wa_ref, da_ref, db_ref, dg_ref, dpm_ref, do_ref):
        dm = _dot_nt(dx_ref[...].astype(CDT), w_ref[...])
        da = (dm * gp_ref[...].astype(F32)).astype(da_ref.dtype)
        db = (dm * ga_ref[...].astype(F32)).astype(db_ref.dtype)
        da_ref[...] = da
        db_ref[...] = db
        dg_ref[:, :D_MODEL] = (dm * fp_ref[...].astype(F32)).astype(dg_ref.dtype)
        dg_ref[:, D_MODEL:] = (dm * fa_ref[...].astype(F32)).astype(dg_ref.dtype)
        dpm_ref[...] = branch_dx(da, wp_ref).astype(dpm_ref.dtype)
        do_ref[...] = branch_dx(db, wa_ref).astype(do_ref.dtype)

    full = pl.BlockSpec((tm, D_MODEL), lambda i: (i, 0))
    half = pl.BlockSpec((tm, POOL_DIM), lambda i: (i, 0))
    gate = pl.BlockSpec((tm, GATE_DIM), lambda i: (i, 0))
    wspec = pl.BlockSpec(w_pb.shape, lambda i: (0, 0, 0))
    out = jax.ShapeDtypeStruct((T, D_MODEL), CDT)
    out_half = jax.ShapeDtypeStruct((T, POOL_DIM), CDT)
    return pl.pallas_call(
        body, name=name, grid=(T // tm,),
        in_specs=[full, pl.BlockSpec((D_MODEL, D_MODEL), lambda i: (0, 0))] + [full] * 4 + [wspec, wspec],
        out_specs=[full, full, gate, half, half],
        out_shape=[out, out, jax.ShapeDtypeStruct((T, GATE_DIM), CDT), out_half, out_half],
        compiler_params=_params("parallel"),
    )(dxo, w_out, *factors, w_pb, w_ab)


def _adamw(w, g, m, v, name):
    Rr, C = w.shape
    tr = _tile(Rr, max(8, (1 << 19) // C // 8 * 8))

    def body(w_ref, g_ref, m_ref, v_ref, go_ref, d_ref, nm_ref, nv_ref):
        gv = g_ref[...]
        go_ref[...] = gv
        nm = ADAM_B1 * m_ref[...] + (1.0 - ADAM_B1) * gv
        nv = ADAM_B2 * v_ref[...] + (1.0 - ADAM_B2) * (gv * gv)
        m_hat = nm / (1.0 - ADAM_B1 ** ADAM_STEP)
        v_hat = nv / (1.0 - ADAM_B2 ** ADAM_STEP)
        d_ref[...] = -ADAM_LR * (m_hat / (jnp.sqrt(v_hat) + ADAM_EPS) + ADAM_WD * w_ref[...])
        nm_ref[...] = nm
        nv_ref[...] = nv

    blk = pl.BlockSpec((tr, C), lambda i: (i, 0))
    out = jax.ShapeDtypeStruct((Rr, C), F32)
    return pl.pallas_call(
        body, name=name, grid=(Rr // tr,), in_specs=[blk] * 4, out_specs=[blk] * 4, out_shape=[out] * 4,
        compiler_params=_params("parallel"),
    )(w, g, m, v)


def _place():
    return lax.axis_index("x"), lax.axis_index("y"), lax.axis_index("c")


def _other_chip(x, y, d):
    return (1 - x if d & 2 else x), (1 - y if d & 1 else y)


def _rcopy(src, dst, ssem, rsem, dev):
    return pltpu.make_async_remote_copy(src_ref=src, dst_ref=dst, send_sem=ssem, recv_sem=rsem, device_id=dev,
                                        device_id_type=MESH)


def _row_half(rows, c):
    return pl.ds(c * (rows // 2), rows // 2)


def _is_wide(name):
    return name in WIDE


def _block(ref, wide, j, rows, n):
    if wide:
        return ref.at[rows, pl.ds(pl.multiple_of(j * n, LANES), n)]
    return ref.at[j, rows]


def _gathered_shape(shard, wide):
    _, a, n = shard.shape
    return jax.ShapeDtypeStruct((a, N_CHIPS * n) if wide else (N_CHIPS, a, n), shard.dtype)


def _gather_ici_side(shards, wides, l):
    k_of = lambda w, d: 3 * w + d - 1

    def issue(ins, outs, ssem, rsem):
        x, y, c = _place()
        cps = []
        for w, (shard, wide) in enumerate(zip(shards, wides)):
            _, a, n = shard.shape
            half = _row_half(a, c)
            for d in (1, 2, 3):
                px, py = _other_chip(x, y, d)
                cps.append(_rcopy(ins[w].at[l, half], _block(outs[w], wide, 2 * x + y, half, n),
                                  ssem.at[k_of(w, d)], rsem.at[k_of(w, d)], (px, py, c)))
        return cps

    return _Side(shards, [_gathered_shape(s_, wd) for s_, wd in zip(shards, wides)], 3 * len(shards), issue)


def _gather_d2d_side(shards, wides, gathered, l):
    nw = len(shards)

    def issue(ins, outs, ssem, rsem):
        x, y, c = _place()
        sibling = (x, y, 1 - c)
        cps = []
        for w, (shard, wide) in enumerate(zip(shards, wides)):
            _, a, n = shard.shape
            half = _row_half(a, c)
            for d in (1, 2, 3):
                px, py = _other_chip(x, y, d)
                k = 3 * w + d - 1
                got = _block(outs[w], wide, 2 * px + py, half, n)
                cps.append(_rcopy(got, got, ssem.at[k], rsem.at[k], sibling))
            cps.append(_rcopy(ins[nw + w].at[l], _block(outs[w], wide, 2 * x + y, pl.ds(0, a), n),
                              ssem.at[3 * nw + w], rsem.at[3 * nw + w], sibling))
        return cps

    return _Side(list(gathered) + list(shards), [jax.ShapeDtypeStruct(g.shape, g.dtype) for g in gathered], 4 * nw, issue,
                 aliases={w: w for w in range(nw)})


def _half_shape(g, wide):
    if wide:
        return jax.ShapeDtypeStruct((g.shape[0] // 2, g.shape[1]), g.dtype)
    return jax.ShapeDtypeStruct((N_CHIPS, g.shape[1] // 2, g.shape[2]), g.dtype)


def _reduce_sibling_side(gms, wides):
    def issue(ins, outs, ssem, rsem):
        x, y, c = _place()
        cps = []
        for w, (g, wide) in enumerate(zip(gms, wides)):
            src = ins[w].at[_row_half(g.shape[0], 1 - c)] if wide else ins[w].at[:, _row_half(g.shape[1], 1 - c)]
            cps.append(_rcopy(src, outs[w], ssem.at[w], rsem.at[w], (x, y, 1 - c)))
        return cps

    return _Side(gms, [_half_shape(g, wd) for g, wd in zip(gms, wides)], len(gms), issue)


def _reduce_chip_side(ps, wides):
    def slot_shape(p, wide):
        return jax.ShapeDtypeStruct((N_CHIPS, p.shape[0], p.shape[1] // N_CHIPS) if wide else p.shape, p.dtype)

    def issue(ins, outs, ssem, rsem):
        x, y, c = _place()
        cps = []
        for w, (p, wide) in enumerate(zip(ps, wides)):
            ah, n = (p.shape[0], p.shape[1] // N_CHIPS) if wide else p.shape[1:]
            for d in (1, 2, 3):
                px, py = _other_chip(x, y, d)
                k = 3 * w + d - 1
                cps.append(_rcopy(_block(ins[w], wide, 2 * px + py, pl.ds(0, ah), n), outs[w].at[2 * x + y],
                                  ssem.at[k], rsem.at[k], (px, py, c)))
        return cps

    return _Side(ps, [slot_shape(p, wd) for p, wd in zip(ps, wides)], 3 * len(ps), issue)


def _share_side(accs, items):
    def issue(ins, outs, ssem, rsem):
        x, y, c = _place()
        cps = []
        for k, (w, layer) in enumerate(items):
            mine = outs[w].at[layer, _row_half(accs[w].shape[1], c)]
            cps.append(_rcopy(mine, mine, ssem.at[k], rsem.at[k], (x, y, 1 - c)))
        return cps

    return _Side(accs, [jax.ShapeDtypeStruct(a.shape, a.dtype) for a in accs], len(items), issue,
                 aliases={w: w for w in range(len(accs))})


def _sum_rows(rows, b):
    return _tile(rows, max(16, (1 << 19) // b // 16 * 16), 16)


def _pair_sum(g, recv, wide, place, name):
    ah, b = recv.shape[-2:]
    ta = _sum_rows(ah, b)
    nr = ah // ta

    def body(p_ref, g_ref, r_ref, o_ref):
        o_ref[...] = (g_ref[...].astype(F32) + r_ref[...].astype(F32)).astype(o_ref.dtype)

    if wide:
        grid = (nr,)
        specs = [pl.BlockSpec((ta, b), lambda r, p: (p[0] * nr + r, 0)), pl.BlockSpec((ta, b), lambda r, p: (r, 0))]
        out_spec = pl.BlockSpec((ta, b), lambda r, p: (r, 0))
    else:
        grid = (N_CHIPS, nr)
        specs = [pl.BlockSpec((None, ta, b), lambda j, r, p: (j, p[0] * nr + r, 0)),
                 pl.BlockSpec((None, ta, b), lambda j, r, p: (j, r, 0))]
        out_spec = pl.BlockSpec((None, ta, b), lambda j, r, p: (j, r, 0))
    return pl.pallas_call(
        body, name=name,
        grid_spec=pltpu.PrefetchScalarGridSpec(num_scalar_prefetch=1, grid=grid, in_specs=specs, out_specs=out_spec),
        out_shape=jax.ShapeDtypeStruct(recv.shape, recv.dtype), compiler_params=_params(*["parallel"] * len(grid)),
    )(place, g, recv)


def _chip_sum(slots, part, wide, place, acc, l, name):
    _, ah, b = slots.shape
    ta = _sum_rows(ah, b)
    nr = ah // ta

    def body(p_ref, s_ref, own_ref, acc_ref, o_ref):
        j = p_ref[1]
        own = own_ref[...].astype(F32)
        term = [jnp.where(j == s_, own, s_ref[s_].astype(F32)) for s_ in range(N_CHIPS)]
        o_ref[...] = ((term[0] + term[1]) + term[2]) + term[3]

    own_spec = (pl.BlockSpec((ta, b), lambda r, p: (r, p[1])) if wide else
                pl.BlockSpec((None, ta, b), lambda r, p: (p[1], r, 0)))
    return pl.pallas_call(
        body, name=name,
        grid_spec=pltpu.PrefetchScalarGridSpec(
            num_scalar_prefetch=1, grid=(nr,),
            in_specs=[pl.BlockSpec((N_CHIPS, ta, b), lambda r, p: (0, r, 0)), own_spec, ANY],
            out_specs=pl.BlockSpec((None, ta, b), lambda r, p: (l, p[0] * nr + r, 0))),
        out_shape=jax.ShapeDtypeStruct(acc.shape, F32), input_output_aliases={3: 0},
        compiler_params=_params("parallel"),
    )(place, slots, part, acc)


def _small_side(v):
    def issue(ins, outs, ssem, rsem):
        x, y, c = _place()
        cps = []
        for d in range(1, N_DEV):
            px, py = _other_chip(x, y, d >> 1)
            pc = 1 - c if d & 1 else c
            cps.append(_rcopy(ins[0], outs[0].at[4 * x + 2 * y + c], ssem.at[d - 1], rsem.at[d - 1], (px, py, pc)))
        return cps

    return _Side([v], [jax.ShapeDtypeStruct((N_DEV,) + v.shape, v.dtype)], N_DEV - 1, issue)


def _small_sum(slots, v, place, name):
    def body(p_ref, s_ref, v_ref, o_ref):
        me = 2 * p_ref[1] + p_ref[0]
        acc = jnp.where(me == 0, v_ref[...], s_ref[0])
        for s_ in range(1, N_DEV):
            acc = acc + jnp.where(me == s_, v_ref[...], s_ref[s_])
        o_ref[...] = acc

    return pl.pallas_call(
        body, name=name,
        grid_spec=pltpu.PrefetchScalarGridSpec(
            num_scalar_prefetch=1, grid=(1,),
            in_specs=[pl.BlockSpec(slots.shape, lambda i, p: (0, 0, 0)), pl.BlockSpec(v.shape, lambda i, p: (0, 0))],
            out_specs=pl.BlockSpec(v.shape, lambda i, p: (0, 0))),
        out_shape=jax.ShapeDtypeStruct(v.shape, F32), compiler_params=_params("arbitrary"),
    )(place, slots, v)


def _ffn_forward(x, p, tag, side_of):
    h, gu, act = _ffn_up(x, p[f"ln_{tag}"], p[f"w_{tag}_gu"], f"{tag}_up", side_of(f"{tag}_up"))
    x_out = _mm_nn(act, p[f"w_{tag}_down"], f"{tag}_down", F32, res=x, scale=0.5, side=side_of(f"{tag}_down"))
    return x_out, (x, h, gu, act)


def _row_blocks_of(dw):
    return dw.reshape(N_CHIPS, dw.shape[0] // N_CHIPS, dw.shape[1])


def _ffn_backward(dxo, saved, p, tag, side_of, grad):
    x, h, gu, act = saved
    dgu = _ffn_down_bwd(dxo, p[f"w_{tag}_down"], gu, f"{tag}_down_bwd", side_of(f"{tag}_down_bwd"))
    grad(f"w_{tag}_down", _row_blocks_of(_mm_tn(act, dxo, f"{tag}_dwd", scale=0.5, side=side_of(f"{tag}_dwd"))))
    grad(f"w_{tag}_gu", _mm_tn(h, dgu, f"{tag}_dwgu", tn_target=2816, tm_target=1024, side=side_of(f"{tag}_dwgu")))
    dx, d_ln = _mm_nt_norm_bwd([dgu], p[f"w_{tag}_gu"], x, p[f"ln_{tag}"], dxo, f"{tag}_dh_norm_bwd",
                               side_of(f"{tag}_dh_norm_bwd"))
    grad(f"ln_{tag}", d_ln[0])
    return dx


def _mixer_forward(x, p, tabs, side_of):
    h, zu, zqk, zv, zg, qkn, pm = _mm_in(x, p["ln_mix"], p["w_in"], p["gqk"], *tabs, p["pool_w"], p["pool_scale"],
                                         "mix_in", side_of("mix_in"))
    o, lse = _attn_fwd(qkn, zv, p["sinks"], "attn_fwd", side_of("attn_fwd"))
    m, *factors = _merge_fwd(pm, o, p["w_pool_branch"], p["w_attn_branch"], zg, "merge_fwd", side_of("merge_fwd"))
    x_out = _mm_nn(m, p["w_out"], "mix_out", F32, res=x, scale=1.0)
    return x_out, (x, h, zu, zqk, zv, pm, qkn, o, lse, factors, m)


def _mixer_backward(dxo, saved, p, tabs, side_of, grad):
    x, h, zu, zqk, zv, pm, qkn, o, lse, factors, m = saved
    d_a, d_b, dgl, dpm, do = _merge_bwd(dxo, p["w_out"], factors, p["w_pool_branch"], p["w_attn_branch"], "merge_bwd")
    grad("w_out", _row_blocks_of(_mm_tn(m, dxo, "mix_dwout")))
    grad("w_pool_branch", _mm_tn(pm, d_a, "pool_branch_dw", col_blocks=N_CHIPS))
    grad("w_attn_branch", _mm_tn(o, d_b, "attn_branch_dw", col_blocks=N_CHIPS))
    du, d_pool_w, d_pool_scale = _pool_bwd(zu, dpm, p["pool_w"], p["pool_scale"], "pool_bwd")
    grad("pool_w", d_pool_w)
    grad("pool_scale", d_pool_scale)
    dq, dkc, dkp, dvc, dvp, dsink = _attn_bwd(qkn, zv, p["sinks"], do, o, lse, "attn_bwd", side_of("attn_bwd"))
    dzqk, dv, dgqk = _qk_bwd(dq, (dkc, dkp), (dvc, dvp), zqk, p["gqk"], *tabs, "qk_bwd")
    grad("q_norm", dgqk[0, :ATTN_DIM].reshape(N_Q_HEADS, HEAD_DIM).sum(axis=0))
    grad("k_norm", dgqk[0, ATTN_DIM:].reshape(KV_DIM // HEAD_DIM, HEAD_DIM).sum(axis=0))
    grad("sinks", -dsink[:, 0])
    dz = [du, dzqk, dv, dgl]
    grad("w_in", _blocks_from_full("w_in", _mm_tn_parts(h, dz, "mix_dwin")).astype(WIRE_DT))
    dx, d_ln = _mm_nt_norm_bwd(dz, p["w_in"], x, p["ln_mix"], dxo, "mix_dh_norm_bwd", side_of("mix_dh_norm_bwd"))
    grad("ln_mix", d_ln[0])
    return dx


class _NoComm:
    def __init__(self, layers):
        self.layers, self.grads = layers, [dict() for _ in layers]

    def weight(self, l, name):
        return self.layers[l][name]

    def side(self, phase, l, host):
        return None

    def grad(self, l, name, value):
        self.grads[l][name] = value


class _Layer:
    def __init__(self, hooks, l):
        self.hooks, self.l, self.got = hooks, l, {}

    def __getitem__(self, name):
        if name not in self.got:
            self.got[name] = self.hooks.weight(self.l, name)
        return self.got[name]


def _local_step(x, tgt, n_layers, hooks):
    T = x.shape[0]
    tabs = _rope_tables(T)
    saved, params = [], []
    for l in range(n_layers):
        p = _Layer(hooks, l)
        side_of = functools.partial(hooks.side, "fwd", l)
        x, s1 = _ffn_forward(x, p, "ffn1", side_of)
        x, s2 = _mixer_forward(x, p, tabs, side_of)
        x, s3 = _ffn_forward(x, p, "ffn2", side_of)
        saved.append((s1, s2, s3))
        params.append(p)
    dx, loss = _loss_head(x, tgt, "loss_head")
    for l in reversed(range(n_layers)):
        p = params[l]
        s1, s2, s3 = saved[l]
        side_of = functools.partial(hooks.side, "bwd", l)
        grad = functools.partial(hooks.grad, l)
        dx = _ffn_backward(dx, s3, p, "ffn2", side_of, grad)
        dx = _mixer_backward(dx, s2, p, tabs, side_of, grad)
        dx = _ffn_backward(dx, s1, p, "ffn1", side_of, grad)
    return loss, dx


def _full_from_blocks(name, blocks):
    if name in COL_SHARDED:
        return jnp.transpose(blocks, (1, 0, 2)).reshape(blocks.shape[1], N_CHIPS * blocks.shape[2])
    return blocks.reshape(N_CHIPS * blocks.shape[1], blocks.shape[2])


def _blocks_from_full(name, full):
    K, N = full.shape
    if name in COL_SHARDED:
        return jnp.transpose(full.reshape(K, N_CHIPS, N // N_CHIPS), (1, 0, 2))
    return full.reshape(N_CHIPS, K // N_CHIPS, N)


JOBS = {"a": ("w_ffn1_gu", "w_ffn1_down"), "b": ("w_in", "w_pool_branch", "w_attn_branch", "w_out"),
        "c": ("w_ffn2_gu", "w_ffn2_down")}
GATHER_PLAN = {"ffn1_up": ("ici", "b", JOBS["b"], 0), "ffn1_down": ("d2d", "b", JOBS["b"], 0),
               "mix_in": ("ici", "c", JOBS["c"][:1], 0), "attn_fwd": ("ici", "c", JOBS["c"][1:], 0),
               "merge_fwd": ("d2d", "c", JOBS["c"], 0),
               "ffn2_up": ("ici", "a", JOBS["a"], 1), "ffn2_down": ("d2d", "a", JOBS["a"], 1)}
REDUCE_PLAN = {"ffn2_down_bwd": ("sibling", "a", 1), "ffn2_dwgu": ("chip", "a", 1),
               "ffn2_dh_norm_bwd": ("sibling", "c", 0), "attn_bwd": ("chip", "c", 0),
               "mix_dh_norm_bwd": ("sibling", "b", 0), "ffn1_down_bwd": ("chip", "b", 0)}
SHARE_HOST = "ffn1_dwgu"
SMALL_HOST = "ffn2_dwd"
LAST_GRAD = "ln_ffn1"


class _Exchange:
    def __init__(self, shards, small, place, n_layers):
        self.shards, self.small, self.place, self.n_layers = shards, small, place, n_layers
        first, wides = [shards[n] for n in JOBS["a"]], [_is_wide(n) for n in JOBS["a"]]
        got = _run_side(_gather_ici_side(first, wides, 0), "gather_ici")
        got = _run_side(_gather_d2d_side(first, wides, got, 0), "gather_d2d")
        self.blocks = {(n, 0): g for n, g in zip(JOBS["a"], got)}
        self.landed = {}
        self.handed = []
        self.acc = {n: lax.empty(shards[n].shape, F32) for n in BIG}
        self.grads = [dict() for _ in range(n_layers)]
        self.reduce = {}
        self.summed = set()
        self.unshared, self.sharing = [], None
        self.small_sides, self.small_waiting = {}, None

    def weight(self, l, name):
        if name not in BIG:
            return self.small(l)[name]
        for names, layer, done in self.handed:
            self.blocks.update({(n, layer): g for n, g in zip(names, done.outs)})
        self.handed.clear()
        blocks = self.blocks.pop((name, l))
        return blocks if name in USED_AS_BLOCKS + WIDE else _full_from_blocks(name, blocks)

    def _gather_side(self, l, host):
        step, job, names, ahead = GATHER_PLAN[host]
        layer = l + ahead
        if layer >= self.n_layers:
            return None
        if step == "ici":
            side = _gather_ici_side([self.shards[n] for n in names], [_is_wide(n) for n in names], layer)
            self.landed.setdefault((job, layer), []).append((names, side))
            return side
        names, gathered = JOBS[job], {}
        for part_names, side in self.landed.pop((job, layer)):
            gathered.update(zip(part_names, side.outs))
        done = _gather_d2d_side([self.shards[n] for n in names], [_is_wide(n) for n in names],
                                [gathered[n] for n in names], layer)
        self.handed.append((names, layer, done))
        return done

    def grad(self, l, name, value):
        self.grads[l][name] = value
        if name == LAST_GRAD and l > 0:
            packed, self.small_spans = _pack_small([self.grads[l][n] for n in SMALL])
            self.small_sides[l] = _small_side(packed)
            self.small_waiting = l

    def reduced_small(self, loss_part):
        packed, spans = _pack_small([self.grads[0][n] for n in SMALL] + [loss_part])
        self.small_sides[0] = _small_side(packed)
        _run_side(self.small_sides[0], "all_reduce_small")
        shapes = [self.grads[0][n].shape for n in SMALL]
        per_layer = []
        for l in range(self.n_layers):
            side = self.small_sides[l]
            summed = _small_sum(side.outs[0], side.ins[0], self.place, "small_sum")
            per_layer.append(_unpack_small(summed, spans, shapes + [(1, 1)] * (l == 0)))
        loss = per_layer[0][-1][0, 0]
        return {n: jnp.stack([vals[k] for vals in per_layer]) for k, n in enumerate(SMALL)}, loss

    def _reduce_side(self, l, host):
        step, job, ahead = REDUCE_PLAN[host]
        layer = l + ahead
        if layer >= self.n_layers:
            return None
        return self._reduce_step(step, job, layer)

    def _reduce_step(self, step, job, layer):
        if step == "sibling":
            st = self.reduce[(job, layer)] = dict(gm=[self.grads[layer][n] for n in JOBS[job]],
                                                  wide=[_is_wide(n) for n in JOBS[job]])
            st["sibling"] = _reduce_sibling_side(st["gm"], st["wide"])
            return st["sibling"]
        st = self.reduce[(job, layer)]
        st["part"] = [_pair_sum(g, r, wd, self.place, "grad_pair_sum")
                      for g, r, wd in zip(st["gm"], st["sibling"].outs, st["wide"])]
        st["chip"] = _reduce_chip_side(st["part"], st["wide"])
        return st["chip"]

    def _chip_sums(self):
        if self.sharing is not None:
            self.acc.update(zip(BIG, self.sharing.outs))
            self.sharing = None
        for (job, layer), st in self.reduce.items():
            if (job, layer) not in self.summed and "chip" in st and st["chip"].outs is not None:
                self.summed.add((job, layer))
                for n, slots, part, wd in zip(JOBS[job], st["chip"].outs, st["part"], st["wide"]):
                    self.acc[n] = _chip_sum(slots, part, wd, self.place, self.acc[n], layer, "grad_chip_sum")
                    self.unshared.append((BIG.index(n), layer))

    def _share(self):
        side = _share_side([self.acc[n] for n in BIG], self.unshared)
        self.unshared = []
        return side

    def side(self, phase, l, host):
        if phase == "fwd":
            return self._gather_side(l, host) if host in GATHER_PLAN else None
        self._chip_sums()
        if host == SMALL_HOST and self.small_waiting is not None:
            side, self.small_waiting = self.small_sides[self.small_waiting], None
            return side
        if host == SHARE_HOST and self.unshared:
            self.sharing = self._share()
            return self.sharing
        return self._reduce_side(l, host) if host in REDUCE_PLAN else None

    def reduced(self):
        _run_side(self._reduce_step("sibling", "a", 0), "grad_sibling_exchange")
        _run_side(self._reduce_step("chip", "a", 0), "grad_chip_exchange")
        self._chip_sums()
        return dict(zip(BIG, _run_side(self._share(), "grad_sibling_share")))


def _pack_small(parts):
    rows, spans, lo = [], [], 0
    for v in parts:
        flat = v.reshape(-1)
        nrow = -(-flat.shape[0] // LANES)
        flat = jnp.pad(flat, (0, nrow * LANES - flat.shape[0]))
        rows.append(flat.reshape(nrow, LANES))
        spans.append((lo, nrow))
        lo += nrow
    pad = -lo % 8
    if pad:
        rows.append(jnp.zeros((pad, LANES), F32))
    return jnp.concatenate(rows, axis=0), spans


def _unpack_small(packed, spans, shapes):
    out = []
    for (lo, nrow), shape in zip(spans, shapes):
        size = 1
        for s in shape:
            size *= s
        out.append(packed[lo:lo + nrow].reshape(-1)[:size].reshape(shape))
    return out


def kernel(x, ln_ffn1, w_ffn1_gu, w_ffn1_down, ln_mix, w_in, pool_w, pool_scale, w_pool_branch, q_norm, k_norm, sinks, w_attn_branch, w_out, ln_ffn2, w_ffn2_gu, w_ffn2_down, loss_target, m_ln_ffn1, m_w_ffn1_gu, m_w_ffn1_down, m_ln_mix, m_w_in, m_pool_w, m_pool_scale, m_w_pool_branch, m_q_norm, m_k_norm, m_sinks, m_w_attn_branch, m_w_out, m_ln_ffn2, m_w_ffn2_gu, m_w_ffn2_down, v_ln_ffn1, v_w_ffn1_gu, v_w_ffn1_down, v_ln_mix, v_w_in, v_pool_w, v_pool_scale, v_w_pool_branch, v_q_norm, v_k_norm, v_sinks, v_w_attn_branch, v_w_out, v_ln_ffn2, v_w_ffn2_gu, v_w_ffn2_down):
    w = dict(ln_ffn1=ln_ffn1, w_ffn1_gu=w_ffn1_gu, w_ffn1_down=w_ffn1_down, ln_mix=ln_mix, w_in=w_in, pool_w=pool_w,
             pool_scale=pool_scale, w_pool_branch=w_pool_branch, q_norm=q_norm, k_norm=k_norm, sinks=sinks,
             w_attn_branch=w_attn_branch, w_out=w_out, ln_ffn2=ln_ffn2, w_ffn2_gu=w_ffn2_gu, w_ffn2_down=w_ffn2_down)
    mom = dict(ln_ffn1=m_ln_ffn1, w_ffn1_gu=m_w_ffn1_gu, w_ffn1_down=m_w_ffn1_down, ln_mix=m_ln_mix, w_in=m_w_in,
               pool_w=m_pool_w, pool_scale=m_pool_scale, w_pool_branch=m_w_pool_branch, q_norm=m_q_norm, k_norm=m_k_norm,
               sinks=m_sinks, w_attn_branch=m_w_attn_branch, w_out=m_w_out, ln_ffn2=m_ln_ffn2, w_ffn2_gu=m_w_ffn2_gu,
               w_ffn2_down=m_w_ffn2_down)
    var = dict(ln_ffn1=v_ln_ffn1, w_ffn1_gu=v_w_ffn1_gu, w_ffn1_down=v_w_ffn1_down, ln_mix=v_ln_mix, w_in=v_w_in,
               pool_w=v_pool_w, pool_scale=v_pool_scale, w_pool_branch=v_w_pool_branch, q_norm=v_q_norm, k_norm=v_k_norm,
               sinks=v_sinks, w_attn_branch=v_w_attn_branch, w_out=v_w_out, ln_ffn2=v_ln_ffn2, w_ffn2_gu=v_w_ffn2_gu,
               w_ffn2_down=v_w_ffn2_down)
    L = ln_ffn1.shape[0]

    def small(l):
        return dict(ln_ffn1=ln_ffn1[l], ln_mix=ln_mix[l], ln_ffn2=ln_ffn2[l], pool_w=pool_w[l].astype(CDT),
                    pool_scale=pool_scale[l], sinks=sinks[l],
                    gqk=jnp.concatenate([jnp.tile(q_norm[l], N_Q_HEADS), jnp.tile(k_norm[l], KV_DIM // HEAD_DIM)]).reshape(1, QK_DIM))

    place = jnp.stack([lax.axis_index("c"), 2 * lax.axis_index("x") + lax.axis_index("y")]).astype(jnp.int32)
    hooks = _Exchange({n: w[n].astype(CDT) for n in BIG}, small, place, L)
    loss_part, grad_x = _local_step(x[0], loss_target[0], L, hooks)
    g_big = hooks.reduced()
    g_small, loss = hooks.reduced_small(loss_part)
    g_small = {n: v.reshape(w[n].shape) for n, v in g_small.items()}

    grad_out, delta, new_m, new_v = {}, {}, {}, {}
    for n in BIG:
        shape = w[n].shape
        flat = (shape[0] * shape[1], shape[2])
        go, d, nm, nv = _adamw(w[n].reshape(flat), g_big[n].reshape(flat), mom[n].reshape(flat), var[n].reshape(flat), "adamw")
        grad_out[n], delta[n], new_m[n], new_v[n] = go.reshape(shape), d.reshape(shape), nm.reshape(shape), nv.reshape(shape)
    pw, _ = _pack_small([w[n] for n in SMALL])
    pg, sp = _pack_small([g_small[n] for n in SMALL])
    pm_, _ = _pack_small([mom[n] for n in SMALL])
    pv, _ = _pack_small([var[n] for n in SMALL])
    _, d, nm, nv = _adamw(pw, pg, pm_, pv, "adamw_small")
    shapes = [w[n].shape for n in SMALL]
    for n, dv, mv, vv in zip(SMALL, _unpack_small(d, sp, shapes), _unpack_small(nm, sp, shapes), _unpack_small(nv, sp, shapes)):
        grad_out[n], delta[n], new_m[n], new_v[n] = g_small[n], dv, mv, vv

    return (loss, grad_x[None], *[grad_out[n] for n in WEIGHTS], *[delta[n] for n in WEIGHTS],
            *[new_m[n] for n in WEIGHTS], *[new_v[n] for n in WEIGHTS])
```

```python
import functools
import math

import jax
import jax.numpy as jnp
from jax import lax
from jax.experimental import pallas as pl
from jax.experimental.pallas import tpu as pltpu

F32 = jnp.float32
CDT = jnp.bfloat16
WIRE_DT = jnp.bfloat16

D_MODEL = 1024
POOL_WINDOWS = (2, 4, 8, 16)
POOL_WMAX = 16
GROUP = 128
POOL_DIM = 512
HEAD_DIM = 64
N_Q_HEADS = 8
ATTN_DIM = 512
KV_DIM = 128
QK_DIM = ATTN_DIM + KV_DIM
GATE_DIM = 2 * D_MODEL
BLOCK = 128
ROPE_THETA = 500000.0
ROT_DIM = 16
EPS = 1e-6
ATTN_SCALE = HEAD_DIM ** -0.5

ADAM_LR = 0.001
ADAM_B1 = 0.9
ADAM_B2 = 0.999
ADAM_EPS = 1e-08
ADAM_WD = 0.01
ADAM_STEP = 10

N_CHIPS = 4
N_DEV = 8
LANES = 128
VMEM_LIMIT_BYTES = 48 * 1024 * 1024

MESH = pl.DeviceIdType.MESH
ANY = pl.BlockSpec(memory_space=pl.ANY)

BIG = ("w_ffn1_gu", "w_ffn1_down", "w_in", "w_pool_branch", "w_attn_branch", "w_out", "w_ffn2_gu", "w_ffn2_down")
COL_SHARDED = ("w_ffn1_gu", "w_in", "w_pool_branch", "w_attn_branch", "w_ffn2_gu")
USED_AS_BLOCKS = ("w_pool_branch", "w_attn_branch")
WIDE = ("w_ffn1_gu", "w_ffn2_gu")
SMALL = ("ln_ffn1", "ln_mix", "pool_w", "pool_scale", "q_norm", "k_norm", "sinks", "ln_ffn2")
WEIGHTS = ("ln_ffn1", "w_ffn1_gu", "w_ffn1_down", "ln_mix", "w_in", "pool_w", "pool_scale", "w_pool_branch",
           "q_norm", "k_norm", "sinks", "w_attn_branch", "w_out", "ln_ffn2", "w_ffn2_gu", "w_ffn2_down")


def _tile(n, target, mult=8):
    if n <= target:
        return n
    for t in range(target - target % mult, 0, -mult):
        if n % t == 0:
            return t
    raise ValueError((n, target, mult))


def _params(*sem):
    return pltpu.CompilerParams(dimension_semantics=sem, vmem_limit_bytes=VMEM_LIMIT_BYTES)


def _sigmoid(v):
    return 0.5 * jnp.tanh(0.5 * v) + 0.5


def _dot(a, b):
    return jnp.dot(a, b, preferred_element_type=F32)


def _dot_nt(a, b):
    return lax.dot_general(a, b, (((1,), (1,)), ((), ())), preferred_element_type=F32)


def _dot_tn(a, b):
    return lax.dot_general(a, b, (((0,), (0,)), ((), ())), preferred_element_type=F32)


class _Side:
    def __init__(self, ins, out_shapes, n_sems, issue, aliases=None):
        self.ins, self.out_shapes, self.n_sems, self.issue = list(ins), list(out_shapes), n_sems, issue
        self.aliases = dict(aliases or {})
        self.outs = None


def _pcall(body, name, grid, in_specs, out_specs, out_shape, args, dims, side=None, scratch=()):
    scratch = list(scratch)
    if side is None:
        return pl.pallas_call(body, name=name, grid=grid, in_specs=in_specs, out_specs=out_specs, out_shape=out_shape,
                              scratch_shapes=scratch, compiler_params=_params(*dims))(*args)
    n_in, n_out, s_in, s_out = len(in_specs), len(out_specs), len(side.ins), len(side.out_shapes)

    def wrapped(*refs):
        main_in, side_in = refs[:n_in], refs[n_in:n_in + s_in]
        main_out = refs[n_in + s_in:n_in + s_in + n_out]
        side_out = refs[n_in + s_in + n_out:n_in + s_in + n_out + s_out]
        rest = refs[n_in + s_in + n_out + s_out:]
        main_scratch, (ssem, rsem) = rest[:len(scratch)], rest[len(scratch):]
        ids = [pl.program_id(ax) for ax in range(len(grid))]
        first = functools.reduce(jnp.logical_and, [i == 0 for i in ids])
        last = functools.reduce(jnp.logical_and, [i == g - 1 for i, g in zip(ids, grid)])

        @pl.when(first)
        def _():
            for cp in side.issue(side_in, side_out, ssem, rsem):
                cp.start()

        body(*main_in, *main_out, *main_scratch)

        @pl.when(last)
        def _():
            cps = side.issue(side_in, side_out, ssem, rsem)
            for cp in cps:
                cp.wait_recv()
            for cp in cps:
                cp.wait_send()

    outs = pl.pallas_call(
        wrapped, name=name, grid=grid, in_specs=list(in_specs) + [ANY] * s_in, out_specs=list(out_specs) + [ANY] * s_out,
        out_shape=list(out_shape) + side.out_shapes,
        input_output_aliases={n_in + i: n_out + o for i, o in side.aliases.items()},
        scratch_shapes=scratch + [pltpu.SemaphoreType.DMA((side.n_sems,))] * 2,
        compiler_params=_params(*["arbitrary"] * len(grid)),
    )(*args, *side.ins)
    side.outs = list(outs[n_out:])
    return list(outs[:n_out])


def _run_side(side, name):
    s_in = len(side.ins)

    def body(*refs):
        ssem, rsem = refs[s_in + len(side.out_shapes):]
        cps = side.issue(refs[:s_in], refs[s_in:s_in + len(side.out_shapes)], ssem, rsem)
        for cp in cps:
            cp.start()
        for cp in cps:
            cp.wait_recv()
        for cp in cps:
            cp.wait_send()

    side.outs = list(pl.pallas_call(
        body, name=name, in_specs=[ANY] * s_in, out_specs=[ANY] * len(side.out_shapes), out_shape=side.out_shapes,
        input_output_aliases=side.aliases, scratch_shapes=[pltpu.SemaphoreType.DMA((side.n_sems,))] * 2,
    )(*side.ins))
    return side.outs


def _loss_head(y, tgt, name):
    T, Dm = y.shape
    tm = _tile(T, 512)

    def body(y_ref, t_ref, dy_ref, loss_ref):
        @pl.when(pl.program_id(0) == 0)
        def _():
            loss_ref[...] = jnp.zeros_like(loss_ref)

        diff = y_ref[...] - t_ref[...]
        dy_ref[...] = diff * (1.0 / Dm)
        part = jnp.sum(jnp.mean(diff * diff, axis=-1, keepdims=True), axis=0, keepdims=True)
        loss_ref[...] += 0.5 * part

    row = pl.BlockSpec((tm, Dm), lambda i: (i, 0))
    one = pl.BlockSpec((1, 1), lambda i: (0, 0))
    return pl.pallas_call(
        body, name=name, grid=(T // tm,),
        in_specs=[row, row], out_specs=[row, one],
        out_shape=[jax.ShapeDtypeStruct((T, Dm), F32), jax.ShapeDtypeStruct((1, 1), F32)],
        compiler_params=_params("arbitrary"),
    )(y, tgt)


def _mm_nn(a, b, name, out_dtype, res=None, scale=1.0, tm_target=512, side=None):
    M, K = a.shape
    N = b.shape[1]
    tm = _tile(M, tm_target)

    def body(a_ref, b_ref, *rest):
        acc = _dot(a_ref[...].astype(CDT), b_ref[...])
        if res is None:
            (o_ref,) = rest
        else:
            r_ref, o_ref = rest
            acc = r_ref[...] + scale * acc
        o_ref[...] = acc.astype(o_ref.dtype)

    in_specs = [pl.BlockSpec((tm, K), lambda i: (i, 0)), pl.BlockSpec((K, N), lambda i: (0, 0))]
    args = [a, b]
    if res is not None:
        in_specs.append(pl.BlockSpec((tm, N), lambda i: (i, 0)))
        args.append(res)
    return _pcall(body, name, (M // tm,), in_specs, [pl.BlockSpec((tm, N), lambda i: (i, 0))],
                  [jax.ShapeDtypeStruct((M, N), out_dtype)], args, ("parallel",), side)[0]


def _mm_tn(x, dy, name, scale=1.0, col_blocks=1, tn_target=1664, tm_target=1408, tk_target=1024, side=None):
    T, M = x.shape
    split = dy.ndim == 3
    Nh = dy.shape[-1]
    N = 2 * Nh if split else Nh
    nb = N // col_blocks
    whole = col_blocks > 1 and not split and N <= tn_target
    tm = _tile(M, tm_target, LANES)
    tn = N if whole else _tile(math.gcd(Nh, nb), tn_target, LANES)
    tk = _tile(T, tk_target)
    nk = T // tk
    njh, njb = Nh // tn, max(nb // tn, 1)

    def body(x_ref, dy_ref, o_ref, acc_ref):
        k = pl.program_id(2)

        @pl.when(k == 0)
        def _():
            acc_ref[...] = jnp.zeros_like(acc_ref)

        acc_ref[...] += _dot_tn(x_ref[...].astype(CDT), dy_ref[...].astype(CDT))

        @pl.when(k == nk - 1)
        def _():
            res = (acc_ref[...] if scale == 1.0 else scale * acc_ref[...]).astype(o_ref.dtype)
            if whole:
                for b in range(col_blocks):
                    o_ref[b] = res[:, b * nb:(b + 1) * nb]
            else:
                o_ref[...] = res

    if split:
        dy_spec = pl.BlockSpec((None, tk, tn), lambda i, j, k: (j // njh, k, j % njh))
    else:
        dy_spec = pl.BlockSpec((tk, tn), lambda i, j, k: (k, j))
    if col_blocks == 1:
        out_spec, out_dims = pl.BlockSpec((tm, tn), lambda i, j, k: (i, j)), (M, N)
    elif whole:
        out_spec, out_dims = pl.BlockSpec((col_blocks, tm, nb), lambda i, j, k: (0, i, 0)), (col_blocks, M, nb)
    else:
        out_spec, out_dims = pl.BlockSpec((None, tm, tn), lambda i, j, k: (j // njb, i, j % njb)), (col_blocks, M, nb)
    return _pcall(body, name, (M // tm, N // tn, nk), [pl.BlockSpec((tk, tm), lambda i, j, k: (k, i)), dy_spec],
                  [out_spec], [jax.ShapeDtypeStruct(out_dims, WIRE_DT)], (x, dy), ("parallel", "parallel", "arbitrary"),
                  side, [pltpu.VMEM((tm, tn), F32)])[0]


def _mm_tn_parts(x, parts, name):
    T, M = x.shape
    widths = [p.shape[1] for p in parts]
    N = sum(widths)
    tk = _tile(T, 512)

    def body(x_ref, *refs):
        o_ref = refs[-1]

        @pl.when(pl.program_id(0) == 0)
        def _():
            o_ref[...] = jnp.zeros_like(o_ref)

        parts_cat = jnp.concatenate([p_ref[...].astype(CDT) for p_ref in refs[:-1]], axis=1)
        o_ref[...] += _dot_tn(x_ref[...].astype(CDT), parts_cat)

    return pl.pallas_call(
        body, name=name, grid=(T // tk,),
        in_specs=[pl.BlockSpec((tk, M), lambda k: (k, 0))] + [pl.BlockSpec((tk, wd), lambda k: (k, 0)) for wd in widths],
        out_specs=pl.BlockSpec((M, N), lambda k: (0, 0)),
        out_shape=jax.ShapeDtypeStruct((M, N), F32), compiler_params=_params("arbitrary"),
    )(x, *parts)


def _rmsnorm_rows(x_ref, g_ref):
    xv = x_ref[...]
    r = lax.rsqrt(jnp.mean(xv * xv, axis=-1, keepdims=True) + EPS)
    return (xv * r * g_ref[...]).astype(CDT)


def _ffn_up(x, ln, wgu, name, side=None):
    T, Dm = x.shape
    Fd = wgu.shape[1] // 2
    tm = _tile(T, 256)

    def body(x_ref, ln_ref, wg_ref, wu_ref, h_ref, gu_ref, a_ref):
        hv = _rmsnorm_rows(x_ref, ln_ref)
        h_ref[...] = hv
        g = _dot(hv, wg_ref[...])
        u = _dot(hv, wu_ref[...])
        sg = _sigmoid(g)
        silu = g * sg
        a_ref[...] = (silu * u).astype(a_ref.dtype)
        gu_ref[0] = (0.5 * u * (sg * (1.0 + g * (1.0 - sg)))).astype(gu_ref.dtype)
        gu_ref[1] = (0.5 * silu).astype(gu_ref.dtype)

    row = pl.BlockSpec((tm, Dm), lambda i: (i, 0))
    return _pcall(
        body, name, (T // tm,),
        [row, pl.BlockSpec((1, Dm), lambda i: (0, 0)),
         pl.BlockSpec((Dm, Fd), lambda i: (0, 0), pipeline_mode=pl.Buffered(1)),
         pl.BlockSpec((Dm, Fd), lambda i: (0, 1), pipeline_mode=pl.Buffered(1))],
        [row, pl.BlockSpec((2, tm, Fd), lambda i: (0, i, 0)), pl.BlockSpec((tm, Fd), lambda i: (i, 0))],
        [jax.ShapeDtypeStruct((T, Dm), CDT), jax.ShapeDtypeStruct((2, T, Fd), CDT), jax.ShapeDtypeStruct((T, Fd), CDT)],
        (x, ln.reshape(1, Dm), wgu, wgu), ("parallel",), side)


def _ffn_down_bwd(dxo, wd, gu, name, side=None):
    T, Dm = dxo.shape
    Fd = wd.shape[0]
    tm = _tile(T, 256)

    def body(dx_ref, wd_ref, gu_ref, dgu_ref):
        da = _dot_nt(dx_ref[...].astype(CDT), wd_ref[...])
        dgu_ref[0] = (da * gu_ref[0].astype(F32)).astype(dgu_ref.dtype)
        dgu_ref[1] = (da * gu_ref[1].astype(F32)).astype(dgu_ref.dtype)

    gu_spec = pl.BlockSpec((2, tm, Fd), lambda i: (0, i, 0))
    return _pcall(
        body, name, (T // tm,),
        [pl.BlockSpec((tm, Dm), lambda i: (i, 0)),
         pl.BlockSpec((Fd, Dm), lambda i: (0, 0), pipeline_mode=pl.Buffered(1)), gu_spec],
        [gu_spec], [jax.ShapeDtypeStruct((2, T, Fd), CDT)],
        (dxo, wd, gu), ("parallel",), side)[0]


def _mm_nt_norm_bwd(a_parts, b, x, g, dres, name, side=None):
    T, Dm = x.shape
    tm = _tile(T, 512)

    def b_cols(b_ref, lo, wd):
        if b.ndim == 2:
            return [(0, wd, b_ref[:, lo:lo + wd])]
        kb = b.shape[2]
        return [(j * kb - lo, kb, b_ref[j]) for j in range(lo // kb, (lo + wd) // kb)]

    def body(*refs):
        a_refs, (b_ref, x_ref, g_ref, dres_ref, dx_ref, dg_ref) = refs[:len(a_parts)], refs[len(a_parts):]

        @pl.when(pl.program_id(0) == 0)
        def _():
            dg_ref[...] = jnp.zeros_like(dg_ref)

        dh, lo = None, 0
        if b.ndim == 2 and len(a_parts) > 1 and all(p.ndim == 2 for p in a_parts):
            dh = _dot_nt(jnp.concatenate([a_ref[...].astype(CDT) for a_ref in a_refs], axis=1), b_ref[...])
            a_refs = ()
        for a_ref, part in zip(a_refs, a_parts):
            slabs = [a_ref] if part.ndim == 2 else [a_ref.at[s_] for s_ in range(part.shape[0])]
            for slab in slabs:
                for off, wd, bv in b_cols(b_ref, lo, part.shape[-1]):
                    term = _dot_nt(slab[:, off:off + wd].astype(CDT), bv)
                    dh = term if dh is None else dh + term
                lo += part.shape[-1]
        xv = x_ref[...]
        r = lax.rsqrt(jnp.mean(xv * xv, axis=-1, keepdims=True) + EPS)
        xh = xv * r
        dg_ref[...] += jnp.sum(dh * xh, axis=0, keepdims=True)
        dxh = dh * g_ref[...]
        dx_ref[...] = dres_ref[...] + r * (dxh - xh * jnp.mean(dxh * xh, axis=-1, keepdims=True))

    row = pl.BlockSpec((tm, Dm), lambda i: (i, 0))
    vec = pl.BlockSpec((1, Dm), lambda i: (0, 0))
    a_specs = [pl.BlockSpec((tm, p.shape[1]), lambda i: (i, 0)) if p.ndim == 2 else
               pl.BlockSpec((p.shape[0], tm, p.shape[2]), lambda i: (0, i, 0)) for p in a_parts]
    b_spec = pl.BlockSpec(b.shape, lambda i: (0,) * b.ndim, pipeline_mode=pl.Buffered(1))
    return _pcall(body, name, (T // tm,), a_specs + [b_spec, row, vec, row], [row, vec],
                  [jax.ShapeDtypeStruct((T, Dm), F32), jax.ShapeDtypeStruct((1, Dm), F32)],
                  (*a_parts, b, x, g.reshape(1, Dm), dres), ("arbitrary",), side)


def _mm_in(x, ln, w_in, gqk, cos_t, sin_t, pool_w, pool_scale, name, side=None):
    T, Dm = x.shape
    tm = _tile(T, 256, POOL_WMAX)
    widths = (POOL_DIM, QK_DIM, KV_DIM, GATE_DIM)

    def body(x_ref, ln_ref, w_ref, g_ref, c_ref, s_ref, pw_ref, sc_ref,
             h_ref, zu_ref, zqk_ref, zv_ref, zg_ref, qkn_ref, pm_ref, halo_ref):
        i = pl.program_id(0)
        hv = _rmsnorm_rows(x_ref, ln_ref)
        h_ref[...] = hv
        z = _dot(hv, w_ref[...])
        lo = 0
        for o_ref, wd in zip((zu_ref, zqk_ref, zv_ref, zg_ref), widths):
            o_ref[...] = z[:, lo:lo + wd].astype(o_ref.dtype)
            lo += wd
        first, low = _lane_masks()
        cosv, sinv = c_ref[...], s_ref[...]
        for c in range(QK_DIM // LANES):
            sl = slice(c * LANES, (c + 1) * LANES)
            xv = z[:, POOL_DIM + c * LANES:POOL_DIM + (c + 1) * LANES]
            r = lax.rsqrt(_head_mean(xv * xv, first) + EPS)
            xn = xv * r * g_ref[:, sl]
            qkn_ref[:, sl] = (xn * cosv + _rope_partner(xn, low) * sinv).astype(qkn_ref.dtype)
        @pl.when(i == 0)
        def _():
            halo_ref[...] = jnp.zeros_like(halo_ref)

        zu = z[:, :POOL_DIM]
        ext = jnp.concatenate([halo_ref[...], zu], axis=0)
        halo_ref[...] = zu[tm - POOL_WMAX:, :]
        pos = i * tm + lax.broadcasted_iota(jnp.int32, (tm, 1), 0)
        ys = [_dot(_window_mean_minus_token(ext, zu, g, w, pos).astype(CDT), pw_ref[g])
              for g, w in enumerate(POOL_WINDOWS)]
        pm_ref[...] = (jnp.concatenate(ys, axis=1) * sc_ref[...]).astype(pm_ref.dtype)

    row = pl.BlockSpec((tm, Dm), lambda i: (i, 0))
    tab = pl.BlockSpec((tm, LANES), lambda i: (i, 0))
    return _pcall(body, name, (T // tm,),
                  [row, pl.BlockSpec((1, Dm), lambda i: (0, 0)),
                   pl.BlockSpec(w_in.shape, lambda i: (0, 0), pipeline_mode=pl.Buffered(1)),
                   pl.BlockSpec((1, QK_DIM), lambda i: (0, 0)), tab, tab,
                   pl.BlockSpec(pool_w.shape, lambda i: (0, 0, 0)), pl.BlockSpec((1, POOL_DIM), lambda i: (0, 0))],
                  [row] + [pl.BlockSpec((tm, wd), lambda i: (i, 0)) for wd in widths + (QK_DIM, POOL_DIM)],
                  [jax.ShapeDtypeStruct((T, Dm), CDT)]
                  + [jax.ShapeDtypeStruct((T, wd), dt) for wd, dt in zip(widths, (F32, F32, F32, CDT))]
                  + [jax.ShapeDtypeStruct((T, QK_DIM), CDT), jax.ShapeDtypeStruct((T, POOL_DIM), CDT)],
                  (x, ln.reshape(1, Dm), w_in, gqk, cos_t, sin_t, pool_w, pool_scale.reshape(1, POOL_DIM)),
                  ("arbitrary",), side, [pltpu.VMEM((POOL_WMAX, POOL_DIM), F32)])


def _window_mean_minus_token(ext, u, g, w, pos):
    sl = slice(g * GROUP, (g + 1) * GROUP)
    s = ext[:, sl]
    span = 1
    while span < w:
        s = s + pltpu.roll(s, span, axis=0)
        span *= 2
    cnt = jnp.minimum(pos + 1, w).astype(F32)
    return s[POOL_WMAX:, :] / cnt - u[:, sl]


def _pool_bwd(zu, dpm, pool_w, scale, name):
    T = zu.shape[0]
    tm = _tile(T, 512, POOL_WMAX)
    hb = tm // POOL_WMAX
    nsteps = T // tm
    ext_rows = tm + POOL_WMAX

    def body(u_ref, halo_ref, dpm_ref, dnext_ref, pw_ref, sc_ref, du_ref, dpw_ref, dsc_ref):
        i = pl.program_id(0)

        @pl.when(i == 0)
        def _():
            dpw_ref[...] = jnp.zeros_like(dpw_ref)
            dsc_ref[...] = jnp.zeros_like(dsc_ref)

        u = u_ref[...]
        halo = jnp.where(i > 0, halo_ref[...], 0.0)
        ext = jnp.concatenate([halo, u], axis=0)
        dpm_t = dpm_ref[...].astype(F32)
        dnext = jnp.where(i < nsteps - 1, dnext_ref[...].astype(F32), 0.0)
        dext = jnp.concatenate([dpm_t, dnext], axis=0)
        sc = sc_ref[...]
        pos = i * tm + lax.broadcasted_iota(jnp.int32, (tm, 1), 0)
        pos_ext = i * tm + lax.broadcasted_iota(jnp.int32, (ext_rows, 1), 0)
        dus, dscs = [], []
        for g, w in enumerate(POOL_WINDOWS):
            sl = slice(g * GROUP, (g + 1) * GROUP)
            dc = _window_mean_minus_token(ext, u, g, w, pos).astype(CDT)
            y = _dot(dc, pw_ref[g])
            dscs.append(jnp.sum(dpm_t[:, sl] * y, axis=0, keepdims=True))
            dy_ext = (dext[:, sl] * sc[:, sl]).astype(CDT)
            dpw_ref[g] += _dot_tn(dc, dy_ext[:tm])
            dd = _dot_nt(dy_ext, pw_ref[g])
            r = dd / jnp.minimum(pos_ext + 1, w).astype(F32)
            span = 1
            while span < w:
                r = r + pltpu.roll(r, ext_rows - span, axis=0)
                span *= 2
            dus.append(r[:tm] - dd[:tm])
        du_ref[...] = jnp.concatenate(dus, axis=1).astype(du_ref.dtype)
        dsc_ref[...] += jnp.concatenate(dscs, axis=1)

    row = pl.BlockSpec((tm, POOL_DIM), lambda i: (i, 0))
    prev = pl.BlockSpec((POOL_WMAX, POOL_DIM), lambda i: (jnp.maximum(i * hb - 1, 0), 0))
    nxt = pl.BlockSpec((POOL_WMAX, POOL_DIM), lambda i: (jnp.minimum((i + 1) * hb, nsteps * hb - 1), 0))
    return pl.pallas_call(
        body, name=name, grid=(nsteps,),
        in_specs=[row, prev, row, nxt, pl.BlockSpec(pool_w.shape, lambda i: (0, 0, 0)),
                  pl.BlockSpec((1, POOL_DIM), lambda i: (0, 0))],
        out_specs=[row, pl.BlockSpec(pool_w.shape, lambda i: (0, 0, 0)), pl.BlockSpec((1, POOL_DIM), lambda i: (0, 0))],
        out_shape=[jax.ShapeDtypeStruct((T, POOL_DIM), CDT), jax.ShapeDtypeStruct(pool_w.shape, F32),
                   jax.ShapeDtypeStruct((1, POOL_DIM), F32)],
        compiler_params=_params("arbitrary"),
    )(zu, zu, dpm, dpm, pool_w, scale.reshape(1, POOL_DIM))


def _rope_tables(T):
    pos = jnp.arange(T, dtype=F32)
    inv_freq = ROPE_THETA ** (-jnp.arange(0, ROT_DIM, 2, dtype=F32) / ROT_DIM)
    ang = pos[:, None] * inv_freq[None, :]
    cos, sin = jnp.cos(ang), jnp.sin(ang)
    rest = HEAD_DIM - ROT_DIM
    cos_h = jnp.concatenate([cos, cos, jnp.ones((T, rest), F32)], axis=1)
    sin_h = jnp.concatenate([-sin, sin, jnp.zeros((T, rest), F32)], axis=1)
    return jnp.tile(cos_h, (1, 2)), jnp.tile(sin_h, (1, 2))


def _lane_masks():
    lane = lax.broadcasted_iota(jnp.int32, (1, LANES), 1)
    in_head = lane % HEAD_DIM
    return lane < HEAD_DIM, in_head < ROT_DIM // 2


def _rope_partner(v, low):
    lane = lax.broadcasted_iota(jnp.int32, (1, LANES), 1)
    swapped = jnp.where(low, pltpu.roll(v, LANES - ROT_DIM // 2, axis=1), pltpu.roll(v, ROT_DIM // 2, axis=1))
    return jnp.where(lane % HEAD_DIM < ROT_DIM, swapped, 0.0)


def _head_mean(v, first):
    lo = jnp.sum(jnp.where(first, v, 0.0), axis=-1, keepdims=True)
    hi = jnp.sum(jnp.where(first, 0.0, v), axis=-1, keepdims=True)
    return jnp.where(first, lo, hi) * (1.0 / HEAD_DIM)


def _qk_bwd(dq, dk_parts, dv_parts, zqk, gqk, cos_t, sin_t, name):
    T = zqk.shape[0]
    tm = _tile(T, 512, BLOCK)
    nsteps = T // tm
    nq = ATTN_DIM // LANES

    def shifted_sum(cur_ref, prev_ref, next_ref, last):
        nxt = jnp.where(last, 0.0, next_ref[...])
        return cur_ref[...] + jnp.concatenate([prev_ref[BLOCK:, :], nxt], axis=0)

    def body(dq_ref, kc_ref, kp_ref, kn_ref, vc_ref, vp_ref, vn_ref, z_ref, g_ref, c_ref, s_ref, dz_ref, dv_ref, dg_ref):
        i = pl.program_id(0)

        @pl.when(i == 0)
        def _():
            dg_ref[...] = jnp.zeros_like(dg_ref)

        last = i == nsteps - 1
        dk = shifted_sum(kc_ref, kp_ref, kn_ref, last)
        dv_ref[...] = shifted_sum(vc_ref, vp_ref, vn_ref, last).astype(dv_ref.dtype)
        first, low = _lane_masks()
        cosv, sinv = c_ref[...], s_ref[...]
        dgs = []
        for c in range(QK_DIM // LANES):
            sl = slice(c * LANES, (c + 1) * LANES)
            dout = dq_ref[:, sl] if c < nq else dk
            dxn = dout * cosv + _rope_partner(dout * sinv, low)
            xv = z_ref[:, sl]
            r = lax.rsqrt(_head_mean(xv * xv, first) + EPS)
            xh = xv * r
            dgs.append(jnp.sum(dxn * xh, axis=0, keepdims=True))
            dxh = dxn * g_ref[:, sl]
            dz_ref[:, sl] = (r * (dxh - xh * _head_mean(dxh * xh, first))).astype(dz_ref.dtype)
        dg_ref[...] += jnp.concatenate(dgs, axis=1)

    row = pl.BlockSpec((tm, QK_DIM), lambda i: (i, 0))
    tab = pl.BlockSpec((tm, LANES), lambda i: (i, 0))
    vec = pl.BlockSpec((1, QK_DIM), lambda i: (0, 0))
    nxt = pl.BlockSpec((BLOCK, LANES), lambda i: (jnp.minimum((i + 1) * (tm // BLOCK), T // BLOCK - 1), 0))
    kv = [tab, tab, nxt]
    return pl.pallas_call(
        body, name=name, grid=(nsteps,),
        in_specs=[pl.BlockSpec((tm, ATTN_DIM), lambda i: (i, 0))] + kv + kv + [row, vec, tab, tab],
        out_specs=[row, tab, vec],
        out_shape=[jax.ShapeDtypeStruct((T, QK_DIM), CDT), jax.ShapeDtypeStruct((T, KV_DIM), CDT),
                   jax.ShapeDtypeStruct((1, QK_DIM), F32)],
        compiler_params=_params("arbitrary"),
    )(dq, dk_parts[0], dk_parts[1], dk_parts[1], dv_parts[0], dv_parts[1], dv_parts[1], zqk, gqk, cos_t, sin_t)


def _dup_half(v, first, kv):
    swapped = pltpu.roll(v, HEAD_DIM, axis=1)
    return jnp.where(first, v, swapped) if kv == 0 else jnp.where(first, swapped, v)


HEADS_PER_KV = 4
HEAD_STACK_FWD = 1
HEAD_STACK_BWD = 2


def _attn_bias(stack):
    qi = lax.broadcasted_iota(jnp.int32, (stack * BLOCK, 2 * BLOCK), 0) % BLOCK
    ki = lax.broadcasted_iota(jnp.int32, (stack * BLOCK, 2 * BLOCK), 1)
    diff = qi + BLOCK - ki
    band = (diff >= 0) & (diff < BLOCK)
    return jnp.stack([jnp.where(band, 0.0, -jnp.inf), jnp.where(band & (ki >= BLOCK), 0.0, -jnp.inf)]).astype(F32)


def _attn_blocks(T):
    return _tile(T // BLOCK, 4, 1)


def _stack_heads(ref, rows, kv, heads, first):
    parts = []
    for h in heads:
        c = 2 * kv + h // 2
        v = ref[rows, c * LANES:(c + 1) * LANES].astype(CDT)
        zero = jnp.zeros_like(v)
        parts.append(jnp.where(first, v, zero) if h % 2 == 0 else jnp.where(first, zero, v))
    return parts[0] if len(parts) == 1 else jnp.concatenate(parts, axis=0)


def _row_blocks(v, n):
    return [v[b * BLOCK:(b + 1) * BLOCK] for b in range(n)]


def _sink_column(sink_ref, kv, heads):
    cols = [jnp.full((BLOCK, 1), sink_ref[HEADS_PER_KV * kv + h], F32) for h in heads]
    return cols[0] if len(cols) == 1 else jnp.concatenate(cols, axis=0)


def _head_groups(stack):
    return [tuple(range(g, g + stack)) for g in range(0, HEADS_PER_KV, stack)]


def _softmax_with_sink(qst, kdup, sinkcol, bias):
    s = _dot_nt(qst, kdup) * ATTN_SCALE + bias
    m = jnp.maximum(jnp.max(s, axis=-1, keepdims=True), sinkcol)
    pu = jnp.exp(s - m)
    denom = jnp.sum(pu, axis=-1, keepdims=True) + jnp.exp(sinkcol - m)
    return pu * (1.0 / denom), m + jnp.log(denom)


def _attn_fwd(qkn, zv, sinks, name, side=None):
    T = qkn.shape[0]
    R = _attn_blocks(T)
    tq = R * BLOCK

    def body(sink_ref, bias_ref, qk_ref, qkp_ref, v_ref, vp_ref, o_ref, lse_ref):
        i = pl.program_id(0)
        first, _ = _lane_masks()
        lane = lax.broadcasted_iota(jnp.int32, (1, LANES), 1)
        kall = jnp.concatenate([qkp_ref[:, ATTN_DIM:], qk_ref[:, ATTN_DIM:]], axis=0)
        vall = jnp.concatenate([vp_ref[...], v_ref[...]], axis=0).astype(CDT)
        for r in range(R):
            bias = bias_ref[jnp.where(i == 0, 1, 0)] if r == 0 else bias_ref[0]
            rows = slice(r * BLOCK, (r + 2) * BLOCK)
            qrows = slice(r * BLOCK, (r + 1) * BLOCK)
            lse_rows = jnp.zeros((BLOCK, LANES), F32)
            for kv in range(2):
                kdup = _dup_half(kall[rows], first, kv)
                vdup = _dup_half(vall[rows], first, kv)
                res = []
                for heads in _head_groups(HEAD_STACK_FWD):
                    p, lse = _softmax_with_sink(_stack_heads(qk_ref, qrows, kv, heads, first), kdup,
                                                _sink_column(sink_ref, kv, heads), bias)
                    res += _row_blocks(_dot(p.astype(CDT), vdup), len(heads))
                    for b, col in enumerate(_row_blocks(lse, len(heads))):
                        lse_rows = jnp.where(lane == HEADS_PER_KV * kv + heads[b], col, lse_rows)
                o_ref[qrows, 2 * kv * LANES:(2 * kv + 1) * LANES] = jnp.where(first, res[0], res[1]).astype(o_ref.dtype)
                o_ref[qrows, (2 * kv + 1) * LANES:(2 * kv + 2) * LANES] = jnp.where(first, res[2], res[3]).astype(o_ref.dtype)
            lse_ref[qrows, :] = lse_rows

    bias = _attn_bias(HEAD_STACK_FWD)
    prev = lambda i: (jnp.maximum(i * R - 1, 0), 0)
    return _pcall(
        body, name, (T // tq,),
        [pl.BlockSpec(memory_space=pltpu.SMEM), pl.BlockSpec(bias.shape, lambda i: (0, 0, 0)),
         pl.BlockSpec((tq, QK_DIM), lambda i: (i, 0)), pl.BlockSpec((BLOCK, QK_DIM), prev),
         pl.BlockSpec((tq, KV_DIM), lambda i: (i, 0)), pl.BlockSpec((BLOCK, KV_DIM), prev)],
        [pl.BlockSpec((tq, ATTN_DIM), lambda i: (i, 0)), pl.BlockSpec((tq, LANES), lambda i: (i, 0))],
        [jax.ShapeDtypeStruct((T, ATTN_DIM), CDT), jax.ShapeDtypeStruct((T, LANES), F32)],
        (sinks, bias, qkn, qkn, zv, zv), ("parallel",), side)


def _attn_bwd(qkn, zv, sinks, do, o, lse, name, side=None):
    T = qkn.shape[0]
    R = _attn_blocks(T)
    tq = R * BLOCK

    def body(sink_ref, bias_ref, qk_ref, qkp_ref, v_ref, vp_ref, do_ref, o_ref, lse_ref,
             dq_ref, dkc_ref, dkp_ref, dvc_ref, dvp_ref, ds_ref):
        i = pl.program_id(0)

        @pl.when(i == 0)
        def _():
            ds_ref[...] = jnp.zeros_like(ds_ref)

        first, _ = _lane_masks()
        lane = lax.broadcasted_iota(jnp.int32, (1, LANES), 1)
        kall = jnp.concatenate([qkp_ref[:, ATTN_DIM:], qk_ref[:, ATTN_DIM:]], axis=0)
        vall = jnp.concatenate([vp_ref[...], v_ref[...]], axis=0).astype(CDT)
        for r in range(R):
            bias = bias_ref[jnp.where(i == 0, 1, 0)] if r == 0 else bias_ref[0]
            rows = slice(r * BLOCK, (r + 2) * BLOCK)
            qrows = slice(r * BLOCK, (r + 1) * BLOCK)
            dk_out, dv_out = [], []
            lse_rows = lse_ref[qrows, :]
            for kv in range(2):
                kdup = _dup_half(kall[rows], first, kv)
                vdup = _dup_half(vall[rows], first, kv)
                dq_h = []
                dk_acc = jnp.zeros((2 * BLOCK, LANES), F32)
                dv_acc = jnp.zeros((2 * BLOCK, LANES), F32)
                for heads in _head_groups(HEAD_STACK_BWD):
                    qst = _stack_heads(qk_ref, qrows, kv, heads, first)
                    dost = _stack_heads(do_ref, qrows, kv, heads, first)
                    lse_cols, delta_cols = [], []
                    for h in heads:
                        cols = slice((2 * kv + h // 2) * LANES, (2 * kv + h // 2 + 1) * LANES)
                        prod = do_ref[qrows, cols].astype(F32) * o_ref[qrows, cols].astype(F32)
                        own = first if h % 2 == 0 else jnp.logical_not(first)
                        delta_cols.append(jnp.sum(jnp.where(own, prod, 0.0), axis=-1, keepdims=True))
                        lse_cols.append(jnp.sum(jnp.where(lane == HEADS_PER_KV * kv + h, lse_rows, 0.0), axis=-1, keepdims=True))
                    lse_col = lse_cols[0] if len(heads) == 1 else jnp.concatenate(lse_cols, axis=0)
                    delta = delta_cols[0] if len(heads) == 1 else jnp.concatenate(delta_cols, axis=0)
                    p = jnp.exp(_dot_nt(qst, kdup) * ATTN_SCALE + bias - lse_col)
                    dsc = (p * (_dot_nt(dost, vdup) - delta)).astype(CDT)
                    psink = jnp.exp(_sink_column(sink_ref, kv, heads) - lse_col)
                    for b, term in enumerate(_row_blocks(psink * delta, len(heads))):
                        row = HEADS_PER_KV * kv + heads[b]
                        ds_ref[row:row + 1, :] += jnp.sum(term, axis=0, keepdims=True)
                    dq_h += _row_blocks(_dot(dsc, kdup) * ATTN_SCALE, len(heads))
                    dk_acc = dk_acc + _dot_tn(dsc, qst) * ATTN_SCALE
                    dv_acc = dv_acc + _dot_tn(p.astype(CDT), dost)
                dq_ref[qrows, 2 * kv * LANES:(2 * kv + 1) * LANES] = jnp.where(first, dq_h[0], dq_h[1])
                dq_ref[qrows, (2 * kv + 1) * LANES:(2 * kv + 2) * LANES] = jnp.where(first, dq_h[2], dq_h[3])
                dk_out.append(dk_acc + pltpu.roll(dk_acc, HEAD_DIM, axis=1))
                dv_out.append(dv_acc + pltpu.roll(dv_acc, HEAD_DIM, axis=1))
            dk = jnp.where(first, dk_out[0], dk_out[1])
            dv = jnp.where(first, dv_out[0], dv_out[1])
            dkp_ref[qrows, :] = dk[:BLOCK]
            dkc_ref[qrows, :] = dk[BLOCK:]
            dvp_ref[qrows, :] = dv[:BLOCK]
            dvc_ref[qrows, :] = dv[BLOCK:]

    bias = _attn_bias(HEAD_STACK_BWD)
    prev = lambda i: (jnp.maximum(i * R - 1, 0), 0)
    kvrow = pl.BlockSpec((tq, KV_DIM), lambda i: (i, 0))
    qrow = pl.BlockSpec((tq, ATTN_DIM), lambda i: (i, 0))
    kv_shape = jax.ShapeDtypeStruct((T, KV_DIM), F32)
    return _pcall(
        body, name, (T // tq,),
        [pl.BlockSpec(memory_space=pltpu.SMEM), pl.BlockSpec(bias.shape, lambda i: (0, 0, 0)),
         pl.BlockSpec((tq, QK_DIM), lambda i: (i, 0)), pl.BlockSpec((BLOCK, QK_DIM), prev),
         kvrow, pl.BlockSpec((BLOCK, KV_DIM), prev), qrow, qrow, kvrow],
        [qrow, kvrow, kvrow, kvrow, kvrow, pl.BlockSpec((N_Q_HEADS, LANES), lambda i: (0, 0))],
        [jax.ShapeDtypeStruct((T, ATTN_DIM), F32), kv_shape, kv_shape, kv_shape, kv_shape,
         jax.ShapeDtypeStruct((N_Q_HEADS, LANES), F32)],
        (sinks, bias, qkn, qkn, zv, zv, do, o, lse), ("arbitrary",), side)


def _merge_fwd(pm, o, w_pb, w_ab, zg, name, side=None):
    T = pm.shape[0]
    tm = _tile(T, 512)

    def body(pm_ref, o_ref, wp_ref, wa_ref, zg_ref, m_ref, gp_ref, ga_ref, fp_ref, fa_ref):
        pmv, ov = pm_ref[...], o_ref[...]
        a = jnp.concatenate([_dot(pmv, wp_ref[j]) for j in range(N_CHIPS)], axis=1)
        b = jnp.concatenate([_dot(ov, wa_ref[j]) for j in range(N_CHIPS)], axis=1)
        gp = _sigmoid(zg_ref[:, :D_MODEL].astype(F32))
        ga = _sigmoid(zg_ref[:, D_MODEL:].astype(F32))
        ap, ba = gp * a, ga * b
        m_ref[...] = (ap + ba).astype(m_ref.dtype)
        gp_ref[...] = gp.astype(gp_ref.dtype)
        ga_ref[...] = ga.astype(ga_ref.dtype)
        fp_ref[...] = (ap * (1.0 - gp)).astype(fp_ref.dtype)
        fa_ref[...] = (ba * (1.0 - ga)).astype(fa_ref.dtype)

    half = pl.BlockSpec((tm, POOL_DIM), lambda i: (i, 0))
    full = pl.BlockSpec((tm, D_MODEL), lambda i: (i, 0))
    wspec = pl.BlockSpec(w_pb.shape, lambda i: (0, 0, 0))
    out = jax.ShapeDtypeStruct((T, D_MODEL), CDT)
    return _pcall(body, name, (T // tm,), [half, half, wspec, wspec, pl.BlockSpec((tm, GATE_DIM), lambda i: (i, 0))],
                  [full] * 5, [out] * 5, (pm, o, w_pb, w_ab, zg), ("parallel",), side)


def _merge_bwd(dxo, w_out, factors, w_pb, w_ab, name):
    T = dxo.shape[0]
    tm = _tile(T, 512)
    kb = w_pb.shape[2]

    def branch_dx(dv, b_ref):
        acc = _dot_nt(dv[:, :kb], b_ref[0])
        for j in range(1, N_CHIPS):
            acc = acc + _dot_nt(dv[:, j * kb:(j + 1) * kb], b_ref[j])
        return acc

    def body(dx_ref, w_ref, gp_ref, ga_ref, fp_ref, fa_ref, wp_ref, wa_ref, da_ref, db_ref, dg_ref, dpm_ref, do_ref):
        dm = _dot_nt(dx_ref[...].astype(CDT), w_ref[...])
        da = (dm * gp_ref[...].astype(F32)).astype(da_ref.dtype)
        db = (dm * ga_ref[...].astype(F32)).astype(db_ref.dtype)
        da_ref[...] = da
        db_ref[...] = db
        dg_ref[:, :D_MODEL] = (dm * fp_ref[...].astype(F32)).astype(dg_ref.dtype)
        dg_ref[:, D_MODEL:] = (dm * fa_ref[...].astype(F32)).astype(dg_ref.dtype)
        dpm_ref[...] = branch_dx(da, wp_ref).astype(dpm_ref.dtype)
        do_ref[...] = branch_dx(db, wa_ref).astype(do_ref.dtype)

    full = pl.BlockSpec((tm, D_MODEL), lambda i: (i, 0))
    half = pl.BlockSpec((tm, POOL_DIM), lambda i: (i, 0))
    gate = pl.BlockSpec((tm, GATE_DIM), lambda i: (i, 0))
    wspec = pl.BlockSpec(w_pb.shape, lambda i: (0, 0, 0))
    out = jax.ShapeDtypeStruct((T, D_MODEL), CDT)
    out_half = jax.ShapeDtypeStruct((T, POOL_DIM), CDT)
    return pl.pallas_call(
        body, name=name, grid=(T // tm,),
        in_specs=[full, pl.BlockSpec((D_MODEL, D_MODEL), lambda i: (0, 0))] + [full] * 4 + [wspec, wspec],
        out_specs=[full, full, gate, half, half],
        out_shape=[out, out, jax.ShapeDtypeStruct((T, GATE_DIM), CDT), out_half, out_half],
        compiler_params=_params("parallel"),
    )(dxo, w_out, *factors, w_pb, w_ab)


def _cast_weights(ws, name, side=None):
    L = ws[0].shape[0]

    def body(*refs):
        for i_ref, o_ref in zip(refs[:len(ws)], refs[len(ws):]):
            o_ref[...] = i_ref[...].astype(o_ref.dtype)

    specs = [pl.BlockSpec((1,) + w.shape[1:], lambda l: (l, 0, 0)) for w in ws]
    return _pcall(body, name, (L,), specs, specs, [jax.ShapeDtypeStruct(w.shape, CDT) for w in ws], tuple(ws),
                  ("parallel",), side)


def _adamw(w, g, m, v, name):
    Rr, C = w.shape
    tr = _tile(Rr, max(8, (1 << 19) // C // 8 * 8))

    def body(w_ref, g_ref, m_ref, v_ref, go_ref, d_ref, nm_ref, nv_ref):
        gv = g_ref[...]
        go_ref[...] = gv
        nm = ADAM_B1 * m_ref[...] + (1.0 - ADAM_B1) * gv
        nv = ADAM_B2 * v_ref[...] + (1.0 - ADAM_B2) * (gv * gv)
        m_hat = nm / (1.0 - ADAM_B1 ** ADAM_STEP)
        v_hat = nv / (1.0 - ADAM_B2 ** ADAM_STEP)
        d_ref[...] = -ADAM_LR * (m_hat / (jnp.sqrt(v_hat) + ADAM_EPS) + ADAM_WD * w_ref[...])
        nm_ref[...] = nm
        nv_ref[...] = nv

    blk = pl.BlockSpec((tr, C), lambda i: (i, 0))
    out = jax.ShapeDtypeStruct((Rr, C), F32)
    return pl.pallas_call(
        body, name=name, grid=(Rr // tr,), in_specs=[blk] * 4, out_specs=[blk] * 4, out_shape=[out] * 4,
        compiler_params=_params("parallel"),
    )(w, g, m, v)


def _place():
    return lax.axis_index("x"), lax.axis_index("y"), lax.axis_index("c")


def _other_chip(x, y, d):
    return (1 - x if d & 2 else x), (1 - y if d & 1 else y)


def _rcopy(src, dst, ssem, rsem, dev):
    return pltpu.make_async_remote_copy(src_ref=src, dst_ref=dst, send_sem=ssem, recv_sem=rsem, device_id=dev,
                                        device_id_type=MESH)


def _row_half(rows, c):
    return pl.ds(c * (rows // 2), rows // 2)


def _is_wide(name):
    return name in WIDE


def _block(ref, wide, j, rows, n):
    if wide:
        return ref.at[rows, pl.ds(pl.multiple_of(j * n, LANES), n)]
    return ref.at[j, rows]


def _gathered_shape(shard, wide):
    _, a, n = shard.shape
    return jax.ShapeDtypeStruct((a, N_CHIPS * n) if wide else (N_CHIPS, a, n), shard.dtype)


def _gather_ici_side(shards, wides, l):
    k_of = lambda w, d: 3 * w + d - 1

    def issue(ins, outs, ssem, rsem):
        x, y, c = _place()
        cps = []
        for w, (shard, wide) in enumerate(zip(shards, wides)):
            _, a, n = shard.shape
            half = _row_half(a, c)
            for d in (1, 2, 3):
                px, py = _other_chip(x, y, d)
                cps.append(_rcopy(ins[w].at[l, half], _block(outs[w], wide, 2 * x + y, half, n),
                                  ssem.at[k_of(w, d)], rsem.at[k_of(w, d)], (px, py, c)))
        return cps

    return _Side(shards, [_gathered_shape(s_, wd) for s_, wd in zip(shards, wides)], 3 * len(shards), issue)


def _gather_d2d_side(shards, wides, gathered, l):
    nw = len(shards)

    def issue(ins, outs, ssem, rsem):
        x, y, c = _place()
        sibling = (x, y, 1 - c)
        cps = []
        for w, (shard, wide) in enumerate(zip(shards, wides)):
            _, a, n = shard.shape
            half = _row_half(a, c)
            for d in (1, 2, 3):
                px, py = _other_chip(x, y, d)
                k = 3 * w + d - 1
                got = _block(outs[w], wide, 2 * px + py, half, n)
                cps.append(_rcopy(got, got, ssem.at[k], rsem.at[k], sibling))
            cps.append(_rcopy(ins[nw + w].at[l], _block(outs[w], wide, 2 * x + y, pl.ds(0, a), n),
                              ssem.at[3 * nw + w], rsem.at[3 * nw + w], sibling))
        return cps

    return _Side(list(gathered) + list(shards), [jax.ShapeDtypeStruct(g.shape, g.dtype) for g in gathered], 4 * nw, issue,
                 aliases={w: w for w in range(nw)})


def _half_shape(g, wide):
    if wide:
        return jax.ShapeDtypeStruct((g.shape[0] // 2, g.shape[1]), g.dtype)
    return jax.ShapeDtypeStruct((N_CHIPS, g.shape[1] // 2, g.shape[2]), g.dtype)


def _reduce_sibling_side(gms, wides):
    def issue(ins, outs, ssem, rsem):
        x, y, c = _place()
        cps = []
        for w, (g, wide) in enumerate(zip(gms, wides)):
            src = ins[w].at[_row_half(g.shape[0], 1 - c)] if wide else ins[w].at[:, _row_half(g.shape[1], 1 - c)]
            cps.append(_rcopy(src, outs[w], ssem.at[w], rsem.at[w], (x, y, 1 - c)))
        return cps

    return _Side(gms, [_half_shape(g, wd) for g, wd in zip(gms, wides)], len(gms), issue)


def _reduce_chip_side(ps, wides):
    def slot_shape(p, wide):
        return jax.ShapeDtypeStruct((N_CHIPS, p.shape[0], p.shape[1] // N_CHIPS) if wide else p.shape, p.dtype)

    def issue(ins, outs, ssem, rsem):
        x, y, c = _place()
        cps = []
        for w, (p, wide) in enumerate(zip(ps, wides)):
            ah, n = (p.shape[0], p.shape[1] // N_CHIPS) if wide else p.shape[1:]
            for d in (1, 2, 3):
                px, py = _other_chip(x, y, d)
                k = 3 * w + d - 1
                cps.append(_rcopy(_block(ins[w], wide, 2 * px + py, pl.ds(0, ah), n), outs[w].at[2 * x + y],
                                  ssem.at[k], rsem.at[k], (px, py, c)))
        return cps

    return _Side(ps, [slot_shape(p, wd) for p, wd in zip(ps, wides)], 3 * len(ps), issue)


def _share_side(accs, items):
    def issue(ins, outs, ssem, rsem):
        x, y, c = _place()
        cps = []
        for k, (w, layer) in enumerate(items):
            mine = outs[w].at[layer, _row_half(accs[w].shape[1], c)]
            cps.append(_rcopy(mine, mine, ssem.at[k], rsem.at[k], (x, y, 1 - c)))
        return cps

    return _Side(accs, [jax.ShapeDtypeStruct(a.shape, a.dtype) for a in accs], len(items), issue,
                 aliases={w: w for w in range(len(accs))})


def _sum_rows(rows, b):
    return _tile(rows, max(16, (1 << 19) // b // 16 * 16), 16)


def _pair_sum(g, recv, wide, place, name):
    ah, b = recv.shape[-2:]
    ta = _sum_rows(ah, b)
    nr = ah // ta

    def body(p_ref, g_ref, r_ref, o_ref):
        o_ref[...] = (g_ref[...].astype(F32) + r_ref[...].astype(F32)).astype(o_ref.dtype)

    if wide:
        grid = (nr,)
        specs = [pl.BlockSpec((ta, b), lambda r, p: (p[0] * nr + r, 0)), pl.BlockSpec((ta, b), lambda r, p: (r, 0))]
        out_spec = pl.BlockSpec((ta, b), lambda r, p: (r, 0))
    else:
        grid = (N_CHIPS, nr)
        specs = [pl.BlockSpec((None, ta, b), lambda j, r, p: (j, p[0] * nr + r, 0)),
                 pl.BlockSpec((None, ta, b), lambda j, r, p: (j, r, 0))]
        out_spec = pl.BlockSpec((None, ta, b), lambda j, r, p: (j, r, 0))
    return pl.pallas_call(
        body, name=name,
        grid_spec=pltpu.PrefetchScalarGridSpec(num_scalar_prefetch=1, grid=grid, in_specs=specs, out_specs=out_spec),
        out_shape=jax.ShapeDtypeStruct(recv.shape, recv.dtype), compiler_params=_params(*["parallel"] * len(grid)),
    )(place, g, recv)


def _chip_sum(slots, part, wide, place, acc, l, name):
    _, ah, b = slots.shape
    ta = _sum_rows(ah, b)
    nr = ah // ta

    def body(p_ref, s_ref, own_ref, acc_ref, o_ref):
        j = p_ref[1]
        own = own_ref[...].astype(F32)
        term = [jnp.where(j == s_, own, s_ref[s_].astype(F32)) for s_ in range(N_CHIPS)]
        o_ref[...] = ((term[0] + term[1]) + term[2]) + term[3]

    own_spec = (pl.BlockSpec((ta, b), lambda r, p: (r, p[1])) if wide else
                pl.BlockSpec((None, ta, b), lambda r, p: (p[1], r, 0)))
    return pl.pallas_call(
        body, name=name,
        grid_spec=pltpu.PrefetchScalarGridSpec(
            num_scalar_prefetch=1, grid=(nr,),
            in_specs=[pl.BlockSpec((N_CHIPS, ta, b), lambda r, p: (0, r, 0)), own_spec, ANY],
            out_specs=pl.BlockSpec((None, ta, b), lambda r, p: (l, p[0] * nr + r, 0))),
        out_shape=jax.ShapeDtypeStruct(acc.shape, F32), input_output_aliases={3: 0},
        compiler_params=_params("parallel"),
    )(place, slots, part, acc)


def _small_side(v):
    def issue(ins, outs, ssem, rsem):
        x, y, c = _place()
        cps = []
        for d in range(1, N_DEV):
            px, py = _other_chip(x, y, d >> 1)
            pc = 1 - c if d & 1 else c
            cps.append(_rcopy(ins[0], outs[0].at[4 * x + 2 * y + c], ssem.at[d - 1], rsem.at[d - 1], (px, py, pc)))
        return cps

    return _Side([v], [jax.ShapeDtypeStruct((N_DEV,) + v.shape, v.dtype)], N_DEV - 1, issue)


def _small_sum(slots, v, place, name):
    def body(p_ref, s_ref, v_ref, o_ref):
        me = 2 * p_ref[1] + p_ref[0]
        acc = jnp.where(me == 0, v_ref[...], s_ref[0])
        for s_ in range(1, N_DEV):
            acc = acc + jnp.where(me == s_, v_ref[...], s_ref[s_])
        o_ref[...] = acc

    return pl.pallas_call(
        body, name=name,
        grid_spec=pltpu.PrefetchScalarGridSpec(
            num_scalar_prefetch=1, grid=(1,),
            in_specs=[pl.BlockSpec(slots.shape, lambda i, p: (0, 0, 0)), pl.BlockSpec(v.shape, lambda i, p: (0, 0))],
            out_specs=pl.BlockSpec(v.shape, lambda i, p: (0, 0))),
        out_shape=jax.ShapeDtypeStruct(v.shape, F32), compiler_params=_params("arbitrary"),
    )(place, slots, v)


def _ffn_forward(x, p, tag, side_of):
    h, gu, act = _ffn_up(x, p[f"ln_{tag}"], p[f"w_{tag}_gu"], f"{tag}_up", side_of(f"{tag}_up"))
    x_out = _mm_nn(act, p[f"w_{tag}_down"], f"{tag}_down", F32, res=x, scale=0.5, side=side_of(f"{tag}_down"))
    return x_out, (x, h, gu, act)


def _row_blocks_of(dw):
    return dw.reshape(N_CHIPS, dw.shape[0] // N_CHIPS, dw.shape[1])


def _ffn_backward(dxo, saved, p, tag, side_of, grad):
    x, h, gu, act = saved
    dgu = _ffn_down_bwd(dxo, p[f"w_{tag}_down"], gu, f"{tag}_down_bwd", side_of(f"{tag}_down_bwd"))
    grad(f"w_{tag}_down", _row_blocks_of(_mm_tn(act, dxo, f"{tag}_dwd", scale=0.5, side=side_of(f"{tag}_dwd"))))
    grad(f"w_{tag}_gu", _mm_tn(h, dgu, f"{tag}_dwgu", tn_target=2816, tm_target=1024, side=side_of(f"{tag}_dwgu")))
    dx, d_ln = _mm_nt_norm_bwd([dgu], p[f"w_{tag}_gu"], x, p[f"ln_{tag}"], dxo, f"{tag}_dh_norm_bwd",
                               side_of(f"{tag}_dh_norm_bwd"))
    grad(f"ln_{tag}", d_ln[0])
    return dx


def _mixer_forward(x, p, tabs, side_of):
    h, zu, zqk, zv, zg, qkn, pm = _mm_in(x, p["ln_mix"], p["w_in"], p["gqk"], *tabs, p["pool_w"], p["pool_scale"],
                                         "mix_in", side_of("mix_in"))
    o, lse = _attn_fwd(qkn, zv, p["sinks"], "attn_fwd", side_of("attn_fwd"))
    m, *factors = _merge_fwd(pm, o, p["w_pool_branch"], p["w_attn_branch"], zg, "merge_fwd", side_of("merge_fwd"))
    x_out = _mm_nn(m, p["w_out"], "mix_out", F32, res=x, scale=1.0)
    return x_out, (x, h, zu, zqk, zv, pm, qkn, o, lse, factors, m)


def _mixer_backward(dxo, saved, p, tabs, side_of, grad):
    x, h, zu, zqk, zv, pm, qkn, o, lse, factors, m = saved
    d_a, d_b, dgl, dpm, do = _merge_bwd(dxo, p["w_out"], factors, p["w_pool_branch"], p["w_attn_branch"], "merge_bwd")
    grad("w_out", _row_blocks_of(_mm_tn(m, dxo, "mix_dwout")))
    grad("w_pool_branch", _mm_tn(pm, d_a, "pool_branch_dw", col_blocks=N_CHIPS))
    grad("w_attn_branch", _mm_tn(o, d_b, "attn_branch_dw", col_blocks=N_CHIPS))
    du, d_pool_w, d_pool_scale = _pool_bwd(zu, dpm, p["pool_w"], p["pool_scale"], "pool_bwd")
    grad("pool_w", d_pool_w)
    grad("pool_scale", d_pool_scale)
    dq, dkc, dkp, dvc, dvp, dsink = _attn_bwd(qkn, zv, p["sinks"], do, o, lse, "attn_bwd", side_of("attn_bwd"))
    dzqk, dv, dgqk = _qk_bwd(dq, (dkc, dkp), (dvc, dvp), zqk, p["gqk"], *tabs, "qk_bwd")
    grad("q_norm", dgqk[0, :ATTN_DIM].reshape(N_Q_HEADS, HEAD_DIM).sum(axis=0))
    grad("k_norm", dgqk[0, ATTN_DIM:].reshape(KV_DIM // HEAD_DIM, HEAD_DIM).sum(axis=0))
    grad("sinks", -dsink[:, 0])
    dz = [du, dzqk, dv, dgl]
    grad("w_in", _blocks_from_full("w_in", _mm_tn_parts(h, dz, "mix_dwin")).astype(WIRE_DT))
    dx, d_ln = _mm_nt_norm_bwd(dz, p["w_in"], x, p["ln_mix"], dxo, "mix_dh_norm_bwd", side_of("mix_dh_norm_bwd"))
    grad("ln_mix", d_ln[0])
    return dx


class _NoComm:
    def __init__(self, layers):
        self.layers, self.grads = layers, [dict() for _ in layers]

    def weight(self, l, name):
        return self.layers[l][name]

    def side(self, phase, l, host):
        return None

    def grad(self, l, name, value):
        self.grads[l][name] = value


class _Layer:
    def __init__(self, hooks, l):
        self.hooks, self.l, self.got = hooks, l, {}

    def __getitem__(self, name):
        if name not in self.got:
            self.got[name] = self.hooks.weight(self.l, name)
        return self.got[name]


def _local_step(x, tgt, n_layers, hooks):
    T = x.shape[0]
    tabs = _rope_tables(T)
    saved, params = [], []
    for l in range(n_layers):
        p = _Layer(hooks, l)
        side_of = functools.partial(hooks.side, "fwd", l)
        x, s1 = _ffn_forward(x, p, "ffn1", side_of)
        x, s2 = _mixer_forward(x, p, tabs, side_of)
        x, s3 = _ffn_forward(x, p, "ffn2", side_of)
        saved.append((s1, s2, s3))
        params.append(p)
    dx, loss = _loss_head(x, tgt, "loss_head")
    for l in reversed(range(n_layers)):
        p = params[l]
        s1, s2, s3 = saved[l]
        side_of = functools.partial(hooks.side, "bwd", l)
        grad = functools.partial(hooks.grad, l)
        dx = _ffn_backward(dx, s3, p, "ffn2", side_of, grad)
        dx = _mixer_backward(dx, s2, p, tabs, side_of, grad)
        dx = _ffn_backward(dx, s1, p, "ffn1", side_of, grad)
    return loss, dx


def _full_from_blocks(name, blocks):
    if name in COL_SHARDED:
        return jnp.transpose(blocks, (1, 0, 2)).reshape(blocks.shape[1], N_CHIPS * blocks.shape[2])
    return blocks.reshape(N_CHIPS * blocks.shape[1], blocks.shape[2])


def _blocks_from_full(name, full):
    K, N = full.shape
    if name in COL_SHARDED:
        return jnp.transpose(full.reshape(K, N_CHIPS, N // N_CHIPS), (1, 0, 2))
    return full.reshape(N_CHIPS, K // N_CHIPS, N)


JOBS = {"a": ("w_ffn1_gu", "w_ffn1_down"), "b": ("w_in", "w_pool_branch", "w_attn_branch", "w_out"),
        "c": ("w_ffn2_gu", "w_ffn2_down")}
GATHER_PLAN = {"ffn1_up": ("ici", "b", JOBS["b"], 0), "ffn1_down": ("d2d", "b", JOBS["b"], 0),
               "mix_in": ("ici", "c", JOBS["c"][:1], 0), "attn_fwd": ("ici", "c", JOBS["c"][1:], 0),
               "merge_fwd": ("d2d", "c", JOBS["c"], 0),
               "ffn2_up": ("ici", "a", JOBS["a"], 1), "ffn2_down": ("d2d", "a", JOBS["a"], 1)}
REDUCE_PLAN = {"ffn2_down_bwd": ("sibling", "a", 1), "ffn2_dwgu": ("chip", "a", 1),
               "ffn2_dh_norm_bwd": ("sibling", "c", 0), "attn_bwd": ("chip", "c", 0),
               "mix_dh_norm_bwd": ("sibling", "b", 0), "ffn1_down_bwd": ("chip", "b", 0)}
SHARE_HOST = "ffn1_dwgu"
SMALL_HOST = "ffn2_dwd"
LAST_GRAD = "ln_ffn1"


class _Exchange:
    def __init__(self, weights, small, place, n_layers):
        self.small, self.place, self.n_layers = small, place, n_layers
        shards = {n: weights[n].astype(CDT) for n in JOBS["a"]}
        first, wides = [shards[n] for n in JOBS["a"]], [_is_wide(n) for n in JOBS["a"]]
        rest = [n for n in BIG if n not in JOBS["a"]]
        ici = _gather_ici_side(first, wides, 0)
        shards.update(zip(rest, _cast_weights([weights[n] for n in rest], "cast_weights", ici)))
        self.shards = shards
        got = _run_side(_gather_d2d_side(first, wides, ici.outs, 0), "gather_d2d")
        self.blocks = {(n, 0): g for n, g in zip(JOBS["a"], got)}
        self.landed = {}
        self.handed = []
        self.acc = {n: lax.empty(shards[n].shape, F32) for n in BIG}
        self.grads = [dict() for _ in range(n_layers)]
        self.reduce = {}
        self.summed = set()
        self.unshared, self.sharing = [], None
        self.small_sides, self.small_waiting = {}, None

    def weight(self, l, name):
        if name not in BIG:
            return self.small(l)[name]
        for names, layer, done in self.handed:
            self.blocks.update({(n, layer): g for n, g in zip(names, done.outs)})
        self.handed.clear()
        blocks = self.blocks.pop((name, l))
        return blocks if name in USED_AS_BLOCKS + WIDE else _full_from_blocks(name, blocks)

    def _gather_side(self, l, host):
        step, job, names, ahead = GATHER_PLAN[host]
        layer = l + ahead
        if layer >= self.n_layers:
            return None
        if step == "ici":
            side = _gather_ici_side([self.shards[n] for n in names], [_is_wide(n) for n in names], layer)
            self.landed.setdefault((job, layer), []).append((names, side))
            return side
        names, gathered = JOBS[job], {}
        for part_names, side in self.landed.pop((job, layer)):
            gathered.update(zip(part_names, side.outs))
        done = _gather_d2d_side([self.shards[n] for n in names], [_is_wide(n) for n in names],
                                [gathered[n] for n in names], layer)
        self.handed.append((names, layer, done))
        return done

    def grad(self, l, name, value):
        self.grads[l][name] = value
        if name == LAST_GRAD and l > 0:
            packed, self.small_spans = _pack_small([self.grads[l][n] for n in SMALL])
            self.small_sides[l] = _small_side(packed)
            self.small_waiting = l

    def reduced_small(self, loss_part):
        packed, spans = _pack_small([self.grads[0][n] for n in SMALL] + [loss_part])
        self.small_sides[0] = _small_side(packed)
        _run_side(self.small_sides[0], "all_reduce_small")
        shapes = [self.grads[0][n].shape for n in SMALL]
        per_layer = []
        for l in range(self.n_layers):
            side = self.small_sides[l]
            summed = _small_sum(side.outs[0], side.ins[0], self.place, "small_sum")
            per_layer.append(_unpack_small(summed, spans, shapes + [(1, 1)] * (l == 0)))
        loss = per_layer[0][-1][0, 0]
        return {n: jnp.stack([vals[k] for vals in per_layer]) for k, n in enumerate(SMALL)}, loss

    def _reduce_side(self, l, host):
        step, job, ahead = REDUCE_PLAN[host]
        layer = l + ahead
        if layer >= self.n_layers:
            return None
        return self._reduce_step(step, job, layer)

    def _reduce_step(self, step, job, layer):
        if step == "sibling":
            st = self.reduce[(job, layer)] = dict(gm=[self.grads[layer][n] for n in JOBS[job]],
                                                  wide=[_is_wide(n) for n in JOBS[job]])
            st["sibling"] = _reduce_sibling_side(st["gm"], st["wide"])
            return st["sibling"]
        st = self.reduce[(job, layer)]
        st["part"] = [_pair_sum(g, r, wd, self.place, "grad_pair_sum")
                      for g, r, wd in zip(st["gm"], st["sibling"].outs, st["wide"])]
        st["chip"] = _reduce_chip_side(st["part"], st["wide"])
        return st["chip"]

    def _chip_sums(self):
        if self.sharing is not None:
            self.acc.update(zip(BIG, self.sharing.outs))
            self.sharing = None
        for (job, layer), st in self.reduce.items():
            if (job, layer) not in self.summed and "chip" in st and st["chip"].outs is not None:
                self.summed.add((job, layer))
                for n, slots, part, wd in zip(JOBS[job], st["chip"].outs, st["part"], st["wide"]):
                    self.acc[n] = _chip_sum(slots, part, wd, self.place, self.acc[n], layer, "grad_chip_sum")
                    self.unshared.append((BIG.index(n), layer))

    def _share(self):
        side = _share_side([self.acc[n] for n in BIG], self.unshared)
        self.unshared = []
        return side

    def side(self, phase, l, host):
        if phase == "fwd":
            return self._gather_side(l, host) if host in GATHER_PLAN else None
        self._chip_sums()
        if host == SMALL_HOST and self.small_waiting is not None:
            side, self.small_waiting = self.small_sides[self.small_waiting], None
            return side
        if host == SHARE_HOST and self.unshared:
            self.sharing = self._share()
            return self.sharing
        return self._reduce_side(l, host) if host in REDUCE_PLAN else None

    def reduced(self):
        _run_side(self._reduce_step("sibling", "a", 0), "grad_sibling_exchange")
        _run_side(self._reduce_step("chip", "a", 0), "grad_chip_exchange")
        self._chip_sums()
        return dict(zip(BIG, _run_side(self._share(), "grad_sibling_share")))


def _pack_small(parts):
    rows, spans, lo = [], [], 0
    for v in parts:
        flat = v.reshape(-1)
        nrow = -(-flat.shape[0] // LANES)
        flat = jnp.pad(flat, (0, nrow * LANES - flat.shape[0]))
        rows.append(flat.reshape(nrow, LANES))
        spans.append((lo, nrow))
        lo += nrow
    pad = -lo % 8
    if pad:
        rows.append(jnp.zeros((pad, LANES), F32))
    return jnp.concatenate(rows, axis=0), spans


def _unpack_small(packed, spans, shapes):
    out = []
    for (lo, nrow), shape in zip(spans, shapes):
        size = 1
        for s in shape:
            size *= s
        out.append(packed[lo:lo + nrow].reshape(-1)[:size].reshape(shape))
    return out


def kernel(x, ln_ffn1, w_ffn1_gu, w_ffn1_down, ln_mix, w_in, pool_w, pool_scale, w_pool_branch, q_norm, k_norm, sinks, w_attn_branch, w_out, ln_ffn2, w_ffn2_gu, w_ffn2_down, loss_target, m_ln_ffn1, m_w_ffn1_gu, m_w_ffn1_down, m_ln_mix, m_w_in, m_pool_w, m_pool_scale, m_w_pool_branch, m_q_norm, m_k_norm, m_sinks, m_w_attn_branch, m_w_out, m_ln_ffn2, m_w_ffn2_gu, m_w_ffn2_down, v_ln_ffn1, v_w_ffn1_gu, v_w_ffn1_down, v_ln_mix, v_w_in, v_pool_w, v_pool_scale, v_w_pool_branch, v_q_norm, v_k_norm, v_sinks, v_w_attn_branch, v_w_out, v_ln_ffn2, v_w_ffn2_gu, v_w_ffn2_down):
    w = dict(ln_ffn1=ln_ffn1, w_ffn1_gu=w_ffn1_gu, w_ffn1_down=w_ffn1_down, ln_mix=ln_mix, w_in=w_in, pool_w=pool_w,
             pool_scale=pool_scale, w_pool_branch=w_pool_branch, q_norm=q_norm, k_norm=k_norm, sinks=sinks,
             w_attn_branch=w_attn_branch, w_out=w_out, ln_ffn2=ln_ffn2, w_ffn2_gu=w_ffn2_gu, w_ffn2_down=w_ffn2_down)
    mom = dict(ln_ffn1=m_ln_ffn1, w_ffn1_gu=m_w_ffn1_gu, w_ffn1_down=m_w_ffn1_down, ln_mix=m_ln_mix, w_in=m_w_in,
               pool_w=m_pool_w, pool_scale=m_pool_scale, w_pool_branch=m_w_pool_branch, q_norm=m_q_norm, k_norm=m_k_norm,
               sinks=m_sinks, w_attn_branch=m_w_attn_branch, w_out=m_w_out, ln_ffn2=m_ln_ffn2, w_ffn2_gu=m_w_ffn2_gu,
               w_ffn2_down=m_w_ffn2_down)
    var = dict(ln_ffn1=v_ln_ffn1, w_ffn1_gu=v_w_ffn1_gu, w_ffn1_down=v_w_ffn1_down, ln_mix=v_ln_mix, w_in=v_w_in,
               pool_w=v_pool_w, pool_scale=v_pool_scale, w_pool_branch=v_w_pool_branch, q_norm=v_q_norm, k_norm=v_k_norm,
               sinks=v_sinks, w_attn_branch=v_w_attn_branch, w_out=v_w_out, ln_ffn2=v_ln_ffn2, w_ffn2_gu=v_w_ffn2_gu,
               w_ffn2_down=v_w_ffn2_down)
    L = ln_ffn1.shape[0]

    def small(l):
        return dict(ln_ffn1=ln_ffn1[l], ln_mix=ln_mix[l], ln_ffn2=ln_ffn2[l], pool_w=pool_w[l].astype(CDT),
                    pool_scale=pool_scale[l], sinks=sinks[l],
                    gqk=jnp.concatenate([jnp.tile(q_norm[l], N_Q_HEADS), jnp.tile(k_norm[l], KV_DIM // HEAD_DIM)]).reshape(1, QK_DIM))

    place = jnp.stack([lax.axis_index("c"), 2 * lax.axis_index("x") + lax.axis_index("y")]).astype(jnp.int32)
    hooks = _Exchange({n: w[n] for n in BIG}, small, place, L)
    loss_part, grad_x = _local_step(x[0], loss_target[0], L, hooks)
    g_big = hooks.reduced()
    g_small, loss = hooks.reduced_small(loss_part)
    g_small = {n: v.reshape(w[n].shape) for n, v in g_small.items()}

    grad_out, delta, new_m, new_v = {}, {}, {}, {}
    for n in BIG:
        shape = w[n].shape
        flat = (shape[0] * shape[1], shape[2])
        go, d, nm, nv = _adamw(w[n].reshape(flat), g_big[n].reshape(flat), mom[n].reshape(flat), var[n].reshape(flat), "adamw")
        grad_out[n], delta[n], new_m[n], new_v[n] = go.reshape(shape), d.reshape(shape), nm.reshape(shape), nv.reshape(shape)
    pw, _ = _pack_small([w[n] for n in SMALL])
    pg, sp = _pack_small([g_small[n] for n in SMALL])
    pm_, _ = _pack_small([mom[n] for n in SMALL])
    pv, _ = _pack_small([var[n] for n in SMALL])
    _, d, nm, nv = _adamw(pw, pg, pm_, pv, "adamw_small")
    shapes = [w[n].shape for n in SMALL]
    for n, dv, mv, vv in zip(SMALL, _unpack_small(d, sp, shapes), _unpack_small(nm, sp, shapes), _unpack_small(nv, sp, shapes)):
        grad_out[n], delta[n], new_m[n], new_v[n] = g_small[n], dv, mv, vv

    return (loss, grad_x[None], *[grad_out[n] for n in WEIGHTS], *[delta[n] for n in WEIGHTS],
            *[new_m[n] for n in WEIGHTS], *[new_v[n] for n in WEIGHTS])
```
